```python
import math
import jax, jax.numpy as jnp
from jax import lax
import numpy as np

D_MODEL = 1024
BATCH = 8
SEQ = 4096
DEPTH = 2

PL_DIM = 256
N_EVEN = (DEPTH + 1) // 2
N_ODD = DEPTH // 2
CONV_A_WIDTH = D_MODEL
CONV_A_KERNEL = 3
GDN_HEADS = 8
GDN_HEAD_DIM = 128
GDN_WIDTH = GDN_HEADS * GDN_HEAD_DIM
GDN_CONV_KERNEL = 4
GDN_CHUNK = 64
HGRN_HEAD_DIM = 128
HGRN_WIDTH = 2 * D_MODEL
HGRN_HEADS = HGRN_WIDTH // HGRN_HEAD_DIM
HGRN_CHUNK = 32
EVEN_IN = 4 * CONV_A_WIDTH + 4 * GDN_WIDTH + 2 * GDN_HEADS
EVEN_MIX = CONV_A_WIDTH + GDN_WIDTH
ODD_IN = 4 * HGRN_WIDTH
ODD_MIX = HGRN_WIDTH
DEEPNORM_ALPHA = (2.0 * DEPTH) ** 0.25
DEEPNORM_BETA = (8.0 * DEPTH) ** -0.25
NORM_EPS = 1e-5

kernel_name = "hybrid_shortconv_gdn_hgrn2_deepnorm"


def layer_norm(x, g, b):
    xf = x.astype(jnp.float32)
    mu = jnp.mean(xf, axis=-1, keepdims=True)
    var = jnp.mean(jnp.square(xf - mu), axis=-1, keepdims=True)
    return ((xf - mu) * lax.rsqrt(var + NORM_EPS) * g.astype(jnp.float32) + b.astype(jnp.float32)).astype(x.dtype)


def rms_norm(x, g):
    xf = x.astype(jnp.float32)
    y = xf * lax.rsqrt(jnp.mean(jnp.square(xf), axis=-1, keepdims=True) + NORM_EPS)
    return (y * g.astype(jnp.float32)).astype(x.dtype)


def l2_normalize(x):
    xf = x.astype(jnp.float32)
    return (xf * lax.rsqrt(jnp.sum(jnp.square(xf), axis=-1, keepdims=True) + 1e-6)).astype(x.dtype)


def causal_depthwise_conv(x, w):
    k, c = w.shape
    return lax.conv_general_dilated(
        x, w[:, None, :].astype(x.dtype), window_strides=(1,), padding=[(k - 1, 0)],
        dimension_numbers=("NWC", "WIO", "NWC"), feature_group_count=c)


def _to_chunks(t, c):
    b, s, h, d = t.shape
    return t.reshape(b, s // c, c, h, d).transpose(0, 3, 1, 2, 4)


def _from_chunks(t):
    b, h, n, c, d = t.shape
    return t.transpose(0, 2, 3, 1, 4).reshape(b, n * c, h, d)


def gated_delta_rule(q, k, v, g, beta):
    dtype = v.dtype
    c = GDN_CHUNK
    dk, dv = q.shape[-1], v.shape[-1]
    q = _to_chunks(q.astype(jnp.float32) * (dk ** -0.5), c)
    k = _to_chunks(k.astype(jnp.float32), c)
    v = _to_chunks(v.astype(jnp.float32), c)
    g = _to_chunks(g.astype(jnp.float32)[..., None], c)[..., 0]
    beta = _to_chunks(beta.astype(jnp.float32)[..., None], c)
    gc = jnp.cumsum(g, axis=-1)
    incl = jnp.tril(jnp.ones((c, c), bool))
    strict = jnp.tril(jnp.ones((c, c), bool), -1)
    decay = jnp.exp(jnp.where(incl, gc[..., :, None] - gc[..., None, :], -jnp.inf))
    kb = k * beta
    low = jnp.where(strict, jnp.einsum("bhnid,bhnjd->bhnij", kb, k) * decay, 0.0)
    rhs = jnp.concatenate([v * beta, kb * jnp.exp(gc)[..., None]], axis=-1)
    sol = lax.linalg.triangular_solve(low + jnp.eye(c, dtype=jnp.float32), rhs,
                                      left_side=True, lower=True, unit_diagonal=True)
    u, w = sol[..., :dv], sol[..., dv:]
    attn = jnp.einsum("bhnid,bhnjd->bhnij", q, k) * decay
    q_dec = q * jnp.exp(gc)[..., None]
    g_last = gc[..., -1]
    k_dec = k * jnp.exp(g_last[..., None] - gc)[..., None]

    def step(state, xs):
        q_i, k_i, u_i, w_i, a_i, gl_i = xs
        v_new = u_i - jnp.einsum("bhcd,bhde->bhce", w_i, state)
        o_i = jnp.einsum("bhcd,bhde->bhce", q_i, state) + jnp.einsum("bhij,bhje->bhie", a_i, v_new)
        state = state * jnp.exp(gl_i)[..., None, None] + jnp.einsum("bhcd,bhce->bhde", k_i, v_new)
        return state, o_i

    mv = lambda t: jnp.moveaxis(t, 2, 0)
    s0 = jnp.zeros(q.shape[:2] + (dk, dv), jnp.float32)
    _, o = lax.scan(step, s0, (mv(q_dec), mv(k_dec), mv(u), mv(w), mv(attn), mv(g_last)))
    return _from_chunks(jnp.moveaxis(o, 0, 2)).astype(dtype)


def hgrn2_recurrence(q, k, v, logf):
    dtype = v.dtype
    c = HGRN_CHUNK
    q, k, v, logf = (jnp.moveaxis(_to_chunks(t.astype(jnp.float32), c), 2, 0) for t in (q, k, v, logf))
    b = jnp.cumsum(logf, axis=-2)
    incl = jnp.tril(jnp.ones((c, c), bool))[:, :, None]

    def step(state, xs):
        q_i, k_i, v_i, b_i = xs
        b_last = b_i[..., -1:, :]
        decay = jnp.exp(jnp.where(incl, b_i[..., :, None, :] - b_i[..., None, :, :], -jnp.inf))
        attn = jnp.einsum("bhtd,bhsd,bhtsd->bhts", q_i, k_i, decay)
        o_i = (jnp.einsum("bhtd,bhde->bhte", q_i * jnp.exp(b_i), state)
               + jnp.einsum("bhts,bhse->bhte", attn, v_i))
        state = (state * jnp.exp(b_last)[..., 0, :, None]
                 + jnp.einsum("bhsd,bhse->bhde", k_i * jnp.exp(b_last - b_i), v_i))
        return state, o_i

    s0 = jnp.zeros(q.shape[1:3] + (q.shape[-1], v.shape[-1]), jnp.float32)
    _, o = lax.scan(step, s0, (q, k, v, b))
    return _from_chunks(jnp.moveaxis(o, 0, 2)).astype(dtype)


def conv_gdn_mixer(x, w_in, conv_a_w, conv_b_w, a_log, dt_bias, gdn_norm_g, w_out):
    bsz, s, _ = x.shape
    wa, wg, h = CONV_A_WIDTH, GDN_WIDTH, GDN_HEADS
    proj = x @ w_in
    cuts = [wa, 2 * wa, 3 * wa, 4 * wa, 4 * wa + 3 * wg, 4 * wa + 4 * wg, 4 * wa + 4 * wg + h]
    h_a, c_a, b_a, z_a, qkv, z_b, beta_raw, a_raw = jnp.split(proj, cuts, axis=-1)
    y_a = b_a * causal_depthwise_conv(c_a * h_a, conv_a_w) * jax.nn.silu(z_a)
    qkv = jax.nn.silu(causal_depthwise_conv(qkv, conv_b_w))
    q, k, v = (t.reshape(bsz, s, h, GDN_HEAD_DIM) for t in jnp.split(qkv, 3, axis=-1))
    q, k = l2_normalize(q), l2_normalize(k)
    beta = jax.nn.sigmoid(beta_raw.astype(jnp.float32))
    g = -jnp.exp(a_log.astype(jnp.float32)) * jax.nn.softplus(a_raw.astype(jnp.float32) + dt_bias.astype(jnp.float32))
    o = gated_delta_rule(q, k, v, g, beta)
    o = rms_norm(o, gdn_norm_g) * jax.nn.silu(z_b.reshape(bsz, s, h, GDN_HEAD_DIM))
    y = jnp.concatenate([y_a, o.reshape(bsz, s, wg)], axis=-1)
    return y @ w_out


def hgrn2_mixer(x, w_in, lower_bound, hgrn_norm_g, w_out):
    bsz, s, _ = x.shape
    q_raw, f_raw, i_in, z = jnp.split(x @ w_in, 4, axis=-1)
    f = lower_bound + (1.0 - lower_bound) * jax.nn.sigmoid(f_raw.astype(jnp.float32))
    q = jax.nn.silu(q_raw)
    k = (1.0 - f).astype(x.dtype)
    logf = jnp.log(f)
    heads = lambda t: t.reshape(bsz, s, HGRN_HEADS, HGRN_HEAD_DIM)
    o = hgrn2_recurrence(heads(q), heads(k), heads(i_in), heads(logf))
    o = rms_norm(o, hgrn_norm_g) * jax.nn.silu(heads(z))
    return o.reshape(bsz, s, HGRN_WIDTH) @ w_out


def _fwd_setup_inputs(seed: int = 0) -> dict:
    key = jax.random.key(seed)
    ks = jax.random.split(key, 20)
    nrm = lambda k, shape, scale: jax.random.normal(k, shape, jnp.float32) * scale
    dt = jnp.exp(jax.random.uniform(ks[6], (N_EVEN, GDN_HEADS), jnp.float32)
                 * (math.log(0.1) - math.log(0.001)) + math.log(0.001))
    return {
        "x": nrm(ks[0], (BATCH, SEQ, D_MODEL), 1.0),
        "p": nrm(ks[1], (DEPTH, BATCH, SEQ, PL_DIM), 1.0),
        "w_in_even": nrm(ks[2], (N_EVEN, D_MODEL, EVEN_IN), D_MODEL ** -0.5),
        "conv_a_w": nrm(ks[3], (N_EVEN, CONV_A_KERNEL, CONV_A_WIDTH), CONV_A_KERNEL ** -0.5),
        "conv_b_w": nrm(ks[4], (N_EVEN, GDN_CONV_KERNEL, 3 * GDN_WIDTH), GDN_CONV_KERNEL ** -0.5),
        "a_log": jnp.log(jax.random.uniform(ks[5], (N_EVEN, GDN_HEADS), jnp.float32, 1.0, 16.0)),
        "dt_bias": dt + jnp.log(-jnp.expm1(-dt)),
        "gdn_norm_g": 1.0 + nrm(ks[7], (N_EVEN, GDN_HEAD_DIM), 0.02),
        "w_out_even": nrm(ks[8], (N_EVEN, EVEN_MIX, D_MODEL), EVEN_MIX ** -0.5 * DEEPNORM_BETA),
        "w_in_odd": nrm(ks[9], (N_ODD, D_MODEL, ODD_IN), D_MODEL ** -0.5),
        "lower_bounds": nrm(ks[10], (DEPTH, HGRN_WIDTH), 0.1),
        "hgrn_norm_g": 1.0 + nrm(ks[11], (N_ODD, HGRN_HEAD_DIM), 0.02),
        "w_out_odd": nrm(ks[12], (N_ODD, ODD_MIX, D_MODEL), ODD_MIX ** -0.5 * DEEPNORM_BETA),
        "ln_g": 1.0 + nrm(ks[13], (DEPTH, D_MODEL), 0.02),
        "ln_b": nrm(ks[14], (DEPTH, D_MODEL), 0.02),
        "w_pl": nrm(ks[15], (DEPTH, PL_DIM, D_MODEL), PL_DIM ** -0.5),
        "w_pl_gate": nrm(ks[16], (DEPTH, D_MODEL, D_MODEL), D_MODEL ** -0.5),
    }


def _fwd_reference(x, p, w_in_even, conv_a_w, conv_b_w, a_log, dt_bias, gdn_norm_g, w_out_even,
              w_in_odd, lower_bounds, hgrn_norm_g, w_out_odd, ln_g, ln_b, w_pl, w_pl_gate):
    lbs = jnp.cumsum(jax.nn.softmax(lower_bounds.astype(jnp.float32), axis=0), axis=0)
    lbs = lbs - lbs[0]
    for i in range(DEPTH):
        j = i // 2
        if i % 2 == 0:
            s = conv_gdn_mixer(x, w_in_even[j], conv_a_w[j], conv_b_w[j], a_log[j], dt_bias[j],
                               gdn_norm_g[j], w_out_even[j])
        else:
            s = hgrn2_mixer(x, w_in_odd[j], lbs[i], hgrn_norm_g[j], w_out_odd[j])
        x = layer_norm(DEEPNORM_ALPHA * x + s, ln_g[i], ln_b[i])
        gate = jax.nn.sigmoid((x @ w_pl_gate[i]).astype(jnp.float32))
        x = x + ((p[i] @ w_pl[i]).astype(jnp.float32) * gate).astype(x.dtype)
    return x


import jax as _jax
import jax.numpy as _jnp

TWIN_FORMAT = 'train_step'
FWD_PARAMS = ['x', 'p', 'w_in_even', 'conv_a_w', 'conv_b_w', 'a_log', 'dt_bias', 'gdn_norm_g', 'w_out_even', 'w_in_odd', 'lower_bounds', 'hgrn_norm_g', 'w_out_odd', 'ln_g', 'ln_b', 'w_pl', 'w_pl_gate']
TWIN_WEIGHTS = ['w_in_even', 'conv_a_w', 'conv_b_w', 'a_log', 'dt_bias', 'gdn_norm_g', 'w_out_even', 'w_in_odd', 'lower_bounds', 'hgrn_norm_g', 'w_out_odd', 'ln_g', 'ln_b', 'w_pl', 'w_pl_gate']
TWIN_DIFF_INPUT = 'x'
TWIN_INPUTS = ['x', 'p', 'w_in_even', 'conv_a_w', 'conv_b_w', 'a_log', 'dt_bias', 'gdn_norm_g', 'w_out_even', 'w_in_odd', 'lower_bounds', 'hgrn_norm_g', 'w_out_odd', 'ln_g', 'ln_b', 'w_pl', 'w_pl_gate', 'loss_target', 'm_w_in_even', 'm_conv_a_w', 'm_conv_b_w', 'm_a_log', 'm_dt_bias', 'm_gdn_norm_g', 'm_w_out_even', 'm_w_in_odd', 'm_lower_bounds', 'm_hgrn_norm_g', 'm_w_out_odd', 'm_ln_g', 'm_ln_b', 'm_w_pl', 'm_w_pl_gate', 'v_w_in_even', 'v_conv_a_w', 'v_conv_b_w', 'v_a_log', 'v_dt_bias', 'v_gdn_norm_g', 'v_w_out_even', 'v_w_in_odd', 'v_lower_bounds', 'v_hgrn_norm_g', 'v_w_out_odd', 'v_ln_g', 'v_ln_b', 'v_w_pl', 'v_w_pl_gate']
TWIN_OUTPUTS = ['loss', 'grad_x', 'grad_w_in_even', 'grad_conv_a_w', 'grad_conv_b_w', 'grad_a_log', 'grad_dt_bias', 'grad_gdn_norm_g', 'grad_w_out_even', 'grad_w_in_odd', 'grad_lower_bounds', 'grad_hgrn_norm_g', 'grad_w_out_odd', 'grad_ln_g', 'grad_ln_b', 'grad_w_pl', 'grad_w_pl_gate', 'delta_w_in_even', 'delta_conv_a_w', 'delta_conv_b_w', 'delta_a_log', 'delta_dt_bias', 'delta_gdn_norm_g', 'delta_w_out_even', 'delta_w_in_odd', 'delta_lower_bounds', 'delta_hgrn_norm_g', 'delta_w_out_odd', 'delta_ln_g', 'delta_ln_b', 'delta_w_pl', 'delta_w_pl_gate', 'new_m_w_in_even', 'new_m_conv_a_w', 'new_m_conv_b_w', 'new_m_a_log', 'new_m_dt_bias', 'new_m_gdn_norm_g', 'new_m_w_out_even', 'new_m_w_in_odd', 'new_m_lower_bounds', 'new_m_hgrn_norm_g', 'new_m_w_out_odd', 'new_m_ln_g', 'new_m_ln_b', 'new_m_w_pl', 'new_m_w_pl_gate', 'new_v_w_in_even', 'new_v_conv_a_w', 'new_v_conv_b_w', 'new_v_a_log', 'new_v_dt_bias', 'new_v_gdn_norm_g', 'new_v_w_out_even', 'new_v_w_in_odd', 'new_v_lower_bounds', 'new_v_hgrn_norm_g', 'new_v_w_out_odd', 'new_v_ln_g', 'new_v_ln_b', 'new_v_w_pl', 'new_v_w_pl_gate']
TWIN_LEAF_KINDS = {'loss': 'loss', 'grad_x': 'grad_x', 'grad_w_in_even': 'grad_w', 'grad_conv_a_w': 'grad_w', 'grad_conv_b_w': 'grad_w', 'grad_a_log': 'grad_w', 'grad_dt_bias': 'grad_w', 'grad_gdn_norm_g': 'grad_w', 'grad_w_out_even': 'grad_w', 'grad_w_in_odd': 'grad_w', 'grad_lower_bounds': 'grad_w', 'grad_hgrn_norm_g': 'grad_w', 'grad_w_out_odd': 'grad_w', 'grad_ln_g': 'grad_w', 'grad_ln_b': 'grad_w', 'grad_w_pl': 'grad_w', 'grad_w_pl_gate': 'grad_w', 'delta_w_in_even': 'delta_w', 'delta_conv_a_w': 'delta_w', 'delta_conv_b_w': 'delta_w', 'delta_a_log': 'delta_w', 'delta_dt_bias': 'delta_w', 'delta_gdn_norm_g': 'delta_w', 'delta_w_out_even': 'delta_w', 'delta_w_in_odd': 'delta_w', 'delta_lower_bounds': 'delta_w', 'delta_hgrn_norm_g': 'delta_w', 'delta_w_out_odd': 'delta_w', 'delta_ln_g': 'delta_w', 'delta_ln_b': 'delta_w', 'delta_w_pl': 'delta_w', 'delta_w_pl_gate': 'delta_w', 'new_m_w_in_even': 'new_m', 'new_m_conv_a_w': 'new_m', 'new_m_conv_b_w': 'new_m', 'new_m_a_log': 'new_m', 'new_m_dt_bias': 'new_m', 'new_m_gdn_norm_g': 'new_m', 'new_m_w_out_even': 'new_m', 'new_m_w_in_odd': 'new_m', 'new_m_lower_bounds': 'new_m', 'new_m_hgrn_norm_g': 'new_m', 'new_m_w_out_odd': 'new_m', 'new_m_ln_g': 'new_m', 'new_m_ln_b': 'new_m', 'new_m_w_pl': 'new_m', 'new_m_w_pl_gate': 'new_m', 'new_v_w_in_even': 'new_v', 'new_v_conv_a_w': 'new_v', 'new_v_conv_b_w': 'new_v', 'new_v_a_log': 'new_v', 'new_v_dt_bias': 'new_v', 'new_v_gdn_norm_g': 'new_v', 'new_v_w_out_even': 'new_v', 'new_v_w_in_odd': 'new_v', 'new_v_lower_bounds': 'new_v', 'new_v_hgrn_norm_g': 'new_v', 'new_v_w_out_odd': 'new_v', 'new_v_ln_g': 'new_v', 'new_v_ln_b': 'new_v', 'new_v_w_pl': 'new_v', 'new_v_w_pl_gate': 'new_v'}


def _forward(args):
    return _fwd_reference(*[args[k] for k in FWD_PARAMS])


def _output_shape():
    out = _jax.eval_shape(lambda: _forward(_fwd_setup_inputs(0)))
    return out.shape, out.dtype

N_MICROBATCH = 1
ADAM_LR = 0.001
ADAM_B1 = 0.9
ADAM_B2 = 0.999
ADAM_EPS = 1e-08
ADAM_WD = 0.01
ADAM_STEP = 10
PER_EXAMPLE_BATCH_AXIS = {'x': 0, 'p': 1, 'loss_target': 0}
SHARED_INPUTS = []
_WEIGHT_DTYPES = {'w_in_even': _jnp.float32, 'conv_a_w': _jnp.float32, 'conv_b_w': _jnp.float32, 'a_log': _jnp.float32, 'dt_bias': _jnp.float32, 'gdn_norm_g': _jnp.float32, 'w_out_even': _jnp.float32, 'w_in_odd': _jnp.float32, 'lower_bounds': _jnp.float32, 'hgrn_norm_g': _jnp.float32, 'w_out_odd': _jnp.float32, 'ln_g': _jnp.float32, 'ln_b': _jnp.float32, 'w_pl': _jnp.float32, 'w_pl_gate': _jnp.float32}
MOMENT_SCALE = {'w_in_even': 2.485256e-02, 'conv_a_w': 2.892229e-02, 'conv_b_w': 2.440351e-02, 'a_log': 2.061323e-01, 'dt_bias': 2.016308e-01, 'gdn_norm_g': 1.598400e-01, 'w_out_even': 1.162345e-01, 'w_in_odd': 2.344863e-02, 'lower_bounds': 2.798635e-03, 'hgrn_norm_g': 1.393013e-01, 'w_out_odd': 9.242606e-02, 'ln_g': 2.334437e+01, 'ln_b': 2.986443e+00, 'w_pl': 2.957229e-01, 'w_pl_gate': 7.171813e-02}


def _to_microbatches(a, axis):
    t = _jnp.moveaxis(a, axis, 0)
    t = t.reshape((N_MICROBATCH, t.shape[0] // N_MICROBATCH) + t.shape[1:])
    return _jnp.moveaxis(t, 1, axis + 1)


def setup_inputs(seed: int = 0) -> dict:
    inp = _fwd_setup_inputs(seed)
    key = _jax.random.fold_in(_jax.random.key(seed), 7919)
    shape, _ = _output_shape()
    out = dict(inp)
    out["loss_target"] = _jax.random.normal(_jax.random.fold_in(key, 0), shape, _jnp.float32)
    for i, name in enumerate(TWIN_WEIGHTS):
        w = inp[name].astype(_jnp.float32)
        if MOMENT_SCALE is None:
            s = _jnp.sqrt(_jnp.mean(_jnp.square(w)) + 1e-30)
        else:
            s = MOMENT_SCALE[name]
        km, kv = _jax.random.split(_jax.random.fold_in(key, i + 1))
        out[name] = w
        out["m_" + name] = s * _jax.random.normal(km, w.shape, _jnp.float32)
        out["v_" + name] = (s * s) * _jax.random.uniform(kv, w.shape, _jnp.float32, 0.5, 1.5)
    if N_MICROBATCH > 1:
        for name, axis in PER_EXAMPLE_BATCH_AXIS.items():
            out[name] = _to_microbatches(out[name], axis)
    return {'x': out['x'], 'p': out['p'], 'w_in_even': out['w_in_even'], 'conv_a_w': out['conv_a_w'], 'conv_b_w': out['conv_b_w'], 'a_log': out['a_log'], 'dt_bias': out['dt_bias'], 'gdn_norm_g': out['gdn_norm_g'], 'w_out_even': out['w_out_even'], 'w_in_odd': out['w_in_odd'], 'lower_bounds': out['lower_bounds'], 'hgrn_norm_g': out['hgrn_norm_g'], 'w_out_odd': out['w_out_odd'], 'ln_g': out['ln_g'], 'ln_b': out['ln_b'], 'w_pl': out['w_pl'], 'w_pl_gate': out['w_pl_gate'], 'loss_target': out['loss_target'], 'm_w_in_even': out['m_w_in_even'], 'm_conv_a_w': out['m_conv_a_w'], 'm_conv_b_w': out['m_conv_b_w'], 'm_a_log': out['m_a_log'], 'm_dt_bias': out['m_dt_bias'], 'm_gdn_norm_g': out['m_gdn_norm_g'], 'm_w_out_even': out['m_w_out_even'], 'm_w_in_odd': out['m_w_in_odd'], 'm_lower_bounds': out['m_lower_bounds'], 'm_hgrn_norm_g': out['m_hgrn_norm_g'], 'm_w_out_odd': out['m_w_out_odd'], 'm_ln_g': out['m_ln_g'], 'm_ln_b': out['m_ln_b'], 'm_w_pl': out['m_w_pl'], 'm_w_pl_gate': out['m_w_pl_gate'], 'v_w_in_even': out['v_w_in_even'], 'v_conv_a_w': out['v_conv_a_w'], 'v_conv_b_w': out['v_conv_b_w'], 'v_a_log': out['v_a_log'], 'v_dt_bias': out['v_dt_bias'], 'v_gdn_norm_g': out['v_gdn_norm_g'], 'v_w_out_even': out['v_w_out_even'], 'v_w_in_odd': out['v_w_in_odd'], 'v_lower_bounds': out['v_lower_bounds'], 'v_hgrn_norm_g': out['v_hgrn_norm_g'], 'v_w_out_odd': out['v_w_out_odd'], 'v_ln_g': out['v_ln_g'], 'v_ln_b': out['v_ln_b'], 'v_w_pl': out['v_w_pl'], 'v_w_pl_gate': out['v_w_pl_gate']}


def _loss(weights, diff, rest, loss_target):
    with _jax.named_scope("forward"):
        args = {**rest, TWIN_DIFF_INPUT: diff, **{k: w.astype(_WEIGHT_DTYPES[k]) for k, w in weights.items()}}
        y = _forward(args)
    with _jax.named_scope("loss_head"):
        err = _jnp.square(y.astype(_jnp.float32) - loss_target)
        return 0.5 * _jnp.sum(_jnp.mean(err, axis=-1)) if err.ndim else 0.5 * err


def _adamw(w, g, m, v):
    m = ADAM_B1 * m + (1.0 - ADAM_B1) * g
    v = ADAM_B2 * v + (1.0 - ADAM_B2) * _jnp.square(g)
    m_hat = m / (1.0 - ADAM_B1 ** ADAM_STEP)
    v_hat = v / (1.0 - ADAM_B2 ** ADAM_STEP)
    delta = -ADAM_LR * (m_hat / (_jnp.sqrt(v_hat) + ADAM_EPS) + ADAM_WD * w)
    return delta, m, v


def reference(x, p, w_in_even, conv_a_w, conv_b_w, a_log, dt_bias, gdn_norm_g, w_out_even, w_in_odd, lower_bounds, hgrn_norm_g, w_out_odd, ln_g, ln_b, w_pl, w_pl_gate, loss_target, m_w_in_even, m_conv_a_w, m_conv_b_w, m_a_log, m_dt_bias, m_gdn_norm_g, m_w_out_even, m_w_in_odd, m_lower_bounds, m_hgrn_norm_g, m_w_out_odd, m_ln_g, m_ln_b, m_w_pl, m_w_pl_gate, v_w_in_even, v_conv_a_w, v_conv_b_w, v_a_log, v_dt_bias, v_gdn_norm_g, v_w_out_even, v_w_in_odd, v_lower_bounds, v_hgrn_norm_g, v_w_out_odd, v_ln_g, v_ln_b, v_w_pl, v_w_pl_gate):
    given = dict(x=x, p=p, w_in_even=w_in_even, conv_a_w=conv_a_w, conv_b_w=conv_b_w, a_log=a_log, dt_bias=dt_bias, gdn_norm_g=gdn_norm_g, w_out_even=w_out_even, w_in_odd=w_in_odd, lower_bounds=lower_bounds, hgrn_norm_g=hgrn_norm_g, w_out_odd=w_out_odd, ln_g=ln_g, ln_b=ln_b, w_pl=w_pl, w_pl_gate=w_pl_gate, loss_target=loss_target, m_w_in_even=m_w_in_even, m_conv_a_w=m_conv_a_w, m_conv_b_w=m_conv_b_w, m_a_log=m_a_log, m_dt_bias=m_dt_bias, m_gdn_norm_g=m_gdn_norm_g, m_w_out_even=m_w_out_even, m_w_in_odd=m_w_in_odd, m_lower_bounds=m_lower_bounds, m_hgrn_norm_g=m_hgrn_norm_g, m_w_out_odd=m_w_out_odd, m_ln_g=m_ln_g, m_ln_b=m_ln_b, m_w_pl=m_w_pl, m_w_pl_gate=m_w_pl_gate, v_w_in_even=v_w_in_even, v_conv_a_w=v_conv_a_w, v_conv_b_w=v_conv_b_w, v_a_log=v_a_log, v_dt_bias=v_dt_bias, v_gdn_norm_g=v_gdn_norm_g, v_w_out_even=v_w_out_even, v_w_in_odd=v_w_in_odd, v_lower_bounds=v_lower_bounds, v_hgrn_norm_g=v_hgrn_norm_g, v_w_out_odd=v_w_out_odd, v_ln_g=v_ln_g, v_ln_b=v_ln_b, v_w_pl=v_w_pl, v_w_pl_gate=v_w_pl_gate)
    weights = {n: given[n] for n in TWIN_WEIGHTS}
    shared = {n: given[n] for n in SHARED_INPUTS}
    per_example = {n: given[n] for n in ['x', 'p']}
    grad_fn = _jax.value_and_grad(_loss, argnums=(0, 1))

    def one_microbatch(ex, loss_target):
        ex = dict(ex)
        diff = ex.pop(TWIN_DIFF_INPUT)
        return grad_fn(weights, diff, {**shared, **ex}, loss_target)

    if N_MICROBATCH == 1:
        loss, (grad_w, grad_x) = one_microbatch(per_example, given["loss_target"])
    else:
        def body(carry, xs):
            loss_sum, grad_sum = carry
            l_k, (gw_k, gx_k) = one_microbatch(xs[0], xs[1])
            with _jax.named_scope("update"):
                return (loss_sum + l_k, _jax.tree.map(_jnp.add, grad_sum, gw_k)), gx_k

        init = (_jnp.zeros((), _jnp.float32), _jax.tree.map(_jnp.zeros_like, weights))
        (loss, grad_w), grad_x = _jax.lax.scan(body, init, (per_example, given["loss_target"]))
    with _jax.named_scope("update"):
        delta_w, new_m, new_v = {}, {}, {}
        for n in TWIN_WEIGHTS:
            delta_w[n], new_m[n], new_v[n] = _adamw(weights[n], grad_w[n], given["m_" + n], given["v_" + n])
    return (loss, grad_x, *[grad_w[n] for n in TWIN_WEIGHTS], *[delta_w[n] for n in TWIN_WEIGHTS],
            *[new_m[n] for n in TWIN_WEIGHTS], *[new_v[n] for n in TWIN_WEIGHTS])
```

```python
import functools

import jax
import jax.numpy as jnp
from jax import lax
from jax.experimental import pallas as pl
from jax.experimental.pallas import tpu as pltpu

F32 = jnp.float32
BF16 = jnp.bfloat16
MESH = pl.DeviceIdType.MESH
AXES = ("x", "y", "c")

LANES = 128
SUBLANES = 8
HEAD = 128
GDN_CHUNK = 64
HGRN_CHUNK = 32
HGRN_SUB = 16
HGRN_HEADS_PER_STEP = 8
NORM_EPS = 1e-5
DEPTH = 2
ALPHA = (2.0 * DEPTH) ** 0.25
EXP_CLAMP = 80.0
ADAM_LR, ADAM_B1, ADAM_B2, ADAM_EPS, ADAM_WD, ADAM_STEP = 0.001, 0.9, 0.999, 1e-08, 0.01, 10
VMEM_LIMIT = 56 * 1024 * 1024
ROW_TILE = 512
POST_TILE = 256
SMALL_ROWS = 96

_DIMS = {"nn": ((1,), (0,)), "nt": ((1,), (1,)), "tn": ((0,), (0,))}


def _params(**kw):
    return pltpu.CompilerParams(vmem_limit_bytes=VMEM_LIMIT, **kw)


def _dot_raw(a, b, kind, hi):
    dims = (_DIMS[kind], ((), ()))
    if hi:
        return lax.dot_general(a, b, dims, precision=lax.Precision.HIGHEST, preferred_element_type=F32)
    return lax.dot_general(a.astype(BF16), b.astype(BF16), dims, preferred_element_type=F32)


@functools.partial(jax.custom_vjp, nondiff_argnums=(2, 3))
def mdot(a, b, kind, hi):
    return _dot_raw(a, b, kind, hi)


def _mdot_fwd(a, b, kind, hi):
    return _dot_raw(a, b, kind, hi), (a, b)


def _mdot_bwd(kind, hi, res, g):
    a, b = res
    if kind == "nn":
        return _dot_raw(g, b, "nt", hi), _dot_raw(a, g, "tn", hi)
    if kind == "nt":
        return _dot_raw(g, b, "nn", hi), _dot_raw(g, a, "tn", hi)
    return _dot_raw(b, g, "nt", hi), _dot_raw(a, g, "nn", hi)


mdot.defvjp(_mdot_fwd, _mdot_bwd)


@functools.partial(jax.custom_vjp, nondiff_argnums=(1, 2, 3))
def _rows(x, lo, hi, n):
    return x[lo:hi]


def _rows_fwd(x, lo, hi, n):
    return x[lo:hi], None


def _rows_bwd(lo, hi, n, _, g):
    parts = []
    if lo > 0:
        parts.append(jnp.zeros((lo, g.shape[1]), g.dtype))
    parts.append(g)
    if n - hi > 0:
        parts.append(jnp.zeros((n - hi, g.shape[1]), g.dtype))
    return (jnp.concatenate(parts, axis=0) if len(parts) > 1 else g,)


_rows.defvjp(_rows_fwd, _rows_bwd)


def _sigmoid(x):
    return jax.nn.sigmoid(x)


def _silu(x):
    return x * _sigmoid(x)


def _dsilu(x):
    s = _sigmoid(x)
    return s * (1.0 + x * (1.0 - s))


def _log1p(u):
    return jnp.where(u < 1e-4, u * (1.0 - 0.5 * u), jnp.log(1.0 + u))


def _softplus(x):
    return jnp.maximum(x, 0.0) + _log1p(jnp.exp(-jnp.abs(x)))


def _rms_gate(o, gain, z):
    return o * lax.rsqrt(jnp.mean(o * o, axis=-1, keepdims=True) + NORM_EPS) * gain * _silu(z)


def _l2n(x):
    return x * lax.rsqrt(jnp.sum(x * x, axis=-1, keepdims=True) + 1e-6)


def _unit_lower_inverse(low, eye_f, n):
    inv = eye_f - low
    power = low
    span = 2
    while span < n:
        power = mdot(power, power, "nn", True)
        inv = mdot(inv, eye_f + power, "nn", True)
        span *= 2
    return inv


def _gdn_step(S, q, k, v, z, small, alog, dtb, gain):
    H = len(S)
    C = GDN_CHUNK
    row = lax.broadcasted_iota(jnp.int32, (C, C), 0)
    col = lax.broadcasted_iota(jnp.int32, (C, C), 1)
    tril, strict, eye = row >= col, row > col, row == col
    tril_f, eye_f = tril.astype(F32), eye.astype(F32)
    lane = lax.broadcasted_iota(jnp.int32, (C, LANES), 1)
    rowc = lax.broadcasted_iota(jnp.int32, (C, 1), 0)
    beta_all = _sigmoid(small)
    g_all = -jnp.exp(alog) * _softplus(small + dtb)
    gc_all = mdot(tril_f, g_all, "nn", True)
    outs, states = [], []
    for h in range(H):
        beta = jnp.sum(jnp.where(lane == h, beta_all, 0.0), axis=1, keepdims=True)
        gc = jnp.sum(jnp.where(lane == H + h, gc_all, 0.0), axis=1, keepdims=True)
        gc_row = jnp.sum(jnp.where(eye, gc, 0.0), axis=0, keepdims=True)
        decay = jnp.where(tril, jnp.exp(jnp.where(tril, gc - gc_row, 0.0)), 0.0)
        g_last = jnp.sum(jnp.where(rowc == C - 1, gc, 0.0), axis=0, keepdims=True)
        qn = _l2n(q[h]) * (HEAD ** -0.5)
        kn = _l2n(k[h])
        kb = kn * beta
        low = jnp.where(strict, mdot(kb, kn, "nt", False) * decay, 0.0)
        inv = _unit_lower_inverse(low, eye_f, C)
        eg = jnp.exp(gc)
        u = mdot(inv, v[h] * beta, "nn", True)
        w = mdot(inv, kb * eg, "nn", True)
        attn = mdot(qn, kn, "nt", False) * decay
        v_new = u - mdot(w, S[h], "nn", False)
        o = mdot(qn * eg, S[h], "nn", False) + mdot(attn, v_new, "nn", False)
        k_dec = kn * jnp.exp(g_last - gc)
        states.append(S[h] * jnp.exp(g_last) + mdot(k_dec, v_new, "tn", False))
        outs.append(_rms_gate(o, gain, z[h]))
    return tuple(outs), tuple(states)


def _hgrn_step(St, qr, fr, vi, z, lb0, lb1, gain):
    C, SB = HGRN_CHUNK, HGRN_SUB
    row = lax.broadcasted_iota(jnp.int32, (C, C), 0)
    col = lax.broadcasted_iota(jnp.int32, (C, C), 1)
    blk_start = row - (row & (SB - 1))
    in_blk_f = ((row >= col) & (col >= blk_start)).astype(F32)
    before_f = (col < blk_start).astype(F32)
    outs, states = [], []
    for h in range(len(St)):
        m = jnp.maximum(lb0[h], lb1[h])
        e0, e1 = jnp.exp(lb0[h] - m), jnp.exp(lb1[h] - m)
        lb = e1 / (e0 + e1)
        f = lb + (1.0 - lb) * _sigmoid(fr[h])
        q = _silu(qr[h])
        k = 1.0 - f
        logf = jnp.log(f)
        inner = mdot(in_blk_f, logf, "nn", True)
        start = mdot(before_f, logf, "nn", True)
        b = start + inner
        b_last = jnp.sum(logf, axis=0, keepdims=True)
        o = mdot(q * jnp.exp(b), St[h], "nt", False)
        qt = q * jnp.exp(inner)
        parts = []
        for blk in range(C // SB):
            lo, n = blk * SB, (blk + 1) * SB
            ref = jnp.concatenate([_rows(start, lo, n, C)] * (blk + 1), axis=0)
            kt = _rows(k, 0, n, C) * jnp.exp(jnp.minimum(ref - _rows(b, 0, n, C), EXP_CLAMP))
            att = mdot(_rows(qt, lo, n, C), kt, "nt", False)
            t_idx = lax.broadcasted_iota(jnp.int32, (SB, n), 0) + lo
            s_idx = lax.broadcasted_iota(jnp.int32, (SB, n), 1)
            att = jnp.where(s_idx <= t_idx, att, 0.0)
            parts.append(mdot(att, _rows(vi[h], 0, n, C), "nn", False))
        o = o + jnp.concatenate(parts, axis=0)
        k_dec = k * jnp.exp(b_last - b)
        states.append(St[h] * jnp.exp(b_last) + mdot(vi[h], k_dec, "tn", False))
        outs.append(_rms_gate(o, gain, z[h]))
    return tuple(outs), tuple(states)


def _post_norm(s, x, g, b):
    r = ALPHA * x + s
    d = r - jnp.mean(r, axis=-1, keepdims=True)
    var = jnp.mean(d * d, axis=-1, keepdims=True)
    return d * lax.rsqrt(var + NORM_EPS) * g + b


def _post_gate(x1, gate_pre, pp):
    return x1 + pp * _sigmoid(gate_pre)


def _pick(dim, cands):
    for c in cands:
        if dim % c == 0:
            return c
    return dim


def _matmul(a, b, kind, name, add=None):
    if kind == "nn":
        (M, K), N = a.shape, b.shape[1]
    elif kind == "nt":
        (M, K), N = a.shape, b.shape[0]
    else:
        (K, M), N = a.shape, b.shape[1]
    tm = _pick(M, (512, 256, 128))
    tn = _pick(N, (1024, 640, 512, 384, 256, 128))
    tk = _pick(K, (1024, 640, 512, 384, 256, 128))
    nk = K // tk
    a_spec = pl.BlockSpec((tk, tm), lambda i, j, k: (k, i)) if kind == "tn" else pl.BlockSpec((tm, tk), lambda i, j, k: (i, k))
    b_spec = pl.BlockSpec((tn, tk), lambda i, j, k: (j, k)) if kind == "nt" else pl.BlockSpec((tk, tn), lambda i, j, k: (k, j))
    o_spec = pl.BlockSpec((tm, tn), lambda i, j, k: (i, j))
    has_add = add is not None

    def body(a_ref, b_ref, *rest):
        add_ref = rest[0] if has_add else None
        o_ref = rest[1] if has_add else rest[0]
        part = _dot_raw(a_ref[...], b_ref[...], kind, False)
        if nk == 1:
            o_ref[...] = part + add_ref[...] if has_add else part
            return
        acc = rest[-1]
        kk = pl.program_id(2)

        @pl.when(kk == 0)
        def _():
            acc[...] = part

        @pl.when(kk > 0)
        def _():
            acc[...] += part

        @pl.when(kk == nk - 1)
        def _():
            o_ref[...] = acc[...] + add_ref[...] if has_add else acc[...]

    return pl.pallas_call(
        body, name=name, grid=(M // tm, N // tn, nk),
        in_specs=[a_spec, b_spec] + ([o_spec] if has_add else []),
        out_specs=o_spec, out_shape=jax.ShapeDtypeStruct((M, N), F32),
        scratch_shapes=[pltpu.VMEM((tm, tn), F32)] if nk > 1 else [],
        compiler_params=_params(dimension_semantics=("parallel", "parallel", "arbitrary")),
    )(*((a, b, add) if has_add else (a, b)))


def _halo_specs(ts, nt, width, prev=True, main=True, nxt=True):
    per = ts // SUBLANES
    last8 = nt * per - 1
    specs = []
    if prev:
        specs.append(pl.BlockSpec((SUBLANES, width), lambda cb, i: (jnp.maximum(i * per - 1, 0), cb)))
    if main:
        specs.append(pl.BlockSpec((ts, width), lambda cb, i: (i, cb)))
    if nxt:
        specs.append(pl.BlockSpec((SUBLANES, width), lambda cb, i: (jnp.minimum((i + 1) * per, last8), cb)))
    return specs


def _taps(ext, ktaps, lo, size):
    return [ext[lo:lo + size] if j == 0 else pltpu.roll(ext, j, 0)[lo:lo + size] for j in range(ktaps)]


def _ahead(ext, j, size):
    n = ext.shape[0]
    return ext[:size] if j == 0 else pltpu.roll(ext, n - j, 0)[:size]


def _lane_block(ref, k):
    return ref[:, k * LANES:(k + 1) * LANES]


def _mixer_a_fwd(proj_a, conv_w):
    T = proj_a.shape[0]
    nblk = proj_a.shape[1] // (4 * LANES)
    ts = min(ROW_TILE, T)
    nt = T // ts

    def body(pp, pm, w_ref, y_ref):
        i = pl.program_id(1)
        u_prev = jnp.where(i > 0, _lane_block(pp, 0) * _lane_block(pp, 1), 0.0)
        ext = jnp.concatenate([u_prev, _lane_block(pm, 0) * _lane_block(pm, 1)], axis=0)
        t0, t1, t2 = _taps(ext, 3, SUBLANES, ts)
        cv = w_ref[2:3, :] * t0 + w_ref[1:2, :] * t1 + w_ref[0:1, :] * t2
        y_ref[...] = _lane_block(pm, 2) * cv * _silu(_lane_block(pm, 3))

    return pl.pallas_call(
        body, name="mixer_a_fwd", grid=(nblk, nt),
        in_specs=_halo_specs(ts, nt, 4 * LANES, nxt=False) + [pl.BlockSpec((conv_w.shape[0], LANES), lambda cb, i: (0, cb))],
        out_specs=pl.BlockSpec((ts, LANES), lambda cb, i: (i, cb)),
        out_shape=jax.ShapeDtypeStruct((T, nblk * LANES), F32), compiler_params=_params(),
    )(proj_a, proj_a, conv_w)


def _mixer_a_bwd(proj_a, dy, conv_w):
    T = proj_a.shape[0]
    nblk = proj_a.shape[1] // (4 * LANES)
    ts = min(ROW_TILE, T)
    nt = T // ts
    kt = conv_w.shape[0]

    def body(pp, pm, pn, dym, dyn, w_ref, dp_ref, dw_ref):
        i = pl.program_id(1)
        hm, cm, bm, zm = (_lane_block(pm, k) for k in range(4))
        u_prev = jnp.where(i > 0, _lane_block(pp, 0) * _lane_block(pp, 1), 0.0)
        ext = jnp.concatenate([u_prev, hm * cm], axis=0)
        dy_ext = jnp.concatenate([dym[...], jnp.where(i < nt - 1, dyn[...], 0.0)], axis=0)
        b_ext = jnp.concatenate([bm, _lane_block(pn, 2)], axis=0)
        sz_ext = _silu(jnp.concatenate([zm, _lane_block(pn, 3)], axis=0))
        dcv_ext = dy_ext * b_ext * sz_ext
        w = [w_ref[j:j + 1, :] for j in range(kt)]
        du = sum(w[kt - 1 - j] * _ahead(dcv_ext, j, ts) for j in range(kt))
        taps = _taps(ext, kt, SUBLANES, ts)
        cv = sum(w[kt - 1 - j] * taps[j] for j in range(kt))
        dp_ref[:, 0:LANES] = du * cm
        dp_ref[:, LANES:2 * LANES] = du * hm
        dp_ref[:, 2 * LANES:3 * LANES] = dym[...] * cv * sz_ext[:ts]
        dp_ref[:, 3 * LANES:4 * LANES] = dym[...] * bm * cv * _dsilu(zm)
        dcv = dcv_ext[:ts]

        @pl.when(i == 0)
        def _():
            dw_ref[...] = jnp.zeros_like(dw_ref)

        for j in range(kt):
            dw_ref[j:j + 1, :] += jnp.sum(dcv * taps[kt - 1 - j], axis=0, keepdims=True)

    return pl.pallas_call(
        body, name="mixer_a_bwd", grid=(nblk, nt),
        in_specs=_halo_specs(ts, nt, 4 * LANES) + _halo_specs(ts, nt, LANES, prev=False)
        + [pl.BlockSpec((kt, LANES), lambda cb, i: (0, cb))],
        out_specs=[pl.BlockSpec((ts, 4 * LANES), lambda cb, i: (i, cb)),
                   pl.BlockSpec((SUBLANES, LANES), lambda cb, i: (0, cb))],
        out_shape=[jax.ShapeDtypeStruct(proj_a.shape, F32), jax.ShapeDtypeStruct((SUBLANES, nblk * LANES), F32)],
        compiler_params=_params(),
    )(proj_a, proj_a, proj_a, dy, dy, conv_w)


def _conv_b_fwd(raw, conv_w):
    T = raw.shape[0]
    nblk = raw.shape[1] // LANES
    ts = min(ROW_TILE, T)
    nt = T // ts
    kt = conv_w.shape[0]

    def body(rp, rm, w_ref, y_ref):
        i = pl.program_id(1)
        ext = jnp.concatenate([jnp.where(i > 0, rp[...], 0.0), rm[...]], axis=0)
        taps = _taps(ext, kt, SUBLANES, ts)
        y_ref[...] = _silu(sum(w_ref[kt - 1 - j:kt - j, :] * taps[j] for j in range(kt)))

    return pl.pallas_call(
        body, name="conv_b_fwd", grid=(nblk, nt),
        in_specs=_halo_specs(ts, nt, LANES, nxt=False) + [pl.BlockSpec((kt, LANES), lambda cb, i: (0, cb))],
        out_specs=pl.BlockSpec((ts, LANES), lambda cb, i: (i, cb)),
        out_shape=jax.ShapeDtypeStruct(raw.shape, F32), compiler_params=_params(),
    )(raw, raw, conv_w)


def _conv_b_bwd(raw, dy, conv_w):
    T = raw.shape[0]
    nblk = raw.shape[1] // LANES
    ts = min(ROW_TILE, T)
    nt = T // ts
    kt = conv_w.shape[0]

    def body(rp, rm, rn, dym, dyn, w_ref, dr_ref, dw_ref):
        i = pl.program_id(1)
        ext = jnp.concatenate([jnp.where(i > 0, rp[...], 0.0), rm[...], rn[...]], axis=0)
        w = [w_ref[j:j + 1, :] for j in range(kt)]
        taps = _taps(ext, kt, SUBLANES, ts + SUBLANES)
        xc_ext = sum(w[kt - 1 - j] * taps[j] for j in range(kt))
        dy_ext = jnp.concatenate([dym[...], jnp.where(i < nt - 1, dyn[...], 0.0)], axis=0)
        dxc_ext = dy_ext * _dsilu(xc_ext)
        dr_ref[...] = sum(w[kt - 1 - j] * _ahead(dxc_ext, j, ts) for j in range(kt))
        dxc = dxc_ext[:ts]

        @pl.when(i == 0)
        def _():
            dw_ref[...] = jnp.zeros_like(dw_ref)

        for j in range(kt):
            dw_ref[j:j + 1, :] += jnp.sum(dxc * taps[kt - 1 - j][:ts], axis=0, keepdims=True)

    return pl.pallas_call(
        body, name="conv_b_bwd", grid=(nblk, nt),
        in_specs=_halo_specs(ts, nt, LANES) + _halo_specs(ts, nt, LANES, prev=False)
        + [pl.BlockSpec((kt, LANES), lambda cb, i: (0, cb))],
        out_specs=[pl.BlockSpec((ts, LANES), lambda cb, i: (i, cb)),
                   pl.BlockSpec((SUBLANES, LANES), lambda cb, i: (0, cb))],
        out_shape=[jax.ShapeDtypeStruct(raw.shape, F32), jax.ShapeDtypeStruct((SUBLANES, nblk * LANES), F32)],
        compiler_params=_params(),
    )(raw, raw, raw, dy, dy, conv_w)


def _split_heads(ref, base, nheads):
    return tuple(ref[:, base + h * HEAD: base + (h + 1) * HEAD] for h in range(nheads))


def _gdn_fwd(qkv, proj_zs, alog, dtb, gain, H):
    T = qkv.shape[0]
    C, HW = GDN_CHUNK, H * HEAD
    nc = T // C
    zw = HW + LANES

    def body(qkv_ref, zs_ref, alog_ref, dtb_ref, gain_ref, o_ref, sall_ref, s_scr):
        @pl.when(pl.program_id(0) == 0)
        def _():
            s_scr[...] = jnp.zeros_like(s_scr)

        sall_ref[0] = s_scr[...]
        outs, states = _gdn_step(
            tuple(s_scr[h] for h in range(H)), _split_heads(qkv_ref, 0, H), _split_heads(qkv_ref, HW, H),
            _split_heads(qkv_ref, 2 * HW, H), _split_heads(zs_ref, 0, H), zs_ref[:, HW:HW + LANES],
            alog_ref[...], dtb_ref[...], gain_ref[...])
        for h in range(H):
            o_ref[:, h * HEAD:(h + 1) * HEAD] = outs[h]
            s_scr[h] = states[h]

    row = pl.BlockSpec((1, LANES), lambda i: (0, 0))
    return pl.pallas_call(
        body, name="gdn_fwd", grid=(nc,),
        in_specs=[pl.BlockSpec((C, 3 * HW), lambda i: (i, 0)), pl.BlockSpec((C, zw), lambda i: (i, 0)), row, row, row],
        out_specs=[pl.BlockSpec((C, HW), lambda i: (i, 0)), pl.BlockSpec((1, H, HEAD, HEAD), lambda i: (i, 0, 0, 0))],
        out_shape=[jax.ShapeDtypeStruct((T, HW), F32), jax.ShapeDtypeStruct((nc, H, HEAD, HEAD), F32)],
        scratch_shapes=[pltpu.VMEM((H, HEAD, HEAD), F32)], compiler_params=_params(),
    )(qkv, proj_zs, alog, dtb, gain)


def _gdn_bwd(qkv, proj_zs, do, s_all, alog, dtb, gain, H):
    T = qkv.shape[0]
    C, HW = GDN_CHUNK, H * HEAD
    nc = T // C
    zw = HW + LANES

    def body(qkv_ref, zs_ref, do_ref, sin_ref, alog_ref, dtb_ref, gain_ref,
             dqkv_ref, dzs_ref, dalog_ref, ddtb_ref, dgain_ref, ds_scr):
        @pl.when(pl.program_id(0) == 0)
        def _():
            ds_scr[...] = jnp.zeros_like(ds_scr)
            dalog_ref[...] = jnp.zeros_like(dalog_ref)
            ddtb_ref[...] = jnp.zeros_like(ddtb_ref)
            dgain_ref[...] = jnp.zeros_like(dgain_ref)

        primals = (tuple(sin_ref[0, h] for h in range(H)), _split_heads(qkv_ref, 0, H),
                   _split_heads(qkv_ref, HW, H), _split_heads(qkv_ref, 2 * HW, H), _split_heads(zs_ref, 0, H),
                   zs_ref[:, HW:HW + LANES], alog_ref[...], dtb_ref[...], gain_ref[...])
        _, vjp = jax.vjp(_gdn_step, *primals)
        dS, dq, dk, dv, dz, dsmall, dalog, ddtb, dgain = vjp(
            (_split_heads(do_ref, 0, H), tuple(ds_scr[h] for h in range(H))))
        for h in range(H):
            ds_scr[h] = dS[h]
            dqkv_ref[:, h * HEAD:(h + 1) * HEAD] = dq[h]
            dqkv_ref[:, HW + h * HEAD:HW + (h + 1) * HEAD] = dk[h]
            dqkv_ref[:, 2 * HW + h * HEAD:2 * HW + (h + 1) * HEAD] = dv[h]
            dzs_ref[:, h * HEAD:(h + 1) * HEAD] = dz[h]
        dzs_ref[:, HW:HW + LANES] = dsmall
        dalog_ref[...] += dalog
        ddtb_ref[...] += ddtb
        dgain_ref[...] += dgain

    row = pl.BlockSpec((1, LANES), lambda i: (0, 0))
    rev = lambda i: nc - 1 - i
    return pl.pallas_call(
        body, name="gdn_bwd", grid=(nc,),
        in_specs=[pl.BlockSpec((C, 3 * HW), lambda i: (rev(i), 0)), pl.BlockSpec((C, zw), lambda i: (rev(i), 0)),
                  pl.BlockSpec((C, HW), lambda i: (rev(i), 0)),
                  pl.BlockSpec((1, H, HEAD, HEAD), lambda i: (rev(i), 0, 0, 0)), row, row, row],
        out_specs=[pl.BlockSpec((C, 3 * HW), lambda i: (rev(i), 0)), pl.BlockSpec((C, zw), lambda i: (rev(i), 0)),
                   row, row, row],
        out_shape=[jax.ShapeDtypeStruct(qkv.shape, F32), jax.ShapeDtypeStruct(proj_zs.shape, F32)]
        + [jax.ShapeDtypeStruct((1, LANES), F32)] * 3,
        scratch_shapes=[pltpu.VMEM((H, HEAD, HEAD), F32)], compiler_params=_params(),
    )(qkv, proj_zs, do, s_all, alog, dtb, gain)


def _hgrn_refs(proj_ref, lb_ref, HP):
    W = HP * HEAD
    lb0 = tuple(lb_ref[0:1, h * HEAD:(h + 1) * HEAD] for h in range(HP))
    lb1 = tuple(lb_ref[1:2, h * HEAD:(h + 1) * HEAD] for h in range(HP))
    return (_split_heads(proj_ref, 0, HP), _split_heads(proj_ref, W, HP), _split_heads(proj_ref, 2 * W, HP),
            _split_heads(proj_ref, 3 * W, HP), lb0, lb1)


def _hgrn_fwd(proj, lower_bounds, gain, nheads):
    T = proj.shape[0]
    C, HP = HGRN_CHUNK, HGRN_HEADS_PER_STEP
    ng, nc, W = nheads // HP, T // C, HP * HEAD

    def body(proj_ref, lb_ref, gain_ref, o_ref, sall_ref, s_scr):
        @pl.when(pl.program_id(1) == 0)
        def _():
            s_scr[...] = jnp.zeros_like(s_scr)

        sall_ref[0] = s_scr[...]
        qr, fr, vi, z, lb0, lb1 = _hgrn_refs(proj_ref, lb_ref, HP)
        outs, states = _hgrn_step(tuple(s_scr[h] for h in range(HP)), qr, fr, vi, z, lb0, lb1, gain_ref[...])
        for h in range(HP):
            o_ref[:, h * HEAD:(h + 1) * HEAD] = outs[h]
            s_scr[h] = states[h]

    return pl.pallas_call(
        body, name="hgrn_fwd", grid=(ng, nc),
        in_specs=[pl.BlockSpec((C, 4 * W), lambda g, i: (i, g)), pl.BlockSpec((2, W), lambda g, i: (0, g)),
                  pl.BlockSpec((1, LANES), lambda g, i: (0, 0))],
        out_specs=[pl.BlockSpec((C, W), lambda g, i: (i, g)),
                   pl.BlockSpec((1, HP, HEAD, HEAD), lambda g, i: (i, g, 0, 0))],
        out_shape=[jax.ShapeDtypeStruct((T, nheads * HEAD), F32), jax.ShapeDtypeStruct((nc, nheads, HEAD, HEAD), F32)],
        scratch_shapes=[pltpu.VMEM((HP, HEAD, HEAD), F32)], compiler_params=_params(),
    )(proj, lower_bounds, gain)


def _hgrn_bwd(proj, do, s_all, lower_bounds, gain, nheads):
    T = proj.shape[0]
    C, HP = HGRN_CHUNK, HGRN_HEADS_PER_STEP
    ng, nc, W = nheads // HP, T // C, HP * HEAD

    def body(proj_ref, do_ref, sin_ref, lb_ref, gain_ref, dproj_ref, dlb_ref, dgain_ref, ds_scr):
        first = pl.program_id(1) == 0

        @pl.when(first)
        def _():
            ds_scr[...] = jnp.zeros_like(ds_scr)
            dlb_ref[...] = jnp.zeros_like(dlb_ref)

        @pl.when(first & (pl.program_id(0) == 0))
        def _():
            dgain_ref[...] = jnp.zeros_like(dgain_ref)

        qr, fr, vi, z, lb0, lb1 = _hgrn_refs(proj_ref, lb_ref, HP)
        primals = (tuple(sin_ref[0, h] for h in range(HP)), qr, fr, vi, z, lb0, lb1, gain_ref[...])
        _, vjp = jax.vjp(_hgrn_step, *primals)
        dS, dq, df, dv, dz, dlb0, dlb1, dgain = vjp(
            (_split_heads(do_ref, 0, HP), tuple(ds_scr[h] for h in range(HP))))
        for h in range(HP):
            ds_scr[h] = dS[h]
            sl = slice(h * HEAD, (h + 1) * HEAD)
            dproj_ref[:, h * HEAD:(h + 1) * HEAD] = dq[h]
            dproj_ref[:, W + h * HEAD:W + (h + 1) * HEAD] = df[h]
            dproj_ref[:, 2 * W + h * HEAD:2 * W + (h + 1) * HEAD] = dv[h]
            dproj_ref[:, 3 * W + h * HEAD:3 * W + (h + 1) * HEAD] = dz[h]
            dlb_ref[0:1, sl] += dlb0[h]
            dlb_ref[1:2, sl] += dlb1[h]
        dgain_ref[...] += dgain

    rev = lambda i: nc - 1 - i
    return pl.pallas_call(
        body, name="hgrn_bwd", grid=(ng, nc),
        in_specs=[pl.BlockSpec((C, 4 * W), lambda g, i: (rev(i), g)), pl.BlockSpec((C, W), lambda g, i: (rev(i), g)),
                  pl.BlockSpec((1, HP, HEAD, HEAD), lambda g, i: (rev(i), g, 0, 0)),
                  pl.BlockSpec((2, W), lambda g, i: (0, g)), pl.BlockSpec((1, LANES), lambda g, i: (0, 0))],
        out_specs=[pl.BlockSpec((C, 4 * W), lambda g, i: (rev(i), g)), pl.BlockSpec((2, W), lambda g, i: (0, g)),
                   pl.BlockSpec((1, LANES), lambda g, i: (0, 0))],
        out_shape=[jax.ShapeDtypeStruct(proj.shape, F32), jax.ShapeDtypeStruct(lower_bounds.shape, F32),
                   jax.ShapeDtypeStruct((1, LANES), F32)],
        scratch_shapes=[pltpu.VMEM((HP, HEAD, HEAD), F32)], compiler_params=_params(),
    )(proj, do, s_all, lower_bounds, gain)


def _post_specs(T):
    tr = min(POST_TILE, T)
    tile = lambda w: pl.BlockSpec((tr, w), lambda i: (i, 0))
    full = lambda r, w: pl.BlockSpec((r, w), lambda i: (0, 0))
    return tr, tile, full


def _post_fwd(s, x, p, g, b, wg, wpl, name):
    T, D = x.shape
    P = p.shape[1]
    tr, tile, full = _post_specs(T)

    def body(s_ref, x_ref, p_ref, g_ref, b_ref, wg_ref, wpl_ref, o_ref):
        x1 = _post_norm(s_ref[...], x_ref[...], g_ref[...], b_ref[...])
        o_ref[...] = _post_gate(x1, _dot_raw(x1, wg_ref[...], "nn", False), _dot_raw(p_ref[...], wpl_ref[...], "nn", False))

    return pl.pallas_call(
        body, name=name, grid=(T // tr,),
        in_specs=[tile(D), tile(D), tile(P), full(1, D), full(1, D), full(D, D), full(P, D)],
        out_specs=tile(D), out_shape=jax.ShapeDtypeStruct((T, D), F32), compiler_params=_params(),
    )(s, x, p, g, b, wg, wpl)


def _post_bwd(s, x, p, g, b, wg, wpl, dnext, name, with_loss):
    T, D = x.shape
    P = p.shape[1]
    tr, tile, full = _post_specs(T)

    def body(s_ref, x_ref, p_ref, g_ref, b_ref, wg_ref, wpl_ref, dn_ref,
             ds_ref, dx_ref, dg_ref, db_ref, dwg_ref, dwpl_ref, loss_ref):
        @pl.when(pl.program_id(0) == 0)
        def _():
            for r in (dg_ref, db_ref, dwg_ref, dwpl_ref, loss_ref):
                r[...] = jnp.zeros_like(r)

        x1, vjp_norm = jax.vjp(_post_norm, s_ref[...], x_ref[...], g_ref[...], b_ref[...])
        gate_pre = _dot_raw(x1, wg_ref[...], "nn", False)
        pp = _dot_raw(p_ref[...], wpl_ref[...], "nn", False)
        xn, vjp_gate = jax.vjp(_post_gate, x1, gate_pre, pp)
        if with_loss:
            err = xn - dn_ref[...]
            loss_ref[...] += 0.5 * jnp.sum(jnp.sum(err * err, axis=-1, keepdims=True), axis=0, keepdims=True) / D
            dn = err / D
        else:
            dn = dn_ref[...]
        dx1, dgp, dpp = vjp_gate(dn)
        dwg_ref[...] += _dot_raw(x1, dgp, "tn", False)
        dwpl_ref[...] += _dot_raw(p_ref[...], dpp, "tn", False)
        dx1 = dx1 + _dot_raw(dgp, wg_ref[...], "nt", False)
        ds, dx, dg, db = vjp_norm(dx1)
        ds_ref[...] = ds
        dx_ref[...] = dx
        dg_ref[...] += dg
        db_ref[...] += db

    return pl.pallas_call(
        body, name=name, grid=(T // tr,),
        in_specs=[tile(D), tile(D), tile(P), full(1, D), full(1, D), full(D, D), full(P, D), tile(D)],
        out_specs=[tile(D), tile(D), full(1, D), full(1, D), full(D, D), full(P, D), full(SUBLANES, LANES)],
        out_shape=[jax.ShapeDtypeStruct((T, D), F32)] * 2 + [jax.ShapeDtypeStruct((1, D), F32)] * 2
        + [jax.ShapeDtypeStruct((D, D), F32), jax.ShapeDtypeStruct((P, D), F32),
           jax.ShapeDtypeStruct((SUBLANES, LANES), F32)],
        compiler_params=_params(),
    )(s, x, p, g, b, wg, wpl, dnext)


def _adam_math(w, g, m, v):
    m = ADAM_B1 * m + (1.0 - ADAM_B1) * g
    v = ADAM_B2 * v + (1.0 - ADAM_B2) * (g * g)
    m_hat = m / (1.0 - ADAM_B1 ** ADAM_STEP)
    v_hat = v / (1.0 - ADAM_B2 ** ADAM_STEP)
    return -ADAM_LR * (m_hat / (jnp.sqrt(v_hat) + ADAM_EPS) + ADAM_WD * w), m, v


def _rs_add(g8, got, cidx, name):
    _, R, C = g8.shape
    tr = _pick(R, (256, 128, 64, 32, 16, 8))

    def body(c_ref, a_ref, b_ref, o_ref):
        o_ref[...] = a_ref[...] + b_ref[...]

    return pl.pallas_call(
        body, name=name,
        grid_spec=pltpu.PrefetchScalarGridSpec(
            num_scalar_prefetch=1, grid=(4, R // tr),
            in_specs=[pl.BlockSpec((1, tr, C), lambda q, i, c: (2 * q + c[0], i, 0)),
                      pl.BlockSpec((1, tr, C), lambda q, i, c: (q, i, 0))],
            out_specs=pl.BlockSpec((1, tr, C), lambda q, i, c: (q, i, 0))),
        out_shape=jax.ShapeDtypeStruct((4,) + g8.shape[1:], F32), compiler_params=_params(),
    )(cidx, g8, got)


def _adam_sharded(w, m, v, mine, got, qidx, name):
    R, C = w.shape
    tr = _pick(R, (256, 128, 64, 32, 16, 8))

    def body(q_ref, w_ref, m_ref, v_ref, p_ref, r0, r1, r2, g_ref, d_ref, mo_ref, vo_ref):
        g = ((p_ref[0] + r0[0]) + r1[0]) + r2[0]
        d, mn, vn = _adam_math(w_ref[...], g, m_ref[...], v_ref[...])
        g_ref[...] = g
        d_ref[...] = d
        mo_ref[...] = mn
        vo_ref[...] = vn

    t2 = pl.BlockSpec((tr, C), lambda i, q: (i, 0))
    slot = lambda k: pl.BlockSpec((1, tr, C), lambda i, q: (k, i, 0))
    return pl.pallas_call(
        body, name=name,
        grid_spec=pltpu.PrefetchScalarGridSpec(
            num_scalar_prefetch=1, grid=(R // tr,),
            in_specs=[t2, t2, t2, pl.BlockSpec((1, tr, C), lambda i, q: (q[0], i, 0)), slot(0), slot(1), slot(2)],
            out_specs=[t2, t2, t2, t2]),
        out_shape=[jax.ShapeDtypeStruct((R, C), F32)] * 4, compiler_params=_params(),
    )(qidx, w, m, v, mine, got, got, got)


def _adam_replicated(w, m, v, g8):
    def body(w_ref, m_ref, v_ref, g_ref, go_ref, d_ref, mo_ref, vo_ref):
        g = g_ref[0]
        for k in range(1, 8):
            g = g + g_ref[k]
        d, mn, vn = _adam_math(w_ref[...], g, m_ref[...], v_ref[...])
        go_ref[...] = g
        d_ref[...] = d
        mo_ref[...] = mn
        vo_ref[...] = vn

    return pl.pallas_call(
        body, name="adam_replicated", out_shape=[jax.ShapeDtypeStruct(w.shape, F32)] * 4, compiler_params=_params(),
    )(w, m, v, g8)


def _place():
    return lax.axis_index("x"), lax.axis_index("y"), lax.axis_index("c")


def _all_gather(shard, name):
    def body(x_ref, out_ref, send_sems, recv_sems, local_sem):
        x, y, c = _place()
        me, sibling = (x, y, c), (x, y, 1 - c)
        chips = [(1 - x, y), (x, 1 - y), (1 - x, 1 - y)]

        def slab(px, py, pc):
            return out_ref.at[4 * px + 2 * py + pc]

        def copy(k, block, to, src=None):
            return pltpu.make_async_remote_copy(
                src_ref=slab(*block) if src is None else src, dst_ref=slab(*block),
                send_sem=send_sems.at[k], recv_sem=recv_sems.at[k], device_id=to, device_id_type=MESH)

        mine = pltpu.make_async_copy(x_ref, slab(*me), local_sem)
        mine.start()
        first = [copy(0, me, sibling, src=x_ref)]
        first += [copy(1 + j, me, (*chip, c), src=x_ref) for j, chip in enumerate(chips)]
        for cp in first:
            cp.start()
        passed = [copy(4 + j, (*chip, c), sibling) for j, chip in enumerate(chips)]
        for j, chip in enumerate(chips):
            copy(1 + j, (*chip, c), me).wait_recv()
            passed[j].start()
        copy(0, sibling, me).wait_recv()
        for j, chip in enumerate(chips):
            copy(4 + j, (*chip, 1 - c), me).wait_recv()
        for cp in first + passed:
            cp.wait_send()
        mine.wait()

    return pl.pallas_call(
        body, name=name, out_shape=jax.ShapeDtypeStruct((8,) + shard.shape, shard.dtype),
        in_specs=[pl.BlockSpec(memory_space=pl.ANY)], out_specs=pl.BlockSpec(memory_space=pl.ANY),
        scratch_shapes=[pltpu.SemaphoreType.DMA((7,)), pltpu.SemaphoreType.DMA((7,)), pltpu.SemaphoreType.DMA],
    )(shard)


def _rs_to_sibling(g8, name):
    def body(g_ref, out_ref, send_sems, recv_sems):
        x, y, c = _place()
        copies = [pltpu.make_async_remote_copy(
            src_ref=g_ref.at[2 * q + (1 - c)], dst_ref=out_ref.at[q], send_sem=send_sems.at[q],
            recv_sem=recv_sems.at[q], device_id=(x, y, 1 - c), device_id_type=MESH) for q in range(4)]
        for cp in copies:
            cp.start()
        for cp in copies:
            cp.wait()

    return pl.pallas_call(
        body, name=name, out_shape=jax.ShapeDtypeStruct((4,) + g8.shape[1:], g8.dtype),
        in_specs=[pl.BlockSpec(memory_space=pl.ANY)], out_specs=pl.BlockSpec(memory_space=pl.ANY),
        scratch_shapes=[pltpu.SemaphoreType.DMA((4,)), pltpu.SemaphoreType.DMA((4,))],
    )(g8)


def _rs_to_chips(p4, name):
    def body(p_ref, out_ref, send_sems, recv_sems):
        x, y, c = _place()
        chips = [(1 - x, y), (x, 1 - y), (1 - x, 1 - y)]
        copies = [pltpu.make_async_remote_copy(
            src_ref=p_ref.at[2 * px + py], dst_ref=out_ref.at[j], send_sem=send_sems.at[j],
            recv_sem=recv_sems.at[j], device_id=(px, py, c), device_id_type=MESH) for j, (px, py) in enumerate(chips)]
        for cp in copies:
            cp.start()
        for cp in copies:
            cp.wait()

    return pl.pallas_call(
        body, name=name, out_shape=jax.ShapeDtypeStruct((3,) + p4.shape[1:], p4.dtype),
        in_specs=[pl.BlockSpec(memory_space=pl.ANY)], out_specs=pl.BlockSpec(memory_space=pl.ANY),
        scratch_shapes=[pltpu.SemaphoreType.DMA((3,)), pltpu.SemaphoreType.DMA((3,))],
    )(p4)


def _reduce_scatter_adam(g8, w, m, v, cidx, qidx, tag):
    got = _rs_to_sibling(g8, "rs_sibling_" + tag)
    chip_sums = _rs_add(g8, got, cidx, "rs_add_" + tag)
    got2 = _rs_to_chips(chip_sums, "rs_chips_" + tag)
    return _adam_sharded(w, m, v, chip_sums, got2, qidx, "adam_" + tag)


def _local_grads(x, p0, p1, target, w_zs, w_a, w_qkv, wino, woute_a, woute_b, wouto, wg, wpl, conv_a, conv_b,
                 a_log, dt_bias, gdn_gain, lower_bounds, hgrn_gain, ln_g, ln_b):
    H = a_log.shape[1]
    nheads_o = wouto.shape[0] // HEAD
    pad_small = ((0, 0), (H, LANES - 2 * H))
    alog_row = jnp.pad(a_log, pad_small)
    dtb_row = jnp.pad(dt_bias, pad_small)

    proj_zs = _matmul(x, w_zs, "nn", "proj_even_zs")
    proj_a = _matmul(x, w_a, "nn", "proj_even_a")
    proj_qkv = _matmul(x, w_qkv, "nn", "proj_even_qkv")
    y_a = _mixer_a_fwd(proj_a, conv_a)
    qkv = _conv_b_fwd(proj_qkv, conv_b)
    o2, s_gdn = _gdn_fwd(qkv, proj_zs, alog_row, dtb_row, gdn_gain, H)
    s_e = _matmul(o2, woute_b, "nn", "out_even_b", add=_matmul(y_a, woute_a, "nn", "out_even_a"))
    x2 = _post_fwd(s_e, x, p0, ln_g[0:1], ln_b[0:1], wg[0], wpl[0], "post_even_fwd")
    proj_o = _matmul(x2, wino, "nn", "proj_odd")
    o4, s_hgrn = _hgrn_fwd(proj_o, lower_bounds, hgrn_gain, nheads_o)
    s_o = _matmul(o4, wouto, "nn", "out_odd")
    ds_o, dx2, dlng1, dlnb1, dwg1, dwpl1, loss = _post_bwd(
        s_o, x2, p1, ln_g[1:2], ln_b[1:2], wg[1], wpl[1], target, "post_odd_loss_bwd", True)
    do4 = _matmul(ds_o, wouto, "nt", "d_out_odd_act")
    dwouto = _matmul(o4, ds_o, "tn", "d_out_odd_w")
    dproj_o, dlb, dhgain = _hgrn_bwd(proj_o, do4, s_hgrn, lower_bounds, hgrn_gain, nheads_o)
    dx2 = _matmul(dproj_o, wino, "nt", "d_proj_odd_act", add=dx2)
    dwino = _matmul(x2, dproj_o, "tn", "d_proj_odd_w")
    ds_e, dx, dlng0, dlnb0, dwg0, dwpl0, _ = _post_bwd(
        s_e, x, p0, ln_g[0:1], ln_b[0:1], wg[0], wpl[0], dx2, "post_even_bwd", False)
    dy_a = _matmul(ds_e, woute_a, "nt", "d_out_even_a_act")
    do2 = _matmul(ds_e, woute_b, "nt", "d_out_even_b_act")
    dwoute_a = _matmul(y_a, ds_e, "tn", "d_out_even_a_w")
    dwoute_b = _matmul(o2, ds_e, "tn", "d_out_even_b_w")
    dqkv, dproj_zs, dalog, ddtb, dggain = _gdn_bwd(qkv, proj_zs, do2, s_gdn, alog_row, dtb_row, gdn_gain, H)
    dproj_qkv, dconv_b = _conv_b_bwd(proj_qkv, dqkv, conv_b)
    dproj_a, dconv_a = _mixer_a_bwd(proj_a, dy_a, conv_a)
    dx = _matmul(dproj_zs, w_zs, "nt", "d_proj_even_zs_act", add=dx)
    dx = _matmul(dproj_a, w_a, "nt", "d_proj_even_a_act", add=dx)
    dx = _matmul(dproj_qkv, w_qkv, "nt", "d_proj_even_qkv_act", add=dx)
    dw_zs = _matmul(x, dproj_zs, "tn", "d_proj_even_zs_w")
    dw_a = _matmul(x, dproj_a, "tn", "d_proj_even_a_w")
    dw_qkv = _matmul(x, dproj_qkv, "tn", "d_proj_even_qkv_w")
    return dict(
        loss=loss[0, 0], grad_x=dx, w_zs=dw_zs, w_a=dw_a, w_qkv=dw_qkv, wino=dwino,
        woute=jnp.concatenate([dwoute_a, dwoute_b], axis=0), wouto=dwouto,
        wg=jnp.stack([dwg0, dwg1]), wpl=jnp.stack([dwpl0, dwpl1]),
        conv_a=dconv_a[:conv_a.shape[0]], conv_b=dconv_b[:conv_b.shape[0]],
        a_log=dalog[:, H:2 * H], dt_bias=ddtb[:, H:2 * H], gdn_gain=dggain, lower_bounds=dlb, hgrn_gain=dhgain,
        ln_g=jnp.concatenate([dlng0, dlng1], axis=0), ln_b=jnp.concatenate([dlnb0, dlnb1], axis=0))


def _pad_rows(a, rows):
    return jnp.pad(a, ((0, rows - a.shape[0]), (0, 0)))


def _pack_small(a_log, dt_bias, gdn_gain, lower_bounds, hgrn_gain, ln_g, ln_b):
    lane_pad = lambda a: _pad_rows(jnp.pad(a, ((0, 0), (0, LANES - a.shape[1]))), SUBLANES)
    parts = [lane_pad(a_log), lane_pad(dt_bias), lane_pad(gdn_gain), lower_bounds.reshape(-1, LANES),
             lane_pad(hgrn_gain), ln_g.reshape(-1, LANES), ln_b.reshape(-1, LANES)]
    packed = jnp.concatenate(parts, axis=0)
    assert packed.shape[0] == SMALL_ROWS, packed.shape
    return packed


def _unpack_small(packed, shapes):
    out, r = [], 0
    for shp in shapes:
        n = shp[0] * shp[1]
        if n < LANES * SUBLANES and shp[1] <= LANES:
            out.append(packed[r:r + shp[0], :shp[1]])
            r += SUBLANES
        else:
            rows = n // LANES
            out.append(packed[r:r + rows].reshape(shp))
            r += rows
    return out


def _split_in_even(w_full, AW, HW, H):
    D = w_full.shape[0]
    n_a = 4 * AW
    n_main = n_a + 3 * HW
    w_zs = jnp.concatenate([w_full[:, n_main:n_main + HW], w_full[:, n_main + HW:],
                            jnp.zeros((D, LANES - 2 * H), w_full.dtype)], axis=1)
    w_a = w_full[:, :n_a].reshape(D, 4, AW // LANES, LANES).transpose(0, 2, 1, 3).reshape(D, n_a)
    return w_zs, w_a, w_full[:, n_a:n_main]


def _join_in_even(d_zs, d_a, d_qkv, AW, HW, H):
    D = d_a.shape[0]
    a_nat = d_a.reshape(D, AW // LANES, 4, LANES).transpose(0, 2, 1, 3).reshape(D, 4 * AW)
    return jnp.concatenate([a_nat, d_qkv, d_zs[:, :HW], d_zs[:, HW:HW + 2 * H]], axis=1)


def kernel(x, p, w_in_even, conv_a_w, conv_b_w, a_log, dt_bias, gdn_norm_g, w_out_even, w_in_odd, lower_bounds, hgrn_norm_g, w_out_odd, ln_g, ln_b, w_pl, w_pl_gate, loss_target, m_w_in_even, m_conv_a_w, m_conv_b_w, m_a_log, m_dt_bias, m_gdn_norm_g, m_w_out_even, m_w_in_odd, m_lower_bounds, m_hgrn_norm_g, m_w_out_odd, m_ln_g, m_ln_b, m_w_pl, m_w_pl_gate, v_w_in_even, v_conv_a_w, v_conv_b_w, v_a_log, v_dt_bias, v_gdn_norm_g, v_w_out_even, v_w_in_odd, v_lower_bounds, v_hgrn_norm_g, v_w_out_odd, v_ln_g, v_ln_b, v_w_pl, v_w_pl_gate):
    xi, yi, ci = _place()
    cidx = jnp.reshape(ci, (1,)).astype(jnp.int32)
    qidx = jnp.reshape(2 * xi + yi, (1,)).astype(jnp.int32)
    D = x.shape[2]
    H = a_log.shape[1]
    HW = H * HEAD
    AW = conv_a_w.shape[2] * 8
    OW = w_out_odd.shape[1] * 8
    PD = w_pl.shape[1]
    ka, kb = conv_a_w.shape[1], conv_b_w.shape[1]
    ca, cb = conv_a_w.shape[2], conv_b_w.shape[2]
    gw = HGRN_HEADS_PER_STEP * HEAD
    ngrp = OW // gw

    g_ine = _all_gather(w_in_even[0].astype(BF16), "ag_w_in_even")
    w_zs, w_a, w_qkv = _split_in_even(jnp.transpose(g_ine, (1, 0, 2)).reshape(D, -1), AW, HW, H)
    g_ino = _all_gather(w_in_odd[0].astype(BF16), "ag_w_in_odd")
    wino = jnp.transpose(g_ino, (1, 0, 2)).reshape(D, 4, ngrp, gw).transpose(0, 2, 1, 3).reshape(D, 4 * OW)
    woute = _all_gather(w_out_even[0].astype(BF16), "ag_w_out_even").reshape(-1, D)
    wouto = _all_gather(w_out_odd[0].astype(BF16), "ag_w_out_odd").reshape(-1, D)
    g_gate = _all_gather(w_pl_gate.astype(BF16).reshape(-1, D), "ag_w_pl_gate")
    wg = g_gate.reshape(8, DEPTH, D // 8, D).transpose(1, 0, 2, 3).reshape(DEPTH, D, D)
    g_pl = _all_gather(w_pl.astype(BF16).reshape(DEPTH * PD, -1), "ag_w_pl")
    wpl = g_pl.reshape(8, DEPTH, PD, D // 8).transpose(1, 2, 0, 3).reshape(DEPTH, PD, D)
    taps = jnp.concatenate([_pad_rows(conv_a_w[0], SUBLANES), _pad_rows(conv_b_w[0], SUBLANES)], axis=1)
    g_taps = _all_gather(taps, "ag_conv")
    conv_a = jnp.transpose(g_taps[:, :ka, :ca], (1, 0, 2)).reshape(ka, 8 * ca)
    conv_b = jnp.transpose(g_taps[:, :kb, ca:], (1, 0, 2)).reshape(kb, 8 * cb)

    gr = _local_grads(x[0], p[0, 0], p[1, 0], loss_target[0], w_zs, w_a, w_qkv, wino, woute[:AW], woute[AW:], wouto,
                      wg, wpl, conv_a, conv_b, a_log, dt_bias, gdn_norm_g, lower_bounds, hgrn_norm_g, ln_g, ln_b)
    loss = lax.psum(gr["loss"], AXES)

    dw_nat = _join_in_even(gr["w_zs"], gr["w_a"], gr["w_qkv"], AW, HW, H)
    sh = w_in_even.shape[2]
    o_ine = _reduce_scatter_adam(dw_nat.reshape(D, 8, sh).transpose(1, 0, 2), w_in_even[0], m_w_in_even[0],
                                 v_w_in_even[0], cidx, qidx, "w_in_even")
    dwo = gr["wino"].reshape(D, ngrp, 4, gw).transpose(0, 2, 1, 3).reshape(D, 8, 4 * OW // 8).transpose(1, 0, 2)
    o_ino = _reduce_scatter_adam(dwo, w_in_odd[0], m_w_in_odd[0], v_w_in_odd[0], cidx, qidx, "w_in_odd")
    o_oute = _reduce_scatter_adam(gr["woute"].reshape(8, -1, D), w_out_even[0], m_w_out_even[0], v_w_out_even[0],
                                  cidx, qidx, "w_out_even")
    o_outo = _reduce_scatter_adam(gr["wouto"].reshape(8, -1, D), w_out_odd[0], m_w_out_odd[0], v_w_out_odd[0],
                                  cidx, qidx, "w_out_odd")
    dgate = gr["wg"].reshape(DEPTH, 8, D // 8, D).transpose(1, 0, 2, 3).reshape(8, DEPTH * D // 8, D)
    flat_gate = lambda a: a.reshape(DEPTH * D // 8, D)
    o_gate = _reduce_scatter_adam(dgate, flat_gate(w_pl_gate), flat_gate(m_w_pl_gate), flat_gate(v_w_pl_gate),
                                  cidx, qidx, "w_pl_gate")
    dpl = gr["wpl"].reshape(DEPTH, PD, 8, D // 8).transpose(2, 0, 1, 3).reshape(8, DEPTH * PD, D // 8)
    flat_pl = lambda a: a.reshape(DEPTH * PD, D // 8)
    o_pl = _reduce_scatter_adam(dpl, flat_pl(w_pl), flat_pl(m_w_pl), flat_pl(v_w_pl), cidx, qidx, "w_pl")
    dtaps = jnp.concatenate([_pad_rows(gr["conv_a"], SUBLANES).reshape(SUBLANES, 8, ca).transpose(1, 0, 2),
                             _pad_rows(gr["conv_b"], SUBLANES).reshape(SUBLANES, 8, cb).transpose(1, 0, 2)], axis=2)
    pack_taps = lambda a, b: jnp.concatenate([_pad_rows(a[0], SUBLANES), _pad_rows(b[0], SUBLANES)], axis=1)
    o_taps = _reduce_scatter_adam(dtaps, taps, pack_taps(m_conv_a_w, m_conv_b_w), pack_taps(v_conv_a_w, v_conv_b_w),
                                  cidx, qidx, "conv")

    small_g = _pack_small(gr["a_log"], gr["dt_bias"], gr["gdn_gain"], gr["lower_bounds"], gr["hgrn_gain"],
                          gr["ln_g"], gr["ln_b"])
    o_small = _adam_replicated(
        _pack_small(a_log, dt_bias, gdn_norm_g, lower_bounds, hgrn_norm_g, ln_g, ln_b),
        _pack_small(m_a_log, m_dt_bias, m_gdn_norm_g, m_lower_bounds, m_hgrn_norm_g, m_ln_g, m_ln_b),
        _pack_small(v_a_log, v_dt_bias, v_gdn_norm_g, v_lower_bounds, v_hgrn_norm_g, v_ln_g, v_ln_b),
        _all_gather(small_g, "ag_small_grads"))
    small_shapes = [a_log.shape, dt_bias.shape, gdn_norm_g.shape, lower_bounds.shape, hgrn_norm_g.shape,
                    ln_g.shape, ln_b.shape]

    def leaves(kind):
        s_alog, s_dt, s_gg, s_lb, s_hg, s_lng, s_lnb = _unpack_small(o_small[kind], small_shapes)
        t = o_taps[kind]
        return [o_ine[kind][None], t[None, :ka, :ca], t[None, :kb, ca:], s_alog, s_dt, s_gg, o_oute[kind][None],
                o_ino[kind][None], s_lb, s_hg, o_outo[kind][None], s_lng, s_lnb,
                o_pl[kind].reshape(w_pl.shape), o_gate[kind].reshape(w_pl_gate.shape)]

    return (loss, gr["grad_x"][None], *leaves(0), *leaves(1), *leaves(2), *leaves(3))
```

```python
import functools

import jax
import jax.numpy as jnp
from jax import lax
from jax.experimental import pallas as pl
from jax.experimental.pallas import tpu as pltpu

F32 = jnp.float32
BF16 = jnp.bfloat16
MESH = pl.DeviceIdType.MESH
AXES = ("x", "y", "c")

LANES = 128
SUBLANES = 8
HEAD = 128
GDN_CHUNK = 64
HGRN_CHUNK = 32
HGRN_SUB = 16
HGRN_HEADS_PER_STEP = 8
NORM_EPS = 1e-5
DEPTH = 2
ALPHA = (2.0 * DEPTH) ** 0.25
EXP_CLAMP = 80.0
ADAM_LR, ADAM_B1, ADAM_B2, ADAM_EPS, ADAM_WD, ADAM_STEP = 0.001, 0.9, 0.999, 1e-08, 0.01, 10
VMEM_LIMIT = 56 * 1024 * 1024
ROW_TILE = 512
POST_TILE = 256
SMALL_ROWS = 96

_DIMS = {"nn": ((1,), (0,)), "nt": ((1,), (1,)), "tn": ((0,), (0,))}


def _params(**kw):
    return pltpu.CompilerParams(vmem_limit_bytes=VMEM_LIMIT, **kw)


def _dot_raw(a, b, kind, hi):
    dims = (_DIMS[kind], ((), ()))
    if hi:
        return lax.dot_general(a, b, dims, precision=lax.Precision.HIGHEST, preferred_element_type=F32)
    return lax.dot_general(a.astype(BF16), b.astype(BF16), dims, preferred_element_type=F32)


@functools.partial(jax.custom_vjp, nondiff_argnums=(2, 3))
def mdot(a, b, kind, hi):
    return _dot_raw(a, b, kind, hi)


def _mdot_fwd(a, b, kind, hi):
    return _dot_raw(a, b, kind, hi), (a, b)


def _mdot_bwd(kind, hi, res, g):
    a, b = res
    if kind == "nn":
        return _dot_raw(g, b, "nt", hi), _dot_raw(a, g, "tn", hi)
    if kind == "nt":
        return _dot_raw(g, b, "nn", hi), _dot_raw(g, a, "tn", hi)
    return _dot_raw(b, g, "nt", hi), _dot_raw(a, g, "nn", hi)


mdot.defvjp(_mdot_fwd, _mdot_bwd)


@functools.partial(jax.custom_vjp, nondiff_argnums=(1, 2, 3))
def _rows(x, lo, hi, n):
    return x[lo:hi]


def _rows_fwd(x, lo, hi, n):
    return x[lo:hi], None


def _rows_bwd(lo, hi, n, _, g):
    parts = []
    if lo > 0:
        parts.append(jnp.zeros((lo, g.shape[1]), g.dtype))
    parts.append(g)
    if n - hi > 0:
        parts.append(jnp.zeros((n - hi, g.shape[1]), g.dtype))
    return (jnp.concatenate(parts, axis=0) if len(parts) > 1 else g,)


_rows.defvjp(_rows_fwd, _rows_bwd)


def _sigmoid(x):
    return jax.nn.sigmoid(x)


def _silu(x):
    return x * _sigmoid(x)


def _dsilu(x):
    s = _sigmoid(x)
    return s * (1.0 + x * (1.0 - s))


def _log1p(u):
    return jnp.where(u < 1e-4, u * (1.0 - 0.5 * u), jnp.log(1.0 + u))


def _softplus(x):
    return jnp.maximum(x, 0.0) + _log1p(jnp.exp(-jnp.abs(x)))


def _rms_gate(o, gain, z):
    return o * lax.rsqrt(jnp.mean(o * o, axis=-1, keepdims=True) + NORM_EPS) * gain * _silu(z)


def _l2n(x):
    return x * lax.rsqrt(jnp.sum(x * x, axis=-1, keepdims=True) + 1e-6)


def _split_dot_raw(m, x, kind):
    mb = m.astype(BF16)
    hi = x.astype(BF16)
    lo = (x - hi.astype(F32)).astype(BF16)
    dims = (_DIMS[kind], ((), ()))
    return (lax.dot_general(mb, hi, dims, preferred_element_type=F32)
            + lax.dot_general(mb, lo, dims, preferred_element_type=F32))


@jax.custom_vjp
def mask_dot(m, x):
    return _split_dot_raw(m, x, "nn")


def _mask_dot_fwd(m, x):
    return _split_dot_raw(m, x, "nn"), m


def _mask_dot_bwd(m, g):
    return jnp.zeros_like(m), _split_dot_raw(m, g, "tn")


mask_dot.defvjp(_mask_dot_fwd, _mask_dot_bwd)


def _unit_lower_inverse_minus_eye(low, n):
    rest = -low
    power = low
    span = 2
    while span < n:
        power = mdot(power, power, "nn", False)
        rest = rest + power + mdot(rest, power, "nn", False)
        span *= 2
    return rest


def _gdn_step(S, q, k, v, z, small, alog, dtb, gain):
    H = len(S)
    C = GDN_CHUNK
    row = lax.broadcasted_iota(jnp.int32, (C, C), 0)
    col = lax.broadcasted_iota(jnp.int32, (C, C), 1)
    tril, strict, eye = row >= col, row > col, row == col
    tril_f = tril.astype(F32)
    lane = lax.broadcasted_iota(jnp.int32, (C, LANES), 1)
    rowc = lax.broadcasted_iota(jnp.int32, (C, 1), 0)
    beta_all = _sigmoid(small)
    g_all = -jnp.exp(alog) * _softplus(small + dtb)
    gc_all = mask_dot(tril_f, g_all)
    outs, states = [], []
    for h in range(H):
        beta = jnp.sum(jnp.where(lane == h, beta_all, 0.0), axis=1, keepdims=True)
        gc = jnp.sum(jnp.where(lane == H + h, gc_all, 0.0), axis=1, keepdims=True)
        gc_row = jnp.sum(jnp.where(eye, gc, 0.0), axis=0, keepdims=True)
        decay = jnp.where(tril, jnp.exp(jnp.where(tril, gc - gc_row, 0.0)), 0.0)
        g_last = jnp.sum(jnp.where(rowc == C - 1, gc, 0.0), axis=0, keepdims=True)
        qn = _l2n(q[h]) * (HEAD ** -0.5)
        kn = _l2n(k[h])
        kb = kn * beta
        low = jnp.where(strict, mdot(kb, kn, "nt", False) * decay, 0.0)
        inv_rest = _unit_lower_inverse_minus_eye(low, C)
        eg = jnp.exp(gc)
        vb, kbe = v[h] * beta, kb * eg
        u = vb + mdot(inv_rest, vb, "nn", False)
        w = kbe + mdot(inv_rest, kbe, "nn", False)
        attn = mdot(qn, kn, "nt", False) * decay
        v_new = u - mdot(w, S[h], "nn", False)
        o = mdot(qn * eg, S[h], "nn", False) + mdot(attn, v_new, "nn", False)
        k_dec = kn * jnp.exp(g_last - gc)
        states.append(S[h] * jnp.exp(g_last) + mdot(k_dec, v_new, "tn", False))
        outs.append(_rms_gate(o, gain, z[h]))
    return tuple(outs), tuple(states)


def _hgrn_step(St, qr, fr, vi, z, lb0, lb1, gain):
    C, SB = HGRN_CHUNK, HGRN_SUB
    row = lax.broadcasted_iota(jnp.int32, (C, C), 0)
    col = lax.broadcasted_iota(jnp.int32, (C, C), 1)
    blk_start = row - (row & (SB - 1))
    in_blk_f = ((row >= col) & (col >= blk_start)).astype(F32)
    before_f = (col < blk_start).astype(F32)
    sums_f = jnp.concatenate([in_blk_f, before_f], axis=0)
    outs, states = [], []
    for h in range(len(St)):
        m = jnp.maximum(lb0[h], lb1[h])
        e0, e1 = jnp.exp(lb0[h] - m), jnp.exp(lb1[h] - m)
        lb = e1 / (e0 + e1)
        f = lb + (1.0 - lb) * _sigmoid(fr[h])
        q = _silu(qr[h])
        k = 1.0 - f
        logf = jnp.log(f)
        sums = mask_dot(sums_f, logf)
        inner, start = _rows(sums, 0, C, 2 * C), _rows(sums, C, 2 * C, 2 * C)
        b = start + inner
        b_last = jnp.sum(logf, axis=0, keepdims=True)
        o = mdot(q * jnp.exp(b), St[h], "nt", False)
        qt = q * jnp.exp(inner)
        parts = []
        for blk in range(C // SB):
            lo, n = blk * SB, (blk + 1) * SB
            ref = jnp.concatenate([_rows(start, lo, n, C)] * (blk + 1), axis=0)
            kt = _rows(k, 0, n, C) * jnp.exp(jnp.minimum(ref - _rows(b, 0, n, C), EXP_CLAMP))
            att = mdot(_rows(qt, lo, n, C), kt, "nt", False)
            t_idx = lax.broadcasted_iota(jnp.int32, (SB, n), 0) + lo
            s_idx = lax.broadcasted_iota(jnp.int32, (SB, n), 1)
            att = jnp.where(s_idx <= t_idx, att, 0.0)
            parts.append(mdot(att, _rows(vi[h], 0, n, C), "nn", False))
        o = o + jnp.concatenate(parts, axis=0)
        k_dec = k * jnp.exp(b_last - b)
        states.append(St[h] * jnp.exp(b_last) + mdot(vi[h], k_dec, "tn", False))
        outs.append(_rms_gate(o, gain, z[h]))
    return tuple(outs), tuple(states)


def _post_norm(s, x, g, b):
    r = ALPHA * x + s
    d = r - jnp.mean(r, axis=-1, keepdims=True)
    var = jnp.mean(d * d, axis=-1, keepdims=True)
    return d * lax.rsqrt(var + NORM_EPS) * g + b


def _post_gate(x1, gate_pre, pp):
    return x1 + pp * _sigmoid(gate_pre)


def _pick(dim, cands):
    for c in cands:
        if dim % c == 0:
            return c
    return dim


def _matmul(a, b, kind, name, add=None):
    if kind == "nn":
        (M, K), N = a.shape, b.shape[1]
    elif kind == "nt":
        (M, K), N = a.shape, b.shape[0]
    else:
        (K, M), N = a.shape, b.shape[1]
    tm = _pick(M, (512, 256, 128))
    tn = _pick(N, (1024, 640, 512, 384, 256, 128))
    tk = _pick(K, (1024, 640, 512, 384, 256, 128))
    nk = K // tk
    a_spec = pl.BlockSpec((tk, tm), lambda i, j, k: (k, i)) if kind == "tn" else pl.BlockSpec((tm, tk), lambda i, j, k: (i, k))
    b_spec = pl.BlockSpec((tn, tk), lambda i, j, k: (j, k)) if kind == "nt" else pl.BlockSpec((tk, tn), lambda i, j, k: (k, j))
    o_spec = pl.BlockSpec((tm, tn), lambda i, j, k: (i, j))
    has_add = add is not None

    def body(a_ref, b_ref, *rest):
        add_ref = rest[0] if has_add else None
        o_ref = rest[1] if has_add else rest[0]
        part = _dot_raw(a_ref[...], b_ref[...], kind, False)
        if nk == 1:
            o_ref[...] = part + add_ref[...] if has_add else part
            return
        acc = rest[-1]
        kk = pl.program_id(2)

        @pl.when(kk == 0)
        def _():
            acc[...] = part

        @pl.when(kk > 0)
        def _():
            acc[...] += part

        @pl.when(kk == nk - 1)
        def _():
            o_ref[...] = acc[...] + add_ref[...] if has_add else acc[...]

    return pl.pallas_call(
        body, name=name, grid=(M // tm, N // tn, nk),
        in_specs=[a_spec, b_spec] + ([o_spec] if has_add else []),
        out_specs=o_spec, out_shape=jax.ShapeDtypeStruct((M, N), F32),
        scratch_shapes=[pltpu.VMEM((tm, tn), F32)] if nk > 1 else [],
        compiler_params=_params(dimension_semantics=("parallel", "parallel", "arbitrary")),
    )(*((a, b, add) if has_add else (a, b)))


def _halo_specs(ts, nt, width, prev=True, main=True, nxt=True):
    per = ts // SUBLANES
    last8 = nt * per - 1
    specs = []
    if prev:
        specs.append(pl.BlockSpec((SUBLANES, width), lambda cb, i: (jnp.maximum(i * per - 1, 0), cb)))
    if main:
        specs.append(pl.BlockSpec((ts, width), lambda cb, i: (i, cb)))
    if nxt:
        specs.append(pl.BlockSpec((SUBLANES, width), lambda cb, i: (jnp.minimum((i + 1) * per, last8), cb)))
    return specs


def _taps(ext, ktaps, lo, size):
    return [ext[lo:lo + size] if j == 0 else pltpu.roll(ext, j, 0)[lo:lo + size] for j in range(ktaps)]


def _ahead(ext, j, size):
    n = ext.shape[0]
    return ext[:size] if j == 0 else pltpu.roll(ext, n - j, 0)[:size]


def _lane_block(ref, k):
    return ref[:, k * LANES:(k + 1) * LANES]


def _mixer_a_fwd(proj_a, conv_w):
    T = proj_a.shape[0]
    nblk = proj_a.shape[1] // (4 * LANES)
    ts = min(ROW_TILE, T)
    nt = T // ts

    def body(pp, pm, w_ref, y_ref):
        i = pl.program_id(1)
        u_prev = jnp.where(i > 0, _lane_block(pp, 0) * _lane_block(pp, 1), 0.0)
        ext = jnp.concatenate([u_prev, _lane_block(pm, 0) * _lane_block(pm, 1)], axis=0)
        t0, t1, t2 = _taps(ext, 3, SUBLANES, ts)
        cv = w_ref[2:3, :] * t0 + w_ref[1:2, :] * t1 + w_ref[0:1, :] * t2
        y_ref[...] = _lane_block(pm, 2) * cv * _silu(_lane_block(pm, 3))

    return pl.pallas_call(
        body, name="mixer_a_fwd", grid=(nblk, nt),
        in_specs=_halo_specs(ts, nt, 4 * LANES, nxt=False) + [pl.BlockSpec((conv_w.shape[0], LANES), lambda cb, i: (0, cb))],
        out_specs=pl.BlockSpec((ts, LANES), lambda cb, i: (i, cb)),
        out_shape=jax.ShapeDtypeStruct((T, nblk * LANES), F32), compiler_params=_params(),
    )(proj_a, proj_a, conv_w)


def _mixer_a_bwd(proj_a, dy, conv_w):
    T = proj_a.shape[0]
    nblk = proj_a.shape[1] // (4 * LANES)
    ts = min(ROW_TILE, T)
    nt = T // ts
    kt = conv_w.shape[0]

    def body(pp, pm, pn, dym, dyn, w_ref, dp_ref, dw_ref):
        i = pl.program_id(1)
        hm, cm, bm, zm = (_lane_block(pm, k) for k in range(4))
        u_prev = jnp.where(i > 0, _lane_block(pp, 0) * _lane_block(pp, 1), 0.0)
        ext = jnp.concatenate([u_prev, hm * cm], axis=0)
        dy_ext = jnp.concatenate([dym[...], jnp.where(i < nt - 1, dyn[...], 0.0)], axis=0)
        b_ext = jnp.concatenate([bm, _lane_block(pn, 2)], axis=0)
        sz_ext = _silu(jnp.concatenate([zm, _lane_block(pn, 3)], axis=0))
        dcv_ext = dy_ext * b_ext * sz_ext
        w = [w_ref[j:j + 1, :] for j in range(kt)]
        du = sum(w[kt - 1 - j] * _ahead(dcv_ext, j, ts) for j in range(kt))
        taps = _taps(ext, kt, SUBLANES, ts)
        cv = sum(w[kt - 1 - j] * taps[j] for j in range(kt))
        dp_ref[:, 0:LANES] = du * cm
        dp_ref[:, LANES:2 * LANES] = du * hm
        dp_ref[:, 2 * LANES:3 * LANES] = dym[...] * cv * sz_ext[:ts]
        dp_ref[:, 3 * LANES:4 * LANES] = dym[...] * bm * cv * _dsilu(zm)
        dcv = dcv_ext[:ts]

        @pl.when(i == 0)
        def _():
            dw_ref[...] = jnp.zeros_like(dw_ref)

        for j in range(kt):
            dw_ref[j:j + 1, :] += jnp.sum(dcv * taps[kt - 1 - j], axis=0, keepdims=True)

    return pl.pallas_call(
        body, name="mixer_a_bwd", grid=(nblk, nt),
        in_specs=_halo_specs(ts, nt, 4 * LANES) + _halo_specs(ts, nt, LANES, prev=False)
        + [pl.BlockSpec((kt, LANES), lambda cb, i: (0, cb))],
        out_specs=[pl.BlockSpec((ts, 4 * LANES), lambda cb, i: (i, cb)),
                   pl.BlockSpec((SUBLANES, LANES), lambda cb, i: (0, cb))],
        out_shape=[jax.ShapeDtypeStruct(proj_a.shape, F32), jax.ShapeDtypeStruct((SUBLANES, nblk * LANES), F32)],
        compiler_params=_params(),
    )(proj_a, proj_a, proj_a, dy, dy, conv_w)


def _conv_b_fwd(raw, conv_w):
    T = raw.shape[0]
    nblk = raw.shape[1] // LANES
    ts = min(ROW_TILE, T)
    nt = T // ts
    kt = conv_w.shape[0]

    def body(rp, rm, w_ref, y_ref):
        i = pl.program_id(1)
        ext = jnp.concatenate([jnp.where(i > 0, rp[...], 0.0), rm[...]], axis=0)
        taps = _taps(ext, kt, SUBLANES, ts)
        y_ref[...] = _silu(sum(w_ref[kt - 1 - j:kt - j, :] * taps[j] for j in range(kt)))

    return pl.pallas_call(
        body, name="conv_b_fwd", grid=(nblk, nt),
        in_specs=_halo_specs(ts, nt, LANES, nxt=False) + [pl.BlockSpec((kt, LANES), lambda cb, i: (0, cb))],
        out_specs=pl.BlockSpec((ts, LANES), lambda cb, i: (i, cb)),
        out_shape=jax.ShapeDtypeStruct(raw.shape, F32), compiler_params=_params(),
    )(raw, raw, conv_w)


def _conv_b_bwd(raw, dy, conv_w):
    T = raw.shape[0]
    nblk = raw.shape[1] // LANES
    ts = min(ROW_TILE, T)
    nt = T // ts
    kt = conv_w.shape[0]

    def body(rp, rm, rn, dym, dyn, w_ref, dr_ref, dw_ref):
        i = pl.program_id(1)
        ext = jnp.concatenate([jnp.where(i > 0, rp[...], 0.0), rm[...], rn[...]], axis=0)
        w = [w_ref[j:j + 1, :] for j in range(kt)]
        taps = _taps(ext, kt, SUBLANES, ts + SUBLANES)
        xc_ext = sum(w[kt - 1 - j] * taps[j] for j in range(kt))
        dy_ext = jnp.concatenate([dym[...], jnp.where(i < nt - 1, dyn[...], 0.0)], axis=0)
        dxc_ext = dy_ext * _dsilu(xc_ext)
        dr_ref[...] = sum(w[kt - 1 - j] * _ahead(dxc_ext, j, ts) for j in range(kt))
        dxc = dxc_ext[:ts]

        @pl.when(i == 0)
        def _():
            dw_ref[...] = jnp.zeros_like(dw_ref)

        for j in range(kt):
            dw_ref[j:j + 1, :] += jnp.sum(dxc * taps[kt - 1 - j][:ts], axis=0, keepdims=True)

    return pl.pallas_call(
        body, name="conv_b_bwd", grid=(nblk, nt),
        in_specs=_halo_specs(ts, nt, LANES) + _halo_specs(ts, nt, LANES, prev=False)
        + [pl.BlockSpec((kt, LANES), lambda cb, i: (0, cb))],
        out_specs=[pl.BlockSpec((ts, LANES), lambda cb, i: (i, cb)),
                   pl.BlockSpec((SUBLANES, LANES), lambda cb, i: (0, cb))],
        out_shape=[jax.ShapeDtypeStruct(raw.shape, F32), jax.ShapeDtypeStruct((SUBLANES, nblk * LANES), F32)],
        compiler_params=_params(),
    )(raw, raw, raw, dy, dy, conv_w)


def _split_heads(ref, base, nheads):
    return tuple(ref[:, base + h * HEAD: base + (h + 1) * HEAD] for h in range(nheads))


def _gdn_fwd(qkv, proj_zs, alog, dtb, gain, H):
    T = qkv.shape[0]
    C, HW = GDN_CHUNK, H * HEAD
    nc = T // C
    zw = HW + LANES

    def body(qkv_ref, zs_ref, alog_ref, dtb_ref, gain_ref, o_ref, sall_ref, s_scr):
        @pl.when(pl.program_id(0) == 0)
        def _():
            s_scr[...] = jnp.zeros_like(s_scr)

        sall_ref[0] = s_scr[...]
        outs, states = _gdn_step(
            tuple(s_scr[h] for h in range(H)), _split_heads(qkv_ref, 0, H), _split_heads(qkv_ref, HW, H),
            _split_heads(qkv_ref, 2 * HW, H), _split_heads(zs_ref, 0, H), zs_ref[:, HW:HW + LANES],
            alog_ref[...], dtb_ref[...], gain_ref[...])
        for h in range(H):
            o_ref[:, h * HEAD:(h + 1) * HEAD] = outs[h]
            s_scr[h] = states[h]

    row = pl.BlockSpec((1, LANES), lambda i: (0, 0))
    return pl.pallas_call(
        body, name="gdn_fwd", grid=(nc,),
        in_specs=[pl.BlockSpec((C, 3 * HW), lambda i: (i, 0)), pl.BlockSpec((C, zw), lambda i: (i, 0)), row, row, row],
        out_specs=[pl.BlockSpec((C, HW), lambda i: (i, 0)), pl.BlockSpec((1, H, HEAD, HEAD), lambda i: (i, 0, 0, 0))],
        out_shape=[jax.ShapeDtypeStruct((T, HW), F32), jax.ShapeDtypeStruct((nc, H, HEAD, HEAD), F32)],
        scratch_shapes=[pltpu.VMEM((H, HEAD, HEAD), F32)], compiler_params=_params(),
    )(qkv, proj_zs, alog, dtb, gain)


def _gdn_bwd(qkv, proj_zs, do, s_all, alog, dtb, gain, H):
    T = qkv.shape[0]
    C, HW = GDN_CHUNK, H * HEAD
    nc = T // C
    zw = HW + LANES

    def body(qkv_ref, zs_ref, do_ref, sin_ref, alog_ref, dtb_ref, gain_ref,
             dqkv_ref, dzs_ref, dalog_ref, ddtb_ref, dgain_ref, ds_scr):
        @pl.when(pl.program_id(0) == 0)
        def _():
            ds_scr[...] = jnp.zeros_like(ds_scr)
            dalog_ref[...] = jnp.zeros_like(dalog_ref)
            ddtb_ref[...] = jnp.zeros_like(ddtb_ref)
            dgain_ref[...] = jnp.zeros_like(dgain_ref)

        primals = (tuple(sin_ref[0, h] for h in range(H)), _split_heads(qkv_ref, 0, H),
                   _split_heads(qkv_ref, HW, H), _split_heads(qkv_ref, 2 * HW, H), _split_heads(zs_ref, 0, H),
                   zs_ref[:, HW:HW + LANES], alog_ref[...], dtb_ref[...], gain_ref[...])
        _, vjp = jax.vjp(_gdn_step, *primals)
        dS, dq, dk, dv, dz, dsmall, dalog, ddtb, dgain = vjp(
            (_split_heads(do_ref, 0, H), tuple(ds_scr[h] for h in range(H))))
        for h in range(H):
            ds_scr[h] = dS[h]
            dqkv_ref[:, h * HEAD:(h + 1) * HEAD] = dq[h]
            dqkv_ref[:, HW + h * HEAD:HW + (h + 1) * HEAD] = dk[h]
            dqkv_ref[:, 2 * HW + h * HEAD:2 * HW + (h + 1) * HEAD] = dv[h]
            dzs_ref[:, h * HEAD:(h + 1) * HEAD] = dz[h]
        dzs_ref[:, HW:HW + LANES] = dsmall
        dalog_ref[...] += dalog
        ddtb_ref[...] += ddtb
        dgain_ref[...] += dgain

    row = pl.BlockSpec((1, LANES), lambda i: (0, 0))
    rev = lambda i: nc - 1 - i
    return pl.pallas_call(
        body, name="gdn_bwd", grid=(nc,),
        in_specs=[pl.BlockSpec((C, 3 * HW), lambda i: (rev(i), 0)), pl.BlockSpec((C, zw), lambda i: (rev(i), 0)),
                  pl.BlockSpec((C, HW), lambda i: (rev(i), 0)),
                  pl.BlockSpec((1, H, HEAD, HEAD), lambda i: (rev(i), 0, 0, 0)), row, row, row],
        out_specs=[pl.BlockSpec((C, 3 * HW), lambda i: (rev(i), 0)), pl.BlockSpec((C, zw), lambda i: (rev(i), 0)),
                   row, row, row],
        out_shape=[jax.ShapeDtypeStruct(qkv.shape, F32), jax.ShapeDtypeStruct(proj_zs.shape, F32)]
        + [jax.ShapeDtypeStruct((1, LANES), F32)] * 3,
        scratch_shapes=[pltpu.VMEM((H, HEAD, HEAD), F32)], compiler_params=_params(),
    )(qkv, proj_zs, do, s_all, alog, dtb, gain)


def _hgrn_refs(proj_ref, lb_ref, HP):
    W = HP * HEAD
    lb0 = tuple(lb_ref[0:1, h * HEAD:(h + 1) * HEAD] for h in range(HP))
    lb1 = tuple(lb_ref[1:2, h * HEAD:(h + 1) * HEAD] for h in range(HP))
    return (_split_heads(proj_ref, 0, HP), _split_heads(proj_ref, W, HP), _split_heads(proj_ref, 2 * W, HP),
            _split_heads(proj_ref, 3 * W, HP), lb0, lb1)


def _hgrn_fwd(proj, lower_bounds, gain, nheads):
    T = proj.shape[0]
    C, HP = HGRN_CHUNK, HGRN_HEADS_PER_STEP
    ng, nc, W = nheads // HP, T // C, HP * HEAD

    def body(proj_ref, lb_ref, gain_ref, o_ref, sall_ref, s_scr):
        @pl.when(pl.program_id(1) == 0)
        def _():
            s_scr[...] = jnp.zeros_like(s_scr)

        sall_ref[0] = s_scr[...]
        qr, fr, vi, z, lb0, lb1 = _hgrn_refs(proj_ref, lb_ref, HP)
        outs, states = _hgrn_step(tuple(s_scr[h] for h in range(HP)), qr, fr, vi, z, lb0, lb1, gain_ref[...])
        for h in range(HP):
            o_ref[:, h * HEAD:(h + 1) * HEAD] = outs[h]
            s_scr[h] = states[h]

    return pl.pallas_call(
        body, name="hgrn_fwd", grid=(ng, nc),
        in_specs=[pl.BlockSpec((C, 4 * W), lambda g, i: (i, g)), pl.BlockSpec((2, W), lambda g, i: (0, g)),
                  pl.BlockSpec((1, LANES), lambda g, i: (0, 0))],
        out_specs=[pl.BlockSpec((C, W), lambda g, i: (i, g)),
                   pl.BlockSpec((1, HP, HEAD, HEAD), lambda g, i: (i, g, 0, 0))],
        out_shape=[jax.ShapeDtypeStruct((T, nheads * HEAD), F32), jax.ShapeDtypeStruct((nc, nheads, HEAD, HEAD), F32)],
        scratch_shapes=[pltpu.VMEM((HP, HEAD, HEAD), F32)], compiler_params=_params(),
    )(proj, lower_bounds, gain)


def _hgrn_bwd(proj, do, s_all, lower_bounds, gain, nheads):
    T = proj.shape[0]
    C, HP = HGRN_CHUNK, HGRN_HEADS_PER_STEP
    ng, nc, W = nheads // HP, T // C, HP * HEAD

    def body(proj_ref, do_ref, sin_ref, lb_ref, gain_ref, dproj_ref, dlb_ref, dgain_ref, ds_scr):
        first = pl.program_id(1) == 0

        @pl.when(first)
        def _():
            ds_scr[...] = jnp.zeros_like(ds_scr)
            dlb_ref[...] = jnp.zeros_like(dlb_ref)

        @pl.when(first & (pl.program_id(0) == 0))
        def _():
            dgain_ref[...] = jnp.zeros_like(dgain_ref)

        qr, fr, vi, z, lb0, lb1 = _hgrn_refs(proj_ref, lb_ref, HP)
        primals = (tuple(sin_ref[0, h] for h in range(HP)), qr, fr, vi, z, lb0, lb1, gain_ref[...])
        _, vjp = jax.vjp(_hgrn_step, *primals)
        dS, dq, df, dv, dz, dlb0, dlb1, dgain = vjp(
            (_split_heads(do_ref, 0, HP), tuple(ds_scr[h] for h in range(HP))))
        for h in range(HP):
            ds_scr[h] = dS[h]
            sl = slice(h * HEAD, (h + 1) * HEAD)
            dproj_ref[:, h * HEAD:(h + 1) * HEAD] = dq[h]
            dproj_ref[:, W + h * HEAD:W + (h + 1) * HEAD] = df[h]
            dproj_ref[:, 2 * W + h * HEAD:2 * W + (h + 1) * HEAD] = dv[h]
            dproj_ref[:, 3 * W + h * HEAD:3 * W + (h + 1) * HEAD] = dz[h]
            dlb_ref[0:1, sl] += dlb0[h]
            dlb_ref[1:2, sl] += dlb1[h]
        dgain_ref[...] += dgain

    rev = lambda i: nc - 1 - i
    return pl.pallas_call(
        body, name="hgrn_bwd", grid=(ng, nc),
        in_specs=[pl.BlockSpec((C, 4 * W), lambda g, i: (rev(i), g)), pl.BlockSpec((C, W), lambda g, i: (rev(i), g)),
                  pl.BlockSpec((1, HP, HEAD, HEAD), lambda g, i: (rev(i), g, 0, 0)),
                  pl.BlockSpec((2, W), lambda g, i: (0, g)), pl.BlockSpec((1, LANES), lambda g, i: (0, 0))],
        out_specs=[pl.BlockSpec((C, 4 * W), lambda g, i: (rev(i), g)), pl.BlockSpec((2, W), lambda g, i: (0, g)),
                   pl.BlockSpec((1, LANES), lambda g, i: (0, 0))],
        out_shape=[jax.ShapeDtypeStruct(proj.shape, F32), jax.ShapeDtypeStruct(lower_bounds.shape, F32),
                   jax.ShapeDtypeStruct((1, LANES), F32)],
        scratch_shapes=[pltpu.VMEM((HP, HEAD, HEAD), F32)], compiler_params=_params(),
    )(proj, do, s_all, lower_bounds, gain)


def _post_specs(T):
    tr = min(POST_TILE, T)
    tile = lambda w: pl.BlockSpec((tr, w), lambda i: (i, 0))
    full = lambda r, w: pl.BlockSpec((r, w), lambda i: (0, 0))
    return tr, tile, full


def _post_fwd(s, x, p, g, b, wg, wpl, name):
    T, D = x.shape
    P = p.shape[1]
    tr, tile, full = _post_specs(T)

    def body(s_ref, x_ref, p_ref, g_ref, b_ref, wg_ref, wpl_ref, o_ref):
        x1 = _post_norm(s_ref[...], x_ref[...], g_ref[...], b_ref[...])
        o_ref[...] = _post_gate(x1, _dot_raw(x1, wg_ref[...], "nn", False), _dot_raw(p_ref[...], wpl_ref[...], "nn", False))

    return pl.pallas_call(
        body, name=name, grid=(T // tr,),
        in_specs=[tile(D), tile(D), tile(P), full(1, D), full(1, D), full(D, D), full(P, D)],
        out_specs=tile(D), out_shape=jax.ShapeDtypeStruct((T, D), F32), compiler_params=_params(),
    )(s, x, p, g, b, wg, wpl)


def _post_bwd(s, x, p, g, b, wg, wpl, dnext, name, with_loss):
    T, D = x.shape
    P = p.shape[1]
    tr, tile, full = _post_specs(T)

    def body(s_ref, x_ref, p_ref, g_ref, b_ref, wg_ref, wpl_ref, dn_ref,
             ds_ref, dx_ref, dg_ref, db_ref, dwg_ref, dwpl_ref, loss_ref):
        @pl.when(pl.program_id(0) == 0)
        def _():
            for r in (dg_ref, db_ref, dwg_ref, dwpl_ref, loss_ref):
                r[...] = jnp.zeros_like(r)

        x1, vjp_norm = jax.vjp(_post_norm, s_ref[...], x_ref[...], g_ref[...], b_ref[...])
        gate_pre = _dot_raw(x1, wg_ref[...], "nn", False)
        pp = _dot_raw(p_ref[...], wpl_ref[...], "nn", False)
        xn, vjp_gate = jax.vjp(_post_gate, x1, gate_pre, pp)
        if with_loss:
            err = xn - dn_ref[...]
            loss_ref[...] += 0.5 * jnp.sum(jnp.sum(err * err, axis=-1, keepdims=True), axis=0, keepdims=True) / D
            dn = err / D
        else:
            dn = dn_ref[...]
        dx1, dgp, dpp = vjp_gate(dn)
        dwg_ref[...] += _dot_raw(x1, dgp, "tn", False)
        dwpl_ref[...] += _dot_raw(p_ref[...], dpp, "tn", False)
        dx1 = dx1 + _dot_raw(dgp, wg_ref[...], "nt", False)
        ds, dx, dg, db = vjp_norm(dx1)
        ds_ref[...] = ds
        dx_ref[...] = dx
        dg_ref[...] += dg
        db_ref[...] += db

    return pl.pallas_call(
        body, name=name, grid=(T // tr,),
        in_specs=[tile(D), tile(D), tile(P), full(1, D), full(1, D), full(D, D), full(P, D), tile(D)],
        out_specs=[tile(D), tile(D), full(1, D), full(1, D), full(D, D), full(P, D), full(SUBLANES, LANES)],
        out_shape=[jax.ShapeDtypeStruct((T, D), F32)] * 2 + [jax.ShapeDtypeStruct((1, D), F32)] * 2
        + [jax.ShapeDtypeStruct((D, D), F32), jax.ShapeDtypeStruct((P, D), F32),
           jax.ShapeDtypeStruct((SUBLANES, LANES), F32)],
        compiler_params=_params(),
    )(s, x, p, g, b, wg, wpl, dnext)


def _adam_math(w, g, m, v):
    m = ADAM_B1 * m + (1.0 - ADAM_B1) * g
    v = ADAM_B2 * v + (1.0 - ADAM_B2) * (g * g)
    m_hat = m / (1.0 - ADAM_B1 ** ADAM_STEP)
    v_hat = v / (1.0 - ADAM_B2 ** ADAM_STEP)
    return -ADAM_LR * (m_hat / (jnp.sqrt(v_hat) + ADAM_EPS) + ADAM_WD * w), m, v


def _rs_add(g8, got, cidx, name):
    _, R, C = g8.shape
    tr = _pick(R, (256, 128, 64, 32, 16, 8))

    def body(c_ref, a_ref, b_ref, o_ref):
        o_ref[...] = a_ref[...] + b_ref[...]

    return pl.pallas_call(
        body, name=name,
        grid_spec=pltpu.PrefetchScalarGridSpec(
            num_scalar_prefetch=1, grid=(4, R // tr),
            in_specs=[pl.BlockSpec((1, tr, C), lambda q, i, c: (2 * q + c[0], i, 0)),
                      pl.BlockSpec((1, tr, C), lambda q, i, c: (q, i, 0))],
            out_specs=pl.BlockSpec((1, tr, C), lambda q, i, c: (q, i, 0))),
        out_shape=jax.ShapeDtypeStruct((4,) + g8.shape[1:], F32), compiler_params=_params(),
    )(cidx, g8, got)


def _adam_sharded(w, m, v, mine, got, qidx, name):
    R, C = w.shape
    tr = _pick(R, (256, 128, 64, 32, 16, 8))

    def body(q_ref, w_ref, m_ref, v_ref, p_ref, r0, r1, r2, g_ref, d_ref, mo_ref, vo_ref):
        g = ((p_ref[0] + r0[0]) + r1[0]) + r2[0]
        d, mn, vn = _adam_math(w_ref[...], g, m_ref[...], v_ref[...])
        g_ref[...] = g
        d_ref[...] = d
        mo_ref[...] = mn
        vo_ref[...] = vn

    t2 = pl.BlockSpec((tr, C), lambda i, q: (i, 0))
    slot = lambda k: pl.BlockSpec((1, tr, C), lambda i, q: (k, i, 0))
    return pl.pallas_call(
        body, name=name,
        grid_spec=pltpu.PrefetchScalarGridSpec(
            num_scalar_prefetch=1, grid=(R // tr,),
            in_specs=[t2, t2, t2, pl.BlockSpec((1, tr, C), lambda i, q: (q[0], i, 0)), slot(0), slot(1), slot(2)],
            out_specs=[t2, t2, t2, t2]),
        out_shape=[jax.ShapeDtypeStruct((R, C), F32)] * 4, compiler_params=_params(),
    )(qidx, w, m, v, mine, got, got, got)


def _adam_replicated(w, m, v, g8):
    def body(w_ref, m_ref, v_ref, g_ref, go_ref, d_ref, mo_ref, vo_ref):
        g = g_ref[0]
        for k in range(1, 8):
            g = g + g_ref[k]
        d, mn, vn = _adam_math(w_ref[...], g, m_ref[...], v_ref[...])
        go_ref[...] = g
        d_ref[...] = d
        mo_ref[...] = mn
        vo_ref[...] = vn

    return pl.pallas_call(
        body, name="adam_replicated", out_shape=[jax.ShapeDtypeStruct(w.shape, F32)] * 4, compiler_params=_params(),
    )(w, m, v, g8)


def _place():
    return lax.axis_index("x"), lax.axis_index("y"), lax.axis_index("c")


def _all_gather(shard, name):
    def body(x_ref, out_ref, send_sems, recv_sems, local_sem):
        x, y, c = _place()
        me, sibling = (x, y, c), (x, y, 1 - c)
        chips = [(1 - x, y), (x, 1 - y), (1 - x, 1 - y)]

        def slab(px, py, pc):
            return out_ref.at[4 * px + 2 * py + pc]

        def copy(k, block, to, src=None):
            return pltpu.make_async_remote_copy(
                src_ref=slab(*block) if src is None else src, dst_ref=slab(*block),
                send_sem=send_sems.at[k], recv_sem=recv_sems.at[k], device_id=to, device_id_type=MESH)

        mine = pltpu.make_async_copy(x_ref, slab(*me), local_sem)
        mine.start()
        first = [copy(0, me, sibling, src=x_ref)]
        first += [copy(1 + j, me, (*chip, c), src=x_ref) for j, chip in enumerate(chips)]
        for cp in first:
            cp.start()
        passed = [copy(4 + j, (*chip, c), sibling) for j, chip in enumerate(chips)]
        for j, chip in enumerate(chips):
            copy(1 + j, (*chip, c), me).wait_recv()
            passed[j].start()
        copy(0, sibling, me).wait_recv()
        for j, chip in enumerate(chips):
            copy(4 + j, (*chip, 1 - c), me).wait_recv()
        for cp in first + passed:
            cp.wait_send()
        mine.wait()

    return pl.pallas_call(
        body, name=name, out_shape=jax.ShapeDtypeStruct((8,) + shard.shape, shard.dtype),
        in_specs=[pl.BlockSpec(memory_space=pl.ANY)], out_specs=pl.BlockSpec(memory_space=pl.ANY),
        scratch_shapes=[pltpu.SemaphoreType.DMA((7,)), pltpu.SemaphoreType.DMA((7,)), pltpu.SemaphoreType.DMA],
    )(shard)


def _rs_to_sibling(g8, name):
    def body(g_ref, out_ref, send_sems, recv_sems):
        x, y, c = _place()
        copies = [pltpu.make_async_remote_copy(
            src_ref=g_ref.at[2 * q + (1 - c)], dst_ref=out_ref.at[q], send_sem=send_sems.at[q],
            recv_sem=recv_sems.at[q], device_id=(x, y, 1 - c), device_id_type=MESH) for q in range(4)]
        for cp in copies:
            cp.start()
        for cp in copies:
            cp.wait()

    return pl.pallas_call(
        body, name=name, out_shape=jax.ShapeDtypeStruct((4,) + g8.shape[1:], g8.dtype),
        in_specs=[pl.BlockSpec(memory_space=pl.ANY)], out_specs=pl.BlockSpec(memory_space=pl.ANY),
        scratch_shapes=[pltpu.SemaphoreType.DMA((4,)), pltpu.SemaphoreType.DMA((4,))],
    )(g8)


def _rs_to_chips(p4, name):
    def body(p_ref, out_ref, send_sems, recv_sems):
        x, y, c = _place()
        chips = [(1 - x, y), (x, 1 - y), (1 - x, 1 - y)]
        copies = [pltpu.make_async_remote_copy(
            src_ref=p_ref.at[2 * px + py], dst_ref=out_ref.at[j], send_sem=send_sems.at[j],
            recv_sem=recv_sems.at[j], device_id=(px, py, c), device_id_type=MESH) for j, (px, py) in enumerate(chips)]
        for cp in copies:
            cp.start()
        for cp in copies:
            cp.wait()

    return pl.pallas_call(
        body, name=name, out_shape=jax.ShapeDtypeStruct((3,) + p4.shape[1:], p4.dtype),
        in_specs=[pl.BlockSpec(memory_space=pl.ANY)], out_specs=pl.BlockSpec(memory_space=pl.ANY),
        scratch_shapes=[pltpu.SemaphoreType.DMA((3,)), pltpu.SemaphoreType.DMA((3,))],
    )(p4)


def _reduce_scatter_adam(g8, w, m, v, cidx, qidx, tag):
    got = _rs_to_sibling(g8, "rs_sibling_" + tag)
    chip_sums = _rs_add(g8, got, cidx, "rs_add_" + tag)
    got2 = _rs_to_chips(chip_sums, "rs_chips_" + tag)
    return _adam_sharded(w, m, v, chip_sums, got2, qidx, "adam_" + tag)


def _local_grads(x, p0, p1, target, w_zs, w_a, w_qkv, wino, woute_a, woute_b, wouto, wg, wpl, conv_a, conv_b,
                 a_log, dt_bias, gdn_gain, lower_bounds, hgrn_gain, ln_g, ln_b):
    H = a_log.shape[1]
    nheads_o = wouto.shape[0] // HEAD
    pad_small = ((0, 0), (H, LANES - 2 * H))
    alog_row = jnp.pad(a_log, pad_small)
    dtb_row = jnp.pad(dt_bias, pad_small)

    proj_zs = _matmul(x, w_zs, "nn", "proj_even_zs")
    proj_a = _matmul(x, w_a, "nn", "proj_even_a")
    proj_qkv = _matmul(x, w_qkv, "nn", "proj_even_qkv")
    y_a = _mixer_a_fwd(proj_a, conv_a)
    qkv = _conv_b_fwd(proj_qkv, conv_b)
    o2, s_gdn = _gdn_fwd(qkv, proj_zs, alog_row, dtb_row, gdn_gain, H)
    s_e = _matmul(o2, woute_b, "nn", "out_even_b", add=_matmul(y_a, woute_a, "nn", "out_even_a"))
    x2 = _post_fwd(s_e, x, p0, ln_g[0:1], ln_b[0:1], wg[0], wpl[0], "post_even_fwd")
    proj_o = _matmul(x2, wino, "nn", "proj_odd")
    o4, s_hgrn = _hgrn_fwd(proj_o, lower_bounds, hgrn_gain, nheads_o)
    s_o = _matmul(o4, wouto, "nn", "out_odd")
    ds_o, dx2, dlng1, dlnb1, dwg1, dwpl1, loss = _post_bwd(
        s_o, x2, p1, ln_g[1:2], ln_b[1:2], wg[1], wpl[1], target, "post_odd_loss_bwd", True)
    do4 = _matmul(ds_o, wouto, "nt", "d_out_odd_act")
    dwouto = _matmul(o4, ds_o, "tn", "d_out_odd_w")
    dproj_o, dlb, dhgain = _hgrn_bwd(proj_o, do4, s_hgrn, lower_bounds, hgrn_gain, nheads_o)
    dx2 = _matmul(dproj_o, wino, "nt", "d_proj_odd_act", add=dx2)
    dwino = _matmul(x2, dproj_o, "tn", "d_proj_odd_w")
    ds_e, dx, dlng0, dlnb0, dwg0, dwpl0, _ = _post_bwd(
        s_e, x, p0, ln_g[0:1], ln_b[0:1], wg[0], wpl[0], dx2, "post_even_bwd", False)
    dy_a = _matmul(ds_e, woute_a, "nt", "d_out_even_a_act")
    do2 = _matmul(ds_e, woute_b, "nt", "d_out_even_b_act")
    dwoute_a = _matmul(y_a, ds_e, "tn", "d_out_even_a_w")
    dwoute_b = _matmul(o2, ds_e, "tn", "d_out_even_b_w")
    dqkv, dproj_zs, dalog, ddtb, dggain = _gdn_bwd(qkv, proj_zs, do2, s_gdn, alog_row, dtb_row, gdn_gain, H)
    dproj_qkv, dconv_b = _conv_b_bwd(proj_qkv, dqkv, conv_b)
    dproj_a, dconv_a = _mixer_a_bwd(proj_a, dy_a, conv_a)
    dx = _matmul(dproj_zs, w_zs, "nt", "d_proj_even_zs_act", add=dx)
    dx = _matmul(dproj_a, w_a, "nt", "d_proj_even_a_act", add=dx)
    dx = _matmul(dproj_qkv, w_qkv, "nt", "d_proj_even_qkv_act", add=dx)
    dw_zs = _matmul(x, dproj_zs, "tn", "d_proj_even_zs_w")
    dw_a = _matmul(x, dproj_a, "tn", "d_proj_even_a_w")
    dw_qkv = _matmul(x, dproj_qkv, "tn", "d_proj_even_qkv_w")
    return dict(
        loss=loss[0, 0], grad_x=dx, w_zs=dw_zs, w_a=dw_a, w_qkv=dw_qkv, wino=dwino,
        woute=jnp.concatenate([dwoute_a, dwoute_b], axis=0), wouto=dwouto,
        wg=jnp.stack([dwg0, dwg1]), wpl=jnp.stack([dwpl0, dwpl1]),
        conv_a=dconv_a[:conv_a.shape[0]], conv_b=dconv_b[:conv_b.shape[0]],
        a_log=dalog[:, H:2 * H], dt_bias=ddtb[:, H:2 * H], gdn_gain=dggain, lower_bounds=dlb, hgrn_gain=dhgain,
        ln_g=jnp.concatenate([dlng0, dlng1], axis=0), ln_b=jnp.concatenate([dlnb0, dlnb1], axis=0))


def _pad_rows(a, rows):
    return jnp.pad(a, ((0, rows - a.shape[0]), (0, 0)))


def _pack_small(a_log, dt_bias, gdn_gain, lower_bounds, hgrn_gain, ln_g, ln_b):
    lane_pad = lambda a: _pad_rows(jnp.pad(a, ((0, 0), (0, LANES - a.shape[1]))), SUBLANES)
    parts = [lane_pad(a_log), lane_pad(dt_bias), lane_pad(gdn_gain), lower_bounds.reshape(-1, LANES),
             lane_pad(hgrn_gain), ln_g.reshape(-1, LANES), ln_b.reshape(-1, LANES)]
    packed = jnp.concatenate(parts, axis=0)
    assert packed.shape[0] == SMALL_ROWS, packed.shape
    return packed


def _unpack_small(packed, shapes):
    out, r = [], 0
    for shp in shapes:
        n = shp[0] * shp[1]
        if n < LANES * SUBLANES and shp[1] <= LANES:
            out.append(packed[r:r + shp[0], :shp[1]])
            r += SUBLANES
        else:
            rows = n // LANES
            out.append(packed[r:r + rows].reshape(shp))
            r += rows
    return out


def _split_in_even(w_full, AW, HW, H):
    D = w_full.shape[0]
    n_a = 4 * AW
    n_main = n_a + 3 * HW
    w_zs = jnp.concatenate([w_full[:, n_main:n_main + HW], w_full[:, n_main + HW:],
                            jnp.zeros((D, LANES - 2 * H), w_full.dtype)], axis=1)
    w_a = w_full[:, :n_a].reshape(D, 4, AW // LANES, LANES).transpose(0, 2, 1, 3).reshape(D, n_a)
    return w_zs, w_a, w_full[:, n_a:n_main]


def _join_in_even(d_zs, d_a, d_qkv, AW, HW, H):
    D = d_a.shape[0]
    a_nat = d_a.reshape(D, AW // LANES, 4, LANES).transpose(0, 2, 1, 3).reshape(D, 4 * AW)
    return jnp.concatenate([a_nat, d_qkv, d_zs[:, :HW], d_zs[:, HW:HW + 2 * H]], axis=1)


def kernel(x, p, w_in_even, conv_a_w, conv_b_w, a_log, dt_bias, gdn_norm_g, w_out_even, w_in_odd, lower_bounds, hgrn_norm_g, w_out_odd, ln_g, ln_b, w_pl, w_pl_gate, loss_target, m_w_in_even, m_conv_a_w, m_conv_b_w, m_a_log, m_dt_bias, m_gdn_norm_g, m_w_out_even, m_w_in_odd, m_lower_bounds, m_hgrn_norm_g, m_w_out_odd, m_ln_g, m_ln_b, m_w_pl, m_w_pl_gate, v_w_in_even, v_conv_a_w, v_conv_b_w, v_a_log, v_dt_bias, v_gdn_norm_g, v_w_out_even, v_w_in_odd, v_lower_bounds, v_hgrn_norm_g, v_w_out_odd, v_ln_g, v_ln_b, v_w_pl, v_w_pl_gate):
    xi, yi, ci = _place()
    cidx = jnp.reshape(ci, (1,)).astype(jnp.int32)
    qidx = jnp.reshape(2 * xi + yi, (1,)).astype(jnp.int32)
    D = x.shape[2]
    H = a_log.shape[1]
    HW = H * HEAD
    AW = conv_a_w.shape[2] * 8
    OW = w_out_odd.shape[1] * 8
    PD = w_pl.shape[1]
    ka, kb = conv_a_w.shape[1], conv_b_w.shape[1]
    ca, cb = conv_a_w.shape[2], conv_b_w.shape[2]
    gw = HGRN_HEADS_PER_STEP * HEAD
    ngrp = OW // gw

    g_ine = _all_gather(w_in_even[0].astype(BF16), "ag_w_in_even")
    w_zs, w_a, w_qkv = _split_in_even(jnp.transpose(g_ine, (1, 0, 2)).reshape(D, -1), AW, HW, H)
    g_ino = _all_gather(w_in_odd[0].astype(BF16), "ag_w_in_odd")
    wino = jnp.transpose(g_ino, (1, 0, 2)).reshape(D, 4, ngrp, gw).transpose(0, 2, 1, 3).reshape(D, 4 * OW)
    woute = _all_gather(w_out_even[0].astype(BF16), "ag_w_out_even").reshape(-1, D)
    wouto = _all_gather(w_out_odd[0].astype(BF16), "ag_w_out_odd").reshape(-1, D)
    g_gate = _all_gather(w_pl_gate.astype(BF16).reshape(-1, D), "ag_w_pl_gate")
    wg = g_gate.reshape(8, DEPTH, D // 8, D).transpose(1, 0, 2, 3).reshape(DEPTH, D, D)
    g_pl = _all_gather(w_pl.astype(BF16).reshape(DEPTH * PD, -1), "ag_w_pl")
    wpl = g_pl.reshape(8, DEPTH, PD, D // 8).transpose(1, 2, 0, 3).reshape(DEPTH, PD, D)
    taps = jnp.concatenate([_pad_rows(conv_a_w[0], SUBLANES), _pad_rows(conv_b_w[0], SUBLANES)], axis=1)
    g_taps = _all_gather(taps, "ag_conv")
    conv_a = jnp.transpose(g_taps[:, :ka, :ca], (1, 0, 2)).reshape(ka, 8 * ca)
    conv_b = jnp.transpose(g_taps[:, :kb, ca:], (1, 0, 2)).reshape(kb, 8 * cb)

    gr = _local_grads(x[0], p[0, 0], p[1, 0], loss_target[0], w_zs, w_a, w_qkv, wino, woute[:AW], woute[AW:], wouto,
                      wg, wpl, conv_a, conv_b, a_log, dt_bias, gdn_norm_g, lower_bounds, hgrn_norm_g, ln_g, ln_b)
    loss = lax.psum(gr["loss"], AXES)

    dw_nat = _join_in_even(gr["w_zs"], gr["w_a"], gr["w_qkv"], AW, HW, H)
    sh = w_in_even.shape[2]
    o_ine = _reduce_scatter_adam(dw_nat.reshape(D, 8, sh).transpose(1, 0, 2), w_in_even[0], m_w_in_even[0],
                                 v_w_in_even[0], cidx, qidx, "w_in_even")
    dwo = gr["wino"].reshape(D, ngrp, 4, gw).transpose(0, 2, 1, 3).reshape(D, 8, 4 * OW // 8).transpose(1, 0, 2)
    o_ino = _reduce_scatter_adam(dwo, w_in_odd[0], m_w_in_odd[0], v_w_in_odd[0], cidx, qidx, "w_in_odd")
    o_oute = _reduce_scatter_adam(gr["woute"].reshape(8, -1, D), w_out_even[0], m_w_out_even[0], v_w_out_even[0],
                                  cidx, qidx, "w_out_even")
    o_outo = _reduce_scatter_adam(gr["wouto"].reshape(8, -1, D), w_out_odd[0], m_w_out_odd[0], v_w_out_odd[0],
                                  cidx, qidx, "w_out_odd")
    dgate = gr["wg"].reshape(DEPTH, 8, D // 8, D).transpose(1, 0, 2, 3).reshape(8, DEPTH * D // 8, D)
    flat_gate = lambda a: a.reshape(DEPTH * D // 8, D)
    o_gate = _reduce_scatter_adam(dgate, flat_gate(w_pl_gate), flat_gate(m_w_pl_gate), flat_gate(v_w_pl_gate),
                                  cidx, qidx, "w_pl_gate")
    dpl = gr["wpl"].reshape(DEPTH, PD, 8, D // 8).transpose(2, 0, 1, 3).reshape(8, DEPTH * PD, D // 8)
    flat_pl = lambda a: a.reshape(DEPTH * PD, D // 8)
    o_pl = _reduce_scatter_adam(dpl, flat_pl(w_pl), flat_pl(m_w_pl), flat_pl(v_w_pl), cidx, qidx, "w_pl")
    dtaps = jnp.concatenate([_pad_rows(gr["conv_a"], SUBLANES).reshape(SUBLANES, 8, ca).transpose(1, 0, 2),
                             _pad_rows(gr["conv_b"], SUBLANES).reshape(SUBLANES, 8, cb).transpose(1, 0, 2)], axis=2)
    pack_taps = lambda a, b: jnp.concatenate([_pad_rows(a[0], SUBLANES), _pad_rows(b[0], SUBLANES)], axis=1)
    o_taps = _reduce_scatter_adam(dtaps, taps, pack_taps(m_conv_a_w, m_conv_b_w), pack_taps(v_conv_a_w, v_conv_b_w),
                                  cidx, qidx, "conv")

    small_g = _pack_small(gr["a_log"], gr["dt_bias"], gr["gdn_gain"], gr["lower_bounds"], gr["hgrn_gain"],
                          gr["ln_g"], gr["ln_b"])
    o_small = _adam_replicated(
        _pack_small(a_log, dt_bias, gdn_norm_g, lower_bounds, hgrn_norm_g, ln_g, ln_b),
        _pack_small(m_a_log, m_dt_bias, m_gdn_norm_g, m_lower_bounds, m_hgrn_norm_g, m_ln_g, m_ln_b),
        _pack_small(v_a_log, v_dt_bias, v_gdn_norm_g, v_lower_bounds, v_hgrn_norm_g, v_ln_g, v_ln_b),
        _all_gather(small_g, "ag_small_grads"))
    small_shapes = [a_log.shape, dt_bias.shape, gdn_norm_g.shape, lower_bounds.shape, hgrn_norm_g.shape,
                    ln_g.shape, ln_b.shape]

    def leaves(kind):
        s_alog, s_dt, s_gg, s_lb, s_hg, s_lng, s_lnb = _unpack_small(o_small[kind], small_shapes)
        t = o_taps[kind]
        return [o_ine[kind][None], t[None, :ka, :ca], t[None, :kb, ca:], s_alog, s_dt, s_gg, o_oute[kind][None],
                o_ino[kind][None], s_lb, s_hg, o_outo[kind][None], s_lng, s_lnb,
                o_pl[kind].reshape(w_pl.shape), o_gate[kind].reshape(w_pl_gate.shape)]

    return (loss, gr["grad_x"][None], *leaves(0), *leaves(1), *leaves(2), *leaves(3))
```

```python
import functools

import jax
import jax.numpy as jnp
from jax import lax
from jax.experimental import pallas as pl
from jax.experimental.pallas import tpu as pltpu

F32 = jnp.float32
BF16 = jnp.bfloat16
MESH = pl.DeviceIdType.MESH
AXES = ("x", "y", "c")

LANES = 128
SUBLANES = 8
HEAD = 128
GDN_CHUNK = 64
HGRN_CHUNK = 32
HGRN_SUB = 16
HGRN_HEADS_PER_STEP = 16
NORM_EPS = 1e-5
DEPTH = 2
ALPHA = (2.0 * DEPTH) ** 0.25
EXP_CLAMP = 80.0
ADAM_LR, ADAM_B1, ADAM_B2, ADAM_EPS, ADAM_WD, ADAM_STEP = 0.001, 0.9, 0.999, 1e-08, 0.01, 10
VMEM_LIMIT = 56 * 1024 * 1024
ROW_TILE = 512
MIXER_LANES = 256
CONV_LANES = 512
POST_TILE = 256
SMALL_ROWS = 96

_NOBATCH, _BATCH0 = ((), ()), ((0,), (0,))
_DIMS = {"nn": (((1,), (0,)), _NOBATCH), "nt": (((1,), (1,)), _NOBATCH), "tn": (((0,), (0,)), _NOBATCH),
         "bnn": (((2,), (1,)), _BATCH0), "bnt": (((2,), (2,)), _BATCH0), "btn": (((1,), (1,)), _BATCH0)}


def _params(**kw):
    return pltpu.CompilerParams(vmem_limit_bytes=VMEM_LIMIT, **kw)


def _dot_raw(a, b, kind, hi):
    if hi:
        return lax.dot_general(a, b, _DIMS[kind], precision=lax.Precision.HIGHEST, preferred_element_type=F32)
    return lax.dot_general(a.astype(BF16), b.astype(BF16), _DIMS[kind], preferred_element_type=F32)


@functools.partial(jax.custom_vjp, nondiff_argnums=(2, 3))
def mdot(a, b, kind, hi):
    return _dot_raw(a, b, kind, hi)


def _mdot_fwd(a, b, kind, hi):
    return _dot_raw(a, b, kind, hi), (a, b)


def _mdot_bwd(kind, hi, res, g):
    a, b = res
    pre, base = kind[:-2], kind[-2:]
    if base == "nn":
        return _dot_raw(g, b, pre + "nt", hi), _dot_raw(a, g, pre + "tn", hi)
    if base == "nt":
        return _dot_raw(g, b, pre + "nn", hi), _dot_raw(g, a, pre + "tn", hi)
    return _dot_raw(b, g, pre + "nt", hi), _dot_raw(a, g, pre + "nn", hi)


mdot.defvjp(_mdot_fwd, _mdot_bwd)


def _rows(x, lo, hi):
    return _take_rows(x, lo, hi, x.shape[-2])


@functools.partial(jax.custom_vjp, nondiff_argnums=(1, 2, 3))
def _take_rows(x, lo, hi, n):
    return x[..., lo:hi, :]


def _take_rows_fwd(x, lo, hi, n):
    return x[..., lo:hi, :], None


def _take_rows_bwd(lo, hi, n, _, g):
    parts = []
    if lo > 0:
        parts.append(jnp.zeros(g.shape[:-2] + (lo, g.shape[-1]), g.dtype))
    parts.append(g)
    if n - hi > 0:
        parts.append(jnp.zeros(g.shape[:-2] + (n - hi, g.shape[-1]), g.dtype))
    return (jnp.concatenate(parts, axis=-2) if len(parts) > 1 else g,)


_take_rows.defvjp(_take_rows_fwd, _take_rows_bwd)


def _heads_of(wide, nheads):
    return jnp.stack([wide[:, h * HEAD:(h + 1) * HEAD] for h in range(nheads)], axis=0)


def _wide_of(x):
    return jnp.concatenate([x[h] for h in range(x.shape[0])], axis=1)


@functools.partial(jax.custom_vjp, nondiff_argnums=(1,))
def to_heads(wide, nheads):
    return _heads_of(wide, nheads)


to_heads.defvjp(lambda wide, nheads: (_heads_of(wide, nheads), None), lambda nheads, _, g: (_wide_of(g),))


@jax.custom_vjp
def to_wide(x):
    return _wide_of(x)


to_wide.defvjp(lambda x: (_wide_of(x), None), lambda _, g: (_heads_of(g, g.shape[1] // HEAD),))


def _sigmoid(x):
    return jax.nn.sigmoid(x)


def _silu(x):
    return x * _sigmoid(x)


def _dsilu(x):
    s = _sigmoid(x)
    return s * (1.0 + x * (1.0 - s))


def _log1p(u):
    return jnp.where(u < 1e-4, u * (1.0 - 0.5 * u), jnp.log(1.0 + u))


def _softplus(x):
    return jnp.maximum(x, 0.0) + _log1p(jnp.exp(-jnp.abs(x)))


def _rms_gate(o, gain, z):
    return o * lax.rsqrt(jnp.mean(o * o, axis=-1, keepdims=True) + NORM_EPS) * gain * _silu(z)


def _l2n(x):
    return x * lax.rsqrt(jnp.sum(x * x, axis=-1, keepdims=True) + 1e-6)


def _split_dot_raw(m, x, kind):
    mb = m.astype(BF16)
    hi = x.astype(BF16)
    lo = (x - hi.astype(F32)).astype(BF16)
    dims = _DIMS[kind]
    return (lax.dot_general(mb, hi, dims, preferred_element_type=F32)
            + lax.dot_general(mb, lo, dims, preferred_element_type=F32))


@jax.custom_vjp
def mask_dot(m, x):
    return _split_dot_raw(m, x, "nn")


def _mask_dot_fwd(m, x):
    return _split_dot_raw(m, x, "nn"), m


def _mask_dot_bwd(m, g):
    return jnp.zeros_like(m), _split_dot_raw(m, g, "tn")


mask_dot.defvjp(_mask_dot_fwd, _mask_dot_bwd)


def _unit_lower_inverse_minus_eye(low, n):
    rest = -low
    power = low
    span = 2
    while span < n:
        power = mdot(power, power, "bnn", False)
        rest = rest + power + mdot(rest, power, "bnn", False)
        span *= 2
    return rest


def _gdn_step(S, q, k, v, z, small, alog, dtb, gain):
    H = S.shape[0]
    C = GDN_CHUNK
    row = lax.broadcasted_iota(jnp.int32, (C, C), 0)
    col = lax.broadcasted_iota(jnp.int32, (C, C), 1)
    tril, strict, eye = (row >= col)[None], (row > col)[None], (row == col)[None]
    head = lax.broadcasted_iota(jnp.int32, (H, 1, LANES), 0)
    lane = lax.broadcasted_iota(jnp.int32, (H, 1, LANES), 2)
    rowc = lax.broadcasted_iota(jnp.int32, (1, C, 1), 1)
    beta_all = _sigmoid(small)
    g_all = -jnp.exp(alog) * _softplus(small + dtb)
    gc_all = mask_dot((row >= col).astype(F32), g_all)
    beta = jnp.sum(jnp.where(lane == head, beta_all[None], 0.0), axis=-1, keepdims=True)
    gc = jnp.sum(jnp.where(lane == head + H, gc_all[None], 0.0), axis=-1, keepdims=True)
    gc_row = jnp.sum(jnp.where(eye, gc, 0.0), axis=1, keepdims=True)
    decay = jnp.where(tril, jnp.exp(jnp.where(tril, gc - gc_row, 0.0)), 0.0)
    g_last = jnp.sum(jnp.where(rowc == C - 1, gc, 0.0), axis=1, keepdims=True)
    qn = _l2n(q) * (HEAD ** -0.5)
    kn = _l2n(k)
    kb = kn * beta
    low = jnp.where(strict, mdot(kb, kn, "bnt", False) * decay, 0.0)
    inv_rest = _unit_lower_inverse_minus_eye(low, C)
    eg = jnp.exp(gc)
    vb, kbe = v * beta, kb * eg
    u = vb + mdot(inv_rest, vb, "bnn", False)
    w = kbe + mdot(inv_rest, kbe, "bnn", False)
    attn = mdot(qn, kn, "bnt", False) * decay
    v_new = u - mdot(w, S, "bnn", False)
    o = mdot(qn * eg, S, "bnn", False) + mdot(attn, v_new, "bnn", False)
    k_dec = kn * jnp.exp(g_last - gc)
    return _rms_gate(o, gain, z), S * jnp.exp(g_last) + mdot(k_dec, v_new, "btn", False)


def _hgrn_step(St, qr, fr, vi, z, lb0, lb1, gain):
    H = St.shape[0]
    C, SB = HGRN_CHUNK, HGRN_SUB
    row = lax.broadcasted_iota(jnp.int32, (C, C), 0)
    col = lax.broadcasted_iota(jnp.int32, (C, C), 1)
    blk_start = row - (row & (SB - 1))
    in_blk_f = ((row >= col) & (col >= blk_start)).astype(F32)
    before_f = (col < blk_start).astype(F32)
    sums_f = jnp.concatenate([in_blk_f, before_f], axis=0)
    m = jnp.maximum(lb0, lb1)
    e0, e1 = jnp.exp(lb0 - m), jnp.exp(lb1 - m)
    lb = e1 / (e0 + e1)
    f = lb + (1.0 - lb) * _sigmoid(fr)
    q = _silu(qr)
    k = 1.0 - f
    logf = jnp.log(f)
    sums = mask_dot(sums_f, to_wide(logf))
    inner, start = to_heads(_rows(sums, 0, C), H), to_heads(_rows(sums, C, 2 * C), H)
    b = start + inner
    b_last = jnp.sum(logf, axis=1, keepdims=True)
    o = mdot(q * jnp.exp(b), St, "bnt", False)
    qt = q * jnp.exp(inner)
    parts = []
    for blk in range(C // SB):
        lo, n = blk * SB, (blk + 1) * SB
        ref = jnp.concatenate([_rows(start, lo, n)] * (blk + 1), axis=1)
        kt = _rows(k, 0, n) * jnp.exp(jnp.minimum(ref - _rows(b, 0, n), EXP_CLAMP))
        att = mdot(_rows(qt, lo, n), kt, "bnt", False)
        t_idx = lax.broadcasted_iota(jnp.int32, (1, SB, n), 1) + lo
        s_idx = lax.broadcasted_iota(jnp.int32, (1, SB, n), 2)
        att = jnp.where(s_idx <= t_idx, att, 0.0)
        parts.append(mdot(att, _rows(vi, 0, n), "bnn", False))
    o = o + jnp.concatenate(parts, axis=1)
    k_dec = k * jnp.exp(b_last - b)
    return _rms_gate(o, gain, z), St * jnp.exp(b_last) + mdot(vi, k_dec, "btn", False)


def _post_norm(s, x, g, b):
    r = ALPHA * x + s
    d = r - jnp.mean(r, axis=-1, keepdims=True)
    var = jnp.mean(d * d, axis=-1, keepdims=True)
    return d * lax.rsqrt(var + NORM_EPS) * g + b


def _post_gate(x1, gate_pre, pp):
    return x1 + pp * _sigmoid(gate_pre)


def _pick(dim, cands):
    for c in cands:
        if dim % c == 0:
            return c
    return dim


def _matmul(a, b, kind, name, add=None):
    if kind == "nn":
        (M, K), N = a.shape, b.shape[1]
    elif kind == "nt":
        (M, K), N = a.shape, b.shape[0]
    else:
        (K, M), N = a.shape, b.shape[1]
    tm = _pick(M, (512, 256, 128))
    tn = _pick(N, (1024, 640, 512, 384, 256, 128))
    tk = _pick(K, (1024, 640, 512, 384, 256, 128))
    nk = K // tk
    a_spec = pl.BlockSpec((tk, tm), lambda i, j, k: (k, i)) if kind == "tn" else pl.BlockSpec((tm, tk), lambda i, j, k: (i, k))
    b_spec = pl.BlockSpec((tn, tk), lambda i, j, k: (j, k)) if kind == "nt" else pl.BlockSpec((tk, tn), lambda i, j, k: (k, j))
    o_spec = pl.BlockSpec((tm, tn), lambda i, j, k: (i, j))
    has_add = add is not None

    def body(a_ref, b_ref, *rest):
        add_ref = rest[0] if has_add else None
        o_ref = rest[1] if has_add else rest[0]
        part = _dot_raw(a_ref[...], b_ref[...], kind, False)
        if nk == 1:
            o_ref[...] = part + add_ref[...] if has_add else part
            return
        acc = rest[-1]
        kk = pl.program_id(2)

        @pl.when(kk == 0)
        def _():
            acc[...] = part

        @pl.when(kk > 0)
        def _():
            acc[...] += part

        @pl.when(kk == nk - 1)
        def _():
            o_ref[...] = acc[...] + add_ref[...] if has_add else acc[...]

    return pl.pallas_call(
        body, name=name, grid=(M // tm, N // tn, nk),
        in_specs=[a_spec, b_spec] + ([o_spec] if has_add else []),
        out_specs=o_spec, out_shape=jax.ShapeDtypeStruct((M, N), F32),
        scratch_shapes=[pltpu.VMEM((tm, tn), F32)] if nk > 1 else [],
        compiler_params=_params(dimension_semantics=("parallel", "parallel", "arbitrary")),
    )(*((a, b, add) if has_add else (a, b)))


def _halo_specs(ts, nt, width, prev=True, main=True, nxt=True):
    per = ts // SUBLANES
    last8 = nt * per - 1
    specs = []
    if prev:
        specs.append(pl.BlockSpec((SUBLANES, width), lambda cb, i: (jnp.maximum(i * per - 1, 0), cb)))
    if main:
        specs.append(pl.BlockSpec((ts, width), lambda cb, i: (i, cb)))
    if nxt:
        specs.append(pl.BlockSpec((SUBLANES, width), lambda cb, i: (jnp.minimum((i + 1) * per, last8), cb)))
    return specs


def _taps(ext, ktaps, lo, size):
    return [ext[lo:lo + size] if j == 0 else pltpu.roll(ext, j, 0)[lo:lo + size] for j in range(ktaps)]


def _ahead(ext, j, size):
    n = ext.shape[0]
    return ext[:size] if j == 0 else pltpu.roll(ext, n - j, 0)[:size]


def _lane_block(ref, k):
    return ref[:, k * MIXER_LANES:(k + 1) * MIXER_LANES]


def _mixer_a_fwd(proj_a, conv_w):
    T = proj_a.shape[0]
    nblk = proj_a.shape[1] // (4 * MIXER_LANES)
    ts = min(ROW_TILE, T)
    nt = T // ts

    def body(pp, pm, w_ref, y_ref):
        i = pl.program_id(1)
        u_prev = jnp.where(i > 0, _lane_block(pp, 0) * _lane_block(pp, 1), 0.0)
        ext = jnp.concatenate([u_prev, _lane_block(pm, 0) * _lane_block(pm, 1)], axis=0)
        t0, t1, t2 = _taps(ext, 3, SUBLANES, ts)
        cv = w_ref[2:3, :] * t0 + w_ref[1:2, :] * t1 + w_ref[0:1, :] * t2
        y_ref[...] = _lane_block(pm, 2) * cv * _silu(_lane_block(pm, 3))

    return pl.pallas_call(
        body, name="mixer_a_fwd", grid=(nblk, nt),
        in_specs=_halo_specs(ts, nt, 4 * MIXER_LANES, nxt=False)
        + [pl.BlockSpec((conv_w.shape[0], MIXER_LANES), lambda cb, i: (0, cb))],
        out_specs=pl.BlockSpec((ts, MIXER_LANES), lambda cb, i: (i, cb)),
        out_shape=jax.ShapeDtypeStruct((T, nblk * MIXER_LANES), F32), compiler_params=_params(),
    )(proj_a, proj_a, conv_w)


def _mixer_a_bwd(proj_a, dy, conv_w):
    T = proj_a.shape[0]
    nblk = proj_a.shape[1] // (4 * MIXER_LANES)
    ts = min(ROW_TILE, T)
    nt = T // ts
    kt = conv_w.shape[0]

    def body(pp, pm, pn, dym, dyn, w_ref, dp_ref, dw_ref):
        i = pl.program_id(1)
        hm, cm, bm, zm = (_lane_block(pm, k) for k in range(4))
        u_prev = jnp.where(i > 0, _lane_block(pp, 0) * _lane_block(pp, 1), 0.0)
        ext = jnp.concatenate([u_prev, hm * cm], axis=0)
        dy_ext = jnp.concatenate([dym[...], jnp.where(i < nt - 1, dyn[...], 0.0)], axis=0)
        b_ext = jnp.concatenate([bm, _lane_block(pn, 2)], axis=0)
        sz_ext = _silu(jnp.concatenate([zm, _lane_block(pn, 3)], axis=0))
        dcv_ext = dy_ext * b_ext * sz_ext
        w = [w_ref[j:j + 1, :] for j in range(kt)]
        du = sum(w[kt - 1 - j] * _ahead(dcv_ext, j, ts) for j in range(kt))
        taps = _taps(ext, kt, SUBLANES, ts)
        cv = sum(w[kt - 1 - j] * taps[j] for j in range(kt))
        for part, d in enumerate((du * cm, du * hm, dym[...] * cv * sz_ext[:ts], dym[...] * bm * cv * _dsilu(zm))):
            dp_ref[:, part * MIXER_LANES:(part + 1) * MIXER_LANES] = d
        dcv = dcv_ext[:ts]

        @pl.when(i == 0)
        def _():
            dw_ref[...] = jnp.zeros_like(dw_ref)

        for j in range(kt):
            dw_ref[j:j + 1, :] += jnp.sum(dcv * taps[kt - 1 - j], axis=0, keepdims=True)

    return pl.pallas_call(
        body, name="mixer_a_bwd", grid=(nblk, nt),
        in_specs=_halo_specs(ts, nt, 4 * MIXER_LANES) + _halo_specs(ts, nt, MIXER_LANES, prev=False)
        + [pl.BlockSpec((kt, MIXER_LANES), lambda cb, i: (0, cb))],
        out_specs=[pl.BlockSpec((ts, 4 * MIXER_LANES), lambda cb, i: (i, cb)),
                   pl.BlockSpec((SUBLANES, MIXER_LANES), lambda cb, i: (0, cb))],
        out_shape=[jax.ShapeDtypeStruct(proj_a.shape, F32),
                   jax.ShapeDtypeStruct((SUBLANES, nblk * MIXER_LANES), F32)],
        compiler_params=_params(),
    )(proj_a, proj_a, proj_a, dy, dy, conv_w)


def _conv_b_fwd(raw, conv_w):
    T = raw.shape[0]
    nblk = raw.shape[1] // CONV_LANES
    ts = min(ROW_TILE, T)
    nt = T // ts
    kt = conv_w.shape[0]

    def body(rp, rm, w_ref, y_ref):
        i = pl.program_id(1)
        ext = jnp.concatenate([jnp.where(i > 0, rp[...], 0.0), rm[...]], axis=0)
        taps = _taps(ext, kt, SUBLANES, ts)
        y_ref[...] = _silu(sum(w_ref[kt - 1 - j:kt - j, :] * taps[j] for j in range(kt)))

    return pl.pallas_call(
        body, name="conv_b_fwd", grid=(nblk, nt),
        in_specs=_halo_specs(ts, nt, CONV_LANES, nxt=False) + [pl.BlockSpec((kt, CONV_LANES), lambda cb, i: (0, cb))],
        out_specs=pl.BlockSpec((ts, CONV_LANES), lambda cb, i: (i, cb)),
        out_shape=jax.ShapeDtypeStruct(raw.shape, F32), compiler_params=_params(),
    )(raw, raw, conv_w)


def _conv_b_bwd(raw, dy, conv_w):
    T = raw.shape[0]
    nblk = raw.shape[1] // CONV_LANES
    ts = min(ROW_TILE, T)
    nt = T // ts
    kt = conv_w.shape[0]

    def body(rp, rm, rn, dym, dyn, w_ref, dr_ref, dw_ref):
        i = pl.program_id(1)
        ext = jnp.concatenate([jnp.where(i > 0, rp[...], 0.0), rm[...], rn[...]], axis=0)
        w = [w_ref[j:j + 1, :] for j in range(kt)]
        taps = _taps(ext, kt, SUBLANES, ts + SUBLANES)
        xc_ext = sum(w[kt - 1 - j] * taps[j] for j in range(kt))
        dy_ext = jnp.concatenate([dym[...], jnp.where(i < nt - 1, dyn[...], 0.0)], axis=0)
        dxc_ext = dy_ext * _dsilu(xc_ext)
        dr_ref[...] = sum(w[kt - 1 - j] * _ahead(dxc_ext, j, ts) for j in range(kt))
        dxc = dxc_ext[:ts]

        @pl.when(i == 0)
        def _():
            dw_ref[...] = jnp.zeros_like(dw_ref)

        for j in range(kt):
            dw_ref[j:j + 1, :] += jnp.sum(dxc * taps[kt - 1 - j][:ts], axis=0, keepdims=True)

    return pl.pallas_call(
        body, name="conv_b_bwd", grid=(nblk, nt),
        in_specs=_halo_specs(ts, nt, CONV_LANES) + _halo_specs(ts, nt, CONV_LANES, prev=False)
        + [pl.BlockSpec((kt, CONV_LANES), lambda cb, i: (0, cb))],
        out_specs=[pl.BlockSpec((ts, CONV_LANES), lambda cb, i: (i, cb)),
                   pl.BlockSpec((SUBLANES, CONV_LANES), lambda cb, i: (0, cb))],
        out_shape=[jax.ShapeDtypeStruct(raw.shape, F32), jax.ShapeDtypeStruct((SUBLANES, nblk * CONV_LANES), F32)],
        compiler_params=_params(),
    )(raw, raw, raw, dy, dy, conv_w)


def _split_heads(ref, base, nheads, rows=slice(None)):
    return jnp.stack([ref[rows, base + h * HEAD: base + (h + 1) * HEAD] for h in range(nheads)], axis=0)


def _store_heads(ref, base, x, rows=slice(None), accumulate=False):
    for h in range(x.shape[0]):
        lanes = slice(base + h * HEAD, base + (h + 1) * HEAD)
        if accumulate:
            ref[rows, lanes] += x[h]
        else:
            ref[rows, lanes] = x[h]


def _gdn_fwd(qkv, proj_zs, alog, dtb, gain, H):
    T = qkv.shape[0]
    C, HW = GDN_CHUNK, H * HEAD
    nc = T // C
    zw = HW + LANES

    def body(qkv_ref, zs_ref, alog_ref, dtb_ref, gain_ref, o_ref, sall_ref, s_scr):
        @pl.when(pl.program_id(0) == 0)
        def _():
            s_scr[...] = jnp.zeros_like(s_scr)

        sall_ref[0] = s_scr[...]
        outs, states = _gdn_step(
            s_scr[...], _split_heads(qkv_ref, 0, H), _split_heads(qkv_ref, HW, H),
            _split_heads(qkv_ref, 2 * HW, H), _split_heads(zs_ref, 0, H), zs_ref[:, HW:HW + LANES],
            alog_ref[...], dtb_ref[...], gain_ref[...])
        _store_heads(o_ref, 0, outs)
        s_scr[...] = states

    row = pl.BlockSpec((1, LANES), lambda i: (0, 0))
    return pl.pallas_call(
        body, name="gdn_fwd", grid=(nc,),
        in_specs=[pl.BlockSpec((C, 3 * HW), lambda i: (i, 0)), pl.BlockSpec((C, zw), lambda i: (i, 0)), row, row, row],
        out_specs=[pl.BlockSpec((C, HW), lambda i: (i, 0)), pl.BlockSpec((1, H, HEAD, HEAD), lambda i: (i, 0, 0, 0))],
        out_shape=[jax.ShapeDtypeStruct((T, HW), F32), jax.ShapeDtypeStruct((nc, H, HEAD, HEAD), F32)],
        scratch_shapes=[pltpu.VMEM((H, HEAD, HEAD), F32)], compiler_params=_params(),
    )(qkv, proj_zs, alog, dtb, gain)


def _gdn_bwd(qkv, proj_zs, do, s_all, alog, dtb, gain, H):
    T = qkv.shape[0]
    C, HW = GDN_CHUNK, H * HEAD
    nc = T // C
    zw = HW + LANES

    def body(qkv_ref, zs_ref, do_ref, sin_ref, alog_ref, dtb_ref, gain_ref,
             dqkv_ref, dzs_ref, dalog_ref, ddtb_ref, dgain_ref, ds_scr):
        @pl.when(pl.program_id(0) == 0)
        def _():
            ds_scr[...] = jnp.zeros_like(ds_scr)
            dalog_ref[...] = jnp.zeros_like(dalog_ref)
            ddtb_ref[...] = jnp.zeros_like(ddtb_ref)
            dgain_ref[...] = jnp.zeros_like(dgain_ref)

        primals = (sin_ref[0], _split_heads(qkv_ref, 0, H),
                   _split_heads(qkv_ref, HW, H), _split_heads(qkv_ref, 2 * HW, H), _split_heads(zs_ref, 0, H),
                   zs_ref[:, HW:HW + LANES], alog_ref[...], dtb_ref[...], gain_ref[...])
        _, vjp = jax.vjp(_gdn_step, *primals)
        dS, dq, dk, dv, dz, dsmall, dalog, ddtb, dgain = vjp((_split_heads(do_ref, 0, H), ds_scr[...]))
        ds_scr[...] = dS
        _store_heads(dqkv_ref, 0, dq)
        _store_heads(dqkv_ref, HW, dk)
        _store_heads(dqkv_ref, 2 * HW, dv)
        _store_heads(dzs_ref, 0, dz)
        dzs_ref[:, HW:HW + LANES] = dsmall
        dalog_ref[...] += dalog
        ddtb_ref[...] += ddtb
        dgain_ref[...] += dgain

    row = pl.BlockSpec((1, LANES), lambda i: (0, 0))
    rev = lambda i: nc - 1 - i
    return pl.pallas_call(
        body, name="gdn_bwd", grid=(nc,),
        in_specs=[pl.BlockSpec((C, 3 * HW), lambda i: (rev(i), 0)), pl.BlockSpec((C, zw), lambda i: (rev(i), 0)),
                  pl.BlockSpec((C, HW), lambda i: (rev(i), 0)),
                  pl.BlockSpec((1, H, HEAD, HEAD), lambda i: (rev(i), 0, 0, 0)), row, row, row],
        out_specs=[pl.BlockSpec((C, 3 * HW), lambda i: (rev(i), 0)), pl.BlockSpec((C, zw), lambda i: (rev(i), 0)),
                   row, row, row],
        out_shape=[jax.ShapeDtypeStruct(qkv.shape, F32), jax.ShapeDtypeStruct(proj_zs.shape, F32)]
        + [jax.ShapeDtypeStruct((1, LANES), F32)] * 3,
        scratch_shapes=[pltpu.VMEM((H, HEAD, HEAD), F32)], compiler_params=_params(),
    )(qkv, proj_zs, do, s_all, alog, dtb, gain)


def _hgrn_refs(proj_ref, lb_ref, HP):
    W = HP * HEAD
    return (_split_heads(proj_ref, 0, HP), _split_heads(proj_ref, W, HP), _split_heads(proj_ref, 2 * W, HP),
            _split_heads(proj_ref, 3 * W, HP), _split_heads(lb_ref, 0, HP, slice(0, 1)),
            _split_heads(lb_ref, 0, HP, slice(1, 2)))


def _hgrn_fwd(proj, lower_bounds, gain, nheads):
    T = proj.shape[0]
    C, HP = HGRN_CHUNK, HGRN_HEADS_PER_STEP
    ng, nc, W = nheads // HP, T // C, HP * HEAD

    def body(proj_ref, lb_ref, gain_ref, o_ref, sall_ref, s_scr):
        @pl.when(pl.program_id(1) == 0)
        def _():
            s_scr[...] = jnp.zeros_like(s_scr)

        sall_ref[0] = s_scr[...]
        qr, fr, vi, z, lb0, lb1 = _hgrn_refs(proj_ref, lb_ref, HP)
        outs, states = _hgrn_step(s_scr[...], qr, fr, vi, z, lb0, lb1, gain_ref[...])
        _store_heads(o_ref, 0, outs)
        s_scr[...] = states

    return pl.pallas_call(
        body, name="hgrn_fwd", grid=(ng, nc),
        in_specs=[pl.BlockSpec((C, 4 * W), lambda g, i: (i, g)), pl.BlockSpec((2, W), lambda g, i: (0, g)),
                  pl.BlockSpec((1, LANES), lambda g, i: (0, 0))],
        out_specs=[pl.BlockSpec((C, W), lambda g, i: (i, g)),
                   pl.BlockSpec((1, HP, HEAD, HEAD), lambda g, i: (i, g, 0, 0))],
        out_shape=[jax.ShapeDtypeStruct((T, nheads * HEAD), F32), jax.ShapeDtypeStruct((nc, nheads, HEAD, HEAD), F32)],
        scratch_shapes=[pltpu.VMEM((HP, HEAD, HEAD), F32)], compiler_params=_params(),
    )(proj, lower_bounds, gain)


def _hgrn_bwd(proj, do, s_all, lower_bounds, gain, nheads):
    T = proj.shape[0]
    C, HP = HGRN_CHUNK, HGRN_HEADS_PER_STEP
    ng, nc, W = nheads // HP, T // C, HP * HEAD

    def body(proj_ref, do_ref, sin_ref, lb_ref, gain_ref, dproj_ref, dlb_ref, dgain_ref, ds_scr):
        first = pl.program_id(1) == 0

        @pl.when(first)
        def _():
            ds_scr[...] = jnp.zeros_like(ds_scr)
            dlb_ref[...] = jnp.zeros_like(dlb_ref)

        @pl.when(first & (pl.program_id(0) == 0))
        def _():
            dgain_ref[...] = jnp.zeros_like(dgain_ref)

        qr, fr, vi, z, lb0, lb1 = _hgrn_refs(proj_ref, lb_ref, HP)
        primals = (sin_ref[0], qr, fr, vi, z, lb0, lb1, gain_ref[...])
        _, vjp = jax.vjp(_hgrn_step, *primals)
        dS, dq, df, dv, dz, dlb0, dlb1, dgain = vjp((_split_heads(do_ref, 0, HP), ds_scr[...]))
        ds_scr[...] = dS
        for part, d in enumerate((dq, df, dv, dz)):
            _store_heads(dproj_ref, part * W, d)
        _store_heads(dlb_ref, 0, dlb0, slice(0, 1), accumulate=True)
        _store_heads(dlb_ref, 0, dlb1, slice(1, 2), accumulate=True)
        dgain_ref[...] += dgain

    rev = lambda i: nc - 1 - i
    return pl.pallas_call(
        body, name="hgrn_bwd", grid=(ng, nc),
        in_specs=[pl.BlockSpec((C, 4 * W), lambda g, i: (rev(i), g)), pl.BlockSpec((C, W), lambda g, i: (rev(i), g)),
                  pl.BlockSpec((1, HP, HEAD, HEAD), lambda g, i: (rev(i), g, 0, 0)),
                  pl.BlockSpec((2, W), lambda g, i: (0, g)), pl.BlockSpec((1, LANES), lambda g, i: (0, 0))],
        out_specs=[pl.BlockSpec((C, 4 * W), lambda g, i: (rev(i), g)), pl.BlockSpec((2, W), lambda g, i: (0, g)),
                   pl.BlockSpec((1, LANES), lambda g, i: (0, 0))],
        out_shape=[jax.ShapeDtypeStruct(proj.shape, F32), jax.ShapeDtypeStruct(lower_bounds.shape, F32),
                   jax.ShapeDtypeStruct((1, LANES), F32)],
        scratch_shapes=[pltpu.VMEM((HP, HEAD, HEAD), F32)], compiler_params=_params(),
    )(proj, do, s_all, lower_bounds, gain)


def _post_specs(T):
    tr = min(POST_TILE, T)
    tile = lambda w: pl.BlockSpec((tr, w), lambda i: (i, 0))
    full = lambda r, w: pl.BlockSpec((r, w), lambda i: (0, 0))
    return tr, tile, full


def _post_fwd(s, x, p, g, b, wg, wpl, name):
    T, D = x.shape
    P = p.shape[1]
    tr, tile, full = _post_specs(T)

    def body(s_ref, x_ref, p_ref, g_ref, b_ref, wg_ref, wpl_ref, o_ref):
        x1 = _post_norm(s_ref[...], x_ref[...], g_ref[...], b_ref[...])
        o_ref[...] = _post_gate(x1, _dot_raw(x1, wg_ref[...], "nn", False), _dot_raw(p_ref[...], wpl_ref[...], "nn", False))

    return pl.pallas_call(
        body, name=name, grid=(T // tr,),
        in_specs=[tile(D), tile(D), tile(P), full(1, D), full(1, D), full(D, D), full(P, D)],
        out_specs=tile(D), out_shape=jax.ShapeDtypeStruct((T, D), F32), compiler_params=_params(),
    )(s, x, p, g, b, wg, wpl)


def _post_bwd(s, x, p, g, b, wg, wpl, dnext, name, with_loss):
    T, D = x.shape
    P = p.shape[1]
    tr, tile, full = _post_specs(T)

    def body(s_ref, x_ref, p_ref, g_ref, b_ref, wg_ref, wpl_ref, dn_ref,
             ds_ref, dx_ref, dg_ref, db_ref, dwg_ref, dwpl_ref, loss_ref):
        @pl.when(pl.program_id(0) == 0)
        def _():
            for r in (dg_ref, db_ref, dwg_ref, dwpl_ref, loss_ref):
                r[...] = jnp.zeros_like(r)

        x1, vjp_norm = jax.vjp(_post_norm, s_ref[...], x_ref[...], g_ref[...], b_ref[...])
        gate_pre = _dot_raw(x1, wg_ref[...], "nn", False)
        pp = _dot_raw(p_ref[...], wpl_ref[...], "nn", False)
        xn, vjp_gate = jax.vjp(_post_gate, x1, gate_pre, pp)
        if with_loss:
            err = xn - dn_ref[...]
            loss_ref[...] += 0.5 * jnp.sum(jnp.sum(err * err, axis=-1, keepdims=True), axis=0, keepdims=True) / D
            dn = err / D
        else:
            dn = dn_ref[...]
        dx1, dgp, dpp = vjp_gate(dn)
        dwg_ref[...] += _dot_raw(x1, dgp, "tn", False)
        dwpl_ref[...] += _dot_raw(p_ref[...], dpp, "tn", False)
        dx1 = dx1 + _dot_raw(dgp, wg_ref[...], "nt", False)
        ds, dx, dg, db = vjp_norm(dx1)
        ds_ref[...] = ds
        dx_ref[...] = dx
        dg_ref[...] += dg
        db_ref[...] += db

    return pl.pallas_call(
        body, name=name, grid=(T // tr,),
        in_specs=[tile(D), tile(D), tile(P), full(1, D), full(1, D), full(D, D), full(P, D), tile(D)],
        out_specs=[tile(D), tile(D), full(1, D), full(1, D), full(D, D), full(P, D), full(SUBLANES, LANES)],
        out_shape=[jax.ShapeDtypeStruct((T, D), F32)] * 2 + [jax.ShapeDtypeStruct((1, D), F32)] * 2
        + [jax.ShapeDtypeStruct((D, D), F32), jax.ShapeDtypeStruct((P, D), F32),
           jax.ShapeDtypeStruct((SUBLANES, LANES), F32)],
        compiler_params=_params(),
    )(s, x, p, g, b, wg, wpl, dnext)


def _adam_math(w, g, m, v):
    m = ADAM_B1 * m + (1.0 - ADAM_B1) * g
    v = ADAM_B2 * v + (1.0 - ADAM_B2) * (g * g)
    m_hat = m / (1.0 - ADAM_B1 ** ADAM_STEP)
    v_hat = v / (1.0 - ADAM_B2 ** ADAM_STEP)
    return -ADAM_LR * (m_hat / (jnp.sqrt(v_hat) + ADAM_EPS) + ADAM_WD * w), m, v


def _rs_add(g8, got, cidx, name):
    _, R, C = g8.shape
    tr = _pick(R, (256, 128, 64, 32, 16, 8))

    def body(c_ref, a_ref, b_ref, o_ref):
        o_ref[...] = a_ref[...] + b_ref[...]

    return pl.pallas_call(
        body, name=name,
        grid_spec=pltpu.PrefetchScalarGridSpec(
            num_scalar_prefetch=1, grid=(4, R // tr),
            in_specs=[pl.BlockSpec((1, tr, C), lambda q, i, c: (2 * q + c[0], i, 0)),
                      pl.BlockSpec((1, tr, C), lambda q, i, c: (q, i, 0))],
            out_specs=pl.BlockSpec((1, tr, C), lambda q, i, c: (q, i, 0))),
        out_shape=jax.ShapeDtypeStruct((4,) + g8.shape[1:], F32), compiler_params=_params(),
    )(cidx, g8, got)


def _adam_sharded(w, m, v, mine, got, qidx, name):
    R, C = w.shape
    tr = _pick(R, (256, 128, 64, 32, 16, 8))

    def body(q_ref, w_ref, m_ref, v_ref, p_ref, r0, r1, r2, g_ref, d_ref, mo_ref, vo_ref):
        g = ((p_ref[0] + r0[0]) + r1[0]) + r2[0]
        d, mn, vn = _adam_math(w_ref[...], g, m_ref[...], v_ref[...])
        g_ref[...] = g
        d_ref[...] = d
        mo_ref[...] = mn
        vo_ref[...] = vn

    t2 = pl.BlockSpec((tr, C), lambda i, q: (i, 0))
    slot = lambda k: pl.BlockSpec((1, tr, C), lambda i, q: (k, i, 0))
    return pl.pallas_call(
        body, name=name,
        grid_spec=pltpu.PrefetchScalarGridSpec(
            num_scalar_prefetch=1, grid=(R // tr,),
            in_specs=[t2, t2, t2, pl.BlockSpec((1, tr, C), lambda i, q: (q[0], i, 0)), slot(0), slot(1), slot(2)],
            out_specs=[t2, t2, t2, t2]),
        out_shape=[jax.ShapeDtypeStruct((R, C), F32)] * 4, compiler_params=_params(),
    )(qidx, w, m, v, mine, got, got, got)


def _adam_replicated(w, m, v, g8):
    def body(w_ref, m_ref, v_ref, g_ref, go_ref, d_ref, mo_ref, vo_ref):
        g = g_ref[0]
        for k in range(1, 8):
            g = g + g_ref[k]
        d, mn, vn = _adam_math(w_ref[...], g, m_ref[...], v_ref[...])
        go_ref[...] = g
        d_ref[...] = d
        mo_ref[...] = mn
        vo_ref[...] = vn

    return pl.pallas_call(
        body, name="adam_replicated", out_shape=[jax.ShapeDtypeStruct(w.shape, F32)] * 4, compiler_params=_params(),
    )(w, m, v, g8)


def _place():
    return lax.axis_index("x"), lax.axis_index("y"), lax.axis_index("c")


def _all_gather(shard, name):
    def body(x_ref, out_ref, send_sems, recv_sems, local_sem):
        x, y, c = _place()
        me, sibling = (x, y, c), (x, y, 1 - c)
        chips = [(1 - x, y), (x, 1 - y), (1 - x, 1 - y)]

        def slab(px, py, pc):
            return out_ref.at[4 * px + 2 * py + pc]

        def copy(k, block, to, src=None):
            return pltpu.make_async_remote_copy(
                src_ref=slab(*block) if src is None else src, dst_ref=slab(*block),
                send_sem=send_sems.at[k], recv_sem=recv_sems.at[k], device_id=to, device_id_type=MESH)

        mine = pltpu.make_async_copy(x_ref, slab(*me), local_sem)
        mine.start()
        first = [copy(0, me, sibling, src=x_ref)]
        first += [copy(1 + j, me, (*chip, c), src=x_ref) for j, chip in enumerate(chips)]
        for cp in first:
            cp.start()
        passed = [copy(4 + j, (*chip, c), sibling) for j, chip in enumerate(chips)]
        for j, chip in enumerate(chips):
            copy(1 + j, (*chip, c), me).wait_recv()
            passed[j].start()
        copy(0, sibling, me).wait_recv()
        for j, chip in enumerate(chips):
            copy(4 + j, (*chip, 1 - c), me).wait_recv()
        for cp in first + passed:
            cp.wait_send()
        mine.wait()

    return pl.pallas_call(
        body, name=name, out_shape=jax.ShapeDtypeStruct((8,) + shard.shape, shard.dtype),
        in_specs=[pl.BlockSpec(memory_space=pl.ANY)], out_specs=pl.BlockSpec(memory_space=pl.ANY),
        scratch_shapes=[pltpu.SemaphoreType.DMA((7,)), pltpu.SemaphoreType.DMA((7,)), pltpu.SemaphoreType.DMA],
    )(shard)


def _rs_to_sibling(g8, name):
    def body(g_ref, out_ref, send_sems, recv_sems):
        x, y, c = _place()
        copies = [pltpu.make_async_remote_copy(
            src_ref=g_ref.at[2 * q + (1 - c)], dst_ref=out_ref.at[q], send_sem=send_sems.at[q],
            recv_sem=recv_sems.at[q], device_id=(x, y, 1 - c), device_id_type=MESH) for q in range(4)]
        for cp in copies:
            cp.start()
        for cp in copies:
            cp.wait()

    return pl.pallas_call(
        body, name=name, out_shape=jax.ShapeDtypeStruct((4,) + g8.shape[1:], g8.dtype),
        in_specs=[pl.BlockSpec(memory_space=pl.ANY)], out_specs=pl.BlockSpec(memory_space=pl.ANY),
        scratch_shapes=[pltpu.SemaphoreType.DMA((4,)), pltpu.SemaphoreType.DMA((4,))],
    )(g8)


def _rs_to_chips(p4, name):
    def body(p_ref, out_ref, send_sems, recv_sems):
        x, y, c = _place()
        chips = [(1 - x, y), (x, 1 - y), (1 - x, 1 - y)]
        copies = [pltpu.make_async_remote_copy(
            src_ref=p_ref.at[2 * px + py], dst_ref=out_ref.at[j], send_sem=send_sems.at[j],
            recv_sem=recv_sems.at[j], device_id=(px, py, c), device_id_type=MESH) for j, (px, py) in enumerate(chips)]
        for cp in copies:
            cp.start()
        for cp in copies:
            cp.wait()

    return pl.pallas_call(
        body, name=name, out_shape=jax.ShapeDtypeStruct((3,) + p4.shape[1:], p4.dtype),
        in_specs=[pl.BlockSpec(memory_space=pl.ANY)], out_specs=pl.BlockSpec(memory_space=pl.ANY),
        scratch_shapes=[pltpu.SemaphoreType.DMA((3,)), pltpu.SemaphoreType.DMA((3,))],
    )(p4)


def _reduce_scatter_adam(g8, w, m, v, cidx, qidx, tag):
    got = _rs_to_sibling(g8, "rs_sibling_" + tag)
    chip_sums = _rs_add(g8, got, cidx, "rs_add_" + tag)
    got2 = _rs_to_chips(chip_sums, "rs_chips_" + tag)
    return _adam_sharded(w, m, v, chip_sums, got2, qidx, "adam_" + tag)


def _local_grads(x, p0, p1, target, w_zs, w_a, w_qkv, wino, woute_a, woute_b, wouto, wg, wpl, conv_a, conv_b,
                 a_log, dt_bias, gdn_gain, lower_bounds, hgrn_gain, ln_g, ln_b):
    H = a_log.shape[1]
    nheads_o = wouto.shape[0] // HEAD
    pad_small = ((0, 0), (H, LANES - 2 * H))
    alog_row = jnp.pad(a_log, pad_small)
    dtb_row = jnp.pad(dt_bias, pad_small)

    proj_zs = _matmul(x, w_zs, "nn", "proj_even_zs")
    proj_a = _matmul(x, w_a, "nn", "proj_even_a")
    proj_qkv = _matmul(x, w_qkv, "nn", "proj_even_qkv")
    y_a = _mixer_a_fwd(proj_a, conv_a)
    qkv = _conv_b_fwd(proj_qkv, conv_b)
    o2, s_gdn = _gdn_fwd(qkv, proj_zs, alog_row, dtb_row, gdn_gain, H)
    s_e = _matmul(o2, woute_b, "nn", "out_even_b", add=_matmul(y_a, woute_a, "nn", "out_even_a"))
    x2 = _post_fwd(s_e, x, p0, ln_g[0:1], ln_b[0:1], wg[0], wpl[0], "post_even_fwd")
    proj_o = _matmul(x2, wino, "nn", "proj_odd")
    o4, s_hgrn = _hgrn_fwd(proj_o, lower_bounds, hgrn_gain, nheads_o)
    s_o = _matmul(o4, wouto, "nn", "out_odd")
    ds_o, dx2, dlng1, dlnb1, dwg1, dwpl1, loss = _post_bwd(
        s_o, x2, p1, ln_g[1:2], ln_b[1:2], wg[1], wpl[1], target, "post_odd_loss_bwd", True)
    do4 = _matmul(ds_o, wouto, "nt", "d_out_odd_act")
    dwouto = _matmul(o4, ds_o, "tn", "d_out_odd_w")
    dproj_o, dlb, dhgain = _hgrn_bwd(proj_o, do4, s_hgrn, lower_bounds, hgrn_gain, nheads_o)
    dx2 = _matmul(dproj_o, wino, "nt", "d_proj_odd_act", add=dx2)
    dwino = _matmul(x2, dproj_o, "tn", "d_proj_odd_w")
    ds_e, dx, dlng0, dlnb0, dwg0, dwpl0, _ = _post_bwd(
        s_e, x, p0, ln_g[0:1], ln_b[0:1], wg[0], wpl[0], dx2, "post_even_bwd", False)
    dy_a = _matmul(ds_e, woute_a, "nt", "d_out_even_a_act")
    do2 = _matmul(ds_e, woute_b, "nt", "d_out_even_b_act")
    dwoute_a = _matmul(y_a, ds_e, "tn", "d_out_even_a_w")
    dwoute_b = _matmul(o2, ds_e, "tn", "d_out_even_b_w")
    dqkv, dproj_zs, dalog, ddtb, dggain = _gdn_bwd(qkv, proj_zs, do2, s_gdn, alog_row, dtb_row, gdn_gain, H)
    dproj_qkv, dconv_b = _conv_b_bwd(proj_qkv, dqkv, conv_b)
    dproj_a, dconv_a = _mixer_a_bwd(proj_a, dy_a, conv_a)
    dx = _matmul(dproj_zs, w_zs, "nt", "d_proj_even_zs_act", add=dx)
    dx = _matmul(dproj_a, w_a, "nt", "d_proj_even_a_act", add=dx)
    dx = _matmul(dproj_qkv, w_qkv, "nt", "d_proj_even_qkv_act", add=dx)
    dw_zs = _matmul(x, dproj_zs, "tn", "d_proj_even_zs_w")
    dw_a = _matmul(x, dproj_a, "tn", "d_proj_even_a_w")
    dw_qkv = _matmul(x, dproj_qkv, "tn", "d_proj_even_qkv_w")
    return dict(
        loss=loss[0, 0], grad_x=dx, w_zs=dw_zs, w_a=dw_a, w_qkv=dw_qkv, wino=dwino,
        woute=jnp.concatenate([dwoute_a, dwoute_b], axis=0), wouto=dwouto,
        wg=jnp.stack([dwg0, dwg1]), wpl=jnp.stack([dwpl0, dwpl1]),
        conv_a=dconv_a[:conv_a.shape[0]], conv_b=dconv_b[:conv_b.shape[0]],
        a_log=dalog[:, H:2 * H], dt_bias=ddtb[:, H:2 * H], gdn_gain=dggain, lower_bounds=dlb, hgrn_gain=dhgain,
        ln_g=jnp.concatenate([dlng0, dlng1], axis=0), ln_b=jnp.concatenate([dlnb0, dlnb1], axis=0))


def _pad_rows(a, rows):
    return jnp.pad(a, ((0, rows - a.shape[0]), (0, 0)))


def _pack_small(a_log, dt_bias, gdn_gain, lower_bounds, hgrn_gain, ln_g, ln_b):
    lane_pad = lambda a: _pad_rows(jnp.pad(a, ((0, 0), (0, LANES - a.shape[1]))), SUBLANES)
    parts = [lane_pad(a_log), lane_pad(dt_bias), lane_pad(gdn_gain), lower_bounds.reshape(-1, LANES),
             lane_pad(hgrn_gain), ln_g.reshape(-1, LANES), ln_b.reshape(-1, LANES)]
    packed = jnp.concatenate(parts, axis=0)
    assert packed.shape[0] == SMALL_ROWS, packed.shape
    return packed


def _unpack_small(packed, shapes):
    out, r = [], 0
    for shp in shapes:
        n = shp[0] * shp[1]
        if n < LANES * SUBLANES and shp[1] <= LANES:
            out.append(packed[r:r + shp[0], :shp[1]])
            r += SUBLANES
        else:
            rows = n // LANES
            out.append(packed[r:r + rows].reshape(shp))
            r += rows
    return out


def _split_in_even(w_full, AW, HW, H):
    D = w_full.shape[0]
    n_a = 4 * AW
    n_main = n_a + 3 * HW
    w_zs = jnp.concatenate([w_full[:, n_main:n_main + HW], w_full[:, n_main + HW:],
                            jnp.zeros((D, LANES - 2 * H), w_full.dtype)], axis=1)
    w_a = w_full[:, :n_a].reshape(D, 4, AW // MIXER_LANES, MIXER_LANES).transpose(0, 2, 1, 3).reshape(D, n_a)
    return w_zs, w_a, w_full[:, n_a:n_main]


def _join_in_even(d_zs, d_a, d_qkv, AW, HW, H):
    D = d_a.shape[0]
    a_nat = d_a.reshape(D, AW // MIXER_LANES, 4, MIXER_LANES).transpose(0, 2, 1, 3).reshape(D, 4 * AW)
    return jnp.concatenate([a_nat, d_qkv, d_zs[:, :HW], d_zs[:, HW:HW + 2 * H]], axis=1)


def kernel(x, p, w_in_even, conv_a_w, conv_b_w, a_log, dt_bias, gdn_norm_g, w_out_even, w_in_odd, lower_bounds, hgrn_norm_g, w_out_odd, ln_g, ln_b, w_pl, w_pl_gate, loss_target, m_w_in_even, m_conv_a_w, m_conv_b_w, m_a_log, m_dt_bias, m_gdn_norm_g, m_w_out_even, m_w_in_odd, m_lower_bounds, m_hgrn_norm_g, m_w_out_odd, m_ln_g, m_ln_b, m_w_pl, m_w_pl_gate, v_w_in_even, v_conv_a_w, v_conv_b_w, v_a_log, v_dt_bias, v_gdn_norm_g, v_w_out_even, v_w_in_odd, v_lower_bounds, v_hgrn_norm_g, v_w_out_odd, v_ln_g, v_ln_b, v_w_pl, v_w_pl_gate):
    xi, yi, ci = _place()
    cidx = jnp.reshape(ci, (1,)).astype(jnp.int32)
    qidx = jnp.reshape(2 * xi + yi, (1,)).astype(jnp.int32)
    D = x.shape[2]
    H = a_log.shape[1]
    HW = H * HEAD
    AW = conv_a_w.shape[2] * 8
    OW = w_out_odd.shape[1] * 8
    PD = w_pl.shape[1]
    ka, kb = conv_a_w.shape[1], conv_b_w.shape[1]
    ca, cb = conv_a_w.shape[2], conv_b_w.shape[2]
    gw = HGRN_HEADS_PER_STEP * HEAD
    ngrp = OW // gw

    g_ine = _all_gather(w_in_even[0].astype(BF16), "ag_w_in_even")
    w_zs, w_a, w_qkv = _split_in_even(jnp.transpose(g_ine, (1, 0, 2)).reshape(D, -1), AW, HW, H)
    g_ino = _all_gather(w_in_odd[0].astype(BF16), "ag_w_in_odd")
    wino = jnp.transpose(g_ino, (1, 0, 2)).reshape(D, 4, ngrp, gw).transpose(0, 2, 1, 3).reshape(D, 4 * OW)
    woute = _all_gather(w_out_even[0].astype(BF16), "ag_w_out_even").reshape(-1, D)
    wouto = _all_gather(w_out_odd[0].astype(BF16), "ag_w_out_odd").reshape(-1, D)
    g_gate = _all_gather(w_pl_gate.astype(BF16).reshape(-1, D), "ag_w_pl_gate")
    wg = g_gate.reshape(8, DEPTH, D // 8, D).transpose(1, 0, 2, 3).reshape(DEPTH, D, D)
    g_pl = _all_gather(w_pl.astype(BF16).reshape(DEPTH * PD, -1), "ag_w_pl")
    wpl = g_pl.reshape(8, DEPTH, PD, D // 8).transpose(1, 2, 0, 3).reshape(DEPTH, PD, D)
    taps = jnp.concatenate([_pad_rows(conv_a_w[0], SUBLANES), _pad_rows(conv_b_w[0], SUBLANES)], axis=1)
    g_taps = _all_gather(taps, "ag_conv")
    conv_a = jnp.transpose(g_taps[:, :ka, :ca], (1, 0, 2)).reshape(ka, 8 * ca)
    conv_b = jnp.transpose(g_taps[:, :kb, ca:], (1, 0, 2)).reshape(kb, 8 * cb)

    gr = _local_grads(x[0], p[0, 0], p[1, 0], loss_target[0], w_zs, w_a, w_qkv, wino, woute[:AW], woute[AW:], wouto,
                      wg, wpl, conv_a, conv_b, a_log, dt_bias, gdn_norm_g, lower_bounds, hgrn_norm_g, ln_g, ln_b)
    loss = lax.psum(gr["loss"], AXES)

    dw_nat = _join_in_even(gr["w_zs"], gr["w_a"], gr["w_qkv"], AW, HW, H)
    sh = w_in_even.shape[2]
    o_ine = _reduce_scatter_adam(dw_nat.reshape(D, 8, sh).transpose(1, 0, 2), w_in_even[0], m_w_in_even[0],
                                 v_w_in_even[0], cidx, qidx, "w_in_even")
    dwo = gr["wino"].reshape(D, ngrp, 4, gw).transpose(0, 2, 1, 3).reshape(D, 8, 4 * OW // 8).transpose(1, 0, 2)
    o_ino = _reduce_scatter_adam(dwo, w_in_odd[0], m_w_in_odd[0], v_w_in_odd[0], cidx, qidx, "w_in_odd")
    o_oute = _reduce_scatter_adam(gr["woute"].reshape(8, -1, D), w_out_even[0], m_w_out_even[0], v_w_out_even[0],
                                  cidx, qidx, "w_out_even")
    o_outo = _reduce_scatter_adam(gr["wouto"].reshape(8, -1, D), w_out_odd[0], m_w_out_odd[0], v_w_out_odd[0],
                                  cidx, qidx, "w_out_odd")
    dgate = gr["wg"].reshape(DEPTH, 8, D // 8, D).transpose(1, 0, 2, 3).reshape(8, DEPTH * D // 8, D)
    flat_gate = lambda a: a.reshape(DEPTH * D // 8, D)
    o_gate = _reduce_scatter_adam(dgate, flat_gate(w_pl_gate), flat_gate(m_w_pl_gate), flat_gate(v_w_pl_gate),
                                  cidx, qidx, "w_pl_gate")
    dpl = gr["wpl"].reshape(DEPTH, PD, 8, D // 8).transpose(2, 0, 1, 3).reshape(8, DEPTH * PD, D // 8)
    flat_pl = lambda a: a.reshape(DEPTH * PD, D // 8)
    o_pl = _reduce_scatter_adam(dpl, flat_pl(w_pl), flat_pl(m_w_pl), flat_pl(v_w_pl), cidx, qidx, "w_pl")
    dtaps = jnp.concatenate([_pad_rows(gr["conv_a"], SUBLANES).reshape(SUBLANES, 8, ca).transpose(1, 0, 2),
                             _pad_rows(gr["conv_b"], SUBLANES).reshape(SUBLANES, 8, cb).transpose(1, 0, 2)], axis=2)
    pack_taps = lambda a, b: jnp.concatenate([_pad_rows(a[0], SUBLANES), _pad_rows(b[0], SUBLANES)], axis=1)
    o_taps = _reduce_scatter_adam(dtaps, taps, pack_taps(m_conv_a_w, m_conv_b_w), pack_taps(v_conv_a_w, v_conv_b_w),
                                  cidx, qidx, "conv")

    small_g = _pack_small(gr["a_log"], gr["dt_bias"], gr["gdn_gain"], gr["lower_bounds"], gr["hgrn_gain"],
                          gr["ln_g"], gr["ln_b"])
    o_small = _adam_replicated(
        _pack_small(a_log, dt_bias, gdn_norm_g, lower_bounds, hgrn_norm_g, ln_g, ln_b),
        _pack_small(m_a_log, m_dt_bias, m_gdn_norm_g, m_lower_bounds, m_hgrn_norm_g, m_ln_g, m_ln_b),
        _pack_small(v_a_log, v_dt_bias, v_gdn_norm_g, v_lower_bounds, v_hgrn_norm_g, v_ln_g, v_ln_b),
        _all_gather(small_g, "ag_small_grads"))
    small_shapes = [a_log.shape, dt_bias.shape, gdn_norm_g.shape, lower_bounds.shape, hgrn_norm_g.shape,
                    ln_g.shape, ln_b.shape]

    def leaves(kind):
        s_alog, s_dt, s_gg, s_lb, s_hg, s_lng, s_lnb = _unpack_small(o_small[kind], small_shapes)
        t = o_taps[kind]
        return [o_ine[kind][None], t[None, :ka, :ca], t[None, :kb, ca:], s_alog, s_dt, s_gg, o_oute[kind][None],
                o_ino[kind][None], s_lb, s_hg, o_outo[kind][None], s_lng, s_lnb,
                o_pl[kind].reshape(w_pl.shape), o_gate[kind].reshape(w_pl_gate.shape)]

    return (loss, gr["grad_x"][None], *leaves(0), *leaves(1), *leaves(2), *leaves(3))
```

```python
import functools

import jax
import jax.numpy as jnp
from jax import lax
from jax.experimental import pallas as pl
from jax.experimental.pallas import tpu as pltpu

F32 = jnp.float32
BF16 = jnp.bfloat16
MESH = pl.DeviceIdType.MESH
AXES = ("x", "y", "c")

LANES = 128
SUBLANES = 8
HEAD = 128
GDN_CHUNK = 64
HGRN_CHUNK = 32
HGRN_SUB = 16
HGRN_HEADS_PER_STEP = 16
NORM_EPS = 1e-5
DEPTH = 2
ALPHA = (2.0 * DEPTH) ** 0.25
EXP_CLAMP = 80.0
ADAM_LR, ADAM_B1, ADAM_B2, ADAM_EPS, ADAM_WD, ADAM_STEP = 0.001, 0.9, 0.999, 1e-08, 0.01, 10
VMEM_LIMIT = 56 * 1024 * 1024
ROW_TILE = 512
MIXER_LANES = 256
CONV_LANES = 512
POST_TILE = 256
SMALL_ROWS = 96

_NOBATCH, _BATCH0 = ((), ()), ((0,), (0,))
_DIMS = {"nn": (((1,), (0,)), _NOBATCH), "nt": (((1,), (1,)), _NOBATCH), "tn": (((0,), (0,)), _NOBATCH),
         "bnn": (((2,), (1,)), _BATCH0), "bnt": (((2,), (2,)), _BATCH0), "btn": (((1,), (1,)), _BATCH0)}


def _params(**kw):
    return pltpu.CompilerParams(vmem_limit_bytes=VMEM_LIMIT, **kw)


def _dot_raw(a, b, kind, hi):
    if hi:
        return lax.dot_general(a, b, _DIMS[kind], precision=lax.Precision.HIGHEST, preferred_element_type=F32)
    return lax.dot_general(a.astype(BF16), b.astype(BF16), _DIMS[kind], preferred_element_type=F32)


@functools.partial(jax.custom_vjp, nondiff_argnums=(2, 3))
def mdot(a, b, kind, hi):
    return _dot_raw(a, b, kind, hi)


def _mdot_fwd(a, b, kind, hi):
    return _dot_raw(a, b, kind, hi), (a, b)


def _mdot_bwd(kind, hi, res, g):
    a, b = res
    pre, base = kind[:-2], kind[-2:]
    if base == "nn":
        return _dot_raw(g, b, pre + "nt", hi), _dot_raw(a, g, pre + "tn", hi)
    if base == "nt":
        return _dot_raw(g, b, pre + "nn", hi), _dot_raw(g, a, pre + "tn", hi)
    return _dot_raw(b, g, pre + "nt", hi), _dot_raw(a, g, pre + "nn", hi)


mdot.defvjp(_mdot_fwd, _mdot_bwd)


def _rows(x, lo, hi):
    return _take_rows(x, lo, hi, x.shape[-2])


@functools.partial(jax.custom_vjp, nondiff_argnums=(1, 2, 3))
def _take_rows(x, lo, hi, n):
    return x[..., lo:hi, :]


def _take_rows_fwd(x, lo, hi, n):
    return x[..., lo:hi, :], None


def _take_rows_bwd(lo, hi, n, _, g):
    parts = []
    if lo > 0:
        parts.append(jnp.zeros(g.shape[:-2] + (lo, g.shape[-1]), g.dtype))
    parts.append(g)
    if n - hi > 0:
        parts.append(jnp.zeros(g.shape[:-2] + (n - hi, g.shape[-1]), g.dtype))
    return (jnp.concatenate(parts, axis=-2) if len(parts) > 1 else g,)


_take_rows.defvjp(_take_rows_fwd, _take_rows_bwd)


def _heads_of(wide, nheads):
    return jnp.stack([wide[:, h * HEAD:(h + 1) * HEAD] for h in range(nheads)], axis=0)


def _wide_of(x):
    return jnp.concatenate([x[h] for h in range(x.shape[0])], axis=1)


@functools.partial(jax.custom_vjp, nondiff_argnums=(1,))
def to_heads(wide, nheads):
    return _heads_of(wide, nheads)


to_heads.defvjp(lambda wide, nheads: (_heads_of(wide, nheads), None), lambda nheads, _, g: (_wide_of(g),))


@jax.custom_vjp
def to_wide(x):
    return _wide_of(x)


to_wide.defvjp(lambda x: (_wide_of(x), None), lambda _, g: (_heads_of(g, g.shape[1] // HEAD),))


def _sigmoid(x):
    return jax.nn.sigmoid(x)


def _silu(x):
    return x * _sigmoid(x)


def _dsilu(x):
    s = _sigmoid(x)
    return s * (1.0 + x * (1.0 - s))


def _log1p(u):
    return jnp.where(u < 1e-4, u * (1.0 - 0.5 * u), jnp.log(1.0 + u))


def _softplus(x):
    return jnp.maximum(x, 0.0) + _log1p(jnp.exp(-jnp.abs(x)))


def _rms_gate(o, gain, z):
    return o * lax.rsqrt(jnp.mean(o * o, axis=-1, keepdims=True) + NORM_EPS) * gain * _silu(z)


def _l2n(x):
    return x * lax.rsqrt(jnp.sum(x * x, axis=-1, keepdims=True) + 1e-6)


def _split_dot_raw(m, x, kind):
    mb = m.astype(BF16)
    hi = x.astype(BF16)
    lo = (x - hi.astype(F32)).astype(BF16)
    dims = _DIMS[kind]
    return (lax.dot_general(mb, hi, dims, preferred_element_type=F32)
            + lax.dot_general(mb, lo, dims, preferred_element_type=F32))


@jax.custom_vjp
def mask_dot(m, x):
    return _split_dot_raw(m, x, "nn")


def _mask_dot_fwd(m, x):
    return _split_dot_raw(m, x, "nn"), m


def _mask_dot_bwd(m, g):
    return jnp.zeros_like(m), _split_dot_raw(m, g, "tn")


mask_dot.defvjp(_mask_dot_fwd, _mask_dot_bwd)


def _unit_lower_inverse_minus_eye(low, n):
    rest = -low
    power = low
    span = 2
    while span < n:
        power = mdot(power, power, "bnn", False)
        rest = rest + power + mdot(rest, power, "bnn", False)
        span *= 2
    return rest


def _gdn_step(S, q, k, v, z, small, alog, dtb, gain):
    H = S.shape[0]
    C = GDN_CHUNK
    row = lax.broadcasted_iota(jnp.int32, (C, C), 0)
    col = lax.broadcasted_iota(jnp.int32, (C, C), 1)
    tril, strict, eye = (row >= col)[None], (row > col)[None], (row == col)[None]
    head = lax.broadcasted_iota(jnp.int32, (H, 1, LANES), 0)
    lane = lax.broadcasted_iota(jnp.int32, (H, 1, LANES), 2)
    rowc = lax.broadcasted_iota(jnp.int32, (1, C, 1), 1)
    beta_all = _sigmoid(small)
    g_all = -jnp.exp(alog) * _softplus(small + dtb)
    gc_all = mask_dot((row >= col).astype(F32), g_all)
    beta = jnp.sum(jnp.where(lane == head, beta_all[None], 0.0), axis=-1, keepdims=True)
    gc = jnp.sum(jnp.where(lane == head + H, gc_all[None], 0.0), axis=-1, keepdims=True)
    gc_row = jnp.sum(jnp.where(eye, gc, 0.0), axis=1, keepdims=True)
    decay = jnp.where(tril, jnp.exp(jnp.where(tril, gc - gc_row, 0.0)), 0.0)
    g_last = jnp.sum(jnp.where(rowc == C - 1, gc, 0.0), axis=1, keepdims=True)
    qn = _l2n(q) * (HEAD ** -0.5)
    kn = _l2n(k)
    kb = kn * beta
    low = jnp.where(strict, mdot(kb, kn, "bnt", False) * decay, 0.0)
    inv_rest = _unit_lower_inverse_minus_eye(low, C)
    eg = jnp.exp(gc)
    vb, kbe = v * beta, kb * eg
    u = vb + mdot(inv_rest, vb, "bnn", False)
    w = kbe + mdot(inv_rest, kbe, "bnn", False)
    attn = mdot(qn, kn, "bnt", False) * decay
    v_new = u - mdot(w, S, "bnn", False)
    o = mdot(qn * eg, S, "bnn", False) + mdot(attn, v_new, "bnn", False)
    k_dec = kn * jnp.exp(g_last - gc)
    return _rms_gate(o, gain, z), S * jnp.exp(g_last) + mdot(k_dec, v_new, "btn", False)


def _hgrn_step(St, qr, fr, vi, z, lb0, lb1, gain):
    H = St.shape[0]
    C, SB = HGRN_CHUNK, HGRN_SUB
    row = lax.broadcasted_iota(jnp.int32, (C, C), 0)
    col = lax.broadcasted_iota(jnp.int32, (C, C), 1)
    blk_start = row - (row & (SB - 1))
    in_blk_f = ((row >= col) & (col >= blk_start)).astype(F32)
    before_f = (col < blk_start).astype(F32)
    sums_f = jnp.concatenate([in_blk_f, before_f], axis=0)
    m = jnp.maximum(lb0, lb1)
    e0, e1 = jnp.exp(lb0 - m), jnp.exp(lb1 - m)
    lb = e1 / (e0 + e1)
    f = lb + (1.0 - lb) * _sigmoid(fr)
    q = _silu(qr)
    k = 1.0 - f
    logf = jnp.log(f)
    sums = mask_dot(sums_f, to_wide(logf))
    inner, start = to_heads(_rows(sums, 0, C), H), to_heads(_rows(sums, C, 2 * C), H)
    b = start + inner
    b_last = jnp.sum(logf, axis=1, keepdims=True)
    o = mdot(q * jnp.exp(b), St, "bnt", False)
    qt = q * jnp.exp(inner)
    parts = []
    for blk in range(C // SB):
        lo, n = blk * SB, (blk + 1) * SB
        ref = jnp.concatenate([_rows(start, lo, n)] * (blk + 1), axis=1)
        kt = _rows(k, 0, n) * jnp.exp(jnp.minimum(ref - _rows(b, 0, n), EXP_CLAMP))
        att = mdot(_rows(qt, lo, n), kt, "bnt", False)
        t_idx = lax.broadcasted_iota(jnp.int32, (1, SB, n), 1) + lo
        s_idx = lax.broadcasted_iota(jnp.int32, (1, SB, n), 2)
        att = jnp.where(s_idx <= t_idx, att, 0.0)
        parts.append(mdot(att, _rows(vi, 0, n), "bnn", False))
    o = o + jnp.concatenate(parts, axis=1)
    k_dec = k * jnp.exp(b_last - b)
    return _rms_gate(o, gain, z), St * jnp.exp(b_last) + mdot(vi, k_dec, "btn", False)


def _post_norm(s, x, g, b):
    r = ALPHA * x + s
    d = r - jnp.mean(r, axis=-1, keepdims=True)
    var = jnp.mean(d * d, axis=-1, keepdims=True)
    return d * lax.rsqrt(var + NORM_EPS) * g + b


def _post_gate(x1, gate_pre, pp):
    return x1 + pp * _sigmoid(gate_pre)


def _pick(dim, cands):
    for c in cands:
        if dim % c == 0:
            return c
    return dim


def _matmul(a, b, kind, name, add=None):
    if kind == "nn":
        (M, K), N = a.shape, b.shape[1]
    elif kind == "nt":
        (M, K), N = a.shape, b.shape[0]
    else:
        (K, M), N = a.shape, b.shape[1]
    tm = _pick(M, (2048, 1024, 512, 256, 128))
    tn = _pick(N, (512, 640, 384, 256, 128))
    tk = _pick(K, (1024, 640, 512, 384, 256, 128))
    nk = K // tk
    a_spec = pl.BlockSpec((tk, tm), lambda i, j, k: (k, i)) if kind == "tn" else pl.BlockSpec((tm, tk), lambda i, j, k: (i, k))
    b_spec = pl.BlockSpec((tn, tk), lambda i, j, k: (j, k)) if kind == "nt" else pl.BlockSpec((tk, tn), lambda i, j, k: (k, j))
    o_spec = pl.BlockSpec((tm, tn), lambda i, j, k: (i, j))
    has_add = add is not None

    def body(a_ref, b_ref, *rest):
        add_ref = rest[0] if has_add else None
        o_ref = rest[1] if has_add else rest[0]
        part = _dot_raw(a_ref[...], b_ref[...], kind, False)
        if nk == 1:
            o_ref[...] = part + add_ref[...] if has_add else part
            return
        acc = rest[-1]
        kk = pl.program_id(2)

        @pl.when(kk == 0)
        def _():
            acc[...] = part

        @pl.when(kk > 0)
        def _():
            acc[...] += part

        @pl.when(kk == nk - 1)
        def _():
            o_ref[...] = acc[...] + add_ref[...] if has_add else acc[...]

    return pl.pallas_call(
        body, name=name, grid=(M // tm, N // tn, nk),
        in_specs=[a_spec, b_spec] + ([o_spec] if has_add else []),
        out_specs=o_spec, out_shape=jax.ShapeDtypeStruct((M, N), F32),
        scratch_shapes=[pltpu.VMEM((tm, tn), F32)] if nk > 1 else [],
        compiler_params=_params(dimension_semantics=("parallel", "parallel", "arbitrary")),
    )(*((a, b, add) if has_add else (a, b)))


def _halo_specs(ts, nt, width, prev=True, main=True, nxt=True):
    per = ts // SUBLANES
    last8 = nt * per - 1
    specs = []
    if prev:
        specs.append(pl.BlockSpec((SUBLANES, width), lambda cb, i: (jnp.maximum(i * per - 1, 0), cb)))
    if main:
        specs.append(pl.BlockSpec((ts, width), lambda cb, i: (i, cb)))
    if nxt:
        specs.append(pl.BlockSpec((SUBLANES, width), lambda cb, i: (jnp.minimum((i + 1) * per, last8), cb)))
    return specs


def _taps(ext, ktaps, lo, size):
    return [ext[lo:lo + size] if j == 0 else pltpu.roll(ext, j, 0)[lo:lo + size] for j in range(ktaps)]


def _ahead(ext, j, size):
    n = ext.shape[0]
    return ext[:size] if j == 0 else pltpu.roll(ext, n - j, 0)[:size]


def _lane_block(ref, k):
    return ref[:, k * MIXER_LANES:(k + 1) * MIXER_LANES]


def _mixer_a_fwd(proj_a, conv_w):
    T = proj_a.shape[0]
    nblk = proj_a.shape[1] // (4 * MIXER_LANES)
    ts = min(ROW_TILE, T)
    nt = T // ts

    def body(pp, pm, w_ref, y_ref):
        i = pl.program_id(1)
        u_prev = jnp.where(i > 0, _lane_block(pp, 0) * _lane_block(pp, 1), 0.0)
        ext = jnp.concatenate([u_prev, _lane_block(pm, 0) * _lane_block(pm, 1)], axis=0)
        t0, t1, t2 = _taps(ext, 3, SUBLANES, ts)
        cv = w_ref[2:3, :] * t0 + w_ref[1:2, :] * t1 + w_ref[0:1, :] * t2
        y_ref[...] = _lane_block(pm, 2) * cv * _silu(_lane_block(pm, 3))

    return pl.pallas_call(
        body, name="mixer_a_fwd", grid=(nblk, nt),
        in_specs=_halo_specs(ts, nt, 4 * MIXER_LANES, nxt=False)
        + [pl.BlockSpec((conv_w.shape[0], MIXER_LANES), lambda cb, i: (0, cb))],
        out_specs=pl.BlockSpec((ts, MIXER_LANES), lambda cb, i: (i, cb)),
        out_shape=jax.ShapeDtypeStruct((T, nblk * MIXER_LANES), F32), compiler_params=_params(),
    )(proj_a, proj_a, conv_w)


def _mixer_a_bwd(proj_a, dy, conv_w):
    T = proj_a.shape[0]
    nblk = proj_a.shape[1] // (4 * MIXER_LANES)
    ts = min(ROW_TILE, T)
    nt = T // ts
    kt = conv_w.shape[0]

    def body(pp, pm, pn, dym, dyn, w_ref, dp_ref, dw_ref):
        i = pl.program_id(1)
        hm, cm, bm, zm = (_lane_block(pm, k) for k in range(4))
        u_prev = jnp.where(i > 0, _lane_block(pp, 0) * _lane_block(pp, 1), 0.0)
        ext = jnp.concatenate([u_prev, hm * cm], axis=0)
        dy_ext = jnp.concatenate([dym[...], jnp.where(i < nt - 1, dyn[...], 0.0)], axis=0)
        b_ext = jnp.concatenate([bm, _lane_block(pn, 2)], axis=0)
        sz_ext = _silu(jnp.concatenate([zm, _lane_block(pn, 3)], axis=0))
        dcv_ext = dy_ext * b_ext * sz_ext
        w = [w_ref[j:j + 1, :] for j in range(kt)]
        du = sum(w[kt - 1 - j] * _ahead(dcv_ext, j, ts) for j in range(kt))
        taps = _taps(ext, kt, SUBLANES, ts)
        cv = sum(w[kt - 1 - j] * taps[j] for j in range(kt))
        for part, d in enumerate((du * cm, du * hm, dym[...] * cv * sz_ext[:ts], dym[...] * bm * cv * _dsilu(zm))):
            dp_ref[:, part * MIXER_LANES:(part + 1) * MIXER_LANES] = d
        dcv = dcv_ext[:ts]

        @pl.when(i == 0)
        def _():
            dw_ref[...] = jnp.zeros_like(dw_ref)

        for j in range(kt):
            dw_ref[j:j + 1, :] += jnp.sum(dcv * taps[kt - 1 - j], axis=0, keepdims=True)

    return pl.pallas_call(
        body, name="mixer_a_bwd", grid=(nblk, nt),
        in_specs=_halo_specs(ts, nt, 4 * MIXER_LANES) + _halo_specs(ts, nt, MIXER_LANES, prev=False)
        + [pl.BlockSpec((kt, MIXER_LANES), lambda cb, i: (0, cb))],
        out_specs=[pl.BlockSpec((ts, 4 * MIXER_LANES), lambda cb, i: (i, cb)),
                   pl.BlockSpec((SUBLANES, MIXER_LANES), lambda cb, i: (0, cb))],
        out_shape=[jax.ShapeDtypeStruct(proj_a.shape, F32),
                   jax.ShapeDtypeStruct((SUBLANES, nblk * MIXER_LANES), F32)],
        compiler_params=_params(),
    )(proj_a, proj_a, proj_a, dy, dy, conv_w)


def _conv_b_fwd(raw, conv_w):
    T = raw.shape[0]
    nblk = raw.shape[1] // CONV_LANES
    ts = min(ROW_TILE, T)
    nt = T // ts
    kt = conv_w.shape[0]

    def body(rp, rm, w_ref, y_ref):
        i = pl.program_id(1)
        ext = jnp.concatenate([jnp.where(i > 0, rp[...], 0.0), rm[...]], axis=0)
        taps = _taps(ext, kt, SUBLANES, ts)
        y_ref[...] = _silu(sum(w_ref[kt - 1 - j:kt - j, :] * taps[j] for j in range(kt)))

    return pl.pallas_call(
        body, name="conv_b_fwd", grid=(nblk, nt),
        in_specs=_halo_specs(ts, nt, CONV_LANES, nxt=False) + [pl.BlockSpec((kt, CONV_LANES), lambda cb, i: (0, cb))],
        out_specs=pl.BlockSpec((ts, CONV_LANES), lambda cb, i: (i, cb)),
        out_shape=jax.ShapeDtypeStruct(raw.shape, F32), compiler_params=_params(),
    )(raw, raw, conv_w)


def _conv_b_bwd(raw, dy, conv_w):
    T = raw.shape[0]
    nblk = raw.shape[1] // CONV_LANES
    ts = min(ROW_TILE, T)
    nt = T // ts
    kt = conv_w.shape[0]

    def body(rp, rm, rn, dym, dyn, w_ref, dr_ref, dw_ref):
        i = pl.program_id(1)
        ext = jnp.concatenate([jnp.where(i > 0, rp[...], 0.0), rm[...], rn[...]], axis=0)
        w = [w_ref[j:j + 1, :] for j in range(kt)]
        taps = _taps(ext, kt, SUBLANES, ts + SUBLANES)
        xc_ext = sum(w[kt - 1 - j] * taps[j] for j in range(kt))
        dy_ext = jnp.concatenate([dym[...], jnp.where(i < nt - 1, dyn[...], 0.0)], axis=0)
        dxc_ext = dy_ext * _dsilu(xc_ext)
        dr_ref[...] = sum(w[kt - 1 - j] * _ahead(dxc_ext, j, ts) for j in range(kt))
        dxc = dxc_ext[:ts]

        @pl.when(i == 0)
        def _():
            dw_ref[...] = jnp.zeros_like(dw_ref)

        for j in range(kt):
            dw_ref[j:j + 1, :] += jnp.sum(dxc * taps[kt - 1 - j][:ts], axis=0, keepdims=True)

    return pl.pallas_call(
        body, name="conv_b_bwd", grid=(nblk, nt),
        in_specs=_halo_specs(ts, nt, CONV_LANES) + _halo_specs(ts, nt, CONV_LANES, prev=False)
        + [pl.BlockSpec((kt, CONV_LANES), lambda cb, i: (0, cb))],
        out_specs=[pl.BlockSpec((ts, CONV_LANES), lambda cb, i: (i, cb)),
                   pl.BlockSpec((SUBLANES, CONV_LANES), lambda cb, i: (0, cb))],
        out_shape=[jax.ShapeDtypeStruct(raw.shape, F32), jax.ShapeDtypeStruct((SUBLANES, nblk * CONV_LANES), F32)],
        compiler_params=_params(),
    )(raw, raw, raw, dy, dy, conv_w)


def _split_heads(ref, base, nheads, rows=slice(None)):
    return jnp.stack([ref[rows, base + h * HEAD: base + (h + 1) * HEAD] for h in range(nheads)], axis=0)


def _store_heads(ref, base, x, rows=slice(None), accumulate=False):
    for h in range(x.shape[0]):
        lanes = slice(base + h * HEAD, base + (h + 1) * HEAD)
        if accumulate:
            ref[rows, lanes] += x[h]
        else:
            ref[rows, lanes] = x[h]


def _gdn_fwd(qkv, proj_zs, alog, dtb, gain, H):
    T = qkv.shape[0]
    C, HW = GDN_CHUNK, H * HEAD
    nc = T // C
    zw = HW + LANES

    def body(qkv_ref, zs_ref, alog_ref, dtb_ref, gain_ref, o_ref, sall_ref, s_scr):
        @pl.when(pl.program_id(0) == 0)
        def _():
            s_scr[...] = jnp.zeros_like(s_scr)

        sall_ref[0] = s_scr[...]
        outs, states = _gdn_step(
            s_scr[...], _split_heads(qkv_ref, 0, H), _split_heads(qkv_ref, HW, H),
            _split_heads(qkv_ref, 2 * HW, H), _split_heads(zs_ref, 0, H), zs_ref[:, HW:HW + LANES],
            alog_ref[...], dtb_ref[...], gain_ref[...])
        _store_heads(o_ref, 0, outs)
        s_scr[...] = states

    row = pl.BlockSpec((1, LANES), lambda i: (0, 0))
    return pl.pallas_call(
        body, name="gdn_fwd", grid=(nc,),
        in_specs=[pl.BlockSpec((C, 3 * HW), lambda i: (i, 0)), pl.BlockSpec((C, zw), lambda i: (i, 0)), row, row, row],
        out_specs=[pl.BlockSpec((C, HW), lambda i: (i, 0)), pl.BlockSpec((1, H, HEAD, HEAD), lambda i: (i, 0, 0, 0))],
        out_shape=[jax.ShapeDtypeStruct((T, HW), F32), jax.ShapeDtypeStruct((nc, H, HEAD, HEAD), F32)],
        scratch_shapes=[pltpu.VMEM((H, HEAD, HEAD), F32)], compiler_params=_params(),
    )(qkv, proj_zs, alog, dtb, gain)


def _gdn_bwd(qkv, proj_zs, do, s_all, alog, dtb, gain, H):
    T = qkv.shape[0]
    C, HW = GDN_CHUNK, H * HEAD
    nc = T // C
    zw = HW + LANES

    def body(qkv_ref, zs_ref, do_ref, sin_ref, alog_ref, dtb_ref, gain_ref,
             dqkv_ref, dzs_ref, dalog_ref, ddtb_ref, dgain_ref, ds_scr):
        @pl.when(pl.program_id(0) == 0)
        def _():
            ds_scr[...] = jnp.zeros_like(ds_scr)
            dalog_ref[...] = jnp.zeros_like(dalog_ref)
            ddtb_ref[...] = jnp.zeros_like(ddtb_ref)
            dgain_ref[...] = jnp.zeros_like(dgain_ref)

        primals = (sin_ref[0], _split_heads(qkv_ref, 0, H),
                   _split_heads(qkv_ref, HW, H), _split_heads(qkv_ref, 2 * HW, H), _split_heads(zs_ref, 0, H),
                   zs_ref[:, HW:HW + LANES], alog_ref[...], dtb_ref[...], gain_ref[...])
        _, vjp = jax.vjp(_gdn_step, *primals)
        dS, dq, dk, dv, dz, dsmall, dalog, ddtb, dgain = vjp((_split_heads(do_ref, 0, H), ds_scr[...]))
        ds_scr[...] = dS
        _store_heads(dqkv_ref, 0, dq)
        _store_heads(dqkv_ref, HW, dk)
        _store_heads(dqkv_ref, 2 * HW, dv)
        _store_heads(dzs_ref, 0, dz)
        dzs_ref[:, HW:HW + LANES] = dsmall
        dalog_ref[...] += dalog
        ddtb_ref[...] += ddtb
        dgain_ref[...] += dgain

    row = pl.BlockSpec((1, LANES), lambda i: (0, 0))
    rev = lambda i: nc - 1 - i
    return pl.pallas_call(
        body, name="gdn_bwd", grid=(nc,),
        in_specs=[pl.BlockSpec((C, 3 * HW), lambda i: (rev(i), 0)), pl.BlockSpec((C, zw), lambda i: (rev(i), 0)),
                  pl.BlockSpec((C, HW), lambda i: (rev(i), 0)),
                  pl.BlockSpec((1, H, HEAD, HEAD), lambda i: (rev(i), 0, 0, 0)), row, row, row],
        out_specs=[pl.BlockSpec((C, 3 * HW), lambda i: (rev(i), 0)), pl.BlockSpec((C, zw), lambda i: (rev(i), 0)),
                   row, row, row],
        out_shape=[jax.ShapeDtypeStruct(qkv.shape, F32), jax.ShapeDtypeStruct(proj_zs.shape, F32)]
        + [jax.ShapeDtypeStruct((1, LANES), F32)] * 3,
        scratch_shapes=[pltpu.VMEM((H, HEAD, HEAD), F32)], compiler_params=_params(),
    )(qkv, proj_zs, do, s_all, alog, dtb, gain)


def _hgrn_refs(proj_ref, lb_ref, HP):
    W = HP * HEAD
    return (_split_heads(proj_ref, 0, HP), _split_heads(proj_ref, W, HP), _split_heads(proj_ref, 2 * W, HP),
            _split_heads(proj_ref, 3 * W, HP), _split_heads(lb_ref, 0, HP, slice(0, 1)),
            _split_heads(lb_ref, 0, HP, slice(1, 2)))


def _hgrn_fwd(proj, lower_bounds, gain, nheads):
    T = proj.shape[0]
    C, HP = HGRN_CHUNK, HGRN_HEADS_PER_STEP
    ng, nc, W = nheads // HP, T // C, HP * HEAD

    def body(proj_ref, lb_ref, gain_ref, o_ref, sall_ref, s_scr):
        @pl.when(pl.program_id(1) == 0)
        def _():
            s_scr[...] = jnp.zeros_like(s_scr)

        sall_ref[0] = s_scr[...]
        qr, fr, vi, z, lb0, lb1 = _hgrn_refs(proj_ref, lb_ref, HP)
        outs, states = _hgrn_step(s_scr[...], qr, fr, vi, z, lb0, lb1, gain_ref[...])
        _store_heads(o_ref, 0, outs)
        s_scr[...] = states

    return pl.pallas_call(
        body, name="hgrn_fwd", grid=(ng, nc),
        in_specs=[pl.BlockSpec((C, 4 * W), lambda g, i: (i, g)), pl.BlockSpec((2, W), lambda g, i: (0, g)),
                  pl.BlockSpec((1, LANES), lambda g, i: (0, 0))],
        out_specs=[pl.BlockSpec((C, W), lambda g, i: (i, g)),
                   pl.BlockSpec((1, HP, HEAD, HEAD), lambda g, i: (i, g, 0, 0))],
        out_shape=[jax.ShapeDtypeStruct((T, nheads * HEAD), F32), jax.ShapeDtypeStruct((nc, nheads, HEAD, HEAD), F32)],
        scratch_shapes=[pltpu.VMEM((HP, HEAD, HEAD), F32)], compiler_params=_params(),
    )(proj, lower_bounds, gain)


def _hgrn_bwd(proj, do, s_all, lower_bounds, gain, nheads):
    T = proj.shape[0]
    C, HP = HGRN_CHUNK, HGRN_HEADS_PER_STEP
    ng, nc, W = nheads // HP, T // C, HP * HEAD

    def body(proj_ref, do_ref, sin_ref, lb_ref, gain_ref, dproj_ref, dlb_ref, dgain_ref, ds_scr):
        first = pl.program_id(1) == 0

        @pl.when(first)
        def _():
            ds_scr[...] = jnp.zeros_like(ds_scr)
            dlb_ref[...] = jnp.zeros_like(dlb_ref)

        @pl.when(first & (pl.program_id(0) == 0))
        def _():
            dgain_ref[...] = jnp.zeros_like(dgain_ref)

        qr, fr, vi, z, lb0, lb1 = _hgrn_refs(proj_ref, lb_ref, HP)
        primals = (sin_ref[0], qr, fr, vi, z, lb0, lb1, gain_ref[...])
        _, vjp = jax.vjp(_hgrn_step, *primals)
        dS, dq, df, dv, dz, dlb0, dlb1, dgain = vjp((_split_heads(do_ref, 0, HP), ds_scr[...]))
        ds_scr[...] = dS
        for part, d in enumerate((dq, df, dv, dz)):
            _store_heads(dproj_ref, part * W, d)
        _store_heads(dlb_ref, 0, dlb0, slice(0, 1), accumulate=True)
        _store_heads(dlb_ref, 0, dlb1, slice(1, 2), accumulate=True)
        dgain_ref[...] += dgain

    rev = lambda i: nc - 1 - i
    return pl.pallas_call(
        body, name="hgrn_bwd", grid=(ng, nc),
        in_specs=[pl.BlockSpec((C, 4 * W), lambda g, i: (rev(i), g)), pl.BlockSpec((C, W), lambda g, i: (rev(i), g)),
                  pl.BlockSpec((1, HP, HEAD, HEAD), lambda g, i: (rev(i), g, 0, 0)),
                  pl.BlockSpec((2, W), lambda g, i: (0, g)), pl.BlockSpec((1, LANES), lambda g, i: (0, 0))],
        out_specs=[pl.BlockSpec((C, 4 * W), lambda g, i: (rev(i), g)), pl.BlockSpec((2, W), lambda g, i: (0, g)),
                   pl.BlockSpec((1, LANES), lambda g, i: (0, 0))],
        out_shape=[jax.ShapeDtypeStruct(proj.shape, F32), jax.ShapeDtypeStruct(lower_bounds.shape, F32),
                   jax.ShapeDtypeStruct((1, LANES), F32)],
        scratch_shapes=[pltpu.VMEM((HP, HEAD, HEAD), F32)], compiler_params=_params(),
    )(proj, do, s_all, lower_bounds, gain)


def _post_specs(T):
    tr = min(POST_TILE, T)
    tile = lambda w: pl.BlockSpec((tr, w), lambda i: (i, 0))
    full = lambda r, w: pl.BlockSpec((r, w), lambda i: (0, 0))
    return tr, tile, full


def _post_fwd(s, x, p, g, b, wg, wpl, name):
    T, D = x.shape
    P = p.shape[1]
    tr, tile, full = _post_specs(T)

    def body(s_ref, x_ref, p_ref, g_ref, b_ref, wg_ref, wpl_ref, o_ref):
        x1 = _post_norm(s_ref[...], x_ref[...], g_ref[...], b_ref[...])
        o_ref[...] = _post_gate(x1, _dot_raw(x1, wg_ref[...], "nn", False), _dot_raw(p_ref[...], wpl_ref[...], "nn", False))

    return pl.pallas_call(
        body, name=name, grid=(T // tr,),
        in_specs=[tile(D), tile(D), tile(P), full(1, D), full(1, D), full(D, D), full(P, D)],
        out_specs=tile(D), out_shape=jax.ShapeDtypeStruct((T, D), F32), compiler_params=_params(),
    )(s, x, p, g, b, wg, wpl)


def _post_bwd(s, x, p, g, b, wg, wpl, dnext, name, with_loss):
    T, D = x.shape
    P = p.shape[1]
    tr, tile, full = _post_specs(T)

    def body(s_ref, x_ref, p_ref, g_ref, b_ref, wg_ref, wpl_ref, dn_ref,
             ds_ref, dx_ref, dg_ref, db_ref, dwg_ref, dwpl_ref, loss_ref):
        @pl.when(pl.program_id(0) == 0)
        def _():
            for r in (dg_ref, db_ref, dwg_ref, dwpl_ref, loss_ref):
                r[...] = jnp.zeros_like(r)

        x1, vjp_norm = jax.vjp(_post_norm, s_ref[...], x_ref[...], g_ref[...], b_ref[...])
        gate_pre = _dot_raw(x1, wg_ref[...], "nn", False)
        pp = _dot_raw(p_ref[...], wpl_ref[...], "nn", False)
        xn, vjp_gate = jax.vjp(_post_gate, x1, gate_pre, pp)
        if with_loss:
            err = xn - dn_ref[...]
            loss_ref[...] += 0.5 * jnp.sum(jnp.sum(err * err, axis=-1, keepdims=True), axis=0, keepdims=True) / D
            dn = err / D
        else:
            dn = dn_ref[...]
        dx1, dgp, dpp = vjp_gate(dn)
        dwg_ref[...] += _dot_raw(x1, dgp, "tn", False)
        dwpl_ref[...] += _dot_raw(p_ref[...], dpp, "tn", False)
        dx1 = dx1 + _dot_raw(dgp, wg_ref[...], "nt", False)
        ds, dx, dg, db = vjp_norm(dx1)
        ds_ref[...] = ds
        dx_ref[...] = dx
        dg_ref[...] += dg
        db_ref[...] += db

    return pl.pallas_call(
        body, name=name, grid=(T // tr,),
        in_specs=[tile(D), tile(D), tile(P), full(1, D), full(1, D), full(D, D), full(P, D), tile(D)],
        out_specs=[tile(D), tile(D), full(1, D), full(1, D), full(D, D), full(P, D), full(SUBLANES, LANES)],
        out_shape=[jax.ShapeDtypeStruct((T, D), F32)] * 2 + [jax.ShapeDtypeStruct((1, D), F32)] * 2
        + [jax.ShapeDtypeStruct((D, D), F32), jax.ShapeDtypeStruct((P, D), F32),
           jax.ShapeDtypeStruct((SUBLANES, LANES), F32)],
        compiler_params=_params(),
    )(s, x, p, g, b, wg, wpl, dnext)


def _adam_math(w, g, m, v):
    m = ADAM_B1 * m + (1.0 - ADAM_B1) * g
    v = ADAM_B2 * v + (1.0 - ADAM_B2) * (g * g)
    m_hat = m / (1.0 - ADAM_B1 ** ADAM_STEP)
    v_hat = v / (1.0 - ADAM_B2 ** ADAM_STEP)
    return -ADAM_LR * (m_hat / (jnp.sqrt(v_hat) + ADAM_EPS) + ADAM_WD * w), m, v


def _rs_add(g8, got, cidx, name):
    _, R, C = g8.shape
    tr = _pick(R, (256, 128, 64, 32, 16, 8))

    def body(c_ref, a_ref, b_ref, o_ref, o16_ref):
        total = a_ref[...] + b_ref[...]
        o_ref[...] = total
        o16_ref[...] = total.astype(BF16)

    out_spec = pl.BlockSpec((1, tr, C), lambda q, i, c: (q, i, 0))
    return pl.pallas_call(
        body, name=name,
        grid_spec=pltpu.PrefetchScalarGridSpec(
            num_scalar_prefetch=1, grid=(4, R // tr),
            in_specs=[pl.BlockSpec((1, tr, C), lambda q, i, c: (2 * q + c[0], i, 0)),
                      pl.BlockSpec((1, tr, C), lambda q, i, c: (q, i, 0))],
            out_specs=[out_spec, out_spec]),
        out_shape=[jax.ShapeDtypeStruct((4,) + g8.shape[1:], F32), jax.ShapeDtypeStruct((4,) + g8.shape[1:], BF16)],
        compiler_params=_params(),
    )(cidx, g8, got)


def _adam_sharded(w, m, v, mine, got, qidx, name):
    R, C = w.shape
    tr = _pick(R, (256, 128, 64, 32, 16, 8))

    def body(q_ref, w_ref, m_ref, v_ref, p_ref, r0, r1, r2, g_ref, d_ref, mo_ref, vo_ref):
        g = ((p_ref[0] + r0[0].astype(F32)) + r1[0].astype(F32)) + r2[0].astype(F32)
        d, mn, vn = _adam_math(w_ref[...], g, m_ref[...], v_ref[...])
        g_ref[...] = g
        d_ref[...] = d
        mo_ref[...] = mn
        vo_ref[...] = vn

    t2 = pl.BlockSpec((tr, C), lambda i, q: (i, 0))
    slot = lambda k: pl.BlockSpec((1, tr, C), lambda i, q: (k, i, 0))
    return pl.pallas_call(
        body, name=name,
        grid_spec=pltpu.PrefetchScalarGridSpec(
            num_scalar_prefetch=1, grid=(R // tr,),
            in_specs=[t2, t2, t2, pl.BlockSpec((1, tr, C), lambda i, q: (q[0], i, 0)), slot(0), slot(1), slot(2)],
            out_specs=[t2, t2, t2, t2]),
        out_shape=[jax.ShapeDtypeStruct((R, C), F32)] * 4, compiler_params=_params(),
    )(qidx, w, m, v, mine, got, got, got)


def _adam_replicated(w, m, v, g8):
    def body(w_ref, m_ref, v_ref, g_ref, go_ref, d_ref, mo_ref, vo_ref):
        g = g_ref[0]
        for k in range(1, 8):
            g = g + g_ref[k]
        d, mn, vn = _adam_math(w_ref[...], g, m_ref[...], v_ref[...])
        go_ref[...] = g
        d_ref[...] = d
        mo_ref[...] = mn
        vo_ref[...] = vn

    return pl.pallas_call(
        body, name="adam_replicated", out_shape=[jax.ShapeDtypeStruct(w.shape, F32)] * 4, compiler_params=_params(),
    )(w, m, v, g8)


def _place():
    return lax.axis_index("x"), lax.axis_index("y"), lax.axis_index("c")


def _all_gather(shard, name):
    def body(x_ref, out_ref, send_sems, recv_sems, local_sem):
        x, y, c = _place()
        me, sibling = (x, y, c), (x, y, 1 - c)
        chips = [(1 - x, y), (x, 1 - y), (1 - x, 1 - y)]

        def slab(px, py, pc):
            return out_ref.at[4 * px + 2 * py + pc]

        def copy(k, block, to, src=None):
            return pltpu.make_async_remote_copy(
                src_ref=slab(*block) if src is None else src, dst_ref=slab(*block),
                send_sem=send_sems.at[k], recv_sem=recv_sems.at[k], device_id=to, device_id_type=MESH)

        mine = pltpu.make_async_copy(x_ref, slab(*me), local_sem)
        mine.start()
        first = [copy(0, me, sibling, src=x_ref)]
        first += [copy(1 + j, me, (*chip, c), src=x_ref) for j, chip in enumerate(chips)]
        for cp in first:
            cp.start()
        passed = [copy(4 + j, (*chip, c), sibling) for j, chip in enumerate(chips)]
        for j, chip in enumerate(chips):
            copy(1 + j, (*chip, c), me).wait_recv()
            passed[j].start()
        copy(0, sibling, me).wait_recv()
        for j, chip in enumerate(chips):
            copy(4 + j, (*chip, 1 - c), me).wait_recv()
        for cp in first + passed:
            cp.wait_send()
        mine.wait()

    return pl.pallas_call(
        body, name=name, out_shape=jax.ShapeDtypeStruct((8,) + shard.shape, shard.dtype),
        in_specs=[pl.BlockSpec(memory_space=pl.ANY)], out_specs=pl.BlockSpec(memory_space=pl.ANY),
        scratch_shapes=[pltpu.SemaphoreType.DMA((7,)), pltpu.SemaphoreType.DMA((7,)), pltpu.SemaphoreType.DMA],
    )(shard)


def _rs_to_sibling(g8, name):
    def body(g_ref, out_ref, send_sems, recv_sems):
        x, y, c = _place()
        copies = [pltpu.make_async_remote_copy(
            src_ref=g_ref.at[2 * q + (1 - c)], dst_ref=out_ref.at[q], send_sem=send_sems.at[q],
            recv_sem=recv_sems.at[q], device_id=(x, y, 1 - c), device_id_type=MESH) for q in range(4)]
        for cp in copies:
            cp.start()
        for cp in copies:
            cp.wait()

    return pl.pallas_call(
        body, name=name, out_shape=jax.ShapeDtypeStruct((4,) + g8.shape[1:], g8.dtype),
        in_specs=[pl.BlockSpec(memory_space=pl.ANY)], out_specs=pl.BlockSpec(memory_space=pl.ANY),
        scratch_shapes=[pltpu.SemaphoreType.DMA((4,)), pltpu.SemaphoreType.DMA((4,))],
    )(g8)


def _rs_to_chips(p4, name):
    def body(p_ref, out_ref, send_sems, recv_sems):
        x, y, c = _place()
        chips = [(1 - x, y), (x, 1 - y), (1 - x, 1 - y)]
        copies = [pltpu.make_async_remote_copy(
            src_ref=p_ref.at[2 * px + py], dst_ref=out_ref.at[j], send_sem=send_sems.at[j],
            recv_sem=recv_sems.at[j], device_id=(px, py, c), device_id_type=MESH) for j, (px, py) in enumerate(chips)]
        for cp in copies:
            cp.start()
        for cp in copies:
            cp.wait()

    return pl.pallas_call(
        body, name=name, out_shape=jax.ShapeDtypeStruct((3,) + p4.shape[1:], p4.dtype),
        in_specs=[pl.BlockSpec(memory_space=pl.ANY)], out_specs=pl.BlockSpec(memory_space=pl.ANY),
        scratch_shapes=[pltpu.SemaphoreType.DMA((3,)), pltpu.SemaphoreType.DMA((3,))],
    )(p4)


def _reduce_scatter_adam(g8, w, m, v, cidx, qidx, tag):
    got = _rs_to_sibling(g8, "rs_sibling_" + tag)
    chip_sums, chip_sums16 = _rs_add(g8, got, cidx, "rs_add_" + tag)
    got2 = _rs_to_chips(chip_sums16, "rs_chips_" + tag)
    return _adam_sharded(w, m, v, chip_sums, got2, qidx, "adam_" + tag)


def _local_grads(x, p0, p1, target, w_zs, w_a, w_qkv, wino, woute_a, woute_b, wouto, wg, wpl, conv_a, conv_b,
                 a_log, dt_bias, gdn_gain, lower_bounds, hgrn_gain, ln_g, ln_b):
    H = a_log.shape[1]
    nheads_o = wouto.shape[0] // HEAD
    pad_small = ((0, 0), (H, LANES - 2 * H))
    alog_row = jnp.pad(a_log, pad_small)
    dtb_row = jnp.pad(dt_bias, pad_small)

    proj_zs = _matmul(x, w_zs, "nn", "proj_even_zs")
    proj_a = _matmul(x, w_a, "nn", "proj_even_a")
    proj_qkv = _matmul(x, w_qkv, "nn", "proj_even_qkv")
    y_a = _mixer_a_fwd(proj_a, conv_a)
    qkv = _conv_b_fwd(proj_qkv, conv_b)
    o2, s_gdn = _gdn_fwd(qkv, proj_zs, alog_row, dtb_row, gdn_gain, H)
    s_e = _matmul(o2, woute_b, "nn", "out_even_b", add=_matmul(y_a, woute_a, "nn", "out_even_a"))
    x2 = _post_fwd(s_e, x, p0, ln_g[0:1], ln_b[0:1], wg[0], wpl[0], "post_even_fwd")
    proj_o = _matmul(x2, wino, "nn", "proj_odd")
    o4, s_hgrn = _hgrn_fwd(proj_o, lower_bounds, hgrn_gain, nheads_o)
    s_o = _matmul(o4, wouto, "nn", "out_odd")
    ds_o, dx2, dlng1, dlnb1, dwg1, dwpl1, loss = _post_bwd(
        s_o, x2, p1, ln_g[1:2], ln_b[1:2], wg[1], wpl[1], target, "post_odd_loss_bwd", True)
    do4 = _matmul(ds_o, wouto, "nt", "d_out_odd_act")
    dwouto = _matmul(o4, ds_o, "tn", "d_out_odd_w")
    dproj_o, dlb, dhgain = _hgrn_bwd(proj_o, do4, s_hgrn, lower_bounds, hgrn_gain, nheads_o)
    dx2 = _matmul(dproj_o, wino, "nt", "d_proj_odd_act", add=dx2)
    dwino = _matmul(x2, dproj_o, "tn", "d_proj_odd_w")
    ds_e, dx, dlng0, dlnb0, dwg0, dwpl0, _ = _post_bwd(
        s_e, x, p0, ln_g[0:1], ln_b[0:1], wg[0], wpl[0], dx2, "post_even_bwd", False)
    dy_a = _matmul(ds_e, woute_a, "nt", "d_out_even_a_act")
    do2 = _matmul(ds_e, woute_b, "nt", "d_out_even_b_act")
    dwoute_a = _matmul(y_a, ds_e, "tn", "d_out_even_a_w")
    dwoute_b = _matmul(o2, ds_e, "tn", "d_out_even_b_w")
    dqkv, dproj_zs, dalog, ddtb, dggain = _gdn_bwd(qkv, proj_zs, do2, s_gdn, alog_row, dtb_row, gdn_gain, H)
    dproj_qkv, dconv_b = _conv_b_bwd(proj_qkv, dqkv, conv_b)
    dproj_a, dconv_a = _mixer_a_bwd(proj_a, dy_a, conv_a)
    dx = _matmul(dproj_zs, w_zs, "nt", "d_proj_even_zs_act", add=dx)
    dx = _matmul(dproj_a, w_a, "nt", "d_proj_even_a_act", add=dx)
    dx = _matmul(dproj_qkv, w_qkv, "nt", "d_proj_even_qkv_act", add=dx)
    dw_zs = _matmul(x, dproj_zs, "tn", "d_proj_even_zs_w")
    dw_a = _matmul(x, dproj_a, "tn", "d_proj_even_a_w")
    dw_qkv = _matmul(x, dproj_qkv, "tn", "d_proj_even_qkv_w")
    return dict(
        loss=loss[0, 0], grad_x=dx, w_zs=dw_zs, w_a=dw_a, w_qkv=dw_qkv, wino=dwino,
        woute=jnp.concatenate([dwoute_a, dwoute_b], axis=0), wouto=dwouto,
        wg=jnp.stack([dwg0, dwg1]), wpl=jnp.stack([dwpl0, dwpl1]),
        conv_a=dconv_a[:conv_a.shape[0]], conv_b=dconv_b[:conv_b.shape[0]],
        a_log=dalog[:, H:2 * H], dt_bias=ddtb[:, H:2 * H], gdn_gain=dggain, lower_bounds=dlb, hgrn_gain=dhgain,
        ln_g=jnp.concatenate([dlng0, dlng1], axis=0), ln_b=jnp.concatenate([dlnb0, dlnb1], axis=0))


def _pad_rows(a, rows):
    return jnp.pad(a, ((0, rows - a.shape[0]), (0, 0)))


def _pack_small(a_log, dt_bias, gdn_gain, lower_bounds, hgrn_gain, ln_g, ln_b):
    lane_pad = lambda a: _pad_rows(jnp.pad(a, ((0, 0), (0, LANES - a.shape[1]))), SUBLANES)
    parts = [lane_pad(a_log), lane_pad(dt_bias), lane_pad(gdn_gain), lower_bounds.reshape(-1, LANES),
             lane_pad(hgrn_gain), ln_g.reshape(-1, LANES), ln_b.reshape(-1, LANES)]
    packed = jnp.concatenate(parts, axis=0)
    assert packed.shape[0] == SMALL_ROWS, packed.shape
    return packed


def _unpack_small(packed, shapes):
    out, r = [], 0
    for shp in shapes:
        n = shp[0] * shp[1]
        if n < LANES * SUBLANES and shp[1] <= LANES:
            out.append(packed[r:r + shp[0], :shp[1]])
            r += SUBLANES
        else:
            rows = n // LANES
            out.append(packed[r:r + rows].reshape(shp))
            r += rows
    return out


def _split_in_even(w_full, AW, HW, H):
    D = w_full.shape[0]
    n_a = 4 * AW
    n_main = n_a + 3 * HW
    w_zs = jnp.concatenate([w_full[:, n_main:n_main + HW], w_full[:, n_main + HW:],
                            jnp.zeros((D, LANES - 2 * H), w_full.dtype)], axis=1)
    w_a = w_full[:, :n_a].reshape(D, 4, AW // MIXER_LANES, MIXER_LANES).transpose(0, 2, 1, 3).reshape(D, n_a)
    return w_zs, w_a, w_full[:, n_a:n_main]


def _join_in_even(d_zs, d_a, d_qkv, AW, HW, H):
    D = d_a.shape[0]
    a_nat = d_a.reshape(D, AW // MIXER_LANES, 4, MIXER_LANES).transpose(0, 2, 1, 3).reshape(D, 4 * AW)
    return jnp.concatenate([a_nat, d_qkv, d_zs[:, :HW], d_zs[:, HW:HW + 2 * H]], axis=1)


def kernel(x, p, w_in_even, conv_a_w, conv_b_w, a_log, dt_bias, gdn_norm_g, w_out_even, w_in_odd, lower_bounds, hgrn_norm_g, w_out_odd, ln_g, ln_b, w_pl, w_pl_gate, loss_target, m_w_in_even, m_conv_a_w, m_conv_b_w, m_a_log, m_dt_bias, m_gdn_norm_g, m_w_out_even, m_w_in_odd, m_lower_bounds, m_hgrn_norm_g, m_w_out_odd, m_ln_g, m_ln_b, m_w_pl, m_w_pl_gate, v_w_in_even, v_conv_a_w, v_conv_b_w, v_a_log, v_dt_bias, v_gdn_norm_g, v_w_out_even, v_w_in_odd, v_lower_bounds, v_hgrn_norm_g, v_w_out_odd, v_ln_g, v_ln_b, v_w_pl, v_w_pl_gate):
    xi, yi, ci = _place()
    cidx = jnp.reshape(ci, (1,)).astype(jnp.int32)
    qidx = jnp.reshape(2 * xi + yi, (1,)).astype(jnp.int32)
    D = x.shape[2]
    H = a_log.shape[1]
    HW = H * HEAD
    AW = conv_a_w.shape[2] * 8
    OW = w_out_odd.shape[1] * 8
    PD = w_pl.shape[1]
    ka, kb = conv_a_w.shape[1], conv_b_w.shape[1]
    ca, cb = conv_a_w.shape[2], conv_b_w.shape[2]
    gw = HGRN_HEADS_PER_STEP * HEAD
    ngrp = OW // gw

    g_ine = _all_gather(w_in_even[0].astype(BF16), "ag_w_in_even")
    w_zs, w_a, w_qkv = _split_in_even(jnp.transpose(g_ine, (1, 0, 2)).reshape(D, -1), AW, HW, H)
    g_ino = _all_gather(w_in_odd[0].astype(BF16), "ag_w_in_odd")
    wino = jnp.transpose(g_ino, (1, 0, 2)).reshape(D, 4, ngrp, gw).transpose(0, 2, 1, 3).reshape(D, 4 * OW)
    woute = _all_gather(w_out_even[0].astype(BF16), "ag_w_out_even").reshape(-1, D)
    wouto = _all_gather(w_out_odd[0].astype(BF16), "ag_w_out_odd").reshape(-1, D)
    g_gate = _all_gather(w_pl_gate.astype(BF16).reshape(-1, D), "ag_w_pl_gate")
    wg = g_gate.reshape(8, DEPTH, D // 8, D).transpose(1, 0, 2, 3).reshape(DEPTH, D, D)
    g_pl = _all_gather(w_pl.astype(BF16).reshape(DEPTH * PD, -1), "ag_w_pl")
    wpl = g_pl.reshape(8, DEPTH, PD, D // 8).transpose(1, 2, 0, 3).reshape(DEPTH, PD, D)
    taps = jnp.concatenate([_pad_rows(conv_a_w[0], SUBLANES), _pad_rows(conv_b_w[0], SUBLANES)], axis=1)
    g_taps = _all_gather(taps, "ag_conv")
    conv_a = jnp.transpose(g_taps[:, :ka, :ca], (1, 0, 2)).reshape(ka, 8 * ca)
    conv_b = jnp.transpose(g_taps[:, :kb, ca:], (1, 0, 2)).reshape(kb, 8 * cb)

    gr = _local_grads(x[0], p[0, 0], p[1, 0], loss_target[0], w_zs, w_a, w_qkv, wino, woute[:AW], woute[AW:], wouto,
                      wg, wpl, conv_a, conv_b, a_log, dt_bias, gdn_norm_g, lower_bounds, hgrn_norm_g, ln_g, ln_b)
    loss = lax.psum(gr["loss"], AXES)

    dw_nat = _join_in_even(gr["w_zs"], gr["w_a"], gr["w_qkv"], AW, HW, H)
    sh = w_in_even.shape[2]
    o_ine = _reduce_scatter_adam(dw_nat.reshape(D, 8, sh).transpose(1, 0, 2), w_in_even[0], m_w_in_even[0],
                                 v_w_in_even[0], cidx, qidx, "w_in_even")
    dwo = gr["wino"].reshape(D, ngrp, 4, gw).transpose(0, 2, 1, 3).reshape(D, 8, 4 * OW // 8).transpose(1, 0, 2)
    o_ino = _reduce_scatter_adam(dwo, w_in_odd[0], m_w_in_odd[0], v_w_in_odd[0], cidx, qidx, "w_in_odd")
    o_oute = _reduce_scatter_adam(gr["woute"].reshape(8, -1, D), w_out_even[0], m_w_out_even[0], v_w_out_even[0],
                                  cidx, qidx, "w_out_even")
    o_outo = _reduce_scatter_adam(gr["wouto"].reshape(8, -1, D), w_out_odd[0], m_w_out_odd[0], v_w_out_odd[0],
                                  cidx, qidx, "w_out_odd")
    dgate = gr["wg"].reshape(DEPTH, 8, D // 8, D).transpose(1, 0, 2, 3).reshape(8, DEPTH * D // 8, D)
    flat_gate = lambda a: a.reshape(DEPTH * D // 8, D)
    o_gate = _reduce_scatter_adam(dgate, flat_gate(w_pl_gate), flat_gate(m_w_pl_gate), flat_gate(v_w_pl_gate),
                                  cidx, qidx, "w_pl_gate")
    dpl = gr["wpl"].reshape(DEPTH, PD, 8, D // 8).transpose(2, 0, 1, 3).reshape(8, DEPTH * PD, D // 8)
    flat_pl = lambda a: a.reshape(DEPTH * PD, D // 8)
    o_pl = _reduce_scatter_adam(dpl, flat_pl(w_pl), flat_pl(m_w_pl), flat_pl(v_w_pl), cidx, qidx, "w_pl")
    dtaps = jnp.concatenate([_pad_rows(gr["conv_a"], SUBLANES).reshape(SUBLANES, 8, ca).transpose(1, 0, 2),
                             _pad_rows(gr["conv_b"], SUBLANES).reshape(SUBLANES, 8, cb).transpose(1, 0, 2)], axis=2)
    pack_taps = lambda a, b: jnp.concatenate([_pad_rows(a[0], SUBLANES), _pad_rows(b[0], SUBLANES)], axis=1)
    o_taps = _reduce_scatter_adam(dtaps, taps, pack_taps(m_conv_a_w, m_conv_b_w), pack_taps(v_conv_a_w, v_conv_b_w),
                                  cidx, qidx, "conv")

    small_g = _pack_small(gr["a_log"], gr["dt_bias"], gr["gdn_gain"], gr["lower_bounds"], gr["hgrn_gain"],
                          gr["ln_g"], gr["ln_b"])
    o_small = _adam_replicated(
        _pack_small(a_log, dt_bias, gdn_norm_g, lower_bounds, hgrn_norm_g, ln_g, ln_b),
        _pack_small(m_a_log, m_dt_bias, m_gdn_norm_g, m_lower_bounds, m_hgrn_norm_g, m_ln_g, m_ln_b),
        _pack_small(v_a_log, v_dt_bias, v_gdn_norm_g, v_lower_bounds, v_hgrn_norm_g, v_ln_g, v_ln_b),
        _all_gather(small_g, "ag_small_grads"))
    small_shapes = [a_log.shape, dt_bias.shape, gdn_norm_g.shape, lower_bounds.shape, hgrn_norm_g.shape,
                    ln_g.shape, ln_b.shape]

    def leaves(kind):
        s_alog, s_dt, s_gg, s_lb, s_hg, s_lng, s_lnb = _unpack_small(o_small[kind], small_shapes)
        t = o_taps[kind]
        return [o_ine[kind][None], t[None, :ka, :ca], t[None, :kb, ca:], s_alog, s_dt, s_gg, o_oute[kind][None],
                o_ino[kind][None], s_lb, s_hg, o_outo[kind][None], s_lng, s_lnb,
                o_pl[kind].reshape(w_pl.shape), o_gate[kind].reshape(w_pl_gate.shape)]

    return (loss, gr["grad_x"][None], *leaves(0), *leaves(1), *leaves(2), *leaves(3))
```

```python
import functools

import jax
import jax.numpy as jnp
from jax import lax
from jax.experimental import pallas as pl
from jax.experimental.pallas import tpu as pltpu

F32 = jnp.float32
BF16 = jnp.bfloat16
MESH = pl.DeviceIdType.MESH
AXES = ("x", "y", "c")

LANES = 128
SUBLANES = 8
HEAD = 128
GDN_CHUNK = 64
HGRN_CHUNK = 32
HGRN_SUB = 16
HGRN_HEADS_PER_STEP = 16
NORM_EPS = 1e-5
DEPTH = 2
ALPHA = (2.0 * DEPTH) ** 0.25
EXP_CLAMP = 80.0
ADAM_LR, ADAM_B1, ADAM_B2, ADAM_EPS, ADAM_WD, ADAM_STEP = 0.001, 0.9, 0.999, 1e-08, 0.01, 10
VMEM_LIMIT = 56 * 1024 * 1024
MATMUL_VMEM = 36 * 1024 * 1024
ROW_TILE = 512
MIXER_LANES = 256
CONV_LANES = 512
POST_TILE = 256
SMALL_ROWS = 96

_NOBATCH, _BATCH0 = ((), ()), ((0,), (0,))
_DIMS = {"nn": (((1,), (0,)), _NOBATCH), "nt": (((1,), (1,)), _NOBATCH), "tn": (((0,), (0,)), _NOBATCH),
         "bnn": (((2,), (1,)), _BATCH0), "bnt": (((2,), (2,)), _BATCH0), "btn": (((1,), (1,)), _BATCH0)}


def _params(**kw):
    return pltpu.CompilerParams(vmem_limit_bytes=VMEM_LIMIT, **kw)


def _dot_raw(a, b, kind, hi):
    if hi:
        return lax.dot_general(a, b, _DIMS[kind], precision=lax.Precision.HIGHEST, preferred_element_type=F32)
    return lax.dot_general(a.astype(BF16), b.astype(BF16), _DIMS[kind], preferred_element_type=F32)


@functools.partial(jax.custom_vjp, nondiff_argnums=(2, 3))
def mdot(a, b, kind, hi):
    return _dot_raw(a, b, kind, hi)


def _mdot_fwd(a, b, kind, hi):
    return _dot_raw(a, b, kind, hi), (a, b)


def _mdot_bwd(kind, hi, res, g):
    a, b = res
    pre, base = kind[:-2], kind[-2:]
    if base == "nn":
        return _dot_raw(g, b, pre + "nt", hi), _dot_raw(a, g, pre + "tn", hi)
    if base == "nt":
        return _dot_raw(g, b, pre + "nn", hi), _dot_raw(g, a, pre + "tn", hi)
    return _dot_raw(b, g, pre + "nt", hi), _dot_raw(a, g, pre + "nn", hi)


mdot.defvjp(_mdot_fwd, _mdot_bwd)


def _rows(x, lo, hi):
    return _take_rows(x, lo, hi, x.shape[-2])


@functools.partial(jax.custom_vjp, nondiff_argnums=(1, 2, 3))
def _take_rows(x, lo, hi, n):
    return x[..., lo:hi, :]


def _take_rows_fwd(x, lo, hi, n):
    return x[..., lo:hi, :], None


def _take_rows_bwd(lo, hi, n, _, g):
    parts = []
    if lo > 0:
        parts.append(jnp.zeros(g.shape[:-2] + (lo, g.shape[-1]), g.dtype))
    parts.append(g)
    if n - hi > 0:
        parts.append(jnp.zeros(g.shape[:-2] + (n - hi, g.shape[-1]), g.dtype))
    return (jnp.concatenate(parts, axis=-2) if len(parts) > 1 else g,)


_take_rows.defvjp(_take_rows_fwd, _take_rows_bwd)


def _heads_of(wide, nheads):
    return jnp.stack([wide[:, h * HEAD:(h + 1) * HEAD] for h in range(nheads)], axis=0)


def _wide_of(x):
    return jnp.concatenate([x[h] for h in range(x.shape[0])], axis=1)


@functools.partial(jax.custom_vjp, nondiff_argnums=(1,))
def to_heads(wide, nheads):
    return _heads_of(wide, nheads)


to_heads.defvjp(lambda wide, nheads: (_heads_of(wide, nheads), None), lambda nheads, _, g: (_wide_of(g),))


@jax.custom_vjp
def to_wide(x):
    return _wide_of(x)


to_wide.defvjp(lambda x: (_wide_of(x), None), lambda _, g: (_heads_of(g, g.shape[1] // HEAD),))


def _sigmoid(x):
    return jax.nn.sigmoid(x)


def _silu(x):
    return x * _sigmoid(x)


def _dsilu(x):
    s = _sigmoid(x)
    return s * (1.0 + x * (1.0 - s))


def _log1p(u):
    return jnp.where(u < 1e-4, u * (1.0 - 0.5 * u), jnp.log(1.0 + u))


def _softplus(x):
    return jnp.maximum(x, 0.0) + _log1p(jnp.exp(-jnp.abs(x)))


def _rms_gate(o, gain, z):
    return o * lax.rsqrt(jnp.mean(o * o, axis=-1, keepdims=True) + NORM_EPS) * gain * _silu(z)


def _l2n(x):
    return x * lax.rsqrt(jnp.sum(x * x, axis=-1, keepdims=True) + 1e-6)


def _split_dot_raw(m, x, kind):
    mb = m.astype(BF16)
    hi = x.astype(BF16)
    lo = (x - hi.astype(F32)).astype(BF16)
    dims = _DIMS[kind]
    return (lax.dot_general(mb, hi, dims, preferred_element_type=F32)
            + lax.dot_general(mb, lo, dims, preferred_element_type=F32))


@jax.custom_vjp
def mask_dot(m, x):
    return _split_dot_raw(m, x, "nn")


def _mask_dot_fwd(m, x):
    return _split_dot_raw(m, x, "nn"), m


def _mask_dot_bwd(m, g):
    return jnp.zeros_like(m), _split_dot_raw(m, g, "tn")


mask_dot.defvjp(_mask_dot_fwd, _mask_dot_bwd)


def _unit_lower_inverse_minus_eye(low, n):
    rest = -low
    power = low
    span = 2
    while span < n:
        power = mdot(power, power, "bnn", False)
        rest = rest + power + mdot(rest, power, "bnn", False)
        span *= 2
    return rest


def _gdn_step(S, q, k, v, z, small, alog, dtb, gain):
    H = S.shape[0]
    C = GDN_CHUNK
    row = lax.broadcasted_iota(jnp.int32, (C, C), 0)
    col = lax.broadcasted_iota(jnp.int32, (C, C), 1)
    tril, strict, eye = (row >= col)[None], (row > col)[None], (row == col)[None]
    head = lax.broadcasted_iota(jnp.int32, (H, 1, LANES), 0)
    lane = lax.broadcasted_iota(jnp.int32, (H, 1, LANES), 2)
    rowc = lax.broadcasted_iota(jnp.int32, (1, C, 1), 1)
    beta_all = _sigmoid(small)
    g_all = -jnp.exp(alog) * _softplus(small + dtb)
    gc_all = mask_dot((row >= col).astype(F32), g_all)
    beta = jnp.sum(jnp.where(lane == head, beta_all[None], 0.0), axis=-1, keepdims=True)
    gc = jnp.sum(jnp.where(lane == head + H, gc_all[None], 0.0), axis=-1, keepdims=True)
    gc_row = jnp.sum(jnp.where(eye, gc, 0.0), axis=1, keepdims=True)
    decay = jnp.where(tril, jnp.exp(jnp.where(tril, gc - gc_row, 0.0)), 0.0)
    g_last = jnp.sum(jnp.where(rowc == C - 1, gc, 0.0), axis=1, keepdims=True)
    qn = _l2n(q) * (HEAD ** -0.5)
    kn = _l2n(k)
    kb = kn * beta
    low = jnp.where(strict, mdot(kb, kn, "bnt", False) * decay, 0.0)
    inv_rest = _unit_lower_inverse_minus_eye(low, C)
    eg = jnp.exp(gc)
    vb, kbe = v * beta, kb * eg
    u = vb + mdot(inv_rest, vb, "bnn", False)
    w = kbe + mdot(inv_rest, kbe, "bnn", False)
    attn = mdot(qn, kn, "bnt", False) * decay
    v_new = u - mdot(w, S, "bnn", False)
    o = mdot(qn * eg, S, "bnn", False) + mdot(attn, v_new, "bnn", False)
    k_dec = kn * jnp.exp(g_last - gc)
    return _rms_gate(o, gain, z), S * jnp.exp(g_last) + mdot(k_dec, v_new, "btn", False)


def _hgrn_step(St, qr, fr, vi, z, lb0, lb1, gain):
    H = St.shape[0]
    C, SB = HGRN_CHUNK, HGRN_SUB
    row = lax.broadcasted_iota(jnp.int32, (C, C), 0)
    col = lax.broadcasted_iota(jnp.int32, (C, C), 1)
    blk_start = row - (row & (SB - 1))
    in_blk_f = ((row >= col) & (col >= blk_start)).astype(F32)
    before_f = (col < blk_start).astype(F32)
    sums_f = jnp.concatenate([in_blk_f, before_f], axis=0)
    m = jnp.maximum(lb0, lb1)
    e0, e1 = jnp.exp(lb0 - m), jnp.exp(lb1 - m)
    lb = e1 / (e0 + e1)
    f = lb + (1.0 - lb) * _sigmoid(fr)
    q = _silu(qr)
    k = 1.0 - f
    logf = jnp.log(f)
    sums = mask_dot(sums_f, to_wide(logf))
    inner, start = to_heads(_rows(sums, 0, C), H), to_heads(_rows(sums, C, 2 * C), H)
    b = start + inner
    b_last = jnp.sum(logf, axis=1, keepdims=True)
    o = mdot(q * jnp.exp(b), St, "bnt", False)
    qt = q * jnp.exp(inner)
    parts = []
    for blk in range(C // SB):
        lo, n = blk * SB, (blk + 1) * SB
        ref = jnp.concatenate([_rows(start, lo, n)] * (blk + 1), axis=1)
        kt = _rows(k, 0, n) * jnp.exp(jnp.minimum(ref - _rows(b, 0, n), EXP_CLAMP))
        att = mdot(_rows(qt, lo, n), kt, "bnt", False)
        t_idx = lax.broadcasted_iota(jnp.int32, (1, SB, n), 1) + lo
        s_idx = lax.broadcasted_iota(jnp.int32, (1, SB, n), 2)
        att = jnp.where(s_idx <= t_idx, att, 0.0)
        parts.append(mdot(att, _rows(vi, 0, n), "bnn", False))
    o = o + jnp.concatenate(parts, axis=1)
    k_dec = k * jnp.exp(b_last - b)
    return _rms_gate(o, gain, z), St * jnp.exp(b_last) + mdot(vi, k_dec, "btn", False)


def _post_norm(s, x, g, b):
    r = ALPHA * x + s
    d = r - jnp.mean(r, axis=-1, keepdims=True)
    var = jnp.mean(d * d, axis=-1, keepdims=True)
    return d * lax.rsqrt(var + NORM_EPS) * g + b


def _post_gate(x1, gate_pre, pp):
    return x1 + pp * _sigmoid(gate_pre)


def _pick(dim, cands):
    for c in cands:
        if dim % c == 0:
            return c
    return dim


def _matmul_tiles(M, K, tn, a_bytes, b_bytes, has_add):
    for tk in (4096, 2048, 1536, 1152, 1024, 640, 512, 384, 256, 128):
        if K % tk:
            continue
        for tm in (2048, 1024, 512, 256, 128):
            if M % tm:
                continue
            blocks = tm * tk * a_bytes + tk * tn * b_bytes + tm * tn * 4 * (2 if has_add else 1)
            if 2 * blocks + (tm * tn * 4 if tk < K else 0) <= MATMUL_VMEM and tm >= min(M, 1024):
                return tm, tk
    return _pick(M, (512, 256, 128)), _pick(K, (512, 256, 128))


def _matmul(a, b, kind, name, add=None):
    if kind == "nn":
        (M, K), N = a.shape, b.shape[1]
    elif kind == "nt":
        (M, K), N = a.shape, b.shape[0]
    else:
        (K, M), N = a.shape, b.shape[1]
    has_add = add is not None
    tn = _pick(N, (512, 640, 384, 256, 128))
    tm, tk = _matmul_tiles(M, K, tn, a.dtype.itemsize, b.dtype.itemsize, has_add)
    nk = K // tk
    a_spec = pl.BlockSpec((tk, tm), lambda i, j, k: (k, i)) if kind == "tn" else pl.BlockSpec((tm, tk), lambda i, j, k: (i, k))
    b_spec = pl.BlockSpec((tn, tk), lambda i, j, k: (j, k)) if kind == "nt" else pl.BlockSpec((tk, tn), lambda i, j, k: (k, j))
    o_spec = pl.BlockSpec((tm, tn), lambda i, j, k: (i, j))

    def body(a_ref, b_ref, *rest):
        add_ref = rest[0] if has_add else None
        o_ref = rest[1] if has_add else rest[0]
        part = _dot_raw(a_ref[...], b_ref[...], kind, False)
        if nk == 1:
            o_ref[...] = part + add_ref[...] if has_add else part
            return
        acc = rest[-1]
        kk = pl.program_id(2)

        @pl.when(kk == 0)
        def _():
            acc[...] = part

        @pl.when(kk > 0)
        def _():
            acc[...] += part

        @pl.when(kk == nk - 1)
        def _():
            o_ref[...] = acc[...] + add_ref[...] if has_add else acc[...]

    return pl.pallas_call(
        body, name=name, grid=(M // tm, N // tn, nk),
        in_specs=[a_spec, b_spec] + ([o_spec] if has_add else []),
        out_specs=o_spec, out_shape=jax.ShapeDtypeStruct((M, N), F32),
        scratch_shapes=[pltpu.VMEM((tm, tn), F32)] if nk > 1 else [],
        compiler_params=_params(dimension_semantics=("parallel", "parallel", "arbitrary")),
    )(*((a, b, add) if has_add else (a, b)))


def _halo_specs(ts, nt, width, prev=True, main=True, nxt=True):
    per = ts // SUBLANES
    last8 = nt * per - 1
    specs = []
    if prev:
        specs.append(pl.BlockSpec((SUBLANES, width), lambda cb, i: (jnp.maximum(i * per - 1, 0), cb)))
    if main:
        specs.append(pl.BlockSpec((ts, width), lambda cb, i: (i, cb)))
    if nxt:
        specs.append(pl.BlockSpec((SUBLANES, width), lambda cb, i: (jnp.minimum((i + 1) * per, last8), cb)))
    return specs


def _taps(ext, ktaps, lo, size):
    return [ext[lo:lo + size] if j == 0 else pltpu.roll(ext, j, 0)[lo:lo + size] for j in range(ktaps)]


def _ahead(ext, j, size):
    n = ext.shape[0]
    return ext[:size] if j == 0 else pltpu.roll(ext, n - j, 0)[:size]


def _lane_block(ref, k):
    return ref[:, k * MIXER_LANES:(k + 1) * MIXER_LANES]


def _mixer_a_fwd(proj_a, conv_w):
    T = proj_a.shape[0]
    nblk = proj_a.shape[1] // (4 * MIXER_LANES)
    ts = min(ROW_TILE, T)
    nt = T // ts

    def body(pp, pm, w_ref, y_ref):
        i = pl.program_id(1)
        u_prev = jnp.where(i > 0, _lane_block(pp, 0) * _lane_block(pp, 1), 0.0)
        ext = jnp.concatenate([u_prev, _lane_block(pm, 0) * _lane_block(pm, 1)], axis=0)
        t0, t1, t2 = _taps(ext, 3, SUBLANES, ts)
        cv = w_ref[2:3, :] * t0 + w_ref[1:2, :] * t1 + w_ref[0:1, :] * t2
        y_ref[...] = (_lane_block(pm, 2) * cv * _silu(_lane_block(pm, 3))).astype(y_ref.dtype)

    return pl.pallas_call(
        body, name="mixer_a_fwd", grid=(nblk, nt),
        in_specs=_halo_specs(ts, nt, 4 * MIXER_LANES, nxt=False)
        + [pl.BlockSpec((conv_w.shape[0], MIXER_LANES), lambda cb, i: (0, cb))],
        out_specs=pl.BlockSpec((ts, MIXER_LANES), lambda cb, i: (i, cb)),
        out_shape=jax.ShapeDtypeStruct((T, nblk * MIXER_LANES), BF16), compiler_params=_params(),
    )(proj_a, proj_a, conv_w)


def _mixer_a_bwd(proj_a, dy, conv_w):
    T = proj_a.shape[0]
    nblk = proj_a.shape[1] // (4 * MIXER_LANES)
    ts = min(ROW_TILE, T)
    nt = T // ts
    kt = conv_w.shape[0]

    def body(pp, pm, pn, dym, dyn, w_ref, dp_ref, dw_ref):
        i = pl.program_id(1)
        hm, cm, bm, zm = (_lane_block(pm, k) for k in range(4))
        u_prev = jnp.where(i > 0, _lane_block(pp, 0) * _lane_block(pp, 1), 0.0)
        ext = jnp.concatenate([u_prev, hm * cm], axis=0)
        dy_ext = jnp.concatenate([dym[...], jnp.where(i < nt - 1, dyn[...], 0.0)], axis=0)
        b_ext = jnp.concatenate([bm, _lane_block(pn, 2)], axis=0)
        sz_ext = _silu(jnp.concatenate([zm, _lane_block(pn, 3)], axis=0))
        dcv_ext = dy_ext * b_ext * sz_ext
        w = [w_ref[j:j + 1, :] for j in range(kt)]
        du = sum(w[kt - 1 - j] * _ahead(dcv_ext, j, ts) for j in range(kt))
        taps = _taps(ext, kt, SUBLANES, ts)
        cv = sum(w[kt - 1 - j] * taps[j] for j in range(kt))
        for part, d in enumerate((du * cm, du * hm, dym[...] * cv * sz_ext[:ts], dym[...] * bm * cv * _dsilu(zm))):
            dp_ref[:, part * MIXER_LANES:(part + 1) * MIXER_LANES] = d.astype(dp_ref.dtype)
        dcv = dcv_ext[:ts]

        @pl.when(i == 0)
        def _():
            dw_ref[...] = jnp.zeros_like(dw_ref)

        for j in range(kt):
            dw_ref[j:j + 1, :] += jnp.sum(dcv * taps[kt - 1 - j], axis=0, keepdims=True)

    return pl.pallas_call(
        body, name="mixer_a_bwd", grid=(nblk, nt),
        in_specs=_halo_specs(ts, nt, 4 * MIXER_LANES) + _halo_specs(ts, nt, MIXER_LANES, prev=False)
        + [pl.BlockSpec((kt, MIXER_LANES), lambda cb, i: (0, cb))],
        out_specs=[pl.BlockSpec((ts, 4 * MIXER_LANES), lambda cb, i: (i, cb)),
                   pl.BlockSpec((SUBLANES, MIXER_LANES), lambda cb, i: (0, cb))],
        out_shape=[jax.ShapeDtypeStruct(proj_a.shape, BF16),
                   jax.ShapeDtypeStruct((SUBLANES, nblk * MIXER_LANES), F32)],
        compiler_params=_params(),
    )(proj_a, proj_a, proj_a, dy, dy, conv_w)


def _conv_b_fwd(raw, conv_w):
    T = raw.shape[0]
    nblk = raw.shape[1] // CONV_LANES
    ts = min(ROW_TILE, T)
    nt = T // ts
    kt = conv_w.shape[0]

    def body(rp, rm, w_ref, y_ref):
        i = pl.program_id(1)
        ext = jnp.concatenate([jnp.where(i > 0, rp[...], 0.0), rm[...]], axis=0)
        taps = _taps(ext, kt, SUBLANES, ts)
        y_ref[...] = _silu(sum(w_ref[kt - 1 - j:kt - j, :] * taps[j] for j in range(kt)))

    return pl.pallas_call(
        body, name="conv_b_fwd", grid=(nblk, nt),
        in_specs=_halo_specs(ts, nt, CONV_LANES, nxt=False) + [pl.BlockSpec((kt, CONV_LANES), lambda cb, i: (0, cb))],
        out_specs=pl.BlockSpec((ts, CONV_LANES), lambda cb, i: (i, cb)),
        out_shape=jax.ShapeDtypeStruct(raw.shape, F32), compiler_params=_params(),
    )(raw, raw, conv_w)


def _conv_b_bwd(raw, dy, conv_w):
    T = raw.shape[0]
    nblk = raw.shape[1] // CONV_LANES
    ts = min(ROW_TILE, T)
    nt = T // ts
    kt = conv_w.shape[0]

    def body(rp, rm, rn, dym, dyn, w_ref, dr_ref, dw_ref):
        i = pl.program_id(1)
        ext = jnp.concatenate([jnp.where(i > 0, rp[...], 0.0), rm[...], rn[...]], axis=0)
        w = [w_ref[j:j + 1, :] for j in range(kt)]
        taps = _taps(ext, kt, SUBLANES, ts + SUBLANES)
        xc_ext = sum(w[kt - 1 - j] * taps[j] for j in range(kt))
        dy_ext = jnp.concatenate([dym[...], jnp.where(i < nt - 1, dyn[...], 0.0)], axis=0)
        dxc_ext = dy_ext * _dsilu(xc_ext)
        dr_ref[...] = sum(w[kt - 1 - j] * _ahead(dxc_ext, j, ts) for j in range(kt)).astype(dr_ref.dtype)
        dxc = dxc_ext[:ts]

        @pl.when(i == 0)
        def _():
            dw_ref[...] = jnp.zeros_like(dw_ref)

        for j in range(kt):
            dw_ref[j:j + 1, :] += jnp.sum(dxc * taps[kt - 1 - j][:ts], axis=0, keepdims=True)

    return pl.pallas_call(
        body, name="conv_b_bwd", grid=(nblk, nt),
        in_specs=_halo_specs(ts, nt, CONV_LANES) + _halo_specs(ts, nt, CONV_LANES, prev=False)
        + [pl.BlockSpec((kt, CONV_LANES), lambda cb, i: (0, cb))],
        out_specs=[pl.BlockSpec((ts, CONV_LANES), lambda cb, i: (i, cb)),
                   pl.BlockSpec((SUBLANES, CONV_LANES), lambda cb, i: (0, cb))],
        out_shape=[jax.ShapeDtypeStruct(raw.shape, BF16), jax.ShapeDtypeStruct((SUBLANES, nblk * CONV_LANES), F32)],
        compiler_params=_params(),
    )(raw, raw, raw, dy, dy, conv_w)


def _split_heads(ref, base, nheads, rows=slice(None)):
    return jnp.stack([ref[rows, base + h * HEAD: base + (h + 1) * HEAD] for h in range(nheads)], axis=0)


def _store_heads(ref, base, x, rows=slice(None), accumulate=False):
    for h in range(x.shape[0]):
        lanes = slice(base + h * HEAD, base + (h + 1) * HEAD)
        if accumulate:
            ref[rows, lanes] += x[h]
        else:
            ref[rows, lanes] = x[h].astype(ref.dtype)


def _gdn_fwd(qkv, proj_zs, alog, dtb, gain, H):
    T = qkv.shape[0]
    C, HW = GDN_CHUNK, H * HEAD
    nc = T // C
    zw = HW + LANES

    def body(qkv_ref, zs_ref, alog_ref, dtb_ref, gain_ref, o_ref, sall_ref, s_scr):
        @pl.when(pl.program_id(0) == 0)
        def _():
            s_scr[...] = jnp.zeros_like(s_scr)

        sall_ref[0] = s_scr[...]
        outs, states = _gdn_step(
            s_scr[...], _split_heads(qkv_ref, 0, H), _split_heads(qkv_ref, HW, H),
            _split_heads(qkv_ref, 2 * HW, H), _split_heads(zs_ref, 0, H), zs_ref[:, HW:HW + LANES],
            alog_ref[...], dtb_ref[...], gain_ref[...])
        _store_heads(o_ref, 0, outs)
        s_scr[...] = states

    row = pl.BlockSpec((1, LANES), lambda i: (0, 0))
    return pl.pallas_call(
        body, name="gdn_fwd", grid=(nc,),
        in_specs=[pl.BlockSpec((C, 3 * HW), lambda i: (i, 0)), pl.BlockSpec((C, zw), lambda i: (i, 0)), row, row, row],
        out_specs=[pl.BlockSpec((C, HW), lambda i: (i, 0)), pl.BlockSpec((1, H, HEAD, HEAD), lambda i: (i, 0, 0, 0))],
        out_shape=[jax.ShapeDtypeStruct((T, HW), BF16), jax.ShapeDtypeStruct((nc, H, HEAD, HEAD), F32)],
        scratch_shapes=[pltpu.VMEM((H, HEAD, HEAD), F32)], compiler_params=_params(),
    )(qkv, proj_zs, alog, dtb, gain)


def _gdn_bwd(qkv, proj_zs, do, s_all, alog, dtb, gain, H):
    T = qkv.shape[0]
    C, HW = GDN_CHUNK, H * HEAD
    nc = T // C
    zw = HW + LANES

    def body(qkv_ref, zs_ref, do_ref, sin_ref, alog_ref, dtb_ref, gain_ref,
             dqkv_ref, dzs_ref, dalog_ref, ddtb_ref, dgain_ref, ds_scr):
        @pl.when(pl.program_id(0) == 0)
        def _():
            ds_scr[...] = jnp.zeros_like(ds_scr)
            dalog_ref[...] = jnp.zeros_like(dalog_ref)
            ddtb_ref[...] = jnp.zeros_like(ddtb_ref)
            dgain_ref[...] = jnp.zeros_like(dgain_ref)

        primals = (sin_ref[0], _split_heads(qkv_ref, 0, H),
                   _split_heads(qkv_ref, HW, H), _split_heads(qkv_ref, 2 * HW, H), _split_heads(zs_ref, 0, H),
                   zs_ref[:, HW:HW + LANES], alog_ref[...], dtb_ref[...], gain_ref[...])
        _, vjp = jax.vjp(_gdn_step, *primals)
        dS, dq, dk, dv, dz, dsmall, dalog, ddtb, dgain = vjp((_split_heads(do_ref, 0, H), ds_scr[...]))
        ds_scr[...] = dS
        _store_heads(dqkv_ref, 0, dq)
        _store_heads(dqkv_ref, HW, dk)
        _store_heads(dqkv_ref, 2 * HW, dv)
        _store_heads(dzs_ref, 0, dz)
        dzs_ref[:, HW:HW + LANES] = dsmall.astype(dzs_ref.dtype)
        dalog_ref[...] += dalog
        ddtb_ref[...] += ddtb
        dgain_ref[...] += dgain

    row = pl.BlockSpec((1, LANES), lambda i: (0, 0))
    rev = lambda i: nc - 1 - i
    return pl.pallas_call(
        body, name="gdn_bwd", grid=(nc,),
        in_specs=[pl.BlockSpec((C, 3 * HW), lambda i: (rev(i), 0)), pl.BlockSpec((C, zw), lambda i: (rev(i), 0)),
                  pl.BlockSpec((C, HW), lambda i: (rev(i), 0)),
                  pl.BlockSpec((1, H, HEAD, HEAD), lambda i: (rev(i), 0, 0, 0)), row, row, row],
        out_specs=[pl.BlockSpec((C, 3 * HW), lambda i: (rev(i), 0)), pl.BlockSpec((C, zw), lambda i: (rev(i), 0)),
                   row, row, row],
        out_shape=[jax.ShapeDtypeStruct(qkv.shape, F32), jax.ShapeDtypeStruct(proj_zs.shape, BF16)]
        + [jax.ShapeDtypeStruct((1, LANES), F32)] * 3,
        scratch_shapes=[pltpu.VMEM((H, HEAD, HEAD), F32)], compiler_params=_params(),
    )(qkv, proj_zs, do, s_all, alog, dtb, gain)


def _hgrn_refs(proj_ref, lb_ref, HP):
    W = HP * HEAD
    return (_split_heads(proj_ref, 0, HP), _split_heads(proj_ref, W, HP), _split_heads(proj_ref, 2 * W, HP),
            _split_heads(proj_ref, 3 * W, HP), _split_heads(lb_ref, 0, HP, slice(0, 1)),
            _split_heads(lb_ref, 0, HP, slice(1, 2)))


def _hgrn_fwd(proj, lower_bounds, gain, nheads):
    T = proj.shape[0]
    C, HP = HGRN_CHUNK, HGRN_HEADS_PER_STEP
    ng, nc, W = nheads // HP, T // C, HP * HEAD

    def body(proj_ref, lb_ref, gain_ref, o_ref, sall_ref, s_scr):
        @pl.when(pl.program_id(1) == 0)
        def _():
            s_scr[...] = jnp.zeros_like(s_scr)

        sall_ref[0] = s_scr[...]
        qr, fr, vi, z, lb0, lb1 = _hgrn_refs(proj_ref, lb_ref, HP)
        outs, states = _hgrn_step(s_scr[...], qr, fr, vi, z, lb0, lb1, gain_ref[...])
        _store_heads(o_ref, 0, outs)
        s_scr[...] = states

    return pl.pallas_call(
        body, name="hgrn_fwd", grid=(ng, nc),
        in_specs=[pl.BlockSpec((C, 4 * W), lambda g, i: (i, g)), pl.BlockSpec((2, W), lambda g, i: (0, g)),
                  pl.BlockSpec((1, LANES), lambda g, i: (0, 0))],
        out_specs=[pl.BlockSpec((C, W), lambda g, i: (i, g)),
                   pl.BlockSpec((1, HP, HEAD, HEAD), lambda g, i: (i, g, 0, 0))],
        out_shape=[jax.ShapeDtypeStruct((T, nheads * HEAD), BF16), jax.ShapeDtypeStruct((nc, nheads, HEAD, HEAD), F32)],
        scratch_shapes=[pltpu.VMEM((HP, HEAD, HEAD), F32)], compiler_params=_params(),
    )(proj, lower_bounds, gain)


def _hgrn_bwd(proj, do, s_all, lower_bounds, gain, nheads):
    T = proj.shape[0]
    C, HP = HGRN_CHUNK, HGRN_HEADS_PER_STEP
    ng, nc, W = nheads // HP, T // C, HP * HEAD

    def body(proj_ref, do_ref, sin_ref, lb_ref, gain_ref, dproj_ref, dlb_ref, dgain_ref, ds_scr):
        first = pl.program_id(1) == 0

        @pl.when(first)
        def _():
            ds_scr[...] = jnp.zeros_like(ds_scr)
            dlb_ref[...] = jnp.zeros_like(dlb_ref)

        @pl.when(first & (pl.program_id(0) == 0))
        def _():
            dgain_ref[...] = jnp.zeros_like(dgain_ref)

        qr, fr, vi, z, lb0, lb1 = _hgrn_refs(proj_ref, lb_ref, HP)
        primals = (sin_ref[0], qr, fr, vi, z, lb0, lb1, gain_ref[...])
        _, vjp = jax.vjp(_hgrn_step, *primals)
        dS, dq, df, dv, dz, dlb0, dlb1, dgain = vjp((_split_heads(do_ref, 0, HP), ds_scr[...]))
        ds_scr[...] = dS
        for part, d in enumerate((dq, df, dv, dz)):
            _store_heads(dproj_ref, part * W, d)
        _store_heads(dlb_ref, 0, dlb0, slice(0, 1), accumulate=True)
        _store_heads(dlb_ref, 0, dlb1, slice(1, 2), accumulate=True)
        dgain_ref[...] += dgain

    rev = lambda i: nc - 1 - i
    return pl.pallas_call(
        body, name="hgrn_bwd", grid=(ng, nc),
        in_specs=[pl.BlockSpec((C, 4 * W), lambda g, i: (rev(i), g)), pl.BlockSpec((C, W), lambda g, i: (rev(i), g)),
                  pl.BlockSpec((1, HP, HEAD, HEAD), lambda g, i: (rev(i), g, 0, 0)),
                  pl.BlockSpec((2, W), lambda g, i: (0, g)), pl.BlockSpec((1, LANES), lambda g, i: (0, 0))],
        out_specs=[pl.BlockSpec((C, 4 * W), lambda g, i: (rev(i), g)), pl.BlockSpec((2, W), lambda g, i: (0, g)),
                   pl.BlockSpec((1, LANES), lambda g, i: (0, 0))],
        out_shape=[jax.ShapeDtypeStruct(proj.shape, BF16), jax.ShapeDtypeStruct(lower_bounds.shape, F32),
                   jax.ShapeDtypeStruct((1, LANES), F32)],
        scratch_shapes=[pltpu.VMEM((HP, HEAD, HEAD), F32)], compiler_params=_params(),
    )(proj, do, s_all, lower_bounds, gain)


def _post_specs(T):
    tr = min(POST_TILE, T)
    tile = lambda w: pl.BlockSpec((tr, w), lambda i: (i, 0))
    full = lambda r, w: pl.BlockSpec((r, w), lambda i: (0, 0))
    return tr, tile, full


def _post_fwd(s, x, p, g, b, wg, wpl, name):
    T, D = x.shape
    P = p.shape[1]
    tr, tile, full = _post_specs(T)

    def body(s_ref, x_ref, p_ref, g_ref, b_ref, wg_ref, wpl_ref, o_ref, o16_ref):
        x1 = _post_norm(s_ref[...], x_ref[...], g_ref[...], b_ref[...])
        xn = _post_gate(x1, _dot_raw(x1, wg_ref[...], "nn", False), _dot_raw(p_ref[...], wpl_ref[...], "nn", False))
        o_ref[...] = xn
        o16_ref[...] = xn.astype(BF16)

    return pl.pallas_call(
        body, name=name, grid=(T // tr,),
        in_specs=[tile(D), tile(D), tile(P), full(1, D), full(1, D), full(D, D), full(P, D)],
        out_specs=[tile(D), tile(D)],
        out_shape=[jax.ShapeDtypeStruct((T, D), F32), jax.ShapeDtypeStruct((T, D), BF16)], compiler_params=_params(),
    )(s, x, p, g, b, wg, wpl)


def _post_bwd(s, x, p, g, b, wg, wpl, dnext, name, with_loss):
    T, D = x.shape
    P = p.shape[1]
    tr, tile, full = _post_specs(T)

    def body(s_ref, x_ref, p_ref, g_ref, b_ref, wg_ref, wpl_ref, dn_ref,
             ds_ref, dx_ref, dg_ref, db_ref, dwg_ref, dwpl_ref, loss_ref):
        @pl.when(pl.program_id(0) == 0)
        def _():
            for r in (dg_ref, db_ref, dwg_ref, dwpl_ref, loss_ref):
                r[...] = jnp.zeros_like(r)

        x1, vjp_norm = jax.vjp(_post_norm, s_ref[...], x_ref[...], g_ref[...], b_ref[...])
        gate_pre = _dot_raw(x1, wg_ref[...], "nn", False)
        pp = _dot_raw(p_ref[...], wpl_ref[...], "nn", False)
        xn, vjp_gate = jax.vjp(_post_gate, x1, gate_pre, pp)
        if with_loss:
            err = xn - dn_ref[...]
            loss_ref[...] += 0.5 * jnp.sum(jnp.sum(err * err, axis=-1, keepdims=True), axis=0, keepdims=True) / D
            dn = err / D
        else:
            dn = dn_ref[...]
        dx1, dgp, dpp = vjp_gate(dn)
        dwg_ref[...] += _dot_raw(x1, dgp, "tn", False)
        dwpl_ref[...] += _dot_raw(p_ref[...], dpp, "tn", False)
        dx1 = dx1 + _dot_raw(dgp, wg_ref[...], "nt", False)
        ds, dx, dg, db = vjp_norm(dx1)
        ds_ref[...] = ds.astype(ds_ref.dtype)
        dx_ref[...] = dx
        dg_ref[...] += dg
        db_ref[...] += db

    return pl.pallas_call(
        body, name=name, grid=(T // tr,),
        in_specs=[tile(D), tile(D), tile(P), full(1, D), full(1, D), full(D, D), full(P, D), tile(D)],
        out_specs=[tile(D), tile(D), full(1, D), full(1, D), full(D, D), full(P, D), full(SUBLANES, LANES)],
        out_shape=[jax.ShapeDtypeStruct((T, D), BF16), jax.ShapeDtypeStruct((T, D), F32)]
        + [jax.ShapeDtypeStruct((1, D), F32)] * 2
        + [jax.ShapeDtypeStruct((D, D), F32), jax.ShapeDtypeStruct((P, D), F32),
           jax.ShapeDtypeStruct((SUBLANES, LANES), F32)],
        compiler_params=_params(),
    )(s, x, p, g, b, wg, wpl, dnext)


def _adam_math(w, g, m, v):
    m = ADAM_B1 * m + (1.0 - ADAM_B1) * g
    v = ADAM_B2 * v + (1.0 - ADAM_B2) * (g * g)
    m_hat = m / (1.0 - ADAM_B1 ** ADAM_STEP)
    v_hat = v / (1.0 - ADAM_B2 ** ADAM_STEP)
    return -ADAM_LR * (m_hat / (jnp.sqrt(v_hat) + ADAM_EPS) + ADAM_WD * w), m, v


def _rs_add(g8, got, cidx, name):
    _, R, C = g8.shape
    tr = _pick(R, (256, 128, 64, 32, 16, 8))

    def body(c_ref, a_ref, b_ref, o_ref, o16_ref):
        total = a_ref[...] + b_ref[...]
        o_ref[...] = total
        o16_ref[...] = total.astype(BF16)

    out_spec = pl.BlockSpec((1, tr, C), lambda q, i, c: (q, i, 0))
    return pl.pallas_call(
        body, name=name,
        grid_spec=pltpu.PrefetchScalarGridSpec(
            num_scalar_prefetch=1, grid=(4, R // tr),
            in_specs=[pl.BlockSpec((1, tr, C), lambda q, i, c: (2 * q + c[0], i, 0)),
                      pl.BlockSpec((1, tr, C), lambda q, i, c: (q, i, 0))],
            out_specs=[out_spec, out_spec]),
        out_shape=[jax.ShapeDtypeStruct((4,) + g8.shape[1:], F32), jax.ShapeDtypeStruct((4,) + g8.shape[1:], BF16)],
        compiler_params=_params(),
    )(cidx, g8, got)


def _adam_sharded(w, m, v, mine, got, qidx, name):
    R, C = w.shape
    tr = _pick(R, (256, 128, 64, 32, 16, 8))

    def body(q_ref, w_ref, m_ref, v_ref, p_ref, r0, r1, r2, g_ref, d_ref, mo_ref, vo_ref):
        g = ((p_ref[0] + r0[0].astype(F32)) + r1[0].astype(F32)) + r2[0].astype(F32)
        d, mn, vn = _adam_math(w_ref[...], g, m_ref[...], v_ref[...])
        g_ref[...] = g
        d_ref[...] = d
        mo_ref[...] = mn
        vo_ref[...] = vn

    t2 = pl.BlockSpec((tr, C), lambda i, q: (i, 0))
    slot = lambda k: pl.BlockSpec((1, tr, C), lambda i, q: (k, i, 0))
    return pl.pallas_call(
        body, name=name,
        grid_spec=pltpu.PrefetchScalarGridSpec(
            num_scalar_prefetch=1, grid=(R // tr,),
            in_specs=[t2, t2, t2, pl.BlockSpec((1, tr, C), lambda i, q: (q[0], i, 0)), slot(0), slot(1), slot(2)],
            out_specs=[t2, t2, t2, t2]),
        out_shape=[jax.ShapeDtypeStruct((R, C), F32)] * 4, compiler_params=_params(),
    )(qidx, w, m, v, mine, got, got, got)


def _adam_replicated(w, m, v, g8):
    def body(w_ref, m_ref, v_ref, g_ref, go_ref, d_ref, mo_ref, vo_ref):
        g = g_ref[0]
        for k in range(1, 8):
            g = g + g_ref[k]
        d, mn, vn = _adam_math(w_ref[...], g, m_ref[...], v_ref[...])
        go_ref[...] = g
        d_ref[...] = d
        mo_ref[...] = mn
        vo_ref[...] = vn

    return pl.pallas_call(
        body, name="adam_replicated", out_shape=[jax.ShapeDtypeStruct(w.shape, F32)] * 4, compiler_params=_params(),
    )(w, m, v, g8)


def _place():
    return lax.axis_index("x"), lax.axis_index("y"), lax.axis_index("c")


def _all_gather(shard, name):
    def body(x_ref, out_ref, send_sems, recv_sems, local_sem):
        x, y, c = _place()
        me, sibling = (x, y, c), (x, y, 1 - c)
        chips = [(1 - x, y), (x, 1 - y), (1 - x, 1 - y)]

        def slab(px, py, pc):
            return out_ref.at[4 * px + 2 * py + pc]

        def copy(k, block, to, src=None):
            return pltpu.make_async_remote_copy(
                src_ref=slab(*block) if src is None else src, dst_ref=slab(*block),
                send_sem=send_sems.at[k], recv_sem=recv_sems.at[k], device_id=to, device_id_type=MESH)

        mine = pltpu.make_async_copy(x_ref, slab(*me), local_sem)
        mine.start()
        first = [copy(0, me, sibling, src=x_ref)]
        first += [copy(1 + j, me, (*chip, c), src=x_ref) for j, chip in enumerate(chips)]
        for cp in first:
            cp.start()
        passed = [copy(4 + j, (*chip, c), sibling) for j, chip in enumerate(chips)]
        for j, chip in enumerate(chips):
            copy(1 + j, (*chip, c), me).wait_recv()
            passed[j].start()
        copy(0, sibling, me).wait_recv()
        for j, chip in enumerate(chips):
            copy(4 + j, (*chip, 1 - c), me).wait_recv()
        for cp in first + passed:
            cp.wait_send()
        mine.wait()

    return pl.pallas_call(
        body, name=name, out_shape=jax.ShapeDtypeStruct((8,) + shard.shape, shard.dtype),
        in_specs=[pl.BlockSpec(memory_space=pl.ANY)], out_specs=pl.BlockSpec(memory_space=pl.ANY),
        scratch_shapes=[pltpu.SemaphoreType.DMA((7,)), pltpu.SemaphoreType.DMA((7,)), pltpu.SemaphoreType.DMA],
    )(shard)


def _rs_to_sibling(g8, name):
    def body(g_ref, out_ref, send_sems, recv_sems):
        x, y, c = _place()
        copies = [pltpu.make_async_remote_copy(
            src_ref=g_ref.at[2 * q + (1 - c)], dst_ref=out_ref.at[q], send_sem=send_sems.at[q],
            recv_sem=recv_sems.at[q], device_id=(x, y, 1 - c), device_id_type=MESH) for q in range(4)]
        for cp in copies:
            cp.start()
        for cp in copies:
            cp.wait()

    return pl.pallas_call(
        body, name=name, out_shape=jax.ShapeDtypeStruct((4,) + g8.shape[1:], g8.dtype),
        in_specs=[pl.BlockSpec(memory_space=pl.ANY)], out_specs=pl.BlockSpec(memory_space=pl.ANY),
        scratch_shapes=[pltpu.SemaphoreType.DMA((4,)), pltpu.SemaphoreType.DMA((4,))],
    )(g8)


def _rs_to_chips(p4, name):
    def body(p_ref, out_ref, send_sems, recv_sems):
        x, y, c = _place()
        chips = [(1 - x, y), (x, 1 - y), (1 - x, 1 - y)]
        copies = [pltpu.make_async_remote_copy(
            src_ref=p_ref.at[2 * px + py], dst_ref=out_ref.at[j], send_sem=send_sems.at[j],
            recv_sem=recv_sems.at[j], device_id=(px, py, c), device_id_type=MESH) for j, (px, py) in enumerate(chips)]
        for cp in copies:
            cp.start()
        for cp in copies:
            cp.wait()

    return pl.pallas_call(
        body, name=name, out_shape=jax.ShapeDtypeStruct((3,) + p4.shape[1:], p4.dtype),
        in_specs=[pl.BlockSpec(memory_space=pl.ANY)], out_specs=pl.BlockSpec(memory_space=pl.ANY),
        scratch_shapes=[pltpu.SemaphoreType.DMA((3,)), pltpu.SemaphoreType.DMA((3,))],
    )(p4)


def _reduce_scatter_adam(g8, w, m, v, cidx, qidx, tag):
    got = _rs_to_sibling(g8, "rs_sibling_" + tag)
    chip_sums, chip_sums16 = _rs_add(g8, got, cidx, "rs_add_" + tag)
    got2 = _rs_to_chips(chip_sums16, "rs_chips_" + tag)
    return _adam_sharded(w, m, v, chip_sums, got2, qidx, "adam_" + tag)


def _local_grads(x, p0, p1, target, w_zs, w_a, w_qkv, wino, woute_a, woute_b, wouto, wg, wpl, conv_a, conv_b,
                 a_log, dt_bias, gdn_gain, lower_bounds, hgrn_gain, ln_g, ln_b):
    H = a_log.shape[1]
    nheads_o = wouto.shape[0] // HEAD
    pad_small = ((0, 0), (H, LANES - 2 * H))
    alog_row = jnp.pad(a_log, pad_small)
    dtb_row = jnp.pad(dt_bias, pad_small)

    x16 = x.astype(BF16)
    proj_zs = _matmul(x16, w_zs, "nn", "proj_even_zs")
    proj_a = _matmul(x16, w_a, "nn", "proj_even_a")
    proj_qkv = _matmul(x16, w_qkv, "nn", "proj_even_qkv")
    y_a = _mixer_a_fwd(proj_a, conv_a)
    qkv = _conv_b_fwd(proj_qkv, conv_b)
    o2, s_gdn = _gdn_fwd(qkv, proj_zs, alog_row, dtb_row, gdn_gain, H)
    s_e = _matmul(o2, woute_b, "nn", "out_even_b", add=_matmul(y_a, woute_a, "nn", "out_even_a"))
    x2, x2_16 = _post_fwd(s_e, x, p0, ln_g[0:1], ln_b[0:1], wg[0], wpl[0], "post_even_fwd")
    proj_o = _matmul(x2_16, wino, "nn", "proj_odd")
    o4, s_hgrn = _hgrn_fwd(proj_o, lower_bounds, hgrn_gain, nheads_o)
    s_o = _matmul(o4, wouto, "nn", "out_odd")
    ds_o, dx2, dlng1, dlnb1, dwg1, dwpl1, loss = _post_bwd(
        s_o, x2, p1, ln_g[1:2], ln_b[1:2], wg[1], wpl[1], target, "post_odd_loss_bwd", True)
    do4 = _matmul(ds_o, wouto, "nt", "d_out_odd_act")
    dwouto = _matmul(o4, ds_o, "tn", "d_out_odd_w")
    dproj_o, dlb, dhgain = _hgrn_bwd(proj_o, do4, s_hgrn, lower_bounds, hgrn_gain, nheads_o)
    dx2 = _matmul(dproj_o, wino, "nt", "d_proj_odd_act", add=dx2)
    dwino = _matmul(x2_16, dproj_o, "tn", "d_proj_odd_w")
    ds_e, dx, dlng0, dlnb0, dwg0, dwpl0, _ = _post_bwd(
        s_e, x, p0, ln_g[0:1], ln_b[0:1], wg[0], wpl[0], dx2, "post_even_bwd", False)
    dy_a = _matmul(ds_e, woute_a, "nt", "d_out_even_a_act")
    do2 = _matmul(ds_e, woute_b, "nt", "d_out_even_b_act")
    dwoute_a = _matmul(y_a, ds_e, "tn", "d_out_even_a_w")
    dwoute_b = _matmul(o2, ds_e, "tn", "d_out_even_b_w")
    dqkv, dproj_zs, dalog, ddtb, dggain = _gdn_bwd(qkv, proj_zs, do2, s_gdn, alog_row, dtb_row, gdn_gain, H)
    dproj_qkv, dconv_b = _conv_b_bwd(proj_qkv, dqkv, conv_b)
    dproj_a, dconv_a = _mixer_a_bwd(proj_a, dy_a, conv_a)
    dx = _matmul(dproj_zs, w_zs, "nt", "d_proj_even_zs_act", add=dx)
    dx = _matmul(dproj_a, w_a, "nt", "d_proj_even_a_act", add=dx)
    dx = _matmul(dproj_qkv, w_qkv, "nt", "d_proj_even_qkv_act", add=dx)
    dw_zs = _matmul(x16, dproj_zs, "tn", "d_proj_even_zs_w")
    dw_a = _matmul(x16, dproj_a, "tn", "d_proj_even_a_w")
    dw_qkv = _matmul(x16, dproj_qkv, "tn", "d_proj_even_qkv_w")
    return dict(
        loss=loss[0, 0], grad_x=dx, w_zs=dw_zs, w_a=dw_a, w_qkv=dw_qkv, wino=dwino,
        woute=jnp.concatenate([dwoute_a, dwoute_b], axis=0), wouto=dwouto,
        wg=jnp.stack([dwg0, dwg1]), wpl=jnp.stack([dwpl0, dwpl1]),
        conv_a=dconv_a[:conv_a.shape[0]], conv_b=dconv_b[:conv_b.shape[0]],
        a_log=dalog[:, H:2 * H], dt_bias=ddtb[:, H:2 * H], gdn_gain=dggain, lower_bounds=dlb, hgrn_gain=dhgain,
        ln_g=jnp.concatenate([dlng0, dlng1], axis=0), ln_b=jnp.concatenate([dlnb0, dlnb1], axis=0))


def _pad_rows(a, rows):
    return jnp.pad(a, ((0, rows - a.shape[0]), (0, 0)))


def _pack_small(a_log, dt_bias, gdn_gain, lower_bounds, hgrn_gain, ln_g, ln_b):
    lane_pad = lambda a: _pad_rows(jnp.pad(a, ((0, 0), (0, LANES - a.shape[1]))), SUBLANES)
    parts = [lane_pad(a_log), lane_pad(dt_bias), lane_pad(gdn_gain), lower_bounds.reshape(-1, LANES),
             lane_pad(hgrn_gain), ln_g.reshape(-1, LANES), ln_b.reshape(-1, LANES)]
    packed = jnp.concatenate(parts, axis=0)
    assert packed.shape[0] == SMALL_ROWS, packed.shape
    return packed


def _unpack_small(packed, shapes):
    out, r = [], 0
    for shp in shapes:
        n = shp[0] * shp[1]
        if n < LANES * SUBLANES and shp[1] <= LANES:
            out.append(packed[r:r + shp[0], :shp[1]])
            r += SUBLANES
        else:
            rows = n // LANES
            out.append(packed[r:r + rows].reshape(shp))
            r += rows
    return out


def _split_in_even(w_full, AW, HW, H):
    D = w_full.shape[0]
    n_a = 4 * AW
    n_main = n_a + 3 * HW
    w_zs = jnp.concatenate([w_full[:, n_main:n_main + HW], w_full[:, n_main + HW:],
                            jnp.zeros((D, LANES - 2 * H), w_full.dtype)], axis=1)
    w_a = w_full[:, :n_a].reshape(D, 4, AW // MIXER_LANES, MIXER_LANES).transpose(0, 2, 1, 3).reshape(D, n_a)
    return w_zs, w_a, w_full[:, n_a:n_main]


def _join_in_even(d_zs, d_a, d_qkv, AW, HW, H):
    D = d_a.shape[0]
    a_nat = d_a.reshape(D, AW // MIXER_LANES, 4, MIXER_LANES).transpose(0, 2, 1, 3).reshape(D, 4 * AW)
    return jnp.concatenate([a_nat, d_qkv, d_zs[:, :HW], d_zs[:, HW:HW + 2 * H]], axis=1)


def kernel(x, p, w_in_even, conv_a_w, conv_b_w, a_log, dt_bias, gdn_norm_g, w_out_even, w_in_odd, lower_bounds, hgrn_norm_g, w_out_odd, ln_g, ln_b, w_pl, w_pl_gate, loss_target, m_w_in_even, m_conv_a_w, m_conv_b_w, m_a_log, m_dt_bias, m_gdn_norm_g, m_w_out_even, m_w_in_odd, m_lower_bounds, m_hgrn_norm_g, m_w_out_odd, m_ln_g, m_ln_b, m_w_pl, m_w_pl_gate, v_w_in_even, v_conv_a_w, v_conv_b_w, v_a_log, v_dt_bias, v_gdn_norm_g, v_w_out_even, v_w_in_odd, v_lower_bounds, v_hgrn_norm_g, v_w_out_odd, v_ln_g, v_ln_b, v_w_pl, v_w_pl_gate):
    xi, yi, ci = _place()
    cidx = jnp.reshape(ci, (1,)).astype(jnp.int32)
    qidx = jnp.reshape(2 * xi + yi, (1,)).astype(jnp.int32)
    D = x.shape[2]
    H = a_log.shape[1]
    HW = H * HEAD
    AW = conv_a_w.shape[2] * 8
    OW = w_out_odd.shape[1] * 8
    PD = w_pl.shape[1]
    ka, kb = conv_a_w.shape[1], conv_b_w.shape[1]
    ca, cb = conv_a_w.shape[2], conv_b_w.shape[2]
    gw = HGRN_HEADS_PER_STEP * HEAD
    ngrp = OW // gw

    g_ine = _all_gather(w_in_even[0].astype(BF16), "ag_w_in_even")
    w_zs, w_a, w_qkv = _split_in_even(jnp.transpose(g_ine, (1, 0, 2)).reshape(D, -1), AW, HW, H)
    g_ino = _all_gather(w_in_odd[0].astype(BF16), "ag_w_in_odd")
    wino = jnp.transpose(g_ino, (1, 0, 2)).reshape(D, 4, ngrp, gw).transpose(0, 2, 1, 3).reshape(D, 4 * OW)
    woute = _all_gather(w_out_even[0].astype(BF16), "ag_w_out_even").reshape(-1, D)
    wouto = _all_gather(w_out_odd[0].astype(BF16), "ag_w_out_odd").reshape(-1, D)
    g_gate = _all_gather(w_pl_gate.astype(BF16).reshape(-1, D), "ag_w_pl_gate")
    wg = g_gate.reshape(8, DEPTH, D // 8, D).transpose(1, 0, 2, 3).reshape(DEPTH, D, D)
    g_pl = _all_gather(w_pl.astype(BF16).reshape(DEPTH * PD, -1), "ag_w_pl")
    wpl = g_pl.reshape(8, DEPTH, PD, D // 8).transpose(1, 2, 0, 3).reshape(DEPTH, PD, D)
    taps = jnp.concatenate([_pad_rows(conv_a_w[0], SUBLANES), _pad_rows(conv_b_w[0], SUBLANES)], axis=1)
    g_taps = _all_gather(taps, "ag_conv")
    conv_a = jnp.transpose(g_taps[:, :ka, :ca], (1, 0, 2)).reshape(ka, 8 * ca)
    conv_b = jnp.transpose(g_taps[:, :kb, ca:], (1, 0, 2)).reshape(kb, 8 * cb)

    gr = _local_grads(x[0], p[0, 0], p[1, 0], loss_target[0], w_zs, w_a, w_qkv, wino, woute[:AW], woute[AW:], wouto,
                      wg, wpl, conv_a, conv_b, a_log, dt_bias, gdn_norm_g, lower_bounds, hgrn_norm_g, ln_g, ln_b)
    loss = lax.psum(gr["loss"], AXES)

    dw_nat = _join_in_even(gr["w_zs"], gr["w_a"], gr["w_qkv"], AW, HW, H)
    sh = w_in_even.shape[2]
    o_ine = _reduce_scatter_adam(dw_nat.reshape(D, 8, sh).transpose(1, 0, 2), w_in_even[0], m_w_in_even[0],
                                 v_w_in_even[0], cidx, qidx, "w_in_even")
    dwo = gr["wino"].reshape(D, ngrp, 4, gw).transpose(0, 2, 1, 3).reshape(D, 8, 4 * OW // 8).transpose(1, 0, 2)
    o_ino = _reduce_scatter_adam(dwo, w_in_odd[0], m_w_in_odd[0], v_w_in_odd[0], cidx, qidx, "w_in_odd")
    o_oute = _reduce_scatter_adam(gr["woute"].reshape(8, -1, D), w_out_even[0], m_w_out_even[0], v_w_out_even[0],
                                  cidx, qidx, "w_out_even")
    o_outo = _reduce_scatter_adam(gr["wouto"].reshape(8, -1, D), w_out_odd[0], m_w_out_odd[0], v_w_out_odd[0],
                                  cidx, qidx, "w_out_odd")
    dgate = gr["wg"].reshape(DEPTH, 8, D // 8, D).transpose(1, 0, 2, 3).reshape(8, DEPTH * D // 8, D)
    flat_gate = lambda a: a.reshape(DEPTH * D // 8, D)
    o_gate = _reduce_scatter_adam(dgate, flat_gate(w_pl_gate), flat_gate(m_w_pl_gate), flat_gate(v_w_pl_gate),
                                  cidx, qidx, "w_pl_gate")
    dpl = gr["wpl"].reshape(DEPTH, PD, 8, D // 8).transpose(2, 0, 1, 3).reshape(8, DEPTH * PD, D // 8)
    flat_pl = lambda a: a.reshape(DEPTH * PD, D // 8)
    o_pl = _reduce_scatter_adam(dpl, flat_pl(w_pl), flat_pl(m_w_pl), flat_pl(v_w_pl), cidx, qidx, "w_pl")
    dtaps = jnp.concatenate([_pad_rows(gr["conv_a"], SUBLANES).reshape(SUBLANES, 8, ca).transpose(1, 0, 2),
                             _pad_rows(gr["conv_b"], SUBLANES).reshape(SUBLANES, 8, cb).transpose(1, 0, 2)], axis=2)
    pack_taps = lambda a, b: jnp.concatenate([_pad_rows(a[0], SUBLANES), _pad_rows(b[0], SUBLANES)], axis=1)
    o_taps = _reduce_scatter_adam(dtaps, taps, pack_taps(m_conv_a_w, m_conv_b_w), pack_taps(v_conv_a_w, v_conv_b_w),
                                  cidx, qidx, "conv")

    small_g = _pack_small(gr["a_log"], gr["dt_bias"], gr["gdn_gain"], gr["lower_bounds"], gr["hgrn_gain"],
                          gr["ln_g"], gr["ln_b"])
    o_small = _adam_replicated(
        _pack_small(a_log, dt_bias, gdn_norm_g, lower_bounds, hgrn_norm_g, ln_g, ln_b),
        _pack_small(m_a_log, m_dt_bias, m_gdn_norm_g, m_lower_bounds, m_hgrn_norm_g, m_ln_g, m_ln_b),
        _pack_small(v_a_log, v_dt_bias, v_gdn_norm_g, v_lower_bounds, v_hgrn_norm_g, v_ln_g, v_ln_b),
        _all_gather(small_g, "ag_small_grads"))
    small_shapes = [a_log.shape, dt_bias.shape, gdn_norm_g.shape, lower_bounds.shape, hgrn_norm_g.shape,
                    ln_g.shape, ln_b.shape]

    def leaves(kind):
        s_alog, s_dt, s_gg, s_lb, s_hg, s_lng, s_lnb = _unpack_small(o_small[kind], small_shapes)
        t = o_taps[kind]
        return [o_ine[kind][None], t[None, :ka, :ca], t[None, :kb, ca:], s_alog, s_dt, s_gg, o_oute[kind][None],
                o_ino[kind][None], s_lb, s_hg, o_outo[kind][None], s_lng, s_lnb,
                o_pl[kind].reshape(w_pl.shape), o_gate[kind].reshape(w_pl_gate.shape)]

    return (loss, gr["grad_x"][None], *leaves(0), *leaves(1), *leaves(2), *leaves(3))
```

```python
import functools

import jax
import jax.numpy as jnp
from jax import lax
from jax.experimental import pallas as pl
from jax.experimental.pallas import tpu as pltpu

F32 = jnp.float32
BF16 = jnp.bfloat16
MESH = pl.DeviceIdType.MESH
AXES = ("x", "y", "c")

LANES = 128
SUBLANES = 8
HEAD = 128
GDN_CHUNK = 64
HGRN_CHUNK = 32
HGRN_SUB = 16
HGRN_HEADS_PER_STEP = 16
NORM_EPS = 1e-5
DEPTH = 2
ALPHA = (2.0 * DEPTH) ** 0.25
EXP_CLAMP = 80.0
ADAM_LR, ADAM_B1, ADAM_B2, ADAM_EPS, ADAM_WD, ADAM_STEP = 0.001, 0.9, 0.999, 1e-08, 0.01, 10
VMEM_LIMIT = 56 * 1024 * 1024
MATMUL_VMEM = 36 * 1024 * 1024
ROW_TILE = 512
MIXER_LANES = 256
CONV_LANES = 512
POST_TILE = 256
SMALL_ROWS = 96

_NOBATCH, _BATCH0 = ((), ()), ((0,), (0,))
_DIMS = {"nn": (((1,), (0,)), _NOBATCH), "nt": (((1,), (1,)), _NOBATCH), "tn": (((0,), (0,)), _NOBATCH),
         "bnn": (((2,), (1,)), _BATCH0), "bnt": (((2,), (2,)), _BATCH0), "btn": (((1,), (1,)), _BATCH0)}


def _params(**kw):
    return pltpu.CompilerParams(vmem_limit_bytes=VMEM_LIMIT, **kw)


def _dot_raw(a, b, kind, hi):
    if hi:
        return lax.dot_general(a, b, _DIMS[kind], precision=lax.Precision.HIGHEST, preferred_element_type=F32)
    return lax.dot_general(a.astype(BF16), b.astype(BF16), _DIMS[kind], preferred_element_type=F32)


@functools.partial(jax.custom_vjp, nondiff_argnums=(2, 3))
def mdot(a, b, kind, hi):
    return _dot_raw(a, b, kind, hi)


def _mdot_fwd(a, b, kind, hi):
    return _dot_raw(a, b, kind, hi), (a, b)


def _mdot_bwd(kind, hi, res, g):
    a, b = res
    pre, base = kind[:-2], kind[-2:]
    if base == "nn":
        return _dot_raw(g, b, pre + "nt", hi), _dot_raw(a, g, pre + "tn", hi)
    if base == "nt":
        return _dot_raw(g, b, pre + "nn", hi), _dot_raw(g, a, pre + "tn", hi)
    return _dot_raw(b, g, pre + "nt", hi), _dot_raw(a, g, pre + "nn", hi)


mdot.defvjp(_mdot_fwd, _mdot_bwd)


def _rows(x, lo, hi):
    return _take_rows(x, lo, hi, x.shape[-2])


@functools.partial(jax.custom_vjp, nondiff_argnums=(1, 2, 3))
def _take_rows(x, lo, hi, n):
    return x[..., lo:hi, :]


def _take_rows_fwd(x, lo, hi, n):
    return x[..., lo:hi, :], None


def _take_rows_bwd(lo, hi, n, _, g):
    parts = []
    if lo > 0:
        parts.append(jnp.zeros(g.shape[:-2] + (lo, g.shape[-1]), g.dtype))
    parts.append(g)
    if n - hi > 0:
        parts.append(jnp.zeros(g.shape[:-2] + (n - hi, g.shape[-1]), g.dtype))
    return (jnp.concatenate(parts, axis=-2) if len(parts) > 1 else g,)


_take_rows.defvjp(_take_rows_fwd, _take_rows_bwd)


def _heads_of(wide, nheads):
    return jnp.stack([wide[:, h * HEAD:(h + 1) * HEAD] for h in range(nheads)], axis=0)


def _wide_of(x):
    return jnp.concatenate([x[h] for h in range(x.shape[0])], axis=1)


@functools.partial(jax.custom_vjp, nondiff_argnums=(1,))
def to_heads(wide, nheads):
    return _heads_of(wide, nheads)


to_heads.defvjp(lambda wide, nheads: (_heads_of(wide, nheads), None), lambda nheads, _, g: (_wide_of(g),))


@jax.custom_vjp
def to_wide(x):
    return _wide_of(x)


to_wide.defvjp(lambda x: (_wide_of(x), None), lambda _, g: (_heads_of(g, g.shape[1] // HEAD),))


def _sigmoid(x):
    return jax.nn.sigmoid(x)


def _silu(x):
    return x * _sigmoid(x)


def _dsilu(x):
    s = _sigmoid(x)
    return s * (1.0 + x * (1.0 - s))


def _log1p(u):
    return jnp.where(u < 1e-4, u * (1.0 - 0.5 * u), jnp.log(1.0 + u))


def _softplus(x):
    return jnp.maximum(x, 0.0) + _log1p(jnp.exp(-jnp.abs(x)))


def _rms_gate(o, gain, z):
    return o * lax.rsqrt(jnp.mean(o * o, axis=-1, keepdims=True) + NORM_EPS) * gain * _silu(z)


def _l2n(x):
    return x * lax.rsqrt(jnp.sum(x * x, axis=-1, keepdims=True) + 1e-6)


def _split_dot_raw(m, x, kind):
    mb = m.astype(BF16)
    hi = x.astype(BF16)
    lo = (x - hi.astype(F32)).astype(BF16)
    dims = _DIMS[kind]
    return (lax.dot_general(mb, hi, dims, preferred_element_type=F32)
            + lax.dot_general(mb, lo, dims, preferred_element_type=F32))


@jax.custom_vjp
def mask_dot(m, x):
    return _split_dot_raw(m, x, "nn")


def _mask_dot_fwd(m, x):
    return _split_dot_raw(m, x, "nn"), m


def _mask_dot_bwd(m, g):
    return jnp.zeros_like(m), _split_dot_raw(m, g, "tn")


mask_dot.defvjp(_mask_dot_fwd, _mask_dot_bwd)


def _unit_lower_inverse_minus_eye(low, n):
    rest = -low
    power = low
    span = 2
    while span < n:
        power = mdot(power, power, "bnn", False)
        rest = rest + power + mdot(rest, power, "bnn", False)
        span *= 2
    return rest


def _gdn_step(S, q, k, v, z, small, alog, dtb, gain):
    H = S.shape[0]
    C = GDN_CHUNK
    row = lax.broadcasted_iota(jnp.int32, (C, C), 0)
    col = lax.broadcasted_iota(jnp.int32, (C, C), 1)
    tril, strict, eye = (row >= col)[None], (row > col)[None], (row == col)[None]
    head = lax.broadcasted_iota(jnp.int32, (H, 1, LANES), 0)
    lane = lax.broadcasted_iota(jnp.int32, (H, 1, LANES), 2)
    rowc = lax.broadcasted_iota(jnp.int32, (1, C, 1), 1)
    beta_all = _sigmoid(small)
    g_all = -jnp.exp(alog) * _softplus(small + dtb)
    gc_all = mask_dot((row >= col).astype(F32), g_all)
    beta = jnp.sum(jnp.where(lane == head, beta_all[None], 0.0), axis=-1, keepdims=True)
    gc = jnp.sum(jnp.where(lane == head + H, gc_all[None], 0.0), axis=-1, keepdims=True)
    gc_row = jnp.sum(jnp.where(eye, gc, 0.0), axis=1, keepdims=True)
    decay = jnp.where(tril, jnp.exp(jnp.where(tril, gc - gc_row, 0.0)), 0.0)
    g_last = jnp.sum(jnp.where(rowc == C - 1, gc, 0.0), axis=1, keepdims=True)
    qn = _l2n(q) * (HEAD ** -0.5)
    kn = _l2n(k)
    kb = kn * beta
    low = jnp.where(strict, mdot(kb, kn, "bnt", False) * decay, 0.0)
    inv_rest = _unit_lower_inverse_minus_eye(low, C)
    eg = jnp.exp(gc)
    vb, kbe = v * beta, kb * eg
    u = vb + mdot(inv_rest, vb, "bnn", False)
    w = kbe + mdot(inv_rest, kbe, "bnn", False)
    attn = mdot(qn, kn, "bnt", False) * decay
    v_new = u - mdot(w, S, "bnn", False)
    o = mdot(qn * eg, S, "bnn", False) + mdot(attn, v_new, "bnn", False)
    k_dec = kn * jnp.exp(g_last - gc)
    return _rms_gate(o, gain, z), S * jnp.exp(g_last) + mdot(k_dec, v_new, "btn", False)


def _hgrn_step(St, qr, fr, vi, z, lb0, lb1, gain):
    H = St.shape[0]
    C, SB = HGRN_CHUNK, HGRN_SUB
    row = lax.broadcasted_iota(jnp.int32, (C, C), 0)
    col = lax.broadcasted_iota(jnp.int32, (C, C), 1)
    blk_start = row - (row & (SB - 1))
    in_blk_f = ((row >= col) & (col >= blk_start)).astype(F32)
    before_f = (col < blk_start).astype(F32)
    sums_f = jnp.concatenate([in_blk_f, before_f], axis=0)
    m = jnp.maximum(lb0, lb1)
    e0, e1 = jnp.exp(lb0 - m), jnp.exp(lb1 - m)
    lb = e1 / (e0 + e1)
    f = lb + (1.0 - lb) * _sigmoid(fr)
    q = _silu(qr)
    k = 1.0 - f
    logf = jnp.log(f)
    sums = mask_dot(sums_f, to_wide(logf))
    inner, start = to_heads(_rows(sums, 0, C), H), to_heads(_rows(sums, C, 2 * C), H)
    b = start + inner
    b_last = jnp.sum(logf, axis=1, keepdims=True)
    o = mdot(q * jnp.exp(b), St, "bnt", False)
    qt = q * jnp.exp(inner)
    parts = []
    for blk in range(C // SB):
        lo, n = blk * SB, (blk + 1) * SB
        ref = jnp.concatenate([_rows(start, lo, n)] * (blk + 1), axis=1)
        kt = _rows(k, 0, n) * jnp.exp(jnp.minimum(ref - _rows(b, 0, n), EXP_CLAMP))
        att = mdot(_rows(qt, lo, n), kt, "bnt", False)
        t_idx = lax.broadcasted_iota(jnp.int32, (1, SB, n), 1) + lo
        s_idx = lax.broadcasted_iota(jnp.int32, (1, SB, n), 2)
        att = jnp.where(s_idx <= t_idx, att, 0.0)
        parts.append(mdot(att, _rows(vi, 0, n), "bnn", False))
    o = o + jnp.concatenate(parts, axis=1)
    k_dec = k * jnp.exp(b_last - b)
    return _rms_gate(o, gain, z), St * jnp.exp(b_last) + mdot(vi, k_dec, "btn", False)


def _post_norm(s, x, g, b):
    r = ALPHA * x + s
    d = r - jnp.mean(r, axis=-1, keepdims=True)
    var = jnp.mean(d * d, axis=-1, keepdims=True)
    return d * lax.rsqrt(var + NORM_EPS) * g + b


def _post_gate(x1, gate_pre, pp):
    return x1 + pp * _sigmoid(gate_pre)


def _pick(dim, cands):
    for c in cands:
        if dim % c == 0:
            return c
    return dim


def _matmul_tiles(M, K, tn, a_bytes, b_bytes, has_add):
    for tk in (4096, 2048, 1536, 1152, 1024, 640, 512, 384, 256, 128):
        if K % tk:
            continue
        for tm in (2048, 1024, 512, 256, 128):
            if M % tm:
                continue
            blocks = tm * tk * a_bytes + tk * tn * b_bytes + tm * tn * 4 * (2 if has_add else 1)
            if 2 * blocks + (tm * tn * 4 if tk < K else 0) <= MATMUL_VMEM and tm >= min(M, 1024):
                return tm, tk
    return _pick(M, (512, 256, 128)), _pick(K, (512, 256, 128))


def _matmul(a, b, kind, name, add=None, after=None):
    if kind == "nn":
        (M, K), N = a.shape, b.shape[1]
    elif kind == "nt":
        (M, K), N = a.shape, b.shape[0]
    else:
        (K, M), N = a.shape, b.shape[1]
    has_add = add is not None
    tn = _pick(N, (512, 640, 384, 256, 128))
    tm, tk = _matmul_tiles(M, K, tn, a.dtype.itemsize, b.dtype.itemsize, has_add)
    nk = K // tk
    a_spec = pl.BlockSpec((tk, tm), lambda i, j, k: (k, i)) if kind == "tn" else pl.BlockSpec((tm, tk), lambda i, j, k: (i, k))
    b_spec = pl.BlockSpec((tn, tk), lambda i, j, k: (j, k)) if kind == "nt" else pl.BlockSpec((tk, tn), lambda i, j, k: (k, j))
    o_spec = pl.BlockSpec((tm, tn), lambda i, j, k: (i, j))

    extra = ([add] if has_add else []) + ([after] if after is not None else [])
    extra_specs = ([o_spec] if has_add else []) + ([pl.BlockSpec(TOKEN_SHAPE, lambda i, j, k: (0, 0))] if after is not None else [])

    def body(a_ref, b_ref, *rest):
        add_ref = rest[0] if has_add else None
        o_ref = rest[len(extra)]
        part = _dot_raw(a_ref[...], b_ref[...], kind, False)
        if nk == 1:
            o_ref[...] = part + add_ref[...] if has_add else part
            return
        acc = rest[-1]
        kk = pl.program_id(2)

        @pl.when(kk == 0)
        def _():
            acc[...] = part

        @pl.when(kk > 0)
        def _():
            acc[...] += part

        @pl.when(kk == nk - 1)
        def _():
            o_ref[...] = acc[...] + add_ref[...] if has_add else acc[...]

    return pl.pallas_call(
        body, name=name, grid=(M // tm, N // tn, nk),
        in_specs=[a_spec, b_spec] + extra_specs,
        out_specs=o_spec, out_shape=jax.ShapeDtypeStruct((M, N), F32),
        scratch_shapes=[pltpu.VMEM((tm, tn), F32)] if nk > 1 else [],
        compiler_params=_params(dimension_semantics=("parallel", "parallel", "arbitrary")),
    )(a, b, *extra)


def _halo_specs(ts, nt, width, prev=True, main=True, nxt=True):
    per = ts // SUBLANES
    last8 = nt * per - 1
    specs = []
    if prev:
        specs.append(pl.BlockSpec((SUBLANES, width), lambda cb, i: (jnp.maximum(i * per - 1, 0), cb)))
    if main:
        specs.append(pl.BlockSpec((ts, width), lambda cb, i: (i, cb)))
    if nxt:
        specs.append(pl.BlockSpec((SUBLANES, width), lambda cb, i: (jnp.minimum((i + 1) * per, last8), cb)))
    return specs


def _taps(ext, ktaps, lo, size):
    return [ext[lo:lo + size] if j == 0 else pltpu.roll(ext, j, 0)[lo:lo + size] for j in range(ktaps)]


def _ahead(ext, j, size):
    n = ext.shape[0]
    return ext[:size] if j == 0 else pltpu.roll(ext, n - j, 0)[:size]


def _lane_block(ref, k):
    return ref[:, k * MIXER_LANES:(k + 1) * MIXER_LANES]


def _mixer_a_fwd(proj_a, conv_w):
    T = proj_a.shape[0]
    nblk = proj_a.shape[1] // (4 * MIXER_LANES)
    ts = min(ROW_TILE, T)
    nt = T // ts

    def body(pp, pm, w_ref, y_ref):
        i = pl.program_id(1)
        u_prev = jnp.where(i > 0, _lane_block(pp, 0) * _lane_block(pp, 1), 0.0)
        ext = jnp.concatenate([u_prev, _lane_block(pm, 0) * _lane_block(pm, 1)], axis=0)
        t0, t1, t2 = _taps(ext, 3, SUBLANES, ts)
        cv = w_ref[2:3, :] * t0 + w_ref[1:2, :] * t1 + w_ref[0:1, :] * t2
        y_ref[...] = (_lane_block(pm, 2) * cv * _silu(_lane_block(pm, 3))).astype(y_ref.dtype)

    return pl.pallas_call(
        body, name="mixer_a_fwd", grid=(nblk, nt),
        in_specs=_halo_specs(ts, nt, 4 * MIXER_LANES, nxt=False)
        + [pl.BlockSpec((conv_w.shape[0], MIXER_LANES), lambda cb, i: (0, cb))],
        out_specs=pl.BlockSpec((ts, MIXER_LANES), lambda cb, i: (i, cb)),
        out_shape=jax.ShapeDtypeStruct((T, nblk * MIXER_LANES), BF16), compiler_params=_params(),
    )(proj_a, proj_a, conv_w)


def _mixer_a_bwd(proj_a, dy, conv_w):
    T = proj_a.shape[0]
    nblk = proj_a.shape[1] // (4 * MIXER_LANES)
    ts = min(ROW_TILE, T)
    nt = T // ts
    kt = conv_w.shape[0]

    def body(pp, pm, pn, dym, dyn, w_ref, dp_ref, dw_ref):
        i = pl.program_id(1)
        hm, cm, bm, zm = (_lane_block(pm, k) for k in range(4))
        u_prev = jnp.where(i > 0, _lane_block(pp, 0) * _lane_block(pp, 1), 0.0)
        ext = jnp.concatenate([u_prev, hm * cm], axis=0)
        dy_ext = jnp.concatenate([dym[...], jnp.where(i < nt - 1, dyn[...], 0.0)], axis=0)
        b_ext = jnp.concatenate([bm, _lane_block(pn, 2)], axis=0)
        sz_ext = _silu(jnp.concatenate([zm, _lane_block(pn, 3)], axis=0))
        dcv_ext = dy_ext * b_ext * sz_ext
        w = [w_ref[j:j + 1, :] for j in range(kt)]
        du = sum(w[kt - 1 - j] * _ahead(dcv_ext, j, ts) for j in range(kt))
        taps = _taps(ext, kt, SUBLANES, ts)
        cv = sum(w[kt - 1 - j] * taps[j] for j in range(kt))
        for part, d in enumerate((du * cm, du * hm, dym[...] * cv * sz_ext[:ts], dym[...] * bm * cv * _dsilu(zm))):
            dp_ref[:, part * MIXER_LANES:(part + 1) * MIXER_LANES] = d.astype(dp_ref.dtype)
        dcv = dcv_ext[:ts]

        @pl.when(i == 0)
        def _():
            dw_ref[...] = jnp.zeros_like(dw_ref)

        for j in range(kt):
            dw_ref[j:j + 1, :] += jnp.sum(dcv * taps[kt - 1 - j], axis=0, keepdims=True)

    return pl.pallas_call(
        body, name="mixer_a_bwd", grid=(nblk, nt),
        in_specs=_halo_specs(ts, nt, 4 * MIXER_LANES) + _halo_specs(ts, nt, MIXER_LANES, prev=False)
        + [pl.BlockSpec((kt, MIXER_LANES), lambda cb, i: (0, cb))],
        out_specs=[pl.BlockSpec((ts, 4 * MIXER_LANES), lambda cb, i: (i, cb)),
                   pl.BlockSpec((SUBLANES, MIXER_LANES), lambda cb, i: (0, cb))],
        out_shape=[jax.ShapeDtypeStruct(proj_a.shape, BF16),
                   jax.ShapeDtypeStruct((SUBLANES, nblk * MIXER_LANES), F32)],
        compiler_params=_params(),
    )(proj_a, proj_a, proj_a, dy, dy, conv_w)


def _conv_b_fwd(raw, conv_w):
    T = raw.shape[0]
    nblk = raw.shape[1] // CONV_LANES
    ts = min(ROW_TILE, T)
    nt = T // ts
    kt = conv_w.shape[0]

    def body(rp, rm, w_ref, y_ref):
        i = pl.program_id(1)
        ext = jnp.concatenate([jnp.where(i > 0, rp[...], 0.0), rm[...]], axis=0)
        taps = _taps(ext, kt, SUBLANES, ts)
        y_ref[...] = _silu(sum(w_ref[kt - 1 - j:kt - j, :] * taps[j] for j in range(kt)))

    return pl.pallas_call(
        body, name="conv_b_fwd", grid=(nblk, nt),
        in_specs=_halo_specs(ts, nt, CONV_LANES, nxt=False) + [pl.BlockSpec((kt, CONV_LANES), lambda cb, i: (0, cb))],
        out_specs=pl.BlockSpec((ts, CONV_LANES), lambda cb, i: (i, cb)),
        out_shape=jax.ShapeDtypeStruct(raw.shape, F32), compiler_params=_params(),
    )(raw, raw, conv_w)


def _conv_b_bwd(raw, dy, conv_w):
    T = raw.shape[0]
    nblk = raw.shape[1] // CONV_LANES
    ts = min(ROW_TILE, T)
    nt = T // ts
    kt = conv_w.shape[0]

    def body(rp, rm, rn, dym, dyn, w_ref, dr_ref, dw_ref):
        i = pl.program_id(1)
        ext = jnp.concatenate([jnp.where(i > 0, rp[...], 0.0), rm[...], rn[...]], axis=0)
        w = [w_ref[j:j + 1, :] for j in range(kt)]
        taps = _taps(ext, kt, SUBLANES, ts + SUBLANES)
        xc_ext = sum(w[kt - 1 - j] * taps[j] for j in range(kt))
        dy_ext = jnp.concatenate([dym[...], jnp.where(i < nt - 1, dyn[...], 0.0)], axis=0)
        dxc_ext = dy_ext * _dsilu(xc_ext)
        dr_ref[...] = sum(w[kt - 1 - j] * _ahead(dxc_ext, j, ts) for j in range(kt)).astype(dr_ref.dtype)
        dxc = dxc_ext[:ts]

        @pl.when(i == 0)
        def _():
            dw_ref[...] = jnp.zeros_like(dw_ref)

        for j in range(kt):
            dw_ref[j:j + 1, :] += jnp.sum(dxc * taps[kt - 1 - j][:ts], axis=0, keepdims=True)

    return pl.pallas_call(
        body, name="conv_b_bwd", grid=(nblk, nt),
        in_specs=_halo_specs(ts, nt, CONV_LANES) + _halo_specs(ts, nt, CONV_LANES, prev=False)
        + [pl.BlockSpec((kt, CONV_LANES), lambda cb, i: (0, cb))],
        out_specs=[pl.BlockSpec((ts, CONV_LANES), lambda cb, i: (i, cb)),
                   pl.BlockSpec((SUBLANES, CONV_LANES), lambda cb, i: (0, cb))],
        out_shape=[jax.ShapeDtypeStruct(raw.shape, BF16), jax.ShapeDtypeStruct((SUBLANES, nblk * CONV_LANES), F32)],
        compiler_params=_params(),
    )(raw, raw, raw, dy, dy, conv_w)


def _split_heads(ref, base, nheads, rows=slice(None)):
    return jnp.stack([ref[rows, base + h * HEAD: base + (h + 1) * HEAD] for h in range(nheads)], axis=0)


def _store_heads(ref, base, x, rows=slice(None), accumulate=False):
    for h in range(x.shape[0]):
        lanes = slice(base + h * HEAD, base + (h + 1) * HEAD)
        if accumulate:
            ref[rows, lanes] += x[h]
        else:
            ref[rows, lanes] = x[h].astype(ref.dtype)


def _gdn_fwd(qkv, proj_zs, alog, dtb, gain, H):
    T = qkv.shape[0]
    C, HW = GDN_CHUNK, H * HEAD
    nc = T // C
    zw = HW + LANES

    def body(qkv_ref, zs_ref, alog_ref, dtb_ref, gain_ref, o_ref, sall_ref, s_scr):
        @pl.when(pl.program_id(0) == 0)
        def _():
            s_scr[...] = jnp.zeros_like(s_scr)

        sall_ref[0] = s_scr[...]
        outs, states = _gdn_step(
            s_scr[...], _split_heads(qkv_ref, 0, H), _split_heads(qkv_ref, HW, H),
            _split_heads(qkv_ref, 2 * HW, H), _split_heads(zs_ref, 0, H), zs_ref[:, HW:HW + LANES],
            alog_ref[...], dtb_ref[...], gain_ref[...])
        _store_heads(o_ref, 0, outs)
        s_scr[...] = states

    row = pl.BlockSpec((1, LANES), lambda i: (0, 0))
    return pl.pallas_call(
        body, name="gdn_fwd", grid=(nc,),
        in_specs=[pl.BlockSpec((C, 3 * HW), lambda i: (i, 0)), pl.BlockSpec((C, zw), lambda i: (i, 0)), row, row, row],
        out_specs=[pl.BlockSpec((C, HW), lambda i: (i, 0)), pl.BlockSpec((1, H, HEAD, HEAD), lambda i: (i, 0, 0, 0))],
        out_shape=[jax.ShapeDtypeStruct((T, HW), BF16), jax.ShapeDtypeStruct((nc, H, HEAD, HEAD), F32)],
        scratch_shapes=[pltpu.VMEM((H, HEAD, HEAD), F32)], compiler_params=_params(),
    )(qkv, proj_zs, alog, dtb, gain)


def _gdn_bwd(qkv, proj_zs, do, s_all, alog, dtb, gain, H):
    T = qkv.shape[0]
    C, HW = GDN_CHUNK, H * HEAD
    nc = T // C
    zw = HW + LANES

    def body(qkv_ref, zs_ref, do_ref, sin_ref, alog_ref, dtb_ref, gain_ref,
             dqkv_ref, dzs_ref, dalog_ref, ddtb_ref, dgain_ref, ds_scr):
        @pl.when(pl.program_id(0) == 0)
        def _():
            ds_scr[...] = jnp.zeros_like(ds_scr)
            dalog_ref[...] = jnp.zeros_like(dalog_ref)
            ddtb_ref[...] = jnp.zeros_like(ddtb_ref)
            dgain_ref[...] = jnp.zeros_like(dgain_ref)

        primals = (sin_ref[0], _split_heads(qkv_ref, 0, H),
                   _split_heads(qkv_ref, HW, H), _split_heads(qkv_ref, 2 * HW, H), _split_heads(zs_ref, 0, H),
                   zs_ref[:, HW:HW + LANES], alog_ref[...], dtb_ref[...], gain_ref[...])
        _, vjp = jax.vjp(_gdn_step, *primals)
        dS, dq, dk, dv, dz, dsmall, dalog, ddtb, dgain = vjp((_split_heads(do_ref, 0, H), ds_scr[...]))
        ds_scr[...] = dS
        _store_heads(dqkv_ref, 0, dq)
        _store_heads(dqkv_ref, HW, dk)
        _store_heads(dqkv_ref, 2 * HW, dv)
        _store_heads(dzs_ref, 0, dz)
        dzs_ref[:, HW:HW + LANES] = dsmall.astype(dzs_ref.dtype)
        dalog_ref[...] += dalog
        ddtb_ref[...] += ddtb
        dgain_ref[...] += dgain

    row = pl.BlockSpec((1, LANES), lambda i: (0, 0))
    rev = lambda i: nc - 1 - i
    return pl.pallas_call(
        body, name="gdn_bwd", grid=(nc,),
        in_specs=[pl.BlockSpec((C, 3 * HW), lambda i: (rev(i), 0)), pl.BlockSpec((C, zw), lambda i: (rev(i), 0)),
                  pl.BlockSpec((C, HW), lambda i: (rev(i), 0)),
                  pl.BlockSpec((1, H, HEAD, HEAD), lambda i: (rev(i), 0, 0, 0)), row, row, row],
        out_specs=[pl.BlockSpec((C, 3 * HW), lambda i: (rev(i), 0)), pl.BlockSpec((C, zw), lambda i: (rev(i), 0)),
                   row, row, row],
        out_shape=[jax.ShapeDtypeStruct(qkv.shape, F32), jax.ShapeDtypeStruct(proj_zs.shape, BF16)]
        + [jax.ShapeDtypeStruct((1, LANES), F32)] * 3,
        scratch_shapes=[pltpu.VMEM((H, HEAD, HEAD), F32)], compiler_params=_params(),
    )(qkv, proj_zs, do, s_all, alog, dtb, gain)


def _hgrn_refs(proj_ref, lb_ref, HP):
    W = HP * HEAD
    return (_split_heads(proj_ref, 0, HP), _split_heads(proj_ref, W, HP), _split_heads(proj_ref, 2 * W, HP),
            _split_heads(proj_ref, 3 * W, HP), _split_heads(lb_ref, 0, HP, slice(0, 1)),
            _split_heads(lb_ref, 0, HP, slice(1, 2)))


def _hgrn_fwd(proj, lower_bounds, gain, nheads):
    T = proj.shape[0]
    C, HP = HGRN_CHUNK, HGRN_HEADS_PER_STEP
    ng, nc, W = nheads // HP, T // C, HP * HEAD

    def body(proj_ref, lb_ref, gain_ref, o_ref, sall_ref, s_scr):
        @pl.when(pl.program_id(1) == 0)
        def _():
            s_scr[...] = jnp.zeros_like(s_scr)

        sall_ref[0] = s_scr[...]
        qr, fr, vi, z, lb0, lb1 = _hgrn_refs(proj_ref, lb_ref, HP)
        outs, states = _hgrn_step(s_scr[...], qr, fr, vi, z, lb0, lb1, gain_ref[...])
        _store_heads(o_ref, 0, outs)
        s_scr[...] = states

    return pl.pallas_call(
        body, name="hgrn_fwd", grid=(ng, nc),
        in_specs=[pl.BlockSpec((C, 4 * W), lambda g, i: (i, g)), pl.BlockSpec((2, W), lambda g, i: (0, g)),
                  pl.BlockSpec((1, LANES), lambda g, i: (0, 0))],
        out_specs=[pl.BlockSpec((C, W), lambda g, i: (i, g)),
                   pl.BlockSpec((1, HP, HEAD, HEAD), lambda g, i: (i, g, 0, 0))],
        out_shape=[jax.ShapeDtypeStruct((T, nheads * HEAD), BF16), jax.ShapeDtypeStruct((nc, nheads, HEAD, HEAD), F32)],
        scratch_shapes=[pltpu.VMEM((HP, HEAD, HEAD), F32)], compiler_params=_params(),
    )(proj, lower_bounds, gain)


def _hgrn_bwd(proj, do, s_all, lower_bounds, gain, nheads):
    T = proj.shape[0]
    C, HP = HGRN_CHUNK, HGRN_HEADS_PER_STEP
    ng, nc, W = nheads // HP, T // C, HP * HEAD

    def body(proj_ref, do_ref, sin_ref, lb_ref, gain_ref, dproj_ref, dlb_ref, dgain_ref, ds_scr):
        first = pl.program_id(1) == 0

        @pl.when(first)
        def _():
            ds_scr[...] = jnp.zeros_like(ds_scr)
            dlb_ref[...] = jnp.zeros_like(dlb_ref)

        @pl.when(first & (pl.program_id(0) == 0))
        def _():
            dgain_ref[...] = jnp.zeros_like(dgain_ref)

        qr, fr, vi, z, lb0, lb1 = _hgrn_refs(proj_ref, lb_ref, HP)
        primals = (sin_ref[0], qr, fr, vi, z, lb0, lb1, gain_ref[...])
        _, vjp = jax.vjp(_hgrn_step, *primals)
        dS, dq, df, dv, dz, dlb0, dlb1, dgain = vjp((_split_heads(do_ref, 0, HP), ds_scr[...]))
        ds_scr[...] = dS
        for part, d in enumerate((dq, df, dv, dz)):
            _store_heads(dproj_ref, part * W, d)
        _store_heads(dlb_ref, 0, dlb0, slice(0, 1), accumulate=True)
        _store_heads(dlb_ref, 0, dlb1, slice(1, 2), accumulate=True)
        dgain_ref[...] += dgain

    rev = lambda i: nc - 1 - i
    return pl.pallas_call(
        body, name="hgrn_bwd", grid=(ng, nc),
        in_specs=[pl.BlockSpec((C, 4 * W), lambda g, i: (rev(i), g)), pl.BlockSpec((C, W), lambda g, i: (rev(i), g)),
                  pl.BlockSpec((1, HP, HEAD, HEAD), lambda g, i: (rev(i), g, 0, 0)),
                  pl.BlockSpec((2, W), lambda g, i: (0, g)), pl.BlockSpec((1, LANES), lambda g, i: (0, 0))],
        out_specs=[pl.BlockSpec((C, 4 * W), lambda g, i: (rev(i), g)), pl.BlockSpec((2, W), lambda g, i: (0, g)),
                   pl.BlockSpec((1, LANES), lambda g, i: (0, 0))],
        out_shape=[jax.ShapeDtypeStruct(proj.shape, BF16), jax.ShapeDtypeStruct(lower_bounds.shape, F32),
                   jax.ShapeDtypeStruct((1, LANES), F32)],
        scratch_shapes=[pltpu.VMEM((HP, HEAD, HEAD), F32)], compiler_params=_params(),
    )(proj, do, s_all, lower_bounds, gain)


def _post_specs(T):
    tr = min(POST_TILE, T)
    tile = lambda w: pl.BlockSpec((tr, w), lambda i: (i, 0))
    full = lambda r, w: pl.BlockSpec((r, w), lambda i: (0, 0))
    return tr, tile, full


def _post_fwd(s, x, p, g, b, wg, wpl, name):
    T, D = x.shape
    P = p.shape[1]
    tr, tile, full = _post_specs(T)

    def body(s_ref, x_ref, p_ref, g_ref, b_ref, wg_ref, wpl_ref, o_ref, o16_ref):
        x1 = _post_norm(s_ref[...], x_ref[...], g_ref[...], b_ref[...])
        xn = _post_gate(x1, _dot_raw(x1, wg_ref[...], "nn", False), _dot_raw(p_ref[...], wpl_ref[...], "nn", False))
        o_ref[...] = xn
        o16_ref[...] = xn.astype(BF16)

    return pl.pallas_call(
        body, name=name, grid=(T // tr,),
        in_specs=[tile(D), tile(D), tile(P), full(1, D), full(1, D), full(D, D), full(P, D)],
        out_specs=[tile(D), tile(D)],
        out_shape=[jax.ShapeDtypeStruct((T, D), F32), jax.ShapeDtypeStruct((T, D), BF16)], compiler_params=_params(),
    )(s, x, p, g, b, wg, wpl)


def _post_bwd(s, x, p, g, b, wg, wpl, dnext, name, with_loss):
    T, D = x.shape
    P = p.shape[1]
    tr, tile, full = _post_specs(T)

    def body(s_ref, x_ref, p_ref, g_ref, b_ref, wg_ref, wpl_ref, dn_ref,
             ds_ref, dx_ref, dg_ref, db_ref, dwg_ref, dwpl_ref, loss_ref):
        @pl.when(pl.program_id(0) == 0)
        def _():
            for r in (dg_ref, db_ref, dwg_ref, dwpl_ref, loss_ref):
                r[...] = jnp.zeros_like(r)

        x1, vjp_norm = jax.vjp(_post_norm, s_ref[...], x_ref[...], g_ref[...], b_ref[...])
        gate_pre = _dot_raw(x1, wg_ref[...], "nn", False)
        pp = _dot_raw(p_ref[...], wpl_ref[...], "nn", False)
        xn, vjp_gate = jax.vjp(_post_gate, x1, gate_pre, pp)
        if with_loss:
            err = xn - dn_ref[...]
            loss_ref[...] += 0.5 * jnp.sum(jnp.sum(err * err, axis=-1, keepdims=True), axis=0, keepdims=True) / D
            dn = err / D
        else:
            dn = dn_ref[...]
        dx1, dgp, dpp = vjp_gate(dn)
        dwg_ref[...] += _dot_raw(x1, dgp, "tn", False)
        dwpl_ref[...] += _dot_raw(p_ref[...], dpp, "tn", False)
        dx1 = dx1 + _dot_raw(dgp, wg_ref[...], "nt", False)
        ds, dx, dg, db = vjp_norm(dx1)
        ds_ref[...] = ds.astype(ds_ref.dtype)
        dx_ref[...] = dx
        dg_ref[...] += dg
        db_ref[...] += db

    return pl.pallas_call(
        body, name=name, grid=(T // tr,),
        in_specs=[tile(D), tile(D), tile(P), full(1, D), full(1, D), full(D, D), full(P, D), tile(D)],
        out_specs=[tile(D), tile(D), full(1, D), full(1, D), full(D, D), full(P, D), full(SUBLANES, LANES)],
        out_shape=[jax.ShapeDtypeStruct((T, D), BF16), jax.ShapeDtypeStruct((T, D), F32)]
        + [jax.ShapeDtypeStruct((1, D), F32)] * 2
        + [jax.ShapeDtypeStruct((D, D), F32), jax.ShapeDtypeStruct((P, D), F32),
           jax.ShapeDtypeStruct((SUBLANES, LANES), F32)],
        compiler_params=_params(),
    )(s, x, p, g, b, wg, wpl, dnext)


def _adam_math(w, g, m, v):
    m = ADAM_B1 * m + (1.0 - ADAM_B1) * g
    v = ADAM_B2 * v + (1.0 - ADAM_B2) * (g * g)
    m_hat = m / (1.0 - ADAM_B1 ** ADAM_STEP)
    v_hat = v / (1.0 - ADAM_B2 ** ADAM_STEP)
    return -ADAM_LR * (m_hat / (jnp.sqrt(v_hat) + ADAM_EPS) + ADAM_WD * w), m, v


def _rs_add(g8, got, cidx, name):
    _, R, C = g8.shape
    tr = _pick(R, (256, 128, 64, 32, 16, 8))

    def body(c_ref, a_ref, b_ref, o_ref, o16_ref):
        total = a_ref[...] + b_ref[...]
        o_ref[...] = total
        o16_ref[...] = total.astype(BF16)

    out_spec = pl.BlockSpec((1, tr, C), lambda q, i, c: (q, i, 0))
    return pl.pallas_call(
        body, name=name,
        grid_spec=pltpu.PrefetchScalarGridSpec(
            num_scalar_prefetch=1, grid=(4, R // tr),
            in_specs=[pl.BlockSpec((1, tr, C), lambda q, i, c: (2 * q + c[0], i, 0)),
                      pl.BlockSpec((1, tr, C), lambda q, i, c: (q, i, 0))],
            out_specs=[out_spec, out_spec]),
        out_shape=[jax.ShapeDtypeStruct((4,) + g8.shape[1:], F32), jax.ShapeDtypeStruct((4,) + g8.shape[1:], BF16)],
        compiler_params=_params(),
    )(cidx, g8, got)


def _adam_sharded(w, m, v, mine, got, qidx, name):
    R, C = w.shape
    tr = _pick(R, (256, 128, 64, 32, 16, 8))

    def body(q_ref, w_ref, m_ref, v_ref, p_ref, r0, r1, r2, g_ref, d_ref, mo_ref, vo_ref):
        g = ((p_ref[0] + r0[0].astype(F32)) + r1[0].astype(F32)) + r2[0].astype(F32)
        d, mn, vn = _adam_math(w_ref[...], g, m_ref[...], v_ref[...])
        g_ref[...] = g
        d_ref[...] = d
        mo_ref[...] = mn
        vo_ref[...] = vn

    t2 = pl.BlockSpec((tr, C), lambda i, q: (i, 0))
    slot = lambda k: pl.BlockSpec((1, tr, C), lambda i, q: (k, i, 0))
    return pl.pallas_call(
        body, name=name,
        grid_spec=pltpu.PrefetchScalarGridSpec(
            num_scalar_prefetch=1, grid=(R // tr,),
            in_specs=[t2, t2, t2, pl.BlockSpec((1, tr, C), lambda i, q: (q[0], i, 0)), slot(0), slot(1), slot(2)],
            out_specs=[t2, t2, t2, t2]),
        out_shape=[jax.ShapeDtypeStruct((R, C), F32)] * 4, compiler_params=_params(),
    )(qidx, w, m, v, mine, got, got, got)


def _adam_replicated(w, m, v, g8):
    def body(w_ref, m_ref, v_ref, g_ref, go_ref, d_ref, mo_ref, vo_ref):
        g = g_ref[0]
        for k in range(1, 8):
            g = g + g_ref[k]
        d, mn, vn = _adam_math(w_ref[...], g, m_ref[...], v_ref[...])
        go_ref[...] = g
        d_ref[...] = d
        mo_ref[...] = mn
        vo_ref[...] = vn

    return pl.pallas_call(
        body, name="adam_replicated", out_shape=[jax.ShapeDtypeStruct(w.shape, F32)] * 4, compiler_params=_params(),
    )(w, m, v, g8)


def _place():
    return lax.axis_index("x"), lax.axis_index("y"), lax.axis_index("c")


def _all_gather(shard, name):
    def body(x_ref, out_ref, send_sems, recv_sems, local_sem):
        x, y, c = _place()
        me, sibling = (x, y, c), (x, y, 1 - c)
        chips = [(1 - x, y), (x, 1 - y), (1 - x, 1 - y)]

        def slab(px, py, pc):
            return out_ref.at[4 * px + 2 * py + pc]

        def copy(k, block, to, src=None):
            return pltpu.make_async_remote_copy(
                src_ref=slab(*block) if src is None else src, dst_ref=slab(*block),
                send_sem=send_sems.at[k], recv_sem=recv_sems.at[k], device_id=to, device_id_type=MESH)

        mine = pltpu.make_async_copy(x_ref, slab(*me), local_sem)
        mine.start()
        first = [copy(0, me, sibling, src=x_ref)]
        first += [copy(1 + j, me, (*chip, c), src=x_ref) for j, chip in enumerate(chips)]
        for cp in first:
            cp.start()
        passed = [copy(4 + j, (*chip, c), sibling) for j, chip in enumerate(chips)]
        for j, chip in enumerate(chips):
            copy(1 + j, (*chip, c), me).wait_recv()
            passed[j].start()
        copy(0, sibling, me).wait_recv()
        for j, chip in enumerate(chips):
            copy(4 + j, (*chip, 1 - c), me).wait_recv()
        for cp in first + passed:
            cp.wait_send()
        mine.wait()

    return pl.pallas_call(
        body, name=name, out_shape=jax.ShapeDtypeStruct((8,) + shard.shape, shard.dtype),
        in_specs=[pl.BlockSpec(memory_space=pl.ANY)], out_specs=pl.BlockSpec(memory_space=pl.ANY),
        scratch_shapes=[pltpu.SemaphoreType.DMA((7,)), pltpu.SemaphoreType.DMA((7,)), pltpu.SemaphoreType.DMA],
    )(shard)


def _rs_to_sibling(g8, name):
    def body(g_ref, out_ref, send_sems, recv_sems):
        x, y, c = _place()
        copies = [pltpu.make_async_remote_copy(
            src_ref=g_ref.at[2 * q + (1 - c)], dst_ref=out_ref.at[q], send_sem=send_sems.at[q],
            recv_sem=recv_sems.at[q], device_id=(x, y, 1 - c), device_id_type=MESH) for q in range(4)]
        for cp in copies:
            cp.start()
        for cp in copies:
            cp.wait()

    return pl.pallas_call(
        body, name=name, out_shape=jax.ShapeDtypeStruct((4,) + g8.shape[1:], g8.dtype),
        in_specs=[pl.BlockSpec(memory_space=pl.ANY)], out_specs=pl.BlockSpec(memory_space=pl.ANY),
        scratch_shapes=[pltpu.SemaphoreType.DMA((4,)), pltpu.SemaphoreType.DMA((4,))],
    )(g8)


_HBM = pl.BlockSpec(memory_space=pltpu.HBM)
_SEM = pl.BlockSpec(memory_space=pltpu.SEMAPHORE)
_DATAFLOW = pltpu.SideEffectType.DATAFLOW_SIDE_EFFECTING
TOKEN_SHAPE = (SUBLANES, LANES)


def _chip_plan(x, y, c):
    return [(2 * px + py, j, (px, py, c)) for j, (px, py) in enumerate([(1 - x, y), (x, 1 - y), (1 - x, 1 - y)])]


def _exchange_copies(plan, src_ref, land_ref, send_sems, recv_sems):
    return [pltpu.make_async_remote_copy(
        src_ref=src_ref.at[blk], dst_ref=land_ref.at[slot], send_sem=send_sems.at[k], recv_sem=recv_sems.at[k],
        device_id=peer, device_id_type=MESH) for k, (blk, slot, peer) in enumerate(plan(*_place()))]


def _exchange_start(src, n_slots, plan, name):
    land_shape = (n_slots,) + src.shape[1:]
    n = len(plan(0, 0, 0))

    def body(src_ref, land_ref, send_sems, recv_sems, src_thru, land_thru, token):
        for cp in _exchange_copies(plan, src_ref, land_ref, send_sems, recv_sems):
            cp.start()
        token[...] = jnp.zeros_like(token)

    return pl.pallas_call(
        body, name=name,
        out_shape=(pltpu.SemaphoreType.DMA((n,)), pltpu.SemaphoreType.DMA((n,)), pltpu.HBM(src.shape, src.dtype),
                   pltpu.HBM(land_shape, src.dtype), jax.ShapeDtypeStruct(TOKEN_SHAPE, F32)),
        in_specs=(_HBM, _HBM), out_specs=(_SEM, _SEM, _HBM, _HBM, pl.BlockSpec(memory_space=pltpu.VMEM)),
        input_output_aliases={0: 2, 1: 3}, compiler_params=pltpu.CompilerParams(has_side_effects=_DATAFLOW),
    )(pltpu.with_memory_space_constraint(src, pltpu.HBM),
      pltpu.with_memory_space_constraint(lax.empty(land_shape, src.dtype), pltpu.HBM))


def _exchange_wait(handle, plan, after, name):
    send_sems, recv_sems, src_thru, land_thru, _ = handle

    def body(src_ref, land_ref, send_sems, recv_sems, after_ref, src_dead, got_ref):
        for cp in _exchange_copies(plan, src_ref, land_ref, send_sems, recv_sems):
            cp.wait_send()
            cp.wait_recv()

    return pl.pallas_call(
        body, name=name,
        out_shape=(pltpu.HBM(src_thru.shape, src_thru.dtype), pltpu.HBM(land_thru.shape, land_thru.dtype)),
        in_specs=(_HBM, _HBM, _SEM, _SEM, pl.BlockSpec(memory_space=pl.ANY)), out_specs=(_HBM, _HBM),
        input_output_aliases={0: 0, 1: 1}, compiler_params=pltpu.CompilerParams(has_side_effects=_DATAFLOW),
    )(src_thru, land_thru, send_sems, recv_sems, after)[1]


class _GradExchange:
    def __init__(self, cidx, qidx, layouts):
        self.cidx, self.qidx, self.layouts, self.pending = cidx, qidx, layouts, {}

    def start(self, tag, grad):
        g8 = self.layouts[tag](grad)
        got = _rs_to_sibling(g8, "rs_sibling_" + tag)
        chip_sums, chip_sums16 = _rs_add(g8, got, self.cidx, "rs_add_" + tag)
        handle = _exchange_start(chip_sums16, 3, _chip_plan, "rs_chips_start_" + tag)
        self.pending[tag] = (chip_sums, handle)
        return handle[4]

    def finish(self, tag, w, m, v, after):
        chip_sums, handle = self.pending.pop(tag)
        got2 = _exchange_wait(handle, _chip_plan, after, "rs_chips_wait_" + tag)
        return _adam_sharded(w, m, v, chip_sums, got2, self.qidx, "adam_" + tag)


def _local_grads(x, p0, p1, target, w_zs, w_a, w_qkv, wino, woute_a, woute_b, wouto, wg, wpl, conv_a, conv_b,
                 a_log, dt_bias, gdn_gain, lower_bounds, hgrn_gain, ln_g, ln_b, on_grad=None):
    H = a_log.shape[1]
    nheads_o = wouto.shape[0] // HEAD
    pad_small = ((0, 0), (H, LANES - 2 * H))
    alog_row = jnp.pad(a_log, pad_small)
    dtb_row = jnp.pad(dt_bias, pad_small)

    x16 = x.astype(BF16)
    proj_zs = _matmul(x16, w_zs, "nn", "proj_even_zs")
    proj_a = _matmul(x16, w_a, "nn", "proj_even_a")
    proj_qkv = _matmul(x16, w_qkv, "nn", "proj_even_qkv")
    y_a = _mixer_a_fwd(proj_a, conv_a)
    qkv = _conv_b_fwd(proj_qkv, conv_b)
    o2, s_gdn = _gdn_fwd(qkv, proj_zs, alog_row, dtb_row, gdn_gain, H)
    s_e = _matmul(o2, woute_b, "nn", "out_even_b", add=_matmul(y_a, woute_a, "nn", "out_even_a"))
    x2, x2_16 = _post_fwd(s_e, x, p0, ln_g[0:1], ln_b[0:1], wg[0], wpl[0], "post_even_fwd")
    proj_o = _matmul(x2_16, wino, "nn", "proj_odd")
    o4, s_hgrn = _hgrn_fwd(proj_o, lower_bounds, hgrn_gain, nheads_o)
    s_o = _matmul(o4, wouto, "nn", "out_odd")
    ds_o, dx2, dlng1, dlnb1, dwg1, dwpl1, loss = _post_bwd(
        s_o, x2, p1, ln_g[1:2], ln_b[1:2], wg[1], wpl[1], target, "post_odd_loss_bwd", True)
    do4 = _matmul(ds_o, wouto, "nt", "d_out_odd_act")
    grads = {}

    def emit(tag, grad):
        grads[tag] = grad
        return on_grad(tag, grad) if on_grad is not None else jnp.zeros(TOKEN_SHAPE, F32)

    tok = emit("w_out_odd", _matmul(o4, ds_o, "tn", "d_out_odd_w"))
    dproj_o, dlb, dhgain = _hgrn_bwd(proj_o, do4, s_hgrn, lower_bounds, hgrn_gain + tok[0:1], nheads_o)
    dx2 = _matmul(dproj_o, wino, "nt", "d_proj_odd_act", add=dx2)
    tok = emit("w_in_odd", _matmul(x2_16, dproj_o, "tn", "d_proj_odd_w"))
    ds_e, dx, dlng0, dlnb0, dwg0, dwpl0, _ = _post_bwd(
        s_e, x, p0, ln_g[0:1], ln_b[0:1] + tok[0:1, 0:1], wg[0], wpl[0], dx2, "post_even_bwd", False)
    tok = emit("w_pl_gate", jnp.stack([dwg0, dwg1])) + emit("w_pl", jnp.stack([dwpl0, dwpl1]))
    dy_a = _matmul(ds_e, woute_a, "nt", "d_out_even_a_act")
    do2 = _matmul(ds_e, woute_b, "nt", "d_out_even_b_act")
    dwoute_a = _matmul(y_a, ds_e, "tn", "d_out_even_a_w")
    dwoute_b = _matmul(o2, ds_e, "tn", "d_out_even_b_w")
    tok = tok + emit("w_out_even", jnp.concatenate([dwoute_a, dwoute_b], axis=0))
    dqkv, dproj_zs, dalog, ddtb, dggain = _gdn_bwd(qkv, proj_zs, do2, s_gdn, alog_row, dtb_row, gdn_gain + tok[0:1], H)
    dproj_qkv, dconv_b = _conv_b_bwd(proj_qkv, dqkv, conv_b)
    dproj_a, dconv_a = _mixer_a_bwd(proj_a, dy_a, conv_a)
    emit("conv", (dconv_a[:conv_a.shape[0]], dconv_b[:conv_b.shape[0]]))
    tok = emit("w_in_even", (_matmul(x16, dproj_zs, "tn", "d_proj_even_zs_w"), _matmul(x16, dproj_a, "tn", "d_proj_even_a_w"),
                             _matmul(x16, dproj_qkv, "tn", "d_proj_even_qkv_w")))
    dx = _matmul(dproj_zs, w_zs, "nt", "d_proj_even_zs_act", add=dx, after=tok)
    dx = _matmul(dproj_a, w_a, "nt", "d_proj_even_a_act", add=dx)
    dx = _matmul(dproj_qkv, w_qkv, "nt", "d_proj_even_qkv_act", add=dx)
    grads.update(
        loss=loss[0, 0], grad_x=dx, a_log=dalog[:, H:2 * H], dt_bias=ddtb[:, H:2 * H], gdn_gain=dggain,
        lower_bounds=dlb, hgrn_gain=dhgain, ln_g=jnp.concatenate([dlng0, dlng1], axis=0),
        ln_b=jnp.concatenate([dlnb0, dlnb1], axis=0))
    return grads


def _pad_rows(a, rows):
    return jnp.pad(a, ((0, rows - a.shape[0]), (0, 0)))


def _pack_small(a_log, dt_bias, gdn_gain, lower_bounds, hgrn_gain, ln_g, ln_b):
    lane_pad = lambda a: _pad_rows(jnp.pad(a, ((0, 0), (0, LANES - a.shape[1]))), SUBLANES)
    parts = [lane_pad(a_log), lane_pad(dt_bias), lane_pad(gdn_gain), lower_bounds.reshape(-1, LANES),
             lane_pad(hgrn_gain), ln_g.reshape(-1, LANES), ln_b.reshape(-1, LANES)]
    packed = jnp.concatenate(parts, axis=0)
    assert packed.shape[0] == SMALL_ROWS, packed.shape
    return packed


def _unpack_small(packed, shapes):
    out, r = [], 0
    for shp in shapes:
        n = shp[0] * shp[1]
        if n < LANES * SUBLANES and shp[1] <= LANES:
            out.append(packed[r:r + shp[0], :shp[1]])
            r += SUBLANES
        else:
            rows = n // LANES
            out.append(packed[r:r + rows].reshape(shp))
            r += rows
    return out


def _split_in_even(w_full, AW, HW, H):
    D = w_full.shape[0]
    n_a = 4 * AW
    n_main = n_a + 3 * HW
    w_zs = jnp.concatenate([w_full[:, n_main:n_main + HW], w_full[:, n_main + HW:],
                            jnp.zeros((D, LANES - 2 * H), w_full.dtype)], axis=1)
    w_a = w_full[:, :n_a].reshape(D, 4, AW // MIXER_LANES, MIXER_LANES).transpose(0, 2, 1, 3).reshape(D, n_a)
    return w_zs, w_a, w_full[:, n_a:n_main]


def _join_in_even(d_zs, d_a, d_qkv, AW, HW, H):
    D = d_a.shape[0]
    a_nat = d_a.reshape(D, AW // MIXER_LANES, 4, MIXER_LANES).transpose(0, 2, 1, 3).reshape(D, 4 * AW)
    return jnp.concatenate([a_nat, d_qkv, d_zs[:, :HW], d_zs[:, HW:HW + 2 * H]], axis=1)


def kernel(x, p, w_in_even, conv_a_w, conv_b_w, a_log, dt_bias, gdn_norm_g, w_out_even, w_in_odd, lower_bounds, hgrn_norm_g, w_out_odd, ln_g, ln_b, w_pl, w_pl_gate, loss_target, m_w_in_even, m_conv_a_w, m_conv_b_w, m_a_log, m_dt_bias, m_gdn_norm_g, m_w_out_even, m_w_in_odd, m_lower_bounds, m_hgrn_norm_g, m_w_out_odd, m_ln_g, m_ln_b, m_w_pl, m_w_pl_gate, v_w_in_even, v_conv_a_w, v_conv_b_w, v_a_log, v_dt_bias, v_gdn_norm_g, v_w_out_even, v_w_in_odd, v_lower_bounds, v_hgrn_norm_g, v_w_out_odd, v_ln_g, v_ln_b, v_w_pl, v_w_pl_gate):
    xi, yi, ci = _place()
    cidx = jnp.reshape(ci, (1,)).astype(jnp.int32)
    qidx = jnp.reshape(2 * xi + yi, (1,)).astype(jnp.int32)
    D = x.shape[2]
    H = a_log.shape[1]
    HW = H * HEAD
    AW = conv_a_w.shape[2] * 8
    OW = w_out_odd.shape[1] * 8
    PD = w_pl.shape[1]
    ka, kb = conv_a_w.shape[1], conv_b_w.shape[1]
    ca, cb = conv_a_w.shape[2], conv_b_w.shape[2]
    gw = HGRN_HEADS_PER_STEP * HEAD
    ngrp = OW // gw

    g_ine = _all_gather(w_in_even[0].astype(BF16), "ag_w_in_even")
    w_zs, w_a, w_qkv = _split_in_even(jnp.transpose(g_ine, (1, 0, 2)).reshape(D, -1), AW, HW, H)
    g_ino = _all_gather(w_in_odd[0].astype(BF16), "ag_w_in_odd")
    wino = jnp.transpose(g_ino, (1, 0, 2)).reshape(D, 4, ngrp, gw).transpose(0, 2, 1, 3).reshape(D, 4 * OW)
    woute = _all_gather(w_out_even[0].astype(BF16), "ag_w_out_even").reshape(-1, D)
    wouto = _all_gather(w_out_odd[0].astype(BF16), "ag_w_out_odd").reshape(-1, D)
    g_gate = _all_gather(w_pl_gate.astype(BF16).reshape(-1, D), "ag_w_pl_gate")
    wg = g_gate.reshape(8, DEPTH, D // 8, D).transpose(1, 0, 2, 3).reshape(DEPTH, D, D)
    g_pl = _all_gather(w_pl.astype(BF16).reshape(DEPTH * PD, -1), "ag_w_pl")
    wpl = g_pl.reshape(8, DEPTH, PD, D // 8).transpose(1, 2, 0, 3).reshape(DEPTH, PD, D)
    taps = jnp.concatenate([_pad_rows(conv_a_w[0], SUBLANES), _pad_rows(conv_b_w[0], SUBLANES)], axis=1)
    g_taps = _all_gather(taps, "ag_conv")
    conv_a = jnp.transpose(g_taps[:, :ka, :ca], (1, 0, 2)).reshape(ka, 8 * ca)
    conv_b = jnp.transpose(g_taps[:, :kb, ca:], (1, 0, 2)).reshape(kb, 8 * cb)

    sh = w_in_even.shape[2]
    tap_blocks = lambda g, width: _pad_rows(g, SUBLANES).reshape(SUBLANES, 8, width).transpose(1, 0, 2)
    owner_layout = {
        "w_in_even": lambda g: _join_in_even(*g, AW, HW, H).reshape(D, 8, sh).transpose(1, 0, 2),
        "w_in_odd": lambda g: g.reshape(D, ngrp, 4, gw).transpose(0, 2, 1, 3).reshape(D, 8, 4 * OW // 8).transpose(1, 0, 2),
        "w_out_even": lambda g: g.reshape(8, -1, D),
        "w_out_odd": lambda g: g.reshape(8, -1, D),
        "w_pl_gate": lambda g: g.reshape(DEPTH, 8, D // 8, D).transpose(1, 0, 2, 3).reshape(8, DEPTH * D // 8, D),
        "w_pl": lambda g: g.reshape(DEPTH, PD, 8, D // 8).transpose(2, 0, 1, 3).reshape(8, DEPTH * PD, D // 8),
        "conv": lambda g: jnp.concatenate([tap_blocks(g[0], ca), tap_blocks(g[1], cb)], axis=2),
    }
    exchange = _GradExchange(cidx, qidx, owner_layout)
    gr = _local_grads(x[0], p[0, 0], p[1, 0], loss_target[0], w_zs, w_a, w_qkv, wino, woute[:AW], woute[AW:], wouto,
                      wg, wpl, conv_a, conv_b, a_log, dt_bias, gdn_norm_g, lower_bounds, hgrn_norm_g, ln_g, ln_b,
                      on_grad=exchange.start)
    loss = lax.psum(gr["loss"], AXES)

    last = gr["grad_x"]
    flat_gate = lambda a: a.reshape(DEPTH * D // 8, D)
    flat_pl = lambda a: a.reshape(DEPTH * PD, D // 8)
    pack_taps = lambda a, b: jnp.concatenate([_pad_rows(a[0], SUBLANES), _pad_rows(b[0], SUBLANES)], axis=1)
    o_outo = exchange.finish("w_out_odd", w_out_odd[0], m_w_out_odd[0], v_w_out_odd[0], last)
    o_ino = exchange.finish("w_in_odd", w_in_odd[0], m_w_in_odd[0], v_w_in_odd[0], last)
    o_gate = exchange.finish("w_pl_gate", flat_gate(w_pl_gate), flat_gate(m_w_pl_gate), flat_gate(v_w_pl_gate), last)
    o_pl = exchange.finish("w_pl", flat_pl(w_pl), flat_pl(m_w_pl), flat_pl(v_w_pl), last)
    o_oute = exchange.finish("w_out_even", w_out_even[0], m_w_out_even[0], v_w_out_even[0], last)
    o_taps = exchange.finish("conv", taps, pack_taps(m_conv_a_w, m_conv_b_w), pack_taps(v_conv_a_w, v_conv_b_w), last)
    others_done = sum(o[1][0:1, 0:1] for o in (o_outo, o_ino, o_gate, o_pl, o_oute, o_taps))
    o_ine = exchange.finish("w_in_even", w_in_even[0], m_w_in_even[0], v_w_in_even[0], others_done)

    small_g = _pack_small(gr["a_log"], gr["dt_bias"], gr["gdn_gain"], gr["lower_bounds"], gr["hgrn_gain"],
                          gr["ln_g"], gr["ln_b"])
    o_small = _adam_replicated(
        _pack_small(a_log, dt_bias, gdn_norm_g, lower_bounds, hgrn_norm_g, ln_g, ln_b),
        _pack_small(m_a_log, m_dt_bias, m_gdn_norm_g, m_lower_bounds, m_hgrn_norm_g, m_ln_g, m_ln_b),
        _pack_small(v_a_log, v_dt_bias, v_gdn_norm_g, v_lower_bounds, v_hgrn_norm_g, v_ln_g, v_ln_b),
        _all_gather(small_g, "ag_small_grads"))
    small_shapes = [a_log.shape, dt_bias.shape, gdn_norm_g.shape, lower_bounds.shape, hgrn_norm_g.shape,
                    ln_g.shape, ln_b.shape]

    def leaves(kind):
        s_alog, s_dt, s_gg, s_lb, s_hg, s_lng, s_lnb = _unpack_small(o_small[kind], small_shapes)
        t = o_taps[kind]
        return [o_ine[kind][None], t[None, :ka, :ca], t[None, :kb, ca:], s_alog, s_dt, s_gg, o_oute[kind][None],
                o_ino[kind][None], s_lb, s_hg, o_outo[kind][None], s_lng, s_lnb,
                o_pl[kind].reshape(w_pl.shape), o_gate[kind].reshape(w_pl_gate.shape)]

    return (loss, gr["grad_x"][None], *leaves(0), *leaves(1), *leaves(2), *leaves(3))
```

```python
import functools

import jax
import jax.numpy as jnp
from jax import lax
from jax.experimental import pallas as pl
from jax.experimental.pallas import tpu as pltpu

F32 = jnp.float32
BF16 = jnp.bfloat16
MESH = pl.DeviceIdType.MESH
AXES = ("x", "y", "c")

LANES = 128
SUBLANES = 8
HEAD = 128
GDN_CHUNK = 64
HGRN_CHUNK = 32
HGRN_SUB = 16
HGRN_HEADS_PER_STEP = 16
NORM_EPS = 1e-5
DEPTH = 2
ALPHA = (2.0 * DEPTH) ** 0.25
EXP_CLAMP = 80.0
ADAM_LR, ADAM_B1, ADAM_B2, ADAM_EPS, ADAM_WD, ADAM_STEP = 0.001, 0.9, 0.999, 1e-08, 0.01, 10
VMEM_LIMIT = 56 * 1024 * 1024
MATMUL_VMEM = 36 * 1024 * 1024
ROW_TILE = 512
MIXER_LANES = 256
CONV_LANES = 512
POST_TILE = 256
SMALL_ROWS = 96

_NOBATCH, _BATCH0 = ((), ()), ((0,), (0,))
_DIMS = {"nn": (((1,), (0,)), _NOBATCH), "nt": (((1,), (1,)), _NOBATCH), "tn": (((0,), (0,)), _NOBATCH),
         "bnn": (((2,), (1,)), _BATCH0), "bnt": (((2,), (2,)), _BATCH0), "btn": (((1,), (1,)), _BATCH0)}


def _params(**kw):
    return pltpu.CompilerParams(vmem_limit_bytes=VMEM_LIMIT, **kw)


def _dot_raw(a, b, kind, hi):
    if hi:
        return lax.dot_general(a, b, _DIMS[kind], precision=lax.Precision.HIGHEST, preferred_element_type=F32)
    return lax.dot_general(a.astype(BF16), b.astype(BF16), _DIMS[kind], preferred_element_type=F32)


@functools.partial(jax.custom_vjp, nondiff_argnums=(2, 3))
def mdot(a, b, kind, hi):
    return _dot_raw(a, b, kind, hi)


def _mdot_fwd(a, b, kind, hi):
    return _dot_raw(a, b, kind, hi), (a, b)


def _mdot_bwd(kind, hi, res, g):
    a, b = res
    pre, base = kind[:-2], kind[-2:]
    if base == "nn":
        return _dot_raw(g, b, pre + "nt", hi), _dot_raw(a, g, pre + "tn", hi)
    if base == "nt":
        return _dot_raw(g, b, pre + "nn", hi), _dot_raw(g, a, pre + "tn", hi)
    return _dot_raw(b, g, pre + "nt", hi), _dot_raw(a, g, pre + "nn", hi)


mdot.defvjp(_mdot_fwd, _mdot_bwd)


def _rows(x, lo, hi):
    return _take_rows(x, lo, hi, x.shape[-2])


@functools.partial(jax.custom_vjp, nondiff_argnums=(1, 2, 3))
def _take_rows(x, lo, hi, n):
    return x[..., lo:hi, :]


def _take_rows_fwd(x, lo, hi, n):
    return x[..., lo:hi, :], None


def _take_rows_bwd(lo, hi, n, _, g):
    parts = []
    if lo > 0:
        parts.append(jnp.zeros(g.shape[:-2] + (lo, g.shape[-1]), g.dtype))
    parts.append(g)
    if n - hi > 0:
        parts.append(jnp.zeros(g.shape[:-2] + (n - hi, g.shape[-1]), g.dtype))
    return (jnp.concatenate(parts, axis=-2) if len(parts) > 1 else g,)


_take_rows.defvjp(_take_rows_fwd, _take_rows_bwd)


def _heads_of(wide, nheads):
    return jnp.stack([wide[:, h * HEAD:(h + 1) * HEAD] for h in range(nheads)], axis=0)


def _wide_of(x):
    return jnp.concatenate([x[h] for h in range(x.shape[0])], axis=1)


@functools.partial(jax.custom_vjp, nondiff_argnums=(1,))
def to_heads(wide, nheads):
    return _heads_of(wide, nheads)


to_heads.defvjp(lambda wide, nheads: (_heads_of(wide, nheads), None), lambda nheads, _, g: (_wide_of(g),))


@jax.custom_vjp
def to_wide(x):
    return _wide_of(x)


to_wide.defvjp(lambda x: (_wide_of(x), None), lambda _, g: (_heads_of(g, g.shape[1] // HEAD),))


def _sigmoid(x):
    return jax.nn.sigmoid(x)


def _silu(x):
    return x * _sigmoid(x)


def _dsilu(x):
    s = _sigmoid(x)
    return s * (1.0 + x * (1.0 - s))


def _log1p(u):
    return jnp.where(u < 1e-4, u * (1.0 - 0.5 * u), jnp.log(1.0 + u))


def _softplus(x):
    return jnp.maximum(x, 0.0) + _log1p(jnp.exp(-jnp.abs(x)))


def _rms_gate(o, gain, z):
    return o * lax.rsqrt(jnp.mean(o * o, axis=-1, keepdims=True) + NORM_EPS) * gain * _silu(z)


def _l2n(x):
    return x * lax.rsqrt(jnp.sum(x * x, axis=-1, keepdims=True) + 1e-6)


def _split_dot_raw(m, x, kind):
    mb = m.astype(BF16)
    hi = x.astype(BF16)
    lo = (x - hi.astype(F32)).astype(BF16)
    dims = _DIMS[kind]
    return (lax.dot_general(mb, hi, dims, preferred_element_type=F32)
            + lax.dot_general(mb, lo, dims, preferred_element_type=F32))


@jax.custom_vjp
def mask_dot(m, x):
    return _split_dot_raw(m, x, "nn")


def _mask_dot_fwd(m, x):
    return _split_dot_raw(m, x, "nn"), m


def _mask_dot_bwd(m, g):
    return jnp.zeros_like(m), _split_dot_raw(m, g, "tn")


mask_dot.defvjp(_mask_dot_fwd, _mask_dot_bwd)


def _unit_lower_inverse_minus_eye(low, n):
    rest = -low
    power = low
    span = 2
    while span < n:
        power = mdot(power, power, "bnn", False)
        rest = rest + power + mdot(rest, power, "bnn", False)
        span *= 2
    return rest


def _gdn_step(S, q, k, v, z, small, alog, dtb, gain):
    H = S.shape[0]
    C = GDN_CHUNK
    row = lax.broadcasted_iota(jnp.int32, (C, C), 0)
    col = lax.broadcasted_iota(jnp.int32, (C, C), 1)
    tril, strict, eye = (row >= col)[None], (row > col)[None], (row == col)[None]
    head = lax.broadcasted_iota(jnp.int32, (H, 1, LANES), 0)
    lane = lax.broadcasted_iota(jnp.int32, (H, 1, LANES), 2)
    rowc = lax.broadcasted_iota(jnp.int32, (1, C, 1), 1)
    beta_all = _sigmoid(small)
    g_all = -jnp.exp(alog) * _softplus(small + dtb)
    gc_all = mask_dot((row >= col).astype(F32), g_all)
    beta = jnp.sum(jnp.where(lane == head, beta_all[None], 0.0), axis=-1, keepdims=True)
    gc = jnp.sum(jnp.where(lane == head + H, gc_all[None], 0.0), axis=-1, keepdims=True)
    gc_row = jnp.sum(jnp.where(eye, gc, 0.0), axis=1, keepdims=True)
    decay = jnp.where(tril, jnp.exp(jnp.where(tril, gc - gc_row, 0.0)), 0.0)
    g_last = jnp.sum(jnp.where(rowc == C - 1, gc, 0.0), axis=1, keepdims=True)
    qn = _l2n(q) * (HEAD ** -0.5)
    kn = _l2n(k)
    kb = kn * beta
    low = jnp.where(strict, mdot(kb, kn, "bnt", False) * decay, 0.0)
    inv_rest = _unit_lower_inverse_minus_eye(low, C)
    eg = jnp.exp(gc)
    vb, kbe = v * beta, kb * eg
    u = vb + mdot(inv_rest, vb, "bnn", False)
    w = kbe + mdot(inv_rest, kbe, "bnn", False)
    attn = mdot(qn, kn, "bnt", False) * decay
    v_new = u - mdot(w, S, "bnn", False)
    o = mdot(qn * eg, S, "bnn", False) + mdot(attn, v_new, "bnn", False)
    k_dec = kn * jnp.exp(g_last - gc)
    return _rms_gate(o, gain, z), S * jnp.exp(g_last) + mdot(k_dec, v_new, "btn", False)


def _hgrn_step(St, qr, fr, vi, z, lb0, lb1, gain):
    H = St.shape[0]
    C, SB = HGRN_CHUNK, HGRN_SUB
    row = lax.broadcasted_iota(jnp.int32, (C, C), 0)
    col = lax.broadcasted_iota(jnp.int32, (C, C), 1)
    blk_start = row - (row & (SB - 1))
    in_blk_f = ((row >= col) & (col >= blk_start)).astype(F32)
    before_f = (col < blk_start).astype(F32)
    sums_f = jnp.concatenate([in_blk_f, before_f], axis=0)
    m = jnp.maximum(lb0, lb1)
    e0, e1 = jnp.exp(lb0 - m), jnp.exp(lb1 - m)
    lb = e1 / (e0 + e1)
    f = lb + (1.0 - lb) * _sigmoid(fr)
    q = _silu(qr)
    k = 1.0 - f
    logf = jnp.log(f)
    sums = mask_dot(sums_f, to_wide(logf))
    inner, start = to_heads(_rows(sums, 0, C), H), to_heads(_rows(sums, C, 2 * C), H)
    b = start + inner
    b_last = jnp.sum(logf, axis=1, keepdims=True)
    o = mdot(q * jnp.exp(b), St, "bnt", False)
    qt = q * jnp.exp(inner)
    parts = []
    for blk in range(C // SB):
        lo, n = blk * SB, (blk + 1) * SB
        ref = jnp.concatenate([_rows(start, lo, n)] * (blk + 1), axis=1)
        kt = _rows(k, 0, n) * jnp.exp(jnp.minimum(ref - _rows(b, 0, n), EXP_CLAMP))
        att = mdot(_rows(qt, lo, n), kt, "bnt", False)
        t_idx = lax.broadcasted_iota(jnp.int32, (1, SB, n), 1) + lo
        s_idx = lax.broadcasted_iota(jnp.int32, (1, SB, n), 2)
        att = jnp.where(s_idx <= t_idx, att, 0.0)
        parts.append(mdot(att, _rows(vi, 0, n), "bnn", False))
    o = o + jnp.concatenate(parts, axis=1)
    k_dec = k * jnp.exp(b_last - b)
    return _rms_gate(o, gain, z), St * jnp.exp(b_last) + mdot(vi, k_dec, "btn", False)


def _post_norm(s, x, g, b):
    r = ALPHA * x + s
    d = r - jnp.mean(r, axis=-1, keepdims=True)
    var = jnp.mean(d * d, axis=-1, keepdims=True)
    return d * lax.rsqrt(var + NORM_EPS) * g + b


def _post_gate(x1, gate_pre, pp):
    return x1 + pp * _sigmoid(gate_pre)


def _pick(dim, cands):
    for c in cands:
        if dim % c == 0:
            return c
    return dim


def _matmul_tiles(M, K, tn, a_bytes, b_bytes, has_add):
    for tk in (4096, 2048, 1536, 1152, 1024, 640, 512, 384, 256, 128):
        if K % tk:
            continue
        for tm in (2048, 1024, 512, 256, 128):
            if M % tm:
                continue
            blocks = tm * tk * a_bytes + tk * tn * b_bytes + tm * tn * 4 * (2 if has_add else 1)
            if 2 * blocks + (tm * tn * 4 if tk < K else 0) <= MATMUL_VMEM and tm >= min(M, 1024):
                return tm, tk
    return _pick(M, (512, 256, 128)), _pick(K, (512, 256, 128))


def _matmul(a, b, kind, name, add=None, after=None):
    if kind == "nn":
        (M, K), N = a.shape, b.shape[1]
    elif kind == "nt":
        (M, K), N = a.shape, b.shape[0]
    else:
        (K, M), N = a.shape, b.shape[1]
    has_add = add is not None
    tn = _pick(N, (512, 640, 384, 256, 128))
    tm, tk = _matmul_tiles(M, K, tn, a.dtype.itemsize, b.dtype.itemsize, has_add)
    nk = K // tk
    a_spec = pl.BlockSpec((tk, tm), lambda i, j, k: (k, i)) if kind == "tn" else pl.BlockSpec((tm, tk), lambda i, j, k: (i, k))
    b_spec = pl.BlockSpec((tn, tk), lambda i, j, k: (j, k)) if kind == "nt" else pl.BlockSpec((tk, tn), lambda i, j, k: (k, j))
    o_spec = pl.BlockSpec((tm, tn), lambda i, j, k: (i, j))

    extra = ([add] if has_add else []) + ([after] if after is not None else [])
    extra_specs = ([o_spec] if has_add else []) + ([pl.BlockSpec(TOKEN_SHAPE, lambda i, j, k: (0, 0))] if after is not None else [])

    def body(a_ref, b_ref, *rest):
        add_ref = rest[0] if has_add else None
        o_ref = rest[len(extra)]
        part = _dot_raw(a_ref[...], b_ref[...], kind, False)
        if nk == 1:
            o_ref[...] = part + add_ref[...] if has_add else part
            return
        acc = rest[-1]
        kk = pl.program_id(2)

        @pl.when(kk == 0)
        def _():
            acc[...] = part

        @pl.when(kk > 0)
        def _():
            acc[...] += part

        @pl.when(kk == nk - 1)
        def _():
            o_ref[...] = acc[...] + add_ref[...] if has_add else acc[...]

    return pl.pallas_call(
        body, name=name, grid=(M // tm, N // tn, nk),
        in_specs=[a_spec, b_spec] + extra_specs,
        out_specs=o_spec, out_shape=jax.ShapeDtypeStruct((M, N), F32),
        scratch_shapes=[pltpu.VMEM((tm, tn), F32)] if nk > 1 else [],
        compiler_params=_params(dimension_semantics=("parallel", "parallel", "arbitrary")),
    )(a, b, *extra)


def _halo_specs(ts, nt, width, prev=True, main=True, nxt=True):
    per = ts // SUBLANES
    last8 = nt * per - 1
    specs = []
    if prev:
        specs.append(pl.BlockSpec((SUBLANES, width), lambda cb, i: (jnp.maximum(i * per - 1, 0), cb)))
    if main:
        specs.append(pl.BlockSpec((ts, width), lambda cb, i: (i, cb)))
    if nxt:
        specs.append(pl.BlockSpec((SUBLANES, width), lambda cb, i: (jnp.minimum((i + 1) * per, last8), cb)))
    return specs


def _taps(ext, ktaps, lo, size):
    return [ext[lo:lo + size] if j == 0 else pltpu.roll(ext, j, 0)[lo:lo + size] for j in range(ktaps)]


def _ahead(ext, j, size):
    n = ext.shape[0]
    return ext[:size] if j == 0 else pltpu.roll(ext, n - j, 0)[:size]


def _lane_block(ref, k):
    return ref[:, k * MIXER_LANES:(k + 1) * MIXER_LANES]


def _mixer_a_fwd(proj_a, conv_w):
    T = proj_a.shape[0]
    nblk = proj_a.shape[1] // (4 * MIXER_LANES)
    ts = min(ROW_TILE, T)
    nt = T // ts

    def body(pp, pm, w_ref, y_ref):
        i = pl.program_id(1)
        u_prev = jnp.where(i > 0, _lane_block(pp, 0) * _lane_block(pp, 1), 0.0)
        ext = jnp.concatenate([u_prev, _lane_block(pm, 0) * _lane_block(pm, 1)], axis=0)
        t0, t1, t2 = _taps(ext, 3, SUBLANES, ts)
        cv = w_ref[2:3, :] * t0 + w_ref[1:2, :] * t1 + w_ref[0:1, :] * t2
        y_ref[...] = (_lane_block(pm, 2) * cv * _silu(_lane_block(pm, 3))).astype(y_ref.dtype)

    return pl.pallas_call(
        body, name="mixer_a_fwd", grid=(nblk, nt),
        in_specs=_halo_specs(ts, nt, 4 * MIXER_LANES, nxt=False)
        + [pl.BlockSpec((conv_w.shape[0], MIXER_LANES), lambda cb, i: (0, cb))],
        out_specs=pl.BlockSpec((ts, MIXER_LANES), lambda cb, i: (i, cb)),
        out_shape=jax.ShapeDtypeStruct((T, nblk * MIXER_LANES), BF16), compiler_params=_params(),
    )(proj_a, proj_a, conv_w)


def _mixer_a_bwd(proj_a, dy, conv_w):
    T = proj_a.shape[0]
    nblk = proj_a.shape[1] // (4 * MIXER_LANES)
    ts = min(ROW_TILE, T)
    nt = T // ts
    kt = conv_w.shape[0]

    def body(pp, pm, pn, dym, dyn, w_ref, dp_ref, dw_ref):
        i = pl.program_id(1)
        hm, cm, bm, zm = (_lane_block(pm, k) for k in range(4))
        u_prev = jnp.where(i > 0, _lane_block(pp, 0) * _lane_block(pp, 1), 0.0)
        ext = jnp.concatenate([u_prev, hm * cm], axis=0)
        dy_ext = jnp.concatenate([dym[...], jnp.where(i < nt - 1, dyn[...], 0.0)], axis=0)
        b_ext = jnp.concatenate([bm, _lane_block(pn, 2)], axis=0)
        sz_ext = _silu(jnp.concatenate([zm, _lane_block(pn, 3)], axis=0))
        dcv_ext = dy_ext * b_ext * sz_ext
        w = [w_ref[j:j + 1, :] for j in range(kt)]
        du = sum(w[kt - 1 - j] * _ahead(dcv_ext, j, ts) for j in range(kt))
        taps = _taps(ext, kt, SUBLANES, ts)
        cv = sum(w[kt - 1 - j] * taps[j] for j in range(kt))
        for part, d in enumerate((du * cm, du * hm, dym[...] * cv * sz_ext[:ts], dym[...] * bm * cv * _dsilu(zm))):
            dp_ref[:, part * MIXER_LANES:(part + 1) * MIXER_LANES] = d.astype(dp_ref.dtype)
        dcv = dcv_ext[:ts]

        @pl.when(i == 0)
        def _():
            dw_ref[...] = jnp.zeros_like(dw_ref)

        for j in range(kt):
            dw_ref[j:j + 1, :] += jnp.sum(dcv * taps[kt - 1 - j], axis=0, keepdims=True)

    return pl.pallas_call(
        body, name="mixer_a_bwd", grid=(nblk, nt),
        in_specs=_halo_specs(ts, nt, 4 * MIXER_LANES) + _halo_specs(ts, nt, MIXER_LANES, prev=False)
        + [pl.BlockSpec((kt, MIXER_LANES), lambda cb, i: (0, cb))],
        out_specs=[pl.BlockSpec((ts, 4 * MIXER_LANES), lambda cb, i: (i, cb)),
                   pl.BlockSpec((SUBLANES, MIXER_LANES), lambda cb, i: (0, cb))],
        out_shape=[jax.ShapeDtypeStruct(proj_a.shape, BF16),
                   jax.ShapeDtypeStruct((SUBLANES, nblk * MIXER_LANES), F32)],
        compiler_params=_params(),
    )(proj_a, proj_a, proj_a, dy, dy, conv_w)


def _conv_b_fwd(raw, conv_w):
    T = raw.shape[0]
    nblk = raw.shape[1] // CONV_LANES
    ts = min(ROW_TILE, T)
    nt = T // ts
    kt = conv_w.shape[0]

    def body(rp, rm, w_ref, y_ref):
        i = pl.program_id(1)
        ext = jnp.concatenate([jnp.where(i > 0, rp[...], 0.0), rm[...]], axis=0)
        taps = _taps(ext, kt, SUBLANES, ts)
        y_ref[...] = _silu(sum(w_ref[kt - 1 - j:kt - j, :] * taps[j] for j in range(kt)))

    return pl.pallas_call(
        body, name="conv_b_fwd", grid=(nblk, nt),
        in_specs=_halo_specs(ts, nt, CONV_LANES, nxt=False) + [pl.BlockSpec((kt, CONV_LANES), lambda cb, i: (0, cb))],
        out_specs=pl.BlockSpec((ts, CONV_LANES), lambda cb, i: (i, cb)),
        out_shape=jax.ShapeDtypeStruct(raw.shape, F32), compiler_params=_params(),
    )(raw, raw, conv_w)


def _conv_b_bwd(raw, dy, conv_w):
    T = raw.shape[0]
    nblk = raw.shape[1] // CONV_LANES
    ts = min(ROW_TILE, T)
    nt = T // ts
    kt = conv_w.shape[0]

    def body(rp, rm, rn, dym, dyn, w_ref, dr_ref, dw_ref):
        i = pl.program_id(1)
        ext = jnp.concatenate([jnp.where(i > 0, rp[...], 0.0), rm[...], rn[...]], axis=0)
        w = [w_ref[j:j + 1, :] for j in range(kt)]
        taps = _taps(ext, kt, SUBLANES, ts + SUBLANES)
        xc_ext = sum(w[kt - 1 - j] * taps[j] for j in range(kt))
        dy_ext = jnp.concatenate([dym[...], jnp.where(i < nt - 1, dyn[...], 0.0)], axis=0)
        dxc_ext = dy_ext * _dsilu(xc_ext)
        dr_ref[...] = sum(w[kt - 1 - j] * _ahead(dxc_ext, j, ts) for j in range(kt)).astype(dr_ref.dtype)
        dxc = dxc_ext[:ts]

        @pl.when(i == 0)
        def _():
            dw_ref[...] = jnp.zeros_like(dw_ref)

        for j in range(kt):
            dw_ref[j:j + 1, :] += jnp.sum(dxc * taps[kt - 1 - j][:ts], axis=0, keepdims=True)

    return pl.pallas_call(
        body, name="conv_b_bwd", grid=(nblk, nt),
        in_specs=_halo_specs(ts, nt, CONV_LANES) + _halo_specs(ts, nt, CONV_LANES, prev=False)
        + [pl.BlockSpec((kt, CONV_LANES), lambda cb, i: (0, cb))],
        out_specs=[pl.BlockSpec((ts, CONV_LANES), lambda cb, i: (i, cb)),
                   pl.BlockSpec((SUBLANES, CONV_LANES), lambda cb, i: (0, cb))],
        out_shape=[jax.ShapeDtypeStruct(raw.shape, BF16), jax.ShapeDtypeStruct((SUBLANES, nblk * CONV_LANES), F32)],
        compiler_params=_params(),
    )(raw, raw, raw, dy, dy, conv_w)


def _split_heads(ref, base, nheads, rows=slice(None)):
    return jnp.stack([ref[rows, base + h * HEAD: base + (h + 1) * HEAD] for h in range(nheads)], axis=0)


def _store_heads(ref, base, x, rows=slice(None), accumulate=False):
    for h in range(x.shape[0]):
        lanes = slice(base + h * HEAD, base + (h + 1) * HEAD)
        if accumulate:
            ref[rows, lanes] += x[h]
        else:
            ref[rows, lanes] = x[h].astype(ref.dtype)


def _gdn_fwd(qkv, proj_zs, alog, dtb, gain, H):
    T = qkv.shape[0]
    C, HW = GDN_CHUNK, H * HEAD
    nc = T // C
    zw = HW + LANES

    def body(qkv_ref, zs_ref, alog_ref, dtb_ref, gain_ref, o_ref, sall_ref, s_scr):
        @pl.when(pl.program_id(0) == 0)
        def _():
            s_scr[...] = jnp.zeros_like(s_scr)

        sall_ref[0] = s_scr[...]
        outs, states = _gdn_step(
            s_scr[...], _split_heads(qkv_ref, 0, H), _split_heads(qkv_ref, HW, H),
            _split_heads(qkv_ref, 2 * HW, H), _split_heads(zs_ref, 0, H), zs_ref[:, HW:HW + LANES],
            alog_ref[...], dtb_ref[...], gain_ref[...])
        _store_heads(o_ref, 0, outs)
        s_scr[...] = states

    row = pl.BlockSpec((1, LANES), lambda i: (0, 0))
    return pl.pallas_call(
        body, name="gdn_fwd", grid=(nc,),
        in_specs=[pl.BlockSpec((C, 3 * HW), lambda i: (i, 0)), pl.BlockSpec((C, zw), lambda i: (i, 0)), row, row, row],
        out_specs=[pl.BlockSpec((C, HW), lambda i: (i, 0)), pl.BlockSpec((1, H, HEAD, HEAD), lambda i: (i, 0, 0, 0))],
        out_shape=[jax.ShapeDtypeStruct((T, HW), BF16), jax.ShapeDtypeStruct((nc, H, HEAD, HEAD), F32)],
        scratch_shapes=[pltpu.VMEM((H, HEAD, HEAD), F32)], compiler_params=_params(),
    )(qkv, proj_zs, alog, dtb, gain)


def _gdn_bwd(qkv, proj_zs, do, s_all, alog, dtb, gain, H):
    T = qkv.shape[0]
    C, HW = GDN_CHUNK, H * HEAD
    nc = T // C
    zw = HW + LANES

    def body(qkv_ref, zs_ref, do_ref, sin_ref, alog_ref, dtb_ref, gain_ref,
             dqkv_ref, dzs_ref, dalog_ref, ddtb_ref, dgain_ref, ds_scr):
        @pl.when(pl.program_id(0) == 0)
        def _():
            ds_scr[...] = jnp.zeros_like(ds_scr)
            dalog_ref[...] = jnp.zeros_like(dalog_ref)
            ddtb_ref[...] = jnp.zeros_like(ddtb_ref)
            dgain_ref[...] = jnp.zeros_like(dgain_ref)

        primals = (sin_ref[0], _split_heads(qkv_ref, 0, H),
                   _split_heads(qkv_ref, HW, H), _split_heads(qkv_ref, 2 * HW, H), _split_heads(zs_ref, 0, H),
                   zs_ref[:, HW:HW + LANES], alog_ref[...], dtb_ref[...], gain_ref[...])
        _, vjp = jax.vjp(_gdn_step, *primals)
        dS, dq, dk, dv, dz, dsmall, dalog, ddtb, dgain = vjp((_split_heads(do_ref, 0, H), ds_scr[...]))
        ds_scr[...] = dS
        _store_heads(dqkv_ref, 0, dq)
        _store_heads(dqkv_ref, HW, dk)
        _store_heads(dqkv_ref, 2 * HW, dv)
        _store_heads(dzs_ref, 0, dz)
        dzs_ref[:, HW:HW + LANES] = dsmall.astype(dzs_ref.dtype)
        dalog_ref[...] += dalog
        ddtb_ref[...] += ddtb
        dgain_ref[...] += dgain

    row = pl.BlockSpec((1, LANES), lambda i: (0, 0))
    rev = lambda i: nc - 1 - i
    return pl.pallas_call(
        body, name="gdn_bwd", grid=(nc,),
        in_specs=[pl.BlockSpec((C, 3 * HW), lambda i: (rev(i), 0)), pl.BlockSpec((C, zw), lambda i: (rev(i), 0)),
                  pl.BlockSpec((C, HW), lambda i: (rev(i), 0)),
                  pl.BlockSpec((1, H, HEAD, HEAD), lambda i: (rev(i), 0, 0, 0)), row, row, row],
        out_specs=[pl.BlockSpec((C, 3 * HW), lambda i: (rev(i), 0)), pl.BlockSpec((C, zw), lambda i: (rev(i), 0)),
                   row, row, row],
        out_shape=[jax.ShapeDtypeStruct(qkv.shape, F32), jax.ShapeDtypeStruct(proj_zs.shape, BF16)]
        + [jax.ShapeDtypeStruct((1, LANES), F32)] * 3,
        scratch_shapes=[pltpu.VMEM((H, HEAD, HEAD), F32)], compiler_params=_params(),
    )(qkv, proj_zs, do, s_all, alog, dtb, gain)


def _hgrn_refs(proj_ref, lb_ref, HP):
    W = HP * HEAD
    return (_split_heads(proj_ref, 0, HP), _split_heads(proj_ref, W, HP), _split_heads(proj_ref, 2 * W, HP),
            _split_heads(proj_ref, 3 * W, HP), _split_heads(lb_ref, 0, HP, slice(0, 1)),
            _split_heads(lb_ref, 0, HP, slice(1, 2)))


def _hgrn_fwd(proj, lower_bounds, gain, nheads):
    T = proj.shape[0]
    C, HP = HGRN_CHUNK, HGRN_HEADS_PER_STEP
    ng, nc, W = nheads // HP, T // C, HP * HEAD

    def body(proj_ref, lb_ref, gain_ref, o_ref, sall_ref, s_scr):
        @pl.when(pl.program_id(1) == 0)
        def _():
            s_scr[...] = jnp.zeros_like(s_scr)

        sall_ref[0] = s_scr[...]
        qr, fr, vi, z, lb0, lb1 = _hgrn_refs(proj_ref, lb_ref, HP)
        outs, states = _hgrn_step(s_scr[...], qr, fr, vi, z, lb0, lb1, gain_ref[...])
        _store_heads(o_ref, 0, outs)
        s_scr[...] = states

    return pl.pallas_call(
        body, name="hgrn_fwd", grid=(ng, nc),
        in_specs=[pl.BlockSpec((C, 4 * W), lambda g, i: (i, g)), pl.BlockSpec((2, W), lambda g, i: (0, g)),
                  pl.BlockSpec((1, LANES), lambda g, i: (0, 0))],
        out_specs=[pl.BlockSpec((C, W), lambda g, i: (i, g)),
                   pl.BlockSpec((1, HP, HEAD, HEAD), lambda g, i: (i, g, 0, 0))],
        out_shape=[jax.ShapeDtypeStruct((T, nheads * HEAD), BF16), jax.ShapeDtypeStruct((nc, nheads, HEAD, HEAD), F32)],
        scratch_shapes=[pltpu.VMEM((HP, HEAD, HEAD), F32)], compiler_params=_params(),
    )(proj, lower_bounds, gain)


def _hgrn_bwd(proj, do, s_all, lower_bounds, gain, nheads):
    T = proj.shape[0]
    C, HP = HGRN_CHUNK, HGRN_HEADS_PER_STEP
    ng, nc, W = nheads // HP, T // C, HP * HEAD

    def body(proj_ref, do_ref, sin_ref, lb_ref, gain_ref, dproj_ref, dlb_ref, dgain_ref, ds_scr):
        first = pl.program_id(1) == 0

        @pl.when(first)
        def _():
            ds_scr[...] = jnp.zeros_like(ds_scr)
            dlb_ref[...] = jnp.zeros_like(dlb_ref)

        @pl.when(first & (pl.program_id(0) == 0))
        def _():
            dgain_ref[...] = jnp.zeros_like(dgain_ref)

        qr, fr, vi, z, lb0, lb1 = _hgrn_refs(proj_ref, lb_ref, HP)
        primals = (sin_ref[0], qr, fr, vi, z, lb0, lb1, gain_ref[...])
        _, vjp = jax.vjp(_hgrn_step, *primals)
        dS, dq, df, dv, dz, dlb0, dlb1, dgain = vjp((_split_heads(do_ref, 0, HP), ds_scr[...]))
        ds_scr[...] = dS
        for part, d in enumerate((dq, df, dv, dz)):
            _store_heads(dproj_ref, part * W, d)
        _store_heads(dlb_ref, 0, dlb0, slice(0, 1), accumulate=True)
        _store_heads(dlb_ref, 0, dlb1, slice(1, 2), accumulate=True)
        dgain_ref[...] += dgain

    rev = lambda i: nc - 1 - i
    return pl.pallas_call(
        body, name="hgrn_bwd", grid=(ng, nc),
        in_specs=[pl.BlockSpec((C, 4 * W), lambda g, i: (rev(i), g)), pl.BlockSpec((C, W), lambda g, i: (rev(i), g)),
                  pl.BlockSpec((1, HP, HEAD, HEAD), lambda g, i: (rev(i), g, 0, 0)),
                  pl.BlockSpec((2, W), lambda g, i: (0, g)), pl.BlockSpec((1, LANES), lambda g, i: (0, 0))],
        out_specs=[pl.BlockSpec((C, 4 * W), lambda g, i: (rev(i), g)), pl.BlockSpec((2, W), lambda g, i: (0, g)),
                   pl.BlockSpec((1, LANES), lambda g, i: (0, 0))],
        out_shape=[jax.ShapeDtypeStruct(proj.shape, BF16), jax.ShapeDtypeStruct(lower_bounds.shape, F32),
                   jax.ShapeDtypeStruct((1, LANES), F32)],
        scratch_shapes=[pltpu.VMEM((HP, HEAD, HEAD), F32)], compiler_params=_params(),
    )(proj, do, s_all, lower_bounds, gain)


def _post_specs(T):
    tr = min(POST_TILE, T)
    tile = lambda w: pl.BlockSpec((tr, w), lambda i: (i, 0))
    full = lambda r, w: pl.BlockSpec((r, w), lambda i: (0, 0))
    return tr, tile, full


def _post_fwd(s, x, p, g, b, wg, wpl, name):
    T, D = x.shape
    P = p.shape[1]
    tr, tile, full = _post_specs(T)

    def body(s_ref, x_ref, p_ref, g_ref, b_ref, wg_ref, wpl_ref, o_ref, o16_ref):
        x1 = _post_norm(s_ref[...], x_ref[...], g_ref[...], b_ref[...])
        xn = _post_gate(x1, _dot_raw(x1, wg_ref[...], "nn", False), _dot_raw(p_ref[...], wpl_ref[...], "nn", False))
        o_ref[...] = xn
        o16_ref[...] = xn.astype(BF16)

    return pl.pallas_call(
        body, name=name, grid=(T // tr,),
        in_specs=[tile(D), tile(D), tile(P), full(1, D), full(1, D), full(D, D), full(P, D)],
        out_specs=[tile(D), tile(D)],
        out_shape=[jax.ShapeDtypeStruct((T, D), F32), jax.ShapeDtypeStruct((T, D), BF16)], compiler_params=_params(),
    )(s, x, p, g, b, wg, wpl)


def _post_bwd(s, x, p, g, b, wg, wpl, dnext, name, with_loss):
    T, D = x.shape
    P = p.shape[1]
    tr, tile, full = _post_specs(T)

    def body(s_ref, x_ref, p_ref, g_ref, b_ref, wg_ref, wpl_ref, dn_ref,
             ds_ref, dx_ref, dg_ref, db_ref, dwg_ref, dwpl_ref, loss_ref):
        @pl.when(pl.program_id(0) == 0)
        def _():
            for r in (dg_ref, db_ref, dwg_ref, dwpl_ref, loss_ref):
                r[...] = jnp.zeros_like(r)

        x1, vjp_norm = jax.vjp(_post_norm, s_ref[...], x_ref[...], g_ref[...], b_ref[...])
        gate_pre = _dot_raw(x1, wg_ref[...], "nn", False)
        pp = _dot_raw(p_ref[...], wpl_ref[...], "nn", False)
        xn, vjp_gate = jax.vjp(_post_gate, x1, gate_pre, pp)
        if with_loss:
            err = xn - dn_ref[...]
            loss_ref[...] += 0.5 * jnp.sum(jnp.sum(err * err, axis=-1, keepdims=True), axis=0, keepdims=True) / D
            dn = err / D
        else:
            dn = dn_ref[...]
        dx1, dgp, dpp = vjp_gate(dn)
        dwg_ref[...] += _dot_raw(x1, dgp, "tn", False)
        dwpl_ref[...] += _dot_raw(p_ref[...], dpp, "tn", False)
        dx1 = dx1 + _dot_raw(dgp, wg_ref[...], "nt", False)
        ds, dx, dg, db = vjp_norm(dx1)
        ds_ref[...] = ds.astype(ds_ref.dtype)
        dx_ref[...] = dx
        dg_ref[...] += dg
        db_ref[...] += db

    return pl.pallas_call(
        body, name=name, grid=(T // tr,),
        in_specs=[tile(D), tile(D), tile(P), full(1, D), full(1, D), full(D, D), full(P, D), tile(D)],
        out_specs=[tile(D), tile(D), full(1, D), full(1, D), full(D, D), full(P, D), full(SUBLANES, LANES)],
        out_shape=[jax.ShapeDtypeStruct((T, D), BF16), jax.ShapeDtypeStruct((T, D), F32)]
        + [jax.ShapeDtypeStruct((1, D), F32)] * 2
        + [jax.ShapeDtypeStruct((D, D), F32), jax.ShapeDtypeStruct((P, D), F32),
           jax.ShapeDtypeStruct((SUBLANES, LANES), F32)],
        compiler_params=_params(),
    )(s, x, p, g, b, wg, wpl, dnext)


def _adam_math(w, g, m, v):
    m = ADAM_B1 * m + (1.0 - ADAM_B1) * g
    v = ADAM_B2 * v + (1.0 - ADAM_B2) * (g * g)
    m_hat = m / (1.0 - ADAM_B1 ** ADAM_STEP)
    v_hat = v / (1.0 - ADAM_B2 ** ADAM_STEP)
    return -ADAM_LR * (m_hat / (jnp.sqrt(v_hat) + ADAM_EPS) + ADAM_WD * w), m, v


def _rs_add(g8, got, cidx, name):
    _, R, C = g8.shape
    tr = _pick(R, (256, 128, 64, 32, 16, 8))

    def body(c_ref, a_ref, b_ref, o_ref, o16_ref):
        total = a_ref[...] + b_ref[...]
        o_ref[...] = total
        o16_ref[...] = total.astype(BF16)

    out_spec = pl.BlockSpec((1, tr, C), lambda q, i, c: (q, i, 0))
    return pl.pallas_call(
        body, name=name,
        grid_spec=pltpu.PrefetchScalarGridSpec(
            num_scalar_prefetch=1, grid=(4, R // tr),
            in_specs=[pl.BlockSpec((1, tr, C), lambda q, i, c: (2 * q + c[0], i, 0)),
                      pl.BlockSpec((1, tr, C), lambda q, i, c: (q, i, 0))],
            out_specs=[out_spec, out_spec]),
        out_shape=[jax.ShapeDtypeStruct((4,) + g8.shape[1:], F32), jax.ShapeDtypeStruct((4,) + g8.shape[1:], BF16)],
        compiler_params=_params(),
    )(cidx, g8, got)


def _adam_sharded(w, m, v, mine, got, qidx, name):
    R, C = w.shape
    tr = _pick(R, (256, 128, 64, 32, 16, 8))

    def body(q_ref, w_ref, m_ref, v_ref, p_ref, r0, r1, r2, g_ref, d_ref, mo_ref, vo_ref):
        g = ((p_ref[0] + r0[0].astype(F32)) + r1[0].astype(F32)) + r2[0].astype(F32)
        d, mn, vn = _adam_math(w_ref[...], g, m_ref[...], v_ref[...])
        g_ref[...] = g
        d_ref[...] = d
        mo_ref[...] = mn
        vo_ref[...] = vn

    t2 = pl.BlockSpec((tr, C), lambda i, q: (i, 0))
    slot = lambda k: pl.BlockSpec((1, tr, C), lambda i, q: (k, i, 0))
    return pl.pallas_call(
        body, name=name,
        grid_spec=pltpu.PrefetchScalarGridSpec(
            num_scalar_prefetch=1, grid=(R // tr,),
            in_specs=[t2, t2, t2, pl.BlockSpec((1, tr, C), lambda i, q: (q[0], i, 0)), slot(0), slot(1), slot(2)],
            out_specs=[t2, t2, t2, t2]),
        out_shape=[jax.ShapeDtypeStruct((R, C), F32)] * 4, compiler_params=_params(),
    )(qidx, w, m, v, mine, got, got, got)


def _adam_replicated(w, m, v, g8):
    def body(w_ref, m_ref, v_ref, g_ref, go_ref, d_ref, mo_ref, vo_ref):
        g = g_ref[0]
        for k in range(1, 8):
            g = g + g_ref[k]
        d, mn, vn = _adam_math(w_ref[...], g, m_ref[...], v_ref[...])
        go_ref[...] = g
        d_ref[...] = d
        mo_ref[...] = mn
        vo_ref[...] = vn

    return pl.pallas_call(
        body, name="adam_replicated", out_shape=[jax.ShapeDtypeStruct(w.shape, F32)] * 4, compiler_params=_params(),
    )(w, m, v, g8)


def _place():
    return lax.axis_index("x"), lax.axis_index("y"), lax.axis_index("c")


def _all_gather(shard, name):
    def body(x_ref, out_ref, send_sems, recv_sems, local_sem):
        x, y, c = _place()
        me, sibling = (x, y, c), (x, y, 1 - c)
        chips = [(1 - x, y), (x, 1 - y), (1 - x, 1 - y)]

        def slab(px, py, pc):
            return out_ref.at[4 * px + 2 * py + pc]

        def copy(k, block, to, src=None):
            return pltpu.make_async_remote_copy(
                src_ref=slab(*block) if src is None else src, dst_ref=slab(*block),
                send_sem=send_sems.at[k], recv_sem=recv_sems.at[k], device_id=to, device_id_type=MESH)

        mine = pltpu.make_async_copy(x_ref, slab(*me), local_sem)
        mine.start()
        first = [copy(0, me, sibling, src=x_ref)]
        first += [copy(1 + j, me, (*chip, c), src=x_ref) for j, chip in enumerate(chips)]
        for cp in first:
            cp.start()
        passed = [copy(4 + j, (*chip, c), sibling) for j, chip in enumerate(chips)]
        for j, chip in enumerate(chips):
            copy(1 + j, (*chip, c), me).wait_recv()
            passed[j].start()
        copy(0, sibling, me).wait_recv()
        for j, chip in enumerate(chips):
            copy(4 + j, (*chip, 1 - c), me).wait_recv()
        for cp in first + passed:
            cp.wait_send()
        mine.wait()

    return pl.pallas_call(
        body, name=name, out_shape=jax.ShapeDtypeStruct((8,) + shard.shape, shard.dtype),
        in_specs=[pl.BlockSpec(memory_space=pl.ANY)], out_specs=pl.BlockSpec(memory_space=pl.ANY),
        scratch_shapes=[pltpu.SemaphoreType.DMA((7,)), pltpu.SemaphoreType.DMA((7,)), pltpu.SemaphoreType.DMA],
    )(shard)


def _rs_to_sibling(g8, name):
    def body(g_ref, out_ref, send_sems, recv_sems):
        x, y, c = _place()
        copies = [pltpu.make_async_remote_copy(
            src_ref=g_ref.at[2 * q + (1 - c)], dst_ref=out_ref.at[q], send_sem=send_sems.at[q],
            recv_sem=recv_sems.at[q], device_id=(x, y, 1 - c), device_id_type=MESH) for q in range(4)]
        for cp in copies:
            cp.start()
        for cp in copies:
            cp.wait()

    return pl.pallas_call(
        body, name=name, out_shape=jax.ShapeDtypeStruct((4,) + g8.shape[1:], g8.dtype),
        in_specs=[pl.BlockSpec(memory_space=pl.ANY)], out_specs=pl.BlockSpec(memory_space=pl.ANY),
        scratch_shapes=[pltpu.SemaphoreType.DMA((4,)), pltpu.SemaphoreType.DMA((4,))],
    )(g8)


_HBM = pl.BlockSpec(memory_space=pltpu.HBM)
_SEM = pl.BlockSpec(memory_space=pltpu.SEMAPHORE)
_DATAFLOW = pltpu.SideEffectType.DATAFLOW_SIDE_EFFECTING
TOKEN_SHAPE = (SUBLANES, LANES)


def _chip_plan(x, y, c):
    return [(2 * px + py, j, (px, py, c)) for j, (px, py) in enumerate([(1 - x, y), (x, 1 - y), (1 - x, 1 - y)])]


def _exchange_copies(plan, src_ref, land_ref, send_sems, recv_sems):
    return [pltpu.make_async_remote_copy(
        src_ref=src_ref.at[blk], dst_ref=land_ref.at[slot], send_sem=send_sems.at[k], recv_sem=recv_sems.at[k],
        device_id=peer, device_id_type=MESH) for k, (blk, slot, peer) in enumerate(plan(*_place()))]


def _exchange_start(src, n_slots, plan, name):
    land_shape = (n_slots,) + src.shape[1:]
    n = len(plan(0, 0, 0))

    def body(src_ref, land_ref, send_sems, recv_sems, src_thru, land_thru, token):
        for cp in _exchange_copies(plan, src_ref, land_ref, send_sems, recv_sems):
            cp.start()
        token[...] = jnp.zeros_like(token)

    return pl.pallas_call(
        body, name=name,
        out_shape=(pltpu.SemaphoreType.DMA((n,)), pltpu.SemaphoreType.DMA((n,)), pltpu.HBM(src.shape, src.dtype),
                   pltpu.HBM(land_shape, src.dtype), jax.ShapeDtypeStruct(TOKEN_SHAPE, F32)),
        in_specs=(_HBM, _HBM), out_specs=(_SEM, _SEM, _HBM, _HBM, pl.BlockSpec(memory_space=pltpu.VMEM)),
        input_output_aliases={0: 2, 1: 3}, compiler_params=pltpu.CompilerParams(has_side_effects=_DATAFLOW),
    )(pltpu.with_memory_space_constraint(src, pltpu.HBM),
      pltpu.with_memory_space_constraint(lax.empty(land_shape, src.dtype), pltpu.HBM))


def _exchange_wait(handle, plan, after, name):
    send_sems, recv_sems, src_thru, land_thru, _ = handle

    def body(src_ref, land_ref, send_sems, recv_sems, after_ref, src_dead, got_ref):
        for cp in _exchange_copies(plan, src_ref, land_ref, send_sems, recv_sems):
            cp.wait_send()
            cp.wait_recv()

    return pl.pallas_call(
        body, name=name,
        out_shape=(pltpu.HBM(src_thru.shape, src_thru.dtype), pltpu.HBM(land_thru.shape, land_thru.dtype)),
        in_specs=(_HBM, _HBM, _SEM, _SEM, pl.BlockSpec(memory_space=pl.ANY)), out_specs=(_HBM, _HBM),
        input_output_aliases={0: 0, 1: 1}, compiler_params=pltpu.CompilerParams(has_side_effects=_DATAFLOW),
    )(src_thru, land_thru, send_sems, recv_sems, after)[1]


def _gather_plan(x, y, c):
    me = 4 * x + 2 * y + c
    return [(0, me, (x, y, 1 - c))] + [(0, me, (px, py, c)) for px, py in [(1 - x, y), (x, 1 - y), (1 - x, 1 - y)]]


def _gather_pass_on(land, shard, name):
    def body(land_in, x_ref, out_ref, send_sems, recv_sems, local_sem):
        x, y, c = _place()
        mine = pltpu.make_async_copy(x_ref, out_ref.at[4 * x + 2 * y + c], local_sem)
        mine.start()
        copies = []
        for j, (px, py) in enumerate([(1 - x, y), (x, 1 - y), (1 - x, 1 - y)]):
            block = out_ref.at[4 * px + 2 * py + c]
            copies.append(pltpu.make_async_remote_copy(
                src_ref=block, dst_ref=block, send_sem=send_sems.at[j], recv_sem=recv_sems.at[j],
                device_id=(x, y, 1 - c), device_id_type=MESH))
        for cp in copies:
            cp.start()
        for cp in copies:
            cp.wait()
        mine.wait()

    return pl.pallas_call(
        body, name=name, out_shape=jax.ShapeDtypeStruct(land.shape, land.dtype),
        in_specs=[pl.BlockSpec(memory_space=pl.ANY)] * 2, out_specs=pl.BlockSpec(memory_space=pl.ANY),
        scratch_shapes=[pltpu.SemaphoreType.DMA((3,)), pltpu.SemaphoreType.DMA((3,)), pltpu.SemaphoreType.DMA],
        input_output_aliases={0: 0},
    )(land, shard)


class _LateGather:
    def __init__(self, shard, name):
        self.shard, self.name = shard, name
        self.handle = _exchange_start(shard[None], 8, _gather_plan, name + "_start")

    def get(self, after):
        land = _exchange_wait(self.handle, _gather_plan, after, self.name + "_wait")
        return _gather_pass_on(land, self.shard, self.name + "_pass")


class _GradExchange:
    def __init__(self, cidx, qidx, layouts):
        self.cidx, self.qidx, self.layouts, self.pending = cidx, qidx, layouts, {}

    def start(self, tag, grad):
        g8 = self.layouts[tag](grad)
        got = _rs_to_sibling(g8, "rs_sibling_" + tag)
        chip_sums, chip_sums16 = _rs_add(g8, got, self.cidx, "rs_add_" + tag)
        handle = _exchange_start(chip_sums16, 3, _chip_plan, "rs_chips_start_" + tag)
        self.pending[tag] = (chip_sums, handle)
        return handle[4]

    def finish(self, tag, w, m, v, after):
        chip_sums, handle = self.pending.pop(tag)
        got2 = _exchange_wait(handle, _chip_plan, after, "rs_chips_wait_" + tag)
        return _adam_sharded(w, m, v, chip_sums, got2, self.qidx, "adam_" + tag)


def _local_grads(x, p0, p1, target, w_zs, w_a, w_qkv, late, conv_a, conv_b,
                 a_log, dt_bias, gdn_gain, lower_bounds, hgrn_gain, ln_g, ln_b, on_grad=None):
    H = a_log.shape[1]
    pad_small = ((0, 0), (H, LANES - 2 * H))
    alog_row = jnp.pad(a_log, pad_small)
    dtb_row = jnp.pad(dt_bias, pad_small)

    x16 = x.astype(BF16)
    proj_zs = _matmul(x16, w_zs, "nn", "proj_even_zs", after=late.started)
    proj_a = _matmul(x16, w_a, "nn", "proj_even_a", after=late.started)
    proj_qkv = _matmul(x16, w_qkv, "nn", "proj_even_qkv", after=late.started)
    y_a = _mixer_a_fwd(proj_a, conv_a)
    qkv = _conv_b_fwd(proj_qkv, conv_b)
    o2, s_gdn = _gdn_fwd(qkv, proj_zs, alog_row, dtb_row, gdn_gain, H)
    woute_a, woute_b = late.out_even(o2)
    wg, wpl = late.gate(o2)
    s_e = _matmul(o2, woute_b, "nn", "out_even_b", add=_matmul(y_a, woute_a, "nn", "out_even_a"))
    x2, x2_16 = _post_fwd(s_e, x, p0, ln_g[0:1], ln_b[0:1], wg[0], wpl[0], "post_even_fwd")
    wino, wouto = late.odd(s_e)
    nheads_o = wouto.shape[0] // HEAD
    proj_o = _matmul(x2_16, wino, "nn", "proj_odd")
    o4, s_hgrn = _hgrn_fwd(proj_o, lower_bounds, hgrn_gain, nheads_o)
    s_o = _matmul(o4, wouto, "nn", "out_odd")
    ds_o, dx2, dlng1, dlnb1, dwg1, dwpl1, loss = _post_bwd(
        s_o, x2, p1, ln_g[1:2], ln_b[1:2], wg[1], wpl[1], target, "post_odd_loss_bwd", True)
    do4 = _matmul(ds_o, wouto, "nt", "d_out_odd_act")
    grads = {}

    def emit(tag, grad):
        grads[tag] = grad
        return on_grad(tag, grad) if on_grad is not None else jnp.zeros(TOKEN_SHAPE, F32)

    tok = emit("w_out_odd", _matmul(o4, ds_o, "tn", "d_out_odd_w"))
    dproj_o, dlb, dhgain = _hgrn_bwd(proj_o, do4, s_hgrn, lower_bounds, hgrn_gain + tok[0:1], nheads_o)
    dx2 = _matmul(dproj_o, wino, "nt", "d_proj_odd_act", add=dx2)
    tok = emit("w_in_odd", _matmul(x2_16, dproj_o, "tn", "d_proj_odd_w"))
    ds_e, dx, dlng0, dlnb0, dwg0, dwpl0, _ = _post_bwd(
        s_e, x, p0, ln_g[0:1], ln_b[0:1] + tok[0:1, 0:1], wg[0], wpl[0], dx2, "post_even_bwd", False)
    tok = emit("w_pl_gate", jnp.stack([dwg0, dwg1])) + emit("w_pl", jnp.stack([dwpl0, dwpl1]))
    dy_a = _matmul(ds_e, woute_a, "nt", "d_out_even_a_act")
    do2 = _matmul(ds_e, woute_b, "nt", "d_out_even_b_act")
    dwoute_a = _matmul(y_a, ds_e, "tn", "d_out_even_a_w")
    dwoute_b = _matmul(o2, ds_e, "tn", "d_out_even_b_w")
    tok = tok + emit("w_out_even", jnp.concatenate([dwoute_a, dwoute_b], axis=0))
    dqkv, dproj_zs, dalog, ddtb, dggain = _gdn_bwd(qkv, proj_zs, do2, s_gdn, alog_row, dtb_row, gdn_gain + tok[0:1], H)
    dproj_qkv, dconv_b = _conv_b_bwd(proj_qkv, dqkv, conv_b)
    dproj_a, dconv_a = _mixer_a_bwd(proj_a, dy_a, conv_a)
    emit("conv", (dconv_a[:conv_a.shape[0]], dconv_b[:conv_b.shape[0]]))
    tok = emit("w_in_even", (_matmul(x16, dproj_zs, "tn", "d_proj_even_zs_w"), _matmul(x16, dproj_a, "tn", "d_proj_even_a_w"),
                             _matmul(x16, dproj_qkv, "tn", "d_proj_even_qkv_w")))
    dx = _matmul(dproj_zs, w_zs, "nt", "d_proj_even_zs_act", add=dx, after=tok)
    dx = _matmul(dproj_a, w_a, "nt", "d_proj_even_a_act", add=dx)
    dx = _matmul(dproj_qkv, w_qkv, "nt", "d_proj_even_qkv_act", add=dx)
    grads.update(
        loss=loss[0, 0], grad_x=dx, a_log=dalog[:, H:2 * H], dt_bias=ddtb[:, H:2 * H], gdn_gain=dggain,
        lower_bounds=dlb, hgrn_gain=dhgain, ln_g=jnp.concatenate([dlng0, dlng1], axis=0),
        ln_b=jnp.concatenate([dlnb0, dlnb1], axis=0))
    return grads


def _pad_rows(a, rows):
    return jnp.pad(a, ((0, rows - a.shape[0]), (0, 0)))


def _pack_small(a_log, dt_bias, gdn_gain, lower_bounds, hgrn_gain, ln_g, ln_b):
    lane_pad = lambda a: _pad_rows(jnp.pad(a, ((0, 0), (0, LANES - a.shape[1]))), SUBLANES)
    parts = [lane_pad(a_log), lane_pad(dt_bias), lane_pad(gdn_gain), lower_bounds.reshape(-1, LANES),
             lane_pad(hgrn_gain), ln_g.reshape(-1, LANES), ln_b.reshape(-1, LANES)]
    packed = jnp.concatenate(parts, axis=0)
    assert packed.shape[0] == SMALL_ROWS, packed.shape
    return packed


def _unpack_small(packed, shapes):
    out, r = [], 0
    for shp in shapes:
        n = shp[0] * shp[1]
        if n < LANES * SUBLANES and shp[1] <= LANES:
            out.append(packed[r:r + shp[0], :shp[1]])
            r += SUBLANES
        else:
            rows = n // LANES
            out.append(packed[r:r + rows].reshape(shp))
            r += rows
    return out


def _split_in_even(w_full, AW, HW, H):
    D = w_full.shape[0]
    n_a = 4 * AW
    n_main = n_a + 3 * HW
    w_zs = jnp.concatenate([w_full[:, n_main:n_main + HW], w_full[:, n_main + HW:],
                            jnp.zeros((D, LANES - 2 * H), w_full.dtype)], axis=1)
    w_a = w_full[:, :n_a].reshape(D, 4, AW // MIXER_LANES, MIXER_LANES).transpose(0, 2, 1, 3).reshape(D, n_a)
    return w_zs, w_a, w_full[:, n_a:n_main]


def _join_in_even(d_zs, d_a, d_qkv, AW, HW, H):
    D = d_a.shape[0]
    a_nat = d_a.reshape(D, AW // MIXER_LANES, 4, MIXER_LANES).transpose(0, 2, 1, 3).reshape(D, 4 * AW)
    return jnp.concatenate([a_nat, d_qkv, d_zs[:, :HW], d_zs[:, HW:HW + 2 * H]], axis=1)


def kernel(x, p, w_in_even, conv_a_w, conv_b_w, a_log, dt_bias, gdn_norm_g, w_out_even, w_in_odd, lower_bounds, hgrn_norm_g, w_out_odd, ln_g, ln_b, w_pl, w_pl_gate, loss_target, m_w_in_even, m_conv_a_w, m_conv_b_w, m_a_log, m_dt_bias, m_gdn_norm_g, m_w_out_even, m_w_in_odd, m_lower_bounds, m_hgrn_norm_g, m_w_out_odd, m_ln_g, m_ln_b, m_w_pl, m_w_pl_gate, v_w_in_even, v_conv_a_w, v_conv_b_w, v_a_log, v_dt_bias, v_gdn_norm_g, v_w_out_even, v_w_in_odd, v_lower_bounds, v_hgrn_norm_g, v_w_out_odd, v_ln_g, v_ln_b, v_w_pl, v_w_pl_gate):
    xi, yi, ci = _place()
    cidx = jnp.reshape(ci, (1,)).astype(jnp.int32)
    qidx = jnp.reshape(2 * xi + yi, (1,)).astype(jnp.int32)
    D = x.shape[2]
    H = a_log.shape[1]
    HW = H * HEAD
    AW = conv_a_w.shape[2] * 8
    OW = w_out_odd.shape[1] * 8
    PD = w_pl.shape[1]
    ka, kb = conv_a_w.shape[1], conv_b_w.shape[1]
    ca, cb = conv_a_w.shape[2], conv_b_w.shape[2]
    gw = HGRN_HEADS_PER_STEP * HEAD
    ngrp = OW // gw

    g_ine = _all_gather(w_in_even[0].astype(BF16), "ag_w_in_even")
    w_zs, w_a, w_qkv = _split_in_even(jnp.transpose(g_ine, (1, 0, 2)).reshape(D, -1), AW, HW, H)
    behind_first = lambda shard: lax.optimization_barrier((shard, g_ine))[0]
    late_oute = _LateGather(behind_first(w_out_even[0].astype(BF16)), "ag_w_out_even")
    late_gate = _LateGather(behind_first(w_pl_gate.astype(BF16).reshape(-1, D)), "ag_w_pl_gate")
    late_pl = _LateGather(behind_first(w_pl.astype(BF16).reshape(DEPTH * PD, -1)), "ag_w_pl")
    late_ino = _LateGather(behind_first(w_in_odd[0].astype(BF16)), "ag_w_in_odd")
    late_outo = _LateGather(behind_first(w_out_odd[0].astype(BF16)), "ag_w_out_odd")

    class _Late:
        started = sum(g.handle[4] for g in (late_oute, late_gate, late_pl, late_ino, late_outo))

        @staticmethod
        def out_even(after):
            woute = late_oute.get(after).reshape(-1, D)
            return woute[:AW], woute[AW:]

        @staticmethod
        def gate(after):
            g_gate, g_pl = late_gate.get(after), late_pl.get(after)
            return (g_gate.reshape(8, DEPTH, D // 8, D).transpose(1, 0, 2, 3).reshape(DEPTH, D, D),
                    g_pl.reshape(8, DEPTH, PD, D // 8).transpose(1, 2, 0, 3).reshape(DEPTH, PD, D))

        @staticmethod
        def odd(after):
            g_ino = late_ino.get(after)
            wino = jnp.transpose(g_ino, (1, 0, 2)).reshape(D, 4, ngrp, gw).transpose(0, 2, 1, 3).reshape(D, 4 * OW)
            return wino, late_outo.get(after).reshape(-1, D)

    taps = jnp.concatenate([_pad_rows(conv_a_w[0], SUBLANES), _pad_rows(conv_b_w[0], SUBLANES)], axis=1)
    g_taps = _all_gather(taps, "ag_conv")
    conv_a = jnp.transpose(g_taps[:, :ka, :ca], (1, 0, 2)).reshape(ka, 8 * ca)
    conv_b = jnp.transpose(g_taps[:, :kb, ca:], (1, 0, 2)).reshape(kb, 8 * cb)

    sh = w_in_even.shape[2]
    tap_blocks = lambda g, width: _pad_rows(g, SUBLANES).reshape(SUBLANES, 8, width).transpose(1, 0, 2)
    owner_layout = {
        "w_in_even": lambda g: _join_in_even(*g, AW, HW, H).reshape(D, 8, sh).transpose(1, 0, 2),
        "w_in_odd": lambda g: g.reshape(D, ngrp, 4, gw).transpose(0, 2, 1, 3).reshape(D, 8, 4 * OW // 8).transpose(1, 0, 2),
        "w_out_even": lambda g: g.reshape(8, -1, D),
        "w_out_odd": lambda g: g.reshape(8, -1, D),
        "w_pl_gate": lambda g: g.reshape(DEPTH, 8, D // 8, D).transpose(1, 0, 2, 3).reshape(8, DEPTH * D // 8, D),
        "w_pl": lambda g: g.reshape(DEPTH, PD, 8, D // 8).transpose(2, 0, 1, 3).reshape(8, DEPTH * PD, D // 8),
        "conv": lambda g: jnp.concatenate([tap_blocks(g[0], ca), tap_blocks(g[1], cb)], axis=2),
    }
    exchange = _GradExchange(cidx, qidx, owner_layout)
    gr = _local_grads(x[0], p[0, 0], p[1, 0], loss_target[0], w_zs, w_a, w_qkv, _Late, conv_a, conv_b,
                      a_log, dt_bias, gdn_norm_g, lower_bounds, hgrn_norm_g, ln_g, ln_b, on_grad=exchange.start)
    loss = lax.psum(gr["loss"], AXES)

    last = gr["grad_x"]
    flat_gate = lambda a: a.reshape(DEPTH * D // 8, D)
    flat_pl = lambda a: a.reshape(DEPTH * PD, D // 8)
    pack_taps = lambda a, b: jnp.concatenate([_pad_rows(a[0], SUBLANES), _pad_rows(b[0], SUBLANES)], axis=1)
    o_outo = exchange.finish("w_out_odd", w_out_odd[0], m_w_out_odd[0], v_w_out_odd[0], last)
    o_ino = exchange.finish("w_in_odd", w_in_odd[0], m_w_in_odd[0], v_w_in_odd[0], last)
    o_gate = exchange.finish("w_pl_gate", flat_gate(w_pl_gate), flat_gate(m_w_pl_gate), flat_gate(v_w_pl_gate), last)
    o_pl = exchange.finish("w_pl", flat_pl(w_pl), flat_pl(m_w_pl), flat_pl(v_w_pl), last)
    o_oute = exchange.finish("w_out_even", w_out_even[0], m_w_out_even[0], v_w_out_even[0], last)
    o_taps = exchange.finish("conv", taps, pack_taps(m_conv_a_w, m_conv_b_w), pack_taps(v_conv_a_w, v_conv_b_w), last)
    others_done = sum(o[1][0:1, 0:1] for o in (o_outo, o_ino, o_gate, o_pl, o_oute, o_taps))
    o_ine = exchange.finish("w_in_even", w_in_even[0], m_w_in_even[0], v_w_in_even[0], others_done)

    small_g = _pack_small(gr["a_log"], gr["dt_bias"], gr["gdn_gain"], gr["lower_bounds"], gr["hgrn_gain"],
                          gr["ln_g"], gr["ln_b"])
    o_small = _adam_replicated(
        _pack_small(a_log, dt_bias, gdn_norm_g, lower_bounds, hgrn_norm_g, ln_g, ln_b),
        _pack_small(m_a_log, m_dt_bias, m_gdn_norm_g, m_lower_bounds, m_hgrn_norm_g, m_ln_g, m_ln_b),
        _pack_small(v_a_log, v_dt_bias, v_gdn_norm_g, v_lower_bounds, v_hgrn_norm_g, v_ln_g, v_ln_b),
        _all_gather(small_g, "ag_small_grads"))
    small_shapes = [a_log.shape, dt_bias.shape, gdn_norm_g.shape, lower_bounds.shape, hgrn_norm_g.shape,
                    ln_g.shape, ln_b.shape]

    def leaves(kind):
        s_alog, s_dt, s_gg, s_lb, s_hg, s_lng, s_lnb = _unpack_small(o_small[kind], small_shapes)
        t = o_taps[kind]
        return [o_ine[kind][None], t[None, :ka, :ca], t[None, :kb, ca:], s_alog, s_dt, s_gg, o_oute[kind][None],
                o_ino[kind][None], s_lb, s_hg, o_outo[kind][None], s_lng, s_lnb,
                o_pl[kind].reshape(w_pl.shape), o_gate[kind].reshape(w_pl_gate.shape)]

    return (loss, gr["grad_x"][None], *leaves(0), *leaves(1), *leaves(2), *leaves(3))
```

```python
import functools

import jax
import jax.numpy as jnp
from jax import lax
from jax.experimental import pallas as pl
from jax.experimental.pallas import tpu as pltpu

F32 = jnp.float32
BF16 = jnp.bfloat16
MESH = pl.DeviceIdType.MESH
AXES = ("x", "y", "c")

LANES = 128
SUBLANES = 8
HEAD = 128
GDN_CHUNK = 64
HGRN_CHUNK = 32
HGRN_SUB = 16
HGRN_HEADS_PER_STEP = 16
NORM_EPS = 1e-5
DEPTH = 2
ALPHA = (2.0 * DEPTH) ** 0.25
EXP_CLAMP = 80.0
ADAM_LR, ADAM_B1, ADAM_B2, ADAM_EPS, ADAM_WD, ADAM_STEP = 0.001, 0.9, 0.999, 1e-08, 0.01, 10
VMEM_LIMIT = 56 * 1024 * 1024
MATMUL_VMEM = 36 * 1024 * 1024
ROW_TILE = 512
MIXER_LANES = 256
CONV_LANES = 512
POST_TILE = 256
SMALL_ROWS = 96

_NOBATCH, _BATCH0 = ((), ()), ((0,), (0,))
_DIMS = {"nn": (((1,), (0,)), _NOBATCH), "nt": (((1,), (1,)), _NOBATCH), "tn": (((0,), (0,)), _NOBATCH),
         "bnn": (((2,), (1,)), _BATCH0), "bnt": (((2,), (2,)), _BATCH0), "btn": (((1,), (1,)), _BATCH0)}


def _params(**kw):
    return pltpu.CompilerParams(vmem_limit_bytes=VMEM_LIMIT, **kw)


def _dot_raw(a, b, kind, hi):
    if hi:
        return lax.dot_general(a, b, _DIMS[kind], precision=lax.Precision.HIGHEST, preferred_element_type=F32)
    return lax.dot_general(a.astype(BF16), b.astype(BF16), _DIMS[kind], preferred_element_type=F32)


@functools.partial(jax.custom_vjp, nondiff_argnums=(2, 3))
def mdot(a, b, kind, hi):
    return _dot_raw(a, b, kind, hi)


def _mdot_fwd(a, b, kind, hi):
    return _dot_raw(a, b, kind, hi), (a, b)


def _mdot_bwd(kind, hi, res, g):
    a, b = res
    pre, base = kind[:-2], kind[-2:]
    if base == "nn":
        return _dot_raw(g, b, pre + "nt", hi), _dot_raw(a, g, pre + "tn", hi)
    if base == "nt":
        return _dot_raw(g, b, pre + "nn", hi), _dot_raw(g, a, pre + "tn", hi)
    return _dot_raw(b, g, pre + "nt", hi), _dot_raw(a, g, pre + "nn", hi)


mdot.defvjp(_mdot_fwd, _mdot_bwd)


def _rows(x, lo, hi):
    return _take_rows(x, lo, hi, x.shape[-2])


@functools.partial(jax.custom_vjp, nondiff_argnums=(1, 2, 3))
def _take_rows(x, lo, hi, n):
    return x[..., lo:hi, :]


def _take_rows_fwd(x, lo, hi, n):
    return x[..., lo:hi, :], None


def _take_rows_bwd(lo, hi, n, _, g):
    parts = []
    if lo > 0:
        parts.append(jnp.zeros(g.shape[:-2] + (lo, g.shape[-1]), g.dtype))
    parts.append(g)
    if n - hi > 0:
        parts.append(jnp.zeros(g.shape[:-2] + (n - hi, g.shape[-1]), g.dtype))
    return (jnp.concatenate(parts, axis=-2) if len(parts) > 1 else g,)


_take_rows.defvjp(_take_rows_fwd, _take_rows_bwd)


def _heads_of(wide, nheads):
    return jnp.stack([wide[:, h * HEAD:(h + 1) * HEAD] for h in range(nheads)], axis=0)


def _wide_of(x):
    return jnp.concatenate([x[h] for h in range(x.shape[0])], axis=1)


@functools.partial(jax.custom_vjp, nondiff_argnums=(1,))
def to_heads(wide, nheads):
    return _heads_of(wide, nheads)


to_heads.defvjp(lambda wide, nheads: (_heads_of(wide, nheads), None), lambda nheads, _, g: (_wide_of(g),))


@jax.custom_vjp
def to_wide(x):
    return _wide_of(x)


to_wide.defvjp(lambda x: (_wide_of(x), None), lambda _, g: (_heads_of(g, g.shape[1] // HEAD),))


def _sigmoid(x):
    return jax.nn.sigmoid(x)


def _silu(x):
    return x * _sigmoid(x)


def _dsilu(x):
    s = _sigmoid(x)
    return s * (1.0 + x * (1.0 - s))


def _log1p(u):
    return jnp.where(u < 1e-4, u * (1.0 - 0.5 * u), jnp.log(1.0 + u))


def _softplus(x):
    return jnp.maximum(x, 0.0) + _log1p(jnp.exp(-jnp.abs(x)))


def _rms_gate(o, gain, z):
    return o * lax.rsqrt(jnp.mean(o * o, axis=-1, keepdims=True) + NORM_EPS) * gain * _silu(z)


def _l2n(x):
    return x * lax.rsqrt(jnp.sum(x * x, axis=-1, keepdims=True) + 1e-6)


def _split_dot_raw(m, x, kind):
    mb = m.astype(BF16)
    hi = x.astype(BF16)
    lo = (x - hi.astype(F32)).astype(BF16)
    dims = _DIMS[kind]
    return (lax.dot_general(mb, hi, dims, preferred_element_type=F32)
            + lax.dot_general(mb, lo, dims, preferred_element_type=F32))


@jax.custom_vjp
def mask_dot(m, x):
    return _split_dot_raw(m, x, "nn")


def _mask_dot_fwd(m, x):
    return _split_dot_raw(m, x, "nn"), m


def _mask_dot_bwd(m, g):
    return jnp.zeros_like(m), _split_dot_raw(m, g, "tn")


mask_dot.defvjp(_mask_dot_fwd, _mask_dot_bwd)


def _unit_lower_inverse_minus_eye(low, n):
    rest = -low
    power = low
    span = 2
    while span < n:
        power = mdot(power, power, "bnn", False)
        rest = rest + power + mdot(rest, power, "bnn", False)
        span *= 2
    return rest


def _gdn_step(S, q, k, v, z, small, alog, dtb, gain):
    H = S.shape[0]
    C = GDN_CHUNK
    row = lax.broadcasted_iota(jnp.int32, (C, C), 0)
    col = lax.broadcasted_iota(jnp.int32, (C, C), 1)
    tril, strict, eye = (row >= col)[None], (row > col)[None], (row == col)[None]
    head = lax.broadcasted_iota(jnp.int32, (H, 1, LANES), 0)
    lane = lax.broadcasted_iota(jnp.int32, (H, 1, LANES), 2)
    rowc = lax.broadcasted_iota(jnp.int32, (1, C, 1), 1)
    beta_all = _sigmoid(small)
    g_all = -jnp.exp(alog) * _softplus(small + dtb)
    gc_all = mask_dot((row >= col).astype(F32), g_all)
    beta = jnp.sum(jnp.where(lane == head, beta_all[None], 0.0), axis=-1, keepdims=True)
    gc = jnp.sum(jnp.where(lane == head + H, gc_all[None], 0.0), axis=-1, keepdims=True)
    gc_row = jnp.sum(jnp.where(eye, gc, 0.0), axis=1, keepdims=True)
    decay = jnp.where(tril, jnp.exp(jnp.where(tril, gc - gc_row, 0.0)), 0.0)
    g_last = jnp.sum(jnp.where(rowc == C - 1, gc, 0.0), axis=1, keepdims=True)
    qn = _l2n(q) * (HEAD ** -0.5)
    kn = _l2n(k)
    kb = kn * beta
    low = jnp.where(strict, mdot(kb, kn, "bnt", False) * decay, 0.0)
    inv_rest = _unit_lower_inverse_minus_eye(low, C)
    eg = jnp.exp(gc)
    vb, kbe = v * beta, kb * eg
    u = vb + mdot(inv_rest, vb, "bnn", False)
    w = kbe + mdot(inv_rest, kbe, "bnn", False)
    attn = mdot(qn, kn, "bnt", False) * decay
    v_new = u - mdot(w, S, "bnn", False)
    o = mdot(qn * eg, S, "bnn", False) + mdot(attn, v_new, "bnn", False)
    k_dec = kn * jnp.exp(g_last - gc)
    return _rms_gate(o, gain, z), S * jnp.exp(g_last) + mdot(k_dec, v_new, "btn", False)


def _hgrn_step(St, qr, fr, vi, z, lb0, lb1, gain):
    H = St.shape[0]
    C, SB = HGRN_CHUNK, HGRN_SUB
    row = lax.broadcasted_iota(jnp.int32, (C, C), 0)
    col = lax.broadcasted_iota(jnp.int32, (C, C), 1)
    blk_start = row - (row & (SB - 1))
    in_blk_f = ((row >= col) & (col >= blk_start)).astype(F32)
    before_f = (col < blk_start).astype(F32)
    sums_f = jnp.concatenate([in_blk_f, before_f], axis=0)
    m = jnp.maximum(lb0, lb1)
    e0, e1 = jnp.exp(lb0 - m), jnp.exp(lb1 - m)
    lb = e1 / (e0 + e1)
    f = lb + (1.0 - lb) * _sigmoid(fr)
    q = _silu(qr)
    k = 1.0 - f
    logf = jnp.log(f)
    sums = mask_dot(sums_f, to_wide(logf))
    inner, start = to_heads(_rows(sums, 0, C), H), to_heads(_rows(sums, C, 2 * C), H)
    b = start + inner
    b_last = jnp.sum(logf, axis=1, keepdims=True)
    o = mdot(q * jnp.exp(b), St, "bnt", False)
    qt = q * jnp.exp(inner)
    parts = []
    for blk in range(C // SB):
        lo, n = blk * SB, (blk + 1) * SB
        ref = jnp.concatenate([_rows(start, lo, n)] * (blk + 1), axis=1)
        kt = _rows(k, 0, n) * jnp.exp(jnp.minimum(ref - _rows(b, 0, n), EXP_CLAMP))
        att = mdot(_rows(qt, lo, n), kt, "bnt", False)
        t_idx = lax.broadcasted_iota(jnp.int32, (1, SB, n), 1) + lo
        s_idx = lax.broadcasted_iota(jnp.int32, (1, SB, n), 2)
        att = jnp.where(s_idx <= t_idx, att, 0.0)
        parts.append(mdot(att, _rows(vi, 0, n), "bnn", False))
    o = o + jnp.concatenate(parts, axis=1)
    k_dec = k * jnp.exp(b_last - b)
    return _rms_gate(o, gain, z), St * jnp.exp(b_last) + mdot(vi, k_dec, "btn", False)


def _post_norm(s, x, g, b):
    r = ALPHA * x + s
    d = r - jnp.mean(r, axis=-1, keepdims=True)
    var = jnp.mean(d * d, axis=-1, keepdims=True)
    return d * lax.rsqrt(var + NORM_EPS) * g + b


def _post_gate(x1, gate_pre, pp):
    return x1 + pp * _sigmoid(gate_pre)


def _pick(dim, cands):
    for c in cands:
        if dim % c == 0:
            return c
    return dim


def _matmul_tiles(M, K, tn, a_bytes, b_bytes, has_add):
    for tk in (4096, 2048, 1536, 1152, 1024, 640, 512, 384, 256, 128):
        if K % tk:
            continue
        for tm in (2048, 1024, 512, 256, 128):
            if M % tm:
                continue
            blocks = tm * tk * a_bytes + tk * tn * b_bytes + tm * tn * 4 * (2 if has_add else 1)
            if 2 * blocks + (tm * tn * 4 if tk < K else 0) <= MATMUL_VMEM and tm >= min(M, 1024):
                return tm, tk
    return _pick(M, (512, 256, 128)), _pick(K, (512, 256, 128))


def _matmul(a, b, kind, name, add=None, after=None):
    if kind == "nn":
        (M, K), N = a.shape, b.shape[1]
    elif kind == "nt":
        (M, K), N = a.shape, b.shape[0]
    else:
        (K, M), N = a.shape, b.shape[1]
    has_add = add is not None
    tn = _pick(N, (512, 640, 384, 256, 128))
    tm, tk = _matmul_tiles(M, K, tn, a.dtype.itemsize, b.dtype.itemsize, has_add)
    nk = K // tk
    a_spec = pl.BlockSpec((tk, tm), lambda i, j, k: (k, i)) if kind == "tn" else pl.BlockSpec((tm, tk), lambda i, j, k: (i, k))
    b_spec = pl.BlockSpec((tn, tk), lambda i, j, k: (j, k)) if kind == "nt" else pl.BlockSpec((tk, tn), lambda i, j, k: (k, j))
    o_spec = pl.BlockSpec((tm, tn), lambda i, j, k: (i, j))

    extra = ([add] if has_add else []) + ([after] if after is not None else [])
    extra_specs = ([o_spec] if has_add else []) + ([pl.BlockSpec(TOKEN_SHAPE, lambda i, j, k: (0, 0))] if after is not None else [])

    def body(a_ref, b_ref, *rest):
        add_ref = rest[0] if has_add else None
        o_ref = rest[len(extra)]
        part = _dot_raw(a_ref[...], b_ref[...], kind, False)
        if nk == 1:
            o_ref[...] = part + add_ref[...] if has_add else part
            return
        acc = rest[-1]
        kk = pl.program_id(2)

        @pl.when(kk == 0)
        def _():
            acc[...] = part

        @pl.when(kk > 0)
        def _():
            acc[...] += part

        @pl.when(kk == nk - 1)
        def _():
            o_ref[...] = acc[...] + add_ref[...] if has_add else acc[...]

    return pl.pallas_call(
        body, name=name, grid=(M // tm, N // tn, nk),
        in_specs=[a_spec, b_spec] + extra_specs,
        out_specs=o_spec, out_shape=jax.ShapeDtypeStruct((M, N), F32),
        scratch_shapes=[pltpu.VMEM((tm, tn), F32)] if nk > 1 else [],
        compiler_params=_params(dimension_semantics=("parallel", "parallel", "arbitrary")),
    )(a, b, *extra)


def _halo_specs(ts, nt, width, prev=True, main=True, nxt=True):
    per = ts // SUBLANES
    last8 = nt * per - 1
    specs = []
    if prev:
        specs.append(pl.BlockSpec((SUBLANES, width), lambda cb, i: (jnp.maximum(i * per - 1, 0), cb)))
    if main:
        specs.append(pl.BlockSpec((ts, width), lambda cb, i: (i, cb)))
    if nxt:
        specs.append(pl.BlockSpec((SUBLANES, width), lambda cb, i: (jnp.minimum((i + 1) * per, last8), cb)))
    return specs


def _taps(ext, ktaps, lo, size):
    return [ext[lo:lo + size] if j == 0 else pltpu.roll(ext, j, 0)[lo:lo + size] for j in range(ktaps)]


def _ahead(ext, j, size):
    n = ext.shape[0]
    return ext[:size] if j == 0 else pltpu.roll(ext, n - j, 0)[:size]


def _lane_block(ref, k):
    return ref[:, k * MIXER_LANES:(k + 1) * MIXER_LANES]


def _mixer_a_fwd(proj_a, conv_w):
    T = proj_a.shape[0]
    nblk = proj_a.shape[1] // (4 * MIXER_LANES)
    ts = min(ROW_TILE, T)
    nt = T // ts

    def body(pp, pm, w_ref, y_ref):
        i = pl.program_id(1)
        u_prev = jnp.where(i > 0, _lane_block(pp, 0) * _lane_block(pp, 1), 0.0)
        ext = jnp.concatenate([u_prev, _lane_block(pm, 0) * _lane_block(pm, 1)], axis=0)
        t0, t1, t2 = _taps(ext, 3, SUBLANES, ts)
        cv = w_ref[2:3, :] * t0 + w_ref[1:2, :] * t1 + w_ref[0:1, :] * t2
        y_ref[...] = (_lane_block(pm, 2) * cv * _silu(_lane_block(pm, 3))).astype(y_ref.dtype)

    return pl.pallas_call(
        body, name="mixer_a_fwd", grid=(nblk, nt),
        in_specs=_halo_specs(ts, nt, 4 * MIXER_LANES, nxt=False)
        + [pl.BlockSpec((conv_w.shape[0], MIXER_LANES), lambda cb, i: (0, cb))],
        out_specs=pl.BlockSpec((ts, MIXER_LANES), lambda cb, i: (i, cb)),
        out_shape=jax.ShapeDtypeStruct((T, nblk * MIXER_LANES), BF16), compiler_params=_params(),
    )(proj_a, proj_a, conv_w)


def _mixer_a_bwd(proj_a, dy, conv_w):
    T = proj_a.shape[0]
    nblk = proj_a.shape[1] // (4 * MIXER_LANES)
    ts = min(ROW_TILE, T)
    nt = T // ts
    kt = conv_w.shape[0]

    def body(pp, pm, pn, dym, dyn, w_ref, dp_ref, dw_ref):
        i = pl.program_id(1)
        hm, cm, bm, zm = (_lane_block(pm, k) for k in range(4))
        u_prev = jnp.where(i > 0, _lane_block(pp, 0) * _lane_block(pp, 1), 0.0)
        ext = jnp.concatenate([u_prev, hm * cm], axis=0)
        dy_ext = jnp.concatenate([dym[...], jnp.where(i < nt - 1, dyn[...], 0.0)], axis=0)
        b_ext = jnp.concatenate([bm, _lane_block(pn, 2)], axis=0)
        sz_ext = _silu(jnp.concatenate([zm, _lane_block(pn, 3)], axis=0))
        dcv_ext = dy_ext * b_ext * sz_ext
        w = [w_ref[j:j + 1, :] for j in range(kt)]
        du = sum(w[kt - 1 - j] * _ahead(dcv_ext, j, ts) for j in range(kt))
        taps = _taps(ext, kt, SUBLANES, ts)
        cv = sum(w[kt - 1 - j] * taps[j] for j in range(kt))
        for part, d in enumerate((du * cm, du * hm, dym[...] * cv * sz_ext[:ts], dym[...] * bm * cv * _dsilu(zm))):
            dp_ref[:, part * MIXER_LANES:(part + 1) * MIXER_LANES] = d.astype(dp_ref.dtype)
        dcv = dcv_ext[:ts]

        @pl.when(i == 0)
        def _():
            dw_ref[...] = jnp.zeros_like(dw_ref)

        for j in range(kt):
            dw_ref[j:j + 1, :] += jnp.sum(dcv * taps[kt - 1 - j], axis=0, keepdims=True)

    return pl.pallas_call(
        body, name="mixer_a_bwd", grid=(nblk, nt),
        in_specs=_halo_specs(ts, nt, 4 * MIXER_LANES) + _halo_specs(ts, nt, MIXER_LANES, prev=False)
        + [pl.BlockSpec((kt, MIXER_LANES), lambda cb, i: (0, cb))],
        out_specs=[pl.BlockSpec((ts, 4 * MIXER_LANES), lambda cb, i: (i, cb)),
                   pl.BlockSpec((SUBLANES, MIXER_LANES), lambda cb, i: (0, cb))],
        out_shape=[jax.ShapeDtypeStruct(proj_a.shape, BF16),
                   jax.ShapeDtypeStruct((SUBLANES, nblk * MIXER_LANES), F32)],
        compiler_params=_params(),
    )(proj_a, proj_a, proj_a, dy, dy, conv_w)


def _conv_b_fwd(raw, conv_w):
    T = raw.shape[0]
    nblk = raw.shape[1] // CONV_LANES
    ts = min(ROW_TILE, T)
    nt = T // ts
    kt = conv_w.shape[0]

    def body(rp, rm, w_ref, y_ref):
        i = pl.program_id(1)
        ext = jnp.concatenate([jnp.where(i > 0, rp[...], 0.0), rm[...]], axis=0)
        taps = _taps(ext, kt, SUBLANES, ts)
        y_ref[...] = _silu(sum(w_ref[kt - 1 - j:kt - j, :] * taps[j] for j in range(kt)))

    return pl.pallas_call(
        body, name="conv_b_fwd", grid=(nblk, nt),
        in_specs=_halo_specs(ts, nt, CONV_LANES, nxt=False) + [pl.BlockSpec((kt, CONV_LANES), lambda cb, i: (0, cb))],
        out_specs=pl.BlockSpec((ts, CONV_LANES), lambda cb, i: (i, cb)),
        out_shape=jax.ShapeDtypeStruct(raw.shape, F32), compiler_params=_params(),
    )(raw, raw, conv_w)


def _conv_b_bwd(raw, dy, conv_w):
    T = raw.shape[0]
    nblk = raw.shape[1] // CONV_LANES
    ts = min(ROW_TILE, T)
    nt = T // ts
    kt = conv_w.shape[0]

    def body(rp, rm, rn, dym, dyn, w_ref, dr_ref, dw_ref):
        i = pl.program_id(1)
        ext = jnp.concatenate([jnp.where(i > 0, rp[...], 0.0), rm[...], rn[...]], axis=0)
        w = [w_ref[j:j + 1, :] for j in range(kt)]
        taps = _taps(ext, kt, SUBLANES, ts + SUBLANES)
        xc_ext = sum(w[kt - 1 - j] * taps[j] for j in range(kt))
        dy_ext = jnp.concatenate([dym[...], jnp.where(i < nt - 1, dyn[...], 0.0)], axis=0)
        dxc_ext = dy_ext * _dsilu(xc_ext)
        dr_ref[...] = sum(w[kt - 1 - j] * _ahead(dxc_ext, j, ts) for j in range(kt)).astype(dr_ref.dtype)
        dxc = dxc_ext[:ts]

        @pl.when(i == 0)
        def _():
            dw_ref[...] = jnp.zeros_like(dw_ref)

        for j in range(kt):
            dw_ref[j:j + 1, :] += jnp.sum(dxc * taps[kt - 1 - j][:ts], axis=0, keepdims=True)

    return pl.pallas_call(
        body, name="conv_b_bwd", grid=(nblk, nt),
        in_specs=_halo_specs(ts, nt, CONV_LANES) + _halo_specs(ts, nt, CONV_LANES, prev=False)
        + [pl.BlockSpec((kt, CONV_LANES), lambda cb, i: (0, cb))],
        out_specs=[pl.BlockSpec((ts, CONV_LANES), lambda cb, i: (i, cb)),
                   pl.BlockSpec((SUBLANES, CONV_LANES), lambda cb, i: (0, cb))],
        out_shape=[jax.ShapeDtypeStruct(raw.shape, BF16), jax.ShapeDtypeStruct((SUBLANES, nblk * CONV_LANES), F32)],
        compiler_params=_params(),
    )(raw, raw, raw, dy, dy, conv_w)


def _split_heads(ref, base, nheads, rows=slice(None)):
    return jnp.stack([ref[rows, base + h * HEAD: base + (h + 1) * HEAD] for h in range(nheads)], axis=0)


def _store_heads(ref, base, x, rows=slice(None), accumulate=False):
    for h in range(x.shape[0]):
        lanes = slice(base + h * HEAD, base + (h + 1) * HEAD)
        if accumulate:
            ref[rows, lanes] += x[h]
        else:
            ref[rows, lanes] = x[h].astype(ref.dtype)


def _gdn_fwd(qkv, proj_zs, alog, dtb, gain, H):
    T = qkv.shape[0]
    C, HW = GDN_CHUNK, H * HEAD
    nc = T // C
    zw = HW + LANES

    def body(qkv_ref, zs_ref, alog_ref, dtb_ref, gain_ref, o_ref, sall_ref, s_scr):
        @pl.when(pl.program_id(0) == 0)
        def _():
            s_scr[...] = jnp.zeros_like(s_scr)

        sall_ref[0] = s_scr[...]
        outs, states = _gdn_step(
            s_scr[...], _split_heads(qkv_ref, 0, H), _split_heads(qkv_ref, HW, H),
            _split_heads(qkv_ref, 2 * HW, H), _split_heads(zs_ref, 0, H), zs_ref[:, HW:HW + LANES],
            alog_ref[...], dtb_ref[...], gain_ref[...])
        _store_heads(o_ref, 0, outs)
        s_scr[...] = states

    row = pl.BlockSpec((1, LANES), lambda i: (0, 0))
    return pl.pallas_call(
        body, name="gdn_fwd", grid=(nc,),
        in_specs=[pl.BlockSpec((C, 3 * HW), lambda i: (i, 0)), pl.BlockSpec((C, zw), lambda i: (i, 0)), row, row, row],
        out_specs=[pl.BlockSpec((C, HW), lambda i: (i, 0)), pl.BlockSpec((1, H, HEAD, HEAD), lambda i: (i, 0, 0, 0))],
        out_shape=[jax.ShapeDtypeStruct((T, HW), BF16), jax.ShapeDtypeStruct((nc, H, HEAD, HEAD), F32)],
        scratch_shapes=[pltpu.VMEM((H, HEAD, HEAD), F32)], compiler_params=_params(),
    )(qkv, proj_zs, alog, dtb, gain)


def _gdn_bwd(qkv, proj_zs, do, s_all, alog, dtb, gain, H):
    T = qkv.shape[0]
    C, HW = GDN_CHUNK, H * HEAD
    nc = T // C
    zw = HW + LANES

    def body(qkv_ref, zs_ref, do_ref, sin_ref, alog_ref, dtb_ref, gain_ref,
             dqkv_ref, dzs_ref, dalog_ref, ddtb_ref, dgain_ref, ds_scr):
        @pl.when(pl.program_id(0) == 0)
        def _():
            ds_scr[...] = jnp.zeros_like(ds_scr)
            dalog_ref[...] = jnp.zeros_like(dalog_ref)
            ddtb_ref[...] = jnp.zeros_like(ddtb_ref)
            dgain_ref[...] = jnp.zeros_like(dgain_ref)

        primals = (sin_ref[0], _split_heads(qkv_ref, 0, H),
                   _split_heads(qkv_ref, HW, H), _split_heads(qkv_ref, 2 * HW, H), _split_heads(zs_ref, 0, H),
                   zs_ref[:, HW:HW + LANES], alog_ref[...], dtb_ref[...], gain_ref[...])
        _, vjp = jax.vjp(_gdn_step, *primals)
        dS, dq, dk, dv, dz, dsmall, dalog, ddtb, dgain = vjp((_split_heads(do_ref, 0, H), ds_scr[...]))
        ds_scr[...] = dS
        _store_heads(dqkv_ref, 0, dq)
        _store_heads(dqkv_ref, HW, dk)
        _store_heads(dqkv_ref, 2 * HW, dv)
        _store_heads(dzs_ref, 0, dz)
        dzs_ref[:, HW:HW + LANES] = dsmall.astype(dzs_ref.dtype)
        dalog_ref[...] += dalog
        ddtb_ref[...] += ddtb
        dgain_ref[...] += dgain

    row = pl.BlockSpec((1, LANES), lambda i: (0, 0))
    rev = lambda i: nc - 1 - i
    return pl.pallas_call(
        body, name="gdn_bwd", grid=(nc,),
        in_specs=[pl.BlockSpec((C, 3 * HW), lambda i: (rev(i), 0)), pl.BlockSpec((C, zw), lambda i: (rev(i), 0)),
                  pl.BlockSpec((C, HW), lambda i: (rev(i), 0)),
                  pl.BlockSpec((1, H, HEAD, HEAD), lambda i: (rev(i), 0, 0, 0)), row, row, row],
        out_specs=[pl.BlockSpec((C, 3 * HW), lambda i: (rev(i), 0)), pl.BlockSpec((C, zw), lambda i: (rev(i), 0)),
                   row, row, row],
        out_shape=[jax.ShapeDtypeStruct(qkv.shape, F32), jax.ShapeDtypeStruct(proj_zs.shape, BF16)]
        + [jax.ShapeDtypeStruct((1, LANES), F32)] * 3,
        scratch_shapes=[pltpu.VMEM((H, HEAD, HEAD), F32)], compiler_params=_params(),
    )(qkv, proj_zs, do, s_all, alog, dtb, gain)


def _hgrn_refs(proj_ref, lb_ref, HP):
    W = HP * HEAD
    return (_split_heads(proj_ref, 0, HP), _split_heads(proj_ref, W, HP), _split_heads(proj_ref, 2 * W, HP),
            _split_heads(proj_ref, 3 * W, HP), _split_heads(lb_ref, 0, HP, slice(0, 1)),
            _split_heads(lb_ref, 0, HP, slice(1, 2)))


def _hgrn_fwd(proj, lower_bounds, gain, nheads):
    T = proj.shape[0]
    C, HP = HGRN_CHUNK, HGRN_HEADS_PER_STEP
    ng, nc, W = nheads // HP, T // C, HP * HEAD

    def body(proj_ref, lb_ref, gain_ref, o_ref, sall_ref, s_scr):
        @pl.when(pl.program_id(1) == 0)
        def _():
            s_scr[...] = jnp.zeros_like(s_scr)

        sall_ref[0] = s_scr[...]
        qr, fr, vi, z, lb0, lb1 = _hgrn_refs(proj_ref, lb_ref, HP)
        outs, states = _hgrn_step(s_scr[...], qr, fr, vi, z, lb0, lb1, gain_ref[...])
        _store_heads(o_ref, 0, outs)
        s_scr[...] = states

    return pl.pallas_call(
        body, name="hgrn_fwd", grid=(ng, nc),
        in_specs=[pl.BlockSpec((C, 4 * W), lambda g, i: (i, g)), pl.BlockSpec((2, W), lambda g, i: (0, g)),
                  pl.BlockSpec((1, LANES), lambda g, i: (0, 0))],
        out_specs=[pl.BlockSpec((C, W), lambda g, i: (i, g)),
                   pl.BlockSpec((1, HP, HEAD, HEAD), lambda g, i: (i, g, 0, 0))],
        out_shape=[jax.ShapeDtypeStruct((T, nheads * HEAD), BF16), jax.ShapeDtypeStruct((nc, nheads, HEAD, HEAD), F32)],
        scratch_shapes=[pltpu.VMEM((HP, HEAD, HEAD), F32)], compiler_params=_params(),
    )(proj, lower_bounds, gain)


def _hgrn_bwd(proj, do, s_all, lower_bounds, gain, nheads):
    T = proj.shape[0]
    C, HP = HGRN_CHUNK, HGRN_HEADS_PER_STEP
    ng, nc, W = nheads // HP, T // C, HP * HEAD

    def body(proj_ref, do_ref, sin_ref, lb_ref, gain_ref, dproj_ref, dlb_ref, dgain_ref, ds_scr):
        first = pl.program_id(1) == 0

        @pl.when(first)
        def _():
            ds_scr[...] = jnp.zeros_like(ds_scr)
            dlb_ref[...] = jnp.zeros_like(dlb_ref)

        @pl.when(first & (pl.program_id(0) == 0))
        def _():
            dgain_ref[...] = jnp.zeros_like(dgain_ref)

        qr, fr, vi, z, lb0, lb1 = _hgrn_refs(proj_ref, lb_ref, HP)
        primals = (sin_ref[0], qr, fr, vi, z, lb0, lb1, gain_ref[...])
        _, vjp = jax.vjp(_hgrn_step, *primals)
        dS, dq, df, dv, dz, dlb0, dlb1, dgain = vjp((_split_heads(do_ref, 0, HP), ds_scr[...]))
        ds_scr[...] = dS
        for part, d in enumerate((dq, df, dv, dz)):
            _store_heads(dproj_ref, part * W, d)
        _store_heads(dlb_ref, 0, dlb0, slice(0, 1), accumulate=True)
        _store_heads(dlb_ref, 0, dlb1, slice(1, 2), accumulate=True)
        dgain_ref[...] += dgain

    rev = lambda i: nc - 1 - i
    return pl.pallas_call(
        body, name="hgrn_bwd", grid=(ng, nc),
        in_specs=[pl.BlockSpec((C, 4 * W), lambda g, i: (rev(i), g)), pl.BlockSpec((C, W), lambda g, i: (rev(i), g)),
                  pl.BlockSpec((1, HP, HEAD, HEAD), lambda g, i: (rev(i), g, 0, 0)),
                  pl.BlockSpec((2, W), lambda g, i: (0, g)), pl.BlockSpec((1, LANES), lambda g, i: (0, 0))],
        out_specs=[pl.BlockSpec((C, 4 * W), lambda g, i: (rev(i), g)), pl.BlockSpec((2, W), lambda g, i: (0, g)),
                   pl.BlockSpec((1, LANES), lambda g, i: (0, 0))],
        out_shape=[jax.ShapeDtypeStruct(proj.shape, BF16), jax.ShapeDtypeStruct(lower_bounds.shape, F32),
                   jax.ShapeDtypeStruct((1, LANES), F32)],
        scratch_shapes=[pltpu.VMEM((HP, HEAD, HEAD), F32)], compiler_params=_params(),
    )(proj, do, s_all, lower_bounds, gain)


def _post_specs(T):
    tr = min(POST_TILE, T)
    tile = lambda w: pl.BlockSpec((tr, w), lambda i: (i, 0))
    full = lambda r, w: pl.BlockSpec((r, w), lambda i: (0, 0))
    return tr, tile, full


def _post_fwd(s, x, p, g, b, wg, wpl, name):
    T, D = x.shape
    P = p.shape[1]
    tr, tile, full = _post_specs(T)

    def body(s_ref, x_ref, p_ref, g_ref, b_ref, wg_ref, wpl_ref, o_ref, o16_ref):
        x1 = _post_norm(s_ref[...], x_ref[...], g_ref[...], b_ref[...])
        xn = _post_gate(x1, _dot_raw(x1, wg_ref[...], "nn", False), _dot_raw(p_ref[...], wpl_ref[...], "nn", False))
        o_ref[...] = xn
        o16_ref[...] = xn.astype(BF16)

    return pl.pallas_call(
        body, name=name, grid=(T // tr,),
        in_specs=[tile(D), tile(D), tile(P), full(1, D), full(1, D), full(D, D), full(P, D)],
        out_specs=[tile(D), tile(D)],
        out_shape=[jax.ShapeDtypeStruct((T, D), F32), jax.ShapeDtypeStruct((T, D), BF16)], compiler_params=_params(),
    )(s, x, p, g, b, wg, wpl)


def _post_bwd(s, x, p, g, b, wg, wpl, dnext, name, with_loss):
    T, D = x.shape
    P = p.shape[1]
    tr, tile, full = _post_specs(T)

    def body(s_ref, x_ref, p_ref, g_ref, b_ref, wg_ref, wpl_ref, dn_ref,
             ds_ref, dx_ref, dg_ref, db_ref, dwg_ref, dwpl_ref, loss_ref):
        @pl.when(pl.program_id(0) == 0)
        def _():
            for r in (dg_ref, db_ref, dwg_ref, dwpl_ref, loss_ref):
                r[...] = jnp.zeros_like(r)

        x1, vjp_norm = jax.vjp(_post_norm, s_ref[...], x_ref[...], g_ref[...], b_ref[...])
        gate_pre = _dot_raw(x1, wg_ref[...], "nn", False)
        pp = _dot_raw(p_ref[...], wpl_ref[...], "nn", False)
        xn, vjp_gate = jax.vjp(_post_gate, x1, gate_pre, pp)
        if with_loss:
            err = xn - dn_ref[...]
            loss_ref[...] += 0.5 * jnp.sum(jnp.sum(err * err, axis=-1, keepdims=True), axis=0, keepdims=True) / D
            dn = err / D
        else:
            dn = dn_ref[...]
        dx1, dgp, dpp = vjp_gate(dn)
        dwg_ref[...] += _dot_raw(x1, dgp, "tn", False)
        dwpl_ref[...] += _dot_raw(p_ref[...], dpp, "tn", False)
        dx1 = dx1 + _dot_raw(dgp, wg_ref[...], "nt", False)
        ds, dx, dg, db = vjp_norm(dx1)
        ds_ref[...] = ds.astype(ds_ref.dtype)
        dx_ref[...] = dx
        dg_ref[...] += dg
        db_ref[...] += db

    return pl.pallas_call(
        body, name=name, grid=(T // tr,),
        in_specs=[tile(D), tile(D), tile(P), full(1, D), full(1, D), full(D, D), full(P, D), tile(D)],
        out_specs=[tile(D), tile(D), full(1, D), full(1, D), full(D, D), full(P, D), full(SUBLANES, LANES)],
        out_shape=[jax.ShapeDtypeStruct((T, D), BF16), jax.ShapeDtypeStruct((T, D), F32)]
        + [jax.ShapeDtypeStruct((1, D), F32)] * 2
        + [jax.ShapeDtypeStruct((D, D), F32), jax.ShapeDtypeStruct((P, D), F32),
           jax.ShapeDtypeStruct((SUBLANES, LANES), F32)],
        compiler_params=_params(),
    )(s, x, p, g, b, wg, wpl, dnext)


def _adam_math(w, g, m, v):
    m = ADAM_B1 * m + (1.0 - ADAM_B1) * g
    v = ADAM_B2 * v + (1.0 - ADAM_B2) * (g * g)
    m_hat = m / (1.0 - ADAM_B1 ** ADAM_STEP)
    v_hat = v / (1.0 - ADAM_B2 ** ADAM_STEP)
    return -ADAM_LR * (m_hat / (jnp.sqrt(v_hat) + ADAM_EPS) + ADAM_WD * w), m, v


def _rs_add(g8, got, cidx, name):
    _, R, C = g8.shape
    tr = _pick(R, (256, 128, 64, 32, 16, 8))

    def body(c_ref, a_ref, b_ref, o_ref, o16_ref):
        total = a_ref[...] + b_ref[...]
        o_ref[...] = total
        o16_ref[...] = total.astype(BF16)

    out_spec = pl.BlockSpec((1, tr, C), lambda q, i, c: (q, i, 0))
    return pl.pallas_call(
        body, name=name,
        grid_spec=pltpu.PrefetchScalarGridSpec(
            num_scalar_prefetch=1, grid=(4, R // tr),
            in_specs=[pl.BlockSpec((1, tr, C), lambda q, i, c: (2 * q + c[0], i, 0)),
                      pl.BlockSpec((1, tr, C), lambda q, i, c: (q, i, 0))],
            out_specs=[out_spec, out_spec]),
        out_shape=[jax.ShapeDtypeStruct((4,) + g8.shape[1:], F32), jax.ShapeDtypeStruct((4,) + g8.shape[1:], BF16)],
        compiler_params=_params(),
    )(cidx, g8, got)


def _adam_sharded(w, m, v, mine, got, qidx, name):
    L, R, C = w.shape
    tr = _pick(R, (256, 128, 64, 32, 16, 8))
    nr = R // tr

    def body(q_ref, w_ref, m_ref, v_ref, p_ref, r0, r1, r2, g_ref, d_ref, mo_ref, vo_ref):
        g = ((p_ref[0] + r0[0].astype(F32)) + r1[0].astype(F32)) + r2[0].astype(F32)
        d, mn, vn = _adam_math(w_ref[0], g, m_ref[0], v_ref[0])
        g_ref[0] = g
        d_ref[0] = d
        mo_ref[0] = mn
        vo_ref[0] = vn

    t3 = pl.BlockSpec((1, tr, C), lambda l, i, q: (l, i, 0))
    slot = lambda k: pl.BlockSpec((1, tr, C), lambda l, i, q: (k, l * nr + i, 0))
    return pl.pallas_call(
        body, name=name,
        grid_spec=pltpu.PrefetchScalarGridSpec(
            num_scalar_prefetch=1, grid=(L, nr),
            in_specs=[t3, t3, t3, pl.BlockSpec((1, tr, C), lambda l, i, q: (q[0], l * nr + i, 0)),
                      slot(0), slot(1), slot(2)],
            out_specs=[t3, t3, t3, t3]),
        out_shape=[jax.ShapeDtypeStruct((L, R, C), F32)] * 4, compiler_params=_params(),
    )(qidx, w, m, v, mine, got, got, got)


def _adam_replicated(w, m, v, g8):
    def body(w_ref, m_ref, v_ref, g_ref, go_ref, d_ref, mo_ref, vo_ref):
        g = g_ref[0]
        for k in range(1, 8):
            g = g + g_ref[k]
        d, mn, vn = _adam_math(w_ref[...], g, m_ref[...], v_ref[...])
        go_ref[...] = g
        d_ref[...] = d
        mo_ref[...] = mn
        vo_ref[...] = vn

    return pl.pallas_call(
        body, name="adam_replicated", out_shape=[jax.ShapeDtypeStruct(w.shape, F32)] * 4, compiler_params=_params(),
    )(w, m, v, g8)


def _place():
    return lax.axis_index("x"), lax.axis_index("y"), lax.axis_index("c")


def _all_gather(shard, name):
    def body(x_ref, out_ref, send_sems, recv_sems, local_sem):
        x, y, c = _place()
        me, sibling = (x, y, c), (x, y, 1 - c)
        chips = [(1 - x, y), (x, 1 - y), (1 - x, 1 - y)]

        def slab(px, py, pc):
            return out_ref.at[4 * px + 2 * py + pc]

        def copy(k, block, to, src=None):
            return pltpu.make_async_remote_copy(
                src_ref=slab(*block) if src is None else src, dst_ref=slab(*block),
                send_sem=send_sems.at[k], recv_sem=recv_sems.at[k], device_id=to, device_id_type=MESH)

        mine = pltpu.make_async_copy(x_ref, slab(*me), local_sem)
        mine.start()
        first = [copy(0, me, sibling, src=x_ref)]
        first += [copy(1 + j, me, (*chip, c), src=x_ref) for j, chip in enumerate(chips)]
        for cp in first:
            cp.start()
        passed = [copy(4 + j, (*chip, c), sibling) for j, chip in enumerate(chips)]
        for j, chip in enumerate(chips):
            copy(1 + j, (*chip, c), me).wait_recv()
            passed[j].start()
        copy(0, sibling, me).wait_recv()
        for j, chip in enumerate(chips):
            copy(4 + j, (*chip, 1 - c), me).wait_recv()
        for cp in first + passed:
            cp.wait_send()
        mine.wait()

    return pl.pallas_call(
        body, name=name, out_shape=jax.ShapeDtypeStruct((8,) + shard.shape, shard.dtype),
        in_specs=[pl.BlockSpec(memory_space=pl.ANY)], out_specs=pl.BlockSpec(memory_space=pl.ANY),
        scratch_shapes=[pltpu.SemaphoreType.DMA((7,)), pltpu.SemaphoreType.DMA((7,)), pltpu.SemaphoreType.DMA],
    )(shard)


def _rs_to_sibling(g8, name):
    def body(g_ref, out_ref, send_sems, recv_sems):
        x, y, c = _place()
        copies = [pltpu.make_async_remote_copy(
            src_ref=g_ref.at[2 * q + (1 - c)], dst_ref=out_ref.at[q], send_sem=send_sems.at[q],
            recv_sem=recv_sems.at[q], device_id=(x, y, 1 - c), device_id_type=MESH) for q in range(4)]
        for cp in copies:
            cp.start()
        for cp in copies:
            cp.wait()

    return pl.pallas_call(
        body, name=name, out_shape=jax.ShapeDtypeStruct((4,) + g8.shape[1:], g8.dtype),
        in_specs=[pl.BlockSpec(memory_space=pl.ANY)], out_specs=pl.BlockSpec(memory_space=pl.ANY),
        scratch_shapes=[pltpu.SemaphoreType.DMA((4,)), pltpu.SemaphoreType.DMA((4,))],
    )(g8)


_HBM = pl.BlockSpec(memory_space=pltpu.HBM)
_SEM = pl.BlockSpec(memory_space=pltpu.SEMAPHORE)
_DATAFLOW = pltpu.SideEffectType.DATAFLOW_SIDE_EFFECTING
TOKEN_SHAPE = (SUBLANES, LANES)


def _chip_plan(x, y, c):
    return [(2 * px + py, j, (px, py, c)) for j, (px, py) in enumerate([(1 - x, y), (x, 1 - y), (1 - x, 1 - y)])]


def _exchange_copies(plan, src_ref, land_ref, send_sems, recv_sems):
    return [pltpu.make_async_remote_copy(
        src_ref=src_ref.at[blk], dst_ref=land_ref.at[slot], send_sem=send_sems.at[k], recv_sem=recv_sems.at[k],
        device_id=peer, device_id_type=MESH) for k, (blk, slot, peer) in enumerate(plan(*_place()))]


def _exchange_start(src, n_slots, plan, name):
    land_shape = (n_slots,) + src.shape[1:]
    n = len(plan(0, 0, 0))

    def body(src_ref, land_ref, send_sems, recv_sems, src_thru, land_thru, token):
        for cp in _exchange_copies(plan, src_ref, land_ref, send_sems, recv_sems):
            cp.start()
        token[...] = jnp.zeros_like(token)

    return pl.pallas_call(
        body, name=name,
        out_shape=(pltpu.SemaphoreType.DMA((n,)), pltpu.SemaphoreType.DMA((n,)), pltpu.HBM(src.shape, src.dtype),
                   pltpu.HBM(land_shape, src.dtype), jax.ShapeDtypeStruct(TOKEN_SHAPE, F32)),
        in_specs=(_HBM, _HBM), out_specs=(_SEM, _SEM, _HBM, _HBM, pl.BlockSpec(memory_space=pltpu.VMEM)),
        input_output_aliases={0: 2, 1: 3}, compiler_params=pltpu.CompilerParams(has_side_effects=_DATAFLOW),
    )(pltpu.with_memory_space_constraint(src, pltpu.HBM),
      pltpu.with_memory_space_constraint(lax.empty(land_shape, src.dtype), pltpu.HBM))


def _exchange_wait(handle, plan, after, name):
    send_sems, recv_sems, src_thru, land_thru, _ = handle

    def body(src_ref, land_ref, send_sems, recv_sems, after_ref, src_dead, got_ref):
        for cp in _exchange_copies(plan, src_ref, land_ref, send_sems, recv_sems):
            cp.wait_send()
            cp.wait_recv()

    return pl.pallas_call(
        body, name=name,
        out_shape=(pltpu.HBM(src_thru.shape, src_thru.dtype), pltpu.HBM(land_thru.shape, land_thru.dtype)),
        in_specs=(_HBM, _HBM, _SEM, _SEM, pl.BlockSpec(memory_space=pl.ANY)), out_specs=(_HBM, _HBM),
        input_output_aliases={0: 0, 1: 1}, compiler_params=pltpu.CompilerParams(has_side_effects=_DATAFLOW),
    )(src_thru, land_thru, send_sems, recv_sems, after)[1]


def _gather_plan(x, y, c):
    me = 4 * x + 2 * y + c
    flip = lambda v, bit: 1 - v if bit else v
    return [(0, me, (flip(x, r >> 2 & 1), flip(y, r >> 1 & 1), flip(c, r & 1))) for r in range(1, 8)]


class _LateGather:
    def __init__(self, shard, name):
        self.shard, self.name = shard, name
        self.handle = _exchange_start(shard[None], 8, _gather_plan, name + "_start")

    def get(self, after):
        land = _exchange_wait(self.handle, _gather_plan, after, self.name + "_wait")
        x, y, c = _place()
        return lax.dynamic_update_slice(land, self.shard[None], (4 * x + 2 * y + c, 0, 0))


class _GradExchange:
    def __init__(self, cidx, qidx, layouts):
        self.cidx, self.qidx, self.layouts, self.pending = cidx, qidx, layouts, {}

    def start(self, tag, grad):
        g8 = self.layouts[tag](grad)
        got = _rs_to_sibling(g8, "rs_sibling_" + tag)
        chip_sums, chip_sums16 = _rs_add(g8, got, self.cidx, "rs_add_" + tag)
        handle = _exchange_start(chip_sums16, 3, _chip_plan, "rs_chips_start_" + tag)
        self.pending[tag] = (chip_sums, handle)
        return handle[4]

    def finish(self, tag, w, m, v, after):
        chip_sums, handle = self.pending.pop(tag)
        got2 = _exchange_wait(handle, _chip_plan, after, "rs_chips_wait_" + tag)
        return _adam_sharded(w, m, v, chip_sums, got2, self.qidx, "adam_" + tag)


def _local_grads(x, p0, p1, target, w_zs, w_a, w_qkv, late, conv_a, conv_b,
                 a_log, dt_bias, gdn_gain, lower_bounds, hgrn_gain, ln_g, ln_b, on_grad=None):
    H = a_log.shape[1]
    pad_small = ((0, 0), (H, LANES - 2 * H))
    alog_row = jnp.pad(a_log, pad_small)
    dtb_row = jnp.pad(dt_bias, pad_small)

    x16 = x.astype(BF16)
    proj_zs = _matmul(x16, w_zs, "nn", "proj_even_zs", after=late.started)
    proj_a = _matmul(x16, w_a, "nn", "proj_even_a", after=late.started)
    proj_qkv = _matmul(x16, w_qkv, "nn", "proj_even_qkv", after=late.started)
    y_a = _mixer_a_fwd(proj_a, conv_a)
    qkv = _conv_b_fwd(proj_qkv, conv_b)
    o2, s_gdn = _gdn_fwd(qkv, proj_zs, alog_row, dtb_row, gdn_gain, H)
    woute_a, woute_b = late.out_even(o2)
    wg, wpl = late.gate(o2)
    s_e = _matmul(o2, woute_b, "nn", "out_even_b", add=_matmul(y_a, woute_a, "nn", "out_even_a"))
    x2, x2_16 = _post_fwd(s_e, x, p0, ln_g[0:1], ln_b[0:1], wg[0], wpl[0], "post_even_fwd")
    wino, wouto = late.odd(s_e)
    nheads_o = wouto.shape[0] // HEAD
    proj_o = _matmul(x2_16, wino, "nn", "proj_odd")
    o4, s_hgrn = _hgrn_fwd(proj_o, lower_bounds, hgrn_gain, nheads_o)
    s_o = _matmul(o4, wouto, "nn", "out_odd")
    ds_o, dx2, dlng1, dlnb1, dwg1, dwpl1, loss = _post_bwd(
        s_o, x2, p1, ln_g[1:2], ln_b[1:2], wg[1], wpl[1], target, "post_odd_loss_bwd", True)
    do4 = _matmul(ds_o, wouto, "nt", "d_out_odd_act")
    grads = {}

    def emit(tag, grad):
        grads[tag] = grad
        return on_grad(tag, grad) if on_grad is not None else jnp.zeros(TOKEN_SHAPE, F32)

    tok = emit("w_out_odd", _matmul(o4, ds_o, "tn", "d_out_odd_w"))
    dproj_o, dlb, dhgain = _hgrn_bwd(proj_o, do4, s_hgrn, lower_bounds, hgrn_gain + tok[0:1], nheads_o)
    dx2 = _matmul(dproj_o, wino, "nt", "d_proj_odd_act", add=dx2)
    tok = emit("w_in_odd", _matmul(x2_16, dproj_o, "tn", "d_proj_odd_w"))
    ds_e, dx, dlng0, dlnb0, dwg0, dwpl0, _ = _post_bwd(
        s_e, x, p0, ln_g[0:1], ln_b[0:1] + tok[0:1, 0:1], wg[0], wpl[0], dx2, "post_even_bwd", False)
    tok = emit("w_pl_gate", jnp.stack([dwg0, dwg1])) + emit("w_pl", jnp.stack([dwpl0, dwpl1]))
    dy_a = _matmul(ds_e, woute_a, "nt", "d_out_even_a_act")
    do2 = _matmul(ds_e, woute_b, "nt", "d_out_even_b_act")
    dwoute_a = _matmul(y_a, ds_e, "tn", "d_out_even_a_w")
    dwoute_b = _matmul(o2, ds_e, "tn", "d_out_even_b_w")
    tok = tok + emit("w_out_even", jnp.concatenate([dwoute_a, dwoute_b], axis=0))
    dqkv, dproj_zs, dalog, ddtb, dggain = _gdn_bwd(qkv, proj_zs, do2, s_gdn, alog_row, dtb_row, gdn_gain + tok[0:1], H)
    dproj_qkv, dconv_b = _conv_b_bwd(proj_qkv, dqkv, conv_b)
    dproj_a, dconv_a = _mixer_a_bwd(proj_a, dy_a, conv_a)
    emit("conv", (dconv_a[:conv_a.shape[0]], dconv_b[:conv_b.shape[0]]))
    tok = emit("w_in_even", (_matmul(x16, dproj_zs, "tn", "d_proj_even_zs_w"), _matmul(x16, dproj_a, "tn", "d_proj_even_a_w"),
                             _matmul(x16, dproj_qkv, "tn", "d_proj_even_qkv_w")))
    dx = _matmul(dproj_zs, w_zs, "nt", "d_proj_even_zs_act", add=dx, after=tok)
    dx = _matmul(dproj_a, w_a, "nt", "d_proj_even_a_act", add=dx)
    dx = _matmul(dproj_qkv, w_qkv, "nt", "d_proj_even_qkv_act", add=dx)
    grads.update(
        loss=loss[0, 0], grad_x=dx, a_log=dalog[:, H:2 * H], dt_bias=ddtb[:, H:2 * H], gdn_gain=dggain,
        lower_bounds=dlb, hgrn_gain=dhgain, ln_g=jnp.concatenate([dlng0, dlng1], axis=0),
        ln_b=jnp.concatenate([dlnb0, dlnb1], axis=0))
    return grads


def _pad_rows(a, rows):
    return jnp.pad(a, ((0, rows - a.shape[0]), (0, 0)))


def _pack_small(a_log, dt_bias, gdn_gain, lower_bounds, hgrn_gain, ln_g, ln_b):
    lane_pad = lambda a: _pad_rows(jnp.pad(a, ((0, 0), (0, LANES - a.shape[1]))), SUBLANES)
    parts = [lane_pad(a_log), lane_pad(dt_bias), lane_pad(gdn_gain), lower_bounds.reshape(-1, LANES),
             lane_pad(hgrn_gain), ln_g.reshape(-1, LANES), ln_b.reshape(-1, LANES)]
    packed = jnp.concatenate(parts, axis=0)
    assert packed.shape[0] == SMALL_ROWS, packed.shape
    return packed


def _unpack_small(packed, shapes):
    out, r = [], 0
    for shp in shapes:
        n = shp[0] * shp[1]
        if n < LANES * SUBLANES and shp[1] <= LANES:
            out.append(packed[r:r + shp[0], :shp[1]])
            r += SUBLANES
        else:
            rows = n // LANES
            out.append(packed[r:r + rows].reshape(shp))
            r += rows
    return out


def _split_in_even(w_full, AW, HW, H):
    D = w_full.shape[0]
    n_a = 4 * AW
    n_main = n_a + 3 * HW
    w_zs = jnp.concatenate([w_full[:, n_main:n_main + HW], w_full[:, n_main + HW:],
                            jnp.zeros((D, LANES - 2 * H), w_full.dtype)], axis=1)
    w_a = w_full[:, :n_a].reshape(D, 4, AW // MIXER_LANES, MIXER_LANES).transpose(0, 2, 1, 3).reshape(D, n_a)
    return w_zs, w_a, w_full[:, n_a:n_main]


def _join_in_even(d_zs, d_a, d_qkv, AW, HW, H):
    D = d_a.shape[0]
    a_nat = d_a.reshape(D, AW // MIXER_LANES, 4, MIXER_LANES).transpose(0, 2, 1, 3).reshape(D, 4 * AW)
    return jnp.concatenate([a_nat, d_qkv, d_zs[:, :HW], d_zs[:, HW:HW + 2 * H]], axis=1)


def kernel(x, p, w_in_even, conv_a_w, conv_b_w, a_log, dt_bias, gdn_norm_g, w_out_even, w_in_odd, lower_bounds, hgrn_norm_g, w_out_odd, ln_g, ln_b, w_pl, w_pl_gate, loss_target, m_w_in_even, m_conv_a_w, m_conv_b_w, m_a_log, m_dt_bias, m_gdn_norm_g, m_w_out_even, m_w_in_odd, m_lower_bounds, m_hgrn_norm_g, m_w_out_odd, m_ln_g, m_ln_b, m_w_pl, m_w_pl_gate, v_w_in_even, v_conv_a_w, v_conv_b_w, v_a_log, v_dt_bias, v_gdn_norm_g, v_w_out_even, v_w_in_odd, v_lower_bounds, v_hgrn_norm_g, v_w_out_odd, v_ln_g, v_ln_b, v_w_pl, v_w_pl_gate):
    xi, yi, ci = _place()
    cidx = jnp.reshape(ci, (1,)).astype(jnp.int32)
    qidx = jnp.reshape(2 * xi + yi, (1,)).astype(jnp.int32)
    D = x.shape[2]
    H = a_log.shape[1]
    HW = H * HEAD
    AW = conv_a_w.shape[2] * 8
    OW = w_out_odd.shape[1] * 8
    PD = w_pl.shape[1]
    ka, kb = conv_a_w.shape[1], conv_b_w.shape[1]
    ca, cb = conv_a_w.shape[2], conv_b_w.shape[2]
    gw = HGRN_HEADS_PER_STEP * HEAD
    ngrp = OW // gw

    g_ine = _all_gather(w_in_even[0].astype(BF16), "ag_w_in_even")
    w_zs, w_a, w_qkv = _split_in_even(jnp.transpose(g_ine, (1, 0, 2)).reshape(D, -1), AW, HW, H)
    behind = lambda shard, dep: lax.optimization_barrier((shard, dep))[0]
    late_oute = _LateGather(behind(w_out_even[0].astype(BF16), g_ine), "ag_w_out_even")
    late_gate = _LateGather(behind(w_pl_gate.astype(BF16).reshape(-1, D), late_oute.handle[4]), "ag_w_pl_gate")
    late_pl = _LateGather(behind(w_pl.astype(BF16).reshape(DEPTH * PD, -1), late_gate.handle[4]), "ag_w_pl")
    late_ino = _LateGather(behind(w_in_odd[0].astype(BF16), late_pl.handle[4]), "ag_w_in_odd")
    late_outo = _LateGather(behind(w_out_odd[0].astype(BF16), late_ino.handle[4]), "ag_w_out_odd")

    class _Late:
        started = sum(g.handle[4] for g in (late_oute, late_gate, late_pl, late_ino, late_outo))

        @staticmethod
        def out_even(after):
            woute = late_oute.get(after).reshape(-1, D)
            return woute[:AW], woute[AW:]

        @staticmethod
        def gate(after):
            g_gate, g_pl = late_gate.get(after), late_pl.get(after)
            return (g_gate.reshape(8, DEPTH, D // 8, D).transpose(1, 0, 2, 3).reshape(DEPTH, D, D),
                    g_pl.reshape(8, DEPTH, PD, D // 8).transpose(1, 2, 0, 3).reshape(DEPTH, PD, D))

        @staticmethod
        def odd(after):
            g_ino = late_ino.get(after)
            wino = jnp.transpose(g_ino, (1, 0, 2)).reshape(D, 4, ngrp, gw).transpose(0, 2, 1, 3).reshape(D, 4 * OW)
            return wino, late_outo.get(after).reshape(-1, D)

    taps = jnp.concatenate([_pad_rows(conv_a_w[0], SUBLANES), _pad_rows(conv_b_w[0], SUBLANES)], axis=1)
    g_taps = _all_gather(taps, "ag_conv")
    conv_a = jnp.transpose(g_taps[:, :ka, :ca], (1, 0, 2)).reshape(ka, 8 * ca)
    conv_b = jnp.transpose(g_taps[:, :kb, ca:], (1, 0, 2)).reshape(kb, 8 * cb)

    sh = w_in_even.shape[2]
    tap_blocks = lambda g, width: _pad_rows(g, SUBLANES).reshape(SUBLANES, 8, width).transpose(1, 0, 2)
    owner_layout = {
        "w_in_even": lambda g: _join_in_even(*g, AW, HW, H).reshape(D, 8, sh).transpose(1, 0, 2),
        "w_in_odd": lambda g: g.reshape(D, ngrp, 4, gw).transpose(0, 2, 1, 3).reshape(D, 8, 4 * OW // 8).transpose(1, 0, 2),
        "w_out_even": lambda g: g.reshape(8, -1, D),
        "w_out_odd": lambda g: g.reshape(8, -1, D),
        "w_pl_gate": lambda g: g.reshape(DEPTH, 8, D // 8, D).transpose(1, 0, 2, 3).reshape(8, DEPTH * D // 8, D),
        "w_pl": lambda g: g.reshape(DEPTH, PD, 8, D // 8).transpose(2, 0, 1, 3).reshape(8, DEPTH * PD, D // 8),
        "conv": lambda g: jnp.concatenate([tap_blocks(g[0], ca), tap_blocks(g[1], cb)], axis=2),
    }
    exchange = _GradExchange(cidx, qidx, owner_layout)
    gr = _local_grads(x[0], p[0, 0], p[1, 0], loss_target[0], w_zs, w_a, w_qkv, _Late, conv_a, conv_b,
                      a_log, dt_bias, gdn_norm_g, lower_bounds, hgrn_norm_g, ln_g, ln_b, on_grad=exchange.start)
    loss = lax.psum(gr["loss"], AXES)

    last = gr["grad_x"]
    pack_taps = lambda a, b: jnp.concatenate([_pad_rows(a[0], SUBLANES), _pad_rows(b[0], SUBLANES)], axis=1)[None]
    o_outo = exchange.finish("w_out_odd", w_out_odd, m_w_out_odd, v_w_out_odd, last)
    o_ino = exchange.finish("w_in_odd", w_in_odd, m_w_in_odd, v_w_in_odd, last)
    o_gate = exchange.finish("w_pl_gate", w_pl_gate, m_w_pl_gate, v_w_pl_gate, last)
    o_pl = exchange.finish("w_pl", w_pl, m_w_pl, v_w_pl, last)
    o_oute = exchange.finish("w_out_even", w_out_even, m_w_out_even, v_w_out_even, last)
    o_taps = exchange.finish("conv", taps[None], pack_taps(m_conv_a_w, m_conv_b_w), pack_taps(v_conv_a_w, v_conv_b_w), last)
    others_done = sum(o[1][0, 0:1, 0:1] for o in (o_outo, o_ino, o_gate, o_pl, o_oute, o_taps))
    o_ine = exchange.finish("w_in_even", w_in_even, m_w_in_even, v_w_in_even, others_done)

    small_g = _pack_small(gr["a_log"], gr["dt_bias"], gr["gdn_gain"], gr["lower_bounds"], gr["hgrn_gain"],
                          gr["ln_g"], gr["ln_b"])
    o_small = _adam_replicated(
        _pack_small(a_log, dt_bias, gdn_norm_g, lower_bounds, hgrn_norm_g, ln_g, ln_b),
        _pack_small(m_a_log, m_dt_bias, m_gdn_norm_g, m_lower_bounds, m_hgrn_norm_g, m_ln_g, m_ln_b),
        _pack_small(v_a_log, v_dt_bias, v_gdn_norm_g, v_lower_bounds, v_hgrn_norm_g, v_ln_g, v_ln_b),
        _all_gather(small_g, "ag_small_grads"))
    small_shapes = [a_log.shape, dt_bias.shape, gdn_norm_g.shape, lower_bounds.shape, hgrn_norm_g.shape,
                    ln_g.shape, ln_b.shape]

    def leaves(kind):
        s_alog, s_dt, s_gg, s_lb, s_hg, s_lng, s_lnb = _unpack_small(o_small[kind], small_shapes)
        t = o_taps[kind]
        return [o_ine[kind], t[:, :ka, :ca], t[:, :kb, ca:], s_alog, s_dt, s_gg, o_oute[kind],
                o_ino[kind], s_lb, s_hg, o_outo[kind], s_lng, s_lnb, o_pl[kind], o_gate[kind]]

    return (loss, gr["grad_x"][None], *leaves(0), *leaves(1), *leaves(2), *leaves(3))
```

```python
import functools

import jax
import jax.numpy as jnp
from jax import lax
from jax.experimental import pallas as pl
from jax.experimental.pallas import tpu as pltpu

F32 = jnp.float32
BF16 = jnp.bfloat16
MESH = pl.DeviceIdType.MESH
AXES = ("x", "y", "c")

LANES = 128
SUBLANES = 8
HEAD = 128
GDN_CHUNK = 64
HGRN_CHUNK = 32
HGRN_SUB = 16
HGRN_HEADS_PER_STEP = 16
NORM_EPS = 1e-5
DEPTH = 2
ALPHA = (2.0 * DEPTH) ** 0.25
EXP_CLAMP = 80.0
ADAM_LR, ADAM_B1, ADAM_B2, ADAM_EPS, ADAM_WD, ADAM_STEP = 0.001, 0.9, 0.999, 1e-08, 0.01, 10
VMEM_LIMIT = 56 * 1024 * 1024
MATMUL_VMEM = 36 * 1024 * 1024
ROW_TILE = 512
MIXER_LANES = 256
CONV_LANES = 512
POST_TILE = 256
SMALL_ROWS = 96

_NOBATCH, _BATCH0 = ((), ()), ((0,), (0,))
_DIMS = {"nn": (((1,), (0,)), _NOBATCH), "nt": (((1,), (1,)), _NOBATCH), "tn": (((0,), (0,)), _NOBATCH),
         "bnn": (((2,), (1,)), _BATCH0), "bnt": (((2,), (2,)), _BATCH0), "btn": (((1,), (1,)), _BATCH0)}


def _params(**kw):
    return pltpu.CompilerParams(vmem_limit_bytes=VMEM_LIMIT, **kw)


def _dot_raw(a, b, kind, hi):
    if hi:
        return lax.dot_general(a, b, _DIMS[kind], precision=lax.Precision.HIGHEST, preferred_element_type=F32)
    return lax.dot_general(a.astype(BF16), b.astype(BF16), _DIMS[kind], preferred_element_type=F32)


@functools.partial(jax.custom_vjp, nondiff_argnums=(2, 3))
def mdot(a, b, kind, hi):
    return _dot_raw(a, b, kind, hi)


def _mdot_fwd(a, b, kind, hi):
    return _dot_raw(a, b, kind, hi), (a, b)


def _mdot_bwd(kind, hi, res, g):
    a, b = res
    pre, base = kind[:-2], kind[-2:]
    if base == "nn":
        return _dot_raw(g, b, pre + "nt", hi), _dot_raw(a, g, pre + "tn", hi)
    if base == "nt":
        return _dot_raw(g, b, pre + "nn", hi), _dot_raw(g, a, pre + "tn", hi)
    return _dot_raw(b, g, pre + "nt", hi), _dot_raw(a, g, pre + "nn", hi)


mdot.defvjp(_mdot_fwd, _mdot_bwd)


def _rows(x, lo, hi):
    return _take_rows(x, lo, hi, x.shape[-2])


@functools.partial(jax.custom_vjp, nondiff_argnums=(1, 2, 3))
def _take_rows(x, lo, hi, n):
    return x[..., lo:hi, :]


def _take_rows_fwd(x, lo, hi, n):
    return x[..., lo:hi, :], None


def _take_rows_bwd(lo, hi, n, _, g):
    parts = []
    if lo > 0:
        parts.append(jnp.zeros(g.shape[:-2] + (lo, g.shape[-1]), g.dtype))
    parts.append(g)
    if n - hi > 0:
        parts.append(jnp.zeros(g.shape[:-2] + (n - hi, g.shape[-1]), g.dtype))
    return (jnp.concatenate(parts, axis=-2) if len(parts) > 1 else g,)


_take_rows.defvjp(_take_rows_fwd, _take_rows_bwd)


def _heads_of(wide, nheads):
    return jnp.stack([wide[:, h * HEAD:(h + 1) * HEAD] for h in range(nheads)], axis=0)


def _wide_of(x):
    return jnp.concatenate([x[h] for h in range(x.shape[0])], axis=1)


@functools.partial(jax.custom_vjp, nondiff_argnums=(1,))
def to_heads(wide, nheads):
    return _heads_of(wide, nheads)


to_heads.defvjp(lambda wide, nheads: (_heads_of(wide, nheads), None), lambda nheads, _, g: (_wide_of(g),))


@jax.custom_vjp
def to_wide(x):
    return _wide_of(x)


to_wide.defvjp(lambda x: (_wide_of(x), None), lambda _, g: (_heads_of(g, g.shape[1] // HEAD),))


def _sigmoid(x):
    return jax.nn.sigmoid(x)


def _silu(x):
    return x * _sigmoid(x)


def _dsilu(x):
    s = _sigmoid(x)
    return s * (1.0 + x * (1.0 - s))


def _log1p(u):
    return jnp.where(u < 1e-4, u * (1.0 - 0.5 * u), jnp.log(1.0 + u))


def _softplus(x):
    return jnp.maximum(x, 0.0) + _log1p(jnp.exp(-jnp.abs(x)))


def _rms_gate(o, gain, z):
    return o * lax.rsqrt(jnp.mean(o * o, axis=-1, keepdims=True) + NORM_EPS) * gain * _silu(z)


def _l2n(x):
    return x * lax.rsqrt(jnp.sum(x * x, axis=-1, keepdims=True) + 1e-6)


def _split_dot_raw(m, x, kind):
    mb = m.astype(BF16)
    hi = x.astype(BF16)
    lo = (x - hi.astype(F32)).astype(BF16)
    dims = _DIMS[kind]
    return (lax.dot_general(mb, hi, dims, preferred_element_type=F32)
            + lax.dot_general(mb, lo, dims, preferred_element_type=F32))


@jax.custom_vjp
def mask_dot(m, x):
    return _split_dot_raw(m, x, "nn")


def _mask_dot_fwd(m, x):
    return _split_dot_raw(m, x, "nn"), m


def _mask_dot_bwd(m, g):
    return jnp.zeros_like(m), _split_dot_raw(m, g, "tn")


mask_dot.defvjp(_mask_dot_fwd, _mask_dot_bwd)


def _unit_lower_inverse_minus_eye(low, n):
    rest = -low
    power = low
    span = 2
    while span < n:
        power = mdot(power, power, "bnn", False)
        rest = rest + power + mdot(rest, power, "bnn", False)
        span *= 2
    return rest


def _gdn_step(S, q, k, v, z, small, alog, dtb, gain):
    H = S.shape[0]
    C = GDN_CHUNK
    row = lax.broadcasted_iota(jnp.int32, (C, C), 0)
    col = lax.broadcasted_iota(jnp.int32, (C, C), 1)
    tril, strict, eye = (row >= col)[None], (row > col)[None], (row == col)[None]
    head = lax.broadcasted_iota(jnp.int32, (H, 1, LANES), 0)
    lane = lax.broadcasted_iota(jnp.int32, (H, 1, LANES), 2)
    rowc = lax.broadcasted_iota(jnp.int32, (1, C, 1), 1)
    beta_all = _sigmoid(small)
    g_all = -jnp.exp(alog) * _softplus(small + dtb)
    gc_all = mask_dot((row >= col).astype(F32), g_all)
    beta = jnp.sum(jnp.where(lane == head, beta_all[None], 0.0), axis=-1, keepdims=True)
    gc = jnp.sum(jnp.where(lane == head + H, gc_all[None], 0.0), axis=-1, keepdims=True)
    gc_row = jnp.sum(jnp.where(eye, gc, 0.0), axis=1, keepdims=True)
    decay = jnp.where(tril, jnp.exp(jnp.where(tril, gc - gc_row, 0.0)), 0.0)
    g_last = jnp.sum(jnp.where(rowc == C - 1, gc, 0.0), axis=1, keepdims=True)
    qn = _l2n(q) * (HEAD ** -0.5)
    kn = _l2n(k)
    kb = kn * beta
    low = jnp.where(strict, mdot(kb, kn, "bnt", False) * decay, 0.0)
    inv_rest = _unit_lower_inverse_minus_eye(low, C)
    eg = jnp.exp(gc)
    vb, kbe = v * beta, kb * eg
    u = vb + mdot(inv_rest, vb, "bnn", False)
    w = kbe + mdot(inv_rest, kbe, "bnn", False)
    attn = mdot(qn, kn, "bnt", False) * decay
    v_new = u - mdot(w, S, "bnn", False)
    o = mdot(qn * eg, S, "bnn", False) + mdot(attn, v_new, "bnn", False)
    k_dec = kn * jnp.exp(g_last - gc)
    return _rms_gate(o, gain, z), S * jnp.exp(g_last) + mdot(k_dec, v_new, "btn", False)


def _hgrn_step(St, qr, fr, vi, z, lb0, lb1, gain):
    H = St.shape[0]
    C, SB = HGRN_CHUNK, HGRN_SUB
    row = lax.broadcasted_iota(jnp.int32, (C, C), 0)
    col = lax.broadcasted_iota(jnp.int32, (C, C), 1)
    blk_start = row - (row & (SB - 1))
    in_blk_f = ((row >= col) & (col >= blk_start)).astype(F32)
    before_f = (col < blk_start).astype(F32)
    sums_f = jnp.concatenate([in_blk_f, before_f], axis=0)
    m = jnp.maximum(lb0, lb1)
    e0, e1 = jnp.exp(lb0 - m), jnp.exp(lb1 - m)
    lb = e1 / (e0 + e1)
    f = lb + (1.0 - lb) * _sigmoid(fr)
    q = _silu(qr)
    k = 1.0 - f
    logf = jnp.log(f)
    sums = mask_dot(sums_f, to_wide(logf))
    inner, start = to_heads(_rows(sums, 0, C), H), to_heads(_rows(sums, C, 2 * C), H)
    b = start + inner
    b_last = jnp.sum(logf, axis=1, keepdims=True)
    o = mdot(q * jnp.exp(b), St, "bnt", False)
    qt = q * jnp.exp(inner)
    parts = []
    for blk in range(C // SB):
        lo, n = blk * SB, (blk + 1) * SB
        ref = jnp.concatenate([_rows(start, lo, n)] * (blk + 1), axis=1)
        kt = _rows(k, 0, n) * jnp.exp(jnp.minimum(ref - _rows(b, 0, n), EXP_CLAMP))
        att = mdot(_rows(qt, lo, n), kt, "bnt", False)
        t_idx = lax.broadcasted_iota(jnp.int32, (1, SB, n), 1) + lo
        s_idx = lax.broadcasted_iota(jnp.int32, (1, SB, n), 2)
        att = jnp.where(s_idx <= t_idx, att, 0.0)
        parts.append(mdot(att, _rows(vi, 0, n), "bnn", False))
    o = o + jnp.concatenate(parts, axis=1)
    k_dec = k * jnp.exp(b_last - b)
    return _rms_gate(o, gain, z), St * jnp.exp(b_last) + mdot(vi, k_dec, "btn", False)


def _post_norm(s, x, g, b):
    r = ALPHA * x + s
    d = r - jnp.mean(r, axis=-1, keepdims=True)
    var = jnp.mean(d * d, axis=-1, keepdims=True)
    return d * lax.rsqrt(var + NORM_EPS) * g + b


def _post_gate(x1, gate_pre, pp):
    return x1 + pp * _sigmoid(gate_pre)


def _pick(dim, cands):
    for c in cands:
        if dim % c == 0:
            return c
    return dim


def _matmul_tiles(M, K, tn, a_bytes, b_bytes, has_add):
    for tk in (4096, 2048, 1536, 1152, 1024, 640, 512, 384, 256, 128):
        if K % tk:
            continue
        for tm in (2048, 1152, 1024, 512, 384, 256, 128):
            if M % tm:
                continue
            blocks = tm * tk * a_bytes + tk * tn * b_bytes + tm * tn * 4 * (2 if has_add else 1)
            if 2 * blocks + (tm * tn * 4 if tk < K else 0) <= MATMUL_VMEM and tm >= min(M, 1024):
                return tm, tk
    return _pick(M, (512, 256, 128)), _pick(K, (512, 256, 128))


def _matmul(a, b, kind, name, add=None, after=None):
    if kind == "nn":
        (M, K), N = a.shape, b.shape[1]
    elif kind == "nt":
        (M, K), N = a.shape, b.shape[0]
    else:
        (K, M), N = a.shape, b.shape[1]
    has_add = add is not None
    tn = _pick(N, (512, 640, 384, 256, 128))
    tm, tk = _matmul_tiles(M, K, tn, a.dtype.itemsize, b.dtype.itemsize, has_add)
    nk = K // tk
    a_spec = pl.BlockSpec((tk, tm), lambda i, j, k: (k, i)) if kind == "tn" else pl.BlockSpec((tm, tk), lambda i, j, k: (i, k))
    b_spec = pl.BlockSpec((tn, tk), lambda i, j, k: (j, k)) if kind == "nt" else pl.BlockSpec((tk, tn), lambda i, j, k: (k, j))
    o_spec = pl.BlockSpec((tm, tn), lambda i, j, k: (i, j))

    extra = ([add] if has_add else []) + ([after] if after is not None else [])
    extra_specs = ([o_spec] if has_add else []) + ([pl.BlockSpec(TOKEN_SHAPE, lambda i, j, k: (0, 0))] if after is not None else [])

    def body(a_ref, b_ref, *rest):
        add_ref = rest[0] if has_add else None
        o_ref = rest[len(extra)]
        part = _dot_raw(a_ref[...], b_ref[...], kind, False)
        if nk == 1:
            o_ref[...] = part + add_ref[...] if has_add else part
            return
        acc = rest[-1]
        kk = pl.program_id(2)

        @pl.when(kk == 0)
        def _():
            acc[...] = part

        @pl.when(kk > 0)
        def _():
            acc[...] += part

        @pl.when(kk == nk - 1)
        def _():
            o_ref[...] = acc[...] + add_ref[...] if has_add else acc[...]

    return pl.pallas_call(
        body, name=name, grid=(M // tm, N // tn, nk),
        in_specs=[a_spec, b_spec] + extra_specs,
        out_specs=o_spec, out_shape=jax.ShapeDtypeStruct((M, N), F32),
        scratch_shapes=[pltpu.VMEM((tm, tn), F32)] if nk > 1 else [],
        compiler_params=_params(dimension_semantics=("parallel", "parallel", "arbitrary")),
    )(a, b, *extra)


def _halo_specs(ts, nt, width, prev=True, main=True, nxt=True):
    per = ts // SUBLANES
    last8 = nt * per - 1
    specs = []
    if prev:
        specs.append(pl.BlockSpec((SUBLANES, width), lambda cb, i: (jnp.maximum(i * per - 1, 0), cb)))
    if main:
        specs.append(pl.BlockSpec((ts, width), lambda cb, i: (i, cb)))
    if nxt:
        specs.append(pl.BlockSpec((SUBLANES, width), lambda cb, i: (jnp.minimum((i + 1) * per, last8), cb)))
    return specs


def _taps(ext, ktaps, lo, size):
    return [ext[lo:lo + size] if j == 0 else pltpu.roll(ext, j, 0)[lo:lo + size] for j in range(ktaps)]


def _ahead(ext, j, size):
    n = ext.shape[0]
    return ext[:size] if j == 0 else pltpu.roll(ext, n - j, 0)[:size]


def _lane_block(ref, k):
    return ref[:, k * MIXER_LANES:(k + 1) * MIXER_LANES]


def _mixer_a_fwd(proj_a, conv_w):
    T = proj_a.shape[0]
    nblk = proj_a.shape[1] // (4 * MIXER_LANES)
    ts = min(ROW_TILE, T)
    nt = T // ts

    def body(pp, pm, w_ref, y_ref):
        i = pl.program_id(1)
        u_prev = jnp.where(i > 0, _lane_block(pp, 0) * _lane_block(pp, 1), 0.0)
        ext = jnp.concatenate([u_prev, _lane_block(pm, 0) * _lane_block(pm, 1)], axis=0)
        t0, t1, t2 = _taps(ext, 3, SUBLANES, ts)
        cv = w_ref[2:3, :] * t0 + w_ref[1:2, :] * t1 + w_ref[0:1, :] * t2
        y_ref[...] = (_lane_block(pm, 2) * cv * _silu(_lane_block(pm, 3))).astype(y_ref.dtype)

    return pl.pallas_call(
        body, name="mixer_a_fwd", grid=(nblk, nt),
        in_specs=_halo_specs(ts, nt, 4 * MIXER_LANES, nxt=False)
        + [pl.BlockSpec((conv_w.shape[0], MIXER_LANES), lambda cb, i: (0, cb))],
        out_specs=pl.BlockSpec((ts, MIXER_LANES), lambda cb, i: (i, cb)),
        out_shape=jax.ShapeDtypeStruct((T, nblk * MIXER_LANES), BF16), compiler_params=_params(),
    )(proj_a, proj_a, conv_w)


def _mixer_a_bwd(proj_a, dy, conv_w):
    T = proj_a.shape[0]
    nblk = proj_a.shape[1] // (4 * MIXER_LANES)
    ts = min(ROW_TILE, T)
    nt = T // ts
    kt = conv_w.shape[0]

    def body(pp, pm, pn, dym, dyn, w_ref, dp_ref, dw_ref):
        i = pl.program_id(1)
        hm, cm, bm, zm = (_lane_block(pm, k) for k in range(4))
        u_prev = jnp.where(i > 0, _lane_block(pp, 0) * _lane_block(pp, 1), 0.0)
        ext = jnp.concatenate([u_prev, hm * cm], axis=0)
        dy_ext = jnp.concatenate([dym[...], jnp.where(i < nt - 1, dyn[...], 0.0)], axis=0)
        b_ext = jnp.concatenate([bm, _lane_block(pn, 2)], axis=0)
        sz_ext = _silu(jnp.concatenate([zm, _lane_block(pn, 3)], axis=0))
        dcv_ext = dy_ext * b_ext * sz_ext
        w = [w_ref[j:j + 1, :] for j in range(kt)]
        du = sum(w[kt - 1 - j] * _ahead(dcv_ext, j, ts) for j in range(kt))
        taps = _taps(ext, kt, SUBLANES, ts)
        cv = sum(w[kt - 1 - j] * taps[j] for j in range(kt))
        for part, d in enumerate((du * cm, du * hm, dym[...] * cv * sz_ext[:ts], dym[...] * bm * cv * _dsilu(zm))):
            dp_ref[:, part * MIXER_LANES:(part + 1) * MIXER_LANES] = d.astype(dp_ref.dtype)
        dcv = dcv_ext[:ts]

        @pl.when(i == 0)
        def _():
            dw_ref[...] = jnp.zeros_like(dw_ref)

        for j in range(kt):
            dw_ref[j:j + 1, :] += jnp.sum(dcv * taps[kt - 1 - j], axis=0, keepdims=True)

    return pl.pallas_call(
        body, name="mixer_a_bwd", grid=(nblk, nt),
        in_specs=_halo_specs(ts, nt, 4 * MIXER_LANES) + _halo_specs(ts, nt, MIXER_LANES, prev=False)
        + [pl.BlockSpec((kt, MIXER_LANES), lambda cb, i: (0, cb))],
        out_specs=[pl.BlockSpec((ts, 4 * MIXER_LANES), lambda cb, i: (i, cb)),
                   pl.BlockSpec((SUBLANES, MIXER_LANES), lambda cb, i: (0, cb))],
        out_shape=[jax.ShapeDtypeStruct(proj_a.shape, BF16),
                   jax.ShapeDtypeStruct((SUBLANES, nblk * MIXER_LANES), F32)],
        compiler_params=_params(),
    )(proj_a, proj_a, proj_a, dy, dy, conv_w)


def _conv_b_fwd(raw, conv_w):
    T = raw.shape[0]
    nblk = raw.shape[1] // CONV_LANES
    ts = min(ROW_TILE, T)
    nt = T // ts
    kt = conv_w.shape[0]

    def body(rp, rm, w_ref, y_ref):
        i = pl.program_id(1)
        ext = jnp.concatenate([jnp.where(i > 0, rp[...], 0.0), rm[...]], axis=0)
        taps = _taps(ext, kt, SUBLANES, ts)
        y_ref[...] = _silu(sum(w_ref[kt - 1 - j:kt - j, :] * taps[j] for j in range(kt)))

    return pl.pallas_call(
        body, name="conv_b_fwd", grid=(nblk, nt),
        in_specs=_halo_specs(ts, nt, CONV_LANES, nxt=False) + [pl.BlockSpec((kt, CONV_LANES), lambda cb, i: (0, cb))],
        out_specs=pl.BlockSpec((ts, CONV_LANES), lambda cb, i: (i, cb)),
        out_shape=jax.ShapeDtypeStruct(raw.shape, F32), compiler_params=_params(),
    )(raw, raw, conv_w)


def _conv_b_bwd(raw, dy, conv_w):
    T = raw.shape[0]
    nblk = raw.shape[1] // CONV_LANES
    ts = min(ROW_TILE, T)
    nt = T // ts
    kt = conv_w.shape[0]

    def body(rp, rm, rn, dym, dyn, w_ref, dr_ref, dw_ref):
        i = pl.program_id(1)
        ext = jnp.concatenate([jnp.where(i > 0, rp[...], 0.0), rm[...], rn[...]], axis=0)
        w = [w_ref[j:j + 1, :] for j in range(kt)]
        taps = _taps(ext, kt, SUBLANES, ts + SUBLANES)
        xc_ext = sum(w[kt - 1 - j] * taps[j] for j in range(kt))
        dy_ext = jnp.concatenate([dym[...], jnp.where(i < nt - 1, dyn[...], 0.0)], axis=0)
        dxc_ext = dy_ext * _dsilu(xc_ext)
        dr_ref[...] = sum(w[kt - 1 - j] * _ahead(dxc_ext, j, ts) for j in range(kt)).astype(dr_ref.dtype)
        dxc = dxc_ext[:ts]

        @pl.when(i == 0)
        def _():
            dw_ref[...] = jnp.zeros_like(dw_ref)

        for j in range(kt):
            dw_ref[j:j + 1, :] += jnp.sum(dxc * taps[kt - 1 - j][:ts], axis=0, keepdims=True)

    return pl.pallas_call(
        body, name="conv_b_bwd", grid=(nblk, nt),
        in_specs=_halo_specs(ts, nt, CONV_LANES) + _halo_specs(ts, nt, CONV_LANES, prev=False)
        + [pl.BlockSpec((kt, CONV_LANES), lambda cb, i: (0, cb))],
        out_specs=[pl.BlockSpec((ts, CONV_LANES), lambda cb, i: (i, cb)),
                   pl.BlockSpec((SUBLANES, CONV_LANES), lambda cb, i: (0, cb))],
        out_shape=[jax.ShapeDtypeStruct(raw.shape, BF16), jax.ShapeDtypeStruct((SUBLANES, nblk * CONV_LANES), F32)],
        compiler_params=_params(),
    )(raw, raw, raw, dy, dy, conv_w)


def _split_heads(ref, base, nheads, rows=slice(None)):
    return jnp.stack([ref[rows, base + h * HEAD: base + (h + 1) * HEAD] for h in range(nheads)], axis=0)


def _store_heads(ref, base, x, rows=slice(None), accumulate=False):
    for h in range(x.shape[0]):
        lanes = slice(base + h * HEAD, base + (h + 1) * HEAD)
        if accumulate:
            ref[rows, lanes] += x[h]
        else:
            ref[rows, lanes] = x[h].astype(ref.dtype)


def _gdn_fwd(qkv, proj_zs, alog, dtb, gain, H):
    T = qkv.shape[0]
    C, HW = GDN_CHUNK, H * HEAD
    nc = T // C
    zw = HW + LANES

    def body(qkv_ref, zs_ref, alog_ref, dtb_ref, gain_ref, o_ref, sall_ref, s_scr):
        @pl.when(pl.program_id(0) == 0)
        def _():
            s_scr[...] = jnp.zeros_like(s_scr)

        sall_ref[0] = s_scr[...]
        outs, states = _gdn_step(
            s_scr[...], _split_heads(qkv_ref, 0, H), _split_heads(qkv_ref, HW, H),
            _split_heads(qkv_ref, 2 * HW, H), _split_heads(zs_ref, 0, H), zs_ref[:, HW:HW + LANES],
            alog_ref[...], dtb_ref[...], gain_ref[...])
        _store_heads(o_ref, 0, outs)
        s_scr[...] = states

    row = pl.BlockSpec((1, LANES), lambda i: (0, 0))
    return pl.pallas_call(
        body, name="gdn_fwd", grid=(nc,),
        in_specs=[pl.BlockSpec((C, 3 * HW), lambda i: (i, 0)), pl.BlockSpec((C, zw), lambda i: (i, 0)), row, row, row],
        out_specs=[pl.BlockSpec((C, HW), lambda i: (i, 0)), pl.BlockSpec((1, H, HEAD, HEAD), lambda i: (i, 0, 0, 0))],
        out_shape=[jax.ShapeDtypeStruct((T, HW), BF16), jax.ShapeDtypeStruct((nc, H, HEAD, HEAD), F32)],
        scratch_shapes=[pltpu.VMEM((H, HEAD, HEAD), F32)], compiler_params=_params(),
    )(qkv, proj_zs, alog, dtb, gain)


def _gdn_bwd(qkv, proj_zs, do, s_all, alog, dtb, gain, H):
    T = qkv.shape[0]
    C, HW = GDN_CHUNK, H * HEAD
    nc = T // C
    zw = HW + LANES

    def body(qkv_ref, zs_ref, do_ref, sin_ref, alog_ref, dtb_ref, gain_ref,
             dqkv_ref, dzs_ref, dalog_ref, ddtb_ref, dgain_ref, ds_scr):
        @pl.when(pl.program_id(0) == 0)
        def _():
            ds_scr[...] = jnp.zeros_like(ds_scr)
            dalog_ref[...] = jnp.zeros_like(dalog_ref)
            ddtb_ref[...] = jnp.zeros_like(ddtb_ref)
            dgain_ref[...] = jnp.zeros_like(dgain_ref)

        primals = (sin_ref[0], _split_heads(qkv_ref, 0, H),
                   _split_heads(qkv_ref, HW, H), _split_heads(qkv_ref, 2 * HW, H), _split_heads(zs_ref, 0, H),
                   zs_ref[:, HW:HW + LANES], alog_ref[...], dtb_ref[...], gain_ref[...])
        _, vjp = jax.vjp(_gdn_step, *primals)
        dS, dq, dk, dv, dz, dsmall, dalog, ddtb, dgain = vjp((_split_heads(do_ref, 0, H), ds_scr[...]))
        ds_scr[...] = dS
        _store_heads(dqkv_ref, 0, dq)
        _store_heads(dqkv_ref, HW, dk)
        _store_heads(dqkv_ref, 2 * HW, dv)
        _store_heads(dzs_ref, 0, dz)
        dzs_ref[:, HW:HW + LANES] = dsmall.astype(dzs_ref.dtype)
        dalog_ref[...] += dalog
        ddtb_ref[...] += ddtb
        dgain_ref[...] += dgain

    row = pl.BlockSpec((1, LANES), lambda i: (0, 0))
    rev = lambda i: nc - 1 - i
    return pl.pallas_call(
        body, name="gdn_bwd", grid=(nc,),
        in_specs=[pl.BlockSpec((C, 3 * HW), lambda i: (rev(i), 0)), pl.BlockSpec((C, zw), lambda i: (rev(i), 0)),
                  pl.BlockSpec((C, HW), lambda i: (rev(i), 0)),
                  pl.BlockSpec((1, H, HEAD, HEAD), lambda i: (rev(i), 0, 0, 0)), row, row, row],
        out_specs=[pl.BlockSpec((C, 3 * HW), lambda i: (rev(i), 0)), pl.BlockSpec((C, zw), lambda i: (rev(i), 0)),
                   row, row, row],
        out_shape=[jax.ShapeDtypeStruct(qkv.shape, F32), jax.ShapeDtypeStruct(proj_zs.shape, BF16)]
        + [jax.ShapeDtypeStruct((1, LANES), F32)] * 3,
        scratch_shapes=[pltpu.VMEM((H, HEAD, HEAD), F32)], compiler_params=_params(),
    )(qkv, proj_zs, do, s_all, alog, dtb, gain)


def _hgrn_refs(proj_ref, lb_ref, HP):
    W = HP * HEAD
    return (_split_heads(proj_ref, 0, HP), _split_heads(proj_ref, W, HP), _split_heads(proj_ref, 2 * W, HP),
            _split_heads(proj_ref, 3 * W, HP), _split_heads(lb_ref, 0, HP, slice(0, 1)),
            _split_heads(lb_ref, 0, HP, slice(1, 2)))


def _hgrn_fwd(proj, lower_bounds, gain, nheads):
    T = proj.shape[0]
    C, HP = HGRN_CHUNK, HGRN_HEADS_PER_STEP
    ng, nc, W = nheads // HP, T // C, HP * HEAD

    def body(proj_ref, lb_ref, gain_ref, o_ref, sall_ref, s_scr):
        @pl.when(pl.program_id(1) == 0)
        def _():
            s_scr[...] = jnp.zeros_like(s_scr)

        sall_ref[0] = s_scr[...]
        qr, fr, vi, z, lb0, lb1 = _hgrn_refs(proj_ref, lb_ref, HP)
        outs, states = _hgrn_step(s_scr[...], qr, fr, vi, z, lb0, lb1, gain_ref[...])
        _store_heads(o_ref, 0, outs)
        s_scr[...] = states

    return pl.pallas_call(
        body, name="hgrn_fwd", grid=(ng, nc),
        in_specs=[pl.BlockSpec((C, 4 * W), lambda g, i: (i, g)), pl.BlockSpec((2, W), lambda g, i: (0, g)),
                  pl.BlockSpec((1, LANES), lambda g, i: (0, 0))],
        out_specs=[pl.BlockSpec((C, W), lambda g, i: (i, g)),
                   pl.BlockSpec((1, HP, HEAD, HEAD), lambda g, i: (i, g, 0, 0))],
        out_shape=[jax.ShapeDtypeStruct((T, nheads * HEAD), BF16), jax.ShapeDtypeStruct((nc, nheads, HEAD, HEAD), F32)],
        scratch_shapes=[pltpu.VMEM((HP, HEAD, HEAD), F32)], compiler_params=_params(),
    )(proj, lower_bounds, gain)


def _hgrn_bwd(proj, do, s_all, lower_bounds, gain, nheads):
    T = proj.shape[0]
    C, HP = HGRN_CHUNK, HGRN_HEADS_PER_STEP
    ng, nc, W = nheads // HP, T // C, HP * HEAD

    def body(proj_ref, do_ref, sin_ref, lb_ref, gain_ref, dproj_ref, dlb_ref, dgain_ref, ds_scr):
        first = pl.program_id(1) == 0

        @pl.when(first)
        def _():
            ds_scr[...] = jnp.zeros_like(ds_scr)
            dlb_ref[...] = jnp.zeros_like(dlb_ref)

        @pl.when(first & (pl.program_id(0) == 0))
        def _():
            dgain_ref[...] = jnp.zeros_like(dgain_ref)

        qr, fr, vi, z, lb0, lb1 = _hgrn_refs(proj_ref, lb_ref, HP)
        primals = (sin_ref[0], qr, fr, vi, z, lb0, lb1, gain_ref[...])
        _, vjp = jax.vjp(_hgrn_step, *primals)
        dS, dq, df, dv, dz, dlb0, dlb1, dgain = vjp((_split_heads(do_ref, 0, HP), ds_scr[...]))
        ds_scr[...] = dS
        for part, d in enumerate((dq, df, dv, dz)):
            _store_heads(dproj_ref, part * W, d)
        _store_heads(dlb_ref, 0, dlb0, slice(0, 1), accumulate=True)
        _store_heads(dlb_ref, 0, dlb1, slice(1, 2), accumulate=True)
        dgain_ref[...] += dgain

    rev = lambda i: nc - 1 - i
    return pl.pallas_call(
        body, name="hgrn_bwd", grid=(ng, nc),
        in_specs=[pl.BlockSpec((C, 4 * W), lambda g, i: (rev(i), g)), pl.BlockSpec((C, W), lambda g, i: (rev(i), g)),
                  pl.BlockSpec((1, HP, HEAD, HEAD), lambda g, i: (rev(i), g, 0, 0)),
                  pl.BlockSpec((2, W), lambda g, i: (0, g)), pl.BlockSpec((1, LANES), lambda g, i: (0, 0))],
        out_specs=[pl.BlockSpec((C, 4 * W), lambda g, i: (rev(i), g)), pl.BlockSpec((2, W), lambda g, i: (0, g)),
                   pl.BlockSpec((1, LANES), lambda g, i: (0, 0))],
        out_shape=[jax.ShapeDtypeStruct(proj.shape, BF16), jax.ShapeDtypeStruct(lower_bounds.shape, F32),
                   jax.ShapeDtypeStruct((1, LANES), F32)],
        scratch_shapes=[pltpu.VMEM((HP, HEAD, HEAD), F32)], compiler_params=_params(),
    )(proj, do, s_all, lower_bounds, gain)


def _post_specs(T):
    tr = min(POST_TILE, T)
    tile = lambda w: pl.BlockSpec((tr, w), lambda i: (i, 0))
    full = lambda r, w: pl.BlockSpec((r, w), lambda i: (0, 0))
    return tr, tile, full


def _post_fwd(s, x, p, g, b, wg, wpl, name):
    T, D = x.shape
    P = p.shape[1]
    tr, tile, full = _post_specs(T)

    def body(s_ref, x_ref, p_ref, g_ref, b_ref, wg_ref, wpl_ref, o_ref, o16_ref):
        x1 = _post_norm(s_ref[...], x_ref[...], g_ref[...], b_ref[...])
        xn = _post_gate(x1, _dot_raw(x1, wg_ref[...], "nn", False), _dot_raw(p_ref[...], wpl_ref[...], "nn", False))
        o_ref[...] = xn
        o16_ref[...] = xn.astype(BF16)

    return pl.pallas_call(
        body, name=name, grid=(T // tr,),
        in_specs=[tile(D), tile(D), tile(P), full(1, D), full(1, D), full(D, D), full(P, D)],
        out_specs=[tile(D), tile(D)],
        out_shape=[jax.ShapeDtypeStruct((T, D), F32), jax.ShapeDtypeStruct((T, D), BF16)], compiler_params=_params(),
    )(s, x, p, g, b, wg, wpl)


def _post_bwd(s, x, p, g, b, wg, wpl, dnext, name, with_loss):
    T, D = x.shape
    P = p.shape[1]
    tr, tile, full = _post_specs(T)

    def body(s_ref, x_ref, p_ref, g_ref, b_ref, wg_ref, wpl_ref, dn_ref,
             ds_ref, dx_ref, dg_ref, db_ref, dwg_ref, dwpl_ref, loss_ref):
        @pl.when(pl.program_id(0) == 0)
        def _():
            for r in (dg_ref, db_ref, dwg_ref, dwpl_ref, loss_ref):
                r[...] = jnp.zeros_like(r)

        x1, vjp_norm = jax.vjp(_post_norm, s_ref[...], x_ref[...], g_ref[...], b_ref[...])
        gate_pre = _dot_raw(x1, wg_ref[...], "nn", False)
        pp = _dot_raw(p_ref[...], wpl_ref[...], "nn", False)
        xn, vjp_gate = jax.vjp(_post_gate, x1, gate_pre, pp)
        if with_loss:
            err = xn - dn_ref[...]
            loss_ref[...] += 0.5 * jnp.sum(jnp.sum(err * err, axis=-1, keepdims=True), axis=0, keepdims=True) / D
            dn = err / D
        else:
            dn = dn_ref[...]
        dx1, dgp, dpp = vjp_gate(dn)
        dwg_ref[...] += _dot_raw(x1, dgp, "tn", False)
        dwpl_ref[...] += _dot_raw(p_ref[...], dpp, "tn", False)
        dx1 = dx1 + _dot_raw(dgp, wg_ref[...], "nt", False)
        ds, dx, dg, db = vjp_norm(dx1)
        ds_ref[...] = ds.astype(ds_ref.dtype)
        dx_ref[...] = dx
        dg_ref[...] += dg
        db_ref[...] += db

    return pl.pallas_call(
        body, name=name, grid=(T // tr,),
        in_specs=[tile(D), tile(D), tile(P), full(1, D), full(1, D), full(D, D), full(P, D), tile(D)],
        out_specs=[tile(D), tile(D), full(1, D), full(1, D), full(D, D), full(P, D), full(SUBLANES, LANES)],
        out_shape=[jax.ShapeDtypeStruct((T, D), BF16), jax.ShapeDtypeStruct((T, D), F32)]
        + [jax.ShapeDtypeStruct((1, D), F32)] * 2
        + [jax.ShapeDtypeStruct((D, D), F32), jax.ShapeDtypeStruct((P, D), F32),
           jax.ShapeDtypeStruct((SUBLANES, LANES), F32)],
        compiler_params=_params(),
    )(s, x, p, g, b, wg, wpl, dnext)


def _adam_math(w, g, m, v):
    m = ADAM_B1 * m + (1.0 - ADAM_B1) * g
    v = ADAM_B2 * v + (1.0 - ADAM_B2) * (g * g)
    m_hat = m / (1.0 - ADAM_B1 ** ADAM_STEP)
    v_hat = v / (1.0 - ADAM_B2 ** ADAM_STEP)
    return -ADAM_LR * (m_hat / (jnp.sqrt(v_hat) + ADAM_EPS) + ADAM_WD * w), m, v


def _shard_tiles(R, C):
    tr = _pick(R, (256, 128, 64, 32, 16, 8))
    return (tr, C) if tr < R or R % SUBLANES == 0 else (R, _pick(C, (256, 128)))


def _rs_add(g8, got, cidx, name):
    _, R, C = g8.shape
    tr, tc = _shard_tiles(R, C)

    def body(c_ref, a_ref, b_ref, o_ref, o16_ref):
        total = a_ref[...] + b_ref[...]
        o_ref[...] = total
        o16_ref[...] = total.astype(BF16)

    out_spec = pl.BlockSpec((1, tr, tc), lambda q, i, j, c: (q, i, j))
    return pl.pallas_call(
        body, name=name,
        grid_spec=pltpu.PrefetchScalarGridSpec(
            num_scalar_prefetch=1, grid=(4, R // tr, C // tc),
            in_specs=[pl.BlockSpec((1, tr, tc), lambda q, i, j, c: (2 * q + c[0], i, j)),
                      pl.BlockSpec((1, tr, tc), lambda q, i, j, c: (q, i, j))],
            out_specs=[out_spec, out_spec]),
        out_shape=[jax.ShapeDtypeStruct((4,) + g8.shape[1:], F32), jax.ShapeDtypeStruct((4,) + g8.shape[1:], BF16)],
        compiler_params=_params(),
    )(cidx, g8, got)


def _adam_sharded(w, m, v, mine, got, qidx, name):
    L, R, C = w.shape
    tr, tc = _shard_tiles(R, C)
    nr = R // tr

    def body(q_ref, w_ref, m_ref, v_ref, p_ref, r0, r1, r2, g_ref, d_ref, mo_ref, vo_ref):
        g = ((p_ref[0] + r0[0].astype(F32)) + r1[0].astype(F32)) + r2[0].astype(F32)
        d, mn, vn = _adam_math(w_ref[0], g, m_ref[0], v_ref[0])
        g_ref[0] = g
        d_ref[0] = d
        mo_ref[0] = mn
        vo_ref[0] = vn

    t3 = pl.BlockSpec((1, tr, tc), lambda l, i, j, q: (l, i, j))
    slot = lambda k: pl.BlockSpec((1, tr, tc), lambda l, i, j, q: (k, l * nr + i, j))
    return pl.pallas_call(
        body, name=name,
        grid_spec=pltpu.PrefetchScalarGridSpec(
            num_scalar_prefetch=1, grid=(L, nr, C // tc),
            in_specs=[t3, t3, t3, pl.BlockSpec((1, tr, tc), lambda l, i, j, q: (q[0], l * nr + i, j)),
                      slot(0), slot(1), slot(2)],
            out_specs=[t3, t3, t3, t3]),
        out_shape=[jax.ShapeDtypeStruct((L, R, C), F32)] * 4, compiler_params=_params(),
    )(qidx, w, m, v, mine, got, got, got)


def _adam_replicated(w, m, v, g8):
    def body(w_ref, m_ref, v_ref, g_ref, go_ref, d_ref, mo_ref, vo_ref):
        g = g_ref[0]
        for k in range(1, 8):
            g = g + g_ref[k]
        d, mn, vn = _adam_math(w_ref[...], g, m_ref[...], v_ref[...])
        go_ref[...] = g
        d_ref[...] = d
        mo_ref[...] = mn
        vo_ref[...] = vn

    return pl.pallas_call(
        body, name="adam_replicated", out_shape=[jax.ShapeDtypeStruct(w.shape, F32)] * 4, compiler_params=_params(),
    )(w, m, v, g8)


def _place():
    return lax.axis_index("x"), lax.axis_index("y"), lax.axis_index("c")


def _all_gather(shard, name):
    def body(x_ref, out_ref, send_sems, recv_sems, local_sem):
        x, y, c = _place()
        me, sibling = (x, y, c), (x, y, 1 - c)
        chips = [(1 - x, y), (x, 1 - y), (1 - x, 1 - y)]

        def slab(px, py, pc):
            return out_ref.at[4 * px + 2 * py + pc]

        def copy(k, block, to, src=None):
            return pltpu.make_async_remote_copy(
                src_ref=slab(*block) if src is None else src, dst_ref=slab(*block),
                send_sem=send_sems.at[k], recv_sem=recv_sems.at[k], device_id=to, device_id_type=MESH)

        mine = pltpu.make_async_copy(x_ref, slab(*me), local_sem)
        mine.start()
        first = [copy(0, me, sibling, src=x_ref)]
        first += [copy(1 + j, me, (*chip, c), src=x_ref) for j, chip in enumerate(chips)]
        for cp in first:
            cp.start()
        passed = [copy(4 + j, (*chip, c), sibling) for j, chip in enumerate(chips)]
        for j, chip in enumerate(chips):
            copy(1 + j, (*chip, c), me).wait_recv()
            passed[j].start()
        copy(0, sibling, me).wait_recv()
        for j, chip in enumerate(chips):
            copy(4 + j, (*chip, 1 - c), me).wait_recv()
        for cp in first + passed:
            cp.wait_send()
        mine.wait()

    return pl.pallas_call(
        body, name=name, out_shape=jax.ShapeDtypeStruct((8,) + shard.shape, shard.dtype),
        in_specs=[pl.BlockSpec(memory_space=pl.ANY)], out_specs=pl.BlockSpec(memory_space=pl.ANY),
        scratch_shapes=[pltpu.SemaphoreType.DMA((7,)), pltpu.SemaphoreType.DMA((7,)), pltpu.SemaphoreType.DMA],
    )(shard)


def _rs_to_sibling(g8, name):
    def body(g_ref, out_ref, send_sems, recv_sems):
        x, y, c = _place()
        copies = [pltpu.make_async_remote_copy(
            src_ref=g_ref.at[2 * q + (1 - c)], dst_ref=out_ref.at[q], send_sem=send_sems.at[q],
            recv_sem=recv_sems.at[q], device_id=(x, y, 1 - c), device_id_type=MESH) for q in range(4)]
        for cp in copies:
            cp.start()
        for cp in copies:
            cp.wait()

    return pl.pallas_call(
        body, name=name, out_shape=jax.ShapeDtypeStruct((4,) + g8.shape[1:], g8.dtype),
        in_specs=[pl.BlockSpec(memory_space=pl.ANY)], out_specs=pl.BlockSpec(memory_space=pl.ANY),
        scratch_shapes=[pltpu.SemaphoreType.DMA((4,)), pltpu.SemaphoreType.DMA((4,))],
    )(g8)


_HBM = pl.BlockSpec(memory_space=pltpu.HBM)
_SEM = pl.BlockSpec(memory_space=pltpu.SEMAPHORE)
_DATAFLOW = pltpu.SideEffectType.DATAFLOW_SIDE_EFFECTING
TOKEN_SHAPE = (SUBLANES, LANES)


def _chip_plan(x, y, c):
    return [(2 * px + py, j, (px, py, c)) for j, (px, py) in enumerate([(1 - x, y), (x, 1 - y), (1 - x, 1 - y)])]


def _exchange_copies(plan, src_ref, land_ref, send_sems, recv_sems):
    return [pltpu.make_async_remote_copy(
        src_ref=src_ref.at[blk], dst_ref=land_ref.at[slot], send_sem=send_sems.at[k], recv_sem=recv_sems.at[k],
        device_id=peer, device_id_type=MESH) for k, (blk, slot, peer) in enumerate(plan(*_place()))]


def _exchange_start(src, n_slots, plan, name):
    land_shape = (n_slots,) + src.shape[1:]
    n = len(plan(0, 0, 0))

    def body(src_ref, land_ref, send_sems, recv_sems, src_thru, land_thru, token):
        for cp in _exchange_copies(plan, src_ref, land_ref, send_sems, recv_sems):
            cp.start()
        token[...] = jnp.zeros_like(token)

    return pl.pallas_call(
        body, name=name,
        out_shape=(pltpu.SemaphoreType.DMA((n,)), pltpu.SemaphoreType.DMA((n,)), pltpu.HBM(src.shape, src.dtype),
                   pltpu.HBM(land_shape, src.dtype), jax.ShapeDtypeStruct(TOKEN_SHAPE, F32)),
        in_specs=(_HBM, _HBM), out_specs=(_SEM, _SEM, _HBM, _HBM, pl.BlockSpec(memory_space=pltpu.VMEM)),
        input_output_aliases={0: 2, 1: 3}, compiler_params=pltpu.CompilerParams(has_side_effects=_DATAFLOW),
    )(pltpu.with_memory_space_constraint(src, pltpu.HBM),
      pltpu.with_memory_space_constraint(lax.empty(land_shape, src.dtype), pltpu.HBM))


def _exchange_wait(handle, plan, after, name):
    send_sems, recv_sems, src_thru, land_thru, _ = handle

    def body(src_ref, land_ref, send_sems, recv_sems, after_ref, src_dead, got_ref):
        for cp in _exchange_copies(plan, src_ref, land_ref, send_sems, recv_sems):
            cp.wait_send()
            cp.wait_recv()

    return pl.pallas_call(
        body, name=name,
        out_shape=(pltpu.HBM(src_thru.shape, src_thru.dtype), pltpu.HBM(land_thru.shape, land_thru.dtype)),
        in_specs=(_HBM, _HBM, _SEM, _SEM, pl.BlockSpec(memory_space=pl.ANY)), out_specs=(_HBM, _HBM),
        input_output_aliases={0: 0, 1: 1}, compiler_params=pltpu.CompilerParams(has_side_effects=_DATAFLOW),
    )(src_thru, land_thru, send_sems, recv_sems, after)[1]


def _gather_plan(x, y, c):
    me = 4 * x + 2 * y + c
    flip = lambda v, bit: 1 - v if bit else v
    return [(0, me, (flip(x, r >> 2 & 1), flip(y, r >> 1 & 1), flip(c, r & 1))) for r in range(1, 8)]


class _LateGather:
    def __init__(self, shard, name):
        self.shard, self.name = shard, name
        self.handle = _exchange_start(shard[None], 8, _gather_plan, name + "_start")

    def get(self, after):
        land = _exchange_wait(self.handle, _gather_plan, after, self.name + "_wait")
        x, y, c = _place()
        return lax.dynamic_update_slice(land, self.shard[None], (4 * x + 2 * y + c, 0, 0))


class _GradExchange:
    def __init__(self, cidx, qidx, layouts):
        self.cidx, self.qidx, self.layouts, self.pending = cidx, qidx, layouts, {}

    def start(self, tag, grad):
        g8 = self.layouts[tag](grad)
        got = _rs_to_sibling(g8, "rs_sibling_" + tag)
        chip_sums, chip_sums16 = _rs_add(g8, got, self.cidx, "rs_add_" + tag)
        handle = _exchange_start(chip_sums16, 3, _chip_plan, "rs_chips_start_" + tag)
        self.pending[tag] = (chip_sums, handle)
        return handle[4]

    def finish(self, tag, w, m, v, after):
        chip_sums, handle = self.pending.pop(tag)
        got2 = _exchange_wait(handle, _chip_plan, after, "rs_chips_wait_" + tag)
        return _adam_sharded(w, m, v, chip_sums, got2, self.qidx, "adam_" + tag)


def _local_grads(x, p0, p1, target, wt_zs, wt_a, wt_qkv, late, conv_a, conv_b,
                 a_log, dt_bias, gdn_gain, lower_bounds, hgrn_gain, ln_g, ln_b, on_grad=None):
    H = a_log.shape[1]
    pad_small = ((0, 0), (H, LANES - 2 * H))
    alog_row = jnp.pad(a_log, pad_small)
    dtb_row = jnp.pad(dt_bias, pad_small)

    x16 = x.astype(BF16)
    proj_zs = _matmul(x16, wt_zs, "nt", "proj_even_zs", after=late.started)
    proj_a = _matmul(x16, wt_a, "nt", "proj_even_a", after=late.started)
    proj_qkv = _matmul(x16, wt_qkv, "nt", "proj_even_qkv", after=late.started)
    y_a = _mixer_a_fwd(proj_a, conv_a)
    qkv = _conv_b_fwd(proj_qkv, conv_b)
    o2, s_gdn = _gdn_fwd(qkv, proj_zs, alog_row, dtb_row, gdn_gain, H)
    woute_a, woute_b = late.out_even(o2)
    wg, wpl = late.gate(o2)
    s_e = _matmul(o2, woute_b, "nn", "out_even_b", add=_matmul(y_a, woute_a, "nn", "out_even_a"))
    x2, x2_16 = _post_fwd(s_e, x, p0, ln_g[0:1], ln_b[0:1], wg[0], wpl[0], "post_even_fwd")
    wino, wouto = late.odd(s_e)
    nheads_o = wouto.shape[0] // HEAD
    proj_o = _matmul(x2_16, wino, "nn", "proj_odd")
    o4, s_hgrn = _hgrn_fwd(proj_o, lower_bounds, hgrn_gain, nheads_o)
    s_o = _matmul(o4, wouto, "nn", "out_odd")
    ds_o, dx2, dlng1, dlnb1, dwg1, dwpl1, loss = _post_bwd(
        s_o, x2, p1, ln_g[1:2], ln_b[1:2], wg[1], wpl[1], target, "post_odd_loss_bwd", True)
    do4 = _matmul(ds_o, wouto, "nt", "d_out_odd_act")
    grads = {}

    def emit(tag, grad):
        grads[tag] = grad
        return on_grad(tag, grad) if on_grad is not None else jnp.zeros(TOKEN_SHAPE, F32)

    tok = emit("w_out_odd", _matmul(o4, ds_o, "tn", "d_out_odd_w"))
    dproj_o, dlb, dhgain = _hgrn_bwd(proj_o, do4, s_hgrn, lower_bounds, hgrn_gain + tok[0:1], nheads_o)
    dx2 = _matmul(dproj_o, wino, "nt", "d_proj_odd_act", add=dx2)
    tok = emit("w_in_odd", _matmul(x2_16, dproj_o, "tn", "d_proj_odd_w"))
    ds_e, dx, dlng0, dlnb0, dwg0, dwpl0, _ = _post_bwd(
        s_e, x, p0, ln_g[0:1], ln_b[0:1] + tok[0:1, 0:1], wg[0], wpl[0], dx2, "post_even_bwd", False)
    tok = emit("w_pl_gate", jnp.stack([dwg0, dwg1])) + emit("w_pl", jnp.stack([dwpl0, dwpl1]))
    dy_a = _matmul(ds_e, woute_a, "nt", "d_out_even_a_act")
    do2 = _matmul(ds_e, woute_b, "nt", "d_out_even_b_act")
    dwoute_a = _matmul(y_a, ds_e, "tn", "d_out_even_a_w")
    dwoute_b = _matmul(o2, ds_e, "tn", "d_out_even_b_w")
    tok = tok + emit("w_out_even", jnp.concatenate([dwoute_a, dwoute_b], axis=0))
    dqkv, dproj_zs, dalog, ddtb, dggain = _gdn_bwd(qkv, proj_zs, do2, s_gdn, alog_row, dtb_row, gdn_gain + tok[0:1], H)
    dproj_qkv, dconv_b = _conv_b_bwd(proj_qkv, dqkv, conv_b)
    dproj_a, dconv_a = _mixer_a_bwd(proj_a, dy_a, conv_a)
    emit("conv", (dconv_a[:conv_a.shape[0]], dconv_b[:conv_b.shape[0]]))
    tok = emit("w_in_even", (_matmul(dproj_zs, x16, "tn", "d_proj_even_zs_w"), _matmul(dproj_a, x16, "tn", "d_proj_even_a_w"),
                             _matmul(dproj_qkv, x16, "tn", "d_proj_even_qkv_w")))
    dx = _matmul(dproj_zs, wt_zs, "nn", "d_proj_even_zs_act", add=dx, after=tok)
    dx = _matmul(dproj_a, wt_a, "nn", "d_proj_even_a_act", add=dx)
    dx = _matmul(dproj_qkv, wt_qkv, "nn", "d_proj_even_qkv_act", add=dx)
    grads.update(
        loss=loss[0, 0], grad_x=dx, a_log=dalog[:, H:2 * H], dt_bias=ddtb[:, H:2 * H], gdn_gain=dggain,
        lower_bounds=dlb, hgrn_gain=dhgain, ln_g=jnp.concatenate([dlng0, dlng1], axis=0),
        ln_b=jnp.concatenate([dlnb0, dlnb1], axis=0))
    return grads


def _pad_rows(a, rows):
    return jnp.pad(a, ((0, rows - a.shape[0]), (0, 0)))


def _pack_small(a_log, dt_bias, gdn_gain, lower_bounds, hgrn_gain, ln_g, ln_b):
    lane_pad = lambda a: _pad_rows(jnp.pad(a, ((0, 0), (0, LANES - a.shape[1]))), SUBLANES)
    parts = [lane_pad(a_log), lane_pad(dt_bias), lane_pad(gdn_gain), lower_bounds.reshape(-1, LANES),
             lane_pad(hgrn_gain), ln_g.reshape(-1, LANES), ln_b.reshape(-1, LANES)]
    packed = jnp.concatenate(parts, axis=0)
    assert packed.shape[0] == SMALL_ROWS, packed.shape
    return packed


def _unpack_small(packed, shapes):
    out, r = [], 0
    for shp in shapes:
        n = shp[0] * shp[1]
        if n < LANES * SUBLANES and shp[1] <= LANES:
            out.append(packed[r:r + shp[0], :shp[1]])
            r += SUBLANES
        else:
            rows = n // LANES
            out.append(packed[r:r + rows].reshape(shp))
            r += rows
    return out


def _split_in_even(wt_full, AW, HW, H):
    D = wt_full.shape[1]
    n_a = 4 * AW
    n_main = n_a + 3 * HW
    wt_zs = jnp.concatenate([wt_full[n_main:n_main + HW], wt_full[n_main + HW:],
                             jnp.zeros((LANES - 2 * H, D), wt_full.dtype)], axis=0)
    wt_a = wt_full[:n_a].reshape(4, AW // MIXER_LANES, MIXER_LANES, D).transpose(1, 0, 2, 3).reshape(n_a, D)
    return wt_zs, wt_a, wt_full[n_a:n_main]


def _join_in_even(dt_zs, dt_a, dt_qkv, AW, HW, H):
    D = dt_a.shape[1]
    a_nat = dt_a.reshape(AW // MIXER_LANES, 4, MIXER_LANES, D).transpose(1, 0, 2, 3).reshape(4 * AW, D)
    return jnp.concatenate([a_nat, dt_qkv, dt_zs[:HW], dt_zs[HW:HW + 2 * H]], axis=0)


def kernel(x, p, w_in_even, conv_a_w, conv_b_w, a_log, dt_bias, gdn_norm_g, w_out_even, w_in_odd, lower_bounds, hgrn_norm_g, w_out_odd, ln_g, ln_b, w_pl, w_pl_gate, loss_target, m_w_in_even, m_conv_a_w, m_conv_b_w, m_a_log, m_dt_bias, m_gdn_norm_g, m_w_out_even, m_w_in_odd, m_lower_bounds, m_hgrn_norm_g, m_w_out_odd, m_ln_g, m_ln_b, m_w_pl, m_w_pl_gate, v_w_in_even, v_conv_a_w, v_conv_b_w, v_a_log, v_dt_bias, v_gdn_norm_g, v_w_out_even, v_w_in_odd, v_lower_bounds, v_hgrn_norm_g, v_w_out_odd, v_ln_g, v_ln_b, v_w_pl, v_w_pl_gate):
    xi, yi, ci = _place()
    cidx = jnp.reshape(ci, (1,)).astype(jnp.int32)
    qidx = jnp.reshape(2 * xi + yi, (1,)).astype(jnp.int32)
    D = x.shape[2]
    H = a_log.shape[1]
    HW = H * HEAD
    AW = conv_a_w.shape[2] * 8
    OW = w_out_odd.shape[1] * 8
    PD = w_pl.shape[1]
    ka, kb = conv_a_w.shape[1], conv_b_w.shape[1]
    ca, cb = conv_a_w.shape[2], conv_b_w.shape[2]
    gw = HGRN_HEADS_PER_STEP * HEAD
    ngrp = OW // gw

    transposed = lambda a: jnp.transpose(a, (0, 2, 1))
    g_ine = _all_gather(transposed(w_in_even)[0].astype(BF16), "ag_w_in_even")
    wt_zs, wt_a, wt_qkv = _split_in_even(g_ine.reshape(-1, D), AW, HW, H)
    behind = lambda shard, dep: lax.optimization_barrier((shard, dep))[0]
    late_oute = _LateGather(behind(w_out_even[0].astype(BF16), g_ine), "ag_w_out_even")
    late_gate = _LateGather(behind(w_pl_gate.astype(BF16).reshape(-1, D), late_oute.handle[4]), "ag_w_pl_gate")
    late_pl = _LateGather(behind(w_pl.astype(BF16).reshape(DEPTH * PD, -1), late_gate.handle[4]), "ag_w_pl")
    late_ino = _LateGather(behind(w_in_odd[0].astype(BF16), late_pl.handle[4]), "ag_w_in_odd")
    late_outo = _LateGather(behind(w_out_odd[0].astype(BF16), late_ino.handle[4]), "ag_w_out_odd")

    class _Late:
        started = sum(g.handle[4] for g in (late_oute, late_gate, late_pl, late_ino, late_outo))

        @staticmethod
        def out_even(after):
            woute = late_oute.get(after).reshape(-1, D)
            return woute[:AW], woute[AW:]

        @staticmethod
        def gate(after):
            g_gate, g_pl = late_gate.get(after), late_pl.get(after)
            return (g_gate.reshape(8, DEPTH, D // 8, D).transpose(1, 0, 2, 3).reshape(DEPTH, D, D),
                    g_pl.reshape(8, DEPTH, PD, D // 8).transpose(1, 2, 0, 3).reshape(DEPTH, PD, D))

        @staticmethod
        def odd(after):
            g_ino = late_ino.get(after)
            wino = jnp.transpose(g_ino, (1, 0, 2)).reshape(D, 4, ngrp, gw).transpose(0, 2, 1, 3).reshape(D, 4 * OW)
            return wino, late_outo.get(after).reshape(-1, D)

    taps = jnp.concatenate([_pad_rows(conv_a_w[0], SUBLANES), _pad_rows(conv_b_w[0], SUBLANES)], axis=1)
    g_taps = _all_gather(taps, "ag_conv")
    conv_a = jnp.transpose(g_taps[:, :ka, :ca], (1, 0, 2)).reshape(ka, 8 * ca)
    conv_b = jnp.transpose(g_taps[:, :kb, ca:], (1, 0, 2)).reshape(kb, 8 * cb)

    sh = w_in_even.shape[2]
    tap_blocks = lambda g, width: _pad_rows(g, SUBLANES).reshape(SUBLANES, 8, width).transpose(1, 0, 2)
    owner_layout = {
        "w_in_even": lambda g: _join_in_even(*g, AW, HW, H).reshape(8, sh, D),
        "w_in_odd": lambda g: g.reshape(D, ngrp, 4, gw).transpose(0, 2, 1, 3).reshape(D, 8, 4 * OW // 8).transpose(1, 0, 2),
        "w_out_even": lambda g: g.reshape(8, -1, D),
        "w_out_odd": lambda g: g.reshape(8, -1, D),
        "w_pl_gate": lambda g: g.reshape(DEPTH, 8, D // 8, D).transpose(1, 0, 2, 3).reshape(8, DEPTH * D // 8, D),
        "w_pl": lambda g: g.reshape(DEPTH, PD, 8, D // 8).transpose(2, 0, 1, 3).reshape(8, DEPTH * PD, D // 8),
        "conv": lambda g: jnp.concatenate([tap_blocks(g[0], ca), tap_blocks(g[1], cb)], axis=2),
    }
    exchange = _GradExchange(cidx, qidx, owner_layout)
    gr = _local_grads(x[0], p[0, 0], p[1, 0], loss_target[0], wt_zs, wt_a, wt_qkv, _Late, conv_a, conv_b,
                      a_log, dt_bias, gdn_norm_g, lower_bounds, hgrn_norm_g, ln_g, ln_b, on_grad=exchange.start)
    loss = lax.psum(gr["loss"], AXES)

    last = gr["grad_x"]
    pack_taps = lambda a, b: jnp.concatenate([_pad_rows(a[0], SUBLANES), _pad_rows(b[0], SUBLANES)], axis=1)[None]
    o_outo = exchange.finish("w_out_odd", w_out_odd, m_w_out_odd, v_w_out_odd, last)
    o_ino = exchange.finish("w_in_odd", w_in_odd, m_w_in_odd, v_w_in_odd, last)
    o_gate = exchange.finish("w_pl_gate", w_pl_gate, m_w_pl_gate, v_w_pl_gate, last)
    o_pl = exchange.finish("w_pl", w_pl, m_w_pl, v_w_pl, last)
    o_oute = exchange.finish("w_out_even", w_out_even, m_w_out_even, v_w_out_even, last)
    o_taps = exchange.finish("conv", taps[None], pack_taps(m_conv_a_w, m_conv_b_w), pack_taps(v_conv_a_w, v_conv_b_w), last)
    others_done = sum(o[1][0, 0:1, 0:1] for o in (o_outo, o_ino, o_gate, o_pl, o_oute, o_taps))
    o_ine = [transposed(o) for o in exchange.finish(
        "w_in_even", transposed(w_in_even), transposed(m_w_in_even), transposed(v_w_in_even), others_done)]

    small_g = _pack_small(gr["a_log"], gr["dt_bias"], gr["gdn_gain"], gr["lower_bounds"], gr["hgrn_gain"],
                          gr["ln_g"], gr["ln_b"])
    o_small = _adam_replicated(
        _pack_small(a_log, dt_bias, gdn_norm_g, lower_bounds, hgrn_norm_g, ln_g, ln_b),
        _pack_small(m_a_log, m_dt_bias, m_gdn_norm_g, m_lower_bounds, m_hgrn_norm_g, m_ln_g, m_ln_b),
        _pack_small(v_a_log, v_dt_bias, v_gdn_norm_g, v_lower_bounds, v_hgrn_norm_g, v_ln_g, v_ln_b),
        _all_gather(small_g, "ag_small_grads"))
    small_shapes = [a_log.shape, dt_bias.shape, gdn_norm_g.shape, lower_bounds.shape, hgrn_norm_g.shape,
                    ln_g.shape, ln_b.shape]

    def leaves(kind):
        s_alog, s_dt, s_gg, s_lb, s_hg, s_lng, s_lnb = _unpack_small(o_small[kind], small_shapes)
        t = o_taps[kind]
        return [o_ine[kind], t[:, :ka, :ca], t[:, :kb, ca:], s_alog, s_dt, s_gg, o_oute[kind],
                o_ino[kind], s_lb, s_hg, o_outo[kind], s_lng, s_lnb, o_pl[kind], o_gate[kind]]

    return (loss, gr["grad_x"][None], *leaves(0), *leaves(1), *leaves(2), *leaves(3))
```

```python
import functools

import jax
import jax.numpy as jnp
from jax import lax
from jax.experimental import pallas as pl
from jax.experimental.pallas import tpu as pltpu

F32 = jnp.float32
BF16 = jnp.bfloat16
MESH = pl.DeviceIdType.MESH
AXES = ("x", "y", "c")

LANES = 128
SUBLANES = 8
HEAD = 128
GDN_CHUNK = 64
HGRN_CHUNK = 32
HGRN_SUB = 16
HGRN_HEADS_PER_STEP = 16
NORM_EPS = 1e-5
DEPTH = 2
ALPHA = (2.0 * DEPTH) ** 0.25
EXP_CLAMP = 80.0
ADAM_LR, ADAM_B1, ADAM_B2, ADAM_EPS, ADAM_WD, ADAM_STEP = 0.001, 0.9, 0.999, 1e-08, 0.01, 10
VMEM_LIMIT = 56 * 1024 * 1024
MATMUL_VMEM = 36 * 1024 * 1024
ROW_TILE = 512
MIXER_LANES = 256
CONV_LANES = 512
POST_TILE = 256
SMALL_ROWS = 96

_NOBATCH, _BATCH0 = ((), ()), ((0,), (0,))
_DIMS = {"nn": (((1,), (0,)), _NOBATCH), "nt": (((1,), (1,)), _NOBATCH), "tn": (((0,), (0,)), _NOBATCH),
         "bnn": (((2,), (1,)), _BATCH0), "bnt": (((2,), (2,)), _BATCH0), "btn": (((1,), (1,)), _BATCH0)}


def _params(**kw):
    return pltpu.CompilerParams(vmem_limit_bytes=VMEM_LIMIT, **kw)


def _dot_raw(a, b, kind, hi):
    if hi:
        return lax.dot_general(a, b, _DIMS[kind], precision=lax.Precision.HIGHEST, preferred_element_type=F32)
    return lax.dot_general(a.astype(BF16), b.astype(BF16), _DIMS[kind], preferred_element_type=F32)


@functools.partial(jax.custom_vjp, nondiff_argnums=(2, 3))
def mdot(a, b, kind, hi):
    return _dot_raw(a, b, kind, hi)


def _mdot_fwd(a, b, kind, hi):
    return _dot_raw(a, b, kind, hi), (a, b)


def _mdot_bwd(kind, hi, res, g):
    a, b = res
    pre, base = kind[:-2], kind[-2:]
    if base == "nn":
        return _dot_raw(g, b, pre + "nt", hi), _dot_raw(a, g, pre + "tn", hi)
    if base == "nt":
        return _dot_raw(g, b, pre + "nn", hi), _dot_raw(g, a, pre + "tn", hi)
    return _dot_raw(b, g, pre + "nt", hi), _dot_raw(a, g, pre + "nn", hi)


mdot.defvjp(_mdot_fwd, _mdot_bwd)


def _rows(x, lo, hi):
    return _take_rows(x, lo, hi, x.shape[-2])


@functools.partial(jax.custom_vjp, nondiff_argnums=(1, 2, 3))
def _take_rows(x, lo, hi, n):
    return x[..., lo:hi, :]


def _take_rows_fwd(x, lo, hi, n):
    return x[..., lo:hi, :], None


def _take_rows_bwd(lo, hi, n, _, g):
    parts = []
    if lo > 0:
        parts.append(jnp.zeros(g.shape[:-2] + (lo, g.shape[-1]), g.dtype))
    parts.append(g)
    if n - hi > 0:
        parts.append(jnp.zeros(g.shape[:-2] + (n - hi, g.shape[-1]), g.dtype))
    return (jnp.concatenate(parts, axis=-2) if len(parts) > 1 else g,)


_take_rows.defvjp(_take_rows_fwd, _take_rows_bwd)


def _heads_of(wide, nheads):
    return jnp.stack([wide[:, h * HEAD:(h + 1) * HEAD] for h in range(nheads)], axis=0)


def _wide_of(x):
    return jnp.concatenate([x[h] for h in range(x.shape[0])], axis=1)


@functools.partial(jax.custom_vjp, nondiff_argnums=(1,))
def to_heads(wide, nheads):
    return _heads_of(wide, nheads)


to_heads.defvjp(lambda wide, nheads: (_heads_of(wide, nheads), None), lambda nheads, _, g: (_wide_of(g),))


@jax.custom_vjp
def to_wide(x):
    return _wide_of(x)


to_wide.defvjp(lambda x: (_wide_of(x), None), lambda _, g: (_heads_of(g, g.shape[1] // HEAD),))


def _sigmoid(x):
    return jax.nn.sigmoid(x)


def _silu(x):
    return x * _sigmoid(x)


def _dsilu(x):
    s = _sigmoid(x)
    return s * (1.0 + x * (1.0 - s))


def _log1p(u):
    return jnp.where(u < 1e-4, u * (1.0 - 0.5 * u), jnp.log(1.0 + u))


def _softplus(x):
    return jnp.maximum(x, 0.0) + _log1p(jnp.exp(-jnp.abs(x)))


def _rms_gate(o, gain, z):
    return o * lax.rsqrt(jnp.mean(o * o, axis=-1, keepdims=True) + NORM_EPS) * gain * _silu(z)


def _l2n(x):
    return x * lax.rsqrt(jnp.sum(x * x, axis=-1, keepdims=True) + 1e-6)


def _split_dot_raw(m, x, kind):
    mb = m.astype(BF16)
    hi = x.astype(BF16)
    lo = (x - hi.astype(F32)).astype(BF16)
    dims = _DIMS[kind]
    return (lax.dot_general(mb, hi, dims, preferred_element_type=F32)
            + lax.dot_general(mb, lo, dims, preferred_element_type=F32))


@jax.custom_vjp
def mask_dot(m, x):
    return _split_dot_raw(m, x, "nn")


def _mask_dot_fwd(m, x):
    return _split_dot_raw(m, x, "nn"), m


def _mask_dot_bwd(m, g):
    return jnp.zeros_like(m), _split_dot_raw(m, g, "tn")


mask_dot.defvjp(_mask_dot_fwd, _mask_dot_bwd)


def _neumann_rest(low):
    n = low.shape[-1]
    rest = -low
    power = low
    span = 2
    while span < n:
        power = _dot_raw(power, power, "bnn", False)
        rest = rest + power + _dot_raw(rest, power, "bnn", False)
        span *= 2
    return rest


@jax.custom_vjp
def _unit_lower_inverse_minus_eye(low):
    return _neumann_rest(low)


def _inverse_fwd(low):
    rest = _neumann_rest(low)
    return rest, rest


def _inverse_bwd(rest, g):
    left = g + _dot_raw(rest, g, "btn", False)
    return (-(left + _dot_raw(left, rest, "bnt", False)),)


_unit_lower_inverse_minus_eye.defvjp(_inverse_fwd, _inverse_bwd)


def _gdn_step(S, q, k, v, z, small, alog, dtb, gain):
    H = S.shape[0]
    C = GDN_CHUNK
    row = lax.broadcasted_iota(jnp.int32, (C, C), 0)
    col = lax.broadcasted_iota(jnp.int32, (C, C), 1)
    tril, strict, eye = (row >= col)[None], (row > col)[None], (row == col)[None]
    head = lax.broadcasted_iota(jnp.int32, (H, 1, LANES), 0)
    lane = lax.broadcasted_iota(jnp.int32, (H, 1, LANES), 2)
    rowc = lax.broadcasted_iota(jnp.int32, (1, C, 1), 1)
    beta_all = _sigmoid(small)
    g_all = -jnp.exp(alog) * _softplus(small + dtb)
    gc_all = mask_dot((row >= col).astype(F32), g_all)
    beta = jnp.sum(jnp.where(lane == head, beta_all[None], 0.0), axis=-1, keepdims=True)
    gc = jnp.sum(jnp.where(lane == head + H, gc_all[None], 0.0), axis=-1, keepdims=True)
    gc_row = jnp.sum(jnp.where(eye, gc, 0.0), axis=1, keepdims=True)
    decay = jnp.where(tril, jnp.exp(jnp.where(tril, gc - gc_row, 0.0)), 0.0)
    g_last = jnp.sum(jnp.where(rowc == C - 1, gc, 0.0), axis=1, keepdims=True)
    qn = _l2n(q) * (HEAD ** -0.5)
    kn = _l2n(k)
    kb = kn * beta
    low = jnp.where(strict, beta * mdot(kn, kn, "bnt", False) * decay, 0.0)
    inv_rest = _unit_lower_inverse_minus_eye(low)
    eg = jnp.exp(gc)
    vb, kbe = v * beta, kb * eg
    u = vb + mdot(inv_rest, vb, "bnn", False)
    w = kbe + mdot(inv_rest, kbe, "bnn", False)
    attn = mdot(qn, kn, "bnt", False) * decay
    v_new = u - mdot(w, S, "bnn", False)
    o = mdot(qn * eg, S, "bnn", False) + mdot(attn, v_new, "bnn", False)
    k_dec = kn * jnp.exp(g_last - gc)
    return _rms_gate(o, gain, z), S * jnp.exp(g_last) + mdot(k_dec, v_new, "btn", False)


def _hgrn_step(St, qr, fr, vi, z, lb0, lb1, gain):
    H = St.shape[0]
    C, SB = HGRN_CHUNK, HGRN_SUB
    row = lax.broadcasted_iota(jnp.int32, (C, C), 0)
    col = lax.broadcasted_iota(jnp.int32, (C, C), 1)
    blk_start = row - (row & (SB - 1))
    in_blk_f = ((row >= col) & (col >= blk_start)).astype(F32)
    before_f = (col < blk_start).astype(F32)
    sums_f = jnp.concatenate([in_blk_f, before_f], axis=0)
    m = jnp.maximum(lb0, lb1)
    e0, e1 = jnp.exp(lb0 - m), jnp.exp(lb1 - m)
    lb = e1 / (e0 + e1)
    f = lb + (1.0 - lb) * _sigmoid(fr)
    q = _silu(qr)
    k = 1.0 - f
    logf = jnp.log(f)
    sums = mask_dot(sums_f, to_wide(logf))
    inner, start = to_heads(_rows(sums, 0, C), H), to_heads(_rows(sums, C, 2 * C), H)
    b = start + inner
    b_last = jnp.sum(logf, axis=1, keepdims=True)
    o = mdot(q * jnp.exp(b), St, "bnt", False)
    qt = q * jnp.exp(inner)
    parts = []
    for blk in range(C // SB):
        lo, n = blk * SB, (blk + 1) * SB
        ref = jnp.concatenate([_rows(start, lo, n)] * (blk + 1), axis=1)
        kt = _rows(k, 0, n) * jnp.exp(jnp.minimum(ref - _rows(b, 0, n), EXP_CLAMP))
        att = mdot(_rows(qt, lo, n), kt, "bnt", False)
        t_idx = lax.broadcasted_iota(jnp.int32, (1, SB, n), 1) + lo
        s_idx = lax.broadcasted_iota(jnp.int32, (1, SB, n), 2)
        att = jnp.where(s_idx <= t_idx, att, 0.0)
        parts.append(mdot(att, _rows(vi, 0, n), "bnn", False))
    o = o + jnp.concatenate(parts, axis=1)
    k_dec = k * jnp.exp(b_last - b)
    return _rms_gate(o, gain, z), St * jnp.exp(b_last) + mdot(vi, k_dec, "btn", False)


def _post_norm(s, x, g, b):
    r = ALPHA * x + s
    d = r - jnp.mean(r, axis=-1, keepdims=True)
    var = jnp.mean(d * d, axis=-1, keepdims=True)
    return d * lax.rsqrt(var + NORM_EPS) * g + b


def _post_gate(x1, gate_pre, pp):
    return x1 + pp * _sigmoid(gate_pre)


def _pick(dim, cands):
    for c in cands:
        if dim % c == 0:
            return c
    return dim


def _matmul_tiles(M, K, tn, a_bytes, b_bytes, has_add):
    for tk in (4096, 2048, 1536, 1152, 1024, 640, 512, 384, 256, 128):
        if K % tk:
            continue
        for tm in (2048, 1152, 1024, 512, 384, 256, 128):
            if M % tm:
                continue
            blocks = tm * tk * a_bytes + tk * tn * b_bytes + tm * tn * 4 * (2 if has_add else 1)
            if 2 * blocks + (tm * tn * 4 if tk < K else 0) <= MATMUL_VMEM and tm >= min(M, 1024):
                return tm, tk
    return _pick(M, (512, 256, 128)), _pick(K, (512, 256, 128))


def _matmul(a, b, kind, name, add=None, after=None):
    if kind == "nn":
        (M, K), N = a.shape, b.shape[1]
    elif kind == "nt":
        (M, K), N = a.shape, b.shape[0]
    else:
        (K, M), N = a.shape, b.shape[1]
    has_add = add is not None
    tn = _pick(N, (512, 640, 384, 256, 128))
    tm, tk = _matmul_tiles(M, K, tn, a.dtype.itemsize, b.dtype.itemsize, has_add)
    nk = K // tk
    a_spec = pl.BlockSpec((tk, tm), lambda i, j, k: (k, i)) if kind == "tn" else pl.BlockSpec((tm, tk), lambda i, j, k: (i, k))
    b_spec = pl.BlockSpec((tn, tk), lambda i, j, k: (j, k)) if kind == "nt" else pl.BlockSpec((tk, tn), lambda i, j, k: (k, j))
    o_spec = pl.BlockSpec((tm, tn), lambda i, j, k: (i, j))

    extra = ([add] if has_add else []) + ([after] if after is not None else [])
    extra_specs = ([o_spec] if has_add else []) + ([pl.BlockSpec(TOKEN_SHAPE, lambda i, j, k: (0, 0))] if after is not None else [])

    def body(a_ref, b_ref, *rest):
        add_ref = rest[0] if has_add else None
        o_ref = rest[len(extra)]
        part = _dot_raw(a_ref[...], b_ref[...], kind, False)
        if nk == 1:
            o_ref[...] = part + add_ref[...] if has_add else part
            return
        acc = rest[-1]
        kk = pl.program_id(2)

        @pl.when(kk == 0)
        def _():
            acc[...] = part

        @pl.when(kk > 0)
        def _():
            acc[...] += part

        @pl.when(kk == nk - 1)
        def _():
            o_ref[...] = acc[...] + add_ref[...] if has_add else acc[...]

    return pl.pallas_call(
        body, name=name, grid=(M // tm, N // tn, nk),
        in_specs=[a_spec, b_spec] + extra_specs,
        out_specs=o_spec, out_shape=jax.ShapeDtypeStruct((M, N), F32),
        scratch_shapes=[pltpu.VMEM((tm, tn), F32)] if nk > 1 else [],
        compiler_params=_params(dimension_semantics=("parallel", "parallel", "arbitrary")),
    )(a, b, *extra)


def _halo_specs(ts, nt, width, prev=True, main=True, nxt=True):
    per = ts // SUBLANES
    last8 = nt * per - 1
    specs = []
    if prev:
        specs.append(pl.BlockSpec((SUBLANES, width), lambda cb, i: (jnp.maximum(i * per - 1, 0), cb)))
    if main:
        specs.append(pl.BlockSpec((ts, width), lambda cb, i: (i, cb)))
    if nxt:
        specs.append(pl.BlockSpec((SUBLANES, width), lambda cb, i: (jnp.minimum((i + 1) * per, last8), cb)))
    return specs


def _taps(ext, ktaps, lo, size):
    return [ext[lo:lo + size] if j == 0 else pltpu.roll(ext, j, 0)[lo:lo + size] for j in range(ktaps)]


def _ahead(ext, j, size):
    n = ext.shape[0]
    return ext[:size] if j == 0 else pltpu.roll(ext, n - j, 0)[:size]


def _lane_block(ref, k):
    return ref[:, k * MIXER_LANES:(k + 1) * MIXER_LANES]


def _mixer_a_fwd(proj_a, conv_w):
    T = proj_a.shape[0]
    nblk = proj_a.shape[1] // (4 * MIXER_LANES)
    ts = min(ROW_TILE, T)
    nt = T // ts

    def body(pp, pm, w_ref, y_ref):
        i = pl.program_id(1)
        u_prev = jnp.where(i > 0, _lane_block(pp, 0) * _lane_block(pp, 1), 0.0)
        ext = jnp.concatenate([u_prev, _lane_block(pm, 0) * _lane_block(pm, 1)], axis=0)
        t0, t1, t2 = _taps(ext, 3, SUBLANES, ts)
        cv = w_ref[2:3, :] * t0 + w_ref[1:2, :] * t1 + w_ref[0:1, :] * t2
        y_ref[...] = (_lane_block(pm, 2) * cv * _silu(_lane_block(pm, 3))).astype(y_ref.dtype)

    return pl.pallas_call(
        body, name="mixer_a_fwd", grid=(nblk, nt),
        in_specs=_halo_specs(ts, nt, 4 * MIXER_LANES, nxt=False)
        + [pl.BlockSpec((conv_w.shape[0], MIXER_LANES), lambda cb, i: (0, cb))],
        out_specs=pl.BlockSpec((ts, MIXER_LANES), lambda cb, i: (i, cb)),
        out_shape=jax.ShapeDtypeStruct((T, nblk * MIXER_LANES), BF16), compiler_params=_params(),
    )(proj_a, proj_a, conv_w)


def _mixer_a_bwd(proj_a, dy, conv_w):
    T = proj_a.shape[0]
    nblk = proj_a.shape[1] // (4 * MIXER_LANES)
    ts = min(ROW_TILE, T)
    nt = T // ts
    kt = conv_w.shape[0]

    def body(pp, pm, pn, dym, dyn, w_ref, dp_ref, dw_ref):
        i = pl.program_id(1)
        hm, cm, bm, zm = (_lane_block(pm, k) for k in range(4))
        u_prev = jnp.where(i > 0, _lane_block(pp, 0) * _lane_block(pp, 1), 0.0)
        ext = jnp.concatenate([u_prev, hm * cm], axis=0)
        dy_ext = jnp.concatenate([dym[...], jnp.where(i < nt - 1, dyn[...], 0.0)], axis=0)
        b_ext = jnp.concatenate([bm, _lane_block(pn, 2)], axis=0)
        sz_ext = _silu(jnp.concatenate([zm, _lane_block(pn, 3)], axis=0))
        dcv_ext = dy_ext * b_ext * sz_ext
        w = [w_ref[j:j + 1, :] for j in range(kt)]
        du = sum(w[kt - 1 - j] * _ahead(dcv_ext, j, ts) for j in range(kt))
        taps = _taps(ext, kt, SUBLANES, ts)
        cv = sum(w[kt - 1 - j] * taps[j] for j in range(kt))
        for part, d in enumerate((du * cm, du * hm, dym[...] * cv * sz_ext[:ts], dym[...] * bm * cv * _dsilu(zm))):
            dp_ref[:, part * MIXER_LANES:(part + 1) * MIXER_LANES] = d.astype(dp_ref.dtype)
        dcv = dcv_ext[:ts]

        @pl.when(i == 0)
        def _():
            dw_ref[...] = jnp.zeros_like(dw_ref)

        for j in range(kt):
            dw_ref[j:j + 1, :] += jnp.sum(dcv * taps[kt - 1 - j], axis=0, keepdims=True)

    return pl.pallas_call(
        body, name="mixer_a_bwd", grid=(nblk, nt),
        in_specs=_halo_specs(ts, nt, 4 * MIXER_LANES) + _halo_specs(ts, nt, MIXER_LANES, prev=False)
        + [pl.BlockSpec((kt, MIXER_LANES), lambda cb, i: (0, cb))],
        out_specs=[pl.BlockSpec((ts, 4 * MIXER_LANES), lambda cb, i: (i, cb)),
                   pl.BlockSpec((SUBLANES, MIXER_LANES), lambda cb, i: (0, cb))],
        out_shape=[jax.ShapeDtypeStruct(proj_a.shape, BF16),
                   jax.ShapeDtypeStruct((SUBLANES, nblk * MIXER_LANES), F32)],
        compiler_params=_params(),
    )(proj_a, proj_a, proj_a, dy, dy, conv_w)


def _conv_b_fwd(raw, conv_w):
    T = raw.shape[0]
    nblk = raw.shape[1] // CONV_LANES
    ts = min(ROW_TILE, T)
    nt = T // ts
    kt = conv_w.shape[0]

    def body(rp, rm, w_ref, y_ref):
        i = pl.program_id(1)
        ext = jnp.concatenate([jnp.where(i > 0, rp[...], 0.0), rm[...]], axis=0)
        taps = _taps(ext, kt, SUBLANES, ts)
        y_ref[...] = _silu(sum(w_ref[kt - 1 - j:kt - j, :] * taps[j] for j in range(kt)))

    return pl.pallas_call(
        body, name="conv_b_fwd", grid=(nblk, nt),
        in_specs=_halo_specs(ts, nt, CONV_LANES, nxt=False) + [pl.BlockSpec((kt, CONV_LANES), lambda cb, i: (0, cb))],
        out_specs=pl.BlockSpec((ts, CONV_LANES), lambda cb, i: (i, cb)),
        out_shape=jax.ShapeDtypeStruct(raw.shape, F32), compiler_params=_params(),
    )(raw, raw, conv_w)


def _conv_b_bwd(raw, dy, conv_w):
    T = raw.shape[0]
    nblk = raw.shape[1] // CONV_LANES
    ts = min(ROW_TILE, T)
    nt = T // ts
    kt = conv_w.shape[0]

    def body(rp, rm, rn, dym, dyn, w_ref, dr_ref, dw_ref):
        i = pl.program_id(1)
        ext = jnp.concatenate([jnp.where(i > 0, rp[...], 0.0), rm[...], rn[...]], axis=0)
        w = [w_ref[j:j + 1, :] for j in range(kt)]
        taps = _taps(ext, kt, SUBLANES, ts + SUBLANES)
        xc_ext = sum(w[kt - 1 - j] * taps[j] for j in range(kt))
        dy_ext = jnp.concatenate([dym[...], jnp.where(i < nt - 1, dyn[...], 0.0)], axis=0)
        dxc_ext = dy_ext * _dsilu(xc_ext)
        dr_ref[...] = sum(w[kt - 1 - j] * _ahead(dxc_ext, j, ts) for j in range(kt)).astype(dr_ref.dtype)
        dxc = dxc_ext[:ts]

        @pl.when(i == 0)
        def _():
            dw_ref[...] = jnp.zeros_like(dw_ref)

        for j in range(kt):
            dw_ref[j:j + 1, :] += jnp.sum(dxc * taps[kt - 1 - j][:ts], axis=0, keepdims=True)

    return pl.pallas_call(
        body, name="conv_b_bwd", grid=(nblk, nt),
        in_specs=_halo_specs(ts, nt, CONV_LANES) + _halo_specs(ts, nt, CONV_LANES, prev=False)
        + [pl.BlockSpec((kt, CONV_LANES), lambda cb, i: (0, cb))],
        out_specs=[pl.BlockSpec((ts, CONV_LANES), lambda cb, i: (i, cb)),
                   pl.BlockSpec((SUBLANES, CONV_LANES), lambda cb, i: (0, cb))],
        out_shape=[jax.ShapeDtypeStruct(raw.shape, BF16), jax.ShapeDtypeStruct((SUBLANES, nblk * CONV_LANES), F32)],
        compiler_params=_params(),
    )(raw, raw, raw, dy, dy, conv_w)


def _split_heads(ref, base, nheads, rows=slice(None)):
    return jnp.stack([ref[rows, base + h * HEAD: base + (h + 1) * HEAD] for h in range(nheads)], axis=0)


def _store_heads(ref, base, x, rows=slice(None), accumulate=False):
    for h in range(x.shape[0]):
        lanes = slice(base + h * HEAD, base + (h + 1) * HEAD)
        if accumulate:
            ref[rows, lanes] += x[h]
        else:
            ref[rows, lanes] = x[h].astype(ref.dtype)


def _gdn_fwd(qkv, proj_zs, alog, dtb, gain, H):
    T = qkv.shape[0]
    C, HW = GDN_CHUNK, H * HEAD
    nc = T // C
    zw = HW + LANES

    def body(qkv_ref, zs_ref, alog_ref, dtb_ref, gain_ref, o_ref, sall_ref, s_scr):
        @pl.when(pl.program_id(0) == 0)
        def _():
            s_scr[...] = jnp.zeros_like(s_scr)

        sall_ref[0] = s_scr[...]
        outs, states = _gdn_step(
            s_scr[...], _split_heads(qkv_ref, 0, H), _split_heads(qkv_ref, HW, H),
            _split_heads(qkv_ref, 2 * HW, H), _split_heads(zs_ref, 0, H), zs_ref[:, HW:HW + LANES],
            alog_ref[...], dtb_ref[...], gain_ref[...])
        _store_heads(o_ref, 0, outs)
        s_scr[...] = states

    row = pl.BlockSpec((1, LANES), lambda i: (0, 0))
    return pl.pallas_call(
        body, name="gdn_fwd", grid=(nc,),
        in_specs=[pl.BlockSpec((C, 3 * HW), lambda i: (i, 0)), pl.BlockSpec((C, zw), lambda i: (i, 0)), row, row, row],
        out_specs=[pl.BlockSpec((C, HW), lambda i: (i, 0)), pl.BlockSpec((1, H, HEAD, HEAD), lambda i: (i, 0, 0, 0))],
        out_shape=[jax.ShapeDtypeStruct((T, HW), BF16), jax.ShapeDtypeStruct((nc, H, HEAD, HEAD), F32)],
        scratch_shapes=[pltpu.VMEM((H, HEAD, HEAD), F32)], compiler_params=_params(),
    )(qkv, proj_zs, alog, dtb, gain)


def _gdn_bwd(qkv, proj_zs, do, s_all, alog, dtb, gain, H):
    T = qkv.shape[0]
    C, HW = GDN_CHUNK, H * HEAD
    nc = T // C
    zw = HW + LANES

    def body(qkv_ref, zs_ref, do_ref, sin_ref, alog_ref, dtb_ref, gain_ref,
             dqkv_ref, dzs_ref, dalog_ref, ddtb_ref, dgain_ref, ds_scr):
        @pl.when(pl.program_id(0) == 0)
        def _():
            ds_scr[...] = jnp.zeros_like(ds_scr)
            dalog_ref[...] = jnp.zeros_like(dalog_ref)
            ddtb_ref[...] = jnp.zeros_like(ddtb_ref)
            dgain_ref[...] = jnp.zeros_like(dgain_ref)

        primals = (sin_ref[0], _split_heads(qkv_ref, 0, H),
                   _split_heads(qkv_ref, HW, H), _split_heads(qkv_ref, 2 * HW, H), _split_heads(zs_ref, 0, H),
                   zs_ref[:, HW:HW + LANES], alog_ref[...], dtb_ref[...], gain_ref[...])
        _, vjp = jax.vjp(_gdn_step, *primals)
        dS, dq, dk, dv, dz, dsmall, dalog, ddtb, dgain = vjp((_split_heads(do_ref, 0, H), ds_scr[...]))
        ds_scr[...] = dS
        _store_heads(dqkv_ref, 0, dq)
        _store_heads(dqkv_ref, HW, dk)
        _store_heads(dqkv_ref, 2 * HW, dv)
        _store_heads(dzs_ref, 0, dz)
        dzs_ref[:, HW:HW + LANES] = dsmall.astype(dzs_ref.dtype)
        dalog_ref[...] += dalog
        ddtb_ref[...] += ddtb
        dgain_ref[...] += dgain

    row = pl.BlockSpec((1, LANES), lambda i: (0, 0))
    rev = lambda i: nc - 1 - i
    return pl.pallas_call(
        body, name="gdn_bwd", grid=(nc,),
        in_specs=[pl.BlockSpec((C, 3 * HW), lambda i: (rev(i), 0)), pl.BlockSpec((C, zw), lambda i: (rev(i), 0)),
                  pl.BlockSpec((C, HW), lambda i: (rev(i), 0)),
                  pl.BlockSpec((1, H, HEAD, HEAD), lambda i: (rev(i), 0, 0, 0)), row, row, row],
        out_specs=[pl.BlockSpec((C, 3 * HW), lambda i: (rev(i), 0)), pl.BlockSpec((C, zw), lambda i: (rev(i), 0)),
                   row, row, row],
        out_shape=[jax.ShapeDtypeStruct(qkv.shape, F32), jax.ShapeDtypeStruct(proj_zs.shape, BF16)]
        + [jax.ShapeDtypeStruct((1, LANES), F32)] * 3,
        scratch_shapes=[pltpu.VMEM((H, HEAD, HEAD), F32)], compiler_params=_params(),
    )(qkv, proj_zs, do, s_all, alog, dtb, gain)


def _hgrn_refs(proj_ref, lb_ref, HP):
    W = HP * HEAD
    return (_split_heads(proj_ref, 0, HP), _split_heads(proj_ref, W, HP), _split_heads(proj_ref, 2 * W, HP),
            _split_heads(proj_ref, 3 * W, HP), _split_heads(lb_ref, 0, HP, slice(0, 1)),
            _split_heads(lb_ref, 0, HP, slice(1, 2)))


def _hgrn_fwd(proj, lower_bounds, gain, nheads):
    T = proj.shape[0]
    C, HP = HGRN_CHUNK, HGRN_HEADS_PER_STEP
    ng, nc, W = nheads // HP, T // C, HP * HEAD

    def body(proj_ref, lb_ref, gain_ref, o_ref, sall_ref, s_scr):
        @pl.when(pl.program_id(1) == 0)
        def _():
            s_scr[...] = jnp.zeros_like(s_scr)

        sall_ref[0] = s_scr[...]
        qr, fr, vi, z, lb0, lb1 = _hgrn_refs(proj_ref, lb_ref, HP)
        outs, states = _hgrn_step(s_scr[...], qr, fr, vi, z, lb0, lb1, gain_ref[...])
        _store_heads(o_ref, 0, outs)
        s_scr[...] = states

    return pl.pallas_call(
        body, name="hgrn_fwd", grid=(ng, nc),
        in_specs=[pl.BlockSpec((C, 4 * W), lambda g, i: (i, g)), pl.BlockSpec((2, W), lambda g, i: (0, g)),
                  pl.BlockSpec((1, LANES), lambda g, i: (0, 0))],
        out_specs=[pl.BlockSpec((C, W), lambda g, i: (i, g)),
                   pl.BlockSpec((1, HP, HEAD, HEAD), lambda g, i: (i, g, 0, 0))],
        out_shape=[jax.ShapeDtypeStruct((T, nheads * HEAD), BF16), jax.ShapeDtypeStruct((nc, nheads, HEAD, HEAD), F32)],
        scratch_shapes=[pltpu.VMEM((HP, HEAD, HEAD), F32)], compiler_params=_params(),
    )(proj, lower_bounds, gain)


def _hgrn_bwd(proj, do, s_all, lower_bounds, gain, nheads):
    T = proj.shape[0]
    C, HP = HGRN_CHUNK, HGRN_HEADS_PER_STEP
    ng, nc, W = nheads // HP, T // C, HP * HEAD

    def body(proj_ref, do_ref, sin_ref, lb_ref, gain_ref, dproj_ref, dlb_ref, dgain_ref, ds_scr):
        first = pl.program_id(1) == 0

        @pl.when(first)
        def _():
            ds_scr[...] = jnp.zeros_like(ds_scr)
            dlb_ref[...] = jnp.zeros_like(dlb_ref)

        @pl.when(first & (pl.program_id(0) == 0))
        def _():
            dgain_ref[...] = jnp.zeros_like(dgain_ref)

        qr, fr, vi, z, lb0, lb1 = _hgrn_refs(proj_ref, lb_ref, HP)
        primals = (sin_ref[0], qr, fr, vi, z, lb0, lb1, gain_ref[...])
        _, vjp = jax.vjp(_hgrn_step, *primals)
        dS, dq, df, dv, dz, dlb0, dlb1, dgain = vjp((_split_heads(do_ref, 0, HP), ds_scr[...]))
        ds_scr[...] = dS
        for part, d in enumerate((dq, df, dv, dz)):
            _store_heads(dproj_ref, part * W, d)
        _store_heads(dlb_ref, 0, dlb0, slice(0, 1), accumulate=True)
        _store_heads(dlb_ref, 0, dlb1, slice(1, 2), accumulate=True)
        dgain_ref[...] += dgain

    rev = lambda i: nc - 1 - i
    return pl.pallas_call(
        body, name="hgrn_bwd", grid=(ng, nc),
        in_specs=[pl.BlockSpec((C, 4 * W), lambda g, i: (rev(i), g)), pl.BlockSpec((C, W), lambda g, i: (rev(i), g)),
                  pl.BlockSpec((1, HP, HEAD, HEAD), lambda g, i: (rev(i), g, 0, 0)),
                  pl.BlockSpec((2, W), lambda g, i: (0, g)), pl.BlockSpec((1, LANES), lambda g, i: (0, 0))],
        out_specs=[pl.BlockSpec((C, 4 * W), lambda g, i: (rev(i), g)), pl.BlockSpec((2, W), lambda g, i: (0, g)),
                   pl.BlockSpec((1, LANES), lambda g, i: (0, 0))],
        out_shape=[jax.ShapeDtypeStruct(proj.shape, BF16), jax.ShapeDtypeStruct(lower_bounds.shape, F32),
                   jax.ShapeDtypeStruct((1, LANES), F32)],
        scratch_shapes=[pltpu.VMEM((HP, HEAD, HEAD), F32)], compiler_params=_params(),
    )(proj, do, s_all, lower_bounds, gain)


def _post_specs(T):
    tr = min(POST_TILE, T)
    tile = lambda w: pl.BlockSpec((tr, w), lambda i: (i, 0))
    full = lambda r, w: pl.BlockSpec((r, w), lambda i: (0, 0))
    return tr, tile, full


def _post_fwd(s, x, p, g, b, wg, wpl, name):
    T, D = x.shape
    P = p.shape[1]
    tr, tile, full = _post_specs(T)

    def body(s_ref, x_ref, p_ref, g_ref, b_ref, wg_ref, wpl_ref, o_ref, o16_ref):
        x1 = _post_norm(s_ref[...], x_ref[...], g_ref[...], b_ref[...])
        xn = _post_gate(x1, _dot_raw(x1, wg_ref[...], "nn", False), _dot_raw(p_ref[...], wpl_ref[...], "nn", False))
        o_ref[...] = xn
        o16_ref[...] = xn.astype(BF16)

    return pl.pallas_call(
        body, name=name, grid=(T // tr,),
        in_specs=[tile(D), tile(D), tile(P), full(1, D), full(1, D), full(D, D), full(P, D)],
        out_specs=[tile(D), tile(D)],
        out_shape=[jax.ShapeDtypeStruct((T, D), F32), jax.ShapeDtypeStruct((T, D), BF16)], compiler_params=_params(),
    )(s, x, p, g, b, wg, wpl)


def _post_bwd(s, x, p, g, b, wg, wpl, dnext, name, with_loss):
    T, D = x.shape
    P = p.shape[1]
    tr, tile, full = _post_specs(T)

    def body(s_ref, x_ref, p_ref, g_ref, b_ref, wg_ref, wpl_ref, dn_ref,
             ds_ref, dx_ref, dg_ref, db_ref, dwg_ref, dwpl_ref, loss_ref):
        @pl.when(pl.program_id(0) == 0)
        def _():
            for r in (dg_ref, db_ref, dwg_ref, dwpl_ref, loss_ref):
                r[...] = jnp.zeros_like(r)

        x1, vjp_norm = jax.vjp(_post_norm, s_ref[...], x_ref[...], g_ref[...], b_ref[...])
        gate_pre = _dot_raw(x1, wg_ref[...], "nn", False)
        pp = _dot_raw(p_ref[...], wpl_ref[...], "nn", False)
        xn, vjp_gate = jax.vjp(_post_gate, x1, gate_pre, pp)
        if with_loss:
            err = xn - dn_ref[...]
            loss_ref[...] += 0.5 * jnp.sum(jnp.sum(err * err, axis=-1, keepdims=True), axis=0, keepdims=True) / D
            dn = err / D
        else:
            dn = dn_ref[...]
        dx1, dgp, dpp = vjp_gate(dn)
        dwg_ref[...] += _dot_raw(x1, dgp, "tn", False)
        dwpl_ref[...] += _dot_raw(p_ref[...], dpp, "tn", False)
        dx1 = dx1 + _dot_raw(dgp, wg_ref[...], "nt", False)
        ds, dx, dg, db = vjp_norm(dx1)
        ds_ref[...] = ds.astype(ds_ref.dtype)
        dx_ref[...] = dx
        dg_ref[...] += dg
        db_ref[...] += db

    return pl.pallas_call(
        body, name=name, grid=(T // tr,),
        in_specs=[tile(D), tile(D), tile(P), full(1, D), full(1, D), full(D, D), full(P, D), tile(D)],
        out_specs=[tile(D), tile(D), full(1, D), full(1, D), full(D, D), full(P, D), full(SUBLANES, LANES)],
        out_shape=[jax.ShapeDtypeStruct((T, D), BF16), jax.ShapeDtypeStruct((T, D), F32)]
        + [jax.ShapeDtypeStruct((1, D), F32)] * 2
        + [jax.ShapeDtypeStruct((D, D), F32), jax.ShapeDtypeStruct((P, D), F32),
           jax.ShapeDtypeStruct((SUBLANES, LANES), F32)],
        compiler_params=_params(),
    )(s, x, p, g, b, wg, wpl, dnext)


def _adam_math(w, g, m, v):
    m = ADAM_B1 * m + (1.0 - ADAM_B1) * g
    v = ADAM_B2 * v + (1.0 - ADAM_B2) * (g * g)
    m_hat = m / (1.0 - ADAM_B1 ** ADAM_STEP)
    v_hat = v / (1.0 - ADAM_B2 ** ADAM_STEP)
    return -ADAM_LR * (m_hat / (jnp.sqrt(v_hat) + ADAM_EPS) + ADAM_WD * w), m, v


def _shard_tiles(R, C):
    tr = _pick(R, (256, 128, 64, 32, 16, 8))
    return (tr, C) if tr < R or R % SUBLANES == 0 else (R, _pick(C, (256, 128)))


def _rs_add(g8, got, cidx, name):
    R, C = _owner_blocks(g8)
    tr, tc = _shard_tiles(R, C)
    nc = C // tc

    def body(c_ref, a_ref, b_ref, o_ref, o16_ref):
        total = (a_ref[0] if g8.ndim == 3 else a_ref[...]) + b_ref[0]
        o_ref[0] = total
        o16_ref[0] = total.astype(BF16)

    if g8.ndim == 3:
        mine_spec = pl.BlockSpec((1, tr, tc), lambda q, i, j, c: (2 * q + c[0], i, j))
    else:
        mine_spec = pl.BlockSpec((tr, tc), lambda q, i, j, c: (i, (2 * q + c[0]) * nc + j))
    out_spec = pl.BlockSpec((1, tr, tc), lambda q, i, j, c: (q, i, j))
    return pl.pallas_call(
        body, name=name,
        grid_spec=pltpu.PrefetchScalarGridSpec(
            num_scalar_prefetch=1, grid=(4, R // tr, nc),
            in_specs=[mine_spec, out_spec], out_specs=[out_spec, out_spec]),
        out_shape=[jax.ShapeDtypeStruct((4, R, C), F32), jax.ShapeDtypeStruct((4, R, C), BF16)],
        compiler_params=_params(),
    )(cidx, g8, got)


def _adam_sharded(w, m, v, mine, got, qidx, name):
    L, R, C = w.shape
    tr, tc = _shard_tiles(R, C)
    nr = R // tr

    def body(q_ref, w_ref, m_ref, v_ref, p_ref, r0, r1, r2, g_ref, d_ref, mo_ref, vo_ref):
        g = ((p_ref[0] + r0[0].astype(F32)) + r1[0].astype(F32)) + r2[0].astype(F32)
        d, mn, vn = _adam_math(w_ref[0], g, m_ref[0], v_ref[0])
        g_ref[0] = g
        d_ref[0] = d
        mo_ref[0] = mn
        vo_ref[0] = vn

    t3 = pl.BlockSpec((1, tr, tc), lambda l, i, j, q: (l, i, j))
    slot = lambda k: pl.BlockSpec((1, tr, tc), lambda l, i, j, q: (k, l * nr + i, j))
    return pl.pallas_call(
        body, name=name,
        grid_spec=pltpu.PrefetchScalarGridSpec(
            num_scalar_prefetch=1, grid=(L, nr, C // tc),
            in_specs=[t3, t3, t3, pl.BlockSpec((1, tr, tc), lambda l, i, j, q: (q[0], l * nr + i, j)),
                      slot(0), slot(1), slot(2)],
            out_specs=[t3, t3, t3, t3]),
        out_shape=[jax.ShapeDtypeStruct((L, R, C), F32)] * 4, compiler_params=_params(),
    )(qidx, w, m, v, mine, got, got, got)


def _adam_replicated(w, m, v, g8):
    def body(w_ref, m_ref, v_ref, g_ref, go_ref, d_ref, mo_ref, vo_ref):
        g = g_ref[0]
        for k in range(1, 8):
            g = g + g_ref[k]
        d, mn, vn = _adam_math(w_ref[...], g, m_ref[...], v_ref[...])
        go_ref[...] = g
        d_ref[...] = d
        mo_ref[...] = mn
        vo_ref[...] = vn

    return pl.pallas_call(
        body, name="adam_replicated", out_shape=[jax.ShapeDtypeStruct(w.shape, F32)] * 4, compiler_params=_params(),
    )(w, m, v, g8)


def _place():
    return lax.axis_index("x"), lax.axis_index("y"), lax.axis_index("c")


def _all_gather(shard, name):
    def body(x_ref, out_ref, send_sems, recv_sems, local_sem):
        x, y, c = _place()
        me, sibling = (x, y, c), (x, y, 1 - c)
        chips = [(1 - x, y), (x, 1 - y), (1 - x, 1 - y)]

        def slab(px, py, pc):
            return out_ref.at[4 * px + 2 * py + pc]

        def copy(k, block, to, src=None):
            return pltpu.make_async_remote_copy(
                src_ref=slab(*block) if src is None else src, dst_ref=slab(*block),
                send_sem=send_sems.at[k], recv_sem=recv_sems.at[k], device_id=to, device_id_type=MESH)

        mine = pltpu.make_async_copy(x_ref, slab(*me), local_sem)
        mine.start()
        first = [copy(0, me, sibling, src=x_ref)]
        first += [copy(1 + j, me, (*chip, c), src=x_ref) for j, chip in enumerate(chips)]
        for cp in first:
            cp.start()
        passed = [copy(4 + j, (*chip, c), sibling) for j, chip in enumerate(chips)]
        for j, chip in enumerate(chips):
            copy(1 + j, (*chip, c), me).wait_recv()
            passed[j].start()
        copy(0, sibling, me).wait_recv()
        for j, chip in enumerate(chips):
            copy(4 + j, (*chip, 1 - c), me).wait_recv()
        for cp in first + passed:
            cp.wait_send()
        mine.wait()

    return pl.pallas_call(
        body, name=name, out_shape=jax.ShapeDtypeStruct((8,) + shard.shape, shard.dtype),
        in_specs=[pl.BlockSpec(memory_space=pl.ANY)], out_specs=pl.BlockSpec(memory_space=pl.ANY),
        scratch_shapes=[pltpu.SemaphoreType.DMA((7,)), pltpu.SemaphoreType.DMA((7,)), pltpu.SemaphoreType.DMA],
    )(shard)


def _owner_blocks(g8):
    return g8.shape[1:] if g8.ndim == 3 else (g8.shape[0], g8.shape[1] // 8)


def _rs_to_sibling(g8, name):
    R, C = _owner_blocks(g8)

    def body(g_ref, out_ref, send_sems, recv_sems):
        x, y, c = _place()
        block = (lambda k: g_ref.at[k]) if g8.ndim == 3 else (lambda k: g_ref.at[:, pl.ds(k * C, C)])
        copies = [pltpu.make_async_remote_copy(
            src_ref=block(2 * q + (1 - c)), dst_ref=out_ref.at[q], send_sem=send_sems.at[q],
            recv_sem=recv_sems.at[q], device_id=(x, y, 1 - c), device_id_type=MESH) for q in range(4)]
        for cp in copies:
            cp.start()
        for cp in copies:
            cp.wait()

    return pl.pallas_call(
        body, name=name, out_shape=jax.ShapeDtypeStruct((4, R, C), g8.dtype),
        in_specs=[pl.BlockSpec(memory_space=pl.ANY)], out_specs=pl.BlockSpec(memory_space=pl.ANY),
        scratch_shapes=[pltpu.SemaphoreType.DMA((4,)), pltpu.SemaphoreType.DMA((4,))],
    )(g8)


_HBM = pl.BlockSpec(memory_space=pltpu.HBM)
_SEM = pl.BlockSpec(memory_space=pltpu.SEMAPHORE)
_DATAFLOW = pltpu.SideEffectType.DATAFLOW_SIDE_EFFECTING
TOKEN_SHAPE = (SUBLANES, LANES)


def _chip_plan(x, y, c):
    return [(2 * px + py, j, (px, py, c)) for j, (px, py) in enumerate([(1 - x, y), (x, 1 - y), (1 - x, 1 - y)])]


def _exchange_copies(plan, src_ref, land_ref, send_sems, recv_sems):
    return [pltpu.make_async_remote_copy(
        src_ref=src_ref.at[blk], dst_ref=land_ref.at[slot], send_sem=send_sems.at[k], recv_sem=recv_sems.at[k],
        device_id=peer, device_id_type=MESH) for k, (blk, slot, peer) in enumerate(plan(*_place()))]


def _exchange_start(src, n_slots, plan, name):
    land_shape = (n_slots,) + src.shape[1:]
    n = len(plan(0, 0, 0))

    def body(src_ref, land_ref, send_sems, recv_sems, src_thru, land_thru, token):
        for cp in _exchange_copies(plan, src_ref, land_ref, send_sems, recv_sems):
            cp.start()
        token[...] = jnp.zeros_like(token)

    return pl.pallas_call(
        body, name=name,
        out_shape=(pltpu.SemaphoreType.DMA((n,)), pltpu.SemaphoreType.DMA((n,)), pltpu.HBM(src.shape, src.dtype),
                   pltpu.HBM(land_shape, src.dtype), jax.ShapeDtypeStruct(TOKEN_SHAPE, F32)),
        in_specs=(_HBM, _HBM), out_specs=(_SEM, _SEM, _HBM, _HBM, pl.BlockSpec(memory_space=pltpu.VMEM)),
        input_output_aliases={0: 2, 1: 3}, compiler_params=pltpu.CompilerParams(has_side_effects=_DATAFLOW),
    )(pltpu.with_memory_space_constraint(src, pltpu.HBM),
      pltpu.with_memory_space_constraint(lax.empty(land_shape, src.dtype), pltpu.HBM))


def _exchange_wait(handle, plan, after, name):
    send_sems, recv_sems, src_thru, land_thru, _ = handle

    def body(src_ref, land_ref, send_sems, recv_sems, after_ref, src_dead, got_ref):
        for cp in _exchange_copies(plan, src_ref, land_ref, send_sems, recv_sems):
            cp.wait_send()
            cp.wait_recv()

    return pl.pallas_call(
        body, name=name,
        out_shape=(pltpu.HBM(src_thru.shape, src_thru.dtype), pltpu.HBM(land_thru.shape, land_thru.dtype)),
        in_specs=(_HBM, _HBM, _SEM, _SEM, pl.BlockSpec(memory_space=pl.ANY)), out_specs=(_HBM, _HBM),
        input_output_aliases={0: 0, 1: 1}, compiler_params=pltpu.CompilerParams(has_side_effects=_DATAFLOW),
    )(src_thru, land_thru, send_sems, recv_sems, after)[1]


def _gather_plan(x, y, c):
    me = 4 * x + 2 * y + c
    flip = lambda v, bit: 1 - v if bit else v
    return [(0, me, (flip(x, r >> 2 & 1), flip(y, r >> 1 & 1), flip(c, r & 1))) for r in range(1, 8)]


class _LateGather:
    def __init__(self, shard, name):
        self.shard, self.name = shard, name
        self.handle = _exchange_start(shard[None], 8, _gather_plan, name + "_start")

    def get(self, after):
        land = _exchange_wait(self.handle, _gather_plan, after, self.name + "_wait")
        x, y, c = _place()
        return lax.dynamic_update_slice(land, self.shard[None], (4 * x + 2 * y + c, 0, 0))


class _GradExchange:
    def __init__(self, cidx, qidx, layouts):
        self.cidx, self.qidx, self.layouts, self.pending = cidx, qidx, layouts, {}

    def start(self, tag, grad):
        g8 = self.layouts[tag](grad)
        got = _rs_to_sibling(g8, "rs_sibling_" + tag)
        chip_sums, chip_sums16 = _rs_add(g8, got, self.cidx, "rs_add_" + tag)
        handle = _exchange_start(chip_sums16, 3, _chip_plan, "rs_chips_start_" + tag)
        self.pending[tag] = (chip_sums, handle)
        return handle[4]

    def finish(self, tag, w, m, v, after):
        chip_sums, handle = self.pending.pop(tag)
        got2 = _exchange_wait(handle, _chip_plan, after, "rs_chips_wait_" + tag)
        return _adam_sharded(w, m, v, chip_sums, got2, self.qidx, "adam_" + tag)


def _local_grads(x, p0, p1, target, wt_zs, wt_a, wt_qkv, late, conv_a, conv_b,
                 a_log, dt_bias, gdn_gain, lower_bounds, hgrn_gain, ln_g, ln_b, on_grad=None):
    H = a_log.shape[1]
    pad_small = ((0, 0), (H, LANES - 2 * H))
    alog_row = jnp.pad(a_log, pad_small)
    dtb_row = jnp.pad(dt_bias, pad_small)

    x16 = x.astype(BF16)
    proj_zs = _matmul(x16, wt_zs, "nt", "proj_even_zs", after=late.started)
    proj_a = _matmul(x16, wt_a, "nt", "proj_even_a", after=late.started)
    proj_qkv = _matmul(x16, wt_qkv, "nt", "proj_even_qkv", after=late.started)
    y_a = _mixer_a_fwd(proj_a, conv_a)
    qkv = _conv_b_fwd(proj_qkv, conv_b)
    o2, s_gdn = _gdn_fwd(qkv, proj_zs, alog_row, dtb_row, gdn_gain, H)
    woute_a, woute_b = late.out_even(o2)
    wg, wpl = late.gate(o2)
    s_e = _matmul(o2, woute_b, "nn", "out_even_b", add=_matmul(y_a, woute_a, "nn", "out_even_a"))
    x2, x2_16 = _post_fwd(s_e, x, p0, ln_g[0:1], ln_b[0:1], wg[0], wpl[0], "post_even_fwd")
    wino, wouto = late.odd(s_e)
    nheads_o = wouto.shape[0] // HEAD
    proj_o = _matmul(x2_16, wino, "nn", "proj_odd")
    o4, s_hgrn = _hgrn_fwd(proj_o, lower_bounds, hgrn_gain, nheads_o)
    s_o = _matmul(o4, wouto, "nn", "out_odd")
    ds_o, dx2, dlng1, dlnb1, dwg1, dwpl1, loss = _post_bwd(
        s_o, x2, p1, ln_g[1:2], ln_b[1:2], wg[1], wpl[1], target, "post_odd_loss_bwd", True)
    do4 = _matmul(ds_o, wouto, "nt", "d_out_odd_act")
    grads = {}

    def emit(tag, grad):
        grads[tag] = grad
        return on_grad(tag, grad) if on_grad is not None else jnp.zeros(TOKEN_SHAPE, F32)

    tok = emit("w_out_odd", _matmul(o4, ds_o, "tn", "d_out_odd_w"))
    dproj_o, dlb, dhgain = _hgrn_bwd(proj_o, do4, s_hgrn, lower_bounds, hgrn_gain + tok[0:1], nheads_o)
    dx2 = _matmul(dproj_o, wino, "nt", "d_proj_odd_act", add=dx2)
    tok = emit("w_in_odd", _matmul(x2_16, dproj_o, "tn", "d_proj_odd_w"))
    ds_e, dx, dlng0, dlnb0, dwg0, dwpl0, _ = _post_bwd(
        s_e, x, p0, ln_g[0:1], ln_b[0:1] + tok[0:1, 0:1], wg[0], wpl[0], dx2, "post_even_bwd", False)
    tok = emit("w_pl_gate", jnp.stack([dwg0, dwg1])) + emit("w_pl", jnp.stack([dwpl0, dwpl1]))
    dy_a = _matmul(ds_e, woute_a, "nt", "d_out_even_a_act")
    do2 = _matmul(ds_e, woute_b, "nt", "d_out_even_b_act")
    dwoute_a = _matmul(y_a, ds_e, "tn", "d_out_even_a_w")
    dwoute_b = _matmul(o2, ds_e, "tn", "d_out_even_b_w")
    tok = tok + emit("w_out_even", jnp.concatenate([dwoute_a, dwoute_b], axis=0))
    dqkv, dproj_zs, dalog, ddtb, dggain = _gdn_bwd(qkv, proj_zs, do2, s_gdn, alog_row, dtb_row, gdn_gain + tok[0:1], H)
    dproj_qkv, dconv_b = _conv_b_bwd(proj_qkv, dqkv, conv_b)
    dproj_a, dconv_a = _mixer_a_bwd(proj_a, dy_a, conv_a)
    emit("conv", (dconv_a[:conv_a.shape[0]], dconv_b[:conv_b.shape[0]]))
    tok = emit("w_in_even", (_matmul(dproj_zs, x16, "tn", "d_proj_even_zs_w"), _matmul(dproj_a, x16, "tn", "d_proj_even_a_w"),
                             _matmul(dproj_qkv, x16, "tn", "d_proj_even_qkv_w")))
    dx = _matmul(dproj_zs, wt_zs, "nn", "d_proj_even_zs_act", add=dx, after=tok)
    dx = _matmul(dproj_a, wt_a, "nn", "d_proj_even_a_act", add=dx)
    dx = _matmul(dproj_qkv, wt_qkv, "nn", "d_proj_even_qkv_act", add=dx)
    grads.update(
        loss=loss[0, 0], grad_x=dx, a_log=dalog[:, H:2 * H], dt_bias=ddtb[:, H:2 * H], gdn_gain=dggain,
        lower_bounds=dlb, hgrn_gain=dhgain, ln_g=jnp.concatenate([dlng0, dlng1], axis=0),
        ln_b=jnp.concatenate([dlnb0, dlnb1], axis=0))
    return grads


def _pad_rows(a, rows):
    return jnp.pad(a, ((0, rows - a.shape[0]), (0, 0)))


def _pack_small(a_log, dt_bias, gdn_gain, lower_bounds, hgrn_gain, ln_g, ln_b):
    lane_pad = lambda a: _pad_rows(jnp.pad(a, ((0, 0), (0, LANES - a.shape[1]))), SUBLANES)
    parts = [lane_pad(a_log), lane_pad(dt_bias), lane_pad(gdn_gain), lower_bounds.reshape(-1, LANES),
             lane_pad(hgrn_gain), ln_g.reshape(-1, LANES), ln_b.reshape(-1, LANES)]
    packed = jnp.concatenate(parts, axis=0)
    assert packed.shape[0] == SMALL_ROWS, packed.shape
    return packed


def _unpack_small(packed, shapes):
    out, r = [], 0
    for shp in shapes:
        n = shp[0] * shp[1]
        if n < LANES * SUBLANES and shp[1] <= LANES:
            out.append(packed[r:r + shp[0], :shp[1]])
            r += SUBLANES
        else:
            rows = n // LANES
            out.append(packed[r:r + rows].reshape(shp))
            r += rows
    return out


def _split_in_even(wt_full, AW, HW, H):
    D = wt_full.shape[1]
    n_a = 4 * AW
    n_main = n_a + 3 * HW
    wt_zs = jnp.concatenate([wt_full[n_main:n_main + HW], wt_full[n_main + HW:],
                             jnp.zeros((LANES - 2 * H, D), wt_full.dtype)], axis=0)
    wt_a = wt_full[:n_a].reshape(4, AW // MIXER_LANES, MIXER_LANES, D).transpose(1, 0, 2, 3).reshape(n_a, D)
    return wt_zs, wt_a, wt_full[n_a:n_main]


def _join_in_even(dt_zs, dt_a, dt_qkv, AW, HW, H):
    D = dt_a.shape[1]
    a_nat = dt_a.reshape(AW // MIXER_LANES, 4, MIXER_LANES, D).transpose(1, 0, 2, 3).reshape(4 * AW, D)
    return jnp.concatenate([a_nat, dt_qkv, dt_zs[:HW], dt_zs[HW:HW + 2 * H]], axis=0)


def kernel(x, p, w_in_even, conv_a_w, conv_b_w, a_log, dt_bias, gdn_norm_g, w_out_even, w_in_odd, lower_bounds, hgrn_norm_g, w_out_odd, ln_g, ln_b, w_pl, w_pl_gate, loss_target, m_w_in_even, m_conv_a_w, m_conv_b_w, m_a_log, m_dt_bias, m_gdn_norm_g, m_w_out_even, m_w_in_odd, m_lower_bounds, m_hgrn_norm_g, m_w_out_odd, m_ln_g, m_ln_b, m_w_pl, m_w_pl_gate, v_w_in_even, v_conv_a_w, v_conv_b_w, v_a_log, v_dt_bias, v_gdn_norm_g, v_w_out_even, v_w_in_odd, v_lower_bounds, v_hgrn_norm_g, v_w_out_odd, v_ln_g, v_ln_b, v_w_pl, v_w_pl_gate):
    xi, yi, ci = _place()
    cidx = jnp.reshape(ci, (1,)).astype(jnp.int32)
    qidx = jnp.reshape(2 * xi + yi, (1,)).astype(jnp.int32)
    D = x.shape[2]
    H = a_log.shape[1]
    HW = H * HEAD
    AW = conv_a_w.shape[2] * 8
    OW = w_out_odd.shape[1] * 8
    PD = w_pl.shape[1]
    ka, kb = conv_a_w.shape[1], conv_b_w.shape[1]
    ca, cb = conv_a_w.shape[2], conv_b_w.shape[2]
    gw = HGRN_HEADS_PER_STEP * HEAD
    ngrp = OW // gw

    transposed = lambda a: jnp.transpose(a, (0, 2, 1))
    g_ine = _all_gather(transposed(w_in_even)[0].astype(BF16), "ag_w_in_even")
    wt_zs, wt_a, wt_qkv = _split_in_even(g_ine.reshape(-1, D), AW, HW, H)
    behind = lambda shard, dep: lax.optimization_barrier((shard, dep))[0]
    late_oute = _LateGather(behind(w_out_even[0].astype(BF16), g_ine), "ag_w_out_even")
    late_gate = _LateGather(behind(w_pl_gate.astype(BF16).reshape(-1, D), late_oute.handle[4]), "ag_w_pl_gate")
    late_pl = _LateGather(behind(w_pl.astype(BF16).reshape(DEPTH * PD, -1), late_gate.handle[4]), "ag_w_pl")
    late_ino = _LateGather(behind(w_in_odd[0].astype(BF16), late_pl.handle[4]), "ag_w_in_odd")
    late_outo = _LateGather(behind(w_out_odd[0].astype(BF16), late_ino.handle[4]), "ag_w_out_odd")

    class _Late:
        started = sum(g.handle[4] for g in (late_oute, late_gate, late_pl, late_ino, late_outo))

        @staticmethod
        def out_even(after):
            woute = late_oute.get(after).reshape(-1, D)
            return woute[:AW], woute[AW:]

        @staticmethod
        def gate(after):
            g_gate, g_pl = late_gate.get(after), late_pl.get(after)
            return (g_gate.reshape(8, DEPTH, D // 8, D).transpose(1, 0, 2, 3).reshape(DEPTH, D, D),
                    g_pl.reshape(8, DEPTH, PD, D // 8).transpose(1, 2, 0, 3).reshape(DEPTH, PD, D))

        @staticmethod
        def odd(after):
            g_ino = late_ino.get(after)
            wino = jnp.transpose(g_ino, (1, 0, 2)).reshape(D, 4, ngrp, gw).transpose(0, 2, 1, 3).reshape(D, 4 * OW)
            return wino, late_outo.get(after).reshape(-1, D)

    taps = jnp.concatenate([_pad_rows(conv_a_w[0], SUBLANES), _pad_rows(conv_b_w[0], SUBLANES)], axis=1)
    g_taps = _all_gather(taps, "ag_conv")
    conv_a = jnp.transpose(g_taps[:, :ka, :ca], (1, 0, 2)).reshape(ka, 8 * ca)
    conv_b = jnp.transpose(g_taps[:, :kb, ca:], (1, 0, 2)).reshape(kb, 8 * cb)

    sh = w_in_even.shape[2]
    tap_blocks = lambda g, width: _pad_rows(g, SUBLANES).reshape(SUBLANES, 8, width).transpose(1, 0, 2)
    owner_layout = {
        "w_in_even": lambda g: _join_in_even(*g, AW, HW, H).reshape(8, sh, D),
        "w_in_odd": lambda g: g.reshape(D, ngrp, 4, gw).transpose(0, 2, 1, 3).reshape(D, 4 * OW),
        "w_out_even": lambda g: g.reshape(8, -1, D),
        "w_out_odd": lambda g: g.reshape(8, -1, D),
        "w_pl_gate": lambda g: g.reshape(DEPTH, 8, D // 8, D).transpose(1, 0, 2, 3).reshape(8, DEPTH * D // 8, D),
        "w_pl": lambda g: g.reshape(DEPTH, PD, 8, D // 8).transpose(2, 0, 1, 3).reshape(8, DEPTH * PD, D // 8),
        "conv": lambda g: jnp.concatenate([tap_blocks(g[0], ca), tap_blocks(g[1], cb)], axis=2),
    }
    exchange = _GradExchange(cidx, qidx, owner_layout)
    gr = _local_grads(x[0], p[0, 0], p[1, 0], loss_target[0], wt_zs, wt_a, wt_qkv, _Late, conv_a, conv_b,
                      a_log, dt_bias, gdn_norm_g, lower_bounds, hgrn_norm_g, ln_g, ln_b, on_grad=exchange.start)

    last = gr["grad_x"]
    pack_taps = lambda a, b: jnp.concatenate([_pad_rows(a[0], SUBLANES), _pad_rows(b[0], SUBLANES)], axis=1)[None]
    o_outo = exchange.finish("w_out_odd", w_out_odd, m_w_out_odd, v_w_out_odd, last)
    o_ino = exchange.finish("w_in_odd", w_in_odd, m_w_in_odd, v_w_in_odd, last)
    o_gate = exchange.finish("w_pl_gate", w_pl_gate, m_w_pl_gate, v_w_pl_gate, last)
    o_pl = exchange.finish("w_pl", w_pl, m_w_pl, v_w_pl, last)
    o_oute = exchange.finish("w_out_even", w_out_even, m_w_out_even, v_w_out_even, last)
    o_taps = exchange.finish("conv", taps[None], pack_taps(m_conv_a_w, m_conv_b_w), pack_taps(v_conv_a_w, v_conv_b_w), last)
    others_done = sum(o[1][0, 0:1, 0:1] for o in (o_outo, o_ino, o_gate, o_pl, o_oute, o_taps))
    o_ine = [transposed(o) for o in exchange.finish(
        "w_in_even", transposed(w_in_even), transposed(m_w_in_even), transposed(v_w_in_even), others_done)]

    small_g = _pack_small(gr["a_log"], gr["dt_bias"], gr["gdn_gain"], gr["lower_bounds"], gr["hgrn_gain"],
                          gr["ln_g"], gr["ln_b"])
    small_g = small_g.at[0, LANES - 1].set(gr["loss"])
    o_small = _adam_replicated(
        _pack_small(a_log, dt_bias, gdn_norm_g, lower_bounds, hgrn_norm_g, ln_g, ln_b),
        _pack_small(m_a_log, m_dt_bias, m_gdn_norm_g, m_lower_bounds, m_hgrn_norm_g, m_ln_g, m_ln_b),
        _pack_small(v_a_log, v_dt_bias, v_gdn_norm_g, v_lower_bounds, v_hgrn_norm_g, v_ln_g, v_ln_b),
        _all_gather(small_g, "ag_small_grads"))
    small_shapes = [a_log.shape, dt_bias.shape, gdn_norm_g.shape, lower_bounds.shape, hgrn_norm_g.shape,
                    ln_g.shape, ln_b.shape]

    def leaves(kind):
        s_alog, s_dt, s_gg, s_lb, s_hg, s_lng, s_lnb = _unpack_small(o_small[kind], small_shapes)
        t = o_taps[kind]
        return [o_ine[kind], t[:, :ka, :ca], t[:, :kb, ca:], s_alog, s_dt, s_gg, o_oute[kind],
                o_ino[kind], s_lb, s_hg, o_outo[kind], s_lng, s_lnb, o_pl[kind], o_gate[kind]]

    return (o_small[0][0, LANES - 1], gr["grad_x"][None], *leaves(0), *leaves(1), *leaves(2), *leaves(3))
```

```python
import functools

import jax
import jax.numpy as jnp
from jax import lax
from jax.experimental import pallas as pl
from jax.experimental.pallas import tpu as pltpu

F32 = jnp.float32
BF16 = jnp.bfloat16
MESH = pl.DeviceIdType.MESH
AXES = ("x", "y", "c")

LANES = 128
SUBLANES = 8
HEAD = 128
GDN_CHUNK = 64
HGRN_CHUNK = 64
HGRN_SUB = 16
HGRN_HEADS_PER_STEP = 16
NORM_EPS = 1e-5
DEPTH = 2
ALPHA = (2.0 * DEPTH) ** 0.25
EXP_CLAMP = 80.0
ADAM_LR, ADAM_B1, ADAM_B2, ADAM_EPS, ADAM_WD, ADAM_STEP = 0.001, 0.9, 0.999, 1e-08, 0.01, 10
VMEM_LIMIT = 56 * 1024 * 1024
MATMUL_VMEM = 36 * 1024 * 1024
ROW_TILE = 512
MIXER_LANES = 256
CONV_LANES = 512
POST_TILE = 256
SMALL_ROWS = 96

_NOBATCH, _BATCH0 = ((), ()), ((0,), (0,))
_DIMS = {"nn": (((1,), (0,)), _NOBATCH), "nt": (((1,), (1,)), _NOBATCH), "tn": (((0,), (0,)), _NOBATCH),
         "bnn": (((2,), (1,)), _BATCH0), "bnt": (((2,), (2,)), _BATCH0), "btn": (((1,), (1,)), _BATCH0)}


def _params(**kw):
    return pltpu.CompilerParams(vmem_limit_bytes=VMEM_LIMIT, **kw)


def _dot_raw(a, b, kind, hi):
    if hi:
        return lax.dot_general(a, b, _DIMS[kind], precision=lax.Precision.HIGHEST, preferred_element_type=F32)
    return lax.dot_general(a.astype(BF16), b.astype(BF16), _DIMS[kind], preferred_element_type=F32)


@functools.partial(jax.custom_vjp, nondiff_argnums=(2, 3))
def mdot(a, b, kind, hi):
    return _dot_raw(a, b, kind, hi)


def _mdot_fwd(a, b, kind, hi):
    return _dot_raw(a, b, kind, hi), (a, b)


def _mdot_bwd(kind, hi, res, g):
    a, b = res
    pre, base = kind[:-2], kind[-2:]
    if base == "nn":
        return _dot_raw(g, b, pre + "nt", hi), _dot_raw(a, g, pre + "tn", hi)
    if base == "nt":
        return _dot_raw(g, b, pre + "nn", hi), _dot_raw(g, a, pre + "tn", hi)
    return _dot_raw(b, g, pre + "nt", hi), _dot_raw(a, g, pre + "nn", hi)


mdot.defvjp(_mdot_fwd, _mdot_bwd)


def _rows(x, lo, hi):
    return _take_rows(x, lo, hi, x.shape[-2])


@functools.partial(jax.custom_vjp, nondiff_argnums=(1, 2, 3))
def _take_rows(x, lo, hi, n):
    return x[..., lo:hi, :]


def _take_rows_fwd(x, lo, hi, n):
    return x[..., lo:hi, :], None


def _take_rows_bwd(lo, hi, n, _, g):
    parts = []
    if lo > 0:
        parts.append(jnp.zeros(g.shape[:-2] + (lo, g.shape[-1]), g.dtype))
    parts.append(g)
    if n - hi > 0:
        parts.append(jnp.zeros(g.shape[:-2] + (n - hi, g.shape[-1]), g.dtype))
    return (jnp.concatenate(parts, axis=-2) if len(parts) > 1 else g,)


_take_rows.defvjp(_take_rows_fwd, _take_rows_bwd)


def _heads_of(wide, nheads):
    return jnp.stack([wide[:, h * HEAD:(h + 1) * HEAD] for h in range(nheads)], axis=0)


def _wide_of(x):
    return jnp.concatenate([x[h] for h in range(x.shape[0])], axis=1)


@functools.partial(jax.custom_vjp, nondiff_argnums=(1,))
def to_heads(wide, nheads):
    return _heads_of(wide, nheads)


to_heads.defvjp(lambda wide, nheads: (_heads_of(wide, nheads), None), lambda nheads, _, g: (_wide_of(g),))


@jax.custom_vjp
def to_wide(x):
    return _wide_of(x)


to_wide.defvjp(lambda x: (_wide_of(x), None), lambda _, g: (_heads_of(g, g.shape[1] // HEAD),))


def _sigmoid(x):
    return jax.nn.sigmoid(x)


def _silu(x):
    return x * _sigmoid(x)


def _dsilu(x):
    s = _sigmoid(x)
    return s * (1.0 + x * (1.0 - s))


def _log1p(u):
    return jnp.where(u < 1e-4, u * (1.0 - 0.5 * u), jnp.log(1.0 + u))


def _softplus(x):
    return jnp.maximum(x, 0.0) + _log1p(jnp.exp(-jnp.abs(x)))


def _rms_gate(o, gain, z):
    return o * lax.rsqrt(jnp.mean(o * o, axis=-1, keepdims=True) + NORM_EPS) * gain * _silu(z)


def _l2n(x):
    return x * lax.rsqrt(jnp.sum(x * x, axis=-1, keepdims=True) + 1e-6)


def _split_dot_raw(m, x, kind):
    mb = m.astype(BF16)
    hi = x.astype(BF16)
    lo = (x - hi.astype(F32)).astype(BF16)
    dims = _DIMS[kind]
    return (lax.dot_general(mb, hi, dims, preferred_element_type=F32)
            + lax.dot_general(mb, lo, dims, preferred_element_type=F32))


@jax.custom_vjp
def mask_dot(m, x):
    return _split_dot_raw(m, x, "nn")


def _mask_dot_fwd(m, x):
    return _split_dot_raw(m, x, "nn"), m


def _mask_dot_bwd(m, g):
    return jnp.zeros_like(m), _split_dot_raw(m, g, "tn")


mask_dot.defvjp(_mask_dot_fwd, _mask_dot_bwd)


def _neumann_rest(low):
    n = low.shape[-1]
    rest = -low
    power = low
    span = 2
    while span < n:
        power = _dot_raw(power, power, "bnn", False)
        rest = rest + power + _dot_raw(rest, power, "bnn", False)
        span *= 2
    return rest


@jax.custom_vjp
def _unit_lower_inverse_minus_eye(low):
    return _neumann_rest(low)


def _inverse_fwd(low):
    rest = _neumann_rest(low)
    return rest, rest


def _inverse_bwd(rest, g):
    left = g + _dot_raw(rest, g, "btn", False)
    return (-(left + _dot_raw(left, rest, "bnt", False)),)


_unit_lower_inverse_minus_eye.defvjp(_inverse_fwd, _inverse_bwd)


def _gdn_step(S, q, k, v, z, small, alog, dtb, gain):
    H = S.shape[0]
    C = GDN_CHUNK
    row = lax.broadcasted_iota(jnp.int32, (C, C), 0)
    col = lax.broadcasted_iota(jnp.int32, (C, C), 1)
    tril, strict, eye = (row >= col)[None], (row > col)[None], (row == col)[None]
    head = lax.broadcasted_iota(jnp.int32, (H, 1, LANES), 0)
    lane = lax.broadcasted_iota(jnp.int32, (H, 1, LANES), 2)
    rowc = lax.broadcasted_iota(jnp.int32, (1, C, 1), 1)
    beta_all = _sigmoid(small)
    g_all = -jnp.exp(alog) * _softplus(small + dtb)
    gc_all = mask_dot((row >= col).astype(F32), g_all)
    beta = jnp.sum(jnp.where(lane == head, beta_all[None], 0.0), axis=-1, keepdims=True)
    gc = jnp.sum(jnp.where(lane == head + H, gc_all[None], 0.0), axis=-1, keepdims=True)
    gc_row = jnp.sum(jnp.where(eye, gc, 0.0), axis=1, keepdims=True)
    decay = jnp.where(tril, jnp.exp(jnp.where(tril, gc - gc_row, 0.0)), 0.0)
    g_last = jnp.sum(jnp.where(rowc == C - 1, gc, 0.0), axis=1, keepdims=True)
    qn = _l2n(q) * (HEAD ** -0.5)
    kn = _l2n(k)
    kb = kn * beta
    low = jnp.where(strict, beta * mdot(kn, kn, "bnt", False) * decay, 0.0)
    inv_rest = _unit_lower_inverse_minus_eye(low)
    eg = jnp.exp(gc)
    vb, kbe = v * beta, kb * eg
    u = vb + mdot(inv_rest, vb, "bnn", False)
    w = kbe + mdot(inv_rest, kbe, "bnn", False)
    attn = mdot(qn, kn, "bnt", False) * decay
    v_new = u - mdot(w, S, "bnn", False)
    o = mdot(qn * eg, S, "bnn", False) + mdot(attn, v_new, "bnn", False)
    k_dec = kn * jnp.exp(g_last - gc)
    return _rms_gate(o, gain, z), S * jnp.exp(g_last) + mdot(k_dec, v_new, "btn", False)


def _hgrn_step(St, qr, fr, vi, z, lb0, lb1, gain):
    H = St.shape[0]
    C, SB = HGRN_CHUNK, HGRN_SUB
    row = lax.broadcasted_iota(jnp.int32, (C, C), 0)
    col = lax.broadcasted_iota(jnp.int32, (C, C), 1)
    blk_start = row - (row & (SB - 1))
    in_blk_f = ((row >= col) & (col >= blk_start)).astype(F32)
    before_f = (col < blk_start).astype(F32)
    sums_f = jnp.concatenate([in_blk_f, before_f], axis=0)
    m = jnp.maximum(lb0, lb1)
    e0, e1 = jnp.exp(lb0 - m), jnp.exp(lb1 - m)
    lb = e1 / (e0 + e1)
    f = lb + (1.0 - lb) * _sigmoid(fr)
    q = _silu(qr)
    k = 1.0 - f
    logf = jnp.log(f)
    sums = mask_dot(sums_f, to_wide(logf))
    inner, start = to_heads(_rows(sums, 0, C), H), to_heads(_rows(sums, C, 2 * C), H)
    b = start + inner
    b_last = jnp.sum(logf, axis=1, keepdims=True)
    o = mdot(q * jnp.exp(b), St, "bnt", False)
    qt = q * jnp.exp(inner)
    parts = []
    for blk in range(C // SB):
        lo, n = blk * SB, (blk + 1) * SB
        ref = jnp.concatenate([_rows(start, lo, n)] * (blk + 1), axis=1)
        kt = _rows(k, 0, n) * jnp.exp(jnp.minimum(ref - _rows(b, 0, n), EXP_CLAMP))
        att = mdot(_rows(qt, lo, n), kt, "bnt", False)
        t_idx = lax.broadcasted_iota(jnp.int32, (1, SB, n), 1) + lo
        s_idx = lax.broadcasted_iota(jnp.int32, (1, SB, n), 2)
        att = jnp.where(s_idx <= t_idx, att, 0.0)
        parts.append(mdot(att, _rows(vi, 0, n), "bnn", False))
    o = o + jnp.concatenate(parts, axis=1)
    k_dec = k * jnp.exp(b_last - b)
    return _rms_gate(o, gain, z), St * jnp.exp(b_last) + mdot(vi, k_dec, "btn", False)


def _post_norm(s, x, g, b):
    r = ALPHA * x + s
    d = r - jnp.mean(r, axis=-1, keepdims=True)
    var = jnp.mean(d * d, axis=-1, keepdims=True)
    return d * lax.rsqrt(var + NORM_EPS) * g + b


def _post_gate(x1, gate_pre, pp):
    return x1 + pp * _sigmoid(gate_pre)


def _pick(dim, cands):
    for c in cands:
        if dim % c == 0:
            return c
    return dim


def _matmul_tiles(M, K, tn, a_bytes, b_bytes, has_add):
    for tk in (4096, 2048, 1536, 1152, 1024, 640, 512, 384, 256, 128):
        if K % tk:
            continue
        for tm in (2048, 1152, 1024, 512, 384, 256, 128):
            if M % tm:
                continue
            blocks = tm * tk * a_bytes + tk * tn * b_bytes + tm * tn * 4 * (2 if has_add else 1)
            if 2 * blocks + (tm * tn * 4 if tk < K else 0) <= MATMUL_VMEM and tm >= min(M, 1024):
                return tm, tk
    return _pick(M, (512, 256, 128)), _pick(K, (512, 256, 128))


def _matmul(a, b, kind, name, add=None, after=None):
    if kind == "nn":
        (M, K), N = a.shape, b.shape[1]
    elif kind == "nt":
        (M, K), N = a.shape, b.shape[0]
    else:
        (K, M), N = a.shape, b.shape[1]
    has_add = add is not None
    tn = _pick(N, (512, 640, 384, 256, 128))
    tm, tk = _matmul_tiles(M, K, tn, a.dtype.itemsize, b.dtype.itemsize, has_add)
    nk = K // tk
    a_spec = pl.BlockSpec((tk, tm), lambda i, j, k: (k, i)) if kind == "tn" else pl.BlockSpec((tm, tk), lambda i, j, k: (i, k))
    b_spec = pl.BlockSpec((tn, tk), lambda i, j, k: (j, k)) if kind == "nt" else pl.BlockSpec((tk, tn), lambda i, j, k: (k, j))
    o_spec = pl.BlockSpec((tm, tn), lambda i, j, k: (i, j))

    extra = ([add] if has_add else []) + ([after] if after is not None else [])
    extra_specs = ([o_spec] if has_add else []) + ([pl.BlockSpec(TOKEN_SHAPE, lambda i, j, k: (0, 0))] if after is not None else [])

    def body(a_ref, b_ref, *rest):
        add_ref = rest[0] if has_add else None
        o_ref = rest[len(extra)]
        part = _dot_raw(a_ref[...], b_ref[...], kind, False)
        if nk == 1:
            o_ref[...] = part + add_ref[...] if has_add else part
            return
        acc = rest[-1]
        kk = pl.program_id(2)

        @pl.when(kk == 0)
        def _():
            acc[...] = part

        @pl.when(kk > 0)
        def _():
            acc[...] += part

        @pl.when(kk == nk - 1)
        def _():
            o_ref[...] = acc[...] + add_ref[...] if has_add else acc[...]

    return pl.pallas_call(
        body, name=name, grid=(M // tm, N // tn, nk),
        in_specs=[a_spec, b_spec] + extra_specs,
        out_specs=o_spec, out_shape=jax.ShapeDtypeStruct((M, N), F32),
        scratch_shapes=[pltpu.VMEM((tm, tn), F32)] if nk > 1 else [],
        compiler_params=_params(dimension_semantics=("parallel", "parallel", "arbitrary")),
    )(a, b, *extra)


def _halo_specs(ts, nt, width, prev=True, main=True, nxt=True):
    per = ts // SUBLANES
    last8 = nt * per - 1
    specs = []
    if prev:
        specs.append(pl.BlockSpec((SUBLANES, width), lambda cb, i: (jnp.maximum(i * per - 1, 0), cb)))
    if main:
        specs.append(pl.BlockSpec((ts, width), lambda cb, i: (i, cb)))
    if nxt:
        specs.append(pl.BlockSpec((SUBLANES, width), lambda cb, i: (jnp.minimum((i + 1) * per, last8), cb)))
    return specs


def _taps(ext, ktaps, lo, size):
    return [ext[lo:lo + size] if j == 0 else pltpu.roll(ext, j, 0)[lo:lo + size] for j in range(ktaps)]


def _ahead(ext, j, size):
    n = ext.shape[0]
    return ext[:size] if j == 0 else pltpu.roll(ext, n - j, 0)[:size]


def _lane_block(ref, k):
    return ref[:, k * MIXER_LANES:(k + 1) * MIXER_LANES]


def _mixer_a_fwd(proj_a, conv_w):
    T = proj_a.shape[0]
    nblk = proj_a.shape[1] // (4 * MIXER_LANES)
    ts = min(ROW_TILE, T)
    nt = T // ts

    def body(pp, pm, w_ref, y_ref):
        i = pl.program_id(1)
        u_prev = jnp.where(i > 0, _lane_block(pp, 0) * _lane_block(pp, 1), 0.0)
        ext = jnp.concatenate([u_prev, _lane_block(pm, 0) * _lane_block(pm, 1)], axis=0)
        t0, t1, t2 = _taps(ext, 3, SUBLANES, ts)
        cv = w_ref[2:3, :] * t0 + w_ref[1:2, :] * t1 + w_ref[0:1, :] * t2
        y_ref[...] = (_lane_block(pm, 2) * cv * _silu(_lane_block(pm, 3))).astype(y_ref.dtype)

    return pl.pallas_call(
        body, name="mixer_a_fwd", grid=(nblk, nt),
        in_specs=_halo_specs(ts, nt, 4 * MIXER_LANES, nxt=False)
        + [pl.BlockSpec((conv_w.shape[0], MIXER_LANES), lambda cb, i: (0, cb))],
        out_specs=pl.BlockSpec((ts, MIXER_LANES), lambda cb, i: (i, cb)),
        out_shape=jax.ShapeDtypeStruct((T, nblk * MIXER_LANES), BF16), compiler_params=_params(),
    )(proj_a, proj_a, conv_w)


def _mixer_a_bwd(proj_a, dy, conv_w):
    T = proj_a.shape[0]
    nblk = proj_a.shape[1] // (4 * MIXER_LANES)
    ts = min(ROW_TILE, T)
    nt = T // ts
    kt = conv_w.shape[0]

    def body(pp, pm, pn, dym, dyn, w_ref, dp_ref, dw_ref):
        i = pl.program_id(1)
        hm, cm, bm, zm = (_lane_block(pm, k) for k in range(4))
        u_prev = jnp.where(i > 0, _lane_block(pp, 0) * _lane_block(pp, 1), 0.0)
        ext = jnp.concatenate([u_prev, hm * cm], axis=0)
        dy_ext = jnp.concatenate([dym[...], jnp.where(i < nt - 1, dyn[...], 0.0)], axis=0)
        b_ext = jnp.concatenate([bm, _lane_block(pn, 2)], axis=0)
        sz_ext = _silu(jnp.concatenate([zm, _lane_block(pn, 3)], axis=0))
        dcv_ext = dy_ext * b_ext * sz_ext
        w = [w_ref[j:j + 1, :] for j in range(kt)]
        du = sum(w[kt - 1 - j] * _ahead(dcv_ext, j, ts) for j in range(kt))
        taps = _taps(ext, kt, SUBLANES, ts)
        cv = sum(w[kt - 1 - j] * taps[j] for j in range(kt))
        for part, d in enumerate((du * cm, du * hm, dym[...] * cv * sz_ext[:ts], dym[...] * bm * cv * _dsilu(zm))):
            dp_ref[:, part * MIXER_LANES:(part + 1) * MIXER_LANES] = d.astype(dp_ref.dtype)
        dcv = dcv_ext[:ts]

        @pl.when(i == 0)
        def _():
            dw_ref[...] = jnp.zeros_like(dw_ref)

        for j in range(kt):
            dw_ref[j:j + 1, :] += jnp.sum(dcv * taps[kt - 1 - j], axis=0, keepdims=True)

    return pl.pallas_call(
        body, name="mixer_a_bwd", grid=(nblk, nt),
        in_specs=_halo_specs(ts, nt, 4 * MIXER_LANES) + _halo_specs(ts, nt, MIXER_LANES, prev=False)
        + [pl.BlockSpec((kt, MIXER_LANES), lambda cb, i: (0, cb))],
        out_specs=[pl.BlockSpec((ts, 4 * MIXER_LANES), lambda cb, i: (i, cb)),
                   pl.BlockSpec((SUBLANES, MIXER_LANES), lambda cb, i: (0, cb))],
        out_shape=[jax.ShapeDtypeStruct(proj_a.shape, BF16),
                   jax.ShapeDtypeStruct((SUBLANES, nblk * MIXER_LANES), F32)],
        compiler_params=_params(),
    )(proj_a, proj_a, proj_a, dy, dy, conv_w)


def _conv_b_fwd(raw, conv_w):
    T = raw.shape[0]
    nblk = raw.shape[1] // CONV_LANES
    ts = min(ROW_TILE, T)
    nt = T // ts
    kt = conv_w.shape[0]

    def body(rp, rm, w_ref, y_ref):
        i = pl.program_id(1)
        ext = jnp.concatenate([jnp.where(i > 0, rp[...], 0.0), rm[...]], axis=0)
        taps = _taps(ext, kt, SUBLANES, ts)
        y_ref[...] = _silu(sum(w_ref[kt - 1 - j:kt - j, :] * taps[j] for j in range(kt)))

    return pl.pallas_call(
        body, name="conv_b_fwd", grid=(nblk, nt),
        in_specs=_halo_specs(ts, nt, CONV_LANES, nxt=False) + [pl.BlockSpec((kt, CONV_LANES), lambda cb, i: (0, cb))],
        out_specs=pl.BlockSpec((ts, CONV_LANES), lambda cb, i: (i, cb)),
        out_shape=jax.ShapeDtypeStruct(raw.shape, F32), compiler_params=_params(),
    )(raw, raw, conv_w)


def _conv_b_bwd(raw, dy, conv_w):
    T = raw.shape[0]
    nblk = raw.shape[1] // CONV_LANES
    ts = min(ROW_TILE, T)
    nt = T // ts
    kt = conv_w.shape[0]

    def body(rp, rm, rn, dym, dyn, w_ref, dr_ref, dw_ref):
        i = pl.program_id(1)
        ext = jnp.concatenate([jnp.where(i > 0, rp[...], 0.0), rm[...], rn[...]], axis=0)
        w = [w_ref[j:j + 1, :] for j in range(kt)]
        taps = _taps(ext, kt, SUBLANES, ts + SUBLANES)
        xc_ext = sum(w[kt - 1 - j] * taps[j] for j in range(kt))
        dy_ext = jnp.concatenate([dym[...], jnp.where(i < nt - 1, dyn[...], 0.0)], axis=0)
        dxc_ext = dy_ext * _dsilu(xc_ext)
        dr_ref[...] = sum(w[kt - 1 - j] * _ahead(dxc_ext, j, ts) for j in range(kt)).astype(dr_ref.dtype)
        dxc = dxc_ext[:ts]

        @pl.when(i == 0)
        def _():
            dw_ref[...] = jnp.zeros_like(dw_ref)

        for j in range(kt):
            dw_ref[j:j + 1, :] += jnp.sum(dxc * taps[kt - 1 - j][:ts], axis=0, keepdims=True)

    return pl.pallas_call(
        body, name="conv_b_bwd", grid=(nblk, nt),
        in_specs=_halo_specs(ts, nt, CONV_LANES) + _halo_specs(ts, nt, CONV_LANES, prev=False)
        + [pl.BlockSpec((kt, CONV_LANES), lambda cb, i: (0, cb))],
        out_specs=[pl.BlockSpec((ts, CONV_LANES), lambda cb, i: (i, cb)),
                   pl.BlockSpec((SUBLANES, CONV_LANES), lambda cb, i: (0, cb))],
        out_shape=[jax.ShapeDtypeStruct(raw.shape, BF16), jax.ShapeDtypeStruct((SUBLANES, nblk * CONV_LANES), F32)],
        compiler_params=_params(),
    )(raw, raw, raw, dy, dy, conv_w)


def _split_heads(ref, base, nheads, rows=slice(None)):
    return jnp.stack([ref[rows, base + h * HEAD: base + (h + 1) * HEAD] for h in range(nheads)], axis=0)


def _store_heads(ref, base, x, rows=slice(None), accumulate=False):
    for h in range(x.shape[0]):
        lanes = slice(base + h * HEAD, base + (h + 1) * HEAD)
        if accumulate:
            ref[rows, lanes] += x[h]
        else:
            ref[rows, lanes] = x[h].astype(ref.dtype)


def _gdn_fwd(qkv, proj_zs, alog, dtb, gain, H):
    T = qkv.shape[0]
    C, HW = GDN_CHUNK, H * HEAD
    nc = T // C
    zw = HW + LANES

    def body(qkv_ref, zs_ref, alog_ref, dtb_ref, gain_ref, o_ref, sall_ref, s_scr):
        @pl.when(pl.program_id(0) == 0)
        def _():
            s_scr[...] = jnp.zeros_like(s_scr)

        sall_ref[0] = s_scr[...]
        outs, states = _gdn_step(
            s_scr[...], _split_heads(qkv_ref, 0, H), _split_heads(qkv_ref, HW, H),
            _split_heads(qkv_ref, 2 * HW, H), _split_heads(zs_ref, 0, H), zs_ref[:, HW:HW + LANES],
            alog_ref[...], dtb_ref[...], gain_ref[...])
        _store_heads(o_ref, 0, outs)
        s_scr[...] = states

    row = pl.BlockSpec((1, LANES), lambda i: (0, 0))
    return pl.pallas_call(
        body, name="gdn_fwd", grid=(nc,),
        in_specs=[pl.BlockSpec((C, 3 * HW), lambda i: (i, 0)), pl.BlockSpec((C, zw), lambda i: (i, 0)), row, row, row],
        out_specs=[pl.BlockSpec((C, HW), lambda i: (i, 0)), pl.BlockSpec((1, H, HEAD, HEAD), lambda i: (i, 0, 0, 0))],
        out_shape=[jax.ShapeDtypeStruct((T, HW), BF16), jax.ShapeDtypeStruct((nc, H, HEAD, HEAD), F32)],
        scratch_shapes=[pltpu.VMEM((H, HEAD, HEAD), F32)], compiler_params=_params(),
    )(qkv, proj_zs, alog, dtb, gain)


def _gdn_bwd(qkv, proj_zs, do, s_all, alog, dtb, gain, H):
    T = qkv.shape[0]
    C, HW = GDN_CHUNK, H * HEAD
    nc = T // C
    zw = HW + LANES

    def body(qkv_ref, zs_ref, do_ref, sin_ref, alog_ref, dtb_ref, gain_ref,
             dqkv_ref, dzs_ref, dalog_ref, ddtb_ref, dgain_ref, ds_scr):
        @pl.when(pl.program_id(0) == 0)
        def _():
            ds_scr[...] = jnp.zeros_like(ds_scr)
            dalog_ref[...] = jnp.zeros_like(dalog_ref)
            ddtb_ref[...] = jnp.zeros_like(ddtb_ref)
            dgain_ref[...] = jnp.zeros_like(dgain_ref)

        primals = (sin_ref[0], _split_heads(qkv_ref, 0, H),
                   _split_heads(qkv_ref, HW, H), _split_heads(qkv_ref, 2 * HW, H), _split_heads(zs_ref, 0, H),
                   zs_ref[:, HW:HW + LANES], alog_ref[...], dtb_ref[...], gain_ref[...])
        _, vjp = jax.vjp(_gdn_step, *primals)
        dS, dq, dk, dv, dz, dsmall, dalog, ddtb, dgain = vjp((_split_heads(do_ref, 0, H), ds_scr[...]))
        ds_scr[...] = dS
        _store_heads(dqkv_ref, 0, dq)
        _store_heads(dqkv_ref, HW, dk)
        _store_heads(dqkv_ref, 2 * HW, dv)
        _store_heads(dzs_ref, 0, dz)
        dzs_ref[:, HW:HW + LANES] = dsmall.astype(dzs_ref.dtype)
        dalog_ref[...] += dalog
        ddtb_ref[...] += ddtb
        dgain_ref[...] += dgain

    row = pl.BlockSpec((1, LANES), lambda i: (0, 0))
    rev = lambda i: nc - 1 - i
    return pl.pallas_call(
        body, name="gdn_bwd", grid=(nc,),
        in_specs=[pl.BlockSpec((C, 3 * HW), lambda i: (rev(i), 0)), pl.BlockSpec((C, zw), lambda i: (rev(i), 0)),
                  pl.BlockSpec((C, HW), lambda i: (rev(i), 0)),
                  pl.BlockSpec((1, H, HEAD, HEAD), lambda i: (rev(i), 0, 0, 0)), row, row, row],
        out_specs=[pl.BlockSpec((C, 3 * HW), lambda i: (rev(i), 0)), pl.BlockSpec((C, zw), lambda i: (rev(i), 0)),
                   row, row, row],
        out_shape=[jax.ShapeDtypeStruct(qkv.shape, F32), jax.ShapeDtypeStruct(proj_zs.shape, BF16)]
        + [jax.ShapeDtypeStruct((1, LANES), F32)] * 3,
        scratch_shapes=[pltpu.VMEM((H, HEAD, HEAD), F32)], compiler_params=_params(),
    )(qkv, proj_zs, do, s_all, alog, dtb, gain)


def _hgrn_refs(proj_ref, lb_ref, HP):
    W = HP * HEAD
    return (_split_heads(proj_ref, 0, HP), _split_heads(proj_ref, W, HP), _split_heads(proj_ref, 2 * W, HP),
            _split_heads(proj_ref, 3 * W, HP), _split_heads(lb_ref, 0, HP, slice(0, 1)),
            _split_heads(lb_ref, 0, HP, slice(1, 2)))


def _hgrn_fwd(proj, lower_bounds, gain, nheads):
    T = proj.shape[0]
    C, HP = HGRN_CHUNK, HGRN_HEADS_PER_STEP
    ng, nc, W = nheads // HP, T // C, HP * HEAD

    def body(proj_ref, lb_ref, gain_ref, o_ref, sall_ref, s_scr):
        @pl.when(pl.program_id(1) == 0)
        def _():
            s_scr[...] = jnp.zeros_like(s_scr)

        sall_ref[0] = s_scr[...]
        qr, fr, vi, z, lb0, lb1 = _hgrn_refs(proj_ref, lb_ref, HP)
        outs, states = _hgrn_step(s_scr[...], qr, fr, vi, z, lb0, lb1, gain_ref[...])
        _store_heads(o_ref, 0, outs)
        s_scr[...] = states

    return pl.pallas_call(
        body, name="hgrn_fwd", grid=(ng, nc),
        in_specs=[pl.BlockSpec((C, 4 * W), lambda g, i: (i, g)), pl.BlockSpec((2, W), lambda g, i: (0, g)),
                  pl.BlockSpec((1, LANES), lambda g, i: (0, 0))],
        out_specs=[pl.BlockSpec((C, W), lambda g, i: (i, g)),
                   pl.BlockSpec((1, HP, HEAD, HEAD), lambda g, i: (i, g, 0, 0))],
        out_shape=[jax.ShapeDtypeStruct((T, nheads * HEAD), BF16), jax.ShapeDtypeStruct((nc, nheads, HEAD, HEAD), F32)],
        scratch_shapes=[pltpu.VMEM((HP, HEAD, HEAD), F32)], compiler_params=_params(),
    )(proj, lower_bounds, gain)


def _hgrn_bwd(proj, do, s_all, lower_bounds, gain, nheads):
    T = proj.shape[0]
    C, HP = HGRN_CHUNK, HGRN_HEADS_PER_STEP
    ng, nc, W = nheads // HP, T // C, HP * HEAD

    def body(proj_ref, do_ref, sin_ref, lb_ref, gain_ref, dproj_ref, dlb_ref, dgain_ref, ds_scr):
        first = pl.program_id(1) == 0

        @pl.when(first)
        def _():
            ds_scr[...] = jnp.zeros_like(ds_scr)
            dlb_ref[...] = jnp.zeros_like(dlb_ref)

        @pl.when(first & (pl.program_id(0) == 0))
        def _():
            dgain_ref[...] = jnp.zeros_like(dgain_ref)

        qr, fr, vi, z, lb0, lb1 = _hgrn_refs(proj_ref, lb_ref, HP)
        primals = (sin_ref[0], qr, fr, vi, z, lb0, lb1, gain_ref[...])
        _, vjp = jax.vjp(_hgrn_step, *primals)
        dS, dq, df, dv, dz, dlb0, dlb1, dgain = vjp((_split_heads(do_ref, 0, HP), ds_scr[...]))
        ds_scr[...] = dS
        for part, d in enumerate((dq, df, dv, dz)):
            _store_heads(dproj_ref, part * W, d)
        _store_heads(dlb_ref, 0, dlb0, slice(0, 1), accumulate=True)
        _store_heads(dlb_ref, 0, dlb1, slice(1, 2), accumulate=True)
        dgain_ref[...] += dgain

    rev = lambda i: nc - 1 - i
    return pl.pallas_call(
        body, name="hgrn_bwd", grid=(ng, nc),
        in_specs=[pl.BlockSpec((C, 4 * W), lambda g, i: (rev(i), g)), pl.BlockSpec((C, W), lambda g, i: (rev(i), g)),
                  pl.BlockSpec((1, HP, HEAD, HEAD), lambda g, i: (rev(i), g, 0, 0)),
                  pl.BlockSpec((2, W), lambda g, i: (0, g)), pl.BlockSpec((1, LANES), lambda g, i: (0, 0))],
        out_specs=[pl.BlockSpec((C, 4 * W), lambda g, i: (rev(i), g)), pl.BlockSpec((2, W), lambda g, i: (0, g)),
                   pl.BlockSpec((1, LANES), lambda g, i: (0, 0))],
        out_shape=[jax.ShapeDtypeStruct(proj.shape, BF16), jax.ShapeDtypeStruct(lower_bounds.shape, F32),
                   jax.ShapeDtypeStruct((1, LANES), F32)],
        scratch_shapes=[pltpu.VMEM((HP, HEAD, HEAD), F32)], compiler_params=_params(),
    )(proj, do, s_all, lower_bounds, gain)


def _post_specs(T):
    tr = min(POST_TILE, T)
    tile = lambda w: pl.BlockSpec((tr, w), lambda i: (i, 0))
    full = lambda r, w: pl.BlockSpec((r, w), lambda i: (0, 0))
    return tr, tile, full


def _post_fwd(s, x, p, g, b, wg, wpl, name):
    T, D = x.shape
    P = p.shape[1]
    tr, tile, full = _post_specs(T)

    def body(s_ref, x_ref, p_ref, g_ref, b_ref, wg_ref, wpl_ref, o_ref, o16_ref):
        x1 = _post_norm(s_ref[...], x_ref[...], g_ref[...], b_ref[...])
        xn = _post_gate(x1, _dot_raw(x1, wg_ref[...], "nn", False), _dot_raw(p_ref[...], wpl_ref[...], "nn", False))
        o_ref[...] = xn
        o16_ref[...] = xn.astype(BF16)

    return pl.pallas_call(
        body, name=name, grid=(T // tr,),
        in_specs=[tile(D), tile(D), tile(P), full(1, D), full(1, D), full(D, D), full(P, D)],
        out_specs=[tile(D), tile(D)],
        out_shape=[jax.ShapeDtypeStruct((T, D), F32), jax.ShapeDtypeStruct((T, D), BF16)], compiler_params=_params(),
    )(s, x, p, g, b, wg, wpl)


def _post_bwd(s, x, p, g, b, wg, wpl, dnext, name, with_loss):
    T, D = x.shape
    P = p.shape[1]
    tr, tile, full = _post_specs(T)

    def body(s_ref, x_ref, p_ref, g_ref, b_ref, wg_ref, wpl_ref, dn_ref,
             ds_ref, dx_ref, dg_ref, db_ref, dwg_ref, dwpl_ref, loss_ref):
        @pl.when(pl.program_id(0) == 0)
        def _():
            for r in (dg_ref, db_ref, dwg_ref, dwpl_ref, loss_ref):
                r[...] = jnp.zeros_like(r)

        x1, vjp_norm = jax.vjp(_post_norm, s_ref[...], x_ref[...], g_ref[...], b_ref[...])
        gate_pre = _dot_raw(x1, wg_ref[...], "nn", False)
        pp = _dot_raw(p_ref[...], wpl_ref[...], "nn", False)
        xn, vjp_gate = jax.vjp(_post_gate, x1, gate_pre, pp)
        if with_loss:
            err = xn - dn_ref[...]
            loss_ref[...] += 0.5 * jnp.sum(jnp.sum(err * err, axis=-1, keepdims=True), axis=0, keepdims=True) / D
            dn = err / D
        else:
            dn = dn_ref[...]
        dx1, dgp, dpp = vjp_gate(dn)
        dwg_ref[...] += _dot_raw(x1, dgp, "tn", False)
        dwpl_ref[...] += _dot_raw(p_ref[...], dpp, "tn", False)
        dx1 = dx1 + _dot_raw(dgp, wg_ref[...], "nt", False)
        ds, dx, dg, db = vjp_norm(dx1)
        ds_ref[...] = ds.astype(ds_ref.dtype)
        dx_ref[...] = dx
        dg_ref[...] += dg
        db_ref[...] += db

    return pl.pallas_call(
        body, name=name, grid=(T // tr,),
        in_specs=[tile(D), tile(D), tile(P), full(1, D), full(1, D), full(D, D), full(P, D), tile(D)],
        out_specs=[tile(D), tile(D), full(1, D), full(1, D), full(D, D), full(P, D), full(SUBLANES, LANES)],
        out_shape=[jax.ShapeDtypeStruct((T, D), BF16), jax.ShapeDtypeStruct((T, D), F32)]
        + [jax.ShapeDtypeStruct((1, D), F32)] * 2
        + [jax.ShapeDtypeStruct((D, D), F32), jax.ShapeDtypeStruct((P, D), F32),
           jax.ShapeDtypeStruct((SUBLANES, LANES), F32)],
        compiler_params=_params(),
    )(s, x, p, g, b, wg, wpl, dnext)


def _adam_math(w, g, m, v):
    m = ADAM_B1 * m + (1.0 - ADAM_B1) * g
    v = ADAM_B2 * v + (1.0 - ADAM_B2) * (g * g)
    m_hat = m / (1.0 - ADAM_B1 ** ADAM_STEP)
    v_hat = v / (1.0 - ADAM_B2 ** ADAM_STEP)
    return -ADAM_LR * (m_hat / (jnp.sqrt(v_hat) + ADAM_EPS) + ADAM_WD * w), m, v


def _shard_tiles(R, C):
    tr = _pick(R, (256, 128, 64, 32, 16, 8))
    return (tr, C) if tr < R or R % SUBLANES == 0 else (R, _pick(C, (256, 128)))


def _rs_add(g8, got, cidx, name):
    R, C = _owner_blocks(g8)
    tr, tc = _shard_tiles(R, C)
    nc = C // tc

    def body(c_ref, a_ref, b_ref, o_ref, o16_ref):
        total = (a_ref[0] if g8.ndim == 3 else a_ref[...]) + b_ref[0]
        o_ref[0] = total
        o16_ref[0] = total.astype(BF16)

    if g8.ndim == 3:
        mine_spec = pl.BlockSpec((1, tr, tc), lambda q, i, j, c: (2 * q + c[0], i, j))
    else:
        mine_spec = pl.BlockSpec((tr, tc), lambda q, i, j, c: (i, (2 * q + c[0]) * nc + j))
    out_spec = pl.BlockSpec((1, tr, tc), lambda q, i, j, c: (q, i, j))
    return pl.pallas_call(
        body, name=name,
        grid_spec=pltpu.PrefetchScalarGridSpec(
            num_scalar_prefetch=1, grid=(4, R // tr, nc),
            in_specs=[mine_spec, out_spec], out_specs=[out_spec, out_spec]),
        out_shape=[jax.ShapeDtypeStruct((4, R, C), F32), jax.ShapeDtypeStruct((4, R, C), BF16)],
        compiler_params=_params(),
    )(cidx, g8, got)


def _adam_sharded(w, m, v, mine, got, qidx, name):
    L, R, C = w.shape
    tr, tc = _shard_tiles(R, C)
    nr = R // tr

    def body(q_ref, w_ref, m_ref, v_ref, p_ref, r0, r1, r2, g_ref, d_ref, mo_ref, vo_ref):
        g = ((p_ref[0] + r0[0].astype(F32)) + r1[0].astype(F32)) + r2[0].astype(F32)
        d, mn, vn = _adam_math(w_ref[0], g, m_ref[0], v_ref[0])
        g_ref[0] = g
        d_ref[0] = d
        mo_ref[0] = mn
        vo_ref[0] = vn

    t3 = pl.BlockSpec((1, tr, tc), lambda l, i, j, q: (l, i, j))
    slot = lambda k: pl.BlockSpec((1, tr, tc), lambda l, i, j, q: (k, l * nr + i, j))
    return pl.pallas_call(
        body, name=name,
        grid_spec=pltpu.PrefetchScalarGridSpec(
            num_scalar_prefetch=1, grid=(L, nr, C // tc),
            in_specs=[t3, t3, t3, pl.BlockSpec((1, tr, tc), lambda l, i, j, q: (q[0], l * nr + i, j)),
                      slot(0), slot(1), slot(2)],
            out_specs=[t3, t3, t3, t3]),
        out_shape=[jax.ShapeDtypeStruct((L, R, C), F32)] * 4, compiler_params=_params(),
    )(qidx, w, m, v, mine, got, got, got)


def _adam_replicated(w, m, v, g8):
    def body(w_ref, m_ref, v_ref, g_ref, go_ref, d_ref, mo_ref, vo_ref):
        g = g_ref[0]
        for k in range(1, 8):
            g = g + g_ref[k]
        d, mn, vn = _adam_math(w_ref[...], g, m_ref[...], v_ref[...])
        go_ref[...] = g
        d_ref[...] = d
        mo_ref[...] = mn
        vo_ref[...] = vn

    return pl.pallas_call(
        body, name="adam_replicated", out_shape=[jax.ShapeDtypeStruct(w.shape, F32)] * 4, compiler_params=_params(),
    )(w, m, v, g8)


def _place():
    return lax.axis_index("x"), lax.axis_index("y"), lax.axis_index("c")


def _all_gather(shard, name):
    def body(x_ref, out_ref, send_sems, recv_sems, local_sem):
        x, y, c = _place()
        me, sibling = (x, y, c), (x, y, 1 - c)
        chips = [(1 - x, y), (x, 1 - y), (1 - x, 1 - y)]

        def slab(px, py, pc):
            return out_ref.at[4 * px + 2 * py + pc]

        def copy(k, block, to, src=None):
            return pltpu.make_async_remote_copy(
                src_ref=slab(*block) if src is None else src, dst_ref=slab(*block),
                send_sem=send_sems.at[k], recv_sem=recv_sems.at[k], device_id=to, device_id_type=MESH)

        mine = pltpu.make_async_copy(x_ref, slab(*me), local_sem)
        mine.start()
        first = [copy(0, me, sibling, src=x_ref)]
        first += [copy(1 + j, me, (*chip, c), src=x_ref) for j, chip in enumerate(chips)]
        for cp in first:
            cp.start()
        passed = [copy(4 + j, (*chip, c), sibling) for j, chip in enumerate(chips)]
        for j, chip in enumerate(chips):
            copy(1 + j, (*chip, c), me).wait_recv()
            passed[j].start()
        copy(0, sibling, me).wait_recv()
        for j, chip in enumerate(chips):
            copy(4 + j, (*chip, 1 - c), me).wait_recv()
        for cp in first + passed:
            cp.wait_send()
        mine.wait()

    return pl.pallas_call(
        body, name=name, out_shape=jax.ShapeDtypeStruct((8,) + shard.shape, shard.dtype),
        in_specs=[pl.BlockSpec(memory_space=pl.ANY)], out_specs=pl.BlockSpec(memory_space=pl.ANY),
        scratch_shapes=[pltpu.SemaphoreType.DMA((7,)), pltpu.SemaphoreType.DMA((7,)), pltpu.SemaphoreType.DMA],
    )(shard)


def _owner_blocks(g8):
    return g8.shape[1:] if g8.ndim == 3 else (g8.shape[0], g8.shape[1] // 8)


def _rs_to_sibling(g8, name):
    R, C = _owner_blocks(g8)

    def body(g_ref, out_ref, send_sems, recv_sems):
        x, y, c = _place()
        block = (lambda k: g_ref.at[k]) if g8.ndim == 3 else (lambda k: g_ref.at[:, pl.ds(k * C, C)])
        copies = [pltpu.make_async_remote_copy(
            src_ref=block(2 * q + (1 - c)), dst_ref=out_ref.at[q], send_sem=send_sems.at[q],
            recv_sem=recv_sems.at[q], device_id=(x, y, 1 - c), device_id_type=MESH) for q in range(4)]
        for cp in copies:
            cp.start()
        for cp in copies:
            cp.wait()

    return pl.pallas_call(
        body, name=name, out_shape=jax.ShapeDtypeStruct((4, R, C), g8.dtype),
        in_specs=[pl.BlockSpec(memory_space=pl.ANY)], out_specs=pl.BlockSpec(memory_space=pl.ANY),
        scratch_shapes=[pltpu.SemaphoreType.DMA((4,)), pltpu.SemaphoreType.DMA((4,))],
    )(g8)


_HBM = pl.BlockSpec(memory_space=pltpu.HBM)
_SEM = pl.BlockSpec(memory_space=pltpu.SEMAPHORE)
_DATAFLOW = pltpu.SideEffectType.DATAFLOW_SIDE_EFFECTING
TOKEN_SHAPE = (SUBLANES, LANES)


def _chip_plan(x, y, c):
    return [(2 * px + py, j, (px, py, c)) for j, (px, py) in enumerate([(1 - x, y), (x, 1 - y), (1 - x, 1 - y)])]


def _exchange_copies(plan, src_ref, land_ref, send_sems, recv_sems):
    return [pltpu.make_async_remote_copy(
        src_ref=src_ref.at[blk], dst_ref=land_ref.at[slot], send_sem=send_sems.at[k], recv_sem=recv_sems.at[k],
        device_id=peer, device_id_type=MESH) for k, (blk, slot, peer) in enumerate(plan(*_place()))]


def _exchange_start(src, n_slots, plan, name):
    land_shape = (n_slots,) + src.shape[1:]
    n = len(plan(0, 0, 0))

    def body(src_ref, land_ref, send_sems, recv_sems, src_thru, land_thru, token):
        for cp in _exchange_copies(plan, src_ref, land_ref, send_sems, recv_sems):
            cp.start()
        token[...] = jnp.zeros_like(token)

    return pl.pallas_call(
        body, name=name,
        out_shape=(pltpu.SemaphoreType.DMA((n,)), pltpu.SemaphoreType.DMA((n,)), pltpu.HBM(src.shape, src.dtype),
                   pltpu.HBM(land_shape, src.dtype), jax.ShapeDtypeStruct(TOKEN_SHAPE, F32)),
        in_specs=(_HBM, _HBM), out_specs=(_SEM, _SEM, _HBM, _HBM, pl.BlockSpec(memory_space=pltpu.VMEM)),
        input_output_aliases={0: 2, 1: 3}, compiler_params=pltpu.CompilerParams(has_side_effects=_DATAFLOW),
    )(pltpu.with_memory_space_constraint(src, pltpu.HBM),
      pltpu.with_memory_space_constraint(lax.empty(land_shape, src.dtype), pltpu.HBM))


def _exchange_wait(handle, plan, after, name):
    send_sems, recv_sems, src_thru, land_thru, _ = handle

    def body(src_ref, land_ref, send_sems, recv_sems, after_ref, src_dead, got_ref):
        for cp in _exchange_copies(plan, src_ref, land_ref, send_sems, recv_sems):
            cp.wait_send()
            cp.wait_recv()

    return pl.pallas_call(
        body, name=name,
        out_shape=(pltpu.HBM(src_thru.shape, src_thru.dtype), pltpu.HBM(land_thru.shape, land_thru.dtype)),
        in_specs=(_HBM, _HBM, _SEM, _SEM, pl.BlockSpec(memory_space=pl.ANY)), out_specs=(_HBM, _HBM),
        input_output_aliases={0: 0, 1: 1}, compiler_params=pltpu.CompilerParams(has_side_effects=_DATAFLOW),
    )(src_thru, land_thru, send_sems, recv_sems, after)[1]


def _gather_plan(x, y, c):
    me = 4 * x + 2 * y + c
    flip = lambda v, bit: 1 - v if bit else v
    return [(0, me, (flip(x, r >> 2 & 1), flip(y, r >> 1 & 1), flip(c, r & 1))) for r in range(1, 8)]


class _LateGather:
    def __init__(self, shard, name):
        self.shard, self.name = shard, name
        self.handle = _exchange_start(shard[None], 8, _gather_plan, name + "_start")

    def get(self, after):
        land = _exchange_wait(self.handle, _gather_plan, after, self.name + "_wait")
        x, y, c = _place()
        return lax.dynamic_update_slice(land, self.shard[None], (4 * x + 2 * y + c, 0, 0))


class _GradExchange:
    def __init__(self, cidx, qidx, layouts):
        self.cidx, self.qidx, self.layouts, self.pending = cidx, qidx, layouts, {}

    def start(self, tag, grad):
        g8 = self.layouts[tag](grad)
        got = _rs_to_sibling(g8, "rs_sibling_" + tag)
        chip_sums, chip_sums16 = _rs_add(g8, got, self.cidx, "rs_add_" + tag)
        handle = _exchange_start(chip_sums16, 3, _chip_plan, "rs_chips_start_" + tag)
        self.pending[tag] = (chip_sums, handle)
        return handle[4]

    def finish(self, tag, w, m, v, after):
        chip_sums, handle = self.pending.pop(tag)
        got2 = _exchange_wait(handle, _chip_plan, after, "rs_chips_wait_" + tag)
        return _adam_sharded(w, m, v, chip_sums, got2, self.qidx, "adam_" + tag)


def _local_grads(x, p0, p1, target, wt_zs, wt_a, wt_qkv, late, conv_a, conv_b,
                 a_log, dt_bias, gdn_gain, lower_bounds, hgrn_gain, ln_g, ln_b, on_grad=None):
    H = a_log.shape[1]
    pad_small = ((0, 0), (H, LANES - 2 * H))
    alog_row = jnp.pad(a_log, pad_small)
    dtb_row = jnp.pad(dt_bias, pad_small)

    x16 = x.astype(BF16)
    proj_zs = _matmul(x16, wt_zs, "nt", "proj_even_zs", after=late.started)
    proj_a = _matmul(x16, wt_a, "nt", "proj_even_a", after=late.started)
    proj_qkv = _matmul(x16, wt_qkv, "nt", "proj_even_qkv", after=late.started)
    y_a = _mixer_a_fwd(proj_a, conv_a)
    qkv = _conv_b_fwd(proj_qkv, conv_b)
    o2, s_gdn = _gdn_fwd(qkv, proj_zs, alog_row, dtb_row, gdn_gain, H)
    woute_a, woute_b = late.out_even(o2)
    wg, wpl = late.gate(o2)
    s_e = _matmul(o2, woute_b, "nn", "out_even_b", add=_matmul(y_a, woute_a, "nn", "out_even_a"))
    x2, x2_16 = _post_fwd(s_e, x, p0, ln_g[0:1], ln_b[0:1], wg[0], wpl[0], "post_even_fwd")
    wino, wouto = late.odd(s_e)
    nheads_o = wouto.shape[0] // HEAD
    proj_o = _matmul(x2_16, wino, "nn", "proj_odd")
    o4, s_hgrn = _hgrn_fwd(proj_o, lower_bounds, hgrn_gain, nheads_o)
    s_o = _matmul(o4, wouto, "nn", "out_odd")
    ds_o, dx2, dlng1, dlnb1, dwg1, dwpl1, loss = _post_bwd(
        s_o, x2, p1, ln_g[1:2], ln_b[1:2], wg[1], wpl[1], target, "post_odd_loss_bwd", True)
    do4 = _matmul(ds_o, wouto, "nt", "d_out_odd_act")
    grads = {}

    def emit(tag, grad):
        grads[tag] = grad
        return on_grad(tag, grad) if on_grad is not None else jnp.zeros(TOKEN_SHAPE, F32)

    tok = emit("w_out_odd", _matmul(o4, ds_o, "tn", "d_out_odd_w"))
    dproj_o, dlb, dhgain = _hgrn_bwd(proj_o, do4, s_hgrn, lower_bounds, hgrn_gain + tok[0:1], nheads_o)
    dx2 = _matmul(dproj_o, wino, "nt", "d_proj_odd_act", add=dx2)
    tok = emit("w_in_odd", _matmul(x2_16, dproj_o, "tn", "d_proj_odd_w"))
    ds_e, dx, dlng0, dlnb0, dwg0, dwpl0, _ = _post_bwd(
        s_e, x, p0, ln_g[0:1], ln_b[0:1] + tok[0:1, 0:1], wg[0], wpl[0], dx2, "post_even_bwd", False)
    tok = emit("w_pl_gate", jnp.stack([dwg0, dwg1])) + emit("w_pl", jnp.stack([dwpl0, dwpl1]))
    dy_a = _matmul(ds_e, woute_a, "nt", "d_out_even_a_act")
    do2 = _matmul(ds_e, woute_b, "nt", "d_out_even_b_act")
    dwoute_a = _matmul(y_a, ds_e, "tn", "d_out_even_a_w")
    dwoute_b = _matmul(o2, ds_e, "tn", "d_out_even_b_w")
    tok = tok + emit("w_out_even", jnp.concatenate([dwoute_a, dwoute_b], axis=0))
    dqkv, dproj_zs, dalog, ddtb, dggain = _gdn_bwd(qkv, proj_zs, do2, s_gdn, alog_row, dtb_row, gdn_gain + tok[0:1], H)
    dproj_qkv, dconv_b = _conv_b_bwd(proj_qkv, dqkv, conv_b)
    dproj_a, dconv_a = _mixer_a_bwd(proj_a, dy_a, conv_a)
    emit("conv", (dconv_a[:conv_a.shape[0]], dconv_b[:conv_b.shape[0]]))
    tok = emit("w_in_even", (_matmul(dproj_zs, x16, "tn", "d_proj_even_zs_w"), _matmul(dproj_a, x16, "tn", "d_proj_even_a_w"),
                             _matmul(dproj_qkv, x16, "tn", "d_proj_even_qkv_w")))
    dx = _matmul(dproj_zs, wt_zs, "nn", "d_proj_even_zs_act", add=dx, after=tok)
    dx = _matmul(dproj_a, wt_a, "nn", "d_proj_even_a_act", add=dx)
    dx = _matmul(dproj_qkv, wt_qkv, "nn", "d_proj_even_qkv_act", add=dx)
    grads.update(
        loss=loss[0, 0], grad_x=dx, a_log=dalog[:, H:2 * H], dt_bias=ddtb[:, H:2 * H], gdn_gain=dggain,
        lower_bounds=dlb, hgrn_gain=dhgain, ln_g=jnp.concatenate([dlng0, dlng1], axis=0),
        ln_b=jnp.concatenate([dlnb0, dlnb1], axis=0))
    return grads


def _pad_rows(a, rows):
    return jnp.pad(a, ((0, rows - a.shape[0]), (0, 0)))


def _pack_small(a_log, dt_bias, gdn_gain, lower_bounds, hgrn_gain, ln_g, ln_b):
    lane_pad = lambda a: _pad_rows(jnp.pad(a, ((0, 0), (0, LANES - a.shape[1]))), SUBLANES)
    parts = [lane_pad(a_log), lane_pad(dt_bias), lane_pad(gdn_gain), lower_bounds.reshape(-1, LANES),
             lane_pad(hgrn_gain), ln_g.reshape(-1, LANES), ln_b.reshape(-1, LANES)]
    packed = jnp.concatenate(parts, axis=0)
    assert packed.shape[0] == SMALL_ROWS, packed.shape
    return packed


def _unpack_small(packed, shapes):
    out, r = [], 0
    for shp in shapes:
        n = shp[0] * shp[1]
        if n < LANES * SUBLANES and shp[1] <= LANES:
            out.append(packed[r:r + shp[0], :shp[1]])
            r += SUBLANES
        else:
            rows = n // LANES
            out.append(packed[r:r + rows].reshape(shp))
            r += rows
    return out


def _split_in_even(wt_full, AW, HW, H):
    D = wt_full.shape[1]
    n_a = 4 * AW
    n_main = n_a + 3 * HW
    wt_zs = jnp.concatenate([wt_full[n_main:n_main + HW], wt_full[n_main + HW:],
                             jnp.zeros((LANES - 2 * H, D), wt_full.dtype)], axis=0)
    wt_a = wt_full[:n_a].reshape(4, AW // MIXER_LANES, MIXER_LANES, D).transpose(1, 0, 2, 3).reshape(n_a, D)
    return wt_zs, wt_a, wt_full[n_a:n_main]


def _join_in_even(dt_zs, dt_a, dt_qkv, AW, HW, H):
    D = dt_a.shape[1]
    a_nat = dt_a.reshape(AW // MIXER_LANES, 4, MIXER_LANES, D).transpose(1, 0, 2, 3).reshape(4 * AW, D)
    return jnp.concatenate([a_nat, dt_qkv, dt_zs[:HW], dt_zs[HW:HW + 2 * H]], axis=0)


def kernel(x, p, w_in_even, conv_a_w, conv_b_w, a_log, dt_bias, gdn_norm_g, w_out_even, w_in_odd, lower_bounds, hgrn_norm_g, w_out_odd, ln_g, ln_b, w_pl, w_pl_gate, loss_target, m_w_in_even, m_conv_a_w, m_conv_b_w, m_a_log, m_dt_bias, m_gdn_norm_g, m_w_out_even, m_w_in_odd, m_lower_bounds, m_hgrn_norm_g, m_w_out_odd, m_ln_g, m_ln_b, m_w_pl, m_w_pl_gate, v_w_in_even, v_conv_a_w, v_conv_b_w, v_a_log, v_dt_bias, v_gdn_norm_g, v_w_out_even, v_w_in_odd, v_lower_bounds, v_hgrn_norm_g, v_w_out_odd, v_ln_g, v_ln_b, v_w_pl, v_w_pl_gate):
    xi, yi, ci = _place()
    cidx = jnp.reshape(ci, (1,)).astype(jnp.int32)
    qidx = jnp.reshape(2 * xi + yi, (1,)).astype(jnp.int32)
    D = x.shape[2]
    H = a_log.shape[1]
    HW = H * HEAD
    AW = conv_a_w.shape[2] * 8
    OW = w_out_odd.shape[1] * 8
    PD = w_pl.shape[1]
    ka, kb = conv_a_w.shape[1], conv_b_w.shape[1]
    ca, cb = conv_a_w.shape[2], conv_b_w.shape[2]
    gw = HGRN_HEADS_PER_STEP * HEAD
    ngrp = OW // gw

    transposed = lambda a: jnp.transpose(a, (0, 2, 1))
    g_ine = _all_gather(transposed(w_in_even)[0].astype(BF16), "ag_w_in_even")
    wt_zs, wt_a, wt_qkv = _split_in_even(g_ine.reshape(-1, D), AW, HW, H)
    behind = lambda shard, dep: lax.optimization_barrier((shard, dep))[0]
    late_oute = _LateGather(behind(w_out_even[0].astype(BF16), g_ine), "ag_w_out_even")
    late_gate = _LateGather(behind(w_pl_gate.astype(BF16).reshape(-1, D), late_oute.handle[4]), "ag_w_pl_gate")
    late_pl = _LateGather(behind(w_pl.astype(BF16).reshape(DEPTH * PD, -1), late_gate.handle[4]), "ag_w_pl")
    late_ino = _LateGather(behind(w_in_odd[0].astype(BF16), late_pl.handle[4]), "ag_w_in_odd")
    late_outo = _LateGather(behind(w_out_odd[0].astype(BF16), late_ino.handle[4]), "ag_w_out_odd")

    class _Late:
        started = sum(g.handle[4] for g in (late_oute, late_gate, late_pl, late_ino, late_outo))

        @staticmethod
        def out_even(after):
            woute = late_oute.get(after).reshape(-1, D)
            return woute[:AW], woute[AW:]

        @staticmethod
        def gate(after):
            g_gate, g_pl = late_gate.get(after), late_pl.get(after)
            return (g_gate.reshape(8, DEPTH, D // 8, D).transpose(1, 0, 2, 3).reshape(DEPTH, D, D),
                    g_pl.reshape(8, DEPTH, PD, D // 8).transpose(1, 2, 0, 3).reshape(DEPTH, PD, D))

        @staticmethod
        def odd(after):
            g_ino = late_ino.get(after)
            wino = jnp.transpose(g_ino, (1, 0, 2)).reshape(D, 4, ngrp, gw).transpose(0, 2, 1, 3).reshape(D, 4 * OW)
            return wino, late_outo.get(after).reshape(-1, D)

    taps = jnp.concatenate([_pad_rows(conv_a_w[0], SUBLANES), _pad_rows(conv_b_w[0], SUBLANES)], axis=1)
    g_taps = _all_gather(taps, "ag_conv")
    conv_a = jnp.transpose(g_taps[:, :ka, :ca], (1, 0, 2)).reshape(ka, 8 * ca)
    conv_b = jnp.transpose(g_taps[:, :kb, ca:], (1, 0, 2)).reshape(kb, 8 * cb)

    sh = w_in_even.shape[2]
    tap_blocks = lambda g, width: _pad_rows(g, SUBLANES).reshape(SUBLANES, 8, width).transpose(1, 0, 2)
    owner_layout = {
        "w_in_even": lambda g: _join_in_even(*g, AW, HW, H).reshape(8, sh, D),
        "w_in_odd": lambda g: g.reshape(D, ngrp, 4, gw).transpose(0, 2, 1, 3).reshape(D, 4 * OW),
        "w_out_even": lambda g: g.reshape(8, -1, D),
        "w_out_odd": lambda g: g.reshape(8, -1, D),
        "w_pl_gate": lambda g: g.reshape(DEPTH, 8, D // 8, D).transpose(1, 0, 2, 3).reshape(8, DEPTH * D // 8, D),
        "w_pl": lambda g: g.reshape(DEPTH, PD, 8, D // 8).transpose(2, 0, 1, 3).reshape(8, DEPTH * PD, D // 8),
        "conv": lambda g: jnp.concatenate([tap_blocks(g[0], ca), tap_blocks(g[1], cb)], axis=2),
    }
    exchange = _GradExchange(cidx, qidx, owner_layout)
    gr = _local_grads(x[0], p[0, 0], p[1, 0], loss_target[0], wt_zs, wt_a, wt_qkv, _Late, conv_a, conv_b,
                      a_log, dt_bias, gdn_norm_g, lower_bounds, hgrn_norm_g, ln_g, ln_b, on_grad=exchange.start)

    last = gr["grad_x"]
    pack_taps = lambda a, b: jnp.concatenate([_pad_rows(a[0], SUBLANES), _pad_rows(b[0], SUBLANES)], axis=1)[None]
    o_outo = exchange.finish("w_out_odd", w_out_odd, m_w_out_odd, v_w_out_odd, last)
    o_ino = exchange.finish("w_in_odd", w_in_odd, m_w_in_odd, v_w_in_odd, last)
    o_gate = exchange.finish("w_pl_gate", w_pl_gate, m_w_pl_gate, v_w_pl_gate, last)
    o_pl = exchange.finish("w_pl", w_pl, m_w_pl, v_w_pl, last)
    o_oute = exchange.finish("w_out_even", w_out_even, m_w_out_even, v_w_out_even, last)
    o_taps = exchange.finish("conv", taps[None], pack_taps(m_conv_a_w, m_conv_b_w), pack_taps(v_conv_a_w, v_conv_b_w), last)
    others_done = sum(o[1][0, 0:1, 0:1] for o in (o_outo, o_ino, o_gate, o_pl, o_oute, o_taps))
    o_ine = [transposed(o) for o in exchange.finish(
        "w_in_even", transposed(w_in_even), transposed(m_w_in_even), transposed(v_w_in_even), others_done)]

    small_g = _pack_small(gr["a_log"], gr["dt_bias"], gr["gdn_gain"], gr["lower_bounds"], gr["hgrn_gain"],
                          gr["ln_g"], gr["ln_b"])
    small_g = small_g.at[0, LANES - 1].set(gr["loss"])
    o_small = _adam_replicated(
        _pack_small(a_log, dt_bias, gdn_norm_g, lower_bounds, hgrn_norm_g, ln_g, ln_b),
        _pack_small(m_a_log, m_dt_bias, m_gdn_norm_g, m_lower_bounds, m_hgrn_norm_g, m_ln_g, m_ln_b),
        _pack_small(v_a_log, v_dt_bias, v_gdn_norm_g, v_lower_bounds, v_hgrn_norm_g, v_ln_g, v_ln_b),
        _all_gather(small_g, "ag_small_grads"))
    small_shapes = [a_log.shape, dt_bias.shape, gdn_norm_g.shape, lower_bounds.shape, hgrn_norm_g.shape,
                    ln_g.shape, ln_b.shape]

    def leaves(kind):
        s_alog, s_dt, s_gg, s_lb, s_hg, s_lng, s_lnb = _unpack_small(o_small[kind], small_shapes)
        t = o_taps[kind]
        return [o_ine[kind], t[:, :ka, :ca], t[:, :kb, ca:], s_alog, s_dt, s_gg, o_oute[kind],
                o_ino[kind], s_lb, s_hg, o_outo[kind], s_lng, s_lnb, o_pl[kind], o_gate[kind]]

    return (o_small[0][0, LANES - 1], gr["grad_x"][None], *leaves(0), *leaves(1), *leaves(2), *leaves(3))
```

```python
import functools

import jax
import jax.numpy as jnp
from jax import lax
from jax.experimental import pallas as pl
from jax.experimental.pallas import tpu as pltpu

F32 = jnp.float32
BF16 = jnp.bfloat16
MESH = pl.DeviceIdType.MESH
AXES = ("x", "y", "c")

LANES = 128
SUBLANES = 8
HEAD = 128
GDN_CHUNK = 64
HGRN_CHUNK = 64
HGRN_SUB = 16
HGRN_HEADS_PER_STEP = 16
NORM_EPS = 1e-5
DEPTH = 2
ALPHA = (2.0 * DEPTH) ** 0.25
EXP_CLAMP = 80.0
ADAM_LR, ADAM_B1, ADAM_B2, ADAM_EPS, ADAM_WD, ADAM_STEP = 0.001, 0.9, 0.999, 1e-08, 0.01, 10
VMEM_LIMIT = 56 * 1024 * 1024
MATMUL_VMEM = 36 * 1024 * 1024
ROW_TILE = 512
MIXER_LANES = 256
CONV_LANES = 512
POST_TILE = 256
SMALL_ROWS = 96

_NOBATCH, _BATCH0 = ((), ()), ((0,), (0,))
_DIMS = {"nn": (((1,), (0,)), _NOBATCH), "nt": (((1,), (1,)), _NOBATCH), "tn": (((0,), (0,)), _NOBATCH),
         "bnn": (((2,), (1,)), _BATCH0), "bnt": (((2,), (2,)), _BATCH0), "btn": (((1,), (1,)), _BATCH0)}


def _params(**kw):
    return pltpu.CompilerParams(vmem_limit_bytes=VMEM_LIMIT, **kw)


def _dot_raw(a, b, kind, hi):
    if hi:
        return lax.dot_general(a, b, _DIMS[kind], precision=lax.Precision.HIGHEST, preferred_element_type=F32)
    return lax.dot_general(a.astype(BF16), b.astype(BF16), _DIMS[kind], preferred_element_type=F32)


@functools.partial(jax.custom_vjp, nondiff_argnums=(2, 3))
def mdot(a, b, kind, hi):
    return _dot_raw(a, b, kind, hi)


def _mdot_fwd(a, b, kind, hi):
    return _dot_raw(a, b, kind, hi), (a, b)


def _mdot_bwd(kind, hi, res, g):
    a, b = res
    pre, base = kind[:-2], kind[-2:]
    if base == "nn":
        return _dot_raw(g, b, pre + "nt", hi), _dot_raw(a, g, pre + "tn", hi)
    if base == "nt":
        return _dot_raw(g, b, pre + "nn", hi), _dot_raw(g, a, pre + "tn", hi)
    return _dot_raw(b, g, pre + "nt", hi), _dot_raw(a, g, pre + "nn", hi)


mdot.defvjp(_mdot_fwd, _mdot_bwd)


def _rows(x, lo, hi):
    return _take_rows(x, lo, hi, x.shape[-2])


@functools.partial(jax.custom_vjp, nondiff_argnums=(1, 2, 3))
def _take_rows(x, lo, hi, n):
    return x[..., lo:hi, :]


def _take_rows_fwd(x, lo, hi, n):
    return x[..., lo:hi, :], None


def _take_rows_bwd(lo, hi, n, _, g):
    parts = []
    if lo > 0:
        parts.append(jnp.zeros(g.shape[:-2] + (lo, g.shape[-1]), g.dtype))
    parts.append(g)
    if n - hi > 0:
        parts.append(jnp.zeros(g.shape[:-2] + (n - hi, g.shape[-1]), g.dtype))
    return (jnp.concatenate(parts, axis=-2) if len(parts) > 1 else g,)


_take_rows.defvjp(_take_rows_fwd, _take_rows_bwd)


def _heads_of(wide, nheads):
    return jnp.stack([wide[:, h * HEAD:(h + 1) * HEAD] for h in range(nheads)], axis=0)


def _wide_of(x):
    return jnp.concatenate([x[h] for h in range(x.shape[0])], axis=1)


@functools.partial(jax.custom_vjp, nondiff_argnums=(1,))
def to_heads(wide, nheads):
    return _heads_of(wide, nheads)


to_heads.defvjp(lambda wide, nheads: (_heads_of(wide, nheads), None), lambda nheads, _, g: (_wide_of(g),))


@jax.custom_vjp
def to_wide(x):
    return _wide_of(x)


to_wide.defvjp(lambda x: (_wide_of(x), None), lambda _, g: (_heads_of(g, g.shape[1] // HEAD),))


def _sigmoid(x):
    return jax.nn.sigmoid(x)


def _silu(x):
    return x * _sigmoid(x)


def _dsilu(x):
    s = _sigmoid(x)
    return s * (1.0 + x * (1.0 - s))


def _log1p(u):
    return jnp.where(u < 1e-4, u * (1.0 - 0.5 * u), jnp.log(1.0 + u))


def _softplus(x):
    return jnp.maximum(x, 0.0) + _log1p(jnp.exp(-jnp.abs(x)))


def _rms_gate(o, gain, z):
    return o * lax.rsqrt(jnp.mean(o * o, axis=-1, keepdims=True) + NORM_EPS) * gain * _silu(z)


def _l2n(x):
    return x * lax.rsqrt(jnp.sum(x * x, axis=-1, keepdims=True) + 1e-6)


def _split_dot_raw(m, x, kind):
    mb = m.astype(BF16)
    hi = x.astype(BF16)
    lo = (x - hi.astype(F32)).astype(BF16)
    dims = _DIMS[kind]
    return (lax.dot_general(mb, hi, dims, preferred_element_type=F32)
            + lax.dot_general(mb, lo, dims, preferred_element_type=F32))


@jax.custom_vjp
def mask_dot(m, x):
    return _split_dot_raw(m, x, "nn")


def _mask_dot_fwd(m, x):
    return _split_dot_raw(m, x, "nn"), m


def _mask_dot_bwd(m, g):
    return jnp.zeros_like(m), _split_dot_raw(m, g, "tn")


mask_dot.defvjp(_mask_dot_fwd, _mask_dot_bwd)


def _neumann_rest(low):
    n = low.shape[-1]
    rest = -low
    power = low
    span = 2
    while span < n:
        power = _dot_raw(power, power, "bnn", False)
        rest = rest + power + _dot_raw(rest, power, "bnn", False)
        span *= 2
    return rest


@jax.custom_vjp
def _unit_lower_inverse_minus_eye(low):
    return _neumann_rest(low)


def _inverse_fwd(low):
    rest = _neumann_rest(low)
    return rest, rest


def _inverse_bwd(rest, g):
    left = g + _dot_raw(rest, g, "btn", False)
    return (-(left + _dot_raw(left, rest, "bnt", False)),)


_unit_lower_inverse_minus_eye.defvjp(_inverse_fwd, _inverse_bwd)


def _gdn_step(S, q, k, v, z, small, alog, dtb, gain):
    H = S.shape[0]
    C = GDN_CHUNK
    row = lax.broadcasted_iota(jnp.int32, (C, C), 0)
    col = lax.broadcasted_iota(jnp.int32, (C, C), 1)
    tril, strict, eye = (row >= col)[None], (row > col)[None], (row == col)[None]
    head = lax.broadcasted_iota(jnp.int32, (H, 1, LANES), 0)
    lane = lax.broadcasted_iota(jnp.int32, (H, 1, LANES), 2)
    rowc = lax.broadcasted_iota(jnp.int32, (1, C, 1), 1)
    beta_all = _sigmoid(small)
    g_all = -jnp.exp(alog) * _softplus(small + dtb)
    gc_all = mask_dot((row >= col).astype(F32), g_all)
    beta = jnp.sum(jnp.where(lane == head, beta_all[None], 0.0), axis=-1, keepdims=True)
    gc = jnp.sum(jnp.where(lane == head + H, gc_all[None], 0.0), axis=-1, keepdims=True)
    gc_row = jnp.sum(jnp.where(eye, gc, 0.0), axis=1, keepdims=True)
    decay = jnp.where(tril, jnp.exp(jnp.where(tril, gc - gc_row, 0.0)), 0.0)
    g_last = jnp.sum(jnp.where(rowc == C - 1, gc, 0.0), axis=1, keepdims=True)
    qn = _l2n(q) * (HEAD ** -0.5)
    kn = _l2n(k)
    kb = kn * beta
    low = jnp.where(strict, beta * mdot(kn, kn, "bnt", False) * decay, 0.0)
    inv_rest = _unit_lower_inverse_minus_eye(low)
    eg = jnp.exp(gc)
    vb, kbe = v * beta, kb * eg
    u = vb + mdot(inv_rest, vb, "bnn", False)
    w = kbe + mdot(inv_rest, kbe, "bnn", False)
    attn = mdot(qn, kn, "bnt", False) * decay
    v_new = u - mdot(w, S, "bnn", False)
    o = mdot(qn * eg, S, "bnn", False) + mdot(attn, v_new, "bnn", False)
    k_dec = kn * jnp.exp(g_last - gc)
    return _rms_gate(o, gain, z), S * jnp.exp(g_last) + mdot(k_dec, v_new, "btn", False)


def _hgrn_step(St, qr, fr, vi, z, lb0, lb1, gain):
    H = St.shape[0]
    C, SB = HGRN_CHUNK, HGRN_SUB
    row = lax.broadcasted_iota(jnp.int32, (C, C), 0)
    col = lax.broadcasted_iota(jnp.int32, (C, C), 1)
    blk_start = row - (row & (SB - 1))
    in_blk_f = ((row >= col) & (col >= blk_start)).astype(F32)
    before_f = (col < blk_start).astype(F32)
    sums_f = jnp.concatenate([in_blk_f, before_f], axis=0)
    m = jnp.maximum(lb0, lb1)
    e0, e1 = jnp.exp(lb0 - m), jnp.exp(lb1 - m)
    lb = e1 / (e0 + e1)
    f = lb + (1.0 - lb) * _sigmoid(fr)
    q = _silu(qr)
    k = 1.0 - f
    logf = jnp.log(f)
    sums = mask_dot(sums_f, to_wide(logf))
    inner, start = to_heads(_rows(sums, 0, C), H), to_heads(_rows(sums, C, 2 * C), H)
    b = start + inner
    b_last = jnp.sum(logf, axis=1, keepdims=True)
    o = mdot(q * jnp.exp(b), St, "bnt", False)
    qt = q * jnp.exp(inner)
    parts = []
    for blk in range(C // SB):
        lo, n = blk * SB, (blk + 1) * SB
        ref = jnp.concatenate([_rows(start, lo, n)] * (blk + 1), axis=1)
        kt = _rows(k, 0, n) * jnp.exp(jnp.minimum(ref - _rows(b, 0, n), EXP_CLAMP))
        att = mdot(_rows(qt, lo, n), kt, "bnt", False)
        t_idx = lax.broadcasted_iota(jnp.int32, (1, SB, n), 1) + lo
        s_idx = lax.broadcasted_iota(jnp.int32, (1, SB, n), 2)
        att = jnp.where(s_idx <= t_idx, att, 0.0)
        parts.append(mdot(att, _rows(vi, 0, n), "bnn", False))
    o = o + jnp.concatenate(parts, axis=1)
    k_dec = k * jnp.exp(b_last - b)
    return _rms_gate(o, gain, z), St * jnp.exp(b_last) + mdot(vi, k_dec, "btn", False)


def _post_norm(s, x, g, b):
    r = ALPHA * x + s
    d = r - jnp.mean(r, axis=-1, keepdims=True)
    var = jnp.mean(d * d, axis=-1, keepdims=True)
    return d * lax.rsqrt(var + NORM_EPS) * g + b


def _post_gate(x1, gate_pre, pp):
    return x1 + pp * _sigmoid(gate_pre)


def _pick(dim, cands):
    for c in cands:
        if dim % c == 0:
            return c
    return dim


def _matmul_tiles(M, K, tn, a_bytes, b_bytes, has_add):
    for tk in (4096, 2048, 1536, 1152, 1024, 640, 512, 384, 256, 128):
        if K % tk:
            continue
        for tm in (2048, 1152, 1024, 512, 384, 256, 128):
            if M % tm:
                continue
            blocks = tm * tk * a_bytes + tk * tn * b_bytes + tm * tn * 4 * (2 if has_add else 1)
            if 2 * blocks + (tm * tn * 4 if tk < K else 0) <= MATMUL_VMEM and tm >= min(M, 1024):
                return tm, tk
    return _pick(M, (512, 256, 128)), _pick(K, (512, 256, 128))


def _matmul(a, b, kind, name, add=None, after=None):
    if kind == "nn":
        (M, K), N = a.shape, b.shape[1]
    elif kind == "nt":
        (M, K), N = a.shape, b.shape[0]
    else:
        (K, M), N = a.shape, b.shape[1]
    has_add = add is not None
    tn = _pick(N, (512, 640, 384, 256, 128))
    tm, tk = _matmul_tiles(M, K, tn, a.dtype.itemsize, b.dtype.itemsize, has_add)
    nk = K // tk
    a_spec = pl.BlockSpec((tk, tm), lambda i, j, k: (k, i)) if kind == "tn" else pl.BlockSpec((tm, tk), lambda i, j, k: (i, k))
    b_spec = pl.BlockSpec((tn, tk), lambda i, j, k: (j, k)) if kind == "nt" else pl.BlockSpec((tk, tn), lambda i, j, k: (k, j))
    o_spec = pl.BlockSpec((tm, tn), lambda i, j, k: (i, j))

    extra = ([add] if has_add else []) + ([after] if after is not None else [])
    extra_specs = ([o_spec] if has_add else []) + ([pl.BlockSpec(TOKEN_SHAPE, lambda i, j, k: (0, 0))] if after is not None else [])

    def body(a_ref, b_ref, *rest):
        add_ref = rest[0] if has_add else None
        o_ref = rest[len(extra)]
        part = _dot_raw(a_ref[...], b_ref[...], kind, False)
        if nk == 1:
            o_ref[...] = part + add_ref[...] if has_add else part
            return
        acc = rest[-1]
        kk = pl.program_id(2)

        @pl.when(kk == 0)
        def _():
            acc[...] = part

        @pl.when(kk > 0)
        def _():
            acc[...] += part

        @pl.when(kk == nk - 1)
        def _():
            o_ref[...] = acc[...] + add_ref[...] if has_add else acc[...]

    return pl.pallas_call(
        body, name=name, grid=(M // tm, N // tn, nk),
        in_specs=[a_spec, b_spec] + extra_specs,
        out_specs=o_spec, out_shape=jax.ShapeDtypeStruct((M, N), F32),
        scratch_shapes=[pltpu.VMEM((tm, tn), F32)] if nk > 1 else [],
        compiler_params=_params(dimension_semantics=("parallel", "parallel", "arbitrary")),
    )(a, b, *extra)


def _halo_specs(ts, nt, width, prev=True, main=True, nxt=True):
    per = ts // SUBLANES
    last8 = nt * per - 1
    specs = []
    if prev:
        specs.append(pl.BlockSpec((SUBLANES, width), lambda cb, i: (jnp.maximum(i * per - 1, 0), cb)))
    if main:
        specs.append(pl.BlockSpec((ts, width), lambda cb, i: (i, cb)))
    if nxt:
        specs.append(pl.BlockSpec((SUBLANES, width), lambda cb, i: (jnp.minimum((i + 1) * per, last8), cb)))
    return specs


def _taps(ext, ktaps, lo, size):
    return [ext[lo:lo + size] if j == 0 else pltpu.roll(ext, j, 0)[lo:lo + size] for j in range(ktaps)]


def _ahead(ext, j, size):
    n = ext.shape[0]
    return ext[:size] if j == 0 else pltpu.roll(ext, n - j, 0)[:size]


def _lane_block(ref, k):
    return ref[:, k * MIXER_LANES:(k + 1) * MIXER_LANES]


def _mixer_a_fwd(proj_a, conv_w):
    T = proj_a.shape[0]
    nblk = proj_a.shape[1] // (4 * MIXER_LANES)
    ts = min(ROW_TILE, T)
    nt = T // ts

    def body(pp, pm, w_ref, y_ref):
        i = pl.program_id(1)
        u_prev = jnp.where(i > 0, _lane_block(pp, 0) * _lane_block(pp, 1), 0.0)
        ext = jnp.concatenate([u_prev, _lane_block(pm, 0) * _lane_block(pm, 1)], axis=0)
        t0, t1, t2 = _taps(ext, 3, SUBLANES, ts)
        cv = w_ref[2:3, :] * t0 + w_ref[1:2, :] * t1 + w_ref[0:1, :] * t2
        y_ref[...] = (_lane_block(pm, 2) * cv * _silu(_lane_block(pm, 3))).astype(y_ref.dtype)

    return pl.pallas_call(
        body, name="mixer_a_fwd", grid=(nblk, nt),
        in_specs=_halo_specs(ts, nt, 4 * MIXER_LANES, nxt=False)
        + [pl.BlockSpec((conv_w.shape[0], MIXER_LANES), lambda cb, i: (0, cb))],
        out_specs=pl.BlockSpec((ts, MIXER_LANES), lambda cb, i: (i, cb)),
        out_shape=jax.ShapeDtypeStruct((T, nblk * MIXER_LANES), BF16), compiler_params=_params(),
    )(proj_a, proj_a, conv_w)


def _mixer_a_bwd(proj_a, dy, conv_w):
    T = proj_a.shape[0]
    nblk = proj_a.shape[1] // (4 * MIXER_LANES)
    ts = min(ROW_TILE, T)
    nt = T // ts
    kt = conv_w.shape[0]

    def body(pp, pm, pn, dym, dyn, w_ref, dp_ref, dw_ref):
        i = pl.program_id(1)
        hm, cm, bm, zm = (_lane_block(pm, k) for k in range(4))
        u_prev = jnp.where(i > 0, _lane_block(pp, 0) * _lane_block(pp, 1), 0.0)
        ext = jnp.concatenate([u_prev, hm * cm], axis=0)
        dy_ext = jnp.concatenate([dym[...], jnp.where(i < nt - 1, dyn[...], 0.0)], axis=0)
        b_ext = jnp.concatenate([bm, _lane_block(pn, 2)], axis=0)
        sz_ext = _silu(jnp.concatenate([zm, _lane_block(pn, 3)], axis=0))
        dcv_ext = dy_ext * b_ext * sz_ext
        w = [w_ref[j:j + 1, :] for j in range(kt)]
        du = sum(w[kt - 1 - j] * _ahead(dcv_ext, j, ts) for j in range(kt))
        taps = _taps(ext, kt, SUBLANES, ts)
        cv = sum(w[kt - 1 - j] * taps[j] for j in range(kt))
        for part, d in enumerate((du * cm, du * hm, dym[...] * cv * sz_ext[:ts], dym[...] * bm * cv * _dsilu(zm))):
            dp_ref[:, part * MIXER_LANES:(part + 1) * MIXER_LANES] = d.astype(dp_ref.dtype)
        dcv = dcv_ext[:ts]

        @pl.when(i == 0)
        def _():
            dw_ref[...] = jnp.zeros_like(dw_ref)

        for j in range(kt):
            dw_ref[j:j + 1, :] += jnp.sum(dcv * taps[kt - 1 - j], axis=0, keepdims=True)

    return pl.pallas_call(
        body, name="mixer_a_bwd", grid=(nblk, nt),
        in_specs=_halo_specs(ts, nt, 4 * MIXER_LANES) + _halo_specs(ts, nt, MIXER_LANES, prev=False)
        + [pl.BlockSpec((kt, MIXER_LANES), lambda cb, i: (0, cb))],
        out_specs=[pl.BlockSpec((ts, 4 * MIXER_LANES), lambda cb, i: (i, cb)),
                   pl.BlockSpec((SUBLANES, MIXER_LANES), lambda cb, i: (0, cb))],
        out_shape=[jax.ShapeDtypeStruct(proj_a.shape, BF16),
                   jax.ShapeDtypeStruct((SUBLANES, nblk * MIXER_LANES), F32)],
        compiler_params=_params(),
    )(proj_a, proj_a, proj_a, dy, dy, conv_w)


def _conv_b_fwd(raw, conv_w):
    T = raw.shape[0]
    nblk = raw.shape[1] // CONV_LANES
    ts = min(ROW_TILE, T)
    nt = T // ts
    kt = conv_w.shape[0]

    def body(rp, rm, w_ref, y_ref):
        i = pl.program_id(1)
        ext = jnp.concatenate([jnp.where(i > 0, rp[...], 0.0), rm[...]], axis=0)
        taps = _taps(ext, kt, SUBLANES, ts)
        y_ref[...] = _silu(sum(w_ref[kt - 1 - j:kt - j, :] * taps[j] for j in range(kt)))

    return pl.pallas_call(
        body, name="conv_b_fwd", grid=(nblk, nt),
        in_specs=_halo_specs(ts, nt, CONV_LANES, nxt=False) + [pl.BlockSpec((kt, CONV_LANES), lambda cb, i: (0, cb))],
        out_specs=pl.BlockSpec((ts, CONV_LANES), lambda cb, i: (i, cb)),
        out_shape=jax.ShapeDtypeStruct(raw.shape, F32), compiler_params=_params(),
    )(raw, raw, conv_w)


def _conv_b_bwd(raw, dy, conv_w):
    T = raw.shape[0]
    nblk = raw.shape[1] // CONV_LANES
    ts = min(ROW_TILE, T)
    nt = T // ts
    kt = conv_w.shape[0]

    def body(rp, rm, rn, dym, dyn, w_ref, dr_ref, dw_ref):
        i = pl.program_id(1)
        ext = jnp.concatenate([jnp.where(i > 0, rp[...], 0.0), rm[...], rn[...]], axis=0)
        w = [w_ref[j:j + 1, :] for j in range(kt)]
        taps = _taps(ext, kt, SUBLANES, ts + SUBLANES)
        xc_ext = sum(w[kt - 1 - j] * taps[j] for j in range(kt))
        dy_ext = jnp.concatenate([dym[...], jnp.where(i < nt - 1, dyn[...], 0.0)], axis=0)
        dxc_ext = dy_ext * _dsilu(xc_ext)
        dr_ref[...] = sum(w[kt - 1 - j] * _ahead(dxc_ext, j, ts) for j in range(kt)).astype(dr_ref.dtype)
        dxc = dxc_ext[:ts]

        @pl.when(i == 0)
        def _():
            dw_ref[...] = jnp.zeros_like(dw_ref)

        for j in range(kt):
            dw_ref[j:j + 1, :] += jnp.sum(dxc * taps[kt - 1 - j][:ts], axis=0, keepdims=True)

    return pl.pallas_call(
        body, name="conv_b_bwd", grid=(nblk, nt),
        in_specs=_halo_specs(ts, nt, CONV_LANES) + _halo_specs(ts, nt, CONV_LANES, prev=False)
        + [pl.BlockSpec((kt, CONV_LANES), lambda cb, i: (0, cb))],
        out_specs=[pl.BlockSpec((ts, CONV_LANES), lambda cb, i: (i, cb)),
                   pl.BlockSpec((SUBLANES, CONV_LANES), lambda cb, i: (0, cb))],
        out_shape=[jax.ShapeDtypeStruct(raw.shape, BF16), jax.ShapeDtypeStruct((SUBLANES, nblk * CONV_LANES), F32)],
        compiler_params=_params(),
    )(raw, raw, raw, dy, dy, conv_w)


def _split_heads(ref, base, nheads, rows=slice(None)):
    return jnp.stack([ref[rows, base + h * HEAD: base + (h + 1) * HEAD] for h in range(nheads)], axis=0)


def _store_heads(ref, base, x, rows=slice(None), accumulate=False):
    for h in range(x.shape[0]):
        lanes = slice(base + h * HEAD, base + (h + 1) * HEAD)
        if accumulate:
            ref[rows, lanes] += x[h]
        else:
            ref[rows, lanes] = x[h].astype(ref.dtype)


def _gdn_fwd(qkv, proj_zs, alog, dtb, gain, H):
    T = qkv.shape[0]
    C, HW = GDN_CHUNK, H * HEAD
    nc = T // C
    zw = HW + LANES

    def body(qkv_ref, zs_ref, alog_ref, dtb_ref, gain_ref, o_ref, sall_ref, s_scr):
        @pl.when(pl.program_id(0) == 0)
        def _():
            s_scr[...] = jnp.zeros_like(s_scr)

        sall_ref[0] = s_scr[...]
        outs, states = _gdn_step(
            s_scr[...], _split_heads(qkv_ref, 0, H), _split_heads(qkv_ref, HW, H),
            _split_heads(qkv_ref, 2 * HW, H), _split_heads(zs_ref, 0, H), zs_ref[:, HW:HW + LANES],
            alog_ref[...], dtb_ref[...], gain_ref[...])
        _store_heads(o_ref, 0, outs)
        s_scr[...] = states

    row = pl.BlockSpec((1, LANES), lambda i: (0, 0))
    return pl.pallas_call(
        body, name="gdn_fwd", grid=(nc,),
        in_specs=[pl.BlockSpec((C, 3 * HW), lambda i: (i, 0)), pl.BlockSpec((C, zw), lambda i: (i, 0)), row, row, row],
        out_specs=[pl.BlockSpec((C, HW), lambda i: (i, 0)), pl.BlockSpec((1, H, HEAD, HEAD), lambda i: (i, 0, 0, 0))],
        out_shape=[jax.ShapeDtypeStruct((T, HW), BF16), jax.ShapeDtypeStruct((nc, H, HEAD, HEAD), F32)],
        scratch_shapes=[pltpu.VMEM((H, HEAD, HEAD), F32)], compiler_params=_params(),
    )(qkv, proj_zs, alog, dtb, gain)


def _gdn_bwd(qkv, proj_zs, do, s_all, alog, dtb, gain, H):
    T = qkv.shape[0]
    C, HW = GDN_CHUNK, H * HEAD
    nc = T // C
    zw = HW + LANES

    def body(qkv_ref, zs_ref, do_ref, sin_ref, alog_ref, dtb_ref, gain_ref,
             dqkv_ref, dzs_ref, dalog_ref, ddtb_ref, dgain_ref, ds_scr):
        @pl.when(pl.program_id(0) == 0)
        def _():
            ds_scr[...] = jnp.zeros_like(ds_scr)
            dalog_ref[...] = jnp.zeros_like(dalog_ref)
            ddtb_ref[...] = jnp.zeros_like(ddtb_ref)
            dgain_ref[...] = jnp.zeros_like(dgain_ref)

        primals = (sin_ref[0], _split_heads(qkv_ref, 0, H),
                   _split_heads(qkv_ref, HW, H), _split_heads(qkv_ref, 2 * HW, H), _split_heads(zs_ref, 0, H),
                   zs_ref[:, HW:HW + LANES], alog_ref[...], dtb_ref[...], gain_ref[...])
        _, vjp = jax.vjp(_gdn_step, *primals)
        dS, dq, dk, dv, dz, dsmall, dalog, ddtb, dgain = vjp((_split_heads(do_ref, 0, H), ds_scr[...]))
        ds_scr[...] = dS
        _store_heads(dqkv_ref, 0, dq)
        _store_heads(dqkv_ref, HW, dk)
        _store_heads(dqkv_ref, 2 * HW, dv)
        _store_heads(dzs_ref, 0, dz)
        dzs_ref[:, HW:HW + LANES] = dsmall.astype(dzs_ref.dtype)
        dalog_ref[...] += dalog
        ddtb_ref[...] += ddtb
        dgain_ref[...] += dgain

    row = pl.BlockSpec((1, LANES), lambda i: (0, 0))
    rev = lambda i: nc - 1 - i
    return pl.pallas_call(
        body, name="gdn_bwd", grid=(nc,),
        in_specs=[pl.BlockSpec((C, 3 * HW), lambda i: (rev(i), 0)), pl.BlockSpec((C, zw), lambda i: (rev(i), 0)),
                  pl.BlockSpec((C, HW), lambda i: (rev(i), 0)),
                  pl.BlockSpec((1, H, HEAD, HEAD), lambda i: (rev(i), 0, 0, 0)), row, row, row],
        out_specs=[pl.BlockSpec((C, 3 * HW), lambda i: (rev(i), 0)), pl.BlockSpec((C, zw), lambda i: (rev(i), 0)),
                   row, row, row],
        out_shape=[jax.ShapeDtypeStruct(qkv.shape, F32), jax.ShapeDtypeStruct(proj_zs.shape, BF16)]
        + [jax.ShapeDtypeStruct((1, LANES), F32)] * 3,
        scratch_shapes=[pltpu.VMEM((H, HEAD, HEAD), F32)], compiler_params=_params(),
    )(qkv, proj_zs, do, s_all, alog, dtb, gain)


def _hgrn_refs(proj_ref, lb_ref, HP):
    W = HP * HEAD
    return (_split_heads(proj_ref, 0, HP), _split_heads(proj_ref, W, HP), _split_heads(proj_ref, 2 * W, HP),
            _split_heads(proj_ref, 3 * W, HP), _split_heads(lb_ref, 0, HP, slice(0, 1)),
            _split_heads(lb_ref, 0, HP, slice(1, 2)))


def _hgrn_fwd(proj, lower_bounds, gain, nheads):
    T = proj.shape[0]
    C, HP = HGRN_CHUNK, HGRN_HEADS_PER_STEP
    ng, nc, W = nheads // HP, T // C, HP * HEAD

    def body(proj_ref, lb_ref, gain_ref, o_ref, sall_ref, s_scr):
        @pl.when(pl.program_id(1) == 0)
        def _():
            s_scr[...] = jnp.zeros_like(s_scr)

        sall_ref[0] = s_scr[...]
        qr, fr, vi, z, lb0, lb1 = _hgrn_refs(proj_ref, lb_ref, HP)
        outs, states = _hgrn_step(s_scr[...], qr, fr, vi, z, lb0, lb1, gain_ref[...])
        _store_heads(o_ref, 0, outs)
        s_scr[...] = states

    return pl.pallas_call(
        body, name="hgrn_fwd", grid=(ng, nc),
        in_specs=[pl.BlockSpec((C, 4 * W), lambda g, i: (i, g)), pl.BlockSpec((2, W), lambda g, i: (0, g)),
                  pl.BlockSpec((1, LANES), lambda g, i: (0, 0))],
        out_specs=[pl.BlockSpec((C, W), lambda g, i: (i, g)),
                   pl.BlockSpec((1, HP, HEAD, HEAD), lambda g, i: (i, g, 0, 0))],
        out_shape=[jax.ShapeDtypeStruct((T, nheads * HEAD), BF16), jax.ShapeDtypeStruct((nc, nheads, HEAD, HEAD), F32)],
        scratch_shapes=[pltpu.VMEM((HP, HEAD, HEAD), F32)], compiler_params=_params(),
    )(proj, lower_bounds, gain)


def _hgrn_bwd(proj, do, s_all, lower_bounds, gain, nheads):
    T = proj.shape[0]
    C, HP = HGRN_CHUNK, HGRN_HEADS_PER_STEP
    ng, nc, W = nheads // HP, T // C, HP * HEAD

    def body(proj_ref, do_ref, sin_ref, lb_ref, gain_ref, dproj_ref, dlb_ref, dgain_ref, ds_scr):
        first = pl.program_id(1) == 0

        @pl.when(first)
        def _():
            ds_scr[...] = jnp.zeros_like(ds_scr)
            dlb_ref[...] = jnp.zeros_like(dlb_ref)

        @pl.when(first & (pl.program_id(0) == 0))
        def _():
            dgain_ref[...] = jnp.zeros_like(dgain_ref)

        qr, fr, vi, z, lb0, lb1 = _hgrn_refs(proj_ref, lb_ref, HP)
        primals = (sin_ref[0], qr, fr, vi, z, lb0, lb1, gain_ref[...])
        _, vjp = jax.vjp(_hgrn_step, *primals)
        dS, dq, df, dv, dz, dlb0, dlb1, dgain = vjp((_split_heads(do_ref, 0, HP), ds_scr[...]))
        ds_scr[...] = dS
        for part, d in enumerate((dq, df, dv, dz)):
            _store_heads(dproj_ref, part * W, d)
        _store_heads(dlb_ref, 0, dlb0, slice(0, 1), accumulate=True)
        _store_heads(dlb_ref, 0, dlb1, slice(1, 2), accumulate=True)
        dgain_ref[...] += dgain

    rev = lambda i: nc - 1 - i
    return pl.pallas_call(
        body, name="hgrn_bwd", grid=(ng, nc),
        in_specs=[pl.BlockSpec((C, 4 * W), lambda g, i: (rev(i), g)), pl.BlockSpec((C, W), lambda g, i: (rev(i), g)),
                  pl.BlockSpec((1, HP, HEAD, HEAD), lambda g, i: (rev(i), g, 0, 0)),
                  pl.BlockSpec((2, W), lambda g, i: (0, g)), pl.BlockSpec((1, LANES), lambda g, i: (0, 0))],
        out_specs=[pl.BlockSpec((C, 4 * W), lambda g, i: (rev(i), g)), pl.BlockSpec((2, W), lambda g, i: (0, g)),
                   pl.BlockSpec((1, LANES), lambda g, i: (0, 0))],
        out_shape=[jax.ShapeDtypeStruct(proj.shape, BF16), jax.ShapeDtypeStruct(lower_bounds.shape, F32),
                   jax.ShapeDtypeStruct((1, LANES), F32)],
        scratch_shapes=[pltpu.VMEM((HP, HEAD, HEAD), F32)], compiler_params=_params(),
    )(proj, do, s_all, lower_bounds, gain)


def _post_specs(T):
    tr = min(POST_TILE, T)
    tile = lambda w: pl.BlockSpec((tr, w), lambda i: (i, 0))
    full = lambda r, w: pl.BlockSpec((r, w), lambda i: (0, 0))
    return tr, tile, full


def _post_fwd(s, x, p, g, b, wg, wpl, name):
    T, D = x.shape
    P = p.shape[1]
    tr, tile, full = _post_specs(T)

    def body(s_ref, x_ref, p_ref, g_ref, b_ref, wg_ref, wpl_ref, o_ref, o16_ref):
        x1 = _post_norm(s_ref[...], x_ref[...], g_ref[...], b_ref[...])
        xn = _post_gate(x1, _dot_raw(x1, wg_ref[...], "nn", False), _dot_raw(p_ref[...], wpl_ref[...], "nn", False))
        o_ref[...] = xn
        o16_ref[...] = xn.astype(BF16)

    return pl.pallas_call(
        body, name=name, grid=(T // tr,),
        in_specs=[tile(D), tile(D), tile(P), full(1, D), full(1, D), full(D, D), full(P, D)],
        out_specs=[tile(D), tile(D)],
        out_shape=[jax.ShapeDtypeStruct((T, D), F32), jax.ShapeDtypeStruct((T, D), BF16)], compiler_params=_params(),
    )(s, x, p, g, b, wg, wpl)


def _post_bwd(s, x, p, g, b, wg, wpl, dnext, name, with_loss):
    T, D = x.shape
    P = p.shape[1]
    tr, tile, full = _post_specs(T)

    def body(s_ref, x_ref, p_ref, g_ref, b_ref, wg_ref, wpl_ref, dn_ref,
             ds_ref, dx_ref, dg_ref, db_ref, dwg_ref, dwpl_ref, loss_ref):
        @pl.when(pl.program_id(0) == 0)
        def _():
            for r in (dg_ref, db_ref, dwg_ref, dwpl_ref, loss_ref):
                r[...] = jnp.zeros_like(r)

        x1, vjp_norm = jax.vjp(_post_norm, s_ref[...], x_ref[...], g_ref[...], b_ref[...])
        gate_pre = _dot_raw(x1, wg_ref[...], "nn", False)
        pp = _dot_raw(p_ref[...], wpl_ref[...], "nn", False)
        xn, vjp_gate = jax.vjp(_post_gate, x1, gate_pre, pp)
        if with_loss:
            err = xn - dn_ref[...]
            loss_ref[...] += 0.5 * jnp.sum(jnp.sum(err * err, axis=-1, keepdims=True), axis=0, keepdims=True) / D
            dn = err / D
        else:
            dn = dn_ref[...]
        dx1, dgp, dpp = vjp_gate(dn)
        dwg_ref[...] += _dot_raw(x1, dgp, "tn", False)
        dwpl_ref[...] += _dot_raw(p_ref[...], dpp, "tn", False)
        dx1 = dx1 + _dot_raw(dgp, wg_ref[...], "nt", False)
        ds, dx, dg, db = vjp_norm(dx1)
        ds_ref[...] = ds.astype(ds_ref.dtype)
        dx_ref[...] = dx
        dg_ref[...] += dg
        db_ref[...] += db

    return pl.pallas_call(
        body, name=name, grid=(T // tr,),
        in_specs=[tile(D), tile(D), tile(P), full(1, D), full(1, D), full(D, D), full(P, D), tile(D)],
        out_specs=[tile(D), tile(D), full(1, D), full(1, D), full(D, D), full(P, D), full(SUBLANES, LANES)],
        out_shape=[jax.ShapeDtypeStruct((T, D), BF16), jax.ShapeDtypeStruct((T, D), F32)]
        + [jax.ShapeDtypeStruct((1, D), F32)] * 2
        + [jax.ShapeDtypeStruct((D, D), F32), jax.ShapeDtypeStruct((P, D), F32),
           jax.ShapeDtypeStruct((SUBLANES, LANES), F32)],
        compiler_params=_params(),
    )(s, x, p, g, b, wg, wpl, dnext)


def _adam_math(w, g, m, v):
    m = ADAM_B1 * m + (1.0 - ADAM_B1) * g
    v = ADAM_B2 * v + (1.0 - ADAM_B2) * (g * g)
    m_hat = m / (1.0 - ADAM_B1 ** ADAM_STEP)
    v_hat = v / (1.0 - ADAM_B2 ** ADAM_STEP)
    return -ADAM_LR * (m_hat / (jnp.sqrt(v_hat) + ADAM_EPS) + ADAM_WD * w), m, v


def _shard_tiles(R, C):
    tr = _pick(R, (256, 128, 64, 32, 16, 8))
    return (tr, C) if tr < R or R % SUBLANES == 0 else (R, _pick(C, (256, 128)))


def _rs_add(g8, got, cidx, name):
    R, C = _owner_blocks(g8)
    tr, tc = _shard_tiles(R, C)
    nc = C // tc

    def body(c_ref, a_ref, b_ref, o_ref, o16_ref):
        total = (a_ref[0] if g8.ndim == 3 else a_ref[...]) + b_ref[0]
        o_ref[0] = total
        o16_ref[0] = total.astype(BF16)

    if g8.ndim == 3:
        mine_spec = pl.BlockSpec((1, tr, tc), lambda q, i, j, c: (2 * q + c[0], i, j))
    else:
        mine_spec = pl.BlockSpec((tr, tc), lambda q, i, j, c: (i, (2 * q + c[0]) * nc + j))
    out_spec = pl.BlockSpec((1, tr, tc), lambda q, i, j, c: (q, i, j))
    return pl.pallas_call(
        body, name=name,
        grid_spec=pltpu.PrefetchScalarGridSpec(
            num_scalar_prefetch=1, grid=(4, R // tr, nc),
            in_specs=[mine_spec, out_spec], out_specs=[out_spec, out_spec]),
        out_shape=[jax.ShapeDtypeStruct((4, R, C), F32), jax.ShapeDtypeStruct((4, R, C), BF16)],
        compiler_params=_params(),
    )(cidx, g8, got)


def _adam_sharded(w, m, v, mine, got, qidx, name):
    rows_apart = w.shape[1] == 1 and w.shape[0] > 1
    (L, R, C) = (1, w.shape[0], w.shape[2]) if rows_apart else w.shape
    tr, tc = (R, LANES) if rows_apart else _shard_tiles(R, C)
    nr = R // tr

    def body(q_ref, w_ref, m_ref, v_ref, p_ref, r0, r1, r2, g_ref, d_ref, mo_ref, vo_ref):
        g = ((p_ref[0] + r0[0].astype(F32)) + r1[0].astype(F32)) + r2[0].astype(F32)
        if rows_apart:
            d, mn, vn = _adam_math(w_ref[:, 0, :], g, m_ref[:, 0, :], v_ref[:, 0, :])
            for ref, val in ((g_ref, g), (d_ref, d), (mo_ref, mn), (vo_ref, vn)):
                ref[:, 0, :] = val
            return
        d, mn, vn = _adam_math(w_ref[0], g, m_ref[0], v_ref[0])
        g_ref[0] = g
        d_ref[0] = d
        mo_ref[0] = mn
        vo_ref[0] = vn

    if rows_apart:
        t3 = pl.BlockSpec((tr, 1, tc), lambda l, i, j, q: (i, 0, j))
    else:
        t3 = pl.BlockSpec((1, tr, tc), lambda l, i, j, q: (l, i, j))
    slot = lambda k: pl.BlockSpec((1, tr, tc), lambda l, i, j, q: (k, l * nr + i, j))
    return pl.pallas_call(
        body, name=name,
        grid_spec=pltpu.PrefetchScalarGridSpec(
            num_scalar_prefetch=1, grid=(L, nr, C // tc),
            in_specs=[t3, t3, t3, pl.BlockSpec((1, tr, tc), lambda l, i, j, q: (q[0], l * nr + i, j)),
                      slot(0), slot(1), slot(2)],
            out_specs=[t3, t3, t3, t3]),
        out_shape=[jax.ShapeDtypeStruct(w.shape, F32)] * 4, compiler_params=_params(),
    )(qidx, w, m, v, mine, got, got, got)


def _adam_replicated(w, m, v, g8):
    def body(w_ref, m_ref, v_ref, g_ref, go_ref, d_ref, mo_ref, vo_ref):
        g = g_ref[0]
        for k in range(1, 8):
            g = g + g_ref[k]
        d, mn, vn = _adam_math(w_ref[...], g, m_ref[...], v_ref[...])
        go_ref[...] = g
        d_ref[...] = d
        mo_ref[...] = mn
        vo_ref[...] = vn

    return pl.pallas_call(
        body, name="adam_replicated", out_shape=[jax.ShapeDtypeStruct(w.shape, F32)] * 4, compiler_params=_params(),
    )(w, m, v, g8)


def _place():
    return lax.axis_index("x"), lax.axis_index("y"), lax.axis_index("c")


def _all_gather(shard, name):
    def body(x_ref, out_ref, send_sems, recv_sems, local_sem):
        x, y, c = _place()
        me, sibling = (x, y, c), (x, y, 1 - c)
        chips = [(1 - x, y), (x, 1 - y), (1 - x, 1 - y)]

        def slab(px, py, pc):
            return out_ref.at[4 * px + 2 * py + pc]

        def copy(k, block, to, src=None):
            return pltpu.make_async_remote_copy(
                src_ref=slab(*block) if src is None else src, dst_ref=slab(*block),
                send_sem=send_sems.at[k], recv_sem=recv_sems.at[k], device_id=to, device_id_type=MESH)

        mine = pltpu.make_async_copy(x_ref, slab(*me), local_sem)
        mine.start()
        first = [copy(0, me, sibling, src=x_ref)]
        first += [copy(1 + j, me, (*chip, c), src=x_ref) for j, chip in enumerate(chips)]
        for cp in first:
            cp.start()
        passed = [copy(4 + j, (*chip, c), sibling) for j, chip in enumerate(chips)]
        for j, chip in enumerate(chips):
            copy(1 + j, (*chip, c), me).wait_recv()
            passed[j].start()
        copy(0, sibling, me).wait_recv()
        for j, chip in enumerate(chips):
            copy(4 + j, (*chip, 1 - c), me).wait_recv()
        for cp in first + passed:
            cp.wait_send()
        mine.wait()

    return pl.pallas_call(
        body, name=name, out_shape=jax.ShapeDtypeStruct((8,) + shard.shape, shard.dtype),
        in_specs=[pl.BlockSpec(memory_space=pl.ANY)], out_specs=pl.BlockSpec(memory_space=pl.ANY),
        scratch_shapes=[pltpu.SemaphoreType.DMA((7,)), pltpu.SemaphoreType.DMA((7,)), pltpu.SemaphoreType.DMA],
    )(shard)


def _owner_blocks(g8):
    return g8.shape[1:] if g8.ndim == 3 else (g8.shape[0], g8.shape[1] // 8)


def _rs_to_sibling(g8, name):
    R, C = _owner_blocks(g8)

    def body(g_ref, out_ref, send_sems, recv_sems):
        x, y, c = _place()
        block = (lambda k: g_ref.at[k]) if g8.ndim == 3 else (lambda k: g_ref.at[:, pl.ds(k * C, C)])
        copies = [pltpu.make_async_remote_copy(
            src_ref=block(2 * q + (1 - c)), dst_ref=out_ref.at[q], send_sem=send_sems.at[q],
            recv_sem=recv_sems.at[q], device_id=(x, y, 1 - c), device_id_type=MESH) for q in range(4)]
        for cp in copies:
            cp.start()
        for cp in copies:
            cp.wait()

    return pl.pallas_call(
        body, name=name, out_shape=jax.ShapeDtypeStruct((4, R, C), g8.dtype),
        in_specs=[pl.BlockSpec(memory_space=pl.ANY)], out_specs=pl.BlockSpec(memory_space=pl.ANY),
        scratch_shapes=[pltpu.SemaphoreType.DMA((4,)), pltpu.SemaphoreType.DMA((4,))],
    )(g8)


_HBM = pl.BlockSpec(memory_space=pltpu.HBM)
_SEM = pl.BlockSpec(memory_space=pltpu.SEMAPHORE)
_DATAFLOW = pltpu.SideEffectType.DATAFLOW_SIDE_EFFECTING
TOKEN_SHAPE = (SUBLANES, LANES)


def _chip_plan(x, y, c):
    return [(2 * px + py, j, (px, py, c)) for j, (px, py) in enumerate([(1 - x, y), (x, 1 - y), (1 - x, 1 - y)])]


def _exchange_copies(plan, src_ref, land_ref, send_sems, recv_sems):
    return [pltpu.make_async_remote_copy(
        src_ref=src_ref.at[blk], dst_ref=land_ref.at[slot], send_sem=send_sems.at[k], recv_sem=recv_sems.at[k],
        device_id=peer, device_id_type=MESH) for k, (blk, slot, peer) in enumerate(plan(*_place()))]


def _exchange_start(src, n_slots, plan, name):
    land_shape = (n_slots,) + src.shape[1:]
    n = len(plan(0, 0, 0))

    def body(src_ref, land_ref, send_sems, recv_sems, src_thru, land_thru, token):
        for cp in _exchange_copies(plan, src_ref, land_ref, send_sems, recv_sems):
            cp.start()
        token[...] = jnp.zeros_like(token)

    return pl.pallas_call(
        body, name=name,
        out_shape=(pltpu.SemaphoreType.DMA((n,)), pltpu.SemaphoreType.DMA((n,)), pltpu.HBM(src.shape, src.dtype),
                   pltpu.HBM(land_shape, src.dtype), jax.ShapeDtypeStruct(TOKEN_SHAPE, F32)),
        in_specs=(_HBM, _HBM), out_specs=(_SEM, _SEM, _HBM, _HBM, pl.BlockSpec(memory_space=pltpu.VMEM)),
        input_output_aliases={0: 2, 1: 3}, compiler_params=pltpu.CompilerParams(has_side_effects=_DATAFLOW),
    )(pltpu.with_memory_space_constraint(src, pltpu.HBM),
      pltpu.with_memory_space_constraint(lax.empty(land_shape, src.dtype), pltpu.HBM))


def _exchange_wait(handle, plan, after, name):
    send_sems, recv_sems, src_thru, land_thru, _ = handle

    def body(src_ref, land_ref, send_sems, recv_sems, after_ref, src_dead, got_ref):
        for cp in _exchange_copies(plan, src_ref, land_ref, send_sems, recv_sems):
            cp.wait_send()
            cp.wait_recv()

    return pl.pallas_call(
        body, name=name,
        out_shape=(pltpu.HBM(src_thru.shape, src_thru.dtype), pltpu.HBM(land_thru.shape, land_thru.dtype)),
        in_specs=(_HBM, _HBM, _SEM, _SEM, pl.BlockSpec(memory_space=pl.ANY)), out_specs=(_HBM, _HBM),
        input_output_aliases={0: 0, 1: 1}, compiler_params=pltpu.CompilerParams(has_side_effects=_DATAFLOW),
    )(src_thru, land_thru, send_sems, recv_sems, after)[1]


def _gather_plan(x, y, c):
    me = 4 * x + 2 * y + c
    flip = lambda v, bit: 1 - v if bit else v
    return [(0, me, (flip(x, r >> 2 & 1), flip(y, r >> 1 & 1), flip(c, r & 1))) for r in range(1, 8)]


class _LateGather:
    def __init__(self, shard, name):
        self.shard, self.name = shard, name
        self.handle = _exchange_start(shard[None], 8, _gather_plan, name + "_start")

    def get(self, after):
        land = _exchange_wait(self.handle, _gather_plan, after, self.name + "_wait")
        x, y, c = _place()
        return lax.dynamic_update_slice(land, self.shard[None], (4 * x + 2 * y + c, 0, 0))


class _GradExchange:
    def __init__(self, cidx, qidx, layouts):
        self.cidx, self.qidx, self.layouts, self.pending = cidx, qidx, layouts, {}

    def start(self, tag, grad):
        g8 = self.layouts[tag](grad)
        got = _rs_to_sibling(g8, "rs_sibling_" + tag)
        chip_sums, chip_sums16 = _rs_add(g8, got, self.cidx, "rs_add_" + tag)
        handle = _exchange_start(chip_sums16, 3, _chip_plan, "rs_chips_start_" + tag)
        self.pending[tag] = (chip_sums, handle)
        return handle[4]

    def finish(self, tag, w, m, v, after):
        chip_sums, handle = self.pending.pop(tag)
        got2 = _exchange_wait(handle, _chip_plan, after, "rs_chips_wait_" + tag)
        return _adam_sharded(w, m, v, chip_sums, got2, self.qidx, "adam_" + tag)


def _local_grads(x, p0, p1, target, wt_zs, wt_a, wt_qkv, late, conv_a, conv_b,
                 a_log, dt_bias, gdn_gain, lower_bounds, hgrn_gain, ln_g, ln_b, on_grad=None):
    H = a_log.shape[1]
    pad_small = ((0, 0), (H, LANES - 2 * H))
    alog_row = jnp.pad(a_log, pad_small)
    dtb_row = jnp.pad(dt_bias, pad_small)

    x16 = x.astype(BF16)
    proj_zs = _matmul(x16, wt_zs, "nt", "proj_even_zs", after=late.started)
    proj_a = _matmul(x16, wt_a, "nt", "proj_even_a", after=late.started)
    proj_qkv = _matmul(x16, wt_qkv, "nt", "proj_even_qkv", after=late.started)
    y_a = _mixer_a_fwd(proj_a, conv_a)
    qkv = _conv_b_fwd(proj_qkv, conv_b)
    o2, s_gdn = _gdn_fwd(qkv, proj_zs, alog_row, dtb_row, gdn_gain, H)
    woute_a, woute_b = late.out_even(o2)
    wg, wpl = late.gate(o2)
    s_e = _matmul(o2, woute_b, "nn", "out_even_b", add=_matmul(y_a, woute_a, "nn", "out_even_a"))
    x2, x2_16 = _post_fwd(s_e, x, p0, ln_g[0:1], ln_b[0:1], wg[0], wpl[0], "post_even_fwd")
    wino, wouto = late.odd(s_e)
    nheads_o = wouto.shape[0] // HEAD
    proj_o = _matmul(x2_16, wino, "nn", "proj_odd")
    o4, s_hgrn = _hgrn_fwd(proj_o, lower_bounds, hgrn_gain, nheads_o)
    s_o = _matmul(o4, wouto, "nn", "out_odd")
    ds_o, dx2, dlng1, dlnb1, dwg1, dwpl1, loss = _post_bwd(
        s_o, x2, p1, ln_g[1:2], ln_b[1:2], wg[1], wpl[1], target, "post_odd_loss_bwd", True)
    do4 = _matmul(ds_o, wouto, "nt", "d_out_odd_act")
    grads = {}

    def emit(tag, grad):
        grads[tag] = grad
        return on_grad(tag, grad) if on_grad is not None else jnp.zeros(TOKEN_SHAPE, F32)

    tok = emit("w_out_odd", _matmul(o4, ds_o, "tn", "d_out_odd_w"))
    dproj_o, dlb, dhgain = _hgrn_bwd(proj_o, do4, s_hgrn, lower_bounds, hgrn_gain + tok[0:1], nheads_o)
    dx2 = _matmul(dproj_o, wino, "nt", "d_proj_odd_act", add=dx2)
    tok = emit("w_in_odd", _matmul(x2_16, dproj_o, "tn", "d_proj_odd_w"))
    ds_e, dx, dlng0, dlnb0, dwg0, dwpl0, _ = _post_bwd(
        s_e, x, p0, ln_g[0:1], ln_b[0:1] + tok[0:1, 0:1], wg[0], wpl[0], dx2, "post_even_bwd", False)
    tok = emit("w_pl_gate", jnp.stack([dwg0, dwg1])) + emit("w_pl", jnp.stack([dwpl0, dwpl1]))
    dy_a = _matmul(ds_e, woute_a, "nt", "d_out_even_a_act")
    do2 = _matmul(ds_e, woute_b, "nt", "d_out_even_b_act")
    dwoute_a = _matmul(y_a, ds_e, "tn", "d_out_even_a_w")
    dwoute_b = _matmul(o2, ds_e, "tn", "d_out_even_b_w")
    tok = tok + emit("w_out_even", jnp.concatenate([dwoute_a, dwoute_b], axis=0))
    dqkv, dproj_zs, dalog, ddtb, dggain = _gdn_bwd(qkv, proj_zs, do2, s_gdn, alog_row, dtb_row, gdn_gain + tok[0:1], H)
    dproj_qkv, dconv_b = _conv_b_bwd(proj_qkv, dqkv, conv_b)
    dproj_a, dconv_a = _mixer_a_bwd(proj_a, dy_a, conv_a)
    emit("conv", (dconv_a[:conv_a.shape[0]], dconv_b[:conv_b.shape[0]]))
    tok = emit("w_in_even", (_matmul(dproj_zs, x16, "tn", "d_proj_even_zs_w"), _matmul(dproj_a, x16, "tn", "d_proj_even_a_w"),
                             _matmul(dproj_qkv, x16, "tn", "d_proj_even_qkv_w")))
    dx = _matmul(dproj_zs, wt_zs, "nn", "d_proj_even_zs_act", add=dx, after=tok)
    dx = _matmul(dproj_a, wt_a, "nn", "d_proj_even_a_act", add=dx)
    dx = _matmul(dproj_qkv, wt_qkv, "nn", "d_proj_even_qkv_act", add=dx)
    grads.update(
        loss=loss[0, 0], grad_x=dx, a_log=dalog[:, H:2 * H], dt_bias=ddtb[:, H:2 * H], gdn_gain=dggain,
        lower_bounds=dlb, hgrn_gain=dhgain, ln_g=jnp.concatenate([dlng0, dlng1], axis=0),
        ln_b=jnp.concatenate([dlnb0, dlnb1], axis=0))
    return grads


def _pad_rows(a, rows):
    return jnp.pad(a, ((0, rows - a.shape[0]), (0, 0)))


def _pack_small(a_log, dt_bias, gdn_gain, lower_bounds, hgrn_gain, ln_g, ln_b):
    lane_pad = lambda a: _pad_rows(jnp.pad(a, ((0, 0), (0, LANES - a.shape[1]))), SUBLANES)
    parts = [lane_pad(a_log), lane_pad(dt_bias), lane_pad(gdn_gain), lower_bounds.reshape(-1, LANES),
             lane_pad(hgrn_gain), ln_g.reshape(-1, LANES), ln_b.reshape(-1, LANES)]
    packed = jnp.concatenate(parts, axis=0)
    assert packed.shape[0] == SMALL_ROWS, packed.shape
    return packed


def _unpack_small(packed, shapes):
    out, r = [], 0
    for shp in shapes:
        n = shp[0] * shp[1]
        if n < LANES * SUBLANES and shp[1] <= LANES:
            out.append(packed[r:r + shp[0], :shp[1]])
            r += SUBLANES
        else:
            rows = n // LANES
            out.append(packed[r:r + rows].reshape(shp))
            r += rows
    return out


def _split_in_even(wt_full, AW, HW, H):
    D = wt_full.shape[1]
    n_a = 4 * AW
    n_main = n_a + 3 * HW
    wt_zs = jnp.concatenate([wt_full[n_main:n_main + HW], wt_full[n_main + HW:],
                             jnp.zeros((LANES - 2 * H, D), wt_full.dtype)], axis=0)
    wt_a = wt_full[:n_a].reshape(4, AW // MIXER_LANES, MIXER_LANES, D).transpose(1, 0, 2, 3).reshape(n_a, D)
    return wt_zs, wt_a, wt_full[n_a:n_main]


def _join_in_even(dt_zs, dt_a, dt_qkv, AW, HW, H):
    D = dt_a.shape[1]
    a_nat = dt_a.reshape(AW // MIXER_LANES, 4, MIXER_LANES, D).transpose(1, 0, 2, 3).reshape(4 * AW, D)
    return jnp.concatenate([a_nat, dt_qkv, dt_zs[:HW], dt_zs[HW:HW + 2 * H]], axis=0)


def kernel(x, p, w_in_even, conv_a_w, conv_b_w, a_log, dt_bias, gdn_norm_g, w_out_even, w_in_odd, lower_bounds, hgrn_norm_g, w_out_odd, ln_g, ln_b, w_pl, w_pl_gate, loss_target, m_w_in_even, m_conv_a_w, m_conv_b_w, m_a_log, m_dt_bias, m_gdn_norm_g, m_w_out_even, m_w_in_odd, m_lower_bounds, m_hgrn_norm_g, m_w_out_odd, m_ln_g, m_ln_b, m_w_pl, m_w_pl_gate, v_w_in_even, v_conv_a_w, v_conv_b_w, v_a_log, v_dt_bias, v_gdn_norm_g, v_w_out_even, v_w_in_odd, v_lower_bounds, v_hgrn_norm_g, v_w_out_odd, v_ln_g, v_ln_b, v_w_pl, v_w_pl_gate):
    xi, yi, ci = _place()
    cidx = jnp.reshape(ci, (1,)).astype(jnp.int32)
    qidx = jnp.reshape(2 * xi + yi, (1,)).astype(jnp.int32)
    D = x.shape[2]
    H = a_log.shape[1]
    HW = H * HEAD
    AW = conv_a_w.shape[2] * 8
    OW = w_out_odd.shape[1] * 8
    PD = w_pl.shape[1]
    ka, kb = conv_a_w.shape[1], conv_b_w.shape[1]
    ca, cb = conv_a_w.shape[2], conv_b_w.shape[2]
    gw = HGRN_HEADS_PER_STEP * HEAD
    ngrp = OW // gw

    transposed = lambda a: jnp.transpose(a, (0, 2, 1))
    g_ine = _all_gather(transposed(w_in_even)[0].astype(BF16), "ag_w_in_even")
    wt_zs, wt_a, wt_qkv = _split_in_even(g_ine.reshape(-1, D), AW, HW, H)
    behind = lambda shard, dep: lax.optimization_barrier((shard, dep))[0]
    late_oute = _LateGather(behind(w_out_even[0].astype(BF16), g_ine), "ag_w_out_even")
    late_gate = _LateGather(behind(w_pl_gate.astype(BF16).reshape(-1, D), late_oute.handle[4]), "ag_w_pl_gate")
    late_pl = _LateGather(behind(w_pl.astype(BF16).reshape(DEPTH * PD, -1), late_gate.handle[4]), "ag_w_pl")
    late_ino = _LateGather(behind(w_in_odd[0].astype(BF16), late_pl.handle[4]), "ag_w_in_odd")
    late_outo = _LateGather(behind(w_out_odd[0].astype(BF16), late_ino.handle[4]), "ag_w_out_odd")

    class _Late:
        started = sum(g.handle[4] for g in (late_oute, late_gate, late_pl, late_ino, late_outo))

        @staticmethod
        def out_even(after):
            woute = late_oute.get(after).reshape(-1, D)
            return woute[:AW], woute[AW:]

        @staticmethod
        def gate(after):
            g_gate, g_pl = late_gate.get(after), late_pl.get(after)
            return (g_gate.reshape(8, DEPTH, D // 8, D).transpose(1, 0, 2, 3).reshape(DEPTH, D, D),
                    g_pl.reshape(8, DEPTH, PD, D // 8).transpose(1, 2, 0, 3).reshape(DEPTH, PD, D))

        @staticmethod
        def odd(after):
            g_ino = late_ino.get(after)
            wino = jnp.transpose(g_ino, (1, 0, 2)).reshape(D, 4, ngrp, gw).transpose(0, 2, 1, 3).reshape(D, 4 * OW)
            return wino, late_outo.get(after).reshape(-1, D)

    taps = jnp.concatenate([_pad_rows(conv_a_w[0], SUBLANES), _pad_rows(conv_b_w[0], SUBLANES)], axis=1)
    g_taps = _all_gather(taps, "ag_conv")
    conv_a = jnp.transpose(g_taps[:, :ka, :ca], (1, 0, 2)).reshape(ka, 8 * ca)
    conv_b = jnp.transpose(g_taps[:, :kb, ca:], (1, 0, 2)).reshape(kb, 8 * cb)

    sh = w_in_even.shape[2]
    tap_blocks = lambda g, width: _pad_rows(g, SUBLANES).reshape(SUBLANES, 8, width).transpose(1, 0, 2)
    owner_layout = {
        "w_in_even": lambda g: _join_in_even(*g, AW, HW, H).reshape(8, sh, D),
        "w_in_odd": lambda g: g.reshape(D, ngrp, 4, gw).transpose(0, 2, 1, 3).reshape(D, 4 * OW),
        "w_out_even": lambda g: g.reshape(8, -1, D),
        "w_out_odd": lambda g: g.reshape(8, -1, D),
        "w_pl_gate": lambda g: g.reshape(DEPTH, 8, D // 8, D).transpose(1, 0, 2, 3).reshape(8, DEPTH * D // 8, D),
        "w_pl": lambda g: g.reshape(DEPTH, PD, 8, D // 8).transpose(2, 0, 1, 3).reshape(8, DEPTH * PD, D // 8),
        "conv": lambda g: jnp.concatenate([tap_blocks(g[0], ca), tap_blocks(g[1], cb)], axis=2),
    }
    exchange = _GradExchange(cidx, qidx, owner_layout)
    gr = _local_grads(x[0], p[0, 0], p[1, 0], loss_target[0], wt_zs, wt_a, wt_qkv, _Late, conv_a, conv_b,
                      a_log, dt_bias, gdn_norm_g, lower_bounds, hgrn_norm_g, ln_g, ln_b, on_grad=exchange.start)

    last = gr["grad_x"]
    pack_taps = lambda a, b: jnp.concatenate([_pad_rows(a[0], SUBLANES), _pad_rows(b[0], SUBLANES)], axis=1)[None]
    o_outo = exchange.finish("w_out_odd", w_out_odd, m_w_out_odd, v_w_out_odd, last)
    o_ino = exchange.finish("w_in_odd", w_in_odd, m_w_in_odd, v_w_in_odd, last)
    o_gate = exchange.finish("w_pl_gate", w_pl_gate, m_w_pl_gate, v_w_pl_gate, last)
    o_pl = exchange.finish("w_pl", w_pl, m_w_pl, v_w_pl, last)
    o_oute = exchange.finish("w_out_even", w_out_even, m_w_out_even, v_w_out_even, last)
    o_taps = exchange.finish("conv", taps[None], pack_taps(m_conv_a_w, m_conv_b_w), pack_taps(v_conv_a_w, v_conv_b_w), last)
    others_done = sum(o[1][0, 0:1, 0:1] for o in (o_outo, o_ino, o_gate, o_pl, o_oute, o_taps))
    rows_first = lambda a: jnp.transpose(a, (2, 0, 1))
    o_ine = [jnp.transpose(o, (1, 2, 0)) for o in exchange.finish(
        "w_in_even", rows_first(w_in_even), rows_first(m_w_in_even), rows_first(v_w_in_even), others_done)]

    small_g = _pack_small(gr["a_log"], gr["dt_bias"], gr["gdn_gain"], gr["lower_bounds"], gr["hgrn_gain"],
                          gr["ln_g"], gr["ln_b"])
    small_g = small_g.at[0, LANES - 1].set(gr["loss"])
    o_small = _adam_replicated(
        _pack_small(a_log, dt_bias, gdn_norm_g, lower_bounds, hgrn_norm_g, ln_g, ln_b),
        _pack_small(m_a_log, m_dt_bias, m_gdn_norm_g, m_lower_bounds, m_hgrn_norm_g, m_ln_g, m_ln_b),
        _pack_small(v_a_log, v_dt_bias, v_gdn_norm_g, v_lower_bounds, v_hgrn_norm_g, v_ln_g, v_ln_b),
        _all_gather(small_g, "ag_small_grads"))
    small_shapes = [a_log.shape, dt_bias.shape, gdn_norm_g.shape, lower_bounds.shape, hgrn_norm_g.shape,
                    ln_g.shape, ln_b.shape]

    def leaves(kind):
        s_alog, s_dt, s_gg, s_lb, s_hg, s_lng, s_lnb = _unpack_small(o_small[kind], small_shapes)
        t = o_taps[kind]
        return [o_ine[kind], t[:, :ka, :ca], t[:, :kb, ca:], s_alog, s_dt, s_gg, o_oute[kind],
                o_ino[kind], s_lb, s_hg, o_outo[kind], s_lng, s_lnb, o_pl[kind], o_gate[kind]]

    return (o_small[0][0, LANES - 1], gr["grad_x"][None], *leaves(0), *leaves(1), *leaves(2), *leaves(3))
```

```python
import functools

import jax
import jax.numpy as jnp
from jax import lax
from jax.experimental import pallas as pl
from jax.experimental.pallas import tpu as pltpu

F32 = jnp.float32
BF16 = jnp.bfloat16
MESH = pl.DeviceIdType.MESH
AXES = ("x", "y", "c")

LANES = 128
SUBLANES = 8
HEAD = 128
GDN_CHUNK = 64
HGRN_CHUNK = 64
HGRN_SUB = 16
HGRN_HEADS_PER_STEP = 16
NORM_EPS = 1e-5
DEPTH = 2
ALPHA = (2.0 * DEPTH) ** 0.25
EXP_CLAMP = 80.0
ADAM_LR, ADAM_B1, ADAM_B2, ADAM_EPS, ADAM_WD, ADAM_STEP = 0.001, 0.9, 0.999, 1e-08, 0.01, 10
VMEM_LIMIT = 56 * 1024 * 1024
MATMUL_VMEM = 36 * 1024 * 1024
ROW_TILE = 512
MIXER_LANES = 256
CONV_LANES = 512
POST_TILE = 256
SMALL_ROWS = 96

_NOBATCH, _BATCH0 = ((), ()), ((0,), (0,))
_DIMS = {"nn": (((1,), (0,)), _NOBATCH), "nt": (((1,), (1,)), _NOBATCH), "tn": (((0,), (0,)), _NOBATCH),
         "bnn": (((2,), (1,)), _BATCH0), "bnt": (((2,), (2,)), _BATCH0), "btn": (((1,), (1,)), _BATCH0)}


def _params(**kw):
    return pltpu.CompilerParams(vmem_limit_bytes=VMEM_LIMIT, **kw)


def _dot_raw(a, b, kind, hi):
    if hi:
        return lax.dot_general(a, b, _DIMS[kind], precision=lax.Precision.HIGHEST, preferred_element_type=F32)
    return lax.dot_general(a.astype(BF16), b.astype(BF16), _DIMS[kind], preferred_element_type=F32)


@functools.partial(jax.custom_vjp, nondiff_argnums=(2, 3))
def mdot(a, b, kind, hi):
    return _dot_raw(a, b, kind, hi)


def _mdot_fwd(a, b, kind, hi):
    return _dot_raw(a, b, kind, hi), (a, b)


def _mdot_bwd(kind, hi, res, g):
    a, b = res
    pre, base = kind[:-2], kind[-2:]
    if base == "nn":
        return _dot_raw(g, b, pre + "nt", hi), _dot_raw(a, g, pre + "tn", hi)
    if base == "nt":
        return _dot_raw(g, b, pre + "nn", hi), _dot_raw(g, a, pre + "tn", hi)
    return _dot_raw(b, g, pre + "nt", hi), _dot_raw(a, g, pre + "nn", hi)


mdot.defvjp(_mdot_fwd, _mdot_bwd)


def _rows(x, lo, hi):
    return _take_rows(x, lo, hi, x.shape[-2])


@functools.partial(jax.custom_vjp, nondiff_argnums=(1, 2, 3))
def _take_rows(x, lo, hi, n):
    return x[..., lo:hi, :]


def _take_rows_fwd(x, lo, hi, n):
    return x[..., lo:hi, :], None


def _take_rows_bwd(lo, hi, n, _, g):
    parts = []
    if lo > 0:
        parts.append(jnp.zeros(g.shape[:-2] + (lo, g.shape[-1]), g.dtype))
    parts.append(g)
    if n - hi > 0:
        parts.append(jnp.zeros(g.shape[:-2] + (n - hi, g.shape[-1]), g.dtype))
    return (jnp.concatenate(parts, axis=-2) if len(parts) > 1 else g,)


_take_rows.defvjp(_take_rows_fwd, _take_rows_bwd)


def _heads_of(wide, nheads):
    return jnp.stack([wide[:, h * HEAD:(h + 1) * HEAD] for h in range(nheads)], axis=0)


def _wide_of(x):
    return jnp.concatenate([x[h] for h in range(x.shape[0])], axis=1)


@functools.partial(jax.custom_vjp, nondiff_argnums=(1,))
def to_heads(wide, nheads):
    return _heads_of(wide, nheads)


to_heads.defvjp(lambda wide, nheads: (_heads_of(wide, nheads), None), lambda nheads, _, g: (_wide_of(g),))


@jax.custom_vjp
def to_wide(x):
    return _wide_of(x)


to_wide.defvjp(lambda x: (_wide_of(x), None), lambda _, g: (_heads_of(g, g.shape[1] // HEAD),))


def _sigmoid(x):
    return jax.nn.sigmoid(x)


def _silu(x):
    return x * _sigmoid(x)


def _dsilu(x):
    s = _sigmoid(x)
    return s * (1.0 + x * (1.0 - s))


def _log1p(u):
    return jnp.where(u < 1e-4, u * (1.0 - 0.5 * u), jnp.log(1.0 + u))


def _softplus(x):
    return jnp.maximum(x, 0.0) + _log1p(jnp.exp(-jnp.abs(x)))


def _rms_gate(o, gain, z):
    return o * lax.rsqrt(jnp.mean(o * o, axis=-1, keepdims=True) + NORM_EPS) * gain * _silu(z)


def _l2n(x):
    return x * lax.rsqrt(jnp.sum(x * x, axis=-1, keepdims=True) + 1e-6)


def _split_dot_raw(m, x, kind):
    mb = m.astype(BF16)
    hi = x.astype(BF16)
    lo = (x - hi.astype(F32)).astype(BF16)
    dims = _DIMS[kind]
    return (lax.dot_general(mb, hi, dims, preferred_element_type=F32)
            + lax.dot_general(mb, lo, dims, preferred_element_type=F32))


@jax.custom_vjp
def mask_dot(m, x):
    return _split_dot_raw(m, x, "nn")


def _mask_dot_fwd(m, x):
    return _split_dot_raw(m, x, "nn"), m


def _mask_dot_bwd(m, g):
    return jnp.zeros_like(m), _split_dot_raw(m, g, "tn")


mask_dot.defvjp(_mask_dot_fwd, _mask_dot_bwd)


def _neumann_rest(low):
    n = low.shape[-1]
    rest = -low
    power = low
    span = 2
    while span < n:
        power = _dot_raw(power, power, "bnn", False)
        rest = rest + power + _dot_raw(rest, power, "bnn", False)
        span *= 2
    return rest


@jax.custom_vjp
def _unit_lower_inverse_minus_eye(low):
    return _neumann_rest(low)


def _inverse_fwd(low):
    rest = _neumann_rest(low)
    return rest, rest


def _inverse_bwd(rest, g):
    left = g + _dot_raw(rest, g, "btn", False)
    return (-(left + _dot_raw(left, rest, "bnt", False)),)


_unit_lower_inverse_minus_eye.defvjp(_inverse_fwd, _inverse_bwd)


def _gdn_step(S, q, k, v, z, small, alog, dtb, gain):
    H = S.shape[0]
    C = GDN_CHUNK
    row = lax.broadcasted_iota(jnp.int32, (C, C), 0)
    col = lax.broadcasted_iota(jnp.int32, (C, C), 1)
    tril, strict, eye = (row >= col)[None], (row > col)[None], (row == col)[None]
    head = lax.broadcasted_iota(jnp.int32, (H, 1, LANES), 0)
    lane = lax.broadcasted_iota(jnp.int32, (H, 1, LANES), 2)
    rowc = lax.broadcasted_iota(jnp.int32, (1, C, 1), 1)
    beta_all = _sigmoid(small)
    g_all = -jnp.exp(alog) * _softplus(small + dtb)
    gc_all = mask_dot((row >= col).astype(F32), g_all)
    beta = jnp.sum(jnp.where(lane == head, beta_all[None], 0.0), axis=-1, keepdims=True)
    gc = jnp.sum(jnp.where(lane == head + H, gc_all[None], 0.0), axis=-1, keepdims=True)
    gc_row = jnp.sum(jnp.where(eye, gc, 0.0), axis=1, keepdims=True)
    decay = jnp.where(tril, jnp.exp(jnp.where(tril, gc - gc_row, 0.0)), 0.0)
    g_last = jnp.sum(jnp.where(rowc == C - 1, gc, 0.0), axis=1, keepdims=True)
    qn = _l2n(q) * (HEAD ** -0.5)
    kn = _l2n(k)
    kb = kn * beta
    low = jnp.where(strict, beta * mdot(kn, kn, "bnt", False) * decay, 0.0)
    inv_rest = _unit_lower_inverse_minus_eye(low)
    eg = jnp.exp(gc)
    vb, kbe = v * beta, kb * eg
    u = vb + mdot(inv_rest, vb, "bnn", False)
    w = kbe + mdot(inv_rest, kbe, "bnn", False)
    attn = mdot(qn, kn, "bnt", False) * decay
    v_new = u - mdot(w, S, "bnn", False)
    o = mdot(qn * eg, S, "bnn", False) + mdot(attn, v_new, "bnn", False)
    k_dec = kn * jnp.exp(g_last - gc)
    return _rms_gate(o, gain, z), S * jnp.exp(g_last) + mdot(k_dec, v_new, "btn", False)


def _hgrn_step(St, qr, fr, vi, z, lb0, lb1, gain):
    H = St.shape[0]
    C, SB = HGRN_CHUNK, HGRN_SUB
    row = lax.broadcasted_iota(jnp.int32, (C, C), 0)
    col = lax.broadcasted_iota(jnp.int32, (C, C), 1)
    blk_start = row - (row & (SB - 1))
    in_blk_f = ((row >= col) & (col >= blk_start)).astype(F32)
    before_f = (col < blk_start).astype(F32)
    sums_f = jnp.concatenate([in_blk_f, before_f], axis=0)
    m = jnp.maximum(lb0, lb1)
    e0, e1 = jnp.exp(lb0 - m), jnp.exp(lb1 - m)
    lb = e1 / (e0 + e1)
    f = lb + (1.0 - lb) * _sigmoid(fr)
    q = _silu(qr)
    k = 1.0 - f
    logf = jnp.log(f)
    sums = mask_dot(sums_f, to_wide(logf))
    inner, start = to_heads(_rows(sums, 0, C), H), to_heads(_rows(sums, C, 2 * C), H)
    b = start + inner
    b_last = jnp.sum(logf, axis=1, keepdims=True)
    o = mdot(q * jnp.exp(b), St, "bnt", False)
    qt = q * jnp.exp(inner)
    parts = []
    for blk in range(C // SB):
        lo, n = blk * SB, (blk + 1) * SB
        ref = jnp.concatenate([_rows(start, lo, n)] * (blk + 1), axis=1)
        kt = _rows(k, 0, n) * jnp.exp(jnp.minimum(ref - _rows(b, 0, n), EXP_CLAMP))
        att = mdot(_rows(qt, lo, n), kt, "bnt", False)
        t_idx = lax.broadcasted_iota(jnp.int32, (1, SB, n), 1) + lo
        s_idx = lax.broadcasted_iota(jnp.int32, (1, SB, n), 2)
        att = jnp.where(s_idx <= t_idx, att, 0.0)
        parts.append(mdot(att, _rows(vi, 0, n), "bnn", False))
    o = o + jnp.concatenate(parts, axis=1)
    k_dec = k * jnp.exp(b_last - b)
    return _rms_gate(o, gain, z), St * jnp.exp(b_last) + mdot(vi, k_dec, "btn", False)


def _post_norm(s, x, g, b):
    r = ALPHA * x + s
    d = r - jnp.mean(r, axis=-1, keepdims=True)
    var = jnp.mean(d * d, axis=-1, keepdims=True)
    return d * lax.rsqrt(var + NORM_EPS) * g + b


def _post_gate(x1, gate_pre, pp):
    return x1 + pp * _sigmoid(gate_pre)


def _pick(dim, cands):
    for c in cands:
        if dim % c == 0:
            return c
    return dim


def _matmul_tiles(M, K, tn, a_bytes, b_bytes, has_add):
    for tk in (4096, 2048, 1536, 1152, 1024, 640, 512, 384, 256, 128):
        if K % tk:
            continue
        for tm in (2048, 1152, 1024, 512, 384, 256, 128):
            if M % tm:
                continue
            blocks = tm * tk * a_bytes + tk * tn * b_bytes + tm * tn * 4 * (2 if has_add else 1)
            if 2 * blocks + (tm * tn * 4 if tk < K else 0) <= MATMUL_VMEM and tm >= min(M, 1024):
                return tm, tk
    return _pick(M, (512, 256, 128)), _pick(K, (512, 256, 128))


def _matmul(a, b, kind, name, add=None, after=None):
    if kind == "nn":
        (M, K), N = a.shape, b.shape[1]
    elif kind == "nt":
        (M, K), N = a.shape, b.shape[0]
    else:
        (K, M), N = a.shape, b.shape[1]
    has_add = add is not None
    tn = _pick(N, (512, 640, 384, 256, 128))
    tm, tk = _matmul_tiles(M, K, tn, a.dtype.itemsize, b.dtype.itemsize, has_add)
    nk = K // tk
    a_spec = pl.BlockSpec((tk, tm), lambda i, j, k: (k, i)) if kind == "tn" else pl.BlockSpec((tm, tk), lambda i, j, k: (i, k))
    b_spec = pl.BlockSpec((tn, tk), lambda i, j, k: (j, k)) if kind == "nt" else pl.BlockSpec((tk, tn), lambda i, j, k: (k, j))
    o_spec = pl.BlockSpec((tm, tn), lambda i, j, k: (i, j))

    extra = ([add] if has_add else []) + ([after] if after is not None else [])
    extra_specs = ([o_spec] if has_add else []) + ([pl.BlockSpec(TOKEN_SHAPE, lambda i, j, k: (0, 0))] if after is not None else [])

    def body(a_ref, b_ref, *rest):
        add_ref = rest[0] if has_add else None
        o_ref = rest[len(extra)]
        part = _dot_raw(a_ref[...], b_ref[...], kind, False)
        if nk == 1:
            o_ref[...] = part + add_ref[...] if has_add else part
            return
        acc = rest[-1]
        kk = pl.program_id(2)

        @pl.when(kk == 0)
        def _():
            acc[...] = part

        @pl.when(kk > 0)
        def _():
            acc[...] += part

        @pl.when(kk == nk - 1)
        def _():
            o_ref[...] = acc[...] + add_ref[...] if has_add else acc[...]

    return pl.pallas_call(
        body, name=name, grid=(M // tm, N // tn, nk),
        in_specs=[a_spec, b_spec] + extra_specs,
        out_specs=o_spec, out_shape=jax.ShapeDtypeStruct((M, N), F32),
        scratch_shapes=[pltpu.VMEM((tm, tn), F32)] if nk > 1 else [],
        compiler_params=_params(dimension_semantics=("parallel", "parallel", "arbitrary")),
    )(a, b, *extra)


def _halo_specs(ts, nt, width, prev=True, main=True, nxt=True):
    per = ts // SUBLANES
    last8 = nt * per - 1
    specs = []
    if prev:
        specs.append(pl.BlockSpec((SUBLANES, width), lambda cb, i: (jnp.maximum(i * per - 1, 0), cb)))
    if main:
        specs.append(pl.BlockSpec((ts, width), lambda cb, i: (i, cb)))
    if nxt:
        specs.append(pl.BlockSpec((SUBLANES, width), lambda cb, i: (jnp.minimum((i + 1) * per, last8), cb)))
    return specs


def _taps(ext, ktaps, lo, size):
    return [ext[lo:lo + size] if j == 0 else pltpu.roll(ext, j, 0)[lo:lo + size] for j in range(ktaps)]


def _ahead(ext, j, size):
    n = ext.shape[0]
    return ext[:size] if j == 0 else pltpu.roll(ext, n - j, 0)[:size]


def _lane_block(ref, k):
    return ref[:, k * MIXER_LANES:(k + 1) * MIXER_LANES]


def _mixer_a_fwd(proj_a, conv_w):
    T = proj_a.shape[0]
    nblk = proj_a.shape[1] // (4 * MIXER_LANES)
    ts = min(ROW_TILE, T)
    nt = T // ts

    def body(pp, pm, w_ref, y_ref):
        i = pl.program_id(1)
        u_prev = jnp.where(i > 0, _lane_block(pp, 0) * _lane_block(pp, 1), 0.0)
        ext = jnp.concatenate([u_prev, _lane_block(pm, 0) * _lane_block(pm, 1)], axis=0)
        t0, t1, t2 = _taps(ext, 3, SUBLANES, ts)
        cv = w_ref[2:3, :] * t0 + w_ref[1:2, :] * t1 + w_ref[0:1, :] * t2
        y_ref[...] = (_lane_block(pm, 2) * cv * _silu(_lane_block(pm, 3))).astype(y_ref.dtype)

    return pl.pallas_call(
        body, name="mixer_a_fwd", grid=(nblk, nt),
        in_specs=_halo_specs(ts, nt, 4 * MIXER_LANES, nxt=False)
        + [pl.BlockSpec((conv_w.shape[0], MIXER_LANES), lambda cb, i: (0, cb))],
        out_specs=pl.BlockSpec((ts, MIXER_LANES), lambda cb, i: (i, cb)),
        out_shape=jax.ShapeDtypeStruct((T, nblk * MIXER_LANES), BF16), compiler_params=_params(),
    )(proj_a, proj_a, conv_w)


def _mixer_a_bwd(proj_a, dy, conv_w):
    T = proj_a.shape[0]
    nblk = proj_a.shape[1] // (4 * MIXER_LANES)
    ts = min(ROW_TILE, T)
    nt = T // ts
    kt = conv_w.shape[0]

    def body(pp, pm, pn, dym, dyn, w_ref, dp_ref, dw_ref):
        i = pl.program_id(1)
        hm, cm, bm, zm = (_lane_block(pm, k) for k in range(4))
        u_prev = jnp.where(i > 0, _lane_block(pp, 0) * _lane_block(pp, 1), 0.0)
        ext = jnp.concatenate([u_prev, hm * cm], axis=0)
        dy_ext = jnp.concatenate([dym[...], jnp.where(i < nt - 1, dyn[...], 0.0)], axis=0)
        b_ext = jnp.concatenate([bm, _lane_block(pn, 2)], axis=0)
        sz_ext = _silu(jnp.concatenate([zm, _lane_block(pn, 3)], axis=0))
        dcv_ext = dy_ext * b_ext * sz_ext
        w = [w_ref[j:j + 1, :] for j in range(kt)]
        du = sum(w[kt - 1 - j] * _ahead(dcv_ext, j, ts) for j in range(kt))
        taps = _taps(ext, kt, SUBLANES, ts)
        cv = sum(w[kt - 1 - j] * taps[j] for j in range(kt))
        for part, d in enumerate((du * cm, du * hm, dym[...] * cv * sz_ext[:ts], dym[...] * bm * cv * _dsilu(zm))):
            dp_ref[:, part * MIXER_LANES:(part + 1) * MIXER_LANES] = d.astype(dp_ref.dtype)
        dcv = dcv_ext[:ts]

        @pl.when(i == 0)
        def _():
            dw_ref[...] = jnp.zeros_like(dw_ref)

        for j in range(kt):
            dw_ref[j:j + 1, :] += jnp.sum(dcv * taps[kt - 1 - j], axis=0, keepdims=True)

    return pl.pallas_call(
        body, name="mixer_a_bwd", grid=(nblk, nt),
        in_specs=_halo_specs(ts, nt, 4 * MIXER_LANES) + _halo_specs(ts, nt, MIXER_LANES, prev=False)
        + [pl.BlockSpec((kt, MIXER_LANES), lambda cb, i: (0, cb))],
        out_specs=[pl.BlockSpec((ts, 4 * MIXER_LANES), lambda cb, i: (i, cb)),
                   pl.BlockSpec((SUBLANES, MIXER_LANES), lambda cb, i: (0, cb))],
        out_shape=[jax.ShapeDtypeStruct(proj_a.shape, BF16),
                   jax.ShapeDtypeStruct((SUBLANES, nblk * MIXER_LANES), F32)],
        compiler_params=_params(),
    )(proj_a, proj_a, proj_a, dy, dy, conv_w)


def _conv_b_fwd(raw, conv_w):
    T = raw.shape[0]
    nblk = raw.shape[1] // CONV_LANES
    ts = min(ROW_TILE, T)
    nt = T // ts
    kt = conv_w.shape[0]

    def body(rp, rm, w_ref, y_ref):
        i = pl.program_id(1)
        ext = jnp.concatenate([jnp.where(i > 0, rp[...], 0.0), rm[...]], axis=0)
        taps = _taps(ext, kt, SUBLANES, ts)
        y_ref[...] = _silu(sum(w_ref[kt - 1 - j:kt - j, :] * taps[j] for j in range(kt)))

    return pl.pallas_call(
        body, name="conv_b_fwd", grid=(nblk, nt),
        in_specs=_halo_specs(ts, nt, CONV_LANES, nxt=False) + [pl.BlockSpec((kt, CONV_LANES), lambda cb, i: (0, cb))],
        out_specs=pl.BlockSpec((ts, CONV_LANES), lambda cb, i: (i, cb)),
        out_shape=jax.ShapeDtypeStruct(raw.shape, F32), compiler_params=_params(),
    )(raw, raw, conv_w)


def _conv_b_bwd(raw, dy, conv_w):
    T = raw.shape[0]
    nblk = raw.shape[1] // CONV_LANES
    ts = min(ROW_TILE, T)
    nt = T // ts
    kt = conv_w.shape[0]

    def body(rp, rm, rn, dym, dyn, w_ref, dr_ref, dw_ref):
        i = pl.program_id(1)
        ext = jnp.concatenate([jnp.where(i > 0, rp[...], 0.0), rm[...], rn[...]], axis=0)
        w = [w_ref[j:j + 1, :] for j in range(kt)]
        taps = _taps(ext, kt, SUBLANES, ts + SUBLANES)
        xc_ext = sum(w[kt - 1 - j] * taps[j] for j in range(kt))
        dy_ext = jnp.concatenate([dym[...], jnp.where(i < nt - 1, dyn[...], 0.0)], axis=0)
        dxc_ext = dy_ext * _dsilu(xc_ext)
        dr_ref[...] = sum(w[kt - 1 - j] * _ahead(dxc_ext, j, ts) for j in range(kt)).astype(dr_ref.dtype)
        dxc = dxc_ext[:ts]

        @pl.when(i == 0)
        def _():
            dw_ref[...] = jnp.zeros_like(dw_ref)

        for j in range(kt):
            dw_ref[j:j + 1, :] += jnp.sum(dxc * taps[kt - 1 - j][:ts], axis=0, keepdims=True)

    return pl.pallas_call(
        body, name="conv_b_bwd", grid=(nblk, nt),
        in_specs=_halo_specs(ts, nt, CONV_LANES) + _halo_specs(ts, nt, CONV_LANES, prev=False)
        + [pl.BlockSpec((kt, CONV_LANES), lambda cb, i: (0, cb))],
        out_specs=[pl.BlockSpec((ts, CONV_LANES), lambda cb, i: (i, cb)),
                   pl.BlockSpec((SUBLANES, CONV_LANES), lambda cb, i: (0, cb))],
        out_shape=[jax.ShapeDtypeStruct(raw.shape, BF16), jax.ShapeDtypeStruct((SUBLANES, nblk * CONV_LANES), F32)],
        compiler_params=_params(),
    )(raw, raw, raw, dy, dy, conv_w)


def _split_heads(ref, base, nheads, rows=slice(None)):
    return jnp.stack([ref[rows, base + h * HEAD: base + (h + 1) * HEAD] for h in range(nheads)], axis=0)


def _store_heads(ref, base, x, rows=slice(None), accumulate=False):
    for h in range(x.shape[0]):
        lanes = slice(base + h * HEAD, base + (h + 1) * HEAD)
        if accumulate:
            ref[rows, lanes] += x[h]
        else:
            ref[rows, lanes] = x[h].astype(ref.dtype)


def _gdn_fwd(qkv, proj_zs, alog, dtb, gain, H):
    T = qkv.shape[0]
    C, HW = GDN_CHUNK, H * HEAD
    nc = T // C
    zw = HW + LANES

    def body(qkv_ref, zs_ref, alog_ref, dtb_ref, gain_ref, o_ref, sall_ref, s_scr):
        @pl.when(pl.program_id(0) == 0)
        def _():
            s_scr[...] = jnp.zeros_like(s_scr)

        sall_ref[0] = s_scr[...]
        outs, states = _gdn_step(
            s_scr[...], _split_heads(qkv_ref, 0, H), _split_heads(qkv_ref, HW, H),
            _split_heads(qkv_ref, 2 * HW, H), _split_heads(zs_ref, 0, H), zs_ref[:, HW:HW + LANES],
            alog_ref[...], dtb_ref[...], gain_ref[...])
        _store_heads(o_ref, 0, outs)
        s_scr[...] = states

    row = pl.BlockSpec((1, LANES), lambda i: (0, 0))
    return pl.pallas_call(
        body, name="gdn_fwd", grid=(nc,),
        in_specs=[pl.BlockSpec((C, 3 * HW), lambda i: (i, 0)), pl.BlockSpec((C, zw), lambda i: (i, 0)), row, row, row],
        out_specs=[pl.BlockSpec((C, HW), lambda i: (i, 0)), pl.BlockSpec((1, H, HEAD, HEAD), lambda i: (i, 0, 0, 0))],
        out_shape=[jax.ShapeDtypeStruct((T, HW), BF16), jax.ShapeDtypeStruct((nc, H, HEAD, HEAD), F32)],
        scratch_shapes=[pltpu.VMEM((H, HEAD, HEAD), F32)], compiler_params=_params(),
    )(qkv, proj_zs, alog, dtb, gain)


def _gdn_bwd(qkv, proj_zs, do, s_all, alog, dtb, gain, H):
    T = qkv.shape[0]
    C, HW = GDN_CHUNK, H * HEAD
    nc = T // C
    zw = HW + LANES

    def body(qkv_ref, zs_ref, do_ref, sin_ref, alog_ref, dtb_ref, gain_ref,
             dqkv_ref, dzs_ref, dalog_ref, ddtb_ref, dgain_ref, ds_scr):
        @pl.when(pl.program_id(0) == 0)
        def _():
            ds_scr[...] = jnp.zeros_like(ds_scr)
            dalog_ref[...] = jnp.zeros_like(dalog_ref)
            ddtb_ref[...] = jnp.zeros_like(ddtb_ref)
            dgain_ref[...] = jnp.zeros_like(dgain_ref)

        primals = (sin_ref[0], _split_heads(qkv_ref, 0, H),
                   _split_heads(qkv_ref, HW, H), _split_heads(qkv_ref, 2 * HW, H), _split_heads(zs_ref, 0, H),
                   zs_ref[:, HW:HW + LANES], alog_ref[...], dtb_ref[...], gain_ref[...])
        _, vjp = jax.vjp(_gdn_step, *primals)
        dS, dq, dk, dv, dz, dsmall, dalog, ddtb, dgain = vjp((_split_heads(do_ref, 0, H), ds_scr[...]))
        ds_scr[...] = dS
        _store_heads(dqkv_ref, 0, dq)
        _store_heads(dqkv_ref, HW, dk)
        _store_heads(dqkv_ref, 2 * HW, dv)
        _store_heads(dzs_ref, 0, dz)
        dzs_ref[:, HW:HW + LANES] = dsmall.astype(dzs_ref.dtype)
        dalog_ref[...] += dalog
        ddtb_ref[...] += ddtb
        dgain_ref[...] += dgain

    row = pl.BlockSpec((1, LANES), lambda i: (0, 0))
    rev = lambda i: nc - 1 - i
    return pl.pallas_call(
        body, name="gdn_bwd", grid=(nc,),
        in_specs=[pl.BlockSpec((C, 3 * HW), lambda i: (rev(i), 0)), pl.BlockSpec((C, zw), lambda i: (rev(i), 0)),
                  pl.BlockSpec((C, HW), lambda i: (rev(i), 0)),
                  pl.BlockSpec((1, H, HEAD, HEAD), lambda i: (rev(i), 0, 0, 0)), row, row, row],
        out_specs=[pl.BlockSpec((C, 3 * HW), lambda i: (rev(i), 0)), pl.BlockSpec((C, zw), lambda i: (rev(i), 0)),
                   row, row, row],
        out_shape=[jax.ShapeDtypeStruct(qkv.shape, F32), jax.ShapeDtypeStruct(proj_zs.shape, BF16)]
        + [jax.ShapeDtypeStruct((1, LANES), F32)] * 3,
        scratch_shapes=[pltpu.VMEM((H, HEAD, HEAD), F32)], compiler_params=_params(),
    )(qkv, proj_zs, do, s_all, alog, dtb, gain)


def _hgrn_refs(proj_ref, lb_ref, HP):
    W = HP * HEAD
    return (_split_heads(proj_ref, 0, HP), _split_heads(proj_ref, W, HP), _split_heads(proj_ref, 2 * W, HP),
            _split_heads(proj_ref, 3 * W, HP), _split_heads(lb_ref, 0, HP, slice(0, 1)),
            _split_heads(lb_ref, 0, HP, slice(1, 2)))


def _hgrn_fwd(proj, lower_bounds, gain, nheads):
    T = proj.shape[0]
    C, HP = HGRN_CHUNK, HGRN_HEADS_PER_STEP
    ng, nc, W = nheads // HP, T // C, HP * HEAD

    def body(proj_ref, lb_ref, gain_ref, o_ref, sall_ref, s_scr):
        @pl.when(pl.program_id(1) == 0)
        def _():
            s_scr[...] = jnp.zeros_like(s_scr)

        sall_ref[0] = s_scr[...]
        qr, fr, vi, z, lb0, lb1 = _hgrn_refs(proj_ref, lb_ref, HP)
        outs, states = _hgrn_step(s_scr[...], qr, fr, vi, z, lb0, lb1, gain_ref[...])
        _store_heads(o_ref, 0, outs)
        s_scr[...] = states

    return pl.pallas_call(
        body, name="hgrn_fwd", grid=(ng, nc),
        in_specs=[pl.BlockSpec((C, 4 * W), lambda g, i: (i, g)), pl.BlockSpec((2, W), lambda g, i: (0, g)),
                  pl.BlockSpec((1, LANES), lambda g, i: (0, 0))],
        out_specs=[pl.BlockSpec((C, W), lambda g, i: (i, g)),
                   pl.BlockSpec((1, HP, HEAD, HEAD), lambda g, i: (i, g, 0, 0))],
        out_shape=[jax.ShapeDtypeStruct((T, nheads * HEAD), BF16), jax.ShapeDtypeStruct((nc, nheads, HEAD, HEAD), F32)],
        scratch_shapes=[pltpu.VMEM((HP, HEAD, HEAD), F32)], compiler_params=_params(),
    )(proj, lower_bounds, gain)


def _hgrn_bwd(proj, do, s_all, lower_bounds, gain, nheads):
    T = proj.shape[0]
    C, HP = HGRN_CHUNK, HGRN_HEADS_PER_STEP
    ng, nc, W = nheads // HP, T // C, HP * HEAD

    def body(proj_ref, do_ref, sin_ref, lb_ref, gain_ref, dproj_ref, dlb_ref, dgain_ref, ds_scr):
        first = pl.program_id(1) == 0

        @pl.when(first)
        def _():
            ds_scr[...] = jnp.zeros_like(ds_scr)
            dlb_ref[...] = jnp.zeros_like(dlb_ref)

        @pl.when(first & (pl.program_id(0) == 0))
        def _():
            dgain_ref[...] = jnp.zeros_like(dgain_ref)

        qr, fr, vi, z, lb0, lb1 = _hgrn_refs(proj_ref, lb_ref, HP)
        primals = (sin_ref[0], qr, fr, vi, z, lb0, lb1, gain_ref[...])
        _, vjp = jax.vjp(_hgrn_step, *primals)
        dS, dq, df, dv, dz, dlb0, dlb1, dgain = vjp((_split_heads(do_ref, 0, HP), ds_scr[...]))
        ds_scr[...] = dS
        for part, d in enumerate((dq, df, dv, dz)):
            _store_heads(dproj_ref, part * W, d)
        _store_heads(dlb_ref, 0, dlb0, slice(0, 1), accumulate=True)
        _store_heads(dlb_ref, 0, dlb1, slice(1, 2), accumulate=True)
        dgain_ref[...] += dgain

    rev = lambda i: nc - 1 - i
    return pl.pallas_call(
        body, name="hgrn_bwd", grid=(ng, nc),
        in_specs=[pl.BlockSpec((C, 4 * W), lambda g, i: (rev(i), g)), pl.BlockSpec((C, W), lambda g, i: (rev(i), g)),
                  pl.BlockSpec((1, HP, HEAD, HEAD), lambda g, i: (rev(i), g, 0, 0)),
                  pl.BlockSpec((2, W), lambda g, i: (0, g)), pl.BlockSpec((1, LANES), lambda g, i: (0, 0))],
        out_specs=[pl.BlockSpec((C, 4 * W), lambda g, i: (rev(i), g)), pl.BlockSpec((2, W), lambda g, i: (0, g)),
                   pl.BlockSpec((1, LANES), lambda g, i: (0, 0))],
        out_shape=[jax.ShapeDtypeStruct(proj.shape, BF16), jax.ShapeDtypeStruct(lower_bounds.shape, F32),
                   jax.ShapeDtypeStruct((1, LANES), F32)],
        scratch_shapes=[pltpu.VMEM((HP, HEAD, HEAD), F32)], compiler_params=_params(),
    )(proj, do, s_all, lower_bounds, gain)


def _post_specs(T):
    tr = min(POST_TILE, T)
    tile = lambda w: pl.BlockSpec((tr, w), lambda i: (i, 0))
    full = lambda r, w: pl.BlockSpec((r, w), lambda i: (0, 0))
    return tr, tile, full


def _post_fwd(s, x, p, g, b, wg, wpl, name):
    T, D = x.shape
    P = p.shape[1]
    tr, tile, full = _post_specs(T)

    def body(s_ref, x_ref, p_ref, g_ref, b_ref, wg_ref, wpl_ref, o_ref, o16_ref):
        x1 = _post_norm(s_ref[...], x_ref[...], g_ref[...], b_ref[...])
        xn = _post_gate(x1, _dot_raw(x1, wg_ref[...], "nn", False), _dot_raw(p_ref[...], wpl_ref[...], "nn", False))
        o_ref[...] = xn
        o16_ref[...] = xn.astype(BF16)

    return pl.pallas_call(
        body, name=name, grid=(T // tr,),
        in_specs=[tile(D), tile(D), tile(P), full(1, D), full(1, D), full(D, D), full(P, D)],
        out_specs=[tile(D), tile(D)],
        out_shape=[jax.ShapeDtypeStruct((T, D), F32), jax.ShapeDtypeStruct((T, D), BF16)], compiler_params=_params(),
    )(s, x, p, g, b, wg, wpl)


def _post_bwd(s, x, p, g, b, wg, wpl, dnext, name, with_loss):
    T, D = x.shape
    P = p.shape[1]
    tr, tile, full = _post_specs(T)

    def body(s_ref, x_ref, p_ref, g_ref, b_ref, wg_ref, wpl_ref, dn_ref,
             ds_ref, dx_ref, dg_ref, db_ref, dwg_ref, dwpl_ref, loss_ref):
        @pl.when(pl.program_id(0) == 0)
        def _():
            for r in (dg_ref, db_ref, dwg_ref, dwpl_ref, loss_ref):
                r[...] = jnp.zeros_like(r)

        x1, vjp_norm = jax.vjp(_post_norm, s_ref[...], x_ref[...], g_ref[...], b_ref[...])
        gate_pre = _dot_raw(x1, wg_ref[...], "nn", False)
        pp = _dot_raw(p_ref[...], wpl_ref[...], "nn", False)
        xn, vjp_gate = jax.vjp(_post_gate, x1, gate_pre, pp)
        if with_loss:
            err = xn - dn_ref[...]
            loss_ref[...] += 0.5 * jnp.sum(jnp.sum(err * err, axis=-1, keepdims=True), axis=0, keepdims=True) / D
            dn = err / D
        else:
            dn = dn_ref[...]
        dx1, dgp, dpp = vjp_gate(dn)
        dwg_ref[...] += _dot_raw(x1, dgp, "tn", False)
        dwpl_ref[...] += _dot_raw(p_ref[...], dpp, "tn", False)
        dx1 = dx1 + _dot_raw(dgp, wg_ref[...], "nt", False)
        ds, dx, dg, db = vjp_norm(dx1)
        ds_ref[...] = ds.astype(ds_ref.dtype)
        dx_ref[...] = dx
        dg_ref[...] += dg
        db_ref[...] += db

    return pl.pallas_call(
        body, name=name, grid=(T // tr,),
        in_specs=[tile(D), tile(D), tile(P), full(1, D), full(1, D), full(D, D), full(P, D), tile(D)],
        out_specs=[tile(D), tile(D), full(1, D), full(1, D), full(D, D), full(P, D), full(SUBLANES, LANES)],
        out_shape=[jax.ShapeDtypeStruct((T, D), BF16), jax.ShapeDtypeStruct((T, D), F32)]
        + [jax.ShapeDtypeStruct((1, D), F32)] * 2
        + [jax.ShapeDtypeStruct((D, D), F32), jax.ShapeDtypeStruct((P, D), F32),
           jax.ShapeDtypeStruct((SUBLANES, LANES), F32)],
        compiler_params=_params(),
    )(s, x, p, g, b, wg, wpl, dnext)


def _adam_math(w, g, m, v):
    m = ADAM_B1 * m + (1.0 - ADAM_B1) * g
    v = ADAM_B2 * v + (1.0 - ADAM_B2) * (g * g)
    m_hat = m / (1.0 - ADAM_B1 ** ADAM_STEP)
    v_hat = v / (1.0 - ADAM_B2 ** ADAM_STEP)
    return -ADAM_LR * (m_hat / (jnp.sqrt(v_hat) + ADAM_EPS) + ADAM_WD * w), m, v


def _shard_tiles(R, C):
    tr = _pick(R, (256, 128, 64, 32, 16, 8))
    return (tr, C) if tr < R or R % SUBLANES == 0 else (R, _pick(C, (256, 128)))


def _adam_sharded(w, m, v, g8, got, me, name):
    rows_apart = w.shape[1] == 1 and w.shape[0] > 1
    (L, R, C) = (1, w.shape[0], w.shape[2]) if rows_apart else w.shape
    tr, tc = (R, LANES) if rows_apart else _shard_tiles(R, C)
    nr, nc = R // tr, C // tc
    side_by_side = g8.ndim == 2

    def body(me_ref, w_ref, m_ref, v_ref, p_ref, *rest):
        got_refs, (g_ref, d_ref, mo_ref, vo_ref) = rest[:7], rest[7:]
        g = p_ref[...] if side_by_side else p_ref[0]
        for r in got_refs:
            g = g + r[0].astype(F32)
        if rows_apart:
            d, mn, vn = _adam_math(w_ref[:, 0, :], g, m_ref[:, 0, :], v_ref[:, 0, :])
            for ref, val in ((g_ref, g), (d_ref, d), (mo_ref, mn), (vo_ref, vn)):
                ref[:, 0, :] = val
            return
        d, mn, vn = _adam_math(w_ref[0], g, m_ref[0], v_ref[0])
        g_ref[0] = g
        d_ref[0] = d
        mo_ref[0] = mn
        vo_ref[0] = vn

    if rows_apart:
        t3 = pl.BlockSpec((tr, 1, tc), lambda l, i, j, q: (i, 0, j))
    else:
        t3 = pl.BlockSpec((1, tr, tc), lambda l, i, j, q: (l, i, j))
    slot = lambda k: pl.BlockSpec((1, tr, tc), lambda l, i, j, q: (k, l * nr + i, j))
    if side_by_side:
        mine = pl.BlockSpec((tr, tc), lambda l, i, j, q: (l * nr + i, q[0] * nc + j))
    else:
        mine = pl.BlockSpec((1, tr, tc), lambda l, i, j, q: (q[0], l * nr + i, j))
    return pl.pallas_call(
        body, name=name,
        grid_spec=pltpu.PrefetchScalarGridSpec(
            num_scalar_prefetch=1, grid=(L, nr, nc),
            in_specs=[t3, t3, t3, mine] + [slot(k) for k in range(7)], out_specs=[t3, t3, t3, t3]),
        out_shape=[jax.ShapeDtypeStruct(w.shape, F32)] * 4, compiler_params=_params(),
    )(me, w, m, v, g8, *([got] * 7))


def _adam_replicated(w, m, v, g8):
    def body(w_ref, m_ref, v_ref, g_ref, go_ref, d_ref, mo_ref, vo_ref):
        g = g_ref[0]
        for k in range(1, 8):
            g = g + g_ref[k]
        d, mn, vn = _adam_math(w_ref[...], g, m_ref[...], v_ref[...])
        go_ref[...] = g
        d_ref[...] = d
        mo_ref[...] = mn
        vo_ref[...] = vn

    return pl.pallas_call(
        body, name="adam_replicated", out_shape=[jax.ShapeDtypeStruct(w.shape, F32)] * 4, compiler_params=_params(),
    )(w, m, v, g8)


def _place():
    return lax.axis_index("x"), lax.axis_index("y"), lax.axis_index("c")


def _all_gather(shard, name):
    def body(x_ref, out_ref, send_sems, recv_sems, local_sem):
        x, y, c = _place()
        me, sibling = (x, y, c), (x, y, 1 - c)
        chips = [(1 - x, y), (x, 1 - y), (1 - x, 1 - y)]

        def slab(px, py, pc):
            return out_ref.at[4 * px + 2 * py + pc]

        def copy(k, block, to, src=None):
            return pltpu.make_async_remote_copy(
                src_ref=slab(*block) if src is None else src, dst_ref=slab(*block),
                send_sem=send_sems.at[k], recv_sem=recv_sems.at[k], device_id=to, device_id_type=MESH)

        mine = pltpu.make_async_copy(x_ref, slab(*me), local_sem)
        mine.start()
        first = [copy(0, me, sibling, src=x_ref)]
        first += [copy(1 + j, me, (*chip, c), src=x_ref) for j, chip in enumerate(chips)]
        for cp in first:
            cp.start()
        passed = [copy(4 + j, (*chip, c), sibling) for j, chip in enumerate(chips)]
        for j, chip in enumerate(chips):
            copy(1 + j, (*chip, c), me).wait_recv()
            passed[j].start()
        copy(0, sibling, me).wait_recv()
        for j, chip in enumerate(chips):
            copy(4 + j, (*chip, 1 - c), me).wait_recv()
        for cp in first + passed:
            cp.wait_send()
        mine.wait()

    return pl.pallas_call(
        body, name=name, out_shape=jax.ShapeDtypeStruct((8,) + shard.shape, shard.dtype),
        in_specs=[pl.BlockSpec(memory_space=pl.ANY)], out_specs=pl.BlockSpec(memory_space=pl.ANY),
        scratch_shapes=[pltpu.SemaphoreType.DMA((7,)), pltpu.SemaphoreType.DMA((7,)), pltpu.SemaphoreType.DMA],
    )(shard)


_HBM = pl.BlockSpec(memory_space=pltpu.HBM)
_SEM = pl.BlockSpec(memory_space=pltpu.SEMAPHORE)
_DATAFLOW = pltpu.SideEffectType.DATAFLOW_SIDE_EFFECTING
TOKEN_SHAPE = (SUBLANES, LANES)


def _peers(x, y, c):
    flip = lambda v, bit: 1 - v if bit else v
    return [(flip(x, r >> 2 & 1), flip(y, r >> 1 & 1), flip(c, r & 1)) for r in range(1, 8)]


def _scatter_plan(x, y, c):
    return [(4 * px + 2 * py + pc, k, (px, py, pc)) for k, (px, py, pc) in enumerate(_peers(x, y, c))]


def _exchange_copies(plan, src_ref, land_ref, send_sems, recv_sems):
    C = land_ref.shape[-1]
    block = (lambda b: src_ref.at[b]) if len(src_ref.shape) == len(land_ref.shape) else (
        lambda b: src_ref.at[:, pl.ds(b * C, C)])
    return [pltpu.make_async_remote_copy(
        src_ref=block(blk), dst_ref=land_ref.at[slot], send_sem=send_sems.at[k], recv_sem=recv_sems.at[k],
        device_id=peer, device_id_type=MESH) for k, (blk, slot, peer) in enumerate(plan(*_place()))]


def _exchange_start(src, n_slots, block_shape, plan, name):
    land_shape = (n_slots,) + tuple(block_shape)
    n = len(plan(0, 0, 0))

    def body(src_ref, land_ref, send_sems, recv_sems, src_thru, land_thru, token):
        for cp in _exchange_copies(plan, src_ref, land_ref, send_sems, recv_sems):
            cp.start()
        token[...] = jnp.zeros_like(token)

    return pl.pallas_call(
        body, name=name,
        out_shape=(pltpu.SemaphoreType.DMA((n,)), pltpu.SemaphoreType.DMA((n,)), pltpu.HBM(src.shape, src.dtype),
                   pltpu.HBM(land_shape, src.dtype), jax.ShapeDtypeStruct(TOKEN_SHAPE, F32)),
        in_specs=(_HBM, _HBM), out_specs=(_SEM, _SEM, _HBM, _HBM, pl.BlockSpec(memory_space=pltpu.VMEM)),
        input_output_aliases={0: 2, 1: 3}, compiler_params=pltpu.CompilerParams(has_side_effects=_DATAFLOW),
    )(pltpu.with_memory_space_constraint(src, pltpu.HBM),
      pltpu.with_memory_space_constraint(lax.empty(land_shape, src.dtype), pltpu.HBM))


def _exchange_wait(handle, plan, after, name):
    send_sems, recv_sems, src_thru, land_thru, _ = handle

    def body(src_ref, land_ref, send_sems, recv_sems, after_ref, src_dead, got_ref):
        for cp in _exchange_copies(plan, src_ref, land_ref, send_sems, recv_sems):
            cp.wait_send()
            cp.wait_recv()

    return pl.pallas_call(
        body, name=name,
        out_shape=(pltpu.HBM(src_thru.shape, src_thru.dtype), pltpu.HBM(land_thru.shape, land_thru.dtype)),
        in_specs=(_HBM, _HBM, _SEM, _SEM, pl.BlockSpec(memory_space=pl.ANY)), out_specs=(_HBM, _HBM),
        input_output_aliases={0: 0, 1: 1}, compiler_params=pltpu.CompilerParams(has_side_effects=_DATAFLOW),
    )(src_thru, land_thru, send_sems, recv_sems, after)[1]


def _gather_plan(x, y, c):
    return [(0, 4 * x + 2 * y + c, peer) for peer in _peers(x, y, c)]


class _LateGather:
    def __init__(self, shard, name):
        self.shard, self.name = shard, name
        self.handle = _exchange_start(shard[None], 8, shard.shape, _gather_plan, name + "_start")

    def get(self, after):
        land = _exchange_wait(self.handle, _gather_plan, after, self.name + "_wait")
        x, y, c = _place()
        return lax.dynamic_update_slice(land, self.shard[None], (4 * x + 2 * y + c, 0, 0))


class _GradExchange:
    def __init__(self, me, layouts):
        self.me, self.layouts, self.pending = me, layouts, {}

    def start(self, tag, grad):
        g8 = self.layouts[tag](grad)
        block_shape = g8.shape[1:] if g8.ndim == 3 else (g8.shape[0], g8.shape[1] // 8)
        handle = _exchange_start(g8.astype(BF16), 7, block_shape, _scatter_plan, "rs_start_" + tag)
        self.pending[tag] = (g8, handle)
        return handle[4]

    def finish(self, tag, w, m, v, after):
        g8, handle = self.pending.pop(tag)
        got = _exchange_wait(handle, _scatter_plan, after, "rs_wait_" + tag)
        return _adam_sharded(w, m, v, g8, got, self.me, "adam_" + tag)


def _local_grads(x, p0, p1, target, wt_zs, wt_a, wt_qkv, late, conv_a, conv_b,
                 a_log, dt_bias, gdn_gain, lower_bounds, hgrn_gain, ln_g, ln_b, on_grad=None):
    H = a_log.shape[1]
    pad_small = ((0, 0), (H, LANES - 2 * H))
    alog_row = jnp.pad(a_log, pad_small)
    dtb_row = jnp.pad(dt_bias, pad_small)

    x16 = x.astype(BF16)
    proj_zs = _matmul(x16, wt_zs, "nt", "proj_even_zs", after=late.started)
    proj_a = _matmul(x16, wt_a, "nt", "proj_even_a", after=late.started)
    proj_qkv = _matmul(x16, wt_qkv, "nt", "proj_even_qkv", after=late.started)
    y_a = _mixer_a_fwd(proj_a, conv_a)
    qkv = _conv_b_fwd(proj_qkv, conv_b)
    o2, s_gdn = _gdn_fwd(qkv, proj_zs, alog_row, dtb_row, gdn_gain, H)
    woute_a, woute_b = late.out_even(o2)
    wg, wpl = late.gate(o2)
    s_e = _matmul(o2, woute_b, "nn", "out_even_b", add=_matmul(y_a, woute_a, "nn", "out_even_a"))
    x2, x2_16 = _post_fwd(s_e, x, p0, ln_g[0:1], ln_b[0:1], wg[0], wpl[0], "post_even_fwd")
    wino, wouto = late.odd(s_e)
    nheads_o = wouto.shape[0] // HEAD
    proj_o = _matmul(x2_16, wino, "nn", "proj_odd")
    o4, s_hgrn = _hgrn_fwd(proj_o, lower_bounds, hgrn_gain, nheads_o)
    s_o = _matmul(o4, wouto, "nn", "out_odd")
    ds_o, dx2, dlng1, dlnb1, dwg1, dwpl1, loss = _post_bwd(
        s_o, x2, p1, ln_g[1:2], ln_b[1:2], wg[1], wpl[1], target, "post_odd_loss_bwd", True)
    do4 = _matmul(ds_o, wouto, "nt", "d_out_odd_act")
    grads = {}

    def emit(tag, grad):
        grads[tag] = grad
        return on_grad(tag, grad) if on_grad is not None else jnp.zeros(TOKEN_SHAPE, F32)

    tok = emit("w_out_odd", _matmul(o4, ds_o, "tn", "d_out_odd_w"))
    dproj_o, dlb, dhgain = _hgrn_bwd(proj_o, do4, s_hgrn, lower_bounds, hgrn_gain + tok[0:1], nheads_o)
    dx2 = _matmul(dproj_o, wino, "nt", "d_proj_odd_act", add=dx2)
    tok = emit("w_in_odd", _matmul(x2_16, dproj_o, "tn", "d_proj_odd_w"))
    ds_e, dx, dlng0, dlnb0, dwg0, dwpl0, _ = _post_bwd(
        s_e, x, p0, ln_g[0:1], ln_b[0:1] + tok[0:1, 0:1], wg[0], wpl[0], dx2, "post_even_bwd", False)
    tok = emit("w_pl_gate", jnp.stack([dwg0, dwg1])) + emit("w_pl", jnp.stack([dwpl0, dwpl1]))
    dy_a = _matmul(ds_e, woute_a, "nt", "d_out_even_a_act")
    do2 = _matmul(ds_e, woute_b, "nt", "d_out_even_b_act")
    dwoute_a = _matmul(y_a, ds_e, "tn", "d_out_even_a_w")
    dwoute_b = _matmul(o2, ds_e, "tn", "d_out_even_b_w")
    tok = tok + emit("w_out_even", jnp.concatenate([dwoute_a, dwoute_b], axis=0))
    dqkv, dproj_zs, dalog, ddtb, dggain = _gdn_bwd(qkv, proj_zs, do2, s_gdn, alog_row, dtb_row, gdn_gain + tok[0:1], H)
    dproj_qkv, dconv_b = _conv_b_bwd(proj_qkv, dqkv, conv_b)
    dproj_a, dconv_a = _mixer_a_bwd(proj_a, dy_a, conv_a)
    emit("conv", (dconv_a[:conv_a.shape[0]], dconv_b[:conv_b.shape[0]]))
    tok = emit("w_in_even", (_matmul(dproj_zs, x16, "tn", "d_proj_even_zs_w"), _matmul(dproj_a, x16, "tn", "d_proj_even_a_w"),
                             _matmul(dproj_qkv, x16, "tn", "d_proj_even_qkv_w")))
    dx = _matmul(dproj_zs, wt_zs, "nn", "d_proj_even_zs_act", add=dx, after=tok)
    dx = _matmul(dproj_a, wt_a, "nn", "d_proj_even_a_act", add=dx)
    dx = _matmul(dproj_qkv, wt_qkv, "nn", "d_proj_even_qkv_act", add=dx)
    grads.update(
        loss=loss[0, 0], grad_x=dx, a_log=dalog[:, H:2 * H], dt_bias=ddtb[:, H:2 * H], gdn_gain=dggain,
        lower_bounds=dlb, hgrn_gain=dhgain, ln_g=jnp.concatenate([dlng0, dlng1], axis=0),
        ln_b=jnp.concatenate([dlnb0, dlnb1], axis=0))
    return grads


def _pad_rows(a, rows):
    return jnp.pad(a, ((0, rows - a.shape[0]), (0, 0)))


def _pack_small(a_log, dt_bias, gdn_gain, lower_bounds, hgrn_gain, ln_g, ln_b):
    lane_pad = lambda a: _pad_rows(jnp.pad(a, ((0, 0), (0, LANES - a.shape[1]))), SUBLANES)
    parts = [lane_pad(a_log), lane_pad(dt_bias), lane_pad(gdn_gain), lower_bounds.reshape(-1, LANES),
             lane_pad(hgrn_gain), ln_g.reshape(-1, LANES), ln_b.reshape(-1, LANES)]
    packed = jnp.concatenate(parts, axis=0)
    assert packed.shape[0] == SMALL_ROWS, packed.shape
    return packed


def _unpack_small(packed, shapes):
    out, r = [], 0
    for shp in shapes:
        n = shp[0] * shp[1]
        if n < LANES * SUBLANES and shp[1] <= LANES:
            out.append(packed[r:r + shp[0], :shp[1]])
            r += SUBLANES
        else:
            rows = n // LANES
            out.append(packed[r:r + rows].reshape(shp))
            r += rows
    return out


def _split_in_even(wt_full, AW, HW, H):
    D = wt_full.shape[1]
    n_a = 4 * AW
    n_main = n_a + 3 * HW
    wt_zs = jnp.concatenate([wt_full[n_main:n_main + HW], wt_full[n_main + HW:],
                             jnp.zeros((LANES - 2 * H, D), wt_full.dtype)], axis=0)
    wt_a = wt_full[:n_a].reshape(4, AW // MIXER_LANES, MIXER_LANES, D).transpose(1, 0, 2, 3).reshape(n_a, D)
    return wt_zs, wt_a, wt_full[n_a:n_main]


def _join_in_even(dt_zs, dt_a, dt_qkv, AW, HW, H):
    D = dt_a.shape[1]
    a_nat = dt_a.reshape(AW // MIXER_LANES, 4, MIXER_LANES, D).transpose(1, 0, 2, 3).reshape(4 * AW, D)
    return jnp.concatenate([a_nat, dt_qkv, dt_zs[:HW], dt_zs[HW:HW + 2 * H]], axis=0)


def kernel(x, p, w_in_even, conv_a_w, conv_b_w, a_log, dt_bias, gdn_norm_g, w_out_even, w_in_odd, lower_bounds, hgrn_norm_g, w_out_odd, ln_g, ln_b, w_pl, w_pl_gate, loss_target, m_w_in_even, m_conv_a_w, m_conv_b_w, m_a_log, m_dt_bias, m_gdn_norm_g, m_w_out_even, m_w_in_odd, m_lower_bounds, m_hgrn_norm_g, m_w_out_odd, m_ln_g, m_ln_b, m_w_pl, m_w_pl_gate, v_w_in_even, v_conv_a_w, v_conv_b_w, v_a_log, v_dt_bias, v_gdn_norm_g, v_w_out_even, v_w_in_odd, v_lower_bounds, v_hgrn_norm_g, v_w_out_odd, v_ln_g, v_ln_b, v_w_pl, v_w_pl_gate):
    xi, yi, ci = _place()
    me = jnp.reshape(4 * xi + 2 * yi + ci, (1,)).astype(jnp.int32)
    D = x.shape[2]
    H = a_log.shape[1]
    HW = H * HEAD
    AW = conv_a_w.shape[2] * 8
    OW = w_out_odd.shape[1] * 8
    PD = w_pl.shape[1]
    ka, kb = conv_a_w.shape[1], conv_b_w.shape[1]
    ca, cb = conv_a_w.shape[2], conv_b_w.shape[2]
    gw = HGRN_HEADS_PER_STEP * HEAD
    ngrp = OW // gw

    transposed = lambda a: jnp.transpose(a, (0, 2, 1))
    g_ine = _all_gather(transposed(w_in_even)[0].astype(BF16), "ag_w_in_even")
    wt_zs, wt_a, wt_qkv = _split_in_even(g_ine.reshape(-1, D), AW, HW, H)
    behind = lambda shard, dep: lax.optimization_barrier((shard, dep))[0]
    late_oute = _LateGather(behind(w_out_even[0].astype(BF16), g_ine), "ag_w_out_even")
    late_gate = _LateGather(behind(w_pl_gate.astype(BF16).reshape(-1, D), late_oute.handle[4]), "ag_w_pl_gate")
    late_pl = _LateGather(behind(w_pl.astype(BF16).reshape(DEPTH * PD, -1), late_gate.handle[4]), "ag_w_pl")
    late_ino = _LateGather(behind(w_in_odd[0].astype(BF16), late_pl.handle[4]), "ag_w_in_odd")
    late_outo = _LateGather(behind(w_out_odd[0].astype(BF16), late_ino.handle[4]), "ag_w_out_odd")

    class _Late:
        started = sum(g.handle[4] for g in (late_oute, late_gate, late_pl, late_ino, late_outo))

        @staticmethod
        def out_even(after):
            woute = late_oute.get(after).reshape(-1, D)
            return woute[:AW], woute[AW:]

        @staticmethod
        def gate(after):
            g_gate, g_pl = late_gate.get(after), late_pl.get(after)
            return (g_gate.reshape(8, DEPTH, D // 8, D).transpose(1, 0, 2, 3).reshape(DEPTH, D, D),
                    g_pl.reshape(8, DEPTH, PD, D // 8).transpose(1, 2, 0, 3).reshape(DEPTH, PD, D))

        @staticmethod
        def odd(after):
            g_ino = late_ino.get(after)
            wino = jnp.transpose(g_ino, (1, 0, 2)).reshape(D, 4, ngrp, gw).transpose(0, 2, 1, 3).reshape(D, 4 * OW)
            return wino, late_outo.get(after).reshape(-1, D)

    taps = jnp.concatenate([_pad_rows(conv_a_w[0], SUBLANES), _pad_rows(conv_b_w[0], SUBLANES)], axis=1)
    g_taps = _all_gather(taps, "ag_conv")
    conv_a = jnp.transpose(g_taps[:, :ka, :ca], (1, 0, 2)).reshape(ka, 8 * ca)
    conv_b = jnp.transpose(g_taps[:, :kb, ca:], (1, 0, 2)).reshape(kb, 8 * cb)

    sh = w_in_even.shape[2]
    tap_blocks = lambda g, width: _pad_rows(g, SUBLANES).reshape(SUBLANES, 8, width).transpose(1, 0, 2)
    owner_layout = {
        "w_in_even": lambda g: _join_in_even(*g, AW, HW, H).reshape(8, sh, D),
        "w_in_odd": lambda g: g.reshape(D, ngrp, 4, gw).transpose(0, 2, 1, 3).reshape(D, 4 * OW),
        "w_out_even": lambda g: g.reshape(8, -1, D),
        "w_out_odd": lambda g: g.reshape(8, -1, D),
        "w_pl_gate": lambda g: g.reshape(DEPTH, 8, D // 8, D).transpose(1, 0, 2, 3).reshape(8, DEPTH * D // 8, D),
        "w_pl": lambda g: g.reshape(DEPTH, PD, 8, D // 8).transpose(2, 0, 1, 3).reshape(8, DEPTH * PD, D // 8),
        "conv": lambda g: jnp.concatenate([tap_blocks(g[0], ca), tap_blocks(g[1], cb)], axis=2),
    }
    exchange = _GradExchange(me, owner_layout)
    gr = _local_grads(x[0], p[0, 0], p[1, 0], loss_target[0], wt_zs, wt_a, wt_qkv, _Late, conv_a, conv_b,
                      a_log, dt_bias, gdn_norm_g, lower_bounds, hgrn_norm_g, ln_g, ln_b, on_grad=exchange.start)

    last = gr["grad_x"]
    pack_taps = lambda a, b: jnp.concatenate([_pad_rows(a[0], SUBLANES), _pad_rows(b[0], SUBLANES)], axis=1)[None]
    o_outo = exchange.finish("w_out_odd", w_out_odd, m_w_out_odd, v_w_out_odd, last)
    o_ino = exchange.finish("w_in_odd", w_in_odd, m_w_in_odd, v_w_in_odd, last)
    o_gate = exchange.finish("w_pl_gate", w_pl_gate, m_w_pl_gate, v_w_pl_gate, last)
    o_pl = exchange.finish("w_pl", w_pl, m_w_pl, v_w_pl, last)
    o_oute = exchange.finish("w_out_even", w_out_even, m_w_out_even, v_w_out_even, last)
    o_taps = exchange.finish("conv", taps[None], pack_taps(m_conv_a_w, m_conv_b_w), pack_taps(v_conv_a_w, v_conv_b_w), last)
    others_done = sum(o[1][0, 0:1, 0:1] for o in (o_outo, o_ino, o_gate, o_pl, o_oute, o_taps))
    rows_first = lambda a: jnp.transpose(a, (2, 0, 1))
    o_ine = [jnp.transpose(o, (1, 2, 0)) for o in exchange.finish(
        "w_in_even", rows_first(w_in_even), rows_first(m_w_in_even), rows_first(v_w_in_even), others_done)]

    small_g = _pack_small(gr["a_log"], gr["dt_bias"], gr["gdn_gain"], gr["lower_bounds"], gr["hgrn_gain"],
                          gr["ln_g"], gr["ln_b"])
    small_g = small_g.at[0, LANES - 1].set(gr["loss"])
    o_small = _adam_replicated(
        _pack_small(a_log, dt_bias, gdn_norm_g, lower_bounds, hgrn_norm_g, ln_g, ln_b),
        _pack_small(m_a_log, m_dt_bias, m_gdn_norm_g, m_lower_bounds, m_hgrn_norm_g, m_ln_g, m_ln_b),
        _pack_small(v_a_log, v_dt_bias, v_gdn_norm_g, v_lower_bounds, v_hgrn_norm_g, v_ln_g, v_ln_b),
        _all_gather(small_g, "ag_small_grads"))
    small_shapes = [a_log.shape, dt_bias.shape, gdn_norm_g.shape, lower_bounds.shape, hgrn_norm_g.shape,
                    ln_g.shape, ln_b.shape]

    def leaves(kind):
        s_alog, s_dt, s_gg, s_lb, s_hg, s_lng, s_lnb = _unpack_small(o_small[kind], small_shapes)
        t = o_taps[kind]
        return [o_ine[kind], t[:, :ka, :ca], t[:, :kb, ca:], s_alog, s_dt, s_gg, o_oute[kind],
                o_ino[kind], s_lb, s_hg, o_outo[kind], s_lng, s_lnb, o_pl[kind], o_gate[kind]]

    return (o_small[0][0, LANES - 1], gr["grad_x"][None], *leaves(0), *leaves(1), *leaves(2), *leaves(3))
```

```python
import functools

import jax
import jax.numpy as jnp
from jax import lax
from jax.experimental import pallas as pl
from jax.experimental.pallas import tpu as pltpu

F32 = jnp.float32
BF16 = jnp.bfloat16
MESH = pl.DeviceIdType.MESH
AXES = ("x", "y", "c")

LANES = 128
SUBLANES = 8
HEAD = 128
GDN_CHUNK = 64
HGRN_CHUNK = 64
HGRN_SUB = 16
HGRN_HEADS_PER_STEP = 16
NORM_EPS = 1e-5
DEPTH = 2
ALPHA = (2.0 * DEPTH) ** 0.25
EXP_CLAMP = 80.0
ADAM_LR, ADAM_B1, ADAM_B2, ADAM_EPS, ADAM_WD, ADAM_STEP = 0.001, 0.9, 0.999, 1e-08, 0.01, 10
VMEM_LIMIT = 56 * 1024 * 1024
MATMUL_VMEM = 36 * 1024 * 1024
ROW_TILE = 512
MIXER_LANES = 256
CONV_LANES = 512
POST_TILE = 512
SMALL_ROWS = 96

_NOBATCH, _BATCH0 = ((), ()), ((0,), (0,))
_DIMS = {"nn": (((1,), (0,)), _NOBATCH), "nt": (((1,), (1,)), _NOBATCH), "tn": (((0,), (0,)), _NOBATCH),
         "bnn": (((2,), (1,)), _BATCH0), "bnt": (((2,), (2,)), _BATCH0), "btn": (((1,), (1,)), _BATCH0)}


def _params(**kw):
    return pltpu.CompilerParams(vmem_limit_bytes=VMEM_LIMIT, **kw)


def _dot_raw(a, b, kind, hi):
    if hi:
        return lax.dot_general(a, b, _DIMS[kind], precision=lax.Precision.HIGHEST, preferred_element_type=F32)
    return lax.dot_general(a.astype(BF16), b.astype(BF16), _DIMS[kind], preferred_element_type=F32)


@functools.partial(jax.custom_vjp, nondiff_argnums=(2, 3))
def mdot(a, b, kind, hi):
    return _dot_raw(a, b, kind, hi)


def _mdot_fwd(a, b, kind, hi):
    return _dot_raw(a, b, kind, hi), (a, b)


def _mdot_bwd(kind, hi, res, g):
    a, b = res
    pre, base = kind[:-2], kind[-2:]
    if base == "nn":
        return _dot_raw(g, b, pre + "nt", hi), _dot_raw(a, g, pre + "tn", hi)
    if base == "nt":
        return _dot_raw(g, b, pre + "nn", hi), _dot_raw(g, a, pre + "tn", hi)
    return _dot_raw(b, g, pre + "nt", hi), _dot_raw(a, g, pre + "nn", hi)


mdot.defvjp(_mdot_fwd, _mdot_bwd)


def _rows(x, lo, hi):
    return _take_rows(x, lo, hi, x.shape[-2])


@functools.partial(jax.custom_vjp, nondiff_argnums=(1, 2, 3))
def _take_rows(x, lo, hi, n):
    return x[..., lo:hi, :]


def _take_rows_fwd(x, lo, hi, n):
    return x[..., lo:hi, :], None


def _take_rows_bwd(lo, hi, n, _, g):
    parts = []
    if lo > 0:
        parts.append(jnp.zeros(g.shape[:-2] + (lo, g.shape[-1]), g.dtype))
    parts.append(g)
    if n - hi > 0:
        parts.append(jnp.zeros(g.shape[:-2] + (n - hi, g.shape[-1]), g.dtype))
    return (jnp.concatenate(parts, axis=-2) if len(parts) > 1 else g,)


_take_rows.defvjp(_take_rows_fwd, _take_rows_bwd)


def _heads_of(wide, nheads):
    return jnp.stack([wide[:, h * HEAD:(h + 1) * HEAD] for h in range(nheads)], axis=0)


def _wide_of(x):
    return jnp.concatenate([x[h] for h in range(x.shape[0])], axis=1)


@functools.partial(jax.custom_vjp, nondiff_argnums=(1,))
def to_heads(wide, nheads):
    return _heads_of(wide, nheads)


to_heads.defvjp(lambda wide, nheads: (_heads_of(wide, nheads), None), lambda nheads, _, g: (_wide_of(g),))


@jax.custom_vjp
def to_wide(x):
    return _wide_of(x)


to_wide.defvjp(lambda x: (_wide_of(x), None), lambda _, g: (_heads_of(g, g.shape[1] // HEAD),))


def _sigmoid(x):
    return jax.nn.sigmoid(x)


def _silu(x):
    return x * _sigmoid(x)


def _dsilu(x):
    s = _sigmoid(x)
    return s * (1.0 + x * (1.0 - s))


def _log1p(u):
    return jnp.where(u < 1e-4, u * (1.0 - 0.5 * u), jnp.log(1.0 + u))


def _softplus(x):
    return jnp.maximum(x, 0.0) + _log1p(jnp.exp(-jnp.abs(x)))


def _rms_gate(o, gain, z):
    return o * lax.rsqrt(jnp.mean(o * o, axis=-1, keepdims=True) + NORM_EPS) * gain * _silu(z)


def _l2n(x):
    return x * lax.rsqrt(jnp.sum(x * x, axis=-1, keepdims=True) + 1e-6)


def _split_dot_raw(m, x, kind):
    mb = m.astype(BF16)
    hi = x.astype(BF16)
    lo = (x - hi.astype(F32)).astype(BF16)
    dims = _DIMS[kind]
    return (lax.dot_general(mb, hi, dims, preferred_element_type=F32)
            + lax.dot_general(mb, lo, dims, preferred_element_type=F32))


@jax.custom_vjp
def mask_dot(m, x):
    return _split_dot_raw(m, x, "nn")


def _mask_dot_fwd(m, x):
    return _split_dot_raw(m, x, "nn"), m


def _mask_dot_bwd(m, g):
    return jnp.zeros_like(m), _split_dot_raw(m, g, "tn")


mask_dot.defvjp(_mask_dot_fwd, _mask_dot_bwd)


def _neumann_rest(low):
    n = low.shape[-1]
    rest = -low
    power = low
    span = 2
    while span < n:
        power = _dot_raw(power, power, "bnn", False)
        rest = rest + power + _dot_raw(rest, power, "bnn", False)
        span *= 2
    return rest


@jax.custom_vjp
def _unit_lower_inverse_minus_eye(low):
    return _neumann_rest(low)


def _inverse_fwd(low):
    rest = _neumann_rest(low)
    return rest, rest


def _inverse_bwd(rest, g):
    left = g + _dot_raw(rest, g, "btn", False)
    return (-(left + _dot_raw(left, rest, "bnt", False)),)


_unit_lower_inverse_minus_eye.defvjp(_inverse_fwd, _inverse_bwd)


def _gdn_step(S, q, k, v, z, small, alog, dtb, gain):
    H = S.shape[0]
    C = GDN_CHUNK
    row = lax.broadcasted_iota(jnp.int32, (C, C), 0)
    col = lax.broadcasted_iota(jnp.int32, (C, C), 1)
    tril, strict, eye = (row >= col)[None], (row > col)[None], (row == col)[None]
    head = lax.broadcasted_iota(jnp.int32, (H, 1, LANES), 0)
    lane = lax.broadcasted_iota(jnp.int32, (H, 1, LANES), 2)
    rowc = lax.broadcasted_iota(jnp.int32, (1, C, 1), 1)
    beta_all = _sigmoid(small)
    g_all = -jnp.exp(alog) * _softplus(small + dtb)
    gc_all = mask_dot((row >= col).astype(F32), g_all)
    beta = jnp.sum(jnp.where(lane == head, beta_all[None], 0.0), axis=-1, keepdims=True)
    gc = jnp.sum(jnp.where(lane == head + H, gc_all[None], 0.0), axis=-1, keepdims=True)
    gc_row = jnp.sum(jnp.where(eye, gc, 0.0), axis=1, keepdims=True)
    decay = jnp.where(tril, jnp.exp(jnp.where(tril, gc - gc_row, 0.0)), 0.0)
    g_last = jnp.sum(jnp.where(rowc == C - 1, gc, 0.0), axis=1, keepdims=True)
    qn = _l2n(q) * (HEAD ** -0.5)
    kn = _l2n(k)
    kb = kn * beta
    low = jnp.where(strict, beta * mdot(kn, kn, "bnt", False) * decay, 0.0)
    inv_rest = _unit_lower_inverse_minus_eye(low)
    eg = jnp.exp(gc)
    vb, kbe = v * beta, kb * eg
    u = vb + mdot(inv_rest, vb, "bnn", False)
    w = kbe + mdot(inv_rest, kbe, "bnn", False)
    attn = mdot(qn, kn, "bnt", False) * decay
    v_new = u - mdot(w, S, "bnn", False)
    o = mdot(qn * eg, S, "bnn", False) + mdot(attn, v_new, "bnn", False)
    k_dec = kn * jnp.exp(g_last - gc)
    return _rms_gate(o, gain, z), S * jnp.exp(g_last) + mdot(k_dec, v_new, "btn", False)


def _hgrn_step(St, qr, fr, vi, z, lb0, lb1, gain):
    H = St.shape[0]
    C, SB = HGRN_CHUNK, HGRN_SUB
    row = lax.broadcasted_iota(jnp.int32, (C, C), 0)
    col = lax.broadcasted_iota(jnp.int32, (C, C), 1)
    blk_start = row - (row & (SB - 1))
    in_blk_f = ((row >= col) & (col >= blk_start)).astype(F32)
    before_f = (col < blk_start).astype(F32)
    sums_f = jnp.concatenate([in_blk_f, before_f], axis=0)
    m = jnp.maximum(lb0, lb1)
    e0, e1 = jnp.exp(lb0 - m), jnp.exp(lb1 - m)
    lb = e1 / (e0 + e1)
    f = lb + (1.0 - lb) * _sigmoid(fr)
    q = _silu(qr)
    k = 1.0 - f
    logf = jnp.log(f)
    sums = mask_dot(sums_f, to_wide(logf))
    inner, start = to_heads(_rows(sums, 0, C), H), to_heads(_rows(sums, C, 2 * C), H)
    b = start + inner
    b_last = jnp.sum(logf, axis=1, keepdims=True)
    o = mdot(q * jnp.exp(b), St, "bnt", False)
    qt = q * jnp.exp(inner)
    parts = []
    for blk in range(C // SB):
        lo, n = blk * SB, (blk + 1) * SB
        ref = jnp.concatenate([_rows(start, lo, n)] * (blk + 1), axis=1)
        kt = _rows(k, 0, n) * jnp.exp(jnp.minimum(ref - _rows(b, 0, n), EXP_CLAMP))
        att = mdot(_rows(qt, lo, n), kt, "bnt", False)
        t_idx = lax.broadcasted_iota(jnp.int32, (1, SB, n), 1) + lo
        s_idx = lax.broadcasted_iota(jnp.int32, (1, SB, n), 2)
        att = jnp.where(s_idx <= t_idx, att, 0.0)
        parts.append(mdot(att, _rows(vi, 0, n), "bnn", False))
    o = o + jnp.concatenate(parts, axis=1)
    k_dec = k * jnp.exp(b_last - b)
    return _rms_gate(o, gain, z), St * jnp.exp(b_last) + mdot(vi, k_dec, "btn", False)


def _post_norm(s, x, g, b):
    r = ALPHA * x + s
    d = r - jnp.mean(r, axis=-1, keepdims=True)
    var = jnp.mean(d * d, axis=-1, keepdims=True)
    return d * lax.rsqrt(var + NORM_EPS) * g + b


def _post_gate(x1, gate_pre, pp):
    return x1 + pp * _sigmoid(gate_pre)


def _pick(dim, cands):
    for c in cands:
        if dim % c == 0:
            return c
    return dim


def _matmul_tiles(M, K, tn, a_bytes, b_bytes, has_add):
    for tk in (4096, 2048, 1536, 1152, 1024, 640, 512, 384, 256, 128):
        if K % tk:
            continue
        for tm in (2048, 1152, 1024, 512, 384, 256, 128):
            if M % tm:
                continue
            blocks = tm * tk * a_bytes + tk * tn * b_bytes + tm * tn * 4 * (2 if has_add else 1)
            if 2 * blocks + (tm * tn * 4 if tk < K else 0) <= MATMUL_VMEM and tm >= min(M, 1024):
                return tm, tk
    return _pick(M, (512, 256, 128)), _pick(K, (512, 256, 128))


def _matmul(a, b, kind, name, add=None, after=None, also_bf16=False):
    if kind == "nn":
        (M, K), N = a.shape, b.shape[1]
    elif kind == "nt":
        (M, K), N = a.shape, b.shape[0]
    else:
        (K, M), N = a.shape, b.shape[1]
    has_add = add is not None
    tn = _pick(N, (512, 640, 384, 256, 128))
    tm, tk = _matmul_tiles(M, K, tn, a.dtype.itemsize, b.dtype.itemsize, has_add)
    nk = K // tk
    a_spec = pl.BlockSpec((tk, tm), lambda i, j, k: (k, i)) if kind == "tn" else pl.BlockSpec((tm, tk), lambda i, j, k: (i, k))
    b_spec = pl.BlockSpec((tn, tk), lambda i, j, k: (j, k)) if kind == "nt" else pl.BlockSpec((tk, tn), lambda i, j, k: (k, j))
    o_spec = pl.BlockSpec((tm, tn), lambda i, j, k: (i, j))

    extra = ([add] if has_add else []) + ([after] if after is not None else [])
    extra_specs = ([o_spec] if has_add else []) + ([pl.BlockSpec(TOKEN_SHAPE, lambda i, j, k: (0, 0))] if after is not None else [])

    out_dtypes = (F32, BF16) if also_bf16 else (F32,)

    def body(a_ref, b_ref, *rest):
        outs = rest[len(extra):len(extra) + len(out_dtypes)]

        def write(val):
            if has_add:
                val = val + rest[0][...]
            for o_ref in outs:
                o_ref[...] = val.astype(o_ref.dtype)

        part = _dot_raw(a_ref[...], b_ref[...], kind, False)
        if nk == 1:
            write(part)
            return
        acc = rest[-1]
        kk = pl.program_id(2)

        @pl.when(kk == 0)
        def _():
            acc[...] = part

        @pl.when(kk > 0)
        def _():
            acc[...] += part

        @pl.when(kk == nk - 1)
        def _():
            write(acc[...])

    result = pl.pallas_call(
        body, name=name, grid=(M // tm, N // tn, nk),
        in_specs=[a_spec, b_spec] + extra_specs,
        out_specs=[o_spec] * len(out_dtypes), out_shape=[jax.ShapeDtypeStruct((M, N), dt) for dt in out_dtypes],
        scratch_shapes=[pltpu.VMEM((tm, tn), F32)] if nk > 1 else [],
        compiler_params=_params(dimension_semantics=("parallel", "parallel", "arbitrary")),
    )(a, b, *extra)
    return result if also_bf16 else result[0]


def _halo_specs(ts, nt, width, prev=True, main=True, nxt=True):
    per = ts // SUBLANES
    last8 = nt * per - 1
    specs = []
    if prev:
        specs.append(pl.BlockSpec((SUBLANES, width), lambda cb, i: (jnp.maximum(i * per - 1, 0), cb)))
    if main:
        specs.append(pl.BlockSpec((ts, width), lambda cb, i: (i, cb)))
    if nxt:
        specs.append(pl.BlockSpec((SUBLANES, width), lambda cb, i: (jnp.minimum((i + 1) * per, last8), cb)))
    return specs


def _taps(ext, ktaps, lo, size):
    return [ext[lo:lo + size] if j == 0 else pltpu.roll(ext, j, 0)[lo:lo + size] for j in range(ktaps)]


def _ahead(ext, j, size):
    n = ext.shape[0]
    return ext[:size] if j == 0 else pltpu.roll(ext, n - j, 0)[:size]


def _lane_block(ref, k):
    return ref[:, k * MIXER_LANES:(k + 1) * MIXER_LANES]


def _mixer_a_fwd(proj_a, conv_w):
    T = proj_a.shape[0]
    nblk = proj_a.shape[1] // (4 * MIXER_LANES)
    ts = min(ROW_TILE, T)
    nt = T // ts

    def body(pp, pm, w_ref, y_ref):
        i = pl.program_id(1)
        u_prev = jnp.where(i > 0, _lane_block(pp, 0) * _lane_block(pp, 1), 0.0)
        ext = jnp.concatenate([u_prev, _lane_block(pm, 0) * _lane_block(pm, 1)], axis=0)
        t0, t1, t2 = _taps(ext, 3, SUBLANES, ts)
        cv = w_ref[2:3, :] * t0 + w_ref[1:2, :] * t1 + w_ref[0:1, :] * t2
        y_ref[...] = (_lane_block(pm, 2) * cv * _silu(_lane_block(pm, 3))).astype(y_ref.dtype)

    return pl.pallas_call(
        body, name="mixer_a_fwd", grid=(nblk, nt),
        in_specs=_halo_specs(ts, nt, 4 * MIXER_LANES, nxt=False)
        + [pl.BlockSpec((conv_w.shape[0], MIXER_LANES), lambda cb, i: (0, cb))],
        out_specs=pl.BlockSpec((ts, MIXER_LANES), lambda cb, i: (i, cb)),
        out_shape=jax.ShapeDtypeStruct((T, nblk * MIXER_LANES), BF16), compiler_params=_params(),
    )(proj_a, proj_a, conv_w)


def _mixer_a_bwd(proj_a, dy, conv_w):
    T = proj_a.shape[0]
    nblk = proj_a.shape[1] // (4 * MIXER_LANES)
    ts = min(ROW_TILE, T)
    nt = T // ts
    kt = conv_w.shape[0]

    def body(pp, pm, pn, dym, dyn, w_ref, dp_ref, dw_ref):
        i = pl.program_id(1)
        hm, cm, bm, zm = (_lane_block(pm, k) for k in range(4))
        u_prev = jnp.where(i > 0, _lane_block(pp, 0) * _lane_block(pp, 1), 0.0)
        ext = jnp.concatenate([u_prev, hm * cm], axis=0)
        dy_ext = jnp.concatenate([dym[...], jnp.where(i < nt - 1, dyn[...], 0.0)], axis=0)
        b_ext = jnp.concatenate([bm, _lane_block(pn, 2)], axis=0)
        sz_ext = _silu(jnp.concatenate([zm, _lane_block(pn, 3)], axis=0))
        dcv_ext = dy_ext * b_ext * sz_ext
        w = [w_ref[j:j + 1, :] for j in range(kt)]
        du = sum(w[kt - 1 - j] * _ahead(dcv_ext, j, ts) for j in range(kt))
        taps = _taps(ext, kt, SUBLANES, ts)
        cv = sum(w[kt - 1 - j] * taps[j] for j in range(kt))
        for part, d in enumerate((du * cm, du * hm, dym[...] * cv * sz_ext[:ts], dym[...] * bm * cv * _dsilu(zm))):
            dp_ref[:, part * MIXER_LANES:(part + 1) * MIXER_LANES] = d.astype(dp_ref.dtype)
        dcv = dcv_ext[:ts]

        @pl.when(i == 0)
        def _():
            dw_ref[...] = jnp.zeros_like(dw_ref)

        for j in range(kt):
            dw_ref[j:j + 1, :] += jnp.sum(dcv * taps[kt - 1 - j], axis=0, keepdims=True)

    return pl.pallas_call(
        body, name="mixer_a_bwd", grid=(nblk, nt),
        in_specs=_halo_specs(ts, nt, 4 * MIXER_LANES) + _halo_specs(ts, nt, MIXER_LANES, prev=False)
        + [pl.BlockSpec((kt, MIXER_LANES), lambda cb, i: (0, cb))],
        out_specs=[pl.BlockSpec((ts, 4 * MIXER_LANES), lambda cb, i: (i, cb)),
                   pl.BlockSpec((SUBLANES, MIXER_LANES), lambda cb, i: (0, cb))],
        out_shape=[jax.ShapeDtypeStruct(proj_a.shape, BF16),
                   jax.ShapeDtypeStruct((SUBLANES, nblk * MIXER_LANES), F32)],
        compiler_params=_params(),
    )(proj_a, proj_a, proj_a, dy, dy, conv_w)


def _conv_b_fwd(raw, conv_w):
    T = raw.shape[0]
    nblk = raw.shape[1] // CONV_LANES
    ts = min(ROW_TILE, T)
    nt = T // ts
    kt = conv_w.shape[0]

    def body(rp, rm, w_ref, y_ref):
        i = pl.program_id(1)
        ext = jnp.concatenate([jnp.where(i > 0, rp[...], 0.0), rm[...]], axis=0)
        taps = _taps(ext, kt, SUBLANES, ts)
        y_ref[...] = _silu(sum(w_ref[kt - 1 - j:kt - j, :] * taps[j] for j in range(kt)))

    return pl.pallas_call(
        body, name="conv_b_fwd", grid=(nblk, nt),
        in_specs=_halo_specs(ts, nt, CONV_LANES, nxt=False) + [pl.BlockSpec((kt, CONV_LANES), lambda cb, i: (0, cb))],
        out_specs=pl.BlockSpec((ts, CONV_LANES), lambda cb, i: (i, cb)),
        out_shape=jax.ShapeDtypeStruct(raw.shape, F32), compiler_params=_params(),
    )(raw, raw, conv_w)


def _conv_b_bwd(raw, dy, conv_w):
    T = raw.shape[0]
    nblk = raw.shape[1] // CONV_LANES
    ts = min(ROW_TILE, T)
    nt = T // ts
    kt = conv_w.shape[0]

    def body(rp, rm, rn, dym, dyn, w_ref, dr_ref, dw_ref):
        i = pl.program_id(1)
        ext = jnp.concatenate([jnp.where(i > 0, rp[...], 0.0), rm[...], rn[...]], axis=0)
        w = [w_ref[j:j + 1, :] for j in range(kt)]
        taps = _taps(ext, kt, SUBLANES, ts + SUBLANES)
        xc_ext = sum(w[kt - 1 - j] * taps[j] for j in range(kt))
        dy_ext = jnp.concatenate([dym[...], jnp.where(i < nt - 1, dyn[...], 0.0)], axis=0)
        dxc_ext = dy_ext * _dsilu(xc_ext)
        dr_ref[...] = sum(w[kt - 1 - j] * _ahead(dxc_ext, j, ts) for j in range(kt)).astype(dr_ref.dtype)
        dxc = dxc_ext[:ts]

        @pl.when(i == 0)
        def _():
            dw_ref[...] = jnp.zeros_like(dw_ref)

        for j in range(kt):
            dw_ref[j:j + 1, :] += jnp.sum(dxc * taps[kt - 1 - j][:ts], axis=0, keepdims=True)

    return pl.pallas_call(
        body, name="conv_b_bwd", grid=(nblk, nt),
        in_specs=_halo_specs(ts, nt, CONV_LANES) + _halo_specs(ts, nt, CONV_LANES, prev=False)
        + [pl.BlockSpec((kt, CONV_LANES), lambda cb, i: (0, cb))],
        out_specs=[pl.BlockSpec((ts, CONV_LANES), lambda cb, i: (i, cb)),
                   pl.BlockSpec((SUBLANES, CONV_LANES), lambda cb, i: (0, cb))],
        out_shape=[jax.ShapeDtypeStruct(raw.shape, BF16), jax.ShapeDtypeStruct((SUBLANES, nblk * CONV_LANES), F32)],
        compiler_params=_params(),
    )(raw, raw, raw, dy, dy, conv_w)


def _split_heads(ref, base, nheads, rows=slice(None)):
    return jnp.stack([ref[rows, base + h * HEAD: base + (h + 1) * HEAD] for h in range(nheads)], axis=0)


def _store_heads(ref, base, x, rows=slice(None), accumulate=False):
    for h in range(x.shape[0]):
        lanes = slice(base + h * HEAD, base + (h + 1) * HEAD)
        if accumulate:
            ref[rows, lanes] += x[h]
        else:
            ref[rows, lanes] = x[h].astype(ref.dtype)


def _gdn_fwd(qkv, proj_zs, alog, dtb, gain, H):
    T = qkv.shape[0]
    C, HW = GDN_CHUNK, H * HEAD
    nc = T // C
    zw = HW + LANES

    def body(qkv_ref, zs_ref, alog_ref, dtb_ref, gain_ref, o_ref, sall_ref, s_scr):
        @pl.when(pl.program_id(0) == 0)
        def _():
            s_scr[...] = jnp.zeros_like(s_scr)

        sall_ref[0] = s_scr[...]
        outs, states = _gdn_step(
            s_scr[...], _split_heads(qkv_ref, 0, H), _split_heads(qkv_ref, HW, H),
            _split_heads(qkv_ref, 2 * HW, H), _split_heads(zs_ref, 0, H), zs_ref[:, HW:HW + LANES],
            alog_ref[...], dtb_ref[...], gain_ref[...])
        _store_heads(o_ref, 0, outs)
        s_scr[...] = states

    row = pl.BlockSpec((1, LANES), lambda i: (0, 0))
    return pl.pallas_call(
        body, name="gdn_fwd", grid=(nc,),
        in_specs=[pl.BlockSpec((C, 3 * HW), lambda i: (i, 0)), pl.BlockSpec((C, zw), lambda i: (i, 0)), row, row, row],
        out_specs=[pl.BlockSpec((C, HW), lambda i: (i, 0)), pl.BlockSpec((1, H, HEAD, HEAD), lambda i: (i, 0, 0, 0))],
        out_shape=[jax.ShapeDtypeStruct((T, HW), BF16), jax.ShapeDtypeStruct((nc, H, HEAD, HEAD), F32)],
        scratch_shapes=[pltpu.VMEM((H, HEAD, HEAD), F32)], compiler_params=_params(),
    )(qkv, proj_zs, alog, dtb, gain)


def _gdn_bwd(qkv, proj_zs, do, s_all, alog, dtb, gain, H):
    T = qkv.shape[0]
    C, HW = GDN_CHUNK, H * HEAD
    nc = T // C
    zw = HW + LANES

    def body(qkv_ref, zs_ref, do_ref, sin_ref, alog_ref, dtb_ref, gain_ref,
             dqkv_ref, dzs_ref, dalog_ref, ddtb_ref, dgain_ref, ds_scr):
        @pl.when(pl.program_id(0) == 0)
        def _():
            ds_scr[...] = jnp.zeros_like(ds_scr)
            dalog_ref[...] = jnp.zeros_like(dalog_ref)
            ddtb_ref[...] = jnp.zeros_like(ddtb_ref)
            dgain_ref[...] = jnp.zeros_like(dgain_ref)

        primals = (sin_ref[0], _split_heads(qkv_ref, 0, H),
                   _split_heads(qkv_ref, HW, H), _split_heads(qkv_ref, 2 * HW, H), _split_heads(zs_ref, 0, H),
                   zs_ref[:, HW:HW + LANES], alog_ref[...], dtb_ref[...], gain_ref[...])
        _, vjp = jax.vjp(_gdn_step, *primals)
        dS, dq, dk, dv, dz, dsmall, dalog, ddtb, dgain = vjp((_split_heads(do_ref, 0, H), ds_scr[...]))
        ds_scr[...] = dS
        _store_heads(dqkv_ref, 0, dq)
        _store_heads(dqkv_ref, HW, dk)
        _store_heads(dqkv_ref, 2 * HW, dv)
        _store_heads(dzs_ref, 0, dz)
        dzs_ref[:, HW:HW + LANES] = dsmall.astype(dzs_ref.dtype)
        dalog_ref[...] += dalog
        ddtb_ref[...] += ddtb
        dgain_ref[...] += dgain

    row = pl.BlockSpec((1, LANES), lambda i: (0, 0))
    rev = lambda i: nc - 1 - i
    return pl.pallas_call(
        body, name="gdn_bwd", grid=(nc,),
        in_specs=[pl.BlockSpec((C, 3 * HW), lambda i: (rev(i), 0)), pl.BlockSpec((C, zw), lambda i: (rev(i), 0)),
                  pl.BlockSpec((C, HW), lambda i: (rev(i), 0)),
                  pl.BlockSpec((1, H, HEAD, HEAD), lambda i: (rev(i), 0, 0, 0)), row, row, row],
        out_specs=[pl.BlockSpec((C, 3 * HW), lambda i: (rev(i), 0)), pl.BlockSpec((C, zw), lambda i: (rev(i), 0)),
                   row, row, row],
        out_shape=[jax.ShapeDtypeStruct(qkv.shape, F32), jax.ShapeDtypeStruct(proj_zs.shape, BF16)]
        + [jax.ShapeDtypeStruct((1, LANES), F32)] * 3,
        scratch_shapes=[pltpu.VMEM((H, HEAD, HEAD), F32)], compiler_params=_params(),
    )(qkv, proj_zs, do, s_all, alog, dtb, gain)


def _hgrn_refs(proj_ref, lb_ref, HP):
    W = HP * HEAD
    return (_split_heads(proj_ref, 0, HP), _split_heads(proj_ref, W, HP), _split_heads(proj_ref, 2 * W, HP),
            _split_heads(proj_ref, 3 * W, HP), _split_heads(lb_ref, 0, HP, slice(0, 1)),
            _split_heads(lb_ref, 0, HP, slice(1, 2)))


def _hgrn_fwd(proj, lower_bounds, gain, nheads):
    T = proj.shape[0]
    C, HP = HGRN_CHUNK, HGRN_HEADS_PER_STEP
    ng, nc, W = nheads // HP, T // C, HP * HEAD

    def body(proj_ref, lb_ref, gain_ref, o_ref, sall_ref, s_scr):
        @pl.when(pl.program_id(1) == 0)
        def _():
            s_scr[...] = jnp.zeros_like(s_scr)

        sall_ref[0] = s_scr[...]
        qr, fr, vi, z, lb0, lb1 = _hgrn_refs(proj_ref, lb_ref, HP)
        outs, states = _hgrn_step(s_scr[...], qr, fr, vi, z, lb0, lb1, gain_ref[...])
        _store_heads(o_ref, 0, outs)
        s_scr[...] = states

    return pl.pallas_call(
        body, name="hgrn_fwd", grid=(ng, nc),
        in_specs=[pl.BlockSpec((C, 4 * W), lambda g, i: (i, g)), pl.BlockSpec((2, W), lambda g, i: (0, g)),
                  pl.BlockSpec((1, LANES), lambda g, i: (0, 0))],
        out_specs=[pl.BlockSpec((C, W), lambda g, i: (i, g)),
                   pl.BlockSpec((1, HP, HEAD, HEAD), lambda g, i: (i, g, 0, 0))],
        out_shape=[jax.ShapeDtypeStruct((T, nheads * HEAD), BF16), jax.ShapeDtypeStruct((nc, nheads, HEAD, HEAD), F32)],
        scratch_shapes=[pltpu.VMEM((HP, HEAD, HEAD), F32)], compiler_params=_params(),
    )(proj, lower_bounds, gain)


def _hgrn_bwd(proj, do, s_all, lower_bounds, gain, nheads):
    T = proj.shape[0]
    C, HP = HGRN_CHUNK, HGRN_HEADS_PER_STEP
    ng, nc, W = nheads // HP, T // C, HP * HEAD

    def body(proj_ref, do_ref, sin_ref, lb_ref, gain_ref, dproj_ref, dlb_ref, dgain_ref, ds_scr):
        first = pl.program_id(1) == 0

        @pl.when(first)
        def _():
            ds_scr[...] = jnp.zeros_like(ds_scr)
            dlb_ref[...] = jnp.zeros_like(dlb_ref)

        @pl.when(first & (pl.program_id(0) == 0))
        def _():
            dgain_ref[...] = jnp.zeros_like(dgain_ref)

        qr, fr, vi, z, lb0, lb1 = _hgrn_refs(proj_ref, lb_ref, HP)
        primals = (sin_ref[0], qr, fr, vi, z, lb0, lb1, gain_ref[...])
        _, vjp = jax.vjp(_hgrn_step, *primals)
        dS, dq, df, dv, dz, dlb0, dlb1, dgain = vjp((_split_heads(do_ref, 0, HP), ds_scr[...]))
        ds_scr[...] = dS
        for part, d in enumerate((dq, df, dv, dz)):
            _store_heads(dproj_ref, part * W, d)
        _store_heads(dlb_ref, 0, dlb0, slice(0, 1), accumulate=True)
        _store_heads(dlb_ref, 0, dlb1, slice(1, 2), accumulate=True)
        dgain_ref[...] += dgain

    rev = lambda i: nc - 1 - i
    return pl.pallas_call(
        body, name="hgrn_bwd", grid=(ng, nc),
        in_specs=[pl.BlockSpec((C, 4 * W), lambda g, i: (rev(i), g)), pl.BlockSpec((C, W), lambda g, i: (rev(i), g)),
                  pl.BlockSpec((1, HP, HEAD, HEAD), lambda g, i: (rev(i), g, 0, 0)),
                  pl.BlockSpec((2, W), lambda g, i: (0, g)), pl.BlockSpec((1, LANES), lambda g, i: (0, 0))],
        out_specs=[pl.BlockSpec((C, 4 * W), lambda g, i: (rev(i), g)), pl.BlockSpec((2, W), lambda g, i: (0, g)),
                   pl.BlockSpec((1, LANES), lambda g, i: (0, 0))],
        out_shape=[jax.ShapeDtypeStruct(proj.shape, BF16), jax.ShapeDtypeStruct(lower_bounds.shape, F32),
                   jax.ShapeDtypeStruct((1, LANES), F32)],
        scratch_shapes=[pltpu.VMEM((HP, HEAD, HEAD), F32)], compiler_params=_params(),
    )(proj, do, s_all, lower_bounds, gain)


def _post_specs(T):
    tr = min(POST_TILE, T)
    tile = lambda w: pl.BlockSpec((tr, w), lambda i: (i, 0))
    full = lambda r, w: pl.BlockSpec((r, w), lambda i: (0, 0))
    return tr, tile, full


def _post_fwd(s, x, p, g, b, wg, wpl, name):
    T, D = x.shape
    P = p.shape[1]
    tr, tile, full = _post_specs(T)

    def body(s_ref, x_ref, p_ref, g_ref, b_ref, wg_ref, wpl_ref, o_ref, o16_ref):
        x1 = _post_norm(s_ref[...], x_ref[...], g_ref[...], b_ref[...])
        xn = _post_gate(x1, _dot_raw(x1, wg_ref[...], "nn", False), _dot_raw(p_ref[...], wpl_ref[...], "nn", False))
        o_ref[...] = xn
        o16_ref[...] = xn.astype(BF16)

    return pl.pallas_call(
        body, name=name, grid=(T // tr,),
        in_specs=[tile(D), tile(D), tile(P), full(1, D), full(1, D), full(D, D), full(P, D)],
        out_specs=[tile(D), tile(D)],
        out_shape=[jax.ShapeDtypeStruct((T, D), F32), jax.ShapeDtypeStruct((T, D), BF16)], compiler_params=_params(),
    )(s, x, p, g, b, wg, wpl)


def _post_bwd(s, x, p, g, b, wg, wpl, dnext, name, with_loss):
    T, D = x.shape
    P = p.shape[1]
    tr, tile, full = _post_specs(T)

    def body(s_ref, x_ref, p_ref, g_ref, b_ref, wg_ref, wpl_ref, dn_ref,
             ds_ref, dx_ref, dg_ref, db_ref, dwg_ref, dwpl_ref, loss_ref):
        @pl.when(pl.program_id(0) == 0)
        def _():
            for r in (dg_ref, db_ref, dwg_ref, dwpl_ref, loss_ref):
                r[...] = jnp.zeros_like(r)

        x1, vjp_norm = jax.vjp(_post_norm, s_ref[...], x_ref[...], g_ref[...], b_ref[...])
        gate_pre = _dot_raw(x1, wg_ref[...], "nn", False)
        pp = _dot_raw(p_ref[...], wpl_ref[...], "nn", False)
        xn, vjp_gate = jax.vjp(_post_gate, x1, gate_pre, pp)
        if with_loss:
            err = xn - dn_ref[...]
            loss_ref[...] += 0.5 * jnp.sum(jnp.sum(err * err, axis=-1, keepdims=True), axis=0, keepdims=True) / D
            dn = err / D
        else:
            dn = dn_ref[...]
        dx1, dgp, dpp = vjp_gate(dn)
        dwg_ref[...] += _dot_raw(x1, dgp, "tn", False)
        dwpl_ref[...] += _dot_raw(p_ref[...], dpp, "tn", False)
        dx1 = dx1 + _dot_raw(dgp, wg_ref[...], "nt", False)
        ds, dx, dg, db = vjp_norm(dx1)
        ds_ref[...] = ds.astype(ds_ref.dtype)
        dx_ref[...] = dx
        dg_ref[...] += dg
        db_ref[...] += db

    return pl.pallas_call(
        body, name=name, grid=(T // tr,),
        in_specs=[tile(D), tile(D), tile(P), full(1, D), full(1, D), full(D, D), full(P, D), tile(D)],
        out_specs=[tile(D), tile(D), full(1, D), full(1, D), full(D, D), full(P, D), full(SUBLANES, LANES)],
        out_shape=[jax.ShapeDtypeStruct((T, D), BF16), jax.ShapeDtypeStruct((T, D), F32)]
        + [jax.ShapeDtypeStruct((1, D), F32)] * 2
        + [jax.ShapeDtypeStruct((D, D), F32), jax.ShapeDtypeStruct((P, D), F32),
           jax.ShapeDtypeStruct((SUBLANES, LANES), F32)],
        compiler_params=_params(),
    )(s, x, p, g, b, wg, wpl, dnext)


def _adam_math(w, g, m, v):
    m = ADAM_B1 * m + (1.0 - ADAM_B1) * g
    v = ADAM_B2 * v + (1.0 - ADAM_B2) * (g * g)
    m_hat = m / (1.0 - ADAM_B1 ** ADAM_STEP)
    v_hat = v / (1.0 - ADAM_B2 ** ADAM_STEP)
    return -ADAM_LR * (m_hat / (jnp.sqrt(v_hat) + ADAM_EPS) + ADAM_WD * w), m, v


def _shard_tiles(R, C):
    tr = _pick(R, (256, 128, 64, 32, 16, 8))
    return (tr, C) if tr < R or R % SUBLANES == 0 else (R, _pick(C, (256, 128)))


def _adam_sharded(w, m, v, g8, got, me, name):
    rows_apart = w.shape[1] == 1 and w.shape[0] > 1
    (L, R, C) = (1, w.shape[0], w.shape[2]) if rows_apart else w.shape
    tr, tc = (R, LANES) if rows_apart else _shard_tiles(R, C)
    nr, nc = R // tr, C // tc
    side_by_side = g8.ndim == 2

    def body(me_ref, w_ref, m_ref, v_ref, p_ref, *rest):
        got_refs, (g_ref, d_ref, mo_ref, vo_ref) = rest[:7], rest[7:]
        g = p_ref[...] if side_by_side else p_ref[0]
        for r in got_refs:
            g = g + r[0].astype(F32)
        if rows_apart:
            d, mn, vn = _adam_math(w_ref[:, 0, :], g, m_ref[:, 0, :], v_ref[:, 0, :])
            for ref, val in ((g_ref, g), (d_ref, d), (mo_ref, mn), (vo_ref, vn)):
                ref[:, 0, :] = val
            return
        d, mn, vn = _adam_math(w_ref[0], g, m_ref[0], v_ref[0])
        g_ref[0] = g
        d_ref[0] = d
        mo_ref[0] = mn
        vo_ref[0] = vn

    if rows_apart:
        t3 = pl.BlockSpec((tr, 1, tc), lambda l, i, j, q: (i, 0, j))
    else:
        t3 = pl.BlockSpec((1, tr, tc), lambda l, i, j, q: (l, i, j))
    slot = lambda k: pl.BlockSpec((1, tr, tc), lambda l, i, j, q: (k, l * nr + i, j))
    if side_by_side:
        mine = pl.BlockSpec((tr, tc), lambda l, i, j, q: (l * nr + i, q[0] * nc + j))
    else:
        mine = pl.BlockSpec((1, tr, tc), lambda l, i, j, q: (q[0], l * nr + i, j))
    return pl.pallas_call(
        body, name=name,
        grid_spec=pltpu.PrefetchScalarGridSpec(
            num_scalar_prefetch=1, grid=(L, nr, nc),
            in_specs=[t3, t3, t3, mine] + [slot(k) for k in range(7)], out_specs=[t3, t3, t3, t3]),
        out_shape=[jax.ShapeDtypeStruct(w.shape, F32)] * 4, compiler_params=_params(),
    )(me, w, m, v, g8, *([got] * 7))


def _adam_replicated(w, m, v, g8):
    def body(w_ref, m_ref, v_ref, g_ref, go_ref, d_ref, mo_ref, vo_ref):
        g = g_ref[0]
        for k in range(1, 8):
            g = g + g_ref[k]
        d, mn, vn = _adam_math(w_ref[...], g, m_ref[...], v_ref[...])
        go_ref[...] = g
        d_ref[...] = d
        mo_ref[...] = mn
        vo_ref[...] = vn

    return pl.pallas_call(
        body, name="adam_replicated", out_shape=[jax.ShapeDtypeStruct(w.shape, F32)] * 4, compiler_params=_params(),
    )(w, m, v, g8)


def _place():
    return lax.axis_index("x"), lax.axis_index("y"), lax.axis_index("c")


def _all_gather(shard, name):
    def body(x_ref, out_ref, send_sems, recv_sems, local_sem):
        x, y, c = _place()
        me, sibling = (x, y, c), (x, y, 1 - c)
        chips = [(1 - x, y), (x, 1 - y), (1 - x, 1 - y)]

        def slab(px, py, pc):
            return out_ref.at[4 * px + 2 * py + pc]

        def copy(k, block, to, src=None):
            return pltpu.make_async_remote_copy(
                src_ref=slab(*block) if src is None else src, dst_ref=slab(*block),
                send_sem=send_sems.at[k], recv_sem=recv_sems.at[k], device_id=to, device_id_type=MESH)

        mine = pltpu.make_async_copy(x_ref, slab(*me), local_sem)
        mine.start()
        first = [copy(0, me, sibling, src=x_ref)]
        first += [copy(1 + j, me, (*chip, c), src=x_ref) for j, chip in enumerate(chips)]
        for cp in first:
            cp.start()
        passed = [copy(4 + j, (*chip, c), sibling) for j, chip in enumerate(chips)]
        for j, chip in enumerate(chips):
            copy(1 + j, (*chip, c), me).wait_recv()
            passed[j].start()
        copy(0, sibling, me).wait_recv()
        for j, chip in enumerate(chips):
            copy(4 + j, (*chip, 1 - c), me).wait_recv()
        for cp in first + passed:
            cp.wait_send()
        mine.wait()

    return pl.pallas_call(
        body, name=name, out_shape=jax.ShapeDtypeStruct((8,) + shard.shape, shard.dtype),
        in_specs=[pl.BlockSpec(memory_space=pl.ANY)], out_specs=pl.BlockSpec(memory_space=pl.ANY),
        scratch_shapes=[pltpu.SemaphoreType.DMA((7,)), pltpu.SemaphoreType.DMA((7,)), pltpu.SemaphoreType.DMA],
    )(shard)


_HBM = pl.BlockSpec(memory_space=pltpu.HBM)
_SEM = pl.BlockSpec(memory_space=pltpu.SEMAPHORE)
_DATAFLOW = pltpu.SideEffectType.DATAFLOW_SIDE_EFFECTING
TOKEN_SHAPE = (SUBLANES, LANES)


def _peers(x, y, c):
    flip = lambda v, bit: 1 - v if bit else v
    return [(flip(x, r >> 2 & 1), flip(y, r >> 1 & 1), flip(c, r & 1)) for r in range(1, 8)]


def _scatter_plan(x, y, c):
    return [(4 * px + 2 * py + pc, k, (px, py, pc)) for k, (px, py, pc) in enumerate(_peers(x, y, c))]


def _exchange_copies(plan, src_ref, land_ref, send_sems, recv_sems):
    C = land_ref.shape[-1]
    block = (lambda b: src_ref.at[b]) if len(src_ref.shape) == len(land_ref.shape) else (
        lambda b: src_ref.at[:, pl.ds(b * C, C)])
    return [pltpu.make_async_remote_copy(
        src_ref=block(blk), dst_ref=land_ref.at[slot], send_sem=send_sems.at[k], recv_sem=recv_sems.at[k],
        device_id=peer, device_id_type=MESH) for k, (blk, slot, peer) in enumerate(plan(*_place()))]


def _exchange_start(src, n_slots, block_shape, plan, name):
    land_shape = (n_slots,) + tuple(block_shape)
    n = len(plan(0, 0, 0))

    def body(src_ref, land_ref, send_sems, recv_sems, src_thru, land_thru, token):
        for cp in _exchange_copies(plan, src_ref, land_ref, send_sems, recv_sems):
            cp.start()
        token[...] = jnp.zeros_like(token)

    return pl.pallas_call(
        body, name=name,
        out_shape=(pltpu.SemaphoreType.DMA((n,)), pltpu.SemaphoreType.DMA((n,)), pltpu.HBM(src.shape, src.dtype),
                   pltpu.HBM(land_shape, src.dtype), jax.ShapeDtypeStruct(TOKEN_SHAPE, F32)),
        in_specs=(_HBM, _HBM), out_specs=(_SEM, _SEM, _HBM, _HBM, pl.BlockSpec(memory_space=pltpu.VMEM)),
        input_output_aliases={0: 2, 1: 3}, compiler_params=pltpu.CompilerParams(has_side_effects=_DATAFLOW),
    )(pltpu.with_memory_space_constraint(src, pltpu.HBM),
      pltpu.with_memory_space_constraint(lax.empty(land_shape, src.dtype), pltpu.HBM))


def _exchange_wait(handle, plan, after, name):
    send_sems, recv_sems, src_thru, land_thru, _ = handle

    def body(src_ref, land_ref, send_sems, recv_sems, after_ref, src_dead, got_ref):
        for cp in _exchange_copies(plan, src_ref, land_ref, send_sems, recv_sems):
            cp.wait_send()
            cp.wait_recv()

    return pl.pallas_call(
        body, name=name,
        out_shape=(pltpu.HBM(src_thru.shape, src_thru.dtype), pltpu.HBM(land_thru.shape, land_thru.dtype)),
        in_specs=(_HBM, _HBM, _SEM, _SEM, pl.BlockSpec(memory_space=pl.ANY)), out_specs=(_HBM, _HBM),
        input_output_aliases={0: 0, 1: 1}, compiler_params=pltpu.CompilerParams(has_side_effects=_DATAFLOW),
    )(src_thru, land_thru, send_sems, recv_sems, after)[1]


def _gather_plan(x, y, c):
    return [(0, 4 * x + 2 * y + c, peer) for peer in _peers(x, y, c)]


class _LateGather:
    def __init__(self, shard, name):
        self.shard, self.name = shard, name
        self.handle = _exchange_start(shard[None], 8, shard.shape, _gather_plan, name + "_start")

    def get(self, after):
        land = _exchange_wait(self.handle, _gather_plan, after, self.name + "_wait")
        x, y, c = _place()
        return lax.dynamic_update_slice(land, self.shard[None], (4 * x + 2 * y + c, 0, 0))


class _GradExchange:
    def __init__(self, me, layouts):
        self.me, self.layouts, self.pending = me, layouts, {}

    def start(self, tag, grad, grad16=None):
        g8 = self.layouts[tag](grad)
        g16 = g8.astype(BF16) if grad16 is None else self.layouts[tag](grad16)
        block_shape = g8.shape[1:] if g8.ndim == 3 else (g8.shape[0], g8.shape[1] // 8)
        handle = _exchange_start(g16, 7, block_shape, _scatter_plan, "rs_start_" + tag)
        self.pending[tag] = (g8, handle)
        return handle[4]

    def finish(self, tag, w, m, v, after):
        g8, handle = self.pending.pop(tag)
        got = _exchange_wait(handle, _scatter_plan, after, "rs_wait_" + tag)
        return _adam_sharded(w, m, v, g8, got, self.me, "adam_" + tag)


def _local_grads(x, p0, p1, target, wt_zs, wt_a, wt_qkv, late, conv_a, conv_b,
                 a_log, dt_bias, gdn_gain, lower_bounds, hgrn_gain, ln_g, ln_b, on_grad=None):
    H = a_log.shape[1]
    pad_small = ((0, 0), (H, LANES - 2 * H))
    alog_row = jnp.pad(a_log, pad_small)
    dtb_row = jnp.pad(dt_bias, pad_small)

    x16 = x.astype(BF16)
    proj_zs = _matmul(x16, wt_zs, "nt", "proj_even_zs", after=late.started)
    proj_a = _matmul(x16, wt_a, "nt", "proj_even_a", after=late.started)
    proj_qkv = _matmul(x16, wt_qkv, "nt", "proj_even_qkv", after=late.started)
    y_a = _mixer_a_fwd(proj_a, conv_a)
    qkv = _conv_b_fwd(proj_qkv, conv_b)
    o2, s_gdn = _gdn_fwd(qkv, proj_zs, alog_row, dtb_row, gdn_gain, H)
    woute_a, woute_b = late.out_even(o2)
    wg, wpl = late.gate(o2)
    s_e = _matmul(o2, woute_b, "nn", "out_even_b", add=_matmul(y_a, woute_a, "nn", "out_even_a"))
    x2, x2_16 = _post_fwd(s_e, x, p0, ln_g[0:1], ln_b[0:1], wg[0], wpl[0], "post_even_fwd")
    wino, wouto = late.odd(s_e)
    nheads_o = wouto.shape[0] // HEAD
    proj_o = _matmul(x2_16, wino, "nn", "proj_odd")
    o4, s_hgrn = _hgrn_fwd(proj_o, lower_bounds, hgrn_gain, nheads_o)
    s_o = _matmul(o4, wouto, "nn", "out_odd")
    ds_o, dx2, dlng1, dlnb1, dwg1, dwpl1, loss = _post_bwd(
        s_o, x2, p1, ln_g[1:2], ln_b[1:2], wg[1], wpl[1], target, "post_odd_loss_bwd", True)
    do4 = _matmul(ds_o, wouto, "nt", "d_out_odd_act")
    grads = {}

    def emit(tag, grad, grad16=None):
        grads[tag] = grad
        return on_grad(tag, grad, grad16) if on_grad is not None else jnp.zeros(TOKEN_SHAPE, F32)

    tok = emit("w_out_odd", *_matmul(o4, ds_o, "tn", "d_out_odd_w", also_bf16=True))
    dproj_o, dlb, dhgain = _hgrn_bwd(proj_o, do4, s_hgrn, lower_bounds, hgrn_gain + tok[0:1], nheads_o)
    dx2 = _matmul(dproj_o, wino, "nt", "d_proj_odd_act", add=dx2)
    tok = emit("w_in_odd", *_matmul(x2_16, dproj_o, "tn", "d_proj_odd_w", also_bf16=True))
    ds_e, dx, dlng0, dlnb0, dwg0, dwpl0, _ = _post_bwd(
        s_e, x, p0, ln_g[0:1], ln_b[0:1] + tok[0:1, 0:1], wg[0], wpl[0], dx2, "post_even_bwd", False)
    tok = emit("w_pl_gate", jnp.stack([dwg0, dwg1])) + emit("w_pl", jnp.stack([dwpl0, dwpl1]))
    dy_a = _matmul(ds_e, woute_a, "nt", "d_out_even_a_act")
    do2 = _matmul(ds_e, woute_b, "nt", "d_out_even_b_act")
    dwoute_a = _matmul(y_a, ds_e, "tn", "d_out_even_a_w")
    dwoute_b = _matmul(o2, ds_e, "tn", "d_out_even_b_w")
    tok = tok + emit("w_out_even", jnp.concatenate([dwoute_a, dwoute_b], axis=0))
    dqkv, dproj_zs, dalog, ddtb, dggain = _gdn_bwd(qkv, proj_zs, do2, s_gdn, alog_row, dtb_row, gdn_gain + tok[0:1], H)
    dproj_qkv, dconv_b = _conv_b_bwd(proj_qkv, dqkv, conv_b)
    dproj_a, dconv_a = _mixer_a_bwd(proj_a, dy_a, conv_a)
    emit("conv", (dconv_a[:conv_a.shape[0]], dconv_b[:conv_b.shape[0]]))
    tok = emit("w_in_even", (_matmul(dproj_zs, x16, "tn", "d_proj_even_zs_w"), _matmul(dproj_a, x16, "tn", "d_proj_even_a_w"),
                             _matmul(dproj_qkv, x16, "tn", "d_proj_even_qkv_w")))
    dx = _matmul(dproj_zs, wt_zs, "nn", "d_proj_even_zs_act", add=dx, after=tok)
    dx = _matmul(dproj_a, wt_a, "nn", "d_proj_even_a_act", add=dx)
    dx = _matmul(dproj_qkv, wt_qkv, "nn", "d_proj_even_qkv_act", add=dx)
    grads.update(
        loss=loss[0, 0], grad_x=dx, a_log=dalog[:, H:2 * H], dt_bias=ddtb[:, H:2 * H], gdn_gain=dggain,
        lower_bounds=dlb, hgrn_gain=dhgain, ln_g=jnp.concatenate([dlng0, dlng1], axis=0),
        ln_b=jnp.concatenate([dlnb0, dlnb1], axis=0))
    return grads


def _pad_rows(a, rows):
    return jnp.pad(a, ((0, rows - a.shape[0]), (0, 0)))


def _pack_small(a_log, dt_bias, gdn_gain, lower_bounds, hgrn_gain, ln_g, ln_b):
    lane_pad = lambda a: _pad_rows(jnp.pad(a, ((0, 0), (0, LANES - a.shape[1]))), SUBLANES)
    parts = [lane_pad(a_log), lane_pad(dt_bias), lane_pad(gdn_gain), lower_bounds.reshape(-1, LANES),
             lane_pad(hgrn_gain), ln_g.reshape(-1, LANES), ln_b.reshape(-1, LANES)]
    packed = jnp.concatenate(parts, axis=0)
    assert packed.shape[0] == SMALL_ROWS, packed.shape
    return packed


def _unpack_small(packed, shapes):
    out, r = [], 0
    for shp in shapes:
        n = shp[0] * shp[1]
        if n < LANES * SUBLANES and shp[1] <= LANES:
            out.append(packed[r:r + shp[0], :shp[1]])
            r += SUBLANES
        else:
            rows = n // LANES
            out.append(packed[r:r + rows].reshape(shp))
            r += rows
    return out


def _split_in_even(wt_full, AW, HW, H):
    D = wt_full.shape[1]
    n_a = 4 * AW
    n_main = n_a + 3 * HW
    wt_zs = jnp.concatenate([wt_full[n_main:n_main + HW], wt_full[n_main + HW:],
                             jnp.zeros((LANES - 2 * H, D), wt_full.dtype)], axis=0)
    wt_a = wt_full[:n_a].reshape(4, AW // MIXER_LANES, MIXER_LANES, D).transpose(1, 0, 2, 3).reshape(n_a, D)
    return wt_zs, wt_a, wt_full[n_a:n_main]


def _join_in_even(dt_zs, dt_a, dt_qkv, AW, HW, H):
    D = dt_a.shape[1]
    a_nat = dt_a.reshape(AW // MIXER_LANES, 4, MIXER_LANES, D).transpose(1, 0, 2, 3).reshape(4 * AW, D)
    return jnp.concatenate([a_nat, dt_qkv, dt_zs[:HW], dt_zs[HW:HW + 2 * H]], axis=0)


def kernel(x, p, w_in_even, conv_a_w, conv_b_w, a_log, dt_bias, gdn_norm_g, w_out_even, w_in_odd, lower_bounds, hgrn_norm_g, w_out_odd, ln_g, ln_b, w_pl, w_pl_gate, loss_target, m_w_in_even, m_conv_a_w, m_conv_b_w, m_a_log, m_dt_bias, m_gdn_norm_g, m_w_out_even, m_w_in_odd, m_lower_bounds, m_hgrn_norm_g, m_w_out_odd, m_ln_g, m_ln_b, m_w_pl, m_w_pl_gate, v_w_in_even, v_conv_a_w, v_conv_b_w, v_a_log, v_dt_bias, v_gdn_norm_g, v_w_out_even, v_w_in_odd, v_lower_bounds, v_hgrn_norm_g, v_w_out_odd, v_ln_g, v_ln_b, v_w_pl, v_w_pl_gate):
    xi, yi, ci = _place()
    me = jnp.reshape(4 * xi + 2 * yi + ci, (1,)).astype(jnp.int32)
    D = x.shape[2]
    H = a_log.shape[1]
    HW = H * HEAD
    AW = conv_a_w.shape[2] * 8
    OW = w_out_odd.shape[1] * 8
    PD = w_pl.shape[1]
    ka, kb = conv_a_w.shape[1], conv_b_w.shape[1]
    ca, cb = conv_a_w.shape[2], conv_b_w.shape[2]
    gw = HGRN_HEADS_PER_STEP * HEAD
    ngrp = OW // gw

    transposed = lambda a: jnp.transpose(a, (0, 2, 1))
    g_ine = _all_gather(transposed(w_in_even)[0].astype(BF16), "ag_w_in_even")
    wt_zs, wt_a, wt_qkv = _split_in_even(g_ine.reshape(-1, D), AW, HW, H)
    behind = lambda shard, dep: lax.optimization_barrier((shard, dep))[0]
    late_oute = _LateGather(behind(w_out_even[0].astype(BF16), g_ine), "ag_w_out_even")
    late_gate = _LateGather(behind(w_pl_gate.astype(BF16).reshape(-1, D), late_oute.handle[4]), "ag_w_pl_gate")
    late_pl = _LateGather(behind(w_pl.astype(BF16).reshape(DEPTH * PD, -1), late_gate.handle[4]), "ag_w_pl")
    late_ino = _LateGather(behind(w_in_odd[0].astype(BF16), late_pl.handle[4]), "ag_w_in_odd")
    late_outo = _LateGather(behind(w_out_odd[0].astype(BF16), late_ino.handle[4]), "ag_w_out_odd")

    class _Late:
        started = sum(g.handle[4] for g in (late_oute, late_gate, late_pl, late_ino, late_outo))

        @staticmethod
        def out_even(after):
            woute = late_oute.get(after).reshape(-1, D)
            return woute[:AW], woute[AW:]

        @staticmethod
        def gate(after):
            g_gate, g_pl = late_gate.get(after), late_pl.get(after)
            return (g_gate.reshape(8, DEPTH, D // 8, D).transpose(1, 0, 2, 3).reshape(DEPTH, D, D),
                    g_pl.reshape(8, DEPTH, PD, D // 8).transpose(1, 2, 0, 3).reshape(DEPTH, PD, D))

        @staticmethod
        def odd(after):
            g_ino = late_ino.get(after)
            wino = jnp.transpose(g_ino, (1, 0, 2)).reshape(D, 4, ngrp, gw).transpose(0, 2, 1, 3).reshape(D, 4 * OW)
            return wino, late_outo.get(after).reshape(-1, D)

    taps = jnp.concatenate([_pad_rows(conv_a_w[0], SUBLANES), _pad_rows(conv_b_w[0], SUBLANES)], axis=1)
    g_taps = _all_gather(taps, "ag_conv")
    conv_a = jnp.transpose(g_taps[:, :ka, :ca], (1, 0, 2)).reshape(ka, 8 * ca)
    conv_b = jnp.transpose(g_taps[:, :kb, ca:], (1, 0, 2)).reshape(kb, 8 * cb)

    sh = w_in_even.shape[2]
    tap_blocks = lambda g, width: _pad_rows(g, SUBLANES).reshape(SUBLANES, 8, width).transpose(1, 0, 2)
    owner_layout = {
        "w_in_even": lambda g: _join_in_even(*g, AW, HW, H).reshape(8, sh, D),
        "w_in_odd": lambda g: g.reshape(D, ngrp, 4, gw).transpose(0, 2, 1, 3).reshape(D, 4 * OW),
        "w_out_even": lambda g: g.reshape(8, -1, D),
        "w_out_odd": lambda g: g.reshape(8, -1, D),
        "w_pl_gate": lambda g: g.reshape(DEPTH, 8, D // 8, D).transpose(1, 0, 2, 3).reshape(8, DEPTH * D // 8, D),
        "w_pl": lambda g: g.reshape(DEPTH, PD, 8, D // 8).transpose(2, 0, 1, 3).reshape(8, DEPTH * PD, D // 8),
        "conv": lambda g: jnp.concatenate([tap_blocks(g[0], ca), tap_blocks(g[1], cb)], axis=2),
    }
    exchange = _GradExchange(me, owner_layout)
    gr = _local_grads(x[0], p[0, 0], p[1, 0], loss_target[0], wt_zs, wt_a, wt_qkv, _Late, conv_a, conv_b,
                      a_log, dt_bias, gdn_norm_g, lower_bounds, hgrn_norm_g, ln_g, ln_b, on_grad=exchange.start)

    last = gr["grad_x"]
    pack_taps = lambda a, b: jnp.concatenate([_pad_rows(a[0], SUBLANES), _pad_rows(b[0], SUBLANES)], axis=1)[None]
    o_outo = exchange.finish("w_out_odd", w_out_odd, m_w_out_odd, v_w_out_odd, last)
    o_ino = exchange.finish("w_in_odd", w_in_odd, m_w_in_odd, v_w_in_odd, last)
    o_gate = exchange.finish("w_pl_gate", w_pl_gate, m_w_pl_gate, v_w_pl_gate, last)
    o_pl = exchange.finish("w_pl", w_pl, m_w_pl, v_w_pl, last)
    o_oute = exchange.finish("w_out_even", w_out_even, m_w_out_even, v_w_out_even, last)
    o_taps = exchange.finish("conv", taps[None], pack_taps(m_conv_a_w, m_conv_b_w), pack_taps(v_conv_a_w, v_conv_b_w), last)
    others_done = sum(o[1][0, 0:1, 0:1] for o in (o_outo, o_ino, o_gate, o_pl, o_oute, o_taps))
    rows_first = lambda a: jnp.transpose(a, (2, 0, 1))
    o_ine = [jnp.transpose(o, (1, 2, 0)) for o in exchange.finish(
        "w_in_even", rows_first(w_in_even), rows_first(m_w_in_even), rows_first(v_w_in_even), others_done)]

    small_g = _pack_small(gr["a_log"], gr["dt_bias"], gr["gdn_gain"], gr["lower_bounds"], gr["hgrn_gain"],
                          gr["ln_g"], gr["ln_b"])
    small_g = small_g.at[0, LANES - 1].set(gr["loss"])
    o_small = _adam_replicated(
        _pack_small(a_log, dt_bias, gdn_norm_g, lower_bounds, hgrn_norm_g, ln_g, ln_b),
        _pack_small(m_a_log, m_dt_bias, m_gdn_norm_g, m_lower_bounds, m_hgrn_norm_g, m_ln_g, m_ln_b),
        _pack_small(v_a_log, v_dt_bias, v_gdn_norm_g, v_lower_bounds, v_hgrn_norm_g, v_ln_g, v_ln_b),
        _all_gather(small_g, "ag_small_grads"))
    small_shapes = [a_log.shape, dt_bias.shape, gdn_norm_g.shape, lower_bounds.shape, hgrn_norm_g.shape,
                    ln_g.shape, ln_b.shape]

    def leaves(kind):
        s_alog, s_dt, s_gg, s_lb, s_hg, s_lng, s_lnb = _unpack_small(o_small[kind], small_shapes)
        t = o_taps[kind]
        return [o_ine[kind], t[:, :ka, :ca], t[:, :kb, ca:], s_alog, s_dt, s_gg, o_oute[kind],
                o_ino[kind], s_lb, s_hg, o_outo[kind], s_lng, s_lnb, o_pl[kind], o_gate[kind]]

    return (o_small[0][0, LANES - 1], gr["grad_x"][None], *leaves(0), *leaves(1), *leaves(2), *leaves(3))
```

```python
import functools

import jax
import jax.numpy as jnp
from jax import lax
from jax.experimental import pallas as pl
from jax.experimental.pallas import tpu as pltpu

F32 = jnp.float32
BF16 = jnp.bfloat16
MESH = pl.DeviceIdType.MESH

LANES = 128
SUBLANES = 8
HEAD = 128
GDN_CHUNK = 64
HGRN_CHUNK = 64
HGRN_SUB = 16
HGRN_HEADS_PER_STEP = 16
NORM_EPS = 1e-5
DEPTH = 2
ALPHA = (2.0 * DEPTH) ** 0.25
EXP_CLAMP = 80.0
ADAM_LR, ADAM_B1, ADAM_B2, ADAM_EPS, ADAM_WD, ADAM_STEP = 0.001, 0.9, 0.999, 1e-08, 0.01, 10
VMEM_LIMIT = 56 * 1024 * 1024
MATMUL_VMEM = 36 * 1024 * 1024
ROW_TILE = 512
MIXER_LANES = 256
CONV_LANES = 512
POST_TILE = 512

_NOBATCH, _BATCH0 = ((), ()), ((0,), (0,))
_DIMS = {"nn": (((1,), (0,)), _NOBATCH), "nt": (((1,), (1,)), _NOBATCH), "tn": (((0,), (0,)), _NOBATCH),
         "bnn": (((2,), (1,)), _BATCH0), "bnt": (((2,), (2,)), _BATCH0), "btn": (((1,), (1,)), _BATCH0)}


def _params(**kw):
    return pltpu.CompilerParams(vmem_limit_bytes=VMEM_LIMIT, **kw)


def _dot_raw(a, b, kind, hi):
    if hi:
        return lax.dot_general(a, b, _DIMS[kind], precision=lax.Precision.HIGHEST, preferred_element_type=F32)
    return lax.dot_general(a.astype(BF16), b.astype(BF16), _DIMS[kind], preferred_element_type=F32)


@functools.partial(jax.custom_vjp, nondiff_argnums=(2, 3))
def mdot(a, b, kind, hi):
    return _dot_raw(a, b, kind, hi)


def _mdot_fwd(a, b, kind, hi):
    return _dot_raw(a, b, kind, hi), (a, b)


def _mdot_bwd(kind, hi, res, g):
    a, b = res
    pre, base = kind[:-2], kind[-2:]
    if base == "nn":
        return _dot_raw(g, b, pre + "nt", hi), _dot_raw(a, g, pre + "tn", hi)
    if base == "nt":
        return _dot_raw(g, b, pre + "nn", hi), _dot_raw(g, a, pre + "tn", hi)
    return _dot_raw(b, g, pre + "nt", hi), _dot_raw(a, g, pre + "nn", hi)


mdot.defvjp(_mdot_fwd, _mdot_bwd)


def _rows(x, lo, hi):
    return _take_rows(x, lo, hi, x.shape[-2])


@functools.partial(jax.custom_vjp, nondiff_argnums=(1, 2, 3))
def _take_rows(x, lo, hi, n):
    return x[..., lo:hi, :]


def _take_rows_fwd(x, lo, hi, n):
    return x[..., lo:hi, :], None


def _take_rows_bwd(lo, hi, n, _, g):
    parts = []
    if lo > 0:
        parts.append(jnp.zeros(g.shape[:-2] + (lo, g.shape[-1]), g.dtype))
    parts.append(g)
    if n - hi > 0:
        parts.append(jnp.zeros(g.shape[:-2] + (n - hi, g.shape[-1]), g.dtype))
    return (jnp.concatenate(parts, axis=-2) if len(parts) > 1 else g,)


_take_rows.defvjp(_take_rows_fwd, _take_rows_bwd)


def _heads_of(wide, nheads):
    return jnp.stack([wide[:, h * HEAD:(h + 1) * HEAD] for h in range(nheads)], axis=0)


def _wide_of(x):
    return jnp.concatenate([x[h] for h in range(x.shape[0])], axis=1)


@functools.partial(jax.custom_vjp, nondiff_argnums=(1,))
def to_heads(wide, nheads):
    return _heads_of(wide, nheads)


to_heads.defvjp(lambda wide, nheads: (_heads_of(wide, nheads), None), lambda nheads, _, g: (_wide_of(g),))


@jax.custom_vjp
def to_wide(x):
    return _wide_of(x)


to_wide.defvjp(lambda x: (_wide_of(x), None), lambda _, g: (_heads_of(g, g.shape[1] // HEAD),))


def _sigmoid(x):
    return jax.nn.sigmoid(x)


def _silu(x):
    return x * _sigmoid(x)


def _dsilu(x):
    s = _sigmoid(x)
    return s * (1.0 + x * (1.0 - s))


def _log1p(u):
    return jnp.where(u < 1e-4, u * (1.0 - 0.5 * u), jnp.log(1.0 + u))


def _softplus(x):
    return jnp.maximum(x, 0.0) + _log1p(jnp.exp(-jnp.abs(x)))


def _rms_gate(o, gain, z):
    return o * lax.rsqrt(jnp.mean(o * o, axis=-1, keepdims=True) + NORM_EPS) * gain * _silu(z)


def _l2n(x):
    return x * lax.rsqrt(jnp.sum(x * x, axis=-1, keepdims=True) + 1e-6)


def _split_dot_raw(m, x, kind):
    mb = m.astype(BF16)
    hi = x.astype(BF16)
    lo = (x - hi.astype(F32)).astype(BF16)
    dims = _DIMS[kind]
    return (lax.dot_general(mb, hi, dims, preferred_element_type=F32)
            + lax.dot_general(mb, lo, dims, preferred_element_type=F32))


@jax.custom_vjp
def mask_dot(m, x):
    return _split_dot_raw(m, x, "nn")


def _mask_dot_fwd(m, x):
    return _split_dot_raw(m, x, "nn"), m


def _mask_dot_bwd(m, g):
    return jnp.zeros_like(m), _split_dot_raw(m, g, "tn")


mask_dot.defvjp(_mask_dot_fwd, _mask_dot_bwd)


def _neumann_rest(low):
    n = low.shape[-1]
    rest = -low
    power = low
    span = 2
    while span < n:
        power = _dot_raw(power, power, "bnn", False)
        rest = rest + power + _dot_raw(rest, power, "bnn", False)
        span *= 2
    return rest


@jax.custom_vjp
def _unit_lower_inverse_minus_eye(low):
    return _neumann_rest(low)


def _inverse_fwd(low):
    rest = _neumann_rest(low)
    return rest, rest


def _inverse_bwd(rest, g):
    left = g + _dot_raw(rest, g, "btn", False)
    return (-(left + _dot_raw(left, rest, "bnt", False)),)


_unit_lower_inverse_minus_eye.defvjp(_inverse_fwd, _inverse_bwd)


def _gdn_step(S, q, k, v, z, small, alog, dtb, gain):
    H = S.shape[0]
    C = GDN_CHUNK
    row = lax.broadcasted_iota(jnp.int32, (C, C), 0)
    col = lax.broadcasted_iota(jnp.int32, (C, C), 1)
    tril, strict, eye = (row >= col)[None], (row > col)[None], (row == col)[None]
    head = lax.broadcasted_iota(jnp.int32, (H, 1, LANES), 0)
    lane = lax.broadcasted_iota(jnp.int32, (H, 1, LANES), 2)
    rowc = lax.broadcasted_iota(jnp.int32, (1, C, 1), 1)
    beta_all = _sigmoid(small)
    g_all = -jnp.exp(alog) * _softplus(small + dtb)
    gc_all = mask_dot((row >= col).astype(F32), g_all)
    beta = jnp.sum(jnp.where(lane == head, beta_all[None], 0.0), axis=-1, keepdims=True)
    gc = jnp.sum(jnp.where(lane == head + H, gc_all[None], 0.0), axis=-1, keepdims=True)
    gc_row = jnp.sum(jnp.where(eye, gc, 0.0), axis=1, keepdims=True)
    decay = jnp.where(tril, jnp.exp(jnp.where(tril, gc - gc_row, 0.0)), 0.0)
    g_last = jnp.sum(jnp.where(rowc == C - 1, gc, 0.0), axis=1, keepdims=True)
    qn = _l2n(q) * (HEAD ** -0.5)
    kn = _l2n(k)
    kb = kn * beta
    low = jnp.where(strict, beta * mdot(kn, kn, "bnt", False) * decay, 0.0)
    inv_rest = _unit_lower_inverse_minus_eye(low)
    eg = jnp.exp(gc)
    vb, kbe = v * beta, kb * eg
    u = vb + mdot(inv_rest, vb, "bnn", False)
    w = kbe + mdot(inv_rest, kbe, "bnn", False)
    attn = mdot(qn, kn, "bnt", False) * decay
    v_new = u - mdot(w, S, "bnn", False)
    o = mdot(qn * eg, S, "bnn", False) + mdot(attn, v_new, "bnn", False)
    k_dec = kn * jnp.exp(g_last - gc)
    return _rms_gate(o, gain, z), S * jnp.exp(g_last) + mdot(k_dec, v_new, "btn", False)


def _hgrn_step(St, qr, fr, vi, z, lb0, lb1, gain):
    H = St.shape[0]
    C, SB = HGRN_CHUNK, HGRN_SUB
    row = lax.broadcasted_iota(jnp.int32, (C, C), 0)
    col = lax.broadcasted_iota(jnp.int32, (C, C), 1)
    blk_start = row - (row & (SB - 1))
    in_blk_f = ((row >= col) & (col >= blk_start)).astype(F32)
    before_f = (col < blk_start).astype(F32)
    sums_f = jnp.concatenate([in_blk_f, before_f], axis=0)
    m = jnp.maximum(lb0, lb1)
    e0, e1 = jnp.exp(lb0 - m), jnp.exp(lb1 - m)
    lb = e1 / (e0 + e1)
    f = lb + (1.0 - lb) * _sigmoid(fr)
    q = _silu(qr)
    k = 1.0 - f
    logf = jnp.log(f)
    sums = mask_dot(sums_f, to_wide(logf))
    inner, start = to_heads(_rows(sums, 0, C), H), to_heads(_rows(sums, C, 2 * C), H)
    b = start + inner
    b_last = jnp.sum(logf, axis=1, keepdims=True)
    o = mdot(q * jnp.exp(b), St, "bnt", False)
    qt = q * jnp.exp(inner)
    parts = []
    for blk in range(C // SB):
        lo, n = blk * SB, (blk + 1) * SB
        ref = jnp.concatenate([_rows(start, lo, n)] * (blk + 1), axis=1)
        kt = _rows(k, 0, n) * jnp.exp(jnp.minimum(ref - _rows(b, 0, n), EXP_CLAMP))
        att = mdot(_rows(qt, lo, n), kt, "bnt", False)
        t_idx = lax.broadcasted_iota(jnp.int32, (1, SB, n), 1) + lo
        s_idx = lax.broadcasted_iota(jnp.int32, (1, SB, n), 2)
        att = jnp.where(s_idx <= t_idx, att, 0.0)
        parts.append(mdot(att, _rows(vi, 0, n), "bnn", False))
    o = o + jnp.concatenate(parts, axis=1)
    k_dec = k * jnp.exp(b_last - b)
    return _rms_gate(o, gain, z), St * jnp.exp(b_last) + mdot(vi, k_dec, "btn", False)


def _post_norm(s, x, g, b):
    r = ALPHA * x + s
    d = r - jnp.mean(r, axis=-1, keepdims=True)
    var = jnp.mean(d * d, axis=-1, keepdims=True)
    return d * lax.rsqrt(var + NORM_EPS) * g + b


def _post_gate(x1, gate_pre, pp):
    return x1 + pp * _sigmoid(gate_pre)


def _pick(dim, cands):
    for c in cands:
        if dim % c == 0:
            return c
    return dim


def _matmul_tiles(M, K, tn, a_bytes, b_bytes, has_add):
    for tk in (4096, 2048, 1536, 1152, 1024, 640, 512, 384, 256, 128):
        if K % tk:
            continue
        for tm in (2048, 1152, 1024, 512, 384, 256, 128):
            if M % tm:
                continue
            blocks = tm * tk * a_bytes + tk * tn * b_bytes + tm * tn * 4 * (2 if has_add else 1)
            if 2 * blocks + (tm * tn * 4 if tk < K else 0) <= MATMUL_VMEM and tm >= min(M, 1024):
                return tm, tk
    return _pick(M, (512, 256, 128)), _pick(K, (512, 256, 128))


def _matmul(a, b, kind, name, add=None, after=None, also_bf16=False):
    if kind == "nn":
        (M, K), N = a.shape, b.shape[1]
    elif kind == "nt":
        (M, K), N = a.shape, b.shape[0]
    else:
        (K, M), N = a.shape, b.shape[1]
    has_add = add is not None
    tn = _pick(N, (512, 640, 384, 256, 128))
    tm, tk = _matmul_tiles(M, K, tn, a.dtype.itemsize, b.dtype.itemsize, has_add)
    nk = K // tk
    a_spec = pl.BlockSpec((tk, tm), lambda i, j, k: (k, i)) if kind == "tn" else pl.BlockSpec((tm, tk), lambda i, j, k: (i, k))
    b_spec = pl.BlockSpec((tn, tk), lambda i, j, k: (j, k)) if kind == "nt" else pl.BlockSpec((tk, tn), lambda i, j, k: (k, j))
    o_spec = pl.BlockSpec((tm, tn), lambda i, j, k: (i, j))

    extra = ([add] if has_add else []) + ([after] if after is not None else [])
    extra_specs = ([o_spec] if has_add else []) + ([pl.BlockSpec(TOKEN_SHAPE, lambda i, j, k: (0, 0))] if after is not None else [])

    out_dtypes = (F32, BF16) if also_bf16 else (F32,)

    def body(a_ref, b_ref, *rest):
        outs = rest[len(extra):len(extra) + len(out_dtypes)]

        def write(val):
            if has_add:
                val = val + rest[0][...]
            for o_ref in outs:
                o_ref[...] = val.astype(o_ref.dtype)

        part = _dot_raw(a_ref[...], b_ref[...], kind, False)
        if nk == 1:
            write(part)
            return
        acc = rest[-1]
        kk = pl.program_id(2)

        @pl.when(kk == 0)
        def _():
            acc[...] = part

        @pl.when(kk > 0)
        def _():
            acc[...] += part

        @pl.when(kk == nk - 1)
        def _():
            write(acc[...])

    result = pl.pallas_call(
        body, name=name, grid=(M // tm, N // tn, nk),
        in_specs=[a_spec, b_spec] + extra_specs,
        out_specs=[o_spec] * len(out_dtypes), out_shape=[jax.ShapeDtypeStruct((M, N), dt) for dt in out_dtypes],
        scratch_shapes=[pltpu.VMEM((tm, tn), F32)] if nk > 1 else [],
        compiler_params=_params(dimension_semantics=("parallel", "parallel", "arbitrary")),
    )(a, b, *extra)
    return result if also_bf16 else result[0]


def _halo_specs(ts, nt, width, prev=True, main=True, nxt=True):
    per = ts // SUBLANES
    last8 = nt * per - 1
    specs = []
    if prev:
        specs.append(pl.BlockSpec((SUBLANES, width), lambda cb, i: (jnp.maximum(i * per - 1, 0), cb)))
    if main:
        specs.append(pl.BlockSpec((ts, width), lambda cb, i: (i, cb)))
    if nxt:
        specs.append(pl.BlockSpec((SUBLANES, width), lambda cb, i: (jnp.minimum((i + 1) * per, last8), cb)))
    return specs


def _taps(ext, ktaps, lo, size):
    return [ext[lo:lo + size] if j == 0 else pltpu.roll(ext, j, 0)[lo:lo + size] for j in range(ktaps)]


def _ahead(ext, j, size):
    n = ext.shape[0]
    return ext[:size] if j == 0 else pltpu.roll(ext, n - j, 0)[:size]


def _lane_block(ref, k):
    return ref[:, k * MIXER_LANES:(k + 1) * MIXER_LANES]


def _mixer_a_fwd(proj_a, conv_w):
    T = proj_a.shape[0]
    nblk = proj_a.shape[1] // (4 * MIXER_LANES)
    ts = min(ROW_TILE, T)
    nt = T // ts

    def body(pp, pm, w_ref, y_ref):
        i = pl.program_id(1)
        u_prev = jnp.where(i > 0, _lane_block(pp, 0) * _lane_block(pp, 1), 0.0)
        ext = jnp.concatenate([u_prev, _lane_block(pm, 0) * _lane_block(pm, 1)], axis=0)
        t0, t1, t2 = _taps(ext, 3, SUBLANES, ts)
        cv = w_ref[2:3, :] * t0 + w_ref[1:2, :] * t1 + w_ref[0:1, :] * t2
        y_ref[...] = (_lane_block(pm, 2) * cv * _silu(_lane_block(pm, 3))).astype(y_ref.dtype)

    return pl.pallas_call(
        body, name="mixer_a_fwd", grid=(nblk, nt),
        in_specs=_halo_specs(ts, nt, 4 * MIXER_LANES, nxt=False)
        + [pl.BlockSpec((conv_w.shape[0], MIXER_LANES), lambda cb, i: (0, cb))],
        out_specs=pl.BlockSpec((ts, MIXER_LANES), lambda cb, i: (i, cb)),
        out_shape=jax.ShapeDtypeStruct((T, nblk * MIXER_LANES), BF16), compiler_params=_params(),
    )(proj_a, proj_a, conv_w)


def _mixer_a_bwd(proj_a, dy, conv_w):
    T = proj_a.shape[0]
    nblk = proj_a.shape[1] // (4 * MIXER_LANES)
    ts = min(ROW_TILE, T)
    nt = T // ts
    kt = conv_w.shape[0]

    def body(pp, pm, pn, dym, dyn, w_ref, dp_ref, dw_ref):
        i = pl.program_id(1)
        hm, cm, bm, zm = (_lane_block(pm, k) for k in range(4))
        u_prev = jnp.where(i > 0, _lane_block(pp, 0) * _lane_block(pp, 1), 0.0)
        ext = jnp.concatenate([u_prev, hm * cm], axis=0)
        dy_ext = jnp.concatenate([dym[...], jnp.where(i < nt - 1, dyn[...], 0.0)], axis=0)
        b_ext = jnp.concatenate([bm, _lane_block(pn, 2)], axis=0)
        sz_ext = _silu(jnp.concatenate([zm, _lane_block(pn, 3)], axis=0))
        dcv_ext = dy_ext * b_ext * sz_ext
        w = [w_ref[j:j + 1, :] for j in range(kt)]
        du = sum(w[kt - 1 - j] * _ahead(dcv_ext, j, ts) for j in range(kt))
        taps = _taps(ext, kt, SUBLANES, ts)
        cv = sum(w[kt - 1 - j] * taps[j] for j in range(kt))
        for part, d in enumerate((du * cm, du * hm, dym[...] * cv * sz_ext[:ts], dym[...] * bm * cv * _dsilu(zm))):
            dp_ref[:, part * MIXER_LANES:(part + 1) * MIXER_LANES] = d.astype(dp_ref.dtype)
        dcv = dcv_ext[:ts]

        @pl.when(i == 0)
        def _():
            dw_ref[...] = jnp.zeros_like(dw_ref)

        for j in range(kt):
            dw_ref[j:j + 1, :] += jnp.sum(dcv * taps[kt - 1 - j], axis=0, keepdims=True)

    return pl.pallas_call(
        body, name="mixer_a_bwd", grid=(nblk, nt),
        in_specs=_halo_specs(ts, nt, 4 * MIXER_LANES) + _halo_specs(ts, nt, MIXER_LANES, prev=False)
        + [pl.BlockSpec((kt, MIXER_LANES), lambda cb, i: (0, cb))],
        out_specs=[pl.BlockSpec((ts, 4 * MIXER_LANES), lambda cb, i: (i, cb)),
                   pl.BlockSpec((SUBLANES, MIXER_LANES), lambda cb, i: (0, cb))],
        out_shape=[jax.ShapeDtypeStruct(proj_a.shape, BF16),
                   jax.ShapeDtypeStruct((SUBLANES, nblk * MIXER_LANES), F32)],
        compiler_params=_params(),
    )(proj_a, proj_a, proj_a, dy, dy, conv_w)


def _conv_b_fwd(raw, conv_w):
    T = raw.shape[0]
    nblk = raw.shape[1] // CONV_LANES
    ts = min(ROW_TILE, T)
    nt = T // ts
    kt = conv_w.shape[0]

    def body(rp, rm, w_ref, y_ref):
        i = pl.program_id(1)
        ext = jnp.concatenate([jnp.where(i > 0, rp[...], 0.0), rm[...]], axis=0)
        taps = _taps(ext, kt, SUBLANES, ts)
        y_ref[...] = _silu(sum(w_ref[kt - 1 - j:kt - j, :] * taps[j] for j in range(kt)))

    return pl.pallas_call(
        body, name="conv_b_fwd", grid=(nblk, nt),
        in_specs=_halo_specs(ts, nt, CONV_LANES, nxt=False) + [pl.BlockSpec((kt, CONV_LANES), lambda cb, i: (0, cb))],
        out_specs=pl.BlockSpec((ts, CONV_LANES), lambda cb, i: (i, cb)),
        out_shape=jax.ShapeDtypeStruct(raw.shape, F32), compiler_params=_params(),
    )(raw, raw, conv_w)


def _conv_b_bwd(raw, dy, conv_w):
    T = raw.shape[0]
    nblk = raw.shape[1] // CONV_LANES
    ts = min(ROW_TILE, T)
    nt = T // ts
    kt = conv_w.shape[0]

    def body(rp, rm, rn, dym, dyn, w_ref, dr_ref, dw_ref):
        i = pl.program_id(1)
        ext = jnp.concatenate([jnp.where(i > 0, rp[...], 0.0), rm[...], rn[...]], axis=0)
        w = [w_ref[j:j + 1, :] for j in range(kt)]
        taps = _taps(ext, kt, SUBLANES, ts + SUBLANES)
        xc_ext = sum(w[kt - 1 - j] * taps[j] for j in range(kt))
        dy_ext = jnp.concatenate([dym[...], jnp.where(i < nt - 1, dyn[...], 0.0)], axis=0)
        dxc_ext = dy_ext * _dsilu(xc_ext)
        dr_ref[...] = sum(w[kt - 1 - j] * _ahead(dxc_ext, j, ts) for j in range(kt)).astype(dr_ref.dtype)
        dxc = dxc_ext[:ts]

        @pl.when(i == 0)
        def _():
            dw_ref[...] = jnp.zeros_like(dw_ref)

        for j in range(kt):
            dw_ref[j:j + 1, :] += jnp.sum(dxc * taps[kt - 1 - j][:ts], axis=0, keepdims=True)

    return pl.pallas_call(
        body, name="conv_b_bwd", grid=(nblk, nt),
        in_specs=_halo_specs(ts, nt, CONV_LANES) + _halo_specs(ts, nt, CONV_LANES, prev=False)
        + [pl.BlockSpec((kt, CONV_LANES), lambda cb, i: (0, cb))],
        out_specs=[pl.BlockSpec((ts, CONV_LANES), lambda cb, i: (i, cb)),
                   pl.BlockSpec((SUBLANES, CONV_LANES), lambda cb, i: (0, cb))],
        out_shape=[jax.ShapeDtypeStruct(raw.shape, BF16), jax.ShapeDtypeStruct((SUBLANES, nblk * CONV_LANES), F32)],
        compiler_params=_params(),
    )(raw, raw, raw, dy, dy, conv_w)


def _split_heads(ref, base, nheads, rows=slice(None)):
    return jnp.stack([ref[rows, base + h * HEAD: base + (h + 1) * HEAD] for h in range(nheads)], axis=0)


def _store_heads(ref, base, x, rows=slice(None), accumulate=False):
    for h in range(x.shape[0]):
        lanes = slice(base + h * HEAD, base + (h + 1) * HEAD)
        if accumulate:
            ref[rows, lanes] += x[h]
        else:
            ref[rows, lanes] = x[h].astype(ref.dtype)


def _gdn_fwd(qkv, proj_zs, alog, dtb, gain, H):
    T = qkv.shape[0]
    C, HW = GDN_CHUNK, H * HEAD
    nc = T // C
    zw = HW + LANES

    def body(qkv_ref, zs_ref, alog_ref, dtb_ref, gain_ref, o_ref, sall_ref, s_scr):
        @pl.when(pl.program_id(0) == 0)
        def _():
            s_scr[...] = jnp.zeros_like(s_scr)

        sall_ref[0] = s_scr[...]
        outs, states = _gdn_step(
            s_scr[...], _split_heads(qkv_ref, 0, H), _split_heads(qkv_ref, HW, H),
            _split_heads(qkv_ref, 2 * HW, H), _split_heads(zs_ref, 0, H), zs_ref[:, HW:HW + LANES],
            alog_ref[...], dtb_ref[...], gain_ref[...])
        _store_heads(o_ref, 0, outs)
        s_scr[...] = states

    row = pl.BlockSpec((1, LANES), lambda i: (0, 0))
    return pl.pallas_call(
        body, name="gdn_fwd", grid=(nc,),
        in_specs=[pl.BlockSpec((C, 3 * HW), lambda i: (i, 0)), pl.BlockSpec((C, zw), lambda i: (i, 0)), row, row, row],
        out_specs=[pl.BlockSpec((C, HW), lambda i: (i, 0)), pl.BlockSpec((1, H, HEAD, HEAD), lambda i: (i, 0, 0, 0))],
        out_shape=[jax.ShapeDtypeStruct((T, HW), BF16), jax.ShapeDtypeStruct((nc, H, HEAD, HEAD), F32)],
        scratch_shapes=[pltpu.VMEM((H, HEAD, HEAD), F32)], compiler_params=_params(),
    )(qkv, proj_zs, alog, dtb, gain)


def _gdn_bwd(qkv, proj_zs, do, s_all, alog, dtb, gain, H):
    T = qkv.shape[0]
    C, HW = GDN_CHUNK, H * HEAD
    nc = T // C
    zw = HW + LANES

    def body(qkv_ref, zs_ref, do_ref, sin_ref, alog_ref, dtb_ref, gain_ref,
             dqkv_ref, dzs_ref, dalog_ref, ddtb_ref, dgain_ref, ds_scr):
        @pl.when(pl.program_id(0) == 0)
        def _():
            ds_scr[...] = jnp.zeros_like(ds_scr)
            dalog_ref[...] = jnp.zeros_like(dalog_ref)
            ddtb_ref[...] = jnp.zeros_like(ddtb_ref)
            dgain_ref[...] = jnp.zeros_like(dgain_ref)

        primals = (sin_ref[0], _split_heads(qkv_ref, 0, H),
                   _split_heads(qkv_ref, HW, H), _split_heads(qkv_ref, 2 * HW, H), _split_heads(zs_ref, 0, H),
                   zs_ref[:, HW:HW + LANES], alog_ref[...], dtb_ref[...], gain_ref[...])
        _, vjp = jax.vjp(_gdn_step, *primals)
        dS, dq, dk, dv, dz, dsmall, dalog, ddtb, dgain = vjp((_split_heads(do_ref, 0, H), ds_scr[...]))
        ds_scr[...] = dS
        _store_heads(dqkv_ref, 0, dq)
        _store_heads(dqkv_ref, HW, dk)
        _store_heads(dqkv_ref, 2 * HW, dv)
        _store_heads(dzs_ref, 0, dz)
        dzs_ref[:, HW:HW + LANES] = dsmall.astype(dzs_ref.dtype)
        dalog_ref[...] += dalog
        ddtb_ref[...] += ddtb
        dgain_ref[...] += dgain

    row = pl.BlockSpec((1, LANES), lambda i: (0, 0))
    rev = lambda i: nc - 1 - i
    return pl.pallas_call(
        body, name="gdn_bwd", grid=(nc,),
        in_specs=[pl.BlockSpec((C, 3 * HW), lambda i: (rev(i), 0)), pl.BlockSpec((C, zw), lambda i: (rev(i), 0)),
                  pl.BlockSpec((C, HW), lambda i: (rev(i), 0)),
                  pl.BlockSpec((1, H, HEAD, HEAD), lambda i: (rev(i), 0, 0, 0)), row, row, row],
        out_specs=[pl.BlockSpec((C, 3 * HW), lambda i: (rev(i), 0)), pl.BlockSpec((C, zw), lambda i: (rev(i), 0)),
                   row, row, row],
        out_shape=[jax.ShapeDtypeStruct(qkv.shape, F32), jax.ShapeDtypeStruct(proj_zs.shape, BF16)]
        + [jax.ShapeDtypeStruct((1, LANES), F32)] * 3,
        scratch_shapes=[pltpu.VMEM((H, HEAD, HEAD), F32)], compiler_params=_params(),
    )(qkv, proj_zs, do, s_all, alog, dtb, gain)


def _hgrn_refs(proj_ref, lb_ref, HP):
    W = HP * HEAD
    return (_split_heads(proj_ref, 0, HP), _split_heads(proj_ref, W, HP), _split_heads(proj_ref, 2 * W, HP),
            _split_heads(proj_ref, 3 * W, HP), _split_heads(lb_ref, 0, HP, slice(0, 1)),
            _split_heads(lb_ref, 0, HP, slice(1, 2)))


def _hgrn_fwd(proj, lower_bounds, gain, nheads):
    T = proj.shape[0]
    C, HP = HGRN_CHUNK, HGRN_HEADS_PER_STEP
    ng, nc, W = nheads // HP, T // C, HP * HEAD

    def body(proj_ref, lb_ref, gain_ref, o_ref, sall_ref, s_scr):
        @pl.when(pl.program_id(1) == 0)
        def _():
            s_scr[...] = jnp.zeros_like(s_scr)

        sall_ref[0] = s_scr[...]
        qr, fr, vi, z, lb0, lb1 = _hgrn_refs(proj_ref, lb_ref, HP)
        outs, states = _hgrn_step(s_scr[...], qr, fr, vi, z, lb0, lb1, gain_ref[...])
        _store_heads(o_ref, 0, outs)
        s_scr[...] = states

    return pl.pallas_call(
        body, name="hgrn_fwd", grid=(ng, nc),
        in_specs=[pl.BlockSpec((C, 4 * W), lambda g, i: (i, g)), pl.BlockSpec((2, W), lambda g, i: (0, g)),
                  pl.BlockSpec((1, LANES), lambda g, i: (0, 0))],
        out_specs=[pl.BlockSpec((C, W), lambda g, i: (i, g)),
                   pl.BlockSpec((1, HP, HEAD, HEAD), lambda g, i: (i, g, 0, 0))],
        out_shape=[jax.ShapeDtypeStruct((T, nheads * HEAD), BF16), jax.ShapeDtypeStruct((nc, nheads, HEAD, HEAD), F32)],
        scratch_shapes=[pltpu.VMEM((HP, HEAD, HEAD), F32)], compiler_params=_params(),
    )(proj, lower_bounds, gain)


def _hgrn_bwd(proj, do, s_all, lower_bounds, gain, nheads):
    T = proj.shape[0]
    C, HP = HGRN_CHUNK, HGRN_HEADS_PER_STEP
    ng, nc, W = nheads // HP, T // C, HP * HEAD

    def body(proj_ref, do_ref, sin_ref, lb_ref, gain_ref, dproj_ref, dlb_ref, dgain_ref, ds_scr):
        first = pl.program_id(1) == 0

        @pl.when(first)
        def _():
            ds_scr[...] = jnp.zeros_like(ds_scr)
            dlb_ref[...] = jnp.zeros_like(dlb_ref)

        @pl.when(first & (pl.program_id(0) == 0))
        def _():
            dgain_ref[...] = jnp.zeros_like(dgain_ref)

        qr, fr, vi, z, lb0, lb1 = _hgrn_refs(proj_ref, lb_ref, HP)
        primals = (sin_ref[0], qr, fr, vi, z, lb0, lb1, gain_ref[...])
        _, vjp = jax.vjp(_hgrn_step, *primals)
        dS, dq, df, dv, dz, dlb0, dlb1, dgain = vjp((_split_heads(do_ref, 0, HP), ds_scr[...]))
        ds_scr[...] = dS
        for part, d in enumerate((dq, df, dv, dz)):
            _store_heads(dproj_ref, part * W, d)
        _store_heads(dlb_ref, 0, dlb0, slice(0, 1), accumulate=True)
        _store_heads(dlb_ref, 0, dlb1, slice(1, 2), accumulate=True)
        dgain_ref[...] += dgain

    rev = lambda i: nc - 1 - i
    return pl.pallas_call(
        body, name="hgrn_bwd", grid=(ng, nc),
        in_specs=[pl.BlockSpec((C, 4 * W), lambda g, i: (rev(i), g)), pl.BlockSpec((C, W), lambda g, i: (rev(i), g)),
                  pl.BlockSpec((1, HP, HEAD, HEAD), lambda g, i: (rev(i), g, 0, 0)),
                  pl.BlockSpec((2, W), lambda g, i: (0, g)), pl.BlockSpec((1, LANES), lambda g, i: (0, 0))],
        out_specs=[pl.BlockSpec((C, 4 * W), lambda g, i: (rev(i), g)), pl.BlockSpec((2, W), lambda g, i: (0, g)),
                   pl.BlockSpec((1, LANES), lambda g, i: (0, 0))],
        out_shape=[jax.ShapeDtypeStruct(proj.shape, BF16), jax.ShapeDtypeStruct(lower_bounds.shape, F32),
                   jax.ShapeDtypeStruct((1, LANES), F32)],
        scratch_shapes=[pltpu.VMEM((HP, HEAD, HEAD), F32)], compiler_params=_params(),
    )(proj, do, s_all, lower_bounds, gain)


def _post_specs(T):
    tr = min(POST_TILE, T)
    tile = lambda w: pl.BlockSpec((tr, w), lambda i: (i, 0))
    full = lambda r, w: pl.BlockSpec((r, w), lambda i: (0, 0))
    return tr, tile, full


def _post_fwd(s, x, p, g, b, wg, wpl, name):
    T, D = x.shape
    P = p.shape[1]
    tr, tile, full = _post_specs(T)

    def body(s_ref, x_ref, p_ref, g_ref, b_ref, wg_ref, wpl_ref, o_ref, o16_ref):
        x1 = _post_norm(s_ref[...], x_ref[...], g_ref[...], b_ref[...])
        xn = _post_gate(x1, _dot_raw(x1, wg_ref[...], "nn", False), _dot_raw(p_ref[...], wpl_ref[...], "nn", False))
        o_ref[...] = xn
        o16_ref[...] = xn.astype(BF16)

    return pl.pallas_call(
        body, name=name, grid=(T // tr,),
        in_specs=[tile(D), tile(D), tile(P), full(1, D), full(1, D), full(D, D), full(P, D)],
        out_specs=[tile(D), tile(D)],
        out_shape=[jax.ShapeDtypeStruct((T, D), F32), jax.ShapeDtypeStruct((T, D), BF16)], compiler_params=_params(),
    )(s, x, p, g, b, wg, wpl)


def _post_bwd(s, x, p, g, b, wg, wpl, dnext, name, with_loss):
    T, D = x.shape
    P = p.shape[1]
    tr, tile, full = _post_specs(T)

    def body(s_ref, x_ref, p_ref, g_ref, b_ref, wg_ref, wpl_ref, dn_ref,
             ds_ref, dx_ref, dg_ref, db_ref, dwg_ref, dwpl_ref, loss_ref):
        @pl.when(pl.program_id(0) == 0)
        def _():
            for r in (dg_ref, db_ref, dwg_ref, dwpl_ref, loss_ref):
                r[...] = jnp.zeros_like(r)

        x1, vjp_norm = jax.vjp(_post_norm, s_ref[...], x_ref[...], g_ref[...], b_ref[...])
        gate_pre = _dot_raw(x1, wg_ref[...], "nn", False)
        pp = _dot_raw(p_ref[...], wpl_ref[...], "nn", False)
        xn, vjp_gate = jax.vjp(_post_gate, x1, gate_pre, pp)
        if with_loss:
            err = xn - dn_ref[...]
            loss_ref[...] += 0.5 * jnp.sum(jnp.sum(err * err, axis=-1, keepdims=True), axis=0, keepdims=True) / D
            dn = err / D
        else:
            dn = dn_ref[...]
        dx1, dgp, dpp = vjp_gate(dn)
        dwg_ref[...] += _dot_raw(x1, dgp, "tn", False)
        dwpl_ref[...] += _dot_raw(p_ref[...], dpp, "tn", False)
        dx1 = dx1 + _dot_raw(dgp, wg_ref[...], "nt", False)
        ds, dx, dg, db = vjp_norm(dx1)
        ds_ref[...] = ds.astype(ds_ref.dtype)
        dx_ref[...] = dx
        dg_ref[...] += dg
        db_ref[...] += db

    return pl.pallas_call(
        body, name=name, grid=(T // tr,),
        in_specs=[tile(D), tile(D), tile(P), full(1, D), full(1, D), full(D, D), full(P, D), tile(D)],
        out_specs=[tile(D), tile(D), full(1, D), full(1, D), full(D, D), full(P, D), full(SUBLANES, LANES)],
        out_shape=[jax.ShapeDtypeStruct((T, D), BF16), jax.ShapeDtypeStruct((T, D), F32)]
        + [jax.ShapeDtypeStruct((1, D), F32)] * 2
        + [jax.ShapeDtypeStruct((D, D), F32), jax.ShapeDtypeStruct((P, D), F32),
           jax.ShapeDtypeStruct((SUBLANES, LANES), F32)],
        compiler_params=_params(),
    )(s, x, p, g, b, wg, wpl, dnext)


def _adam_math(w, g, m, v):
    m = ADAM_B1 * m + (1.0 - ADAM_B1) * g
    v = ADAM_B2 * v + (1.0 - ADAM_B2) * (g * g)
    m_hat = m / (1.0 - ADAM_B1 ** ADAM_STEP)
    v_hat = v / (1.0 - ADAM_B2 ** ADAM_STEP)
    return -ADAM_LR * (m_hat / (jnp.sqrt(v_hat) + ADAM_EPS) + ADAM_WD * w), m, v


def _shard_tiles(R, C):
    tr = _pick(R, (256, 128, 64, 32, 16, 8))
    return (tr, C) if tr < R or R % SUBLANES == 0 else (R, _pick(C, (256, 128)))


def _adam_sharded(w, m, v, g8, got, me, name):
    rows_apart = w.shape[1] == 1 and w.shape[0] > 1
    (L, R, C) = (1, w.shape[0], w.shape[2]) if rows_apart else w.shape
    tr, tc = (R, LANES) if rows_apart else _shard_tiles(R, C)
    nr, nc = R // tr, C // tc
    side_by_side = g8.ndim == 2

    def body(me_ref, w_ref, m_ref, v_ref, p_ref, *rest):
        got_refs, (g_ref, d_ref, mo_ref, vo_ref) = rest[:7], rest[7:]
        g = p_ref[...] if side_by_side else p_ref[0]
        for r in got_refs:
            g = g + r[0].astype(F32)
        if rows_apart:
            d, mn, vn = _adam_math(w_ref[:, 0, :], g, m_ref[:, 0, :], v_ref[:, 0, :])
            for ref, val in ((g_ref, g), (d_ref, d), (mo_ref, mn), (vo_ref, vn)):
                ref[:, 0, :] = val
            return
        d, mn, vn = _adam_math(w_ref[0], g, m_ref[0], v_ref[0])
        g_ref[0] = g
        d_ref[0] = d
        mo_ref[0] = mn
        vo_ref[0] = vn

    if rows_apart:
        t3 = pl.BlockSpec((tr, 1, tc), lambda l, i, j, q: (i, 0, j))
    else:
        t3 = pl.BlockSpec((1, tr, tc), lambda l, i, j, q: (l, i, j))
    slot = lambda k: pl.BlockSpec((1, tr, tc), lambda l, i, j, q: (k, l * nr + i, j))
    if side_by_side:
        mine = pl.BlockSpec((tr, tc), lambda l, i, j, q: (l * nr + i, q[0] * nc + j))
    else:
        mine = pl.BlockSpec((1, tr, tc), lambda l, i, j, q: (q[0], l * nr + i, j))
    return pl.pallas_call(
        body, name=name,
        grid_spec=pltpu.PrefetchScalarGridSpec(
            num_scalar_prefetch=1, grid=(L, nr, nc),
            in_specs=[t3, t3, t3, mine] + [slot(k) for k in range(7)], out_specs=[t3, t3, t3, t3]),
        out_shape=[jax.ShapeDtypeStruct(w.shape, F32)] * 4, compiler_params=_params(),
    )(me, w, m, v, g8, *([got] * 7))


def _small_rows(shapes):
    offsets, r = [], 0
    for rows, _ in shapes:
        offsets.append(r)
        r += rows
    return offsets, -(-(r + 1) // SUBLANES) * SUBLANES, max(cols for _, cols in shapes)


def _pack_small_grads(grads, loss):
    offsets, total, width = _small_rows([g.shape for g in grads])
    packed = jnp.zeros((total, width), F32)
    for g, r in zip(grads, offsets):
        packed = lax.dynamic_update_slice(packed, g, (r, 0))
    return lax.dynamic_update_slice(packed, loss.reshape(1, 1), (total - 1, 0))


def _adam_replicated(params, g8):
    shapes = [w.shape for w, _, _ in params]
    offsets, total, width = _small_rows(shapes)
    n = len(params)

    def body(*refs):
        g_ref, outs, loss_ref, g_scr = refs[3 * n], refs[3 * n + 1:7 * n + 1], refs[7 * n + 1], refs[7 * n + 2]
        g = g_ref[0]
        for k in range(1, 8):
            g = g + g_ref[k]
        g_scr[...] = g
        for i, (rows, cols) in enumerate(shapes):
            gp = g_scr[offsets[i]:offsets[i] + rows, 0:cols]
            d, mn, vn = _adam_math(refs[3 * i][...], gp, refs[3 * i + 1][...], refs[3 * i + 2][...])
            for ref, val in zip(outs[4 * i:4 * i + 4], (gp, d, mn, vn)):
                ref[...] = val
        loss_ref[...] = g_scr[total - 1:total, 0:LANES]

    out = pl.pallas_call(
        body, name="adam_replicated",
        out_shape=[jax.ShapeDtypeStruct(shp, F32) for shp in shapes for _ in range(4)]
        + [jax.ShapeDtypeStruct((1, LANES), F32)],
        scratch_shapes=[pltpu.VMEM((total, width), F32)], compiler_params=_params(),
    )(*[a for triple in params for a in triple], g8)
    return [out[4 * i:4 * i + 4] for i in range(n)], out[4 * n][0, 0]


def _place():
    return lax.axis_index("x"), lax.axis_index("y"), lax.axis_index("c")


def _all_gather(shard, name):
    def body(x_ref, out_ref, send_sems, recv_sems, local_sem):
        x, y, c = _place()
        me, sibling = (x, y, c), (x, y, 1 - c)
        chips = [(1 - x, y), (x, 1 - y), (1 - x, 1 - y)]

        def slab(px, py, pc):
            return out_ref.at[4 * px + 2 * py + pc]

        def copy(k, block, to, src=None):
            return pltpu.make_async_remote_copy(
                src_ref=slab(*block) if src is None else src, dst_ref=slab(*block),
                send_sem=send_sems.at[k], recv_sem=recv_sems.at[k], device_id=to, device_id_type=MESH)

        mine = pltpu.make_async_copy(x_ref, slab(*me), local_sem)
        mine.start()
        first = [copy(0, me, sibling, src=x_ref)]
        first += [copy(1 + j, me, (*chip, c), src=x_ref) for j, chip in enumerate(chips)]
        for cp in first:
            cp.start()
        passed = [copy(4 + j, (*chip, c), sibling) for j, chip in enumerate(chips)]
        for j, chip in enumerate(chips):
            copy(1 + j, (*chip, c), me).wait_recv()
            passed[j].start()
        copy(0, sibling, me).wait_recv()
        for j, chip in enumerate(chips):
            copy(4 + j, (*chip, 1 - c), me).wait_recv()
        for cp in first + passed:
            cp.wait_send()
        mine.wait()

    return pl.pallas_call(
        body, name=name, out_shape=jax.ShapeDtypeStruct((8,) + shard.shape, shard.dtype),
        in_specs=[pl.BlockSpec(memory_space=pl.ANY)], out_specs=pl.BlockSpec(memory_space=pl.ANY),
        scratch_shapes=[pltpu.SemaphoreType.DMA((7,)), pltpu.SemaphoreType.DMA((7,)), pltpu.SemaphoreType.DMA],
    )(shard)


_HBM = pl.BlockSpec(memory_space=pltpu.HBM)
_SEM = pl.BlockSpec(memory_space=pltpu.SEMAPHORE)
_DATAFLOW = pltpu.SideEffectType.DATAFLOW_SIDE_EFFECTING
TOKEN_SHAPE = (SUBLANES, LANES)


def _peers(x, y, c):
    flip = lambda v, bit: 1 - v if bit else v
    return [(flip(x, r >> 2 & 1), flip(y, r >> 1 & 1), flip(c, r & 1)) for r in range(1, 8)]


def _scatter_plan(x, y, c):
    return [(4 * px + 2 * py + pc, k, (px, py, pc)) for k, (px, py, pc) in enumerate(_peers(x, y, c))]


def _exchange_copies(plan, src_ref, land_ref, send_sems, recv_sems):
    C = land_ref.shape[-1]
    block = (lambda b: src_ref.at[b]) if len(src_ref.shape) == len(land_ref.shape) else (
        lambda b: src_ref.at[:, pl.ds(b * C, C)])
    return [pltpu.make_async_remote_copy(
        src_ref=block(blk), dst_ref=land_ref.at[slot], send_sem=send_sems.at[k], recv_sem=recv_sems.at[k],
        device_id=peer, device_id_type=MESH) for k, (blk, slot, peer) in enumerate(plan(*_place()))]


def _exchange_start(src, n_slots, block_shape, plan, name):
    land_shape = (n_slots,) + tuple(block_shape)
    n = len(plan(0, 0, 0))

    def body(src_ref, land_ref, send_sems, recv_sems, src_thru, land_thru, token):
        for cp in _exchange_copies(plan, src_ref, land_ref, send_sems, recv_sems):
            cp.start()
        token[...] = jnp.zeros_like(token)

    return pl.pallas_call(
        body, name=name,
        out_shape=(pltpu.SemaphoreType.DMA((n,)), pltpu.SemaphoreType.DMA((n,)), pltpu.HBM(src.shape, src.dtype),
                   pltpu.HBM(land_shape, src.dtype), jax.ShapeDtypeStruct(TOKEN_SHAPE, F32)),
        in_specs=(_HBM, _HBM), out_specs=(_SEM, _SEM, _HBM, _HBM, pl.BlockSpec(memory_space=pltpu.VMEM)),
        input_output_aliases={0: 2, 1: 3}, compiler_params=pltpu.CompilerParams(has_side_effects=_DATAFLOW),
    )(pltpu.with_memory_space_constraint(src, pltpu.HBM),
      pltpu.with_memory_space_constraint(lax.empty(land_shape, src.dtype), pltpu.HBM))


def _exchange_wait(handle, plan, after, name):
    send_sems, recv_sems, src_thru, land_thru, _ = handle

    def body(src_ref, land_ref, send_sems, recv_sems, after_ref, src_dead, got_ref):
        for cp in _exchange_copies(plan, src_ref, land_ref, send_sems, recv_sems):
            cp.wait_send()
            cp.wait_recv()

    return pl.pallas_call(
        body, name=name,
        out_shape=(pltpu.HBM(src_thru.shape, src_thru.dtype), pltpu.HBM(land_thru.shape, land_thru.dtype)),
        in_specs=(_HBM, _HBM, _SEM, _SEM, pl.BlockSpec(memory_space=pl.ANY)), out_specs=(_HBM, _HBM),
        input_output_aliases={0: 0, 1: 1}, compiler_params=pltpu.CompilerParams(has_side_effects=_DATAFLOW),
    )(src_thru, land_thru, send_sems, recv_sems, after)[1]


def _gather_plan(x, y, c):
    return [(0, 4 * x + 2 * y + c, peer) for peer in _peers(x, y, c)]


class _LateGather:
    def __init__(self, shard, name):
        self.shard, self.name = shard, name
        self.handle = _exchange_start(shard[None], 8, shard.shape, _gather_plan, name + "_start")

    def get(self, after):
        land = _exchange_wait(self.handle, _gather_plan, after, self.name + "_wait")
        x, y, c = _place()
        return lax.dynamic_update_slice(land, self.shard[None], (4 * x + 2 * y + c, 0, 0))


class _GradExchange:
    def __init__(self, me, layouts):
        self.me, self.layouts, self.pending = me, layouts, {}

    def start(self, tag, grad, grad16=None):
        g8 = self.layouts[tag](grad)
        g16 = g8.astype(BF16) if grad16 is None else self.layouts[tag](grad16)
        block_shape = g8.shape[1:] if g8.ndim == 3 else (g8.shape[0], g8.shape[1] // 8)
        handle = _exchange_start(g16, 7, block_shape, _scatter_plan, "rs_start_" + tag)
        self.pending[tag] = (g8, handle)
        return handle[4]

    def finish(self, tag, w, m, v, after):
        g8, handle = self.pending.pop(tag)
        got = _exchange_wait(handle, _scatter_plan, after, "rs_wait_" + tag)
        return _adam_sharded(w, m, v, g8, got, self.me, "adam_" + tag)


def _local_grads(x, p0, p1, target, wt_zs, wt_a, wt_qkv, late, conv_a, conv_b,
                 a_log, dt_bias, gdn_gain, lower_bounds, hgrn_gain, ln_g, ln_b, on_grad=None):
    H = a_log.shape[1]
    pad_small = ((0, 0), (H, LANES - 2 * H))
    alog_row = jnp.pad(a_log, pad_small)
    dtb_row = jnp.pad(dt_bias, pad_small)

    x16 = x.astype(BF16)
    proj_zs = _matmul(x16, wt_zs, "nt", "proj_even_zs", after=late.started)
    proj_a = _matmul(x16, wt_a, "nt", "proj_even_a", after=late.started)
    proj_qkv = _matmul(x16, wt_qkv, "nt", "proj_even_qkv", after=late.started)
    y_a = _mixer_a_fwd(proj_a, conv_a)
    qkv = _conv_b_fwd(proj_qkv, conv_b)
    o2, s_gdn = _gdn_fwd(qkv, proj_zs, alog_row, dtb_row, gdn_gain, H)
    woute_a, woute_b = late.out_even(o2)
    wg, wpl = late.gate(o2)
    s_e = _matmul(o2, woute_b, "nn", "out_even_b", add=_matmul(y_a, woute_a, "nn", "out_even_a"))
    x2, x2_16 = _post_fwd(s_e, x, p0, ln_g[0:1], ln_b[0:1], wg[0], wpl[0], "post_even_fwd")
    wino, wouto = late.odd(s_e)
    nheads_o = wouto.shape[0] // HEAD
    proj_o = _matmul(x2_16, wino, "nn", "proj_odd")
    o4, s_hgrn = _hgrn_fwd(proj_o, lower_bounds, hgrn_gain, nheads_o)
    s_o = _matmul(o4, wouto, "nn", "out_odd")
    ds_o, dx2, dlng1, dlnb1, dwg1, dwpl1, loss = _post_bwd(
        s_o, x2, p1, ln_g[1:2], ln_b[1:2], wg[1], wpl[1], target, "post_odd_loss_bwd", True)
    do4 = _matmul(ds_o, wouto, "nt", "d_out_odd_act")
    grads = {}

    def emit(tag, grad, grad16=None):
        grads[tag] = grad
        return on_grad(tag, grad, grad16) if on_grad is not None else jnp.zeros(TOKEN_SHAPE, F32)

    tok = emit("w_out_odd", *_matmul(o4, ds_o, "tn", "d_out_odd_w", also_bf16=True))
    dproj_o, dlb, dhgain = _hgrn_bwd(proj_o, do4, s_hgrn, lower_bounds, hgrn_gain + tok[0:1], nheads_o)
    dx2 = _matmul(dproj_o, wino, "nt", "d_proj_odd_act", add=dx2)
    tok = emit("w_in_odd", *_matmul(x2_16, dproj_o, "tn", "d_proj_odd_w", also_bf16=True))
    ds_e, dx, dlng0, dlnb0, dwg0, dwpl0, _ = _post_bwd(
        s_e, x, p0, ln_g[0:1], ln_b[0:1] + tok[0:1, 0:1], wg[0], wpl[0], dx2, "post_even_bwd", False)
    tok = emit("w_pl_gate", jnp.stack([dwg0, dwg1])) + emit("w_pl", jnp.stack([dwpl0, dwpl1]))
    dy_a = _matmul(ds_e, woute_a, "nt", "d_out_even_a_act")
    do2 = _matmul(ds_e, woute_b, "nt", "d_out_even_b_act")
    dwoute_a = _matmul(y_a, ds_e, "tn", "d_out_even_a_w")
    dwoute_b = _matmul(o2, ds_e, "tn", "d_out_even_b_w")
    tok = tok + emit("w_out_even", jnp.concatenate([dwoute_a, dwoute_b], axis=0))
    dqkv, dproj_zs, dalog, ddtb, dggain = _gdn_bwd(qkv, proj_zs, do2, s_gdn, alog_row, dtb_row, gdn_gain + tok[0:1], H)
    dproj_qkv, dconv_b = _conv_b_bwd(proj_qkv, dqkv, conv_b)
    dproj_a, dconv_a = _mixer_a_bwd(proj_a, dy_a, conv_a)
    emit("conv", (dconv_a[:conv_a.shape[0]], dconv_b[:conv_b.shape[0]]))
    tok = emit("w_in_even", (_matmul(dproj_zs, x16, "tn", "d_proj_even_zs_w"), _matmul(dproj_a, x16, "tn", "d_proj_even_a_w"),
                             _matmul(dproj_qkv, x16, "tn", "d_proj_even_qkv_w")))
    dx = _matmul(dproj_zs, wt_zs, "nn", "d_proj_even_zs_act", add=dx, after=tok)
    dx = _matmul(dproj_a, wt_a, "nn", "d_proj_even_a_act", add=dx)
    dx = _matmul(dproj_qkv, wt_qkv, "nn", "d_proj_even_qkv_act", add=dx)
    grads.update(
        loss=loss[0, 0], grad_x=dx, a_log=dalog[:, H:2 * H], dt_bias=ddtb[:, H:2 * H], gdn_gain=dggain,
        lower_bounds=dlb, hgrn_gain=dhgain, ln_g=jnp.concatenate([dlng0, dlng1], axis=0),
        ln_b=jnp.concatenate([dlnb0, dlnb1], axis=0))
    return grads


def _pad_rows(a, rows):
    return jnp.pad(a, ((0, rows - a.shape[0]), (0, 0)))


def _split_in_even(wt_full, AW, HW, H):
    D = wt_full.shape[1]
    n_a = 4 * AW
    n_main = n_a + 3 * HW
    wt_zs = jnp.concatenate([wt_full[n_main:n_main + HW], wt_full[n_main + HW:],
                             jnp.zeros((LANES - 2 * H, D), wt_full.dtype)], axis=0)
    wt_a = wt_full[:n_a].reshape(4, AW // MIXER_LANES, MIXER_LANES, D).transpose(1, 0, 2, 3).reshape(n_a, D)
    return wt_zs, wt_a, wt_full[n_a:n_main]


def _join_in_even(dt_zs, dt_a, dt_qkv, AW, HW, H):
    D = dt_a.shape[1]
    a_nat = dt_a.reshape(AW // MIXER_LANES, 4, MIXER_LANES, D).transpose(1, 0, 2, 3).reshape(4 * AW, D)
    return jnp.concatenate([a_nat, dt_qkv, dt_zs[:HW], dt_zs[HW:HW + 2 * H]], axis=0)


def kernel(x, p, w_in_even, conv_a_w, conv_b_w, a_log, dt_bias, gdn_norm_g, w_out_even, w_in_odd, lower_bounds, hgrn_norm_g, w_out_odd, ln_g, ln_b, w_pl, w_pl_gate, loss_target, m_w_in_even, m_conv_a_w, m_conv_b_w, m_a_log, m_dt_bias, m_gdn_norm_g, m_w_out_even, m_w_in_odd, m_lower_bounds, m_hgrn_norm_g, m_w_out_odd, m_ln_g, m_ln_b, m_w_pl, m_w_pl_gate, v_w_in_even, v_conv_a_w, v_conv_b_w, v_a_log, v_dt_bias, v_gdn_norm_g, v_w_out_even, v_w_in_odd, v_lower_bounds, v_hgrn_norm_g, v_w_out_odd, v_ln_g, v_ln_b, v_w_pl, v_w_pl_gate):
    xi, yi, ci = _place()
    me = jnp.reshape(4 * xi + 2 * yi + ci, (1,)).astype(jnp.int32)
    D = x.shape[2]
    H = a_log.shape[1]
    HW = H * HEAD
    AW = conv_a_w.shape[2] * 8
    OW = w_out_odd.shape[1] * 8
    PD = w_pl.shape[1]
    ka, kb = conv_a_w.shape[1], conv_b_w.shape[1]
    ca, cb = conv_a_w.shape[2], conv_b_w.shape[2]
    gw = HGRN_HEADS_PER_STEP * HEAD
    ngrp = OW // gw

    transposed = lambda a: jnp.transpose(a, (0, 2, 1))
    g_ine = _all_gather(transposed(w_in_even)[0].astype(BF16), "ag_w_in_even")
    wt_zs, wt_a, wt_qkv = _split_in_even(g_ine.reshape(-1, D), AW, HW, H)
    behind = lambda shard, dep: lax.optimization_barrier((shard, dep))[0]
    late_oute = _LateGather(behind(w_out_even[0].astype(BF16), g_ine), "ag_w_out_even")
    late_gate = _LateGather(behind(w_pl_gate.astype(BF16).reshape(-1, D), late_oute.handle[4]), "ag_w_pl_gate")
    late_pl = _LateGather(behind(w_pl.astype(BF16).reshape(DEPTH * PD, -1), late_gate.handle[4]), "ag_w_pl")
    late_ino = _LateGather(behind(w_in_odd[0].astype(BF16), late_pl.handle[4]), "ag_w_in_odd")
    late_outo = _LateGather(behind(w_out_odd[0].astype(BF16), late_ino.handle[4]), "ag_w_out_odd")

    class _Late:
        started = sum(g.handle[4] for g in (late_oute, late_gate, late_pl, late_ino, late_outo))

        @staticmethod
        def out_even(after):
            woute = late_oute.get(after).reshape(-1, D)
            return woute[:AW], woute[AW:]

        @staticmethod
        def gate(after):
            g_gate, g_pl = late_gate.get(after), late_pl.get(after)
            return (g_gate.reshape(8, DEPTH, D // 8, D).transpose(1, 0, 2, 3).reshape(DEPTH, D, D),
                    g_pl.reshape(8, DEPTH, PD, D // 8).transpose(1, 2, 0, 3).reshape(DEPTH, PD, D))

        @staticmethod
        def odd(after):
            g_ino = late_ino.get(after)
            wino = jnp.transpose(g_ino, (1, 0, 2)).reshape(D, 4, ngrp, gw).transpose(0, 2, 1, 3).reshape(D, 4 * OW)
            return wino, late_outo.get(after).reshape(-1, D)

    taps = jnp.concatenate([_pad_rows(conv_a_w[0], SUBLANES), _pad_rows(conv_b_w[0], SUBLANES)], axis=1)
    g_taps = _all_gather(taps, "ag_conv")
    conv_a = jnp.transpose(g_taps[:, :ka, :ca], (1, 0, 2)).reshape(ka, 8 * ca)
    conv_b = jnp.transpose(g_taps[:, :kb, ca:], (1, 0, 2)).reshape(kb, 8 * cb)

    sh = w_in_even.shape[2]
    tap_blocks = lambda g, width: _pad_rows(g, SUBLANES).reshape(SUBLANES, 8, width).transpose(1, 0, 2)
    owner_layout = {
        "w_in_even": lambda g: _join_in_even(*g, AW, HW, H).reshape(8, sh, D),
        "w_in_odd": lambda g: g.reshape(D, ngrp, 4, gw).transpose(0, 2, 1, 3).reshape(D, 4 * OW),
        "w_out_even": lambda g: g.reshape(8, -1, D),
        "w_out_odd": lambda g: g.reshape(8, -1, D),
        "w_pl_gate": lambda g: g.reshape(DEPTH, 8, D // 8, D).transpose(1, 0, 2, 3).reshape(8, DEPTH * D // 8, D),
        "w_pl": lambda g: g.reshape(DEPTH, PD, 8, D // 8).transpose(2, 0, 1, 3).reshape(8, DEPTH * PD, D // 8),
        "conv": lambda g: jnp.concatenate([tap_blocks(g[0], ca), tap_blocks(g[1], cb)], axis=2),
    }
    exchange = _GradExchange(me, owner_layout)
    gr = _local_grads(x[0], p[0, 0], p[1, 0], loss_target[0], wt_zs, wt_a, wt_qkv, _Late, conv_a, conv_b,
                      a_log, dt_bias, gdn_norm_g, lower_bounds, hgrn_norm_g, ln_g, ln_b, on_grad=exchange.start)

    last = gr["grad_x"]
    pack_taps = lambda a, b: jnp.concatenate([_pad_rows(a[0], SUBLANES), _pad_rows(b[0], SUBLANES)], axis=1)[None]
    o_outo = exchange.finish("w_out_odd", w_out_odd, m_w_out_odd, v_w_out_odd, last)
    o_ino = exchange.finish("w_in_odd", w_in_odd, m_w_in_odd, v_w_in_odd, last)
    o_gate = exchange.finish("w_pl_gate", w_pl_gate, m_w_pl_gate, v_w_pl_gate, last)
    o_pl = exchange.finish("w_pl", w_pl, m_w_pl, v_w_pl, last)
    o_oute = exchange.finish("w_out_even", w_out_even, m_w_out_even, v_w_out_even, last)
    o_taps = exchange.finish("conv", taps[None], pack_taps(m_conv_a_w, m_conv_b_w), pack_taps(v_conv_a_w, v_conv_b_w), last)
    others_done = sum(o[1][0, 0:1, 0:1] for o in (o_outo, o_ino, o_gate, o_pl, o_oute, o_taps))
    rows_first = lambda a: jnp.transpose(a, (2, 0, 1))
    o_ine = [jnp.transpose(o, (1, 2, 0)) for o in exchange.finish(
        "w_in_even", rows_first(w_in_even), rows_first(m_w_in_even), rows_first(v_w_in_even), others_done)]

    small_g = _pack_small_grads([gr["a_log"], gr["dt_bias"], gr["gdn_gain"], gr["lower_bounds"], gr["hgrn_gain"],
                                 gr["ln_g"], gr["ln_b"]], gr["loss"])
    o_small, loss = _adam_replicated(
        [(a_log, m_a_log, v_a_log), (dt_bias, m_dt_bias, v_dt_bias), (gdn_norm_g, m_gdn_norm_g, v_gdn_norm_g),
         (lower_bounds, m_lower_bounds, v_lower_bounds), (hgrn_norm_g, m_hgrn_norm_g, v_hgrn_norm_g),
         (ln_g, m_ln_g, v_ln_g), (ln_b, m_ln_b, v_ln_b)], _all_gather(small_g, "ag_small_grads"))

    def leaves(kind):
        s_alog, s_dt, s_gg, s_lb, s_hg, s_lng, s_lnb = (o[kind] for o in o_small)
        t = o_taps[kind]
        return [o_ine[kind], t[:, :ka, :ca], t[:, :kb, ca:], s_alog, s_dt, s_gg, o_oute[kind],
                o_ino[kind], s_lb, s_hg, o_outo[kind], s_lng, s_lnb, o_pl[kind], o_gate[kind]]

    return (loss, gr["grad_x"][None], *leaves(0), *leaves(1), *leaves(2), *leaves(3))
```

```python
import functools

import jax
import jax.numpy as jnp
from jax import lax
from jax.experimental import pallas as pl
from jax.experimental.pallas import tpu as pltpu

F32 = jnp.float32
BF16 = jnp.bfloat16
MESH = pl.DeviceIdType.MESH

LANES = 128
SUBLANES = 8
HEAD = 128
GDN_CHUNK = 64
HGRN_CHUNK = 64
HGRN_SUB = 16
HGRN_HEADS_PER_STEP = 16
NORM_EPS = 1e-5
DEPTH = 2
ALPHA = (2.0 * DEPTH) ** 0.25
EXP_CLAMP = 80.0
ADAM_LR, ADAM_B1, ADAM_B2, ADAM_EPS, ADAM_WD, ADAM_STEP = 0.001, 0.9, 0.999, 1e-08, 0.01, 10
VMEM_LIMIT = 56 * 1024 * 1024
MATMUL_VMEM = 36 * 1024 * 1024
ROW_TILE = 512
MIXER_LANES = 256
CONV_LANES = 512
POST_TILE = 512

_NOBATCH, _BATCH0 = ((), ()), ((0,), (0,))
_DIMS = {"nn": (((1,), (0,)), _NOBATCH), "nt": (((1,), (1,)), _NOBATCH), "tn": (((0,), (0,)), _NOBATCH),
         "bnn": (((2,), (1,)), _BATCH0), "bnt": (((2,), (2,)), _BATCH0), "btn": (((1,), (1,)), _BATCH0)}


def _params(**kw):
    return pltpu.CompilerParams(vmem_limit_bytes=VMEM_LIMIT, **kw)


def _dot_raw(a, b, kind, hi):
    if hi:
        return lax.dot_general(a, b, _DIMS[kind], precision=lax.Precision.HIGHEST, preferred_element_type=F32)
    return lax.dot_general(a.astype(BF16), b.astype(BF16), _DIMS[kind], preferred_element_type=F32)


@functools.partial(jax.custom_vjp, nondiff_argnums=(2, 3))
def mdot(a, b, kind, hi):
    return _dot_raw(a, b, kind, hi)


def _mdot_fwd(a, b, kind, hi):
    return _dot_raw(a, b, kind, hi), (a, b)


def _mdot_bwd(kind, hi, res, g):
    a, b = res
    pre, base = kind[:-2], kind[-2:]
    if base == "nn":
        return _dot_raw(g, b, pre + "nt", hi), _dot_raw(a, g, pre + "tn", hi)
    if base == "nt":
        return _dot_raw(g, b, pre + "nn", hi), _dot_raw(g, a, pre + "tn", hi)
    return _dot_raw(b, g, pre + "nt", hi), _dot_raw(a, g, pre + "nn", hi)


mdot.defvjp(_mdot_fwd, _mdot_bwd)


def _rows(x, lo, hi):
    return _take_rows(x, lo, hi, x.shape[-2])


@functools.partial(jax.custom_vjp, nondiff_argnums=(1, 2, 3))
def _take_rows(x, lo, hi, n):
    return x[..., lo:hi, :]


def _take_rows_fwd(x, lo, hi, n):
    return x[..., lo:hi, :], None


def _take_rows_bwd(lo, hi, n, _, g):
    parts = []
    if lo > 0:
        parts.append(jnp.zeros(g.shape[:-2] + (lo, g.shape[-1]), g.dtype))
    parts.append(g)
    if n - hi > 0:
        parts.append(jnp.zeros(g.shape[:-2] + (n - hi, g.shape[-1]), g.dtype))
    return (jnp.concatenate(parts, axis=-2) if len(parts) > 1 else g,)


_take_rows.defvjp(_take_rows_fwd, _take_rows_bwd)


def _heads_of(wide, nheads):
    return jnp.stack([wide[:, h * HEAD:(h + 1) * HEAD] for h in range(nheads)], axis=0)


def _wide_of(x):
    return jnp.concatenate([x[h] for h in range(x.shape[0])], axis=1)


@functools.partial(jax.custom_vjp, nondiff_argnums=(1,))
def to_heads(wide, nheads):
    return _heads_of(wide, nheads)


to_heads.defvjp(lambda wide, nheads: (_heads_of(wide, nheads), None), lambda nheads, _, g: (_wide_of(g),))


@jax.custom_vjp
def to_wide(x):
    return _wide_of(x)


to_wide.defvjp(lambda x: (_wide_of(x), None), lambda _, g: (_heads_of(g, g.shape[1] // HEAD),))


def _sigmoid(x):
    return jax.nn.sigmoid(x)


def _silu(x):
    return x * _sigmoid(x)


def _dsilu(x):
    s = _sigmoid(x)
    return s * (1.0 + x * (1.0 - s))


def _log1p(u):
    return jnp.where(u < 1e-4, u * (1.0 - 0.5 * u), jnp.log(1.0 + u))


def _softplus(x):
    return jnp.maximum(x, 0.0) + _log1p(jnp.exp(-jnp.abs(x)))


def _rms_gate(o, gain, z):
    return o * lax.rsqrt(jnp.mean(o * o, axis=-1, keepdims=True) + NORM_EPS) * gain * _silu(z)


def _l2n(x):
    return x * lax.rsqrt(jnp.sum(x * x, axis=-1, keepdims=True) + 1e-6)


def _split_dot_raw(m, x, kind):
    mb = m.astype(BF16)
    hi = x.astype(BF16)
    lo = (x - hi.astype(F32)).astype(BF16)
    dims = _DIMS[kind]
    return (lax.dot_general(mb, hi, dims, preferred_element_type=F32)
            + lax.dot_general(mb, lo, dims, preferred_element_type=F32))


@jax.custom_vjp
def mask_dot(m, x):
    return _split_dot_raw(m, x, "nn")


def _mask_dot_fwd(m, x):
    return _split_dot_raw(m, x, "nn"), m


def _mask_dot_bwd(m, g):
    return jnp.zeros_like(m), _split_dot_raw(m, g, "tn")


mask_dot.defvjp(_mask_dot_fwd, _mask_dot_bwd)


def _neumann_rest(low):
    n = low.shape[-1]
    rest = -low
    power = low
    span = 2
    while span < n:
        power = _dot_raw(power, power, "bnn", False)
        rest = rest + power + _dot_raw(rest, power, "bnn", False)
        span *= 2
    return rest


@jax.custom_vjp
def _unit_lower_inverse_minus_eye(low):
    return _neumann_rest(low)


def _inverse_fwd(low):
    rest = _neumann_rest(low)
    return rest, rest


def _inverse_bwd(rest, g):
    left = g + _dot_raw(rest, g, "btn", False)
    return (-(left + _dot_raw(left, rest, "bnt", False)),)


_unit_lower_inverse_minus_eye.defvjp(_inverse_fwd, _inverse_bwd)


def _gdn_step(S, q, k, v, z, small, alog, dtb, gain):
    H = S.shape[0]
    C = GDN_CHUNK
    row = lax.broadcasted_iota(jnp.int32, (C, C), 0)
    col = lax.broadcasted_iota(jnp.int32, (C, C), 1)
    tril, strict, eye = (row >= col)[None], (row > col)[None], (row == col)[None]
    head = lax.broadcasted_iota(jnp.int32, (H, 1, LANES), 0)
    lane = lax.broadcasted_iota(jnp.int32, (H, 1, LANES), 2)
    rowc = lax.broadcasted_iota(jnp.int32, (1, C, 1), 1)
    beta_all = _sigmoid(small)
    g_all = -jnp.exp(alog) * _softplus(small + dtb)
    gc_all = mask_dot((row >= col).astype(F32), g_all)
    beta = jnp.sum(jnp.where(lane == head, beta_all[None], 0.0), axis=-1, keepdims=True)
    gc = jnp.sum(jnp.where(lane == head + H, gc_all[None], 0.0), axis=-1, keepdims=True)
    gc_row = jnp.sum(jnp.where(eye, gc, 0.0), axis=1, keepdims=True)
    decay = jnp.where(tril, jnp.exp(jnp.where(tril, gc - gc_row, 0.0)), 0.0)
    g_last = jnp.sum(jnp.where(rowc == C - 1, gc, 0.0), axis=1, keepdims=True)
    qn = _l2n(q) * (HEAD ** -0.5)
    kn = _l2n(k)
    kb = kn * beta
    low = jnp.where(strict, beta * mdot(kn, kn, "bnt", False) * decay, 0.0)
    inv_rest = _unit_lower_inverse_minus_eye(low)
    eg = jnp.exp(gc)
    vb, kbe = v * beta, kb * eg
    u = vb + mdot(inv_rest, vb, "bnn", False)
    w = kbe + mdot(inv_rest, kbe, "bnn", False)
    attn = mdot(qn, kn, "bnt", False) * decay
    v_new = u - mdot(w, S, "bnn", False)
    o = mdot(qn * eg, S, "bnn", False) + mdot(attn, v_new, "bnn", False)
    k_dec = kn * jnp.exp(g_last - gc)
    return _rms_gate(o, gain, z), S * jnp.exp(g_last) + mdot(k_dec, v_new, "btn", False)


def _hgrn_step(St, qr, fr, vi, z, lb0, lb1, gain):
    H = St.shape[0]
    C, SB = HGRN_CHUNK, HGRN_SUB
    row = lax.broadcasted_iota(jnp.int32, (C, C), 0)
    col = lax.broadcasted_iota(jnp.int32, (C, C), 1)
    blk_start = row - (row & (SB - 1))
    in_blk_f = ((row >= col) & (col >= blk_start)).astype(F32)
    before_f = (col < blk_start).astype(F32)
    sums_f = jnp.concatenate([in_blk_f, before_f], axis=0)
    m = jnp.maximum(lb0, lb1)
    e0, e1 = jnp.exp(lb0 - m), jnp.exp(lb1 - m)
    lb = e1 / (e0 + e1)
    f = lb + (1.0 - lb) * _sigmoid(fr)
    q = _silu(qr)
    k = 1.0 - f
    logf = jnp.log(f)
    sums = mask_dot(sums_f, to_wide(logf))
    inner, start = to_heads(_rows(sums, 0, C), H), to_heads(_rows(sums, C, 2 * C), H)
    b = start + inner
    b_last = jnp.sum(logf, axis=1, keepdims=True)
    o = mdot(q * jnp.exp(b), St, "bnt", False)
    qt = q * jnp.exp(inner)
    parts = []
    for blk in range(C // SB):
        lo, n = blk * SB, (blk + 1) * SB
        ref = jnp.concatenate([_rows(start, lo, n)] * (blk + 1), axis=1)
        kt = _rows(k, 0, n) * jnp.exp(jnp.minimum(ref - _rows(b, 0, n), EXP_CLAMP))
        att = mdot(_rows(qt, lo, n), kt, "bnt", False)
        t_idx = lax.broadcasted_iota(jnp.int32, (1, SB, n), 1) + lo
        s_idx = lax.broadcasted_iota(jnp.int32, (1, SB, n), 2)
        att = jnp.where(s_idx <= t_idx, att, 0.0)
        parts.append(mdot(att, _rows(vi, 0, n), "bnn", False))
    o = o + jnp.concatenate(parts, axis=1)
    k_dec = k * jnp.exp(b_last - b)
    return _rms_gate(o, gain, z), St * jnp.exp(b_last) + mdot(vi, k_dec, "btn", False)


def _post_norm(s, x, g, b):
    r = ALPHA * x + s
    d = r - jnp.mean(r, axis=-1, keepdims=True)
    var = jnp.mean(d * d, axis=-1, keepdims=True)
    return d * lax.rsqrt(var + NORM_EPS) * g + b


def _post_gate(x1, gate_pre, pp):
    return x1 + pp * _sigmoid(gate_pre)


def _pick(dim, cands):
    for c in cands:
        if dim % c == 0:
            return c
    return dim


def _matmul_tiles(M, K, tn, a_bytes, b_bytes, has_add):
    for tk in (4096, 2048, 1536, 1152, 1024, 640, 512, 384, 256, 128):
        if K % tk:
            continue
        for tm in (2048, 1152, 1024, 512, 384, 256, 128):
            if M % tm:
                continue
            blocks = tm * tk * a_bytes + tk * tn * b_bytes + tm * tn * 4 * (2 if has_add else 1)
            if 2 * blocks + (tm * tn * 4 if tk < K else 0) <= MATMUL_VMEM and tm >= min(M, 1024):
                return tm, tk
    return _pick(M, (512, 256, 128)), _pick(K, (512, 256, 128))


def _matmul(a, b, kind, name, add=None, after=None, also_bf16=False):
    if kind == "nn":
        (M, K), N = a.shape, b.shape[1]
    elif kind == "nt":
        (M, K), N = a.shape, b.shape[0]
    else:
        (K, M), N = a.shape, b.shape[1]
    has_add = add is not None
    tn = _pick(N, (512, 640, 384, 256, 128))
    tm, tk = _matmul_tiles(M, K, tn, a.dtype.itemsize, b.dtype.itemsize, has_add)
    nk = K // tk
    a_spec = pl.BlockSpec((tk, tm), lambda i, j, k: (k, i)) if kind == "tn" else pl.BlockSpec((tm, tk), lambda i, j, k: (i, k))
    b_spec = pl.BlockSpec((tn, tk), lambda i, j, k: (j, k)) if kind == "nt" else pl.BlockSpec((tk, tn), lambda i, j, k: (k, j))
    o_spec = pl.BlockSpec((tm, tn), lambda i, j, k: (i, j))

    extra = ([add] if has_add else []) + ([after] if after is not None else [])
    extra_specs = ([o_spec] if has_add else []) + ([pl.BlockSpec(TOKEN_SHAPE, lambda i, j, k: (0, 0))] if after is not None else [])

    out_dtypes = (F32, BF16) if also_bf16 else (F32,)

    def body(a_ref, b_ref, *rest):
        outs = rest[len(extra):len(extra) + len(out_dtypes)]

        def write(val):
            if has_add:
                val = val + rest[0][...]
            for o_ref in outs:
                o_ref[...] = val.astype(o_ref.dtype)

        part = _dot_raw(a_ref[...], b_ref[...], kind, False)
        if nk == 1:
            write(part)
            return
        acc = rest[-1]
        kk = pl.program_id(2)

        @pl.when(kk == 0)
        def _():
            acc[...] = part

        @pl.when(kk > 0)
        def _():
            acc[...] += part

        @pl.when(kk == nk - 1)
        def _():
            write(acc[...])

    result = pl.pallas_call(
        body, name=name, grid=(M // tm, N // tn, nk),
        in_specs=[a_spec, b_spec] + extra_specs,
        out_specs=[o_spec] * len(out_dtypes), out_shape=[jax.ShapeDtypeStruct((M, N), dt) for dt in out_dtypes],
        scratch_shapes=[pltpu.VMEM((tm, tn), F32)] if nk > 1 else [],
        compiler_params=_params(dimension_semantics=("parallel", "parallel", "arbitrary")),
    )(a, b, *extra)
    return result if also_bf16 else result[0]


def _halo_specs(ts, nt, width, prev=True, main=True, nxt=True):
    per = ts // SUBLANES
    last8 = nt * per - 1
    specs = []
    if prev:
        specs.append(pl.BlockSpec((SUBLANES, width), lambda cb, i: (jnp.maximum(i * per - 1, 0), cb)))
    if main:
        specs.append(pl.BlockSpec((ts, width), lambda cb, i: (i, cb)))
    if nxt:
        specs.append(pl.BlockSpec((SUBLANES, width), lambda cb, i: (jnp.minimum((i + 1) * per, last8), cb)))
    return specs


def _taps(ext, ktaps, lo, size):
    return [ext[lo:lo + size] if j == 0 else pltpu.roll(ext, j, 0)[lo:lo + size] for j in range(ktaps)]


def _ahead(ext, j, size):
    n = ext.shape[0]
    return ext[:size] if j == 0 else pltpu.roll(ext, n - j, 0)[:size]


def _lane_block(ref, k):
    return ref[:, k * MIXER_LANES:(k + 1) * MIXER_LANES]


def _mixer_a_fwd(proj_a, conv_w):
    T = proj_a.shape[0]
    nblk = proj_a.shape[1] // (4 * MIXER_LANES)
    ts = min(ROW_TILE, T)
    nt = T // ts

    def body(pp, pm, w_ref, y_ref):
        i = pl.program_id(1)
        u_prev = jnp.where(i > 0, _lane_block(pp, 0) * _lane_block(pp, 1), 0.0)
        ext = jnp.concatenate([u_prev, _lane_block(pm, 0) * _lane_block(pm, 1)], axis=0)
        t0, t1, t2 = _taps(ext, 3, SUBLANES, ts)
        cv = w_ref[2:3, :] * t0 + w_ref[1:2, :] * t1 + w_ref[0:1, :] * t2
        y_ref[...] = (_lane_block(pm, 2) * cv * _silu(_lane_block(pm, 3))).astype(y_ref.dtype)

    return pl.pallas_call(
        body, name="mixer_a_fwd", grid=(nblk, nt),
        in_specs=_halo_specs(ts, nt, 4 * MIXER_LANES, nxt=False)
        + [pl.BlockSpec((conv_w.shape[0], MIXER_LANES), lambda cb, i: (0, cb))],
        out_specs=pl.BlockSpec((ts, MIXER_LANES), lambda cb, i: (i, cb)),
        out_shape=jax.ShapeDtypeStruct((T, nblk * MIXER_LANES), BF16), compiler_params=_params(),
    )(proj_a, proj_a, conv_w)


def _mixer_a_bwd(proj_a, dy, conv_w):
    T = proj_a.shape[0]
    nblk = proj_a.shape[1] // (4 * MIXER_LANES)
    ts = min(ROW_TILE, T)
    nt = T // ts
    kt = conv_w.shape[0]

    def body(pp, pm, pn, dym, dyn, w_ref, dp_ref, dw_ref):
        i = pl.program_id(1)
        hm, cm, bm, zm = (_lane_block(pm, k) for k in range(4))
        u_prev = jnp.where(i > 0, _lane_block(pp, 0) * _lane_block(pp, 1), 0.0)
        ext = jnp.concatenate([u_prev, hm * cm], axis=0)
        dy_ext = jnp.concatenate([dym[...], jnp.where(i < nt - 1, dyn[...], 0.0)], axis=0)
        b_ext = jnp.concatenate([bm, _lane_block(pn, 2)], axis=0)
        sz_ext = _silu(jnp.concatenate([zm, _lane_block(pn, 3)], axis=0))
        dcv_ext = dy_ext * b_ext * sz_ext
        w = [w_ref[j:j + 1, :] for j in range(kt)]
        du = sum(w[kt - 1 - j] * _ahead(dcv_ext, j, ts) for j in range(kt))
        taps = _taps(ext, kt, SUBLANES, ts)
        cv = sum(w[kt - 1 - j] * taps[j] for j in range(kt))
        for part, d in enumerate((du * cm, du * hm, dym[...] * cv * sz_ext[:ts], dym[...] * bm * cv * _dsilu(zm))):
            dp_ref[:, part * MIXER_LANES:(part + 1) * MIXER_LANES] = d.astype(dp_ref.dtype)
        dcv = dcv_ext[:ts]

        @pl.when(i == 0)
        def _():
            dw_ref[...] = jnp.zeros_like(dw_ref)

        for j in range(kt):
            dw_ref[j:j + 1, :] += jnp.sum(dcv * taps[kt - 1 - j], axis=0, keepdims=True)

    return pl.pallas_call(
        body, name="mixer_a_bwd", grid=(nblk, nt),
        in_specs=_halo_specs(ts, nt, 4 * MIXER_LANES) + _halo_specs(ts, nt, MIXER_LANES, prev=False)
        + [pl.BlockSpec((kt, MIXER_LANES), lambda cb, i: (0, cb))],
        out_specs=[pl.BlockSpec((ts, 4 * MIXER_LANES), lambda cb, i: (i, cb)),
                   pl.BlockSpec((SUBLANES, MIXER_LANES), lambda cb, i: (0, cb))],
        out_shape=[jax.ShapeDtypeStruct(proj_a.shape, BF16),
                   jax.ShapeDtypeStruct((SUBLANES, nblk * MIXER_LANES), F32)],
        compiler_params=_params(),
    )(proj_a, proj_a, proj_a, dy, dy, conv_w)


def _conv_b_fwd(raw, conv_w):
    T = raw.shape[0]
    nblk = raw.shape[1] // CONV_LANES
    ts = min(ROW_TILE, T)
    nt = T // ts
    kt = conv_w.shape[0]

    def body(rp, rm, w_ref, y_ref):
        i = pl.program_id(1)
        ext = jnp.concatenate([jnp.where(i > 0, rp[...], 0.0), rm[...]], axis=0)
        taps = _taps(ext, kt, SUBLANES, ts)
        y_ref[...] = _silu(sum(w_ref[kt - 1 - j:kt - j, :] * taps[j] for j in range(kt)))

    return pl.pallas_call(
        body, name="conv_b_fwd", grid=(nblk, nt),
        in_specs=_halo_specs(ts, nt, CONV_LANES, nxt=False) + [pl.BlockSpec((kt, CONV_LANES), lambda cb, i: (0, cb))],
        out_specs=pl.BlockSpec((ts, CONV_LANES), lambda cb, i: (i, cb)),
        out_shape=jax.ShapeDtypeStruct(raw.shape, F32), compiler_params=_params(),
    )(raw, raw, conv_w)


def _conv_b_bwd(raw, dy, conv_w):
    T = raw.shape[0]
    nblk = raw.shape[1] // CONV_LANES
    ts = min(ROW_TILE, T)
    nt = T // ts
    kt = conv_w.shape[0]

    def body(rp, rm, rn, dym, dyn, w_ref, dr_ref, dw_ref):
        i = pl.program_id(1)
        ext = jnp.concatenate([jnp.where(i > 0, rp[...], 0.0), rm[...], rn[...]], axis=0)
        w = [w_ref[j:j + 1, :] for j in range(kt)]
        taps = _taps(ext, kt, SUBLANES, ts + SUBLANES)
        xc_ext = sum(w[kt - 1 - j] * taps[j] for j in range(kt))
        dy_ext = jnp.concatenate([dym[...], jnp.where(i < nt - 1, dyn[...], 0.0)], axis=0)
        dxc_ext = dy_ext * _dsilu(xc_ext)
        dr_ref[...] = sum(w[kt - 1 - j] * _ahead(dxc_ext, j, ts) for j in range(kt)).astype(dr_ref.dtype)
        dxc = dxc_ext[:ts]

        @pl.when(i == 0)
        def _():
            dw_ref[...] = jnp.zeros_like(dw_ref)

        for j in range(kt):
            dw_ref[j:j + 1, :] += jnp.sum(dxc * taps[kt - 1 - j][:ts], axis=0, keepdims=True)

    return pl.pallas_call(
        body, name="conv_b_bwd", grid=(nblk, nt),
        in_specs=_halo_specs(ts, nt, CONV_LANES) + _halo_specs(ts, nt, CONV_LANES, prev=False)
        + [pl.BlockSpec((kt, CONV_LANES), lambda cb, i: (0, cb))],
        out_specs=[pl.BlockSpec((ts, CONV_LANES), lambda cb, i: (i, cb)),
                   pl.BlockSpec((SUBLANES, CONV_LANES), lambda cb, i: (0, cb))],
        out_shape=[jax.ShapeDtypeStruct(raw.shape, BF16), jax.ShapeDtypeStruct((SUBLANES, nblk * CONV_LANES), F32)],
        compiler_params=_params(),
    )(raw, raw, raw, dy, dy, conv_w)


def _split_heads(ref, base, nheads, rows=slice(None)):
    return jnp.stack([ref[rows, base + h * HEAD: base + (h + 1) * HEAD] for h in range(nheads)], axis=0)


def _store_heads(ref, base, x, rows=slice(None), accumulate=False):
    for h in range(x.shape[0]):
        lanes = slice(base + h * HEAD, base + (h + 1) * HEAD)
        if accumulate:
            ref[rows, lanes] += x[h]
        else:
            ref[rows, lanes] = x[h].astype(ref.dtype)


def _gdn_fwd(qkv, proj_zs, alog, dtb, gain, H):
    T = qkv.shape[0]
    C, HW = GDN_CHUNK, H * HEAD
    nc = T // C
    zw = HW + LANES

    def body(qkv_ref, zs_ref, alog_ref, dtb_ref, gain_ref, o_ref, sall_ref, s_scr):
        @pl.when(pl.program_id(0) == 0)
        def _():
            s_scr[...] = jnp.zeros_like(s_scr)

        sall_ref[0] = s_scr[...]
        outs, states = _gdn_step(
            s_scr[...], _split_heads(qkv_ref, 0, H), _split_heads(qkv_ref, HW, H),
            _split_heads(qkv_ref, 2 * HW, H), _split_heads(zs_ref, 0, H), zs_ref[:, HW:HW + LANES],
            alog_ref[...], dtb_ref[...], gain_ref[...])
        _store_heads(o_ref, 0, outs)
        s_scr[...] = states

    row = pl.BlockSpec((1, LANES), lambda i: (0, 0))
    return pl.pallas_call(
        body, name="gdn_fwd", grid=(nc,),
        in_specs=[pl.BlockSpec((C, 3 * HW), lambda i: (i, 0)), pl.BlockSpec((C, zw), lambda i: (i, 0)), row, row, row],
        out_specs=[pl.BlockSpec((C, HW), lambda i: (i, 0)), pl.BlockSpec((1, H, HEAD, HEAD), lambda i: (i, 0, 0, 0))],
        out_shape=[jax.ShapeDtypeStruct((T, HW), BF16), jax.ShapeDtypeStruct((nc, H, HEAD, HEAD), F32)],
        scratch_shapes=[pltpu.VMEM((H, HEAD, HEAD), F32)], compiler_params=_params(),
    )(qkv, proj_zs, alog, dtb, gain)


def _gdn_bwd(qkv, proj_zs, do, s_all, alog, dtb, gain, H):
    T = qkv.shape[0]
    C, HW = GDN_CHUNK, H * HEAD
    nc = T // C
    zw = HW + LANES

    def body(qkv_ref, zs_ref, do_ref, sin_ref, alog_ref, dtb_ref, gain_ref,
             dqkv_ref, dzs_ref, dalog_ref, ddtb_ref, dgain_ref, ds_scr):
        @pl.when(pl.program_id(0) == 0)
        def _():
            ds_scr[...] = jnp.zeros_like(ds_scr)
            dalog_ref[...] = jnp.zeros_like(dalog_ref)
            ddtb_ref[...] = jnp.zeros_like(ddtb_ref)
            dgain_ref[...] = jnp.zeros_like(dgain_ref)

        primals = (sin_ref[0], _split_heads(qkv_ref, 0, H),
                   _split_heads(qkv_ref, HW, H), _split_heads(qkv_ref, 2 * HW, H), _split_heads(zs_ref, 0, H),
                   zs_ref[:, HW:HW + LANES], alog_ref[...], dtb_ref[...], gain_ref[...])
        _, vjp = jax.vjp(_gdn_step, *primals)
        dS, dq, dk, dv, dz, dsmall, dalog, ddtb, dgain = vjp((_split_heads(do_ref, 0, H), ds_scr[...]))
        ds_scr[...] = dS
        _store_heads(dqkv_ref, 0, dq)
        _store_heads(dqkv_ref, HW, dk)
        _store_heads(dqkv_ref, 2 * HW, dv)
        _store_heads(dzs_ref, 0, dz)
        dzs_ref[:, HW:HW + LANES] = dsmall.astype(dzs_ref.dtype)
        dalog_ref[...] += dalog
        ddtb_ref[...] += ddtb
        dgain_ref[...] += dgain

    row = pl.BlockSpec((1, LANES), lambda i: (0, 0))
    rev = lambda i: nc - 1 - i
    return pl.pallas_call(
        body, name="gdn_bwd", grid=(nc,),
        in_specs=[pl.BlockSpec((C, 3 * HW), lambda i: (rev(i), 0)), pl.BlockSpec((C, zw), lambda i: (rev(i), 0)),
                  pl.BlockSpec((C, HW), lambda i: (rev(i), 0)),
                  pl.BlockSpec((1, H, HEAD, HEAD), lambda i: (rev(i), 0, 0, 0)), row, row, row],
        out_specs=[pl.BlockSpec((C, 3 * HW), lambda i: (rev(i), 0)), pl.BlockSpec((C, zw), lambda i: (rev(i), 0)),
                   row, row, row],
        out_shape=[jax.ShapeDtypeStruct(qkv.shape, F32), jax.ShapeDtypeStruct(proj_zs.shape, BF16)]
        + [jax.ShapeDtypeStruct((1, LANES), F32)] * 3,
        scratch_shapes=[pltpu.VMEM((H, HEAD, HEAD), F32)], compiler_params=_params(),
    )(qkv, proj_zs, do, s_all, alog, dtb, gain)


def _hgrn_refs(proj_ref, lb_ref, HP):
    W = HP * HEAD
    return (_split_heads(proj_ref, 0, HP), _split_heads(proj_ref, W, HP), _split_heads(proj_ref, 2 * W, HP),
            _split_heads(proj_ref, 3 * W, HP), _split_heads(lb_ref, 0, HP, slice(0, 1)),
            _split_heads(lb_ref, 0, HP, slice(1, 2)))


def _hgrn_fwd(proj, lower_bounds, gain, nheads):
    T = proj.shape[0]
    C, HP = HGRN_CHUNK, HGRN_HEADS_PER_STEP
    ng, nc, W = nheads // HP, T // C, HP * HEAD

    def body(proj_ref, lb_ref, gain_ref, o_ref, sall_ref, s_scr):
        @pl.when(pl.program_id(1) == 0)
        def _():
            s_scr[...] = jnp.zeros_like(s_scr)

        sall_ref[0] = s_scr[...]
        qr, fr, vi, z, lb0, lb1 = _hgrn_refs(proj_ref, lb_ref, HP)
        outs, states = _hgrn_step(s_scr[...], qr, fr, vi, z, lb0, lb1, gain_ref[...])
        _store_heads(o_ref, 0, outs)
        s_scr[...] = states

    return pl.pallas_call(
        body, name="hgrn_fwd", grid=(ng, nc),
        in_specs=[pl.BlockSpec((C, 4 * W), lambda g, i: (i, g)), pl.BlockSpec((2, W), lambda g, i: (0, g)),
                  pl.BlockSpec((1, LANES), lambda g, i: (0, 0))],
        out_specs=[pl.BlockSpec((C, W), lambda g, i: (i, g)),
                   pl.BlockSpec((1, HP, HEAD, HEAD), lambda g, i: (i, g, 0, 0))],
        out_shape=[jax.ShapeDtypeStruct((T, nheads * HEAD), BF16), jax.ShapeDtypeStruct((nc, nheads, HEAD, HEAD), F32)],
        scratch_shapes=[pltpu.VMEM((HP, HEAD, HEAD), F32)], compiler_params=_params(),
    )(proj, lower_bounds, gain)


def _hgrn_bwd(proj, do, s_all, lower_bounds, gain, nheads):
    T = proj.shape[0]
    C, HP = HGRN_CHUNK, HGRN_HEADS_PER_STEP
    ng, nc, W = nheads // HP, T // C, HP * HEAD

    def body(proj_ref, do_ref, sin_ref, lb_ref, gain_ref, dproj_ref, dlb_ref, dgain_ref, ds_scr):
        first = pl.program_id(1) == 0

        @pl.when(first)
        def _():
            ds_scr[...] = jnp.zeros_like(ds_scr)
            dlb_ref[...] = jnp.zeros_like(dlb_ref)

        @pl.when(first & (pl.program_id(0) == 0))
        def _():
            dgain_ref[...] = jnp.zeros_like(dgain_ref)

        qr, fr, vi, z, lb0, lb1 = _hgrn_refs(proj_ref, lb_ref, HP)
        primals = (sin_ref[0], qr, fr, vi, z, lb0, lb1, gain_ref[...])
        _, vjp = jax.vjp(_hgrn_step, *primals)
        dS, dq, df, dv, dz, dlb0, dlb1, dgain = vjp((_split_heads(do_ref, 0, HP), ds_scr[...]))
        ds_scr[...] = dS
        for part, d in enumerate((dq, df, dv, dz)):
            _store_heads(dproj_ref, part * W, d)
        _store_heads(dlb_ref, 0, dlb0, slice(0, 1), accumulate=True)
        _store_heads(dlb_ref, 0, dlb1, slice(1, 2), accumulate=True)
        dgain_ref[...] += dgain

    rev = lambda i: nc - 1 - i
    return pl.pallas_call(
        body, name="hgrn_bwd", grid=(ng, nc),
        in_specs=[pl.BlockSpec((C, 4 * W), lambda g, i: (rev(i), g)), pl.BlockSpec((C, W), lambda g, i: (rev(i), g)),
                  pl.BlockSpec((1, HP, HEAD, HEAD), lambda g, i: (rev(i), g, 0, 0)),
                  pl.BlockSpec((2, W), lambda g, i: (0, g)), pl.BlockSpec((1, LANES), lambda g, i: (0, 0))],
        out_specs=[pl.BlockSpec((C, 4 * W), lambda g, i: (rev(i), g)), pl.BlockSpec((2, W), lambda g, i: (0, g)),
                   pl.BlockSpec((1, LANES), lambda g, i: (0, 0))],
        out_shape=[jax.ShapeDtypeStruct(proj.shape, BF16), jax.ShapeDtypeStruct(lower_bounds.shape, F32),
                   jax.ShapeDtypeStruct((1, LANES), F32)],
        scratch_shapes=[pltpu.VMEM((HP, HEAD, HEAD), F32)], compiler_params=_params(),
    )(proj, do, s_all, lower_bounds, gain)


def _post_specs(T):
    tr = min(POST_TILE, T)
    tile = lambda w: pl.BlockSpec((tr, w), lambda i: (i, 0))
    full = lambda r, w: pl.BlockSpec((r, w), lambda i: (0, 0))
    return tr, tile, full


def _post_fwd(s, x, p, g, b, wg, wpl, name):
    T, D = x.shape
    P = p.shape[1]
    tr, tile, full = _post_specs(T)

    def body(s_ref, x_ref, p_ref, g_ref, b_ref, wg_ref, wpl_ref, o_ref, o16_ref):
        x1 = _post_norm(s_ref[...], x_ref[...], g_ref[...], b_ref[...])
        xn = _post_gate(x1, _dot_raw(x1, wg_ref[...], "nn", False), _dot_raw(p_ref[...], wpl_ref[...], "nn", False))
        o_ref[...] = xn
        o16_ref[...] = xn.astype(BF16)

    return pl.pallas_call(
        body, name=name, grid=(T // tr,),
        in_specs=[tile(D), tile(D), tile(P), full(1, D), full(1, D), full(D, D), full(P, D)],
        out_specs=[tile(D), tile(D)],
        out_shape=[jax.ShapeDtypeStruct((T, D), F32), jax.ShapeDtypeStruct((T, D), BF16)], compiler_params=_params(),
    )(s, x, p, g, b, wg, wpl)


def _post_bwd(s, x, p, g, b, wg, wpl, dnext, name, with_loss):
    T, D = x.shape
    P = p.shape[1]
    tr, tile, full = _post_specs(T)

    def body(s_ref, x_ref, p_ref, g_ref, b_ref, wg_ref, wpl_ref, dn_ref,
             ds_ref, dx_ref, dg_ref, db_ref, dwg_ref, dwpl_ref, loss_ref):
        @pl.when(pl.program_id(0) == 0)
        def _():
            for r in (dg_ref, db_ref, dwg_ref, dwpl_ref, loss_ref):
                r[...] = jnp.zeros_like(r)

        x1, vjp_norm = jax.vjp(_post_norm, s_ref[...], x_ref[...], g_ref[...], b_ref[...])
        gate_pre = _dot_raw(x1, wg_ref[...], "nn", False)
        pp = _dot_raw(p_ref[...], wpl_ref[...], "nn", False)
        xn, vjp_gate = jax.vjp(_post_gate, x1, gate_pre, pp)
        if with_loss:
            err = xn - dn_ref[...]
            loss_ref[...] += 0.5 * jnp.sum(jnp.sum(err * err, axis=-1, keepdims=True), axis=0, keepdims=True) / D
            dn = err / D
        else:
            dn = dn_ref[...]
        dx1, dgp, dpp = vjp_gate(dn)
        dwg_ref[...] += _dot_raw(x1, dgp, "tn", False)
        dwpl_ref[...] += _dot_raw(p_ref[...], dpp, "tn", False)
        dx1 = dx1 + _dot_raw(dgp, wg_ref[...], "nt", False)
        ds, dx, dg, db = vjp_norm(dx1)
        ds_ref[...] = ds.astype(ds_ref.dtype)
        dx_ref[...] = dx
        dg_ref[...] += dg
        db_ref[...] += db

    return pl.pallas_call(
        body, name=name, grid=(T // tr,),
        in_specs=[tile(D), tile(D), tile(P), full(1, D), full(1, D), full(D, D), full(P, D), tile(D)],
        out_specs=[tile(D), tile(D), full(1, D), full(1, D), full(D, D), full(P, D), full(SUBLANES, LANES)],
        out_shape=[jax.ShapeDtypeStruct((T, D), BF16), jax.ShapeDtypeStruct((T, D), F32)]
        + [jax.ShapeDtypeStruct((1, D), F32)] * 2
        + [jax.ShapeDtypeStruct((D, D), F32), jax.ShapeDtypeStruct((P, D), F32),
           jax.ShapeDtypeStruct((SUBLANES, LANES), F32)],
        compiler_params=_params(),
    )(s, x, p, g, b, wg, wpl, dnext)


def _adam_math(w, g, m, v):
    m = ADAM_B1 * m + (1.0 - ADAM_B1) * g
    v = ADAM_B2 * v + (1.0 - ADAM_B2) * (g * g)
    m_hat = m / (1.0 - ADAM_B1 ** ADAM_STEP)
    v_hat = v / (1.0 - ADAM_B2 ** ADAM_STEP)
    return -ADAM_LR * (m_hat / (jnp.sqrt(v_hat) + ADAM_EPS) + ADAM_WD * w), m, v


def _shard_tiles(R, C):
    tr = _pick(R, (256, 128, 64, 32, 16, 8))
    return (tr, C) if tr < R or R % SUBLANES == 0 else (R, _pick(C, (256, 128)))


def _adam_sharded(w, m, v, g8, got, me, name):
    rows_apart = w.shape[1] == 1 and w.shape[0] > 1
    (L, R, C) = (1, w.shape[0], w.shape[2]) if rows_apart else w.shape
    tr, tc = (R, LANES) if rows_apart else _shard_tiles(R, C)
    nr, nc = R // tr, C // tc
    side_by_side = g8.ndim == 2

    def body(me_ref, w_ref, m_ref, v_ref, p_ref, *rest):
        got_refs, (g_ref, d_ref, mo_ref, vo_ref) = rest[:7], rest[7:]
        g = p_ref[...] if side_by_side else p_ref[0]
        for r in got_refs:
            g = g + r[0].astype(F32)
        if rows_apart:
            d, mn, vn = _adam_math(w_ref[:, 0, :], g, m_ref[:, 0, :], v_ref[:, 0, :])
            for ref, val in ((g_ref, g), (d_ref, d), (mo_ref, mn), (vo_ref, vn)):
                ref[:, 0, :] = val
            return
        d, mn, vn = _adam_math(w_ref[0], g, m_ref[0], v_ref[0])
        g_ref[0] = g
        d_ref[0] = d
        mo_ref[0] = mn
        vo_ref[0] = vn

    if rows_apart:
        t3 = pl.BlockSpec((tr, 1, tc), lambda l, i, j, q: (i, 0, j))
    else:
        t3 = pl.BlockSpec((1, tr, tc), lambda l, i, j, q: (l, i, j))
    slot = lambda k: pl.BlockSpec((1, tr, tc), lambda l, i, j, q: (k, l * nr + i, j))
    if side_by_side:
        mine = pl.BlockSpec((tr, tc), lambda l, i, j, q: (l * nr + i, q[0] * nc + j))
    else:
        mine = pl.BlockSpec((1, tr, tc), lambda l, i, j, q: (q[0], l * nr + i, j))
    return pl.pallas_call(
        body, name=name,
        grid_spec=pltpu.PrefetchScalarGridSpec(
            num_scalar_prefetch=1, grid=(L, nr, nc),
            in_specs=[t3, t3, t3, mine] + [slot(k) for k in range(7)], out_specs=[t3, t3, t3, t3]),
        out_shape=[jax.ShapeDtypeStruct(w.shape, F32)] * 4, compiler_params=_params(),
    )(me, w, m, v, g8, *([got] * 7))


def _small_rows(shapes):
    offsets, r = [], 0
    for rows, _ in shapes:
        offsets.append(r)
        r += rows
    return offsets, -(-(r + 1) // SUBLANES) * SUBLANES, max(cols for _, cols in shapes)


def _pack_small_grads(grads, loss):
    offsets, total, width = _small_rows([g.shape for g in grads])
    packed = jnp.zeros((total, width), F32)
    for g, r in zip(grads, offsets):
        packed = lax.dynamic_update_slice(packed, g, (r, 0))
    return lax.dynamic_update_slice(packed, loss.reshape(1, 1), (total - 1, 0))


def _adam_replicated(params, g8):
    shapes = [w.shape for w, _, _ in params]
    offsets, total, width = _small_rows(shapes)
    n = len(params)

    def body(*refs):
        g_ref, outs, loss_ref, g_scr = refs[3 * n], refs[3 * n + 1:7 * n + 1], refs[7 * n + 1], refs[7 * n + 2]
        g = g_ref[0]
        for k in range(1, 8):
            g = g + g_ref[k]
        g_scr[...] = g
        for i, (rows, cols) in enumerate(shapes):
            gp = g_scr[offsets[i]:offsets[i] + rows, 0:cols]
            d, mn, vn = _adam_math(refs[3 * i][...], gp, refs[3 * i + 1][...], refs[3 * i + 2][...])
            for ref, val in zip(outs[4 * i:4 * i + 4], (gp, d, mn, vn)):
                ref[...] = val
        loss_ref[...] = g_scr[total - 1:total, 0:LANES]

    out = pl.pallas_call(
        body, name="adam_replicated",
        out_shape=[jax.ShapeDtypeStruct(shp, F32) for shp in shapes for _ in range(4)]
        + [jax.ShapeDtypeStruct((1, LANES), F32)],
        scratch_shapes=[pltpu.VMEM((total, width), F32)], compiler_params=_params(),
    )(*[a for triple in params for a in triple], g8)
    return [out[4 * i:4 * i + 4] for i in range(n)], out[4 * n][0, 0]


def _place():
    return lax.axis_index("x"), lax.axis_index("y"), lax.axis_index("c")


def _all_gather(shard, name):
    def body(x_ref, out_ref, send_sems, recv_sems, local_sem):
        x, y, c = _place()
        me, sibling = (x, y, c), (x, y, 1 - c)
        chips = [(1 - x, y), (x, 1 - y), (1 - x, 1 - y)]

        def slab(px, py, pc):
            return out_ref.at[4 * px + 2 * py + pc]

        def copy(k, block, to, src=None):
            return pltpu.make_async_remote_copy(
                src_ref=slab(*block) if src is None else src, dst_ref=slab(*block),
                send_sem=send_sems.at[k], recv_sem=recv_sems.at[k], device_id=to, device_id_type=MESH)

        mine = pltpu.make_async_copy(x_ref, slab(*me), local_sem)
        mine.start()
        first = [copy(0, me, sibling, src=x_ref)]
        first += [copy(1 + j, me, (*chip, c), src=x_ref) for j, chip in enumerate(chips)]
        for cp in first:
            cp.start()
        passed = [copy(4 + j, (*chip, c), sibling) for j, chip in enumerate(chips)]
        for j, chip in enumerate(chips):
            copy(1 + j, (*chip, c), me).wait_recv()
            passed[j].start()
        copy(0, sibling, me).wait_recv()
        for j, chip in enumerate(chips):
            copy(4 + j, (*chip, 1 - c), me).wait_recv()
        for cp in first + passed:
            cp.wait_send()
        mine.wait()

    return pl.pallas_call(
        body, name=name, out_shape=jax.ShapeDtypeStruct((8,) + shard.shape, shard.dtype),
        in_specs=[pl.BlockSpec(memory_space=pl.ANY)], out_specs=pl.BlockSpec(memory_space=pl.ANY),
        scratch_shapes=[pltpu.SemaphoreType.DMA((7,)), pltpu.SemaphoreType.DMA((7,)), pltpu.SemaphoreType.DMA],
    )(shard)


_HBM = pl.BlockSpec(memory_space=pltpu.HBM)
_SEM = pl.BlockSpec(memory_space=pltpu.SEMAPHORE)
_DATAFLOW = pltpu.SideEffectType.DATAFLOW_SIDE_EFFECTING
TOKEN_SHAPE = (SUBLANES, LANES)


def _peers(x, y, c):
    flip = lambda v, bit: 1 - v if bit else v
    return [(flip(x, r >> 2 & 1), flip(y, r >> 1 & 1), flip(c, r & 1)) for r in range(1, 8)]


def _scatter_plan(x, y, c):
    return [(4 * px + 2 * py + pc, k, (px, py, pc)) for k, (px, py, pc) in enumerate(_peers(x, y, c))]


def _exchange_copies(plan, src_ref, land_ref, send_sems, recv_sems):
    C = land_ref.shape[-1]
    block = (lambda b: src_ref.at[b]) if len(src_ref.shape) == len(land_ref.shape) else (
        lambda b: src_ref.at[:, pl.ds(b * C, C)])
    return [pltpu.make_async_remote_copy(
        src_ref=block(blk), dst_ref=land_ref.at[slot], send_sem=send_sems.at[k], recv_sem=recv_sems.at[k],
        device_id=peer, device_id_type=MESH) for k, (blk, slot, peer) in enumerate(plan(*_place()))]


def _exchange_start(src, n_slots, block_shape, plan, name):
    land_shape = (n_slots,) + tuple(block_shape)
    n = len(plan(0, 0, 0))

    def body(src_ref, land_ref, send_sems, recv_sems, src_thru, land_thru, token):
        for cp in _exchange_copies(plan, src_ref, land_ref, send_sems, recv_sems):
            cp.start()
        token[...] = jnp.zeros_like(token)

    return pl.pallas_call(
        body, name=name,
        out_shape=(pltpu.SemaphoreType.DMA((n,)), pltpu.SemaphoreType.DMA((n,)), pltpu.HBM(src.shape, src.dtype),
                   pltpu.HBM(land_shape, src.dtype), jax.ShapeDtypeStruct(TOKEN_SHAPE, F32)),
        in_specs=(_HBM, _HBM), out_specs=(_SEM, _SEM, _HBM, _HBM, pl.BlockSpec(memory_space=pltpu.VMEM)),
        input_output_aliases={0: 2, 1: 3}, compiler_params=pltpu.CompilerParams(has_side_effects=_DATAFLOW),
    )(pltpu.with_memory_space_constraint(src, pltpu.HBM),
      pltpu.with_memory_space_constraint(lax.empty(land_shape, src.dtype), pltpu.HBM))


def _exchange_wait(handle, plan, after, name):
    send_sems, recv_sems, src_thru, land_thru, _ = handle

    def body(src_ref, land_ref, send_sems, recv_sems, after_ref, src_dead, got_ref):
        for cp in _exchange_copies(plan, src_ref, land_ref, send_sems, recv_sems):
            cp.wait_send()
            cp.wait_recv()

    return pl.pallas_call(
        body, name=name,
        out_shape=(pltpu.HBM(src_thru.shape, src_thru.dtype), pltpu.HBM(land_thru.shape, land_thru.dtype)),
        in_specs=(_HBM, _HBM, _SEM, _SEM, pl.BlockSpec(memory_space=pl.ANY)), out_specs=(_HBM, _HBM),
        input_output_aliases={0: 0, 1: 1}, compiler_params=pltpu.CompilerParams(has_side_effects=_DATAFLOW),
    )(src_thru, land_thru, send_sems, recv_sems, after)[1]


def _gather_plan(x, y, c):
    return [(0, 4 * x + 2 * y + c, peer) for peer in _peers(x, y, c)]


class _LateGather:
    def __init__(self, shard, name):
        self.shard, self.name = shard, name
        self.handle = _exchange_start(shard[None], 8, shard.shape, _gather_plan, name + "_start")

    def get(self, after):
        land = _exchange_wait(self.handle, _gather_plan, after, self.name + "_wait")
        x, y, c = _place()
        return lax.dynamic_update_slice(land, self.shard[None], (4 * x + 2 * y + c, 0, 0))


class _GradExchange:
    def __init__(self, me, layouts):
        self.me, self.layouts, self.pending = me, layouts, {}

    def start(self, tag, grad, grad16=None):
        g8 = self.layouts[tag](grad)
        g16 = g8.astype(BF16) if grad16 is None else self.layouts[tag](grad16)
        block_shape = g8.shape[1:] if g8.ndim == 3 else (g8.shape[0], g8.shape[1] // 8)
        handle = _exchange_start(g16, 7, block_shape, _scatter_plan, "rs_start_" + tag)
        self.pending[tag] = (g8, handle)
        return handle[4]

    def finish(self, tag, w, m, v, after):
        g8, handle = self.pending.pop(tag)
        got = _exchange_wait(handle, _scatter_plan, after, "rs_wait_" + tag)
        return _adam_sharded(w, m, v, g8, got, self.me, "adam_" + tag)


def _local_grads(x, p0, p1, target, wt_zs, wt_a, wt_qkv, late, conv_a, conv_b,
                 a_log, dt_bias, gdn_gain, lower_bounds, hgrn_gain, ln_g, ln_b, on_grad=None):
    H = a_log.shape[1]
    pad_small = ((0, 0), (H, LANES - 2 * H))
    alog_row = jnp.pad(a_log, pad_small)
    dtb_row = jnp.pad(dt_bias, pad_small)

    x16 = x.astype(BF16)
    proj_zs = _matmul(x16, wt_zs, "nt", "proj_even_zs", after=late.started)
    proj_a = _matmul(x16, wt_a, "nt", "proj_even_a", after=late.started)
    proj_qkv = _matmul(x16, wt_qkv, "nt", "proj_even_qkv", after=late.started)
    y_a = _mixer_a_fwd(proj_a, conv_a)
    qkv = _conv_b_fwd(proj_qkv, conv_b)
    o2, s_gdn = _gdn_fwd(qkv, proj_zs, alog_row, dtb_row, gdn_gain, H)
    woute_a, woute_b = late.out_even(o2)
    wg, wpl = late.gate(o2)
    s_e = _matmul(o2, woute_b, "nn", "out_even_b", add=_matmul(y_a, woute_a, "nn", "out_even_a"))
    x2, x2_16 = _post_fwd(s_e, x, p0, ln_g[0:1], ln_b[0:1], wg[0], wpl[0], "post_even_fwd")
    wino, wouto = late.odd(s_e)
    nheads_o = wouto.shape[0] // HEAD
    proj_o = _matmul(x2_16, wino, "nn", "proj_odd")
    o4, s_hgrn = _hgrn_fwd(proj_o, lower_bounds, hgrn_gain, nheads_o)
    s_o = _matmul(o4, wouto, "nn", "out_odd")
    ds_o, dx2, dlng1, dlnb1, dwg1, dwpl1, loss = _post_bwd(
        s_o, x2, p1, ln_g[1:2], ln_b[1:2], wg[1], wpl[1], target, "post_odd_loss_bwd", True)
    do4 = _matmul(ds_o, wouto, "nt", "d_out_odd_act")
    grads = {}

    def emit(tag, grad, grad16=None):
        grads[tag] = grad
        return on_grad(tag, grad, grad16) if on_grad is not None else jnp.zeros(TOKEN_SHAPE, F32)

    tok = emit("w_out_odd", *_matmul(o4, ds_o, "tn", "d_out_odd_w", also_bf16=True))
    dproj_o, dlb, dhgain = _hgrn_bwd(proj_o, do4, s_hgrn, lower_bounds, hgrn_gain + tok[0:1], nheads_o)
    dx2 = _matmul(dproj_o, wino, "nt", "d_proj_odd_act", add=dx2)
    tok = emit("w_in_odd", *_matmul(x2_16, dproj_o, "tn", "d_proj_odd_w", also_bf16=True))
    ds_e, dx, dlng0, dlnb0, dwg0, dwpl0, _ = _post_bwd(
        s_e, x, p0, ln_g[0:1], ln_b[0:1] + tok[0:1, 0:1], wg[0], wpl[0], dx2, "post_even_bwd", False)
    tok = emit("w_pl_gate", jnp.stack([dwg0, dwg1])) + emit("w_pl", jnp.stack([dwpl0, dwpl1]))
    dy_a = _matmul(ds_e, woute_a, "nt", "d_out_even_a_act")
    do2 = _matmul(ds_e, woute_b, "nt", "d_out_even_b_act")
    dwoute_a = _matmul(y_a, ds_e, "tn", "d_out_even_a_w")
    dwoute_b = _matmul(o2, ds_e, "tn", "d_out_even_b_w")
    tok = tok + emit("w_out_even", jnp.concatenate([dwoute_a, dwoute_b], axis=0))
    dqkv, dproj_zs, dalog, ddtb, dggain = _gdn_bwd(qkv, proj_zs, do2, s_gdn, alog_row, dtb_row, gdn_gain + tok[0:1], H)
    dproj_qkv, dconv_b = _conv_b_bwd(proj_qkv, dqkv, conv_b)
    dproj_a, dconv_a = _mixer_a_bwd(proj_a, dy_a, conv_a)
    emit("conv", (dconv_a[:conv_a.shape[0]], dconv_b[:conv_b.shape[0]]))
    tok = emit("w_in_even", (_matmul(dproj_zs, x16, "tn", "d_proj_even_zs_w"), _matmul(dproj_a, x16, "tn", "d_proj_even_a_w"),
                             _matmul(dproj_qkv, x16, "tn", "d_proj_even_qkv_w")))
    dx = _matmul(dproj_zs, wt_zs, "nn", "d_proj_even_zs_act", add=dx, after=tok)
    dx = _matmul(dproj_a, wt_a, "nn", "d_proj_even_a_act", add=dx)
    dx = _matmul(dproj_qkv, wt_qkv, "nn", "d_proj_even_qkv_act", add=dx)
    grads.update(
        loss=loss[0, 0], grad_x=dx, a_log=dalog[:, H:2 * H], dt_bias=ddtb[:, H:2 * H], gdn_gain=dggain,
        lower_bounds=dlb, hgrn_gain=dhgain, ln_g=jnp.concatenate([dlng0, dlng1], axis=0),
        ln_b=jnp.concatenate([dlnb0, dlnb1], axis=0))
    return grads


def _pad_rows(a, rows):
    return jnp.pad(a, ((0, rows - a.shape[0]), (0, 0)))


def _split_in_even(wt_full, AW, HW, H):
    D = wt_full.shape[1]
    n_a = 4 * AW
    n_main = n_a + 3 * HW
    wt_zs = jnp.concatenate([wt_full[n_main:n_main + HW], wt_full[n_main + HW:],
                             jnp.zeros((LANES - 2 * H, D), wt_full.dtype)], axis=0)
    wt_a = wt_full[:n_a].reshape(4, AW // MIXER_LANES, MIXER_LANES, D).transpose(1, 0, 2, 3).reshape(n_a, D)
    return wt_zs, wt_a, wt_full[n_a:n_main]


def _join_in_even(dt_zs, dt_a, dt_qkv, AW, HW, H):
    D = dt_a.shape[1]
    a_nat = dt_a.reshape(AW // MIXER_LANES, 4, MIXER_LANES, D).transpose(1, 0, 2, 3).reshape(4 * AW, D)
    return jnp.concatenate([a_nat, dt_qkv, dt_zs[:HW], dt_zs[HW:HW + 2 * H]], axis=0)


def kernel(x, p, w_in_even, conv_a_w, conv_b_w, a_log, dt_bias, gdn_norm_g, w_out_even, w_in_odd, lower_bounds, hgrn_norm_g, w_out_odd, ln_g, ln_b, w_pl, w_pl_gate, loss_target, m_w_in_even, m_conv_a_w, m_conv_b_w, m_a_log, m_dt_bias, m_gdn_norm_g, m_w_out_even, m_w_in_odd, m_lower_bounds, m_hgrn_norm_g, m_w_out_odd, m_ln_g, m_ln_b, m_w_pl, m_w_pl_gate, v_w_in_even, v_conv_a_w, v_conv_b_w, v_a_log, v_dt_bias, v_gdn_norm_g, v_w_out_even, v_w_in_odd, v_lower_bounds, v_hgrn_norm_g, v_w_out_odd, v_ln_g, v_ln_b, v_w_pl, v_w_pl_gate):
    xi, yi, ci = _place()
    me = jnp.reshape(4 * xi + 2 * yi + ci, (1,)).astype(jnp.int32)
    D = x.shape[2]
    H = a_log.shape[1]
    HW = H * HEAD
    AW = conv_a_w.shape[2] * 8
    OW = w_out_odd.shape[1] * 8
    PD = w_pl.shape[1]
    ka, kb = conv_a_w.shape[1], conv_b_w.shape[1]
    ca, cb = conv_a_w.shape[2], conv_b_w.shape[2]
    gw = HGRN_HEADS_PER_STEP * HEAD
    ngrp = OW // gw

    transposed = lambda a: jnp.transpose(a, (0, 2, 1))
    g_ine = _all_gather(transposed(w_in_even)[0].astype(BF16), "ag_w_in_even")
    wt_zs, wt_a, wt_qkv = _split_in_even(g_ine.reshape(-1, D), AW, HW, H)
    taps = jnp.concatenate([_pad_rows(conv_a_w[0], SUBLANES), _pad_rows(conv_b_w[0], SUBLANES)], axis=1)
    g_taps = _all_gather(taps, "ag_conv")
    conv_a = jnp.transpose(g_taps[:, :ka, :ca], (1, 0, 2)).reshape(ka, 8 * ca)
    conv_b = jnp.transpose(g_taps[:, :kb, ca:], (1, 0, 2)).reshape(kb, 8 * cb)
    behind = lambda shard, dep: lax.optimization_barrier((shard, dep))[0]
    late_oute = _LateGather(behind(w_out_even[0].astype(BF16), (g_ine, g_taps)), "ag_w_out_even")
    late_gate = _LateGather(behind(w_pl_gate.astype(BF16).reshape(-1, D), late_oute.handle[4]), "ag_w_pl_gate")
    late_pl = _LateGather(behind(w_pl.astype(BF16).reshape(DEPTH * PD, -1), late_gate.handle[4]), "ag_w_pl")
    late_ino = _LateGather(behind(w_in_odd[0].astype(BF16), late_pl.handle[4]), "ag_w_in_odd")
    late_outo = _LateGather(behind(w_out_odd[0].astype(BF16), late_ino.handle[4]), "ag_w_out_odd")

    class _Late:
        started = sum(g.handle[4] for g in (late_oute, late_gate, late_pl, late_ino, late_outo))

        @staticmethod
        def out_even(after):
            woute = late_oute.get(after).reshape(-1, D)
            return woute[:AW], woute[AW:]

        @staticmethod
        def gate(after):
            g_gate, g_pl = late_gate.get(after), late_pl.get(after)
            return (g_gate.reshape(8, DEPTH, D // 8, D).transpose(1, 0, 2, 3).reshape(DEPTH, D, D),
                    g_pl.reshape(8, DEPTH, PD, D // 8).transpose(1, 2, 0, 3).reshape(DEPTH, PD, D))

        @staticmethod
        def odd(after):
            g_ino = late_ino.get(after)
            wino = jnp.transpose(g_ino, (1, 0, 2)).reshape(D, 4, ngrp, gw).transpose(0, 2, 1, 3).reshape(D, 4 * OW)
            return wino, late_outo.get(after).reshape(-1, D)

    sh = w_in_even.shape[2]
    tap_blocks = lambda g, width: _pad_rows(g, SUBLANES).reshape(SUBLANES, 8, width).transpose(1, 0, 2)
    owner_layout = {
        "w_in_even": lambda g: _join_in_even(*g, AW, HW, H).reshape(8, sh, D),
        "w_in_odd": lambda g: g.reshape(D, ngrp, 4, gw).transpose(0, 2, 1, 3).reshape(D, 4 * OW),
        "w_out_even": lambda g: g.reshape(8, -1, D),
        "w_out_odd": lambda g: g.reshape(8, -1, D),
        "w_pl_gate": lambda g: g.reshape(DEPTH, 8, D // 8, D).transpose(1, 0, 2, 3).reshape(8, DEPTH * D // 8, D),
        "w_pl": lambda g: g.reshape(DEPTH, PD, 8, D // 8).transpose(2, 0, 1, 3).reshape(8, DEPTH * PD, D // 8),
        "conv": lambda g: jnp.concatenate([tap_blocks(g[0], ca), tap_blocks(g[1], cb)], axis=2),
    }
    exchange = _GradExchange(me, owner_layout)
    gr = _local_grads(x[0], p[0, 0], p[1, 0], loss_target[0], wt_zs, wt_a, wt_qkv, _Late, conv_a, conv_b,
                      a_log, dt_bias, gdn_norm_g, lower_bounds, hgrn_norm_g, ln_g, ln_b, on_grad=exchange.start)

    last = gr["grad_x"]
    pack_taps = lambda a, b: jnp.concatenate([_pad_rows(a[0], SUBLANES), _pad_rows(b[0], SUBLANES)], axis=1)[None]
    o_outo = exchange.finish("w_out_odd", w_out_odd, m_w_out_odd, v_w_out_odd, last)
    o_ino = exchange.finish("w_in_odd", w_in_odd, m_w_in_odd, v_w_in_odd, last)
    o_gate = exchange.finish("w_pl_gate", w_pl_gate, m_w_pl_gate, v_w_pl_gate, last)
    o_pl = exchange.finish("w_pl", w_pl, m_w_pl, v_w_pl, last)
    o_oute = exchange.finish("w_out_even", w_out_even, m_w_out_even, v_w_out_even, last)
    o_taps = exchange.finish("conv", taps[None], pack_taps(m_conv_a_w, m_conv_b_w), pack_taps(v_conv_a_w, v_conv_b_w), last)
    others_done = sum(o[1][0, 0:1, 0:1] for o in (o_outo, o_ino, o_gate, o_pl, o_oute, o_taps))
    rows_first = lambda a: jnp.transpose(a, (2, 0, 1))
    o_ine = [jnp.transpose(o, (1, 2, 0)) for o in exchange.finish(
        "w_in_even", rows_first(w_in_even), rows_first(m_w_in_even), rows_first(v_w_in_even), others_done)]

    small_g = _pack_small_grads([gr["a_log"], gr["dt_bias"], gr["gdn_gain"], gr["lower_bounds"], gr["hgrn_gain"],
                                 gr["ln_g"], gr["ln_b"]], gr["loss"])
    o_small, loss = _adam_replicated(
        [(a_log, m_a_log, v_a_log), (dt_bias, m_dt_bias, v_dt_bias), (gdn_norm_g, m_gdn_norm_g, v_gdn_norm_g),
         (lower_bounds, m_lower_bounds, v_lower_bounds), (hgrn_norm_g, m_hgrn_norm_g, v_hgrn_norm_g),
         (ln_g, m_ln_g, v_ln_g), (ln_b, m_ln_b, v_ln_b)],
        _all_gather(behind(small_g, o_ine[0]), "ag_small_grads"))

    def leaves(kind):
        s_alog, s_dt, s_gg, s_lb, s_hg, s_lng, s_lnb = (o[kind] for o in o_small)
        t = o_taps[kind]
        return [o_ine[kind], t[:, :ka, :ca], t[:, :kb, ca:], s_alog, s_dt, s_gg, o_oute[kind],
                o_ino[kind], s_lb, s_hg, o_outo[kind], s_lng, s_lnb, o_pl[kind], o_gate[kind]]

    return (loss, gr["grad_x"][None], *leaves(0), *leaves(1), *leaves(2), *leaves(3))
```

```python
import functools

import jax
import jax.numpy as jnp
from jax import lax
from jax.experimental import pallas as pl
from jax.experimental.pallas import tpu as pltpu

F32 = jnp.float32
BF16 = jnp.bfloat16
MESH = pl.DeviceIdType.MESH

LANES = 128
SUBLANES = 8
HEAD = 128
GDN_CHUNK = 128
HGRN_CHUNK = 64
HGRN_SUB = 16
HGRN_HEADS_PER_STEP = 16
NORM_EPS = 1e-5
DEPTH = 2
ALPHA = (2.0 * DEPTH) ** 0.25
EXP_CLAMP = 80.0
ADAM_LR, ADAM_B1, ADAM_B2, ADAM_EPS, ADAM_WD, ADAM_STEP = 0.001, 0.9, 0.999, 1e-08, 0.01, 10
VMEM_LIMIT = 56 * 1024 * 1024
MATMUL_VMEM = 36 * 1024 * 1024
ROW_TILE = 1024
MIXER_LANES = 256
CONV_LANES = 512
POST_TILE = 512

_NOBATCH, _BATCH0 = ((), ()), ((0,), (0,))
_DIMS = {"nn": (((1,), (0,)), _NOBATCH), "nt": (((1,), (1,)), _NOBATCH), "tn": (((0,), (0,)), _NOBATCH),
         "bnn": (((2,), (1,)), _BATCH0), "bnt": (((2,), (2,)), _BATCH0), "btn": (((1,), (1,)), _BATCH0)}


def _params(**kw):
    return pltpu.CompilerParams(vmem_limit_bytes=VMEM_LIMIT, **kw)


def _dot_raw(a, b, kind, hi):
    if hi:
        return lax.dot_general(a, b, _DIMS[kind], precision=lax.Precision.HIGHEST, preferred_element_type=F32)
    return lax.dot_general(a.astype(BF16), b.astype(BF16), _DIMS[kind], preferred_element_type=F32)


@functools.partial(jax.custom_vjp, nondiff_argnums=(2, 3))
def mdot(a, b, kind, hi):
    return _dot_raw(a, b, kind, hi)


def _mdot_fwd(a, b, kind, hi):
    return _dot_raw(a, b, kind, hi), (a, b)


def _mdot_bwd(kind, hi, res, g):
    a, b = res
    pre, base = kind[:-2], kind[-2:]
    if base == "nn":
        return _dot_raw(g, b, pre + "nt", hi), _dot_raw(a, g, pre + "tn", hi)
    if base == "nt":
        return _dot_raw(g, b, pre + "nn", hi), _dot_raw(g, a, pre + "tn", hi)
    return _dot_raw(b, g, pre + "nt", hi), _dot_raw(a, g, pre + "nn", hi)


mdot.defvjp(_mdot_fwd, _mdot_bwd)


def _rows(x, lo, hi):
    return _take_rows(x, lo, hi, x.shape[-2])


@functools.partial(jax.custom_vjp, nondiff_argnums=(1, 2, 3))
def _take_rows(x, lo, hi, n):
    return x[..., lo:hi, :]


def _take_rows_fwd(x, lo, hi, n):
    return x[..., lo:hi, :], None


def _take_rows_bwd(lo, hi, n, _, g):
    parts = []
    if lo > 0:
        parts.append(jnp.zeros(g.shape[:-2] + (lo, g.shape[-1]), g.dtype))
    parts.append(g)
    if n - hi > 0:
        parts.append(jnp.zeros(g.shape[:-2] + (n - hi, g.shape[-1]), g.dtype))
    return (jnp.concatenate(parts, axis=-2) if len(parts) > 1 else g,)


_take_rows.defvjp(_take_rows_fwd, _take_rows_bwd)


def _heads_of(wide, nheads):
    return jnp.stack([wide[:, h * HEAD:(h + 1) * HEAD] for h in range(nheads)], axis=0)


def _wide_of(x):
    return jnp.concatenate([x[h] for h in range(x.shape[0])], axis=1)


@functools.partial(jax.custom_vjp, nondiff_argnums=(1,))
def to_heads(wide, nheads):
    return _heads_of(wide, nheads)


to_heads.defvjp(lambda wide, nheads: (_heads_of(wide, nheads), None), lambda nheads, _, g: (_wide_of(g),))


@jax.custom_vjp
def to_wide(x):
    return _wide_of(x)


to_wide.defvjp(lambda x: (_wide_of(x), None), lambda _, g: (_heads_of(g, g.shape[1] // HEAD),))


def _sigmoid(x):
    return jax.nn.sigmoid(x)


def _silu(x):
    return x * _sigmoid(x)


def _dsilu(x):
    s = _sigmoid(x)
    return s * (1.0 + x * (1.0 - s))


def _log1p(u):
    return jnp.where(u < 1e-4, u * (1.0 - 0.5 * u), jnp.log(1.0 + u))


def _softplus(x):
    return jnp.maximum(x, 0.0) + _log1p(jnp.exp(-jnp.abs(x)))


def _rms_gate(o, gain, z):
    return o * lax.rsqrt(jnp.mean(o * o, axis=-1, keepdims=True) + NORM_EPS) * gain * _silu(z)


def _l2n(x):
    return x * lax.rsqrt(jnp.sum(x * x, axis=-1, keepdims=True) + 1e-6)


def _split_dot_raw(m, x, kind):
    mb = m.astype(BF16)
    hi = x.astype(BF16)
    lo = (x - hi.astype(F32)).astype(BF16)
    dims = _DIMS[kind]
    return (lax.dot_general(mb, hi, dims, preferred_element_type=F32)
            + lax.dot_general(mb, lo, dims, preferred_element_type=F32))


@jax.custom_vjp
def mask_dot(m, x):
    return _split_dot_raw(m, x, "nn")


def _mask_dot_fwd(m, x):
    return _split_dot_raw(m, x, "nn"), m


def _mask_dot_bwd(m, g):
    return jnp.zeros_like(m), _split_dot_raw(m, g, "tn")


mask_dot.defvjp(_mask_dot_fwd, _mask_dot_bwd)


def _neumann_rest(low):
    n = low.shape[-1]
    rest = -low
    power = low
    span = 2
    while span < n:
        power = _dot_raw(power, power, "bnn", False)
        rest = rest + power + _dot_raw(rest, power, "bnn", False)
        span *= 2
    return rest


@jax.custom_vjp
def _unit_lower_inverse_minus_eye(low):
    return _neumann_rest(low)


def _inverse_fwd(low):
    rest = _neumann_rest(low)
    return rest, rest


def _inverse_bwd(rest, g):
    left = g + _dot_raw(rest, g, "btn", False)
    return (-(left + _dot_raw(left, rest, "bnt", False)),)


_unit_lower_inverse_minus_eye.defvjp(_inverse_fwd, _inverse_bwd)


def _gdn_step(S, q, k, v, z, small, alog, dtb, gain):
    H = S.shape[0]
    C = GDN_CHUNK
    row = lax.broadcasted_iota(jnp.int32, (C, C), 0)
    col = lax.broadcasted_iota(jnp.int32, (C, C), 1)
    tril, strict, eye = (row >= col)[None], (row > col)[None], (row == col)[None]
    head = lax.broadcasted_iota(jnp.int32, (H, 1, LANES), 0)
    lane = lax.broadcasted_iota(jnp.int32, (H, 1, LANES), 2)
    rowc = lax.broadcasted_iota(jnp.int32, (1, C, 1), 1)
    beta_all = _sigmoid(small)
    g_all = -jnp.exp(alog) * _softplus(small + dtb)
    gc_all = mask_dot((row >= col).astype(F32), g_all)
    beta = jnp.sum(jnp.where(lane == head, beta_all[None], 0.0), axis=-1, keepdims=True)
    gc = jnp.sum(jnp.where(lane == head + H, gc_all[None], 0.0), axis=-1, keepdims=True)
    gc_row = jnp.sum(jnp.where(eye, gc, 0.0), axis=1, keepdims=True)
    decay = jnp.where(tril, jnp.exp(jnp.where(tril, gc - gc_row, 0.0)), 0.0)
    g_last = jnp.sum(jnp.where(rowc == C - 1, gc, 0.0), axis=1, keepdims=True)
    qn = _l2n(q) * (HEAD ** -0.5)
    kn = _l2n(k)
    kb = kn * beta
    low = jnp.where(strict, beta * mdot(kn, kn, "bnt", False) * decay, 0.0)
    inv_rest = _unit_lower_inverse_minus_eye(low)
    eg = jnp.exp(gc)
    vb, kbe = v * beta, kb * eg
    u = vb + mdot(inv_rest, vb, "bnn", False)
    w = kbe + mdot(inv_rest, kbe, "bnn", False)
    attn = mdot(qn, kn, "bnt", False) * decay
    v_new = u - mdot(w, S, "bnn", False)
    o = mdot(qn * eg, S, "bnn", False) + mdot(attn, v_new, "bnn", False)
    k_dec = kn * jnp.exp(g_last - gc)
    return _rms_gate(o, gain, z), S * jnp.exp(g_last) + mdot(k_dec, v_new, "btn", False)


def _hgrn_step(St, qr, fr, vi, z, lb0, lb1, gain):
    H = St.shape[0]
    C, SB = HGRN_CHUNK, HGRN_SUB
    row = lax.broadcasted_iota(jnp.int32, (C, C), 0)
    col = lax.broadcasted_iota(jnp.int32, (C, C), 1)
    blk_start = row - (row & (SB - 1))
    in_blk_f = ((row >= col) & (col >= blk_start)).astype(F32)
    before_f = (col < blk_start).astype(F32)
    sums_f = jnp.concatenate([in_blk_f, before_f], axis=0)
    m = jnp.maximum(lb0, lb1)
    e0, e1 = jnp.exp(lb0 - m), jnp.exp(lb1 - m)
    lb = e1 / (e0 + e1)
    f = lb + (1.0 - lb) * _sigmoid(fr)
    q = _silu(qr)
    k = 1.0 - f
    logf = jnp.log(f)
    sums = mask_dot(sums_f, to_wide(logf))
    inner, start = to_heads(_rows(sums, 0, C), H), to_heads(_rows(sums, C, 2 * C), H)
    b = start + inner
    b_last = jnp.sum(logf, axis=1, keepdims=True)
    o = mdot(q * jnp.exp(b), St, "bnt", False)
    qt = q * jnp.exp(inner)
    parts = []
    for blk in range(C // SB):
        lo, n = blk * SB, (blk + 1) * SB
        ref = jnp.concatenate([_rows(start, lo, n)] * (blk + 1), axis=1)
        kt = _rows(k, 0, n) * jnp.exp(jnp.minimum(ref - _rows(b, 0, n), EXP_CLAMP))
        att = mdot(_rows(qt, lo, n), kt, "bnt", False)
        t_idx = lax.broadcasted_iota(jnp.int32, (1, SB, n), 1) + lo
        s_idx = lax.broadcasted_iota(jnp.int32, (1, SB, n), 2)
        att = jnp.where(s_idx <= t_idx, att, 0.0)
        parts.append(mdot(att, _rows(vi, 0, n), "bnn", False))
    o = o + jnp.concatenate(parts, axis=1)
    k_dec = k * jnp.exp(b_last - b)
    return _rms_gate(o, gain, z), St * jnp.exp(b_last) + mdot(vi, k_dec, "btn", False)


def _post_norm(s, x, g, b):
    r = ALPHA * x + s
    d = r - jnp.mean(r, axis=-1, keepdims=True)
    var = jnp.mean(d * d, axis=-1, keepdims=True)
    return d * lax.rsqrt(var + NORM_EPS) * g + b


def _post_gate(x1, gate_pre, pp):
    return x1 + pp * _sigmoid(gate_pre)


def _pick(dim, cands):
    for c in cands:
        if dim % c == 0:
            return c
    return dim


def _matmul_tiles(M, K, tn, a_bytes, b_bytes, has_add):
    for tk in (4096, 2048, 1536, 1152, 1024, 640, 512, 384, 256, 128):
        if K % tk:
            continue
        for tm in (2048, 1152, 1024, 512, 384, 256, 128):
            if M % tm:
                continue
            blocks = tm * tk * a_bytes + tk * tn * b_bytes + tm * tn * 4 * (2 if has_add else 1)
            if 2 * blocks + (tm * tn * 4 if tk < K else 0) <= MATMUL_VMEM and tm >= min(M, 1024):
                return tm, tk
    return _pick(M, (512, 256, 128)), _pick(K, (512, 256, 128))


def _matmul(a, b, kind, name, add=None, after=None, also_bf16=False):
    if kind == "nn":
        (M, K), N = a.shape, b.shape[1]
    elif kind == "nt":
        (M, K), N = a.shape, b.shape[0]
    else:
        (K, M), N = a.shape, b.shape[1]
    has_add = add is not None
    tn = _pick(N, (512, 640, 384, 256, 128))
    tm, tk = _matmul_tiles(M, K, tn, a.dtype.itemsize, b.dtype.itemsize, has_add)
    nk = K // tk
    a_spec = pl.BlockSpec((tk, tm), lambda i, j, k: (k, i)) if kind == "tn" else pl.BlockSpec((tm, tk), lambda i, j, k: (i, k))
    b_spec = pl.BlockSpec((tn, tk), lambda i, j, k: (j, k)) if kind == "nt" else pl.BlockSpec((tk, tn), lambda i, j, k: (k, j))
    o_spec = pl.BlockSpec((tm, tn), lambda i, j, k: (i, j))

    extra = ([add] if has_add else []) + ([after] if after is not None else [])
    extra_specs = ([o_spec] if has_add else []) + ([pl.BlockSpec(TOKEN_SHAPE, lambda i, j, k: (0, 0))] if after is not None else [])

    out_dtypes = (F32, BF16) if also_bf16 else (F32,)

    def body(a_ref, b_ref, *rest):
        outs = rest[len(extra):len(extra) + len(out_dtypes)]

        def write(val):
            if has_add:
                val = val + rest[0][...]
            for o_ref in outs:
                o_ref[...] = val.astype(o_ref.dtype)

        part = _dot_raw(a_ref[...], b_ref[...], kind, False)
        if nk == 1:
            write(part)
            return
        acc = rest[-1]
        kk = pl.program_id(2)

        @pl.when(kk == 0)
        def _():
            acc[...] = part

        @pl.when(kk > 0)
        def _():
            acc[...] += part

        @pl.when(kk == nk - 1)
        def _():
            write(acc[...])

    result = pl.pallas_call(
        body, name=name, grid=(M // tm, N // tn, nk),
        in_specs=[a_spec, b_spec] + extra_specs,
        out_specs=[o_spec] * len(out_dtypes), out_shape=[jax.ShapeDtypeStruct((M, N), dt) for dt in out_dtypes],
        scratch_shapes=[pltpu.VMEM((tm, tn), F32)] if nk > 1 else [],
        compiler_params=_params(dimension_semantics=("parallel", "parallel", "arbitrary")),
    )(a, b, *extra)
    return result if also_bf16 else result[0]


def _halo_specs(ts, nt, width, prev=True, main=True, nxt=True):
    per = ts // SUBLANES
    last8 = nt * per - 1
    specs = []
    if prev:
        specs.append(pl.BlockSpec((SUBLANES, width), lambda cb, i: (jnp.maximum(i * per - 1, 0), cb)))
    if main:
        specs.append(pl.BlockSpec((ts, width), lambda cb, i: (i, cb)))
    if nxt:
        specs.append(pl.BlockSpec((SUBLANES, width), lambda cb, i: (jnp.minimum((i + 1) * per, last8), cb)))
    return specs


def _taps(ext, ktaps, lo, size):
    return [ext[lo:lo + size] if j == 0 else pltpu.roll(ext, j, 0)[lo:lo + size] for j in range(ktaps)]


def _ahead(ext, j, size):
    n = ext.shape[0]
    return ext[:size] if j == 0 else pltpu.roll(ext, n - j, 0)[:size]


def _lane_block(ref, k):
    return ref[:, k * MIXER_LANES:(k + 1) * MIXER_LANES]


def _mixer_a_fwd(proj_a, conv_w):
    T = proj_a.shape[0]
    nblk = proj_a.shape[1] // (4 * MIXER_LANES)
    ts = min(ROW_TILE, T)
    nt = T // ts

    def body(pp, pm, w_ref, y_ref):
        i = pl.program_id(1)
        u_prev = jnp.where(i > 0, _lane_block(pp, 0) * _lane_block(pp, 1), 0.0)
        ext = jnp.concatenate([u_prev, _lane_block(pm, 0) * _lane_block(pm, 1)], axis=0)
        t0, t1, t2 = _taps(ext, 3, SUBLANES, ts)
        cv = w_ref[2:3, :] * t0 + w_ref[1:2, :] * t1 + w_ref[0:1, :] * t2
        y_ref[...] = (_lane_block(pm, 2) * cv * _silu(_lane_block(pm, 3))).astype(y_ref.dtype)

    return pl.pallas_call(
        body, name="mixer_a_fwd", grid=(nblk, nt),
        in_specs=_halo_specs(ts, nt, 4 * MIXER_LANES, nxt=False)
        + [pl.BlockSpec((conv_w.shape[0], MIXER_LANES), lambda cb, i: (0, cb))],
        out_specs=pl.BlockSpec((ts, MIXER_LANES), lambda cb, i: (i, cb)),
        out_shape=jax.ShapeDtypeStruct((T, nblk * MIXER_LANES), BF16), compiler_params=_params(),
    )(proj_a, proj_a, conv_w)


def _mixer_a_bwd(proj_a, dy, conv_w):
    T = proj_a.shape[0]
    nblk = proj_a.shape[1] // (4 * MIXER_LANES)
    ts = min(ROW_TILE, T)
    nt = T // ts
    kt = conv_w.shape[0]

    def body(pp, pm, pn, dym, dyn, w_ref, dp_ref, dw_ref):
        i = pl.program_id(1)
        hm, cm, bm, zm = (_lane_block(pm, k) for k in range(4))
        u_prev = jnp.where(i > 0, _lane_block(pp, 0) * _lane_block(pp, 1), 0.0)
        ext = jnp.concatenate([u_prev, hm * cm], axis=0)
        dy_ext = jnp.concatenate([dym[...], jnp.where(i < nt - 1, dyn[...], 0.0)], axis=0)
        b_ext = jnp.concatenate([bm, _lane_block(pn, 2)], axis=0)
        sz_ext = _silu(jnp.concatenate([zm, _lane_block(pn, 3)], axis=0))
        dcv_ext = dy_ext * b_ext * sz_ext
        w = [w_ref[j:j + 1, :] for j in range(kt)]
        du = sum(w[kt - 1 - j] * _ahead(dcv_ext, j, ts) for j in range(kt))
        taps = _taps(ext, kt, SUBLANES, ts)
        cv = sum(w[kt - 1 - j] * taps[j] for j in range(kt))
        for part, d in enumerate((du * cm, du * hm, dym[...] * cv * sz_ext[:ts], dym[...] * bm * cv * _dsilu(zm))):
            dp_ref[:, part * MIXER_LANES:(part + 1) * MIXER_LANES] = d.astype(dp_ref.dtype)
        dcv = dcv_ext[:ts]

        @pl.when(i == 0)
        def _():
            dw_ref[...] = jnp.zeros_like(dw_ref)

        for j in range(kt):
            dw_ref[j:j + 1, :] += jnp.sum(dcv * taps[kt - 1 - j], axis=0, keepdims=True)

    return pl.pallas_call(
        body, name="mixer_a_bwd", grid=(nblk, nt),
        in_specs=_halo_specs(ts, nt, 4 * MIXER_LANES) + _halo_specs(ts, nt, MIXER_LANES, prev=False)
        + [pl.BlockSpec((kt, MIXER_LANES), lambda cb, i: (0, cb))],
        out_specs=[pl.BlockSpec((ts, 4 * MIXER_LANES), lambda cb, i: (i, cb)),
                   pl.BlockSpec((SUBLANES, MIXER_LANES), lambda cb, i: (0, cb))],
        out_shape=[jax.ShapeDtypeStruct(proj_a.shape, BF16),
                   jax.ShapeDtypeStruct((SUBLANES, nblk * MIXER_LANES), F32)],
        compiler_params=_params(),
    )(proj_a, proj_a, proj_a, dy, dy, conv_w)


def _conv_b_fwd(raw, conv_w):
    T = raw.shape[0]
    nblk = raw.shape[1] // CONV_LANES
    ts = min(ROW_TILE, T)
    nt = T // ts
    kt = conv_w.shape[0]

    def body(rp, rm, w_ref, y_ref):
        i = pl.program_id(1)
        ext = jnp.concatenate([jnp.where(i > 0, rp[...], 0.0), rm[...]], axis=0)
        taps = _taps(ext, kt, SUBLANES, ts)
        y_ref[...] = _silu(sum(w_ref[kt - 1 - j:kt - j, :] * taps[j] for j in range(kt)))

    return pl.pallas_call(
        body, name="conv_b_fwd", grid=(nblk, nt),
        in_specs=_halo_specs(ts, nt, CONV_LANES, nxt=False) + [pl.BlockSpec((kt, CONV_LANES), lambda cb, i: (0, cb))],
        out_specs=pl.BlockSpec((ts, CONV_LANES), lambda cb, i: (i, cb)),
        out_shape=jax.ShapeDtypeStruct(raw.shape, F32), compiler_params=_params(),
    )(raw, raw, conv_w)


def _conv_b_bwd(raw, dy, conv_w):
    T = raw.shape[0]
    nblk = raw.shape[1] // CONV_LANES
    ts = min(ROW_TILE, T)
    nt = T // ts
    kt = conv_w.shape[0]

    def body(rp, rm, rn, dym, dyn, w_ref, dr_ref, dw_ref):
        i = pl.program_id(1)
        ext = jnp.concatenate([jnp.where(i > 0, rp[...], 0.0), rm[...], rn[...]], axis=0)
        w = [w_ref[j:j + 1, :] for j in range(kt)]
        taps = _taps(ext, kt, SUBLANES, ts + SUBLANES)
        xc_ext = sum(w[kt - 1 - j] * taps[j] for j in range(kt))
        dy_ext = jnp.concatenate([dym[...], jnp.where(i < nt - 1, dyn[...], 0.0)], axis=0)
        dxc_ext = dy_ext * _dsilu(xc_ext)
        dr_ref[...] = sum(w[kt - 1 - j] * _ahead(dxc_ext, j, ts) for j in range(kt)).astype(dr_ref.dtype)
        dxc = dxc_ext[:ts]

        @pl.when(i == 0)
        def _():
            dw_ref[...] = jnp.zeros_like(dw_ref)

        for j in range(kt):
            dw_ref[j:j + 1, :] += jnp.sum(dxc * taps[kt - 1 - j][:ts], axis=0, keepdims=True)

    return pl.pallas_call(
        body, name="conv_b_bwd", grid=(nblk, nt),
        in_specs=_halo_specs(ts, nt, CONV_LANES) + _halo_specs(ts, nt, CONV_LANES, prev=False)
        + [pl.BlockSpec((kt, CONV_LANES), lambda cb, i: (0, cb))],
        out_specs=[pl.BlockSpec((ts, CONV_LANES), lambda cb, i: (i, cb)),
                   pl.BlockSpec((SUBLANES, CONV_LANES), lambda cb, i: (0, cb))],
        out_shape=[jax.ShapeDtypeStruct(raw.shape, BF16), jax.ShapeDtypeStruct((SUBLANES, nblk * CONV_LANES), F32)],
        compiler_params=_params(),
    )(raw, raw, raw, dy, dy, conv_w)


def _split_heads(ref, base, nheads, rows=slice(None)):
    return jnp.stack([ref[rows, base + h * HEAD: base + (h + 1) * HEAD] for h in range(nheads)], axis=0)


def _store_heads(ref, base, x, rows=slice(None), accumulate=False):
    for h in range(x.shape[0]):
        lanes = slice(base + h * HEAD, base + (h + 1) * HEAD)
        if accumulate:
            ref[rows, lanes] += x[h]
        else:
            ref[rows, lanes] = x[h].astype(ref.dtype)


def _gdn_fwd(qkv, proj_zs, alog, dtb, gain, H):
    T = qkv.shape[0]
    C, HW = GDN_CHUNK, H * HEAD
    nc = T // C
    zw = HW + LANES

    def body(qkv_ref, zs_ref, alog_ref, dtb_ref, gain_ref, o_ref, sall_ref, s_scr):
        @pl.when(pl.program_id(0) == 0)
        def _():
            s_scr[...] = jnp.zeros_like(s_scr)

        sall_ref[0] = s_scr[...]
        outs, states = _gdn_step(
            s_scr[...], _split_heads(qkv_ref, 0, H), _split_heads(qkv_ref, HW, H),
            _split_heads(qkv_ref, 2 * HW, H), _split_heads(zs_ref, 0, H), zs_ref[:, HW:HW + LANES],
            alog_ref[...], dtb_ref[...], gain_ref[...])
        _store_heads(o_ref, 0, outs)
        s_scr[...] = states

    row = pl.BlockSpec((1, LANES), lambda i: (0, 0))
    return pl.pallas_call(
        body, name="gdn_fwd", grid=(nc,),
        in_specs=[pl.BlockSpec((C, 3 * HW), lambda i: (i, 0)), pl.BlockSpec((C, zw), lambda i: (i, 0)), row, row, row],
        out_specs=[pl.BlockSpec((C, HW), lambda i: (i, 0)), pl.BlockSpec((1, H, HEAD, HEAD), lambda i: (i, 0, 0, 0))],
        out_shape=[jax.ShapeDtypeStruct((T, HW), BF16), jax.ShapeDtypeStruct((nc, H, HEAD, HEAD), F32)],
        scratch_shapes=[pltpu.VMEM((H, HEAD, HEAD), F32)], compiler_params=_params(),
    )(qkv, proj_zs, alog, dtb, gain)


def _gdn_bwd(qkv, proj_zs, do, s_all, alog, dtb, gain, H):
    T = qkv.shape[0]
    C, HW = GDN_CHUNK, H * HEAD
    nc = T // C
    zw = HW + LANES

    def body(qkv_ref, zs_ref, do_ref, sin_ref, alog_ref, dtb_ref, gain_ref,
             dqkv_ref, dzs_ref, dalog_ref, ddtb_ref, dgain_ref, ds_scr):
        @pl.when(pl.program_id(0) == 0)
        def _():
            ds_scr[...] = jnp.zeros_like(ds_scr)
            dalog_ref[...] = jnp.zeros_like(dalog_ref)
            ddtb_ref[...] = jnp.zeros_like(ddtb_ref)
            dgain_ref[...] = jnp.zeros_like(dgain_ref)

        primals = (sin_ref[0], _split_heads(qkv_ref, 0, H),
                   _split_heads(qkv_ref, HW, H), _split_heads(qkv_ref, 2 * HW, H), _split_heads(zs_ref, 0, H),
                   zs_ref[:, HW:HW + LANES], alog_ref[...], dtb_ref[...], gain_ref[...])
        _, vjp = jax.vjp(_gdn_step, *primals)
        dS, dq, dk, dv, dz, dsmall, dalog, ddtb, dgain = vjp((_split_heads(do_ref, 0, H), ds_scr[...]))
        ds_scr[...] = dS
        _store_heads(dqkv_ref, 0, dq)
        _store_heads(dqkv_ref, HW, dk)
        _store_heads(dqkv_ref, 2 * HW, dv)
        _store_heads(dzs_ref, 0, dz)
        dzs_ref[:, HW:HW + LANES] = dsmall.astype(dzs_ref.dtype)
        dalog_ref[...] += dalog
        ddtb_ref[...] += ddtb
        dgain_ref[...] += dgain

    row = pl.BlockSpec((1, LANES), lambda i: (0, 0))
    rev = lambda i: nc - 1 - i
    return pl.pallas_call(
        body, name="gdn_bwd", grid=(nc,),
        in_specs=[pl.BlockSpec((C, 3 * HW), lambda i: (rev(i), 0)), pl.BlockSpec((C, zw), lambda i: (rev(i), 0)),
                  pl.BlockSpec((C, HW), lambda i: (rev(i), 0)),
                  pl.BlockSpec((1, H, HEAD, HEAD), lambda i: (rev(i), 0, 0, 0)), row, row, row],
        out_specs=[pl.BlockSpec((C, 3 * HW), lambda i: (rev(i), 0)), pl.BlockSpec((C, zw), lambda i: (rev(i), 0)),
                   row, row, row],
        out_shape=[jax.ShapeDtypeStruct(qkv.shape, F32), jax.ShapeDtypeStruct(proj_zs.shape, BF16)]
        + [jax.ShapeDtypeStruct((1, LANES), F32)] * 3,
        scratch_shapes=[pltpu.VMEM((H, HEAD, HEAD), F32)], compiler_params=_params(),
    )(qkv, proj_zs, do, s_all, alog, dtb, gain)


def _hgrn_refs(proj_ref, lb_ref, HP):
    W = HP * HEAD
    return (_split_heads(proj_ref, 0, HP), _split_heads(proj_ref, W, HP), _split_heads(proj_ref, 2 * W, HP),
            _split_heads(proj_ref, 3 * W, HP), _split_heads(lb_ref, 0, HP, slice(0, 1)),
            _split_heads(lb_ref, 0, HP, slice(1, 2)))


def _hgrn_fwd(proj, lower_bounds, gain, nheads):
    T = proj.shape[0]
    C, HP = HGRN_CHUNK, HGRN_HEADS_PER_STEP
    ng, nc, W = nheads // HP, T // C, HP * HEAD

    def body(proj_ref, lb_ref, gain_ref, o_ref, sall_ref, s_scr):
        @pl.when(pl.program_id(1) == 0)
        def _():
            s_scr[...] = jnp.zeros_like(s_scr)

        sall_ref[0] = s_scr[...]
        qr, fr, vi, z, lb0, lb1 = _hgrn_refs(proj_ref, lb_ref, HP)
        outs, states = _hgrn_step(s_scr[...], qr, fr, vi, z, lb0, lb1, gain_ref[...])
        _store_heads(o_ref, 0, outs)
        s_scr[...] = states

    return pl.pallas_call(
        body, name="hgrn_fwd", grid=(ng, nc),
        in_specs=[pl.BlockSpec((C, 4 * W), lambda g, i: (i, g)), pl.BlockSpec((2, W), lambda g, i: (0, g)),
                  pl.BlockSpec((1, LANES), lambda g, i: (0, 0))],
        out_specs=[pl.BlockSpec((C, W), lambda g, i: (i, g)),
                   pl.BlockSpec((1, HP, HEAD, HEAD), lambda g, i: (i, g, 0, 0))],
        out_shape=[jax.ShapeDtypeStruct((T, nheads * HEAD), BF16), jax.ShapeDtypeStruct((nc, nheads, HEAD, HEAD), F32)],
        scratch_shapes=[pltpu.VMEM((HP, HEAD, HEAD), F32)], compiler_params=_params(),
    )(proj, lower_bounds, gain)


def _hgrn_bwd(proj, do, s_all, lower_bounds, gain, nheads):
    T = proj.shape[0]
    C, HP = HGRN_CHUNK, HGRN_HEADS_PER_STEP
    ng, nc, W = nheads // HP, T // C, HP * HEAD

    def body(proj_ref, do_ref, sin_ref, lb_ref, gain_ref, dproj_ref, dlb_ref, dgain_ref, ds_scr):
        first = pl.program_id(1) == 0

        @pl.when(first)
        def _():
            ds_scr[...] = jnp.zeros_like(ds_scr)
            dlb_ref[...] = jnp.zeros_like(dlb_ref)

        @pl.when(first & (pl.program_id(0) == 0))
        def _():
            dgain_ref[...] = jnp.zeros_like(dgain_ref)

        qr, fr, vi, z, lb0, lb1 = _hgrn_refs(proj_ref, lb_ref, HP)
        primals = (sin_ref[0], qr, fr, vi, z, lb0, lb1, gain_ref[...])
        _, vjp = jax.vjp(_hgrn_step, *primals)
        dS, dq, df, dv, dz, dlb0, dlb1, dgain = vjp((_split_heads(do_ref, 0, HP), ds_scr[...]))
        ds_scr[...] = dS
        for part, d in enumerate((dq, df, dv, dz)):
            _store_heads(dproj_ref, part * W, d)
        _store_heads(dlb_ref, 0, dlb0, slice(0, 1), accumulate=True)
        _store_heads(dlb_ref, 0, dlb1, slice(1, 2), accumulate=True)
        dgain_ref[...] += dgain

    rev = lambda i: nc - 1 - i
    return pl.pallas_call(
        body, name="hgrn_bwd", grid=(ng, nc),
        in_specs=[pl.BlockSpec((C, 4 * W), lambda g, i: (rev(i), g)), pl.BlockSpec((C, W), lambda g, i: (rev(i), g)),
                  pl.BlockSpec((1, HP, HEAD, HEAD), lambda g, i: (rev(i), g, 0, 0)),
                  pl.BlockSpec((2, W), lambda g, i: (0, g)), pl.BlockSpec((1, LANES), lambda g, i: (0, 0))],
        out_specs=[pl.BlockSpec((C, 4 * W), lambda g, i: (rev(i), g)), pl.BlockSpec((2, W), lambda g, i: (0, g)),
                   pl.BlockSpec((1, LANES), lambda g, i: (0, 0))],
        out_shape=[jax.ShapeDtypeStruct(proj.shape, BF16), jax.ShapeDtypeStruct(lower_bounds.shape, F32),
                   jax.ShapeDtypeStruct((1, LANES), F32)],
        scratch_shapes=[pltpu.VMEM((HP, HEAD, HEAD), F32)], compiler_params=_params(),
    )(proj, do, s_all, lower_bounds, gain)


def _post_specs(T):
    tr = min(POST_TILE, T)
    tile = lambda w: pl.BlockSpec((tr, w), lambda i: (i, 0))
    full = lambda r, w: pl.BlockSpec((r, w), lambda i: (0, 0))
    return tr, tile, full


def _post_fwd(s, x, p, g, b, wg, wpl, name):
    T, D = x.shape
    P = p.shape[1]
    tr, tile, full = _post_specs(T)

    def body(s_ref, x_ref, p_ref, g_ref, b_ref, wg_ref, wpl_ref, o_ref, o16_ref):
        x1 = _post_norm(s_ref[...], x_ref[...], g_ref[...], b_ref[...])
        xn = _post_gate(x1, _dot_raw(x1, wg_ref[...], "nn", False), _dot_raw(p_ref[...], wpl_ref[...], "nn", False))
        o_ref[...] = xn
        o16_ref[...] = xn.astype(BF16)

    return pl.pallas_call(
        body, name=name, grid=(T // tr,),
        in_specs=[tile(D), tile(D), tile(P), full(1, D), full(1, D), full(D, D), full(P, D)],
        out_specs=[tile(D), tile(D)],
        out_shape=[jax.ShapeDtypeStruct((T, D), F32), jax.ShapeDtypeStruct((T, D), BF16)], compiler_params=_params(),
    )(s, x, p, g, b, wg, wpl)


def _post_bwd(s, x, p, g, b, wg, wpl, dnext, name, with_loss):
    T, D = x.shape
    P = p.shape[1]
    tr, tile, full = _post_specs(T)

    def body(s_ref, x_ref, p_ref, g_ref, b_ref, wg_ref, wpl_ref, dn_ref,
             ds_ref, dx_ref, dg_ref, db_ref, dwg_ref, dwpl_ref, loss_ref):
        @pl.when(pl.program_id(0) == 0)
        def _():
            for r in (dg_ref, db_ref, dwg_ref, dwpl_ref, loss_ref):
                r[...] = jnp.zeros_like(r)

        x1, vjp_norm = jax.vjp(_post_norm, s_ref[...], x_ref[...], g_ref[...], b_ref[...])
        gate_pre = _dot_raw(x1, wg_ref[...], "nn", False)
        pp = _dot_raw(p_ref[...], wpl_ref[...], "nn", False)
        xn, vjp_gate = jax.vjp(_post_gate, x1, gate_pre, pp)
        if with_loss:
            err = xn - dn_ref[...]
            loss_ref[...] += 0.5 * jnp.sum(jnp.sum(err * err, axis=-1, keepdims=True), axis=0, keepdims=True) / D
            dn = err / D
        else:
            dn = dn_ref[...]
        dx1, dgp, dpp = vjp_gate(dn)
        dwg_ref[...] += _dot_raw(x1, dgp, "tn", False)
        dwpl_ref[...] += _dot_raw(p_ref[...], dpp, "tn", False)
        dx1 = dx1 + _dot_raw(dgp, wg_ref[...], "nt", False)
        ds, dx, dg, db = vjp_norm(dx1)
        ds_ref[...] = ds.astype(ds_ref.dtype)
        dx_ref[...] = dx
        dg_ref[...] += dg
        db_ref[...] += db

    return pl.pallas_call(
        body, name=name, grid=(T // tr,),
        in_specs=[tile(D), tile(D), tile(P), full(1, D), full(1, D), full(D, D), full(P, D), tile(D)],
        out_specs=[tile(D), tile(D), full(1, D), full(1, D), full(D, D), full(P, D), full(SUBLANES, LANES)],
        out_shape=[jax.ShapeDtypeStruct((T, D), BF16), jax.ShapeDtypeStruct((T, D), F32)]
        + [jax.ShapeDtypeStruct((1, D), F32)] * 2
        + [jax.ShapeDtypeStruct((D, D), F32), jax.ShapeDtypeStruct((P, D), F32),
           jax.ShapeDtypeStruct((SUBLANES, LANES), F32)],
        compiler_params=_params(),
    )(s, x, p, g, b, wg, wpl, dnext)


def _adam_math(w, g, m, v):
    m = ADAM_B1 * m + (1.0 - ADAM_B1) * g
    v = ADAM_B2 * v + (1.0 - ADAM_B2) * (g * g)
    m_hat = m / (1.0 - ADAM_B1 ** ADAM_STEP)
    v_hat = v / (1.0 - ADAM_B2 ** ADAM_STEP)
    return -ADAM_LR * (m_hat / (jnp.sqrt(v_hat) + ADAM_EPS) + ADAM_WD * w), m, v


def _shard_tiles(R, C):
    tr = _pick(R, (256, 128, 64, 32, 16, 8))
    return (tr, C) if tr < R or R % SUBLANES == 0 else (R, _pick(C, (256, 128)))


def _adam_sharded(w, m, v, g8, got, me, name):
    rows_apart = w.shape[1] == 1 and w.shape[0] > 1
    (L, R, C) = (1, w.shape[0], w.shape[2]) if rows_apart else w.shape
    tr, tc = (R, LANES) if rows_apart else _shard_tiles(R, C)
    nr, nc = R // tr, C // tc
    side_by_side = g8.ndim == 2

    def body(me_ref, w_ref, m_ref, v_ref, p_ref, *rest):
        got_refs, (g_ref, d_ref, mo_ref, vo_ref) = rest[:7], rest[7:]
        g = p_ref[...] if side_by_side else p_ref[0]
        for r in got_refs:
            g = g + r[0].astype(F32)
        if rows_apart:
            d, mn, vn = _adam_math(w_ref[:, 0, :], g, m_ref[:, 0, :], v_ref[:, 0, :])
            for ref, val in ((g_ref, g), (d_ref, d), (mo_ref, mn), (vo_ref, vn)):
                ref[:, 0, :] = val
            return
        d, mn, vn = _adam_math(w_ref[0], g, m_ref[0], v_ref[0])
        g_ref[0] = g
        d_ref[0] = d
        mo_ref[0] = mn
        vo_ref[0] = vn

    if rows_apart:
        t3 = pl.BlockSpec((tr, 1, tc), lambda l, i, j, q: (i, 0, j))
    else:
        t3 = pl.BlockSpec((1, tr, tc), lambda l, i, j, q: (l, i, j))
    slot = lambda k: pl.BlockSpec((1, tr, tc), lambda l, i, j, q: (k, l * nr + i, j))
    if side_by_side:
        mine = pl.BlockSpec((tr, tc), lambda l, i, j, q: (l * nr + i, q[0] * nc + j))
    else:
        mine = pl.BlockSpec((1, tr, tc), lambda l, i, j, q: (q[0], l * nr + i, j))
    return pl.pallas_call(
        body, name=name,
        grid_spec=pltpu.PrefetchScalarGridSpec(
            num_scalar_prefetch=1, grid=(L, nr, nc),
            in_specs=[t3, t3, t3, mine] + [slot(k) for k in range(7)], out_specs=[t3, t3, t3, t3]),
        out_shape=[jax.ShapeDtypeStruct(w.shape, F32)] * 4, compiler_params=_params(),
    )(me, w, m, v, g8, *([got] * 7))


def _small_rows(shapes):
    offsets, r = [], 0
    for rows, _ in shapes:
        offsets.append(r)
        r += rows
    return offsets, -(-(r + 1) // SUBLANES) * SUBLANES, max(cols for _, cols in shapes)


def _pack_small_grads(grads, loss):
    offsets, total, width = _small_rows([g.shape for g in grads])
    packed = jnp.zeros((total, width), F32)
    for g, r in zip(grads, offsets):
        packed = lax.dynamic_update_slice(packed, g, (r, 0))
    return lax.dynamic_update_slice(packed, loss.reshape(1, 1), (total - 1, 0))


def _adam_replicated(params, g8):
    shapes = [w.shape for w, _, _ in params]
    offsets, total, width = _small_rows(shapes)
    n = len(params)

    def body(*refs):
        g_ref, outs, loss_ref, g_scr = refs[3 * n], refs[3 * n + 1:7 * n + 1], refs[7 * n + 1], refs[7 * n + 2]
        g = g_ref[0]
        for k in range(1, 8):
            g = g + g_ref[k]
        g_scr[...] = g
        for i, (rows, cols) in enumerate(shapes):
            gp = g_scr[offsets[i]:offsets[i] + rows, 0:cols]
            d, mn, vn = _adam_math(refs[3 * i][...], gp, refs[3 * i + 1][...], refs[3 * i + 2][...])
            for ref, val in zip(outs[4 * i:4 * i + 4], (gp, d, mn, vn)):
                ref[...] = val
        loss_ref[...] = g_scr[total - 1:total, 0:LANES]

    out = pl.pallas_call(
        body, name="adam_replicated",
        out_shape=[jax.ShapeDtypeStruct(shp, F32) for shp in shapes for _ in range(4)]
        + [jax.ShapeDtypeStruct((1, LANES), F32)],
        scratch_shapes=[pltpu.VMEM((total, width), F32)], compiler_params=_params(),
    )(*[a for triple in params for a in triple], g8)
    return [out[4 * i:4 * i + 4] for i in range(n)], out[4 * n][0, 0]


def _place():
    return lax.axis_index("x"), lax.axis_index("y"), lax.axis_index("c")


def _all_gather(shard, name):
    def body(x_ref, out_ref, send_sems, recv_sems, local_sem):
        x, y, c = _place()
        me, sibling = (x, y, c), (x, y, 1 - c)
        chips = [(1 - x, y), (x, 1 - y), (1 - x, 1 - y)]

        def slab(px, py, pc):
            return out_ref.at[4 * px + 2 * py + pc]

        def copy(k, block, to, src=None):
            return pltpu.make_async_remote_copy(
                src_ref=slab(*block) if src is None else src, dst_ref=slab(*block),
                send_sem=send_sems.at[k], recv_sem=recv_sems.at[k], device_id=to, device_id_type=MESH)

        mine = pltpu.make_async_copy(x_ref, slab(*me), local_sem)
        mine.start()
        first = [copy(0, me, sibling, src=x_ref)]
        first += [copy(1 + j, me, (*chip, c), src=x_ref) for j, chip in enumerate(chips)]
        for cp in first:
            cp.start()
        passed = [copy(4 + j, (*chip, c), sibling) for j, chip in enumerate(chips)]
        for j, chip in enumerate(chips):
            copy(1 + j, (*chip, c), me).wait_recv()
            passed[j].start()
        copy(0, sibling, me).wait_recv()
        for j, chip in enumerate(chips):
            copy(4 + j, (*chip, 1 - c), me).wait_recv()
        for cp in first + passed:
            cp.wait_send()
        mine.wait()

    return pl.pallas_call(
        body, name=name, out_shape=jax.ShapeDtypeStruct((8,) + shard.shape, shard.dtype),
        in_specs=[pl.BlockSpec(memory_space=pl.ANY)], out_specs=pl.BlockSpec(memory_space=pl.ANY),
        scratch_shapes=[pltpu.SemaphoreType.DMA((7,)), pltpu.SemaphoreType.DMA((7,)), pltpu.SemaphoreType.DMA],
    )(shard)


_HBM = pl.BlockSpec(memory_space=pltpu.HBM)
_SEM = pl.BlockSpec(memory_space=pltpu.SEMAPHORE)
_DATAFLOW = pltpu.SideEffectType.DATAFLOW_SIDE_EFFECTING
TOKEN_SHAPE = (SUBLANES, LANES)


def _peers(x, y, c):
    flip = lambda v, bit: 1 - v if bit else v
    return [(flip(x, r >> 2 & 1), flip(y, r >> 1 & 1), flip(c, r & 1)) for r in range(1, 8)]


def _scatter_plan(x, y, c):
    return [(4 * px + 2 * py + pc, k, (px, py, pc)) for k, (px, py, pc) in enumerate(_peers(x, y, c))]


def _exchange_copies(plan, src_ref, land_ref, send_sems, recv_sems):
    C = land_ref.shape[-1]
    block = (lambda b: src_ref.at[b]) if len(src_ref.shape) == len(land_ref.shape) else (
        lambda b: src_ref.at[:, pl.ds(b * C, C)])
    return [pltpu.make_async_remote_copy(
        src_ref=block(blk), dst_ref=land_ref.at[slot], send_sem=send_sems.at[k], recv_sem=recv_sems.at[k],
        device_id=peer, device_id_type=MESH) for k, (blk, slot, peer) in enumerate(plan(*_place()))]


def _exchange_start(src, n_slots, block_shape, plan, name):
    land_shape = (n_slots,) + tuple(block_shape)
    n = len(plan(0, 0, 0))

    def body(src_ref, land_ref, send_sems, recv_sems, src_thru, land_thru, token):
        for cp in _exchange_copies(plan, src_ref, land_ref, send_sems, recv_sems):
            cp.start()
        token[...] = jnp.zeros_like(token)

    return pl.pallas_call(
        body, name=name,
        out_shape=(pltpu.SemaphoreType.DMA((n,)), pltpu.SemaphoreType.DMA((n,)), pltpu.HBM(src.shape, src.dtype),
                   pltpu.HBM(land_shape, src.dtype), jax.ShapeDtypeStruct(TOKEN_SHAPE, F32)),
        in_specs=(_HBM, _HBM), out_specs=(_SEM, _SEM, _HBM, _HBM, pl.BlockSpec(memory_space=pltpu.VMEM)),
        input_output_aliases={0: 2, 1: 3}, compiler_params=pltpu.CompilerParams(has_side_effects=_DATAFLOW),
    )(pltpu.with_memory_space_constraint(src, pltpu.HBM),
      pltpu.with_memory_space_constraint(lax.empty(land_shape, src.dtype), pltpu.HBM))


def _exchange_wait(handle, plan, after, name):
    send_sems, recv_sems, src_thru, land_thru, _ = handle

    def body(src_ref, land_ref, send_sems, recv_sems, after_ref, src_dead, got_ref):
        for cp in _exchange_copies(plan, src_ref, land_ref, send_sems, recv_sems):
            cp.wait_send()
            cp.wait_recv()

    return pl.pallas_call(
        body, name=name,
        out_shape=(pltpu.HBM(src_thru.shape, src_thru.dtype), pltpu.HBM(land_thru.shape, land_thru.dtype)),
        in_specs=(_HBM, _HBM, _SEM, _SEM, pl.BlockSpec(memory_space=pl.ANY)), out_specs=(_HBM, _HBM),
        input_output_aliases={0: 0, 1: 1}, compiler_params=pltpu.CompilerParams(has_side_effects=_DATAFLOW),
    )(src_thru, land_thru, send_sems, recv_sems, after)[1]


def _gather_plan(x, y, c):
    return [(0, 4 * x + 2 * y + c, peer) for peer in _peers(x, y, c)]


class _LateGather:
    def __init__(self, shard, name):
        self.shard, self.name = shard, name
        self.handle = _exchange_start(shard[None], 8, shard.shape, _gather_plan, name + "_start")

    def get(self, after):
        land = _exchange_wait(self.handle, _gather_plan, after, self.name + "_wait")
        x, y, c = _place()
        return lax.dynamic_update_slice(land, self.shard[None], (4 * x + 2 * y + c, 0, 0))


class _GradExchange:
    def __init__(self, me, layouts):
        self.me, self.layouts, self.pending = me, layouts, {}

    def start(self, tag, grad, grad16=None):
        g8 = self.layouts[tag](grad)
        g16 = g8.astype(BF16) if grad16 is None else self.layouts[tag](grad16)
        block_shape = g8.shape[1:] if g8.ndim == 3 else (g8.shape[0], g8.shape[1] // 8)
        handle = _exchange_start(g16, 7, block_shape, _scatter_plan, "rs_start_" + tag)
        self.pending[tag] = (g8, handle)
        return handle[4]

    def finish(self, tag, w, m, v, after):
        g8, handle = self.pending.pop(tag)
        got = _exchange_wait(handle, _scatter_plan, after, "rs_wait_" + tag)
        return _adam_sharded(w, m, v, g8, got, self.me, "adam_" + tag)


def _local_grads(x, p0, p1, target, wt_zs, wt_a, wt_qkv, late, conv_a, conv_b,
                 a_log, dt_bias, gdn_gain, lower_bounds, hgrn_gain, ln_g, ln_b, on_grad=None):
    H = a_log.shape[1]
    pad_small = ((0, 0), (H, LANES - 2 * H))
    alog_row = jnp.pad(a_log, pad_small)
    dtb_row = jnp.pad(dt_bias, pad_small)

    x16 = x.astype(BF16)
    proj_zs = _matmul(x16, wt_zs, "nt", "proj_even_zs", after=late.started)
    proj_a = _matmul(x16, wt_a, "nt", "proj_even_a", after=late.started)
    proj_qkv = _matmul(x16, wt_qkv, "nt", "proj_even_qkv", after=late.started)
    y_a = _mixer_a_fwd(proj_a, conv_a)
    qkv = _conv_b_fwd(proj_qkv, conv_b)
    o2, s_gdn = _gdn_fwd(qkv, proj_zs, alog_row, dtb_row, gdn_gain, H)
    woute_a, woute_b = late.out_even(o2)
    wg, wpl = late.gate(o2)
    s_e = _matmul(o2, woute_b, "nn", "out_even_b", add=_matmul(y_a, woute_a, "nn", "out_even_a"))
    x2, x2_16 = _post_fwd(s_e, x, p0, ln_g[0:1], ln_b[0:1], wg[0], wpl[0], "post_even_fwd")
    wino, wouto = late.odd(s_e)
    nheads_o = wouto.shape[0] // HEAD
    proj_o = _matmul(x2_16, wino, "nn", "proj_odd")
    o4, s_hgrn = _hgrn_fwd(proj_o, lower_bounds, hgrn_gain, nheads_o)
    s_o = _matmul(o4, wouto, "nn", "out_odd")
    ds_o, dx2, dlng1, dlnb1, dwg1, dwpl1, loss = _post_bwd(
        s_o, x2, p1, ln_g[1:2], ln_b[1:2], wg[1], wpl[1], target, "post_odd_loss_bwd", True)
    do4 = _matmul(ds_o, wouto, "nt", "d_out_odd_act")
    grads = {}

    def emit(tag, grad, grad16=None):
        grads[tag] = grad
        return on_grad(tag, grad, grad16) if on_grad is not None else jnp.zeros(TOKEN_SHAPE, F32)

    tok = emit("w_out_odd", *_matmul(o4, ds_o, "tn", "d_out_odd_w", also_bf16=True))
    dproj_o, dlb, dhgain = _hgrn_bwd(proj_o, do4, s_hgrn, lower_bounds, hgrn_gain + tok[0:1], nheads_o)
    dx2 = _matmul(dproj_o, wino, "nt", "d_proj_odd_act", add=dx2)
    tok = emit("w_in_odd", *_matmul(x2_16, dproj_o, "tn", "d_proj_odd_w", also_bf16=True))
    ds_e, dx, dlng0, dlnb0, dwg0, dwpl0, _ = _post_bwd(
        s_e, x, p0, ln_g[0:1], ln_b[0:1] + tok[0:1, 0:1], wg[0], wpl[0], dx2, "post_even_bwd", False)
    tok = emit("w_pl_gate", jnp.stack([dwg0, dwg1])) + emit("w_pl", jnp.stack([dwpl0, dwpl1]))
    dy_a = _matmul(ds_e, woute_a, "nt", "d_out_even_a_act")
    do2 = _matmul(ds_e, woute_b, "nt", "d_out_even_b_act")
    dwoute_a = _matmul(y_a, ds_e, "tn", "d_out_even_a_w")
    dwoute_b = _matmul(o2, ds_e, "tn", "d_out_even_b_w")
    tok = tok + emit("w_out_even", jnp.concatenate([dwoute_a, dwoute_b], axis=0))
    dqkv, dproj_zs, dalog, ddtb, dggain = _gdn_bwd(qkv, proj_zs, do2, s_gdn, alog_row, dtb_row, gdn_gain + tok[0:1], H)
    dproj_qkv, dconv_b = _conv_b_bwd(proj_qkv, dqkv, conv_b)
    dproj_a, dconv_a = _mixer_a_bwd(proj_a, dy_a, conv_a)
    emit("conv", (dconv_a[:conv_a.shape[0]], dconv_b[:conv_b.shape[0]]))
    tok = emit("w_in_even", (_matmul(dproj_zs, x16, "tn", "d_proj_even_zs_w"), _matmul(dproj_a, x16, "tn", "d_proj_even_a_w"),
                             _matmul(dproj_qkv, x16, "tn", "d_proj_even_qkv_w")))
    dx = _matmul(dproj_zs, wt_zs, "nn", "d_proj_even_zs_act", add=dx, after=tok)
    dx = _matmul(dproj_a, wt_a, "nn", "d_proj_even_a_act", add=dx)
    dx = _matmul(dproj_qkv, wt_qkv, "nn", "d_proj_even_qkv_act", add=dx)
    grads.update(
        loss=loss[0, 0], grad_x=dx, a_log=dalog[:, H:2 * H], dt_bias=ddtb[:, H:2 * H], gdn_gain=dggain,
        lower_bounds=dlb, hgrn_gain=dhgain, ln_g=jnp.concatenate([dlng0, dlng1], axis=0),
        ln_b=jnp.concatenate([dlnb0, dlnb1], axis=0))
    return grads


def _pad_rows(a, rows):
    return jnp.pad(a, ((0, rows - a.shape[0]), (0, 0)))


def _split_in_even(wt_full, AW, HW, H):
    D = wt_full.shape[1]
    n_a = 4 * AW
    n_main = n_a + 3 * HW
    wt_zs = jnp.concatenate([wt_full[n_main:n_main + HW], wt_full[n_main + HW:],
                             jnp.zeros((LANES - 2 * H, D), wt_full.dtype)], axis=0)
    wt_a = wt_full[:n_a].reshape(4, AW // MIXER_LANES, MIXER_LANES, D).transpose(1, 0, 2, 3).reshape(n_a, D)
    return wt_zs, wt_a, wt_full[n_a:n_main]


def _join_in_even(dt_zs, dt_a, dt_qkv, AW, HW, H):
    D = dt_a.shape[1]
    a_nat = dt_a.reshape(AW // MIXER_LANES, 4, MIXER_LANES, D).transpose(1, 0, 2, 3).reshape(4 * AW, D)
    return jnp.concatenate([a_nat, dt_qkv, dt_zs[:HW], dt_zs[HW:HW + 2 * H]], axis=0)


def kernel(x, p, w_in_even, conv_a_w, conv_b_w, a_log, dt_bias, gdn_norm_g, w_out_even, w_in_odd, lower_bounds, hgrn_norm_g, w_out_odd, ln_g, ln_b, w_pl, w_pl_gate, loss_target, m_w_in_even, m_conv_a_w, m_conv_b_w, m_a_log, m_dt_bias, m_gdn_norm_g, m_w_out_even, m_w_in_odd, m_lower_bounds, m_hgrn_norm_g, m_w_out_odd, m_ln_g, m_ln_b, m_w_pl, m_w_pl_gate, v_w_in_even, v_conv_a_w, v_conv_b_w, v_a_log, v_dt_bias, v_gdn_norm_g, v_w_out_even, v_w_in_odd, v_lower_bounds, v_hgrn_norm_g, v_w_out_odd, v_ln_g, v_ln_b, v_w_pl, v_w_pl_gate):
    xi, yi, ci = _place()
    me = jnp.reshape(4 * xi + 2 * yi + ci, (1,)).astype(jnp.int32)
    D = x.shape[2]
    H = a_log.shape[1]
    HW = H * HEAD
    AW = conv_a_w.shape[2] * 8
    OW = w_out_odd.shape[1] * 8
    PD = w_pl.shape[1]
    ka, kb = conv_a_w.shape[1], conv_b_w.shape[1]
    ca, cb = conv_a_w.shape[2], conv_b_w.shape[2]
    gw = HGRN_HEADS_PER_STEP * HEAD
    ngrp = OW // gw

    transposed = lambda a: jnp.transpose(a, (0, 2, 1))
    g_ine = _all_gather(transposed(w_in_even)[0].astype(BF16), "ag_w_in_even")
    wt_zs, wt_a, wt_qkv = _split_in_even(g_ine.reshape(-1, D), AW, HW, H)
    taps = jnp.concatenate([_pad_rows(conv_a_w[0], SUBLANES), _pad_rows(conv_b_w[0], SUBLANES)], axis=1)
    g_taps = _all_gather(taps, "ag_conv")
    conv_a = jnp.transpose(g_taps[:, :ka, :ca], (1, 0, 2)).reshape(ka, 8 * ca)
    conv_b = jnp.transpose(g_taps[:, :kb, ca:], (1, 0, 2)).reshape(kb, 8 * cb)
    behind = lambda shard, dep: lax.optimization_barrier((shard, dep))[0]
    late_oute = _LateGather(behind(w_out_even[0].astype(BF16), (g_ine, g_taps)), "ag_w_out_even")
    late_gate = _LateGather(behind(w_pl_gate.astype(BF16).reshape(-1, D), late_oute.handle[4]), "ag_w_pl_gate")
    late_pl = _LateGather(behind(w_pl.astype(BF16).reshape(DEPTH * PD, -1), late_gate.handle[4]), "ag_w_pl")
    late_ino = _LateGather(behind(w_in_odd[0].astype(BF16), late_pl.handle[4]), "ag_w_in_odd")
    late_outo = _LateGather(behind(w_out_odd[0].astype(BF16), late_ino.handle[4]), "ag_w_out_odd")

    class _Late:
        started = sum(g.handle[4] for g in (late_oute, late_gate, late_pl, late_ino, late_outo))

        @staticmethod
        def out_even(after):
            woute = late_oute.get(after).reshape(-1, D)
            return woute[:AW], woute[AW:]

        @staticmethod
        def gate(after):
            g_gate, g_pl = late_gate.get(after), late_pl.get(after)
            return (g_gate.reshape(8, DEPTH, D // 8, D).transpose(1, 0, 2, 3).reshape(DEPTH, D, D),
                    g_pl.reshape(8, DEPTH, PD, D // 8).transpose(1, 2, 0, 3).reshape(DEPTH, PD, D))

        @staticmethod
        def odd(after):
            g_ino = late_ino.get(after)
            wino = jnp.transpose(g_ino, (1, 0, 2)).reshape(D, 4, ngrp, gw).transpose(0, 2, 1, 3).reshape(D, 4 * OW)
            return wino, late_outo.get(after).reshape(-1, D)

    sh = w_in_even.shape[2]
    tap_blocks = lambda g, width: _pad_rows(g, SUBLANES).reshape(SUBLANES, 8, width).transpose(1, 0, 2)
    owner_layout = {
        "w_in_even": lambda g: _join_in_even(*g, AW, HW, H).reshape(8, sh, D),
        "w_in_odd": lambda g: g.reshape(D, ngrp, 4, gw).transpose(0, 2, 1, 3).reshape(D, 4 * OW),
        "w_out_even": lambda g: g.reshape(8, -1, D),
        "w_out_odd": lambda g: g.reshape(8, -1, D),
        "w_pl_gate": lambda g: g.reshape(DEPTH, 8, D // 8, D).transpose(1, 0, 2, 3).reshape(8, DEPTH * D // 8, D),
        "w_pl": lambda g: g.reshape(DEPTH, PD, 8, D // 8).transpose(2, 0, 1, 3).reshape(8, DEPTH * PD, D // 8),
        "conv": lambda g: jnp.concatenate([tap_blocks(g[0], ca), tap_blocks(g[1], cb)], axis=2),
    }
    exchange = _GradExchange(me, owner_layout)
    gr = _local_grads(x[0], p[0, 0], p[1, 0], loss_target[0], wt_zs, wt_a, wt_qkv, _Late, conv_a, conv_b,
                      a_log, dt_bias, gdn_norm_g, lower_bounds, hgrn_norm_g, ln_g, ln_b, on_grad=exchange.start)

    last = gr["grad_x"]
    pack_taps = lambda a, b: jnp.concatenate([_pad_rows(a[0], SUBLANES), _pad_rows(b[0], SUBLANES)], axis=1)[None]
    o_outo = exchange.finish("w_out_odd", w_out_odd, m_w_out_odd, v_w_out_odd, last)
    o_ino = exchange.finish("w_in_odd", w_in_odd, m_w_in_odd, v_w_in_odd, last)
    o_gate = exchange.finish("w_pl_gate", w_pl_gate, m_w_pl_gate, v_w_pl_gate, last)
    o_pl = exchange.finish("w_pl", w_pl, m_w_pl, v_w_pl, last)
    o_oute = exchange.finish("w_out_even", w_out_even, m_w_out_even, v_w_out_even, last)
    o_taps = exchange.finish("conv", taps[None], pack_taps(m_conv_a_w, m_conv_b_w), pack_taps(v_conv_a_w, v_conv_b_w), last)
    others_done = sum(o[1][0, 0:1, 0:1] for o in (o_outo, o_ino, o_gate, o_pl, o_oute, o_taps))
    rows_first = lambda a: jnp.transpose(a, (2, 0, 1))
    o_ine = [jnp.transpose(o, (1, 2, 0)) for o in exchange.finish(
        "w_in_even", rows_first(w_in_even), rows_first(m_w_in_even), rows_first(v_w_in_even), others_done)]

    small_g = _pack_small_grads([gr["a_log"], gr["dt_bias"], gr["gdn_gain"], gr["lower_bounds"], gr["hgrn_gain"],
                                 gr["ln_g"], gr["ln_b"]], gr["loss"])
    o_small, loss = _adam_replicated(
        [(a_log, m_a_log, v_a_log), (dt_bias, m_dt_bias, v_dt_bias), (gdn_norm_g, m_gdn_norm_g, v_gdn_norm_g),
         (lower_bounds, m_lower_bounds, v_lower_bounds), (hgrn_norm_g, m_hgrn_norm_g, v_hgrn_norm_g),
         (ln_g, m_ln_g, v_ln_g), (ln_b, m_ln_b, v_ln_b)],
        _all_gather(behind(small_g, o_ine[0]), "ag_small_grads"))

    def leaves(kind):
        s_alog, s_dt, s_gg, s_lb, s_hg, s_lng, s_lnb = (o[kind] for o in o_small)
        t = o_taps[kind]
        return [o_ine[kind], t[:, :ka, :ca], t[:, :kb, ca:], s_alog, s_dt, s_gg, o_oute[kind],
                o_ino[kind], s_lb, s_hg, o_outo[kind], s_lng, s_lnb, o_pl[kind], o_gate[kind]]

    return (loss, gr["grad_x"][None], *leaves(0), *leaves(1), *leaves(2), *leaves(3))
```

```python
import functools

import jax
import jax.numpy as jnp
from jax import lax
from jax.experimental import pallas as pl
from jax.experimental.pallas import tpu as pltpu

F32 = jnp.float32
BF16 = jnp.bfloat16
MESH = pl.DeviceIdType.MESH

LANES = 128
SUBLANES = 8
HEAD = 128
GDN_CHUNK = 128
HGRN_CHUNK = 64
HGRN_SUB = 16
HGRN_HEADS_PER_STEP = 16
NORM_EPS = 1e-5
DEPTH = 2
ALPHA = (2.0 * DEPTH) ** 0.25
EXP_CLAMP = 80.0
ADAM_LR, ADAM_B1, ADAM_B2, ADAM_EPS, ADAM_WD, ADAM_STEP = 0.001, 0.9, 0.999, 1e-08, 0.01, 10
VMEM_LIMIT = 56 * 1024 * 1024
MATMUL_VMEM = 36 * 1024 * 1024
ROW_TILE = 1024
MIXER_LANES = 256
CONV_LANES = 512
POST_TILE = 512

_NOBATCH, _BATCH0 = ((), ()), ((0,), (0,))
_DIMS = {"nn": (((1,), (0,)), _NOBATCH), "nt": (((1,), (1,)), _NOBATCH), "tn": (((0,), (0,)), _NOBATCH),
         "bnn": (((2,), (1,)), _BATCH0), "bnt": (((2,), (2,)), _BATCH0), "btn": (((1,), (1,)), _BATCH0)}


def _params(**kw):
    return pltpu.CompilerParams(vmem_limit_bytes=VMEM_LIMIT, **kw)


def _dot_raw(a, b, kind, hi):
    if hi:
        return lax.dot_general(a, b, _DIMS[kind], precision=lax.Precision.HIGHEST, preferred_element_type=F32)
    return lax.dot_general(a.astype(BF16), b.astype(BF16), _DIMS[kind], preferred_element_type=F32)


@functools.partial(jax.custom_vjp, nondiff_argnums=(2, 3))
def mdot(a, b, kind, hi):
    return _dot_raw(a, b, kind, hi)


def _mdot_fwd(a, b, kind, hi):
    return _dot_raw(a, b, kind, hi), (a, b)


def _mdot_bwd(kind, hi, res, g):
    a, b = res
    pre, base = kind[:-2], kind[-2:]
    if base == "nn":
        return _dot_raw(g, b, pre + "nt", hi), _dot_raw(a, g, pre + "tn", hi)
    if base == "nt":
        return _dot_raw(g, b, pre + "nn", hi), _dot_raw(g, a, pre + "tn", hi)
    return _dot_raw(b, g, pre + "nt", hi), _dot_raw(a, g, pre + "nn", hi)


mdot.defvjp(_mdot_fwd, _mdot_bwd)


def _rows(x, lo, hi):
    return _take_rows(x, lo, hi, x.shape[-2])


@functools.partial(jax.custom_vjp, nondiff_argnums=(1, 2, 3))
def _take_rows(x, lo, hi, n):
    return x[..., lo:hi, :]


def _take_rows_fwd(x, lo, hi, n):
    return x[..., lo:hi, :], None


def _take_rows_bwd(lo, hi, n, _, g):
    parts = []
    if lo > 0:
        parts.append(jnp.zeros(g.shape[:-2] + (lo, g.shape[-1]), g.dtype))
    parts.append(g)
    if n - hi > 0:
        parts.append(jnp.zeros(g.shape[:-2] + (n - hi, g.shape[-1]), g.dtype))
    return (jnp.concatenate(parts, axis=-2) if len(parts) > 1 else g,)


_take_rows.defvjp(_take_rows_fwd, _take_rows_bwd)


def _heads_of(wide, nheads):
    return jnp.stack([wide[:, h * HEAD:(h + 1) * HEAD] for h in range(nheads)], axis=0)


def _wide_of(x):
    return jnp.concatenate([x[h] for h in range(x.shape[0])], axis=1)


@functools.partial(jax.custom_vjp, nondiff_argnums=(1,))
def to_heads(wide, nheads):
    return _heads_of(wide, nheads)


to_heads.defvjp(lambda wide, nheads: (_heads_of(wide, nheads), None), lambda nheads, _, g: (_wide_of(g),))


@jax.custom_vjp
def to_wide(x):
    return _wide_of(x)


to_wide.defvjp(lambda x: (_wide_of(x), None), lambda _, g: (_heads_of(g, g.shape[1] // HEAD),))


def _sigmoid(x):
    return jax.nn.sigmoid(x)


def _silu(x):
    return x * _sigmoid(x)


def _dsilu(x):
    s = _sigmoid(x)
    return s * (1.0 + x * (1.0 - s))


def _log1p(u):
    return jnp.where(u < 1e-4, u * (1.0 - 0.5 * u), jnp.log(1.0 + u))


def _softplus(x):
    return jnp.maximum(x, 0.0) + _log1p(jnp.exp(-jnp.abs(x)))


def _rms_gate(o, gain, z):
    return o * lax.rsqrt(jnp.mean(o * o, axis=-1, keepdims=True) + NORM_EPS) * gain * _silu(z)


def _l2n(x):
    return x * lax.rsqrt(jnp.sum(x * x, axis=-1, keepdims=True) + 1e-6)


def _split_dot_raw(m, x, kind):
    mb = m.astype(BF16)
    hi = x.astype(BF16)
    lo = (x - hi.astype(F32)).astype(BF16)
    dims = _DIMS[kind]
    return (lax.dot_general(mb, hi, dims, preferred_element_type=F32)
            + lax.dot_general(mb, lo, dims, preferred_element_type=F32))


@jax.custom_vjp
def mask_dot(m, x):
    return _split_dot_raw(m, x, "nn")


def _mask_dot_fwd(m, x):
    return _split_dot_raw(m, x, "nn"), m


def _mask_dot_bwd(m, g):
    return jnp.zeros_like(m), _split_dot_raw(m, g, "tn")


mask_dot.defvjp(_mask_dot_fwd, _mask_dot_bwd)


def _neumann_rest(low):
    n = low.shape[-1]
    rest = -low
    power = low
    span = 2
    while span < n:
        power = _dot_raw(power, power, "bnn", False)
        rest = rest + power + _dot_raw(rest, power, "bnn", False)
        span *= 2
    return rest


@jax.custom_vjp
def _unit_lower_inverse_minus_eye(low):
    return _neumann_rest(low)


def _inverse_fwd(low):
    rest = _neumann_rest(low)
    return rest, rest


def _inverse_bwd(rest, g):
    left = g + _dot_raw(rest, g, "btn", False)
    return (-(left + _dot_raw(left, rest, "bnt", False)),)


_unit_lower_inverse_minus_eye.defvjp(_inverse_fwd, _inverse_bwd)


def _gdn_step(S, q, k, v, z, small, alog, dtb, gain):
    H = S.shape[0]
    C = GDN_CHUNK
    row = lax.broadcasted_iota(jnp.int32, (C, C), 0)
    col = lax.broadcasted_iota(jnp.int32, (C, C), 1)
    tril, strict, eye = (row >= col)[None], (row > col)[None], (row == col)[None]
    head = lax.broadcasted_iota(jnp.int32, (H, 1, LANES), 0)
    lane = lax.broadcasted_iota(jnp.int32, (H, 1, LANES), 2)
    rowc = lax.broadcasted_iota(jnp.int32, (1, C, 1), 1)
    beta_all = _sigmoid(small)
    g_all = -jnp.exp(alog) * _softplus(small + dtb)
    gc_all = mask_dot((row >= col).astype(F32), g_all)
    beta = jnp.sum(jnp.where(lane == head, beta_all[None], 0.0), axis=-1, keepdims=True)
    gc = jnp.sum(jnp.where(lane == head + H, gc_all[None], 0.0), axis=-1, keepdims=True)
    gc_row = jnp.sum(jnp.where(eye, gc, 0.0), axis=1, keepdims=True)
    decay = jnp.where(tril, jnp.exp(jnp.where(tril, gc - gc_row, 0.0)), 0.0)
    g_last = jnp.sum(jnp.where(rowc == C - 1, gc, 0.0), axis=1, keepdims=True)
    qn = _l2n(q) * (HEAD ** -0.5)
    kn = _l2n(k)
    kb = kn * beta
    low = jnp.where(strict, beta * mdot(kn, kn, "bnt", False) * decay, 0.0)
    inv_rest = _unit_lower_inverse_minus_eye(low)
    eg = jnp.exp(gc)
    vb, kbe = v * beta, kb * eg
    u = vb + mdot(inv_rest, vb, "bnn", False)
    w = kbe + mdot(inv_rest, kbe, "bnn", False)
    attn = mdot(qn, kn, "bnt", False) * decay
    v_new = u - mdot(w, S, "bnn", False)
    o = mdot(qn * eg, S, "bnn", False) + mdot(attn, v_new, "bnn", False)
    k_dec = kn * jnp.exp(g_last - gc)
    return _rms_gate(o, gain, z), S * jnp.exp(g_last) + mdot(k_dec, v_new, "btn", False)


def _hgrn_step(St, qr, fr, vi, z, lb0, lb1, gain):
    H = St.shape[0]
    C, SB = HGRN_CHUNK, HGRN_SUB
    row = lax.broadcasted_iota(jnp.int32, (C, C), 0)
    col = lax.broadcasted_iota(jnp.int32, (C, C), 1)
    blk_start = row - (row & (SB - 1))
    in_blk_f = ((row >= col) & (col >= blk_start)).astype(F32)
    before_f = (col < blk_start).astype(F32)
    sums_f = jnp.concatenate([in_blk_f, before_f], axis=0)
    m = jnp.maximum(lb0, lb1)
    e0, e1 = jnp.exp(lb0 - m), jnp.exp(lb1 - m)
    lb = e1 / (e0 + e1)
    f = lb + (1.0 - lb) * _sigmoid(fr)
    q = _silu(qr)
    k = 1.0 - f
    logf = jnp.log(f)
    sums = mask_dot(sums_f, to_wide(logf))
    inner, start = to_heads(_rows(sums, 0, C), H), to_heads(_rows(sums, C, 2 * C), H)
    b = start + inner
    b_last = jnp.sum(logf, axis=1, keepdims=True)
    o = mdot(q * jnp.exp(b), St, "bnt", False)
    qt = q * jnp.exp(inner)
    parts = []
    for blk in range(C // SB):
        lo, n = blk * SB, (blk + 1) * SB
        ref = jnp.sum(_rows(start, lo, n), axis=1, keepdims=True) * (1.0 / SB)
        kt = k * jnp.exp(jnp.minimum(ref - b, EXP_CLAMP))
        part = mdot(_rows(qt, lo, n), kt, "bnt", False)
        t_idx = lax.broadcasted_iota(jnp.int32, (1, SB, C), 1) + lo
        s_idx = lax.broadcasted_iota(jnp.int32, (1, SB, C), 2)
        parts.append(jnp.where(s_idx <= t_idx, part, 0.0))
    o = o + mdot(jnp.concatenate(parts, axis=1), vi, "bnn", False)
    k_dec = k * jnp.exp(b_last - b)
    return _rms_gate(o, gain, z), St * jnp.exp(b_last) + mdot(vi, k_dec, "btn", False)


def _post_norm(s, x, g, b):
    r = ALPHA * x + s
    d = r - jnp.mean(r, axis=-1, keepdims=True)
    var = jnp.mean(d * d, axis=-1, keepdims=True)
    return d * lax.rsqrt(var + NORM_EPS) * g + b


def _post_gate(x1, gate_pre, pp):
    return x1 + pp * _sigmoid(gate_pre)


def _pick(dim, cands):
    for c in cands:
        if dim % c == 0:
            return c
    return dim


def _matmul_tiles(M, K, tn, a_bytes, b_bytes, has_add):
    for tk in (4096, 2048, 1536, 1152, 1024, 640, 512, 384, 256, 128):
        if K % tk:
            continue
        for tm in (2048, 1152, 1024, 512, 384, 256, 128):
            if M % tm:
                continue
            blocks = tm * tk * a_bytes + tk * tn * b_bytes + tm * tn * 4 * (2 if has_add else 1)
            if 2 * blocks + (tm * tn * 4 if tk < K else 0) <= MATMUL_VMEM and tm >= min(M, 1024):
                return tm, tk
    return _pick(M, (512, 256, 128)), _pick(K, (512, 256, 128))


def _matmul(a, b, kind, name, add=None, after=None, also_bf16=False):
    if kind == "nn":
        (M, K), N = a.shape, b.shape[1]
    elif kind == "nt":
        (M, K), N = a.shape, b.shape[0]
    else:
        (K, M), N = a.shape, b.shape[1]
    has_add = add is not None
    tn = _pick(N, (512, 640, 384, 256, 128))
    tm, tk = _matmul_tiles(M, K, tn, a.dtype.itemsize, b.dtype.itemsize, has_add)
    nk = K // tk
    a_spec = pl.BlockSpec((tk, tm), lambda i, j, k: (k, i)) if kind == "tn" else pl.BlockSpec((tm, tk), lambda i, j, k: (i, k))
    b_spec = pl.BlockSpec((tn, tk), lambda i, j, k: (j, k)) if kind == "nt" else pl.BlockSpec((tk, tn), lambda i, j, k: (k, j))
    o_spec = pl.BlockSpec((tm, tn), lambda i, j, k: (i, j))

    extra = ([add] if has_add else []) + ([after] if after is not None else [])
    extra_specs = ([o_spec] if has_add else []) + ([pl.BlockSpec(TOKEN_SHAPE, lambda i, j, k: (0, 0))] if after is not None else [])

    out_dtypes = (F32, BF16) if also_bf16 else (F32,)

    def body(a_ref, b_ref, *rest):
        outs = rest[len(extra):len(extra) + len(out_dtypes)]

        def write(val):
            if has_add:
                val = val + rest[0][...]
            for o_ref in outs:
                o_ref[...] = val.astype(o_ref.dtype)

        part = _dot_raw(a_ref[...], b_ref[...], kind, False)
        if nk == 1:
            write(part)
            return
        acc = rest[-1]
        kk = pl.program_id(2)

        @pl.when(kk == 0)
        def _():
            acc[...] = part

        @pl.when(kk > 0)
        def _():
            acc[...] += part

        @pl.when(kk == nk - 1)
        def _():
            write(acc[...])

    result = pl.pallas_call(
        body, name=name, grid=(M // tm, N // tn, nk),
        in_specs=[a_spec, b_spec] + extra_specs,
        out_specs=[o_spec] * len(out_dtypes), out_shape=[jax.ShapeDtypeStruct((M, N), dt) for dt in out_dtypes],
        scratch_shapes=[pltpu.VMEM((tm, tn), F32)] if nk > 1 else [],
        compiler_params=_params(dimension_semantics=("parallel", "parallel", "arbitrary")),
    )(a, b, *extra)
    return result if also_bf16 else result[0]


def _halo_specs(ts, nt, width, prev=True, main=True, nxt=True):
    per = ts // SUBLANES
    last8 = nt * per - 1
    specs = []
    if prev:
        specs.append(pl.BlockSpec((SUBLANES, width), lambda cb, i: (jnp.maximum(i * per - 1, 0), cb)))
    if main:
        specs.append(pl.BlockSpec((ts, width), lambda cb, i: (i, cb)))
    if nxt:
        specs.append(pl.BlockSpec((SUBLANES, width), lambda cb, i: (jnp.minimum((i + 1) * per, last8), cb)))
    return specs


def _taps(ext, ktaps, lo, size):
    return [ext[lo:lo + size] if j == 0 else pltpu.roll(ext, j, 0)[lo:lo + size] for j in range(ktaps)]


def _ahead(ext, j, size):
    n = ext.shape[0]
    return ext[:size] if j == 0 else pltpu.roll(ext, n - j, 0)[:size]


def _lane_block(ref, k):
    return ref[:, k * MIXER_LANES:(k + 1) * MIXER_LANES]


def _mixer_a_fwd(proj_a, conv_w):
    T = proj_a.shape[0]
    nblk = proj_a.shape[1] // (4 * MIXER_LANES)
    ts = min(ROW_TILE, T)
    nt = T // ts

    def body(pp, pm, w_ref, y_ref):
        i = pl.program_id(1)
        u_prev = jnp.where(i > 0, _lane_block(pp, 0) * _lane_block(pp, 1), 0.0)
        ext = jnp.concatenate([u_prev, _lane_block(pm, 0) * _lane_block(pm, 1)], axis=0)
        t0, t1, t2 = _taps(ext, 3, SUBLANES, ts)
        cv = w_ref[2:3, :] * t0 + w_ref[1:2, :] * t1 + w_ref[0:1, :] * t2
        y_ref[...] = (_lane_block(pm, 2) * cv * _silu(_lane_block(pm, 3))).astype(y_ref.dtype)

    return pl.pallas_call(
        body, name="mixer_a_fwd", grid=(nblk, nt),
        in_specs=_halo_specs(ts, nt, 4 * MIXER_LANES, nxt=False)
        + [pl.BlockSpec((conv_w.shape[0], MIXER_LANES), lambda cb, i: (0, cb))],
        out_specs=pl.BlockSpec((ts, MIXER_LANES), lambda cb, i: (i, cb)),
        out_shape=jax.ShapeDtypeStruct((T, nblk * MIXER_LANES), BF16), compiler_params=_params(),
    )(proj_a, proj_a, conv_w)


def _mixer_a_bwd(proj_a, dy, conv_w):
    T = proj_a.shape[0]
    nblk = proj_a.shape[1] // (4 * MIXER_LANES)
    ts = min(ROW_TILE, T)
    nt = T // ts
    kt = conv_w.shape[0]

    def body(pp, pm, pn, dym, dyn, w_ref, dp_ref, dw_ref):
        i = pl.program_id(1)
        hm, cm, bm, zm = (_lane_block(pm, k) for k in range(4))
        u_prev = jnp.where(i > 0, _lane_block(pp, 0) * _lane_block(pp, 1), 0.0)
        ext = jnp.concatenate([u_prev, hm * cm], axis=0)
        dy_ext = jnp.concatenate([dym[...], jnp.where(i < nt - 1, dyn[...], 0.0)], axis=0)
        b_ext = jnp.concatenate([bm, _lane_block(pn, 2)], axis=0)
        sz_ext = _silu(jnp.concatenate([zm, _lane_block(pn, 3)], axis=0))
        dcv_ext = dy_ext * b_ext * sz_ext
        w = [w_ref[j:j + 1, :] for j in range(kt)]
        du = sum(w[kt - 1 - j] * _ahead(dcv_ext, j, ts) for j in range(kt))
        taps = _taps(ext, kt, SUBLANES, ts)
        cv = sum(w[kt - 1 - j] * taps[j] for j in range(kt))
        for part, d in enumerate((du * cm, du * hm, dym[...] * cv * sz_ext[:ts], dym[...] * bm * cv * _dsilu(zm))):
            dp_ref[:, part * MIXER_LANES:(part + 1) * MIXER_LANES] = d.astype(dp_ref.dtype)
        dcv = dcv_ext[:ts]

        @pl.when(i == 0)
        def _():
            dw_ref[...] = jnp.zeros_like(dw_ref)

        for j in range(kt):
            dw_ref[j:j + 1, :] += jnp.sum(dcv * taps[kt - 1 - j], axis=0, keepdims=True)

    return pl.pallas_call(
        body, name="mixer_a_bwd", grid=(nblk, nt),
        in_specs=_halo_specs(ts, nt, 4 * MIXER_LANES) + _halo_specs(ts, nt, MIXER_LANES, prev=False)
        + [pl.BlockSpec((kt, MIXER_LANES), lambda cb, i: (0, cb))],
        out_specs=[pl.BlockSpec((ts, 4 * MIXER_LANES), lambda cb, i: (i, cb)),
                   pl.BlockSpec((SUBLANES, MIXER_LANES), lambda cb, i: (0, cb))],
        out_shape=[jax.ShapeDtypeStruct(proj_a.shape, BF16),
                   jax.ShapeDtypeStruct((SUBLANES, nblk * MIXER_LANES), F32)],
        compiler_params=_params(),
    )(proj_a, proj_a, proj_a, dy, dy, conv_w)


def _conv_b_fwd(raw, conv_w):
    T = raw.shape[0]
    nblk = raw.shape[1] // CONV_LANES
    ts = min(ROW_TILE, T)
    nt = T // ts
    kt = conv_w.shape[0]

    def body(rp, rm, w_ref, y_ref):
        i = pl.program_id(1)
        ext = jnp.concatenate([jnp.where(i > 0, rp[...], 0.0), rm[...]], axis=0)
        taps = _taps(ext, kt, SUBLANES, ts)
        y_ref[...] = _silu(sum(w_ref[kt - 1 - j:kt - j, :] * taps[j] for j in range(kt)))

    return pl.pallas_call(
        body, name="conv_b_fwd", grid=(nblk, nt),
        in_specs=_halo_specs(ts, nt, CONV_LANES, nxt=False) + [pl.BlockSpec((kt, CONV_LANES), lambda cb, i: (0, cb))],
        out_specs=pl.BlockSpec((ts, CONV_LANES), lambda cb, i: (i, cb)),
        out_shape=jax.ShapeDtypeStruct(raw.shape, F32), compiler_params=_params(),
    )(raw, raw, conv_w)


def _conv_b_bwd(raw, dy, conv_w):
    T = raw.shape[0]
    nblk = raw.shape[1] // CONV_LANES
    ts = min(ROW_TILE, T)
    nt = T // ts
    kt = conv_w.shape[0]

    def body(rp, rm, rn, dym, dyn, w_ref, dr_ref, dw_ref):
        i = pl.program_id(1)
        ext = jnp.concatenate([jnp.where(i > 0, rp[...], 0.0), rm[...], rn[...]], axis=0)
        w = [w_ref[j:j + 1, :] for j in range(kt)]
        taps = _taps(ext, kt, SUBLANES, ts + SUBLANES)
        xc_ext = sum(w[kt - 1 - j] * taps[j] for j in range(kt))
        dy_ext = jnp.concatenate([dym[...], jnp.where(i < nt - 1, dyn[...], 0.0)], axis=0)
        dxc_ext = dy_ext * _dsilu(xc_ext)
        dr_ref[...] = sum(w[kt - 1 - j] * _ahead(dxc_ext, j, ts) for j in range(kt)).astype(dr_ref.dtype)
        dxc = dxc_ext[:ts]

        @pl.when(i == 0)
        def _():
            dw_ref[...] = jnp.zeros_like(dw_ref)

        for j in range(kt):
            dw_ref[j:j + 1, :] += jnp.sum(dxc * taps[kt - 1 - j][:ts], axis=0, keepdims=True)

    return pl.pallas_call(
        body, name="conv_b_bwd", grid=(nblk, nt),
        in_specs=_halo_specs(ts, nt, CONV_LANES) + _halo_specs(ts, nt, CONV_LANES, prev=False)
        + [pl.BlockSpec((kt, CONV_LANES), lambda cb, i: (0, cb))],
        out_specs=[pl.BlockSpec((ts, CONV_LANES), lambda cb, i: (i, cb)),
                   pl.BlockSpec((SUBLANES, CONV_LANES), lambda cb, i: (0, cb))],
        out_shape=[jax.ShapeDtypeStruct(raw.shape, BF16), jax.ShapeDtypeStruct((SUBLANES, nblk * CONV_LANES), F32)],
        compiler_params=_params(),
    )(raw, raw, raw, dy, dy, conv_w)


def _split_heads(ref, base, nheads, rows=slice(None)):
    return jnp.stack([ref[rows, base + h * HEAD: base + (h + 1) * HEAD] for h in range(nheads)], axis=0)


def _store_heads(ref, base, x, rows=slice(None), accumulate=False):
    for h in range(x.shape[0]):
        lanes = slice(base + h * HEAD, base + (h + 1) * HEAD)
        if accumulate:
            ref[rows, lanes] += x[h]
        else:
            ref[rows, lanes] = x[h].astype(ref.dtype)


def _gdn_fwd(qkv, proj_zs, alog, dtb, gain, H):
    T = qkv.shape[0]
    C, HW = GDN_CHUNK, H * HEAD
    nc = T // C
    zw = HW + LANES

    def body(qkv_ref, zs_ref, alog_ref, dtb_ref, gain_ref, o_ref, sall_ref, s_scr):
        @pl.when(pl.program_id(0) == 0)
        def _():
            s_scr[...] = jnp.zeros_like(s_scr)

        sall_ref[0] = s_scr[...]
        outs, states = _gdn_step(
            s_scr[...], _split_heads(qkv_ref, 0, H), _split_heads(qkv_ref, HW, H),
            _split_heads(qkv_ref, 2 * HW, H), _split_heads(zs_ref, 0, H), zs_ref[:, HW:HW + LANES],
            alog_ref[...], dtb_ref[...], gain_ref[...])
        _store_heads(o_ref, 0, outs)
        s_scr[...] = states

    row = pl.BlockSpec((1, LANES), lambda i: (0, 0))
    return pl.pallas_call(
        body, name="gdn_fwd", grid=(nc,),
        in_specs=[pl.BlockSpec((C, 3 * HW), lambda i: (i, 0)), pl.BlockSpec((C, zw), lambda i: (i, 0)), row, row, row],
        out_specs=[pl.BlockSpec((C, HW), lambda i: (i, 0)), pl.BlockSpec((1, H, HEAD, HEAD), lambda i: (i, 0, 0, 0))],
        out_shape=[jax.ShapeDtypeStruct((T, HW), BF16), jax.ShapeDtypeStruct((nc, H, HEAD, HEAD), F32)],
        scratch_shapes=[pltpu.VMEM((H, HEAD, HEAD), F32)], compiler_params=_params(),
    )(qkv, proj_zs, alog, dtb, gain)


def _gdn_bwd(qkv, proj_zs, do, s_all, alog, dtb, gain, H):
    T = qkv.shape[0]
    C, HW = GDN_CHUNK, H * HEAD
    nc = T // C
    zw = HW + LANES

    def body(qkv_ref, zs_ref, do_ref, sin_ref, alog_ref, dtb_ref, gain_ref,
             dqkv_ref, dzs_ref, dalog_ref, ddtb_ref, dgain_ref, ds_scr):
        @pl.when(pl.program_id(0) == 0)
        def _():
            ds_scr[...] = jnp.zeros_like(ds_scr)
            dalog_ref[...] = jnp.zeros_like(dalog_ref)
            ddtb_ref[...] = jnp.zeros_like(ddtb_ref)
            dgain_ref[...] = jnp.zeros_like(dgain_ref)

        primals = (sin_ref[0], _split_heads(qkv_ref, 0, H),
                   _split_heads(qkv_ref, HW, H), _split_heads(qkv_ref, 2 * HW, H), _split_heads(zs_ref, 0, H),
                   zs_ref[:, HW:HW + LANES], alog_ref[...], dtb_ref[...], gain_ref[...])
        _, vjp = jax.vjp(_gdn_step, *primals)
        dS, dq, dk, dv, dz, dsmall, dalog, ddtb, dgain = vjp((_split_heads(do_ref, 0, H), ds_scr[...]))
        ds_scr[...] = dS
        _store_heads(dqkv_ref, 0, dq)
        _store_heads(dqkv_ref, HW, dk)
        _store_heads(dqkv_ref, 2 * HW, dv)
        _store_heads(dzs_ref, 0, dz)
        dzs_ref[:, HW:HW + LANES] = dsmall.astype(dzs_ref.dtype)
        dalog_ref[...] += dalog
        ddtb_ref[...] += ddtb
        dgain_ref[...] += dgain

    row = pl.BlockSpec((1, LANES), lambda i: (0, 0))
    rev = lambda i: nc - 1 - i
    return pl.pallas_call(
        body, name="gdn_bwd", grid=(nc,),
        in_specs=[pl.BlockSpec((C, 3 * HW), lambda i: (rev(i), 0)), pl.BlockSpec((C, zw), lambda i: (rev(i), 0)),
                  pl.BlockSpec((C, HW), lambda i: (rev(i), 0)),
                  pl.BlockSpec((1, H, HEAD, HEAD), lambda i: (rev(i), 0, 0, 0)), row, row, row],
        out_specs=[pl.BlockSpec((C, 3 * HW), lambda i: (rev(i), 0)), pl.BlockSpec((C, zw), lambda i: (rev(i), 0)),
                   row, row, row],
        out_shape=[jax.ShapeDtypeStruct(qkv.shape, F32), jax.ShapeDtypeStruct(proj_zs.shape, BF16)]
        + [jax.ShapeDtypeStruct((1, LANES), F32)] * 3,
        scratch_shapes=[pltpu.VMEM((H, HEAD, HEAD), F32)], compiler_params=_params(),
    )(qkv, proj_zs, do, s_all, alog, dtb, gain)


def _hgrn_refs(proj_ref, lb_ref, HP):
    W = HP * HEAD
    return (_split_heads(proj_ref, 0, HP), _split_heads(proj_ref, W, HP), _split_heads(proj_ref, 2 * W, HP),
            _split_heads(proj_ref, 3 * W, HP), _split_heads(lb_ref, 0, HP, slice(0, 1)),
            _split_heads(lb_ref, 0, HP, slice(1, 2)))


def _hgrn_fwd(proj, lower_bounds, gain, nheads):
    T = proj.shape[0]
    C, HP = HGRN_CHUNK, HGRN_HEADS_PER_STEP
    ng, nc, W = nheads // HP, T // C, HP * HEAD

    def body(proj_ref, lb_ref, gain_ref, o_ref, sall_ref, s_scr):
        @pl.when(pl.program_id(1) == 0)
        def _():
            s_scr[...] = jnp.zeros_like(s_scr)

        sall_ref[0] = s_scr[...]
        qr, fr, vi, z, lb0, lb1 = _hgrn_refs(proj_ref, lb_ref, HP)
        outs, states = _hgrn_step(s_scr[...], qr, fr, vi, z, lb0, lb1, gain_ref[...])
        _store_heads(o_ref, 0, outs)
        s_scr[...] = states

    return pl.pallas_call(
        body, name="hgrn_fwd", grid=(ng, nc),
        in_specs=[pl.BlockSpec((C, 4 * W), lambda g, i: (i, g)), pl.BlockSpec((2, W), lambda g, i: (0, g)),
                  pl.BlockSpec((1, LANES), lambda g, i: (0, 0))],
        out_specs=[pl.BlockSpec((C, W), lambda g, i: (i, g)),
                   pl.BlockSpec((1, HP, HEAD, HEAD), lambda g, i: (i, g, 0, 0))],
        out_shape=[jax.ShapeDtypeStruct((T, nheads * HEAD), BF16), jax.ShapeDtypeStruct((nc, nheads, HEAD, HEAD), F32)],
        scratch_shapes=[pltpu.VMEM((HP, HEAD, HEAD), F32)], compiler_params=_params(),
    )(proj, lower_bounds, gain)


def _hgrn_bwd(proj, do, s_all, lower_bounds, gain, nheads):
    T = proj.shape[0]
    C, HP = HGRN_CHUNK, HGRN_HEADS_PER_STEP
    ng, nc, W = nheads // HP, T // C, HP * HEAD

    def body(proj_ref, do_ref, sin_ref, lb_ref, gain_ref, dproj_ref, dlb_ref, dgain_ref, ds_scr):
        first = pl.program_id(1) == 0

        @pl.when(first)
        def _():
            ds_scr[...] = jnp.zeros_like(ds_scr)
            dlb_ref[...] = jnp.zeros_like(dlb_ref)

        @pl.when(first & (pl.program_id(0) == 0))
        def _():
            dgain_ref[...] = jnp.zeros_like(dgain_ref)

        qr, fr, vi, z, lb0, lb1 = _hgrn_refs(proj_ref, lb_ref, HP)
        primals = (sin_ref[0], qr, fr, vi, z, lb0, lb1, gain_ref[...])
        _, vjp = jax.vjp(_hgrn_step, *primals)
        dS, dq, df, dv, dz, dlb0, dlb1, dgain = vjp((_split_heads(do_ref, 0, HP), ds_scr[...]))
        ds_scr[...] = dS
        for part, d in enumerate((dq, df, dv, dz)):
            _store_heads(dproj_ref, part * W, d)
        _store_heads(dlb_ref, 0, dlb0, slice(0, 1), accumulate=True)
        _store_heads(dlb_ref, 0, dlb1, slice(1, 2), accumulate=True)
        dgain_ref[...] += dgain

    rev = lambda i: nc - 1 - i
    return pl.pallas_call(
        body, name="hgrn_bwd", grid=(ng, nc),
        in_specs=[pl.BlockSpec((C, 4 * W), lambda g, i: (rev(i), g)), pl.BlockSpec((C, W), lambda g, i: (rev(i), g)),
                  pl.BlockSpec((1, HP, HEAD, HEAD), lambda g, i: (rev(i), g, 0, 0)),
                  pl.BlockSpec((2, W), lambda g, i: (0, g)), pl.BlockSpec((1, LANES), lambda g, i: (0, 0))],
        out_specs=[pl.BlockSpec((C, 4 * W), lambda g, i: (rev(i), g)), pl.BlockSpec((2, W), lambda g, i: (0, g)),
                   pl.BlockSpec((1, LANES), lambda g, i: (0, 0))],
        out_shape=[jax.ShapeDtypeStruct(proj.shape, BF16), jax.ShapeDtypeStruct(lower_bounds.shape, F32),
                   jax.ShapeDtypeStruct((1, LANES), F32)],
        scratch_shapes=[pltpu.VMEM((HP, HEAD, HEAD), F32)], compiler_params=_params(),
    )(proj, do, s_all, lower_bounds, gain)


def _post_specs(T):
    tr = min(POST_TILE, T)
    tile = lambda w: pl.BlockSpec((tr, w), lambda i: (i, 0))
    full = lambda r, w: pl.BlockSpec((r, w), lambda i: (0, 0))
    return tr, tile, full


def _post_fwd(s, x, p, g, b, wg, wpl, name):
    T, D = x.shape
    P = p.shape[1]
    tr, tile, full = _post_specs(T)

    def body(s_ref, x_ref, p_ref, g_ref, b_ref, wg_ref, wpl_ref, o_ref, o16_ref):
        x1 = _post_norm(s_ref[...], x_ref[...], g_ref[...], b_ref[...])
        xn = _post_gate(x1, _dot_raw(x1, wg_ref[...], "nn", False), _dot_raw(p_ref[...], wpl_ref[...], "nn", False))
        o_ref[...] = xn
        o16_ref[...] = xn.astype(BF16)

    return pl.pallas_call(
        body, name=name, grid=(T // tr,),
        in_specs=[tile(D), tile(D), tile(P), full(1, D), full(1, D), full(D, D), full(P, D)],
        out_specs=[tile(D), tile(D)],
        out_shape=[jax.ShapeDtypeStruct((T, D), F32), jax.ShapeDtypeStruct((T, D), BF16)], compiler_params=_params(),
    )(s, x, p, g, b, wg, wpl)


def _post_bwd(s, x, p, g, b, wg, wpl, dnext, name, with_loss):
    T, D = x.shape
    P = p.shape[1]
    tr, tile, full = _post_specs(T)

    def body(s_ref, x_ref, p_ref, g_ref, b_ref, wg_ref, wpl_ref, dn_ref,
             ds_ref, dx_ref, dg_ref, db_ref, dwg_ref, dwpl_ref, loss_ref):
        @pl.when(pl.program_id(0) == 0)
        def _():
            for r in (dg_ref, db_ref, dwg_ref, dwpl_ref, loss_ref):
                r[...] = jnp.zeros_like(r)

        x1, vjp_norm = jax.vjp(_post_norm, s_ref[...], x_ref[...], g_ref[...], b_ref[...])
        gate_pre = _dot_raw(x1, wg_ref[...], "nn", False)
        pp = _dot_raw(p_ref[...], wpl_ref[...], "nn", False)
        xn, vjp_gate = jax.vjp(_post_gate, x1, gate_pre, pp)
        if with_loss:
            err = xn - dn_ref[...]
            loss_ref[...] += 0.5 * jnp.sum(jnp.sum(err * err, axis=-1, keepdims=True), axis=0, keepdims=True) / D
            dn = err / D
        else:
            dn = dn_ref[...]
        dx1, dgp, dpp = vjp_gate(dn)
        dwg_ref[...] += _dot_raw(x1, dgp, "tn", False)
        dwpl_ref[...] += _dot_raw(p_ref[...], dpp, "tn", False)
        dx1 = dx1 + _dot_raw(dgp, wg_ref[...], "nt", False)
        ds, dx, dg, db = vjp_norm(dx1)
        ds_ref[...] = ds.astype(ds_ref.dtype)
        dx_ref[...] = dx
        dg_ref[...] += dg
        db_ref[...] += db

    return pl.pallas_call(
        body, name=name, grid=(T // tr,),
        in_specs=[tile(D), tile(D), tile(P), full(1, D), full(1, D), full(D, D), full(P, D), tile(D)],
        out_specs=[tile(D), tile(D), full(1, D), full(1, D), full(D, D), full(P, D), full(SUBLANES, LANES)],
        out_shape=[jax.ShapeDtypeStruct((T, D), BF16), jax.ShapeDtypeStruct((T, D), F32)]
        + [jax.ShapeDtypeStruct((1, D), F32)] * 2
        + [jax.ShapeDtypeStruct((D, D), F32), jax.ShapeDtypeStruct((P, D), F32),
           jax.ShapeDtypeStruct((SUBLANES, LANES), F32)],
        compiler_params=_params(),
    )(s, x, p, g, b, wg, wpl, dnext)


def _adam_math(w, g, m, v):
    m = ADAM_B1 * m + (1.0 - ADAM_B1) * g
    v = ADAM_B2 * v + (1.0 - ADAM_B2) * (g * g)
    m_hat = m / (1.0 - ADAM_B1 ** ADAM_STEP)
    v_hat = v / (1.0 - ADAM_B2 ** ADAM_STEP)
    return -ADAM_LR * (m_hat / (jnp.sqrt(v_hat) + ADAM_EPS) + ADAM_WD * w), m, v


def _shard_tiles(R, C):
    tr = _pick(R, (256, 128, 64, 32, 16, 8))
    return (tr, C) if tr < R or R % SUBLANES == 0 else (R, _pick(C, (256, 128)))


def _adam_sharded(w, m, v, g8, got, me, name):
    rows_apart = w.shape[1] == 1 and w.shape[0] > 1
    (L, R, C) = (1, w.shape[0], w.shape[2]) if rows_apart else w.shape
    tr, tc = (R, LANES) if rows_apart else _shard_tiles(R, C)
    nr, nc = R // tr, C // tc
    side_by_side = g8.ndim == 2

    def body(me_ref, w_ref, m_ref, v_ref, p_ref, *rest):
        got_refs, (g_ref, d_ref, mo_ref, vo_ref) = rest[:7], rest[7:]
        g = p_ref[...] if side_by_side else p_ref[0]
        for r in got_refs:
            g = g + r[0].astype(F32)
        if rows_apart:
            d, mn, vn = _adam_math(w_ref[:, 0, :], g, m_ref[:, 0, :], v_ref[:, 0, :])
            for ref, val in ((g_ref, g), (d_ref, d), (mo_ref, mn), (vo_ref, vn)):
                ref[:, 0, :] = val
            return
        d, mn, vn = _adam_math(w_ref[0], g, m_ref[0], v_ref[0])
        g_ref[0] = g
        d_ref[0] = d
        mo_ref[0] = mn
        vo_ref[0] = vn

    if rows_apart:
        t3 = pl.BlockSpec((tr, 1, tc), lambda l, i, j, q: (i, 0, j))
    else:
        t3 = pl.BlockSpec((1, tr, tc), lambda l, i, j, q: (l, i, j))
    slot = lambda k: pl.BlockSpec((1, tr, tc), lambda l, i, j, q: (k, l * nr + i, j))
    if side_by_side:
        mine = pl.BlockSpec((tr, tc), lambda l, i, j, q: (l * nr + i, q[0] * nc + j))
    else:
        mine = pl.BlockSpec((1, tr, tc), lambda l, i, j, q: (q[0], l * nr + i, j))
    return pl.pallas_call(
        body, name=name,
        grid_spec=pltpu.PrefetchScalarGridSpec(
            num_scalar_prefetch=1, grid=(L, nr, nc),
            in_specs=[t3, t3, t3, mine] + [slot(k) for k in range(7)], out_specs=[t3, t3, t3, t3]),
        out_shape=[jax.ShapeDtypeStruct(w.shape, F32)] * 4, compiler_params=_params(),
    )(me, w, m, v, g8, *([got] * 7))


def _small_rows(shapes):
    offsets, r = [], 0
    for rows, _ in shapes:
        offsets.append(r)
        r += rows
    return offsets, -(-(r + 1) // SUBLANES) * SUBLANES, max(cols for _, cols in shapes)


def _pack_small_grads(grads, loss):
    offsets, total, width = _small_rows([g.shape for g in grads])
    packed = jnp.zeros((total, width), F32)
    for g, r in zip(grads, offsets):
        packed = lax.dynamic_update_slice(packed, g, (r, 0))
    return lax.dynamic_update_slice(packed, loss.reshape(1, 1), (total - 1, 0))


def _adam_replicated(params, g8):
    shapes = [w.shape for w, _, _ in params]
    offsets, total, width = _small_rows(shapes)
    n = len(params)

    def body(*refs):
        g_ref, outs, loss_ref, g_scr = refs[3 * n], refs[3 * n + 1:7 * n + 1], refs[7 * n + 1], refs[7 * n + 2]
        g = g_ref[0]
        for k in range(1, 8):
            g = g + g_ref[k]
        g_scr[...] = g
        for i, (rows, cols) in enumerate(shapes):
            gp = g_scr[offsets[i]:offsets[i] + rows, 0:cols]
            d, mn, vn = _adam_math(refs[3 * i][...], gp, refs[3 * i + 1][...], refs[3 * i + 2][...])
            for ref, val in zip(outs[4 * i:4 * i + 4], (gp, d, mn, vn)):
                ref[...] = val
        loss_ref[...] = g_scr[total - 1:total, 0:LANES]

    out = pl.pallas_call(
        body, name="adam_replicated",
        out_shape=[jax.ShapeDtypeStruct(shp, F32) for shp in shapes for _ in range(4)]
        + [jax.ShapeDtypeStruct((1, LANES), F32)],
        scratch_shapes=[pltpu.VMEM((total, width), F32)], compiler_params=_params(),
    )(*[a for triple in params for a in triple], g8)
    return [out[4 * i:4 * i + 4] for i in range(n)], out[4 * n][0, 0]


def _place():
    return lax.axis_index("x"), lax.axis_index("y"), lax.axis_index("c")


def _all_gather(shard, name):
    def body(x_ref, out_ref, send_sems, recv_sems, local_sem):
        x, y, c = _place()
        me, sibling = (x, y, c), (x, y, 1 - c)
        chips = [(1 - x, y), (x, 1 - y), (1 - x, 1 - y)]

        def slab(px, py, pc):
            return out_ref.at[4 * px + 2 * py + pc]

        def copy(k, block, to, src=None):
            return pltpu.make_async_remote_copy(
                src_ref=slab(*block) if src is None else src, dst_ref=slab(*block),
                send_sem=send_sems.at[k], recv_sem=recv_sems.at[k], device_id=to, device_id_type=MESH)

        mine = pltpu.make_async_copy(x_ref, slab(*me), local_sem)
        mine.start()
        first = [copy(0, me, sibling, src=x_ref)]
        first += [copy(1 + j, me, (*chip, c), src=x_ref) for j, chip in enumerate(chips)]
        for cp in first:
            cp.start()
        passed = [copy(4 + j, (*chip, c), sibling) for j, chip in enumerate(chips)]
        for j, chip in enumerate(chips):
            copy(1 + j, (*chip, c), me).wait_recv()
            passed[j].start()
        copy(0, sibling, me).wait_recv()
        for j, chip in enumerate(chips):
            copy(4 + j, (*chip, 1 - c), me).wait_recv()
        for cp in first + passed:
            cp.wait_send()
        mine.wait()

    return pl.pallas_call(
        body, name=name, out_shape=jax.ShapeDtypeStruct((8,) + shard.shape, shard.dtype),
        in_specs=[pl.BlockSpec(memory_space=pl.ANY)], out_specs=pl.BlockSpec(memory_space=pl.ANY),
        scratch_shapes=[pltpu.SemaphoreType.DMA((7,)), pltpu.SemaphoreType.DMA((7,)), pltpu.SemaphoreType.DMA],
    )(shard)


_HBM = pl.BlockSpec(memory_space=pltpu.HBM)
_SEM = pl.BlockSpec(memory_space=pltpu.SEMAPHORE)
_DATAFLOW = pltpu.SideEffectType.DATAFLOW_SIDE_EFFECTING
TOKEN_SHAPE = (SUBLANES, LANES)


def _peers(x, y, c):
    flip = lambda v, bit: 1 - v if bit else v
    return [(flip(x, r >> 2 & 1), flip(y, r >> 1 & 1), flip(c, r & 1)) for r in range(1, 8)]


def _scatter_plan(x, y, c):
    return [(4 * px + 2 * py + pc, k, (px, py, pc)) for k, (px, py, pc) in enumerate(_peers(x, y, c))]


def _exchange_copies(plan, src_ref, land_ref, send_sems, recv_sems):
    C = land_ref.shape[-1]
    block = (lambda b: src_ref.at[b]) if len(src_ref.shape) == len(land_ref.shape) else (
        lambda b: src_ref.at[:, pl.ds(b * C, C)])
    return [pltpu.make_async_remote_copy(
        src_ref=block(blk), dst_ref=land_ref.at[slot], send_sem=send_sems.at[k], recv_sem=recv_sems.at[k],
        device_id=peer, device_id_type=MESH) for k, (blk, slot, peer) in enumerate(plan(*_place()))]


def _exchange_start(src, n_slots, block_shape, plan, name):
    land_shape = (n_slots,) + tuple(block_shape)
    n = len(plan(0, 0, 0))

    def body(src_ref, land_ref, send_sems, recv_sems, src_thru, land_thru, token):
        for cp in _exchange_copies(plan, src_ref, land_ref, send_sems, recv_sems):
            cp.start()
        token[...] = jnp.zeros_like(token)

    return pl.pallas_call(
        body, name=name,
        out_shape=(pltpu.SemaphoreType.DMA((n,)), pltpu.SemaphoreType.DMA((n,)), pltpu.HBM(src.shape, src.dtype),
                   pltpu.HBM(land_shape, src.dtype), jax.ShapeDtypeStruct(TOKEN_SHAPE, F32)),
        in_specs=(_HBM, _HBM), out_specs=(_SEM, _SEM, _HBM, _HBM, pl.BlockSpec(memory_space=pltpu.VMEM)),
        input_output_aliases={0: 2, 1: 3}, compiler_params=pltpu.CompilerParams(has_side_effects=_DATAFLOW),
    )(pltpu.with_memory_space_constraint(src, pltpu.HBM),
      pltpu.with_memory_space_constraint(lax.empty(land_shape, src.dtype), pltpu.HBM))


def _exchange_wait(handle, plan, after, name):
    send_sems, recv_sems, src_thru, land_thru, _ = handle

    def body(src_ref, land_ref, send_sems, recv_sems, after_ref, src_dead, got_ref):
        for cp in _exchange_copies(plan, src_ref, land_ref, send_sems, recv_sems):
            cp.wait_send()
            cp.wait_recv()

    return pl.pallas_call(
        body, name=name,
        out_shape=(pltpu.HBM(src_thru.shape, src_thru.dtype), pltpu.HBM(land_thru.shape, land_thru.dtype)),
        in_specs=(_HBM, _HBM, _SEM, _SEM, pl.BlockSpec(memory_space=pl.ANY)), out_specs=(_HBM, _HBM),
        input_output_aliases={0: 0, 1: 1}, compiler_params=pltpu.CompilerParams(has_side_effects=_DATAFLOW),
    )(src_thru, land_thru, send_sems, recv_sems, after)[1]


def _gather_plan(x, y, c):
    return [(0, 4 * x + 2 * y + c, peer) for peer in _peers(x, y, c)]


class _LateGather:
    def __init__(self, shard, name):
        self.shard, self.name = shard, name
        self.handle = _exchange_start(shard[None], 8, shard.shape, _gather_plan, name + "_start")

    def get(self, after):
        land = _exchange_wait(self.handle, _gather_plan, after, self.name + "_wait")
        x, y, c = _place()
        return lax.dynamic_update_slice(land, self.shard[None], (4 * x + 2 * y + c, 0, 0))


class _GradExchange:
    def __init__(self, me, layouts):
        self.me, self.layouts, self.pending = me, layouts, {}

    def start(self, tag, grad, grad16=None):
        g8 = self.layouts[tag](grad)
        g16 = g8.astype(BF16) if grad16 is None else self.layouts[tag](grad16)
        block_shape = g8.shape[1:] if g8.ndim == 3 else (g8.shape[0], g8.shape[1] // 8)
        handle = _exchange_start(g16, 7, block_shape, _scatter_plan, "rs_start_" + tag)
        self.pending[tag] = (g8, handle)
        return handle[4]

    def finish(self, tag, w, m, v, after):
        g8, handle = self.pending.pop(tag)
        got = _exchange_wait(handle, _scatter_plan, after, "rs_wait_" + tag)
        return _adam_sharded(w, m, v, g8, got, self.me, "adam_" + tag)


def _local_grads(x, p0, p1, target, wt_zs, wt_a, wt_qkv, late, conv_a, conv_b,
                 a_log, dt_bias, gdn_gain, lower_bounds, hgrn_gain, ln_g, ln_b, on_grad=None):
    H = a_log.shape[1]
    pad_small = ((0, 0), (H, LANES - 2 * H))
    alog_row = jnp.pad(a_log, pad_small)
    dtb_row = jnp.pad(dt_bias, pad_small)

    x16 = x.astype(BF16)
    proj_zs = _matmul(x16, wt_zs, "nt", "proj_even_zs", after=late.started)
    proj_a = _matmul(x16, wt_a, "nt", "proj_even_a", after=late.started)
    proj_qkv = _matmul(x16, wt_qkv, "nt", "proj_even_qkv", after=late.started)
    y_a = _mixer_a_fwd(proj_a, conv_a)
    qkv = _conv_b_fwd(proj_qkv, conv_b)
    o2, s_gdn = _gdn_fwd(qkv, proj_zs, alog_row, dtb_row, gdn_gain, H)
    woute_a, woute_b = late.out_even(o2)
    wg, wpl = late.gate(o2)
    s_e = _matmul(o2, woute_b, "nn", "out_even_b", add=_matmul(y_a, woute_a, "nn", "out_even_a"))
    x2, x2_16 = _post_fwd(s_e, x, p0, ln_g[0:1], ln_b[0:1], wg[0], wpl[0], "post_even_fwd")
    wino, wouto = late.odd(s_e)
    nheads_o = wouto.shape[0] // HEAD
    proj_o = _matmul(x2_16, wino, "nn", "proj_odd")
    o4, s_hgrn = _hgrn_fwd(proj_o, lower_bounds, hgrn_gain, nheads_o)
    s_o = _matmul(o4, wouto, "nn", "out_odd")
    ds_o, dx2, dlng1, dlnb1, dwg1, dwpl1, loss = _post_bwd(
        s_o, x2, p1, ln_g[1:2], ln_b[1:2], wg[1], wpl[1], target, "post_odd_loss_bwd", True)
    do4 = _matmul(ds_o, wouto, "nt", "d_out_odd_act")
    grads = {}

    def emit(tag, grad, grad16=None):
        grads[tag] = grad
        return on_grad(tag, grad, grad16) if on_grad is not None else jnp.zeros(TOKEN_SHAPE, F32)

    tok = emit("w_out_odd", *_matmul(o4, ds_o, "tn", "d_out_odd_w", also_bf16=True))
    dproj_o, dlb, dhgain = _hgrn_bwd(proj_o, do4, s_hgrn, lower_bounds, hgrn_gain + tok[0:1], nheads_o)
    dx2 = _matmul(dproj_o, wino, "nt", "d_proj_odd_act", add=dx2)
    tok = emit("w_in_odd", *_matmul(x2_16, dproj_o, "tn", "d_proj_odd_w", also_bf16=True))
    ds_e, dx, dlng0, dlnb0, dwg0, dwpl0, _ = _post_bwd(
        s_e, x, p0, ln_g[0:1], ln_b[0:1] + tok[0:1, 0:1], wg[0], wpl[0], dx2, "post_even_bwd", False)
    tok = emit("w_pl_gate", jnp.stack([dwg0, dwg1])) + emit("w_pl", jnp.stack([dwpl0, dwpl1]))
    dy_a = _matmul(ds_e, woute_a, "nt", "d_out_even_a_act")
    do2 = _matmul(ds_e, woute_b, "nt", "d_out_even_b_act")
    dwoute_a = _matmul(y_a, ds_e, "tn", "d_out_even_a_w")
    dwoute_b = _matmul(o2, ds_e, "tn", "d_out_even_b_w")
    tok = tok + emit("w_out_even", jnp.concatenate([dwoute_a, dwoute_b], axis=0))
    dqkv, dproj_zs, dalog, ddtb, dggain = _gdn_bwd(qkv, proj_zs, do2, s_gdn, alog_row, dtb_row, gdn_gain + tok[0:1], H)
    dproj_qkv, dconv_b = _conv_b_bwd(proj_qkv, dqkv, conv_b)
    dproj_a, dconv_a = _mixer_a_bwd(proj_a, dy_a, conv_a)
    emit("conv", (dconv_a[:conv_a.shape[0]], dconv_b[:conv_b.shape[0]]))
    tok = emit("w_in_even", (_matmul(dproj_zs, x16, "tn", "d_proj_even_zs_w"), _matmul(dproj_a, x16, "tn", "d_proj_even_a_w"),
                             _matmul(dproj_qkv, x16, "tn", "d_proj_even_qkv_w")))
    dx = _matmul(dproj_zs, wt_zs, "nn", "d_proj_even_zs_act", add=dx, after=tok)
    dx = _matmul(dproj_a, wt_a, "nn", "d_proj_even_a_act", add=dx)
    dx = _matmul(dproj_qkv, wt_qkv, "nn", "d_proj_even_qkv_act", add=dx)
    grads.update(
        loss=loss[0, 0], grad_x=dx, a_log=dalog[:, H:2 * H], dt_bias=ddtb[:, H:2 * H], gdn_gain=dggain,
        lower_bounds=dlb, hgrn_gain=dhgain, ln_g=jnp.concatenate([dlng0, dlng1], axis=0),
        ln_b=jnp.concatenate([dlnb0, dlnb1], axis=0))
    return grads


def _pad_rows(a, rows):
    return jnp.pad(a, ((0, rows - a.shape[0]), (0, 0)))


def _split_in_even(wt_full, AW, HW, H):
    D = wt_full.shape[1]
    n_a = 4 * AW
    n_main = n_a + 3 * HW
    wt_zs = jnp.concatenate([wt_full[n_main:n_main + HW], wt_full[n_main + HW:],
                             jnp.zeros((LANES - 2 * H, D), wt_full.dtype)], axis=0)
    wt_a = wt_full[:n_a].reshape(4, AW // MIXER_LANES, MIXER_LANES, D).transpose(1, 0, 2, 3).reshape(n_a, D)
    return wt_zs, wt_a, wt_full[n_a:n_main]


def _join_in_even(dt_zs, dt_a, dt_qkv, AW, HW, H):
    D = dt_a.shape[1]
    a_nat = dt_a.reshape(AW // MIXER_LANES, 4, MIXER_LANES, D).transpose(1, 0, 2, 3).reshape(4 * AW, D)
    return jnp.concatenate([a_nat, dt_qkv, dt_zs[:HW], dt_zs[HW:HW + 2 * H]], axis=0)


def kernel(x, p, w_in_even, conv_a_w, conv_b_w, a_log, dt_bias, gdn_norm_g, w_out_even, w_in_odd, lower_bounds, hgrn_norm_g, w_out_odd, ln_g, ln_b, w_pl, w_pl_gate, loss_target, m_w_in_even, m_conv_a_w, m_conv_b_w, m_a_log, m_dt_bias, m_gdn_norm_g, m_w_out_even, m_w_in_odd, m_lower_bounds, m_hgrn_norm_g, m_w_out_odd, m_ln_g, m_ln_b, m_w_pl, m_w_pl_gate, v_w_in_even, v_conv_a_w, v_conv_b_w, v_a_log, v_dt_bias, v_gdn_norm_g, v_w_out_even, v_w_in_odd, v_lower_bounds, v_hgrn_norm_g, v_w_out_odd, v_ln_g, v_ln_b, v_w_pl, v_w_pl_gate):
    xi, yi, ci = _place()
    me = jnp.reshape(4 * xi + 2 * yi + ci, (1,)).astype(jnp.int32)
    D = x.shape[2]
    H = a_log.shape[1]
    HW = H * HEAD
    AW = conv_a_w.shape[2] * 8
    OW = w_out_odd.shape[1] * 8
    PD = w_pl.shape[1]
    ka, kb = conv_a_w.shape[1], conv_b_w.shape[1]
    ca, cb = conv_a_w.shape[2], conv_b_w.shape[2]
    gw = HGRN_HEADS_PER_STEP * HEAD
    ngrp = OW // gw

    transposed = lambda a: jnp.transpose(a, (0, 2, 1))
    g_ine = _all_gather(transposed(w_in_even)[0].astype(BF16), "ag_w_in_even")
    wt_zs, wt_a, wt_qkv = _split_in_even(g_ine.reshape(-1, D), AW, HW, H)
    taps = jnp.concatenate([_pad_rows(conv_a_w[0], SUBLANES), _pad_rows(conv_b_w[0], SUBLANES)], axis=1)
    g_taps = _all_gather(taps, "ag_conv")
    conv_a = jnp.transpose(g_taps[:, :ka, :ca], (1, 0, 2)).reshape(ka, 8 * ca)
    conv_b = jnp.transpose(g_taps[:, :kb, ca:], (1, 0, 2)).reshape(kb, 8 * cb)
    behind = lambda shard, dep: lax.optimization_barrier((shard, dep))[0]
    late_oute = _LateGather(behind(w_out_even[0].astype(BF16), (g_ine, g_taps)), "ag_w_out_even")
    late_gate = _LateGather(behind(w_pl_gate.astype(BF16).reshape(-1, D), late_oute.handle[4]), "ag_w_pl_gate")
    late_pl = _LateGather(behind(w_pl.astype(BF16).reshape(DEPTH * PD, -1), late_gate.handle[4]), "ag_w_pl")
    late_ino = _LateGather(behind(w_in_odd[0].astype(BF16), late_pl.handle[4]), "ag_w_in_odd")
    late_outo = _LateGather(behind(w_out_odd[0].astype(BF16), late_ino.handle[4]), "ag_w_out_odd")

    class _Late:
        started = sum(g.handle[4] for g in (late_oute, late_gate, late_pl, late_ino, late_outo))

        @staticmethod
        def out_even(after):
            woute = late_oute.get(after).reshape(-1, D)
            return woute[:AW], woute[AW:]

        @staticmethod
        def gate(after):
            g_gate, g_pl = late_gate.get(after), late_pl.get(after)
            return (g_gate.reshape(8, DEPTH, D // 8, D).transpose(1, 0, 2, 3).reshape(DEPTH, D, D),
                    g_pl.reshape(8, DEPTH, PD, D // 8).transpose(1, 2, 0, 3).reshape(DEPTH, PD, D))

        @staticmethod
        def odd(after):
            g_ino = late_ino.get(after)
            wino = jnp.transpose(g_ino, (1, 0, 2)).reshape(D, 4, ngrp, gw).transpose(0, 2, 1, 3).reshape(D, 4 * OW)
            return wino, late_outo.get(after).reshape(-1, D)

    sh = w_in_even.shape[2]
    tap_blocks = lambda g, width: _pad_rows(g, SUBLANES).reshape(SUBLANES, 8, width).transpose(1, 0, 2)
    owner_layout = {
        "w_in_even": lambda g: _join_in_even(*g, AW, HW, H).reshape(8, sh, D),
        "w_in_odd": lambda g: g.reshape(D, ngrp, 4, gw).transpose(0, 2, 1, 3).reshape(D, 4 * OW),
        "w_out_even": lambda g: g.reshape(8, -1, D),
        "w_out_odd": lambda g: g.reshape(8, -1, D),
        "w_pl_gate": lambda g: g.reshape(DEPTH, 8, D // 8, D).transpose(1, 0, 2, 3).reshape(8, DEPTH * D // 8, D),
        "w_pl": lambda g: g.reshape(DEPTH, PD, 8, D // 8).transpose(2, 0, 1, 3).reshape(8, DEPTH * PD, D // 8),
        "conv": lambda g: jnp.concatenate([tap_blocks(g[0], ca), tap_blocks(g[1], cb)], axis=2),
    }
    exchange = _GradExchange(me, owner_layout)
    gr = _local_grads(x[0], p[0, 0], p[1, 0], loss_target[0], wt_zs, wt_a, wt_qkv, _Late, conv_a, conv_b,
                      a_log, dt_bias, gdn_norm_g, lower_bounds, hgrn_norm_g, ln_g, ln_b, on_grad=exchange.start)

    last = gr["grad_x"]
    pack_taps = lambda a, b: jnp.concatenate([_pad_rows(a[0], SUBLANES), _pad_rows(b[0], SUBLANES)], axis=1)[None]
    o_outo = exchange.finish("w_out_odd", w_out_odd, m_w_out_odd, v_w_out_odd, last)
    o_ino = exchange.finish("w_in_odd", w_in_odd, m_w_in_odd, v_w_in_odd, last)
    o_gate = exchange.finish("w_pl_gate", w_pl_gate, m_w_pl_gate, v_w_pl_gate, last)
    o_pl = exchange.finish("w_pl", w_pl, m_w_pl, v_w_pl, last)
    o_oute = exchange.finish("w_out_even", w_out_even, m_w_out_even, v_w_out_even, last)
    o_taps = exchange.finish("conv", taps[None], pack_taps(m_conv_a_w, m_conv_b_w), pack_taps(v_conv_a_w, v_conv_b_w), last)
    others_done = sum(o[1][0, 0:1, 0:1] for o in (o_outo, o_ino, o_gate, o_pl, o_oute, o_taps))
    rows_first = lambda a: jnp.transpose(a, (2, 0, 1))
    o_ine = [jnp.transpose(o, (1, 2, 0)) for o in exchange.finish(
        "w_in_even", rows_first(w_in_even), rows_first(m_w_in_even), rows_first(v_w_in_even), others_done)]

    small_g = _pack_small_grads([gr["a_log"], gr["dt_bias"], gr["gdn_gain"], gr["lower_bounds"], gr["hgrn_gain"],
                                 gr["ln_g"], gr["ln_b"]], gr["loss"])
    o_small, loss = _adam_replicated(
        [(a_log, m_a_log, v_a_log), (dt_bias, m_dt_bias, v_dt_bias), (gdn_norm_g, m_gdn_norm_g, v_gdn_norm_g),
         (lower_bounds, m_lower_bounds, v_lower_bounds), (hgrn_norm_g, m_hgrn_norm_g, v_hgrn_norm_g),
         (ln_g, m_ln_g, v_ln_g), (ln_b, m_ln_b, v_ln_b)],
        _all_gather(behind(small_g, o_ine[0]), "ag_small_grads"))

    def leaves(kind):
        s_alog, s_dt, s_gg, s_lb, s_hg, s_lng, s_lnb = (o[kind] for o in o_small)
        t = o_taps[kind]
        return [o_ine[kind], t[:, :ka, :ca], t[:, :kb, ca:], s_alog, s_dt, s_gg, o_oute[kind],
                o_ino[kind], s_lb, s_hg, o_outo[kind], s_lng, s_lnb, o_pl[kind], o_gate[kind]]

    return (loss, gr["grad_x"][None], *leaves(0), *leaves(1), *leaves(2), *leaves(3))
```

```python
import functools

import jax
import jax.numpy as jnp
from jax import lax
from jax.experimental import pallas as pl
from jax.experimental.pallas import tpu as pltpu

F32 = jnp.float32
BF16 = jnp.bfloat16
MESH = pl.DeviceIdType.MESH

LANES = 128
SUBLANES = 8
HEAD = 128
GDN_CHUNK = 128
HGRN_CHUNK = 64
HGRN_SUB = 16
HGRN_HEADS_PER_STEP = 16
NORM_EPS = 1e-5
DEPTH = 2
ALPHA = (2.0 * DEPTH) ** 0.25
EXP_CLAMP = 80.0
ADAM_LR, ADAM_B1, ADAM_B2, ADAM_EPS, ADAM_WD, ADAM_STEP = 0.001, 0.9, 0.999, 1e-08, 0.01, 10
VMEM_LIMIT = 56 * 1024 * 1024
MATMUL_VMEM = 36 * 1024 * 1024
ROW_TILE = 1024
MIXER_LANES = 256
CONV_LANES = 512
POST_TILE = 512

_NOBATCH, _BATCH0 = ((), ()), ((0,), (0,))
_DIMS = {"nn": (((1,), (0,)), _NOBATCH), "nt": (((1,), (1,)), _NOBATCH), "tn": (((0,), (0,)), _NOBATCH),
         "bnn": (((2,), (1,)), _BATCH0), "bnt": (((2,), (2,)), _BATCH0), "btn": (((1,), (1,)), _BATCH0)}


def _params(**kw):
    return pltpu.CompilerParams(vmem_limit_bytes=VMEM_LIMIT, **kw)


def _dot_raw(a, b, kind, hi):
    if hi:
        return lax.dot_general(a, b, _DIMS[kind], precision=lax.Precision.HIGHEST, preferred_element_type=F32)
    return lax.dot_general(a.astype(BF16), b.astype(BF16), _DIMS[kind], preferred_element_type=F32)


@functools.partial(jax.custom_vjp, nondiff_argnums=(2, 3))
def mdot(a, b, kind, hi):
    return _dot_raw(a, b, kind, hi)


def _mdot_fwd(a, b, kind, hi):
    return _dot_raw(a, b, kind, hi), (a, b)


def _mdot_bwd(kind, hi, res, g):
    a, b = res
    pre, base = kind[:-2], kind[-2:]
    if base == "nn":
        return _dot_raw(g, b, pre + "nt", hi), _dot_raw(a, g, pre + "tn", hi)
    if base == "nt":
        return _dot_raw(g, b, pre + "nn", hi), _dot_raw(g, a, pre + "tn", hi)
    return _dot_raw(b, g, pre + "nt", hi), _dot_raw(a, g, pre + "nn", hi)


mdot.defvjp(_mdot_fwd, _mdot_bwd)


def _rows(x, lo, hi):
    return _take_rows(x, lo, hi, x.shape[-2])


@functools.partial(jax.custom_vjp, nondiff_argnums=(1, 2, 3))
def _take_rows(x, lo, hi, n):
    return x[..., lo:hi, :]


def _take_rows_fwd(x, lo, hi, n):
    return x[..., lo:hi, :], None


def _take_rows_bwd(lo, hi, n, _, g):
    parts = []
    if lo > 0:
        parts.append(jnp.zeros(g.shape[:-2] + (lo, g.shape[-1]), g.dtype))
    parts.append(g)
    if n - hi > 0:
        parts.append(jnp.zeros(g.shape[:-2] + (n - hi, g.shape[-1]), g.dtype))
    return (jnp.concatenate(parts, axis=-2) if len(parts) > 1 else g,)


_take_rows.defvjp(_take_rows_fwd, _take_rows_bwd)


def _heads_of(wide, nheads):
    return jnp.stack([wide[:, h * HEAD:(h + 1) * HEAD] for h in range(nheads)], axis=0)


def _wide_of(x):
    return jnp.concatenate([x[h] for h in range(x.shape[0])], axis=1)


@functools.partial(jax.custom_vjp, nondiff_argnums=(1,))
def to_heads(wide, nheads):
    return _heads_of(wide, nheads)


to_heads.defvjp(lambda wide, nheads: (_heads_of(wide, nheads), None), lambda nheads, _, g: (_wide_of(g),))


@jax.custom_vjp
def to_wide(x):
    return _wide_of(x)


to_wide.defvjp(lambda x: (_wide_of(x), None), lambda _, g: (_heads_of(g, g.shape[1] // HEAD),))


def _sigmoid(x):
    return jax.nn.sigmoid(x)


def _silu(x):
    return x * _sigmoid(x)


def _dsilu(x):
    s = _sigmoid(x)
    return s * (1.0 + x * (1.0 - s))


def _log1p(u):
    return jnp.where(u < 1e-4, u * (1.0 - 0.5 * u), jnp.log(1.0 + u))


def _softplus(x):
    return jnp.maximum(x, 0.0) + _log1p(jnp.exp(-jnp.abs(x)))


def _rms_gate(o, gain, z):
    return o * lax.rsqrt(jnp.mean(o * o, axis=-1, keepdims=True) + NORM_EPS) * gain * _silu(z)


def _l2n(x):
    return x * lax.rsqrt(jnp.sum(x * x, axis=-1, keepdims=True) + 1e-6)


def _split_dot_raw(m, x, kind):
    mb = m.astype(BF16)
    hi = x.astype(BF16)
    lo = (x - hi.astype(F32)).astype(BF16)
    dims = _DIMS[kind]
    return (lax.dot_general(mb, hi, dims, preferred_element_type=F32)
            + lax.dot_general(mb, lo, dims, preferred_element_type=F32))


@jax.custom_vjp
def mask_dot(m, x):
    return _split_dot_raw(m, x, "nn")


def _mask_dot_fwd(m, x):
    return _split_dot_raw(m, x, "nn"), m


def _mask_dot_bwd(m, g):
    return jnp.zeros_like(m), _split_dot_raw(m, g, "tn")


mask_dot.defvjp(_mask_dot_fwd, _mask_dot_bwd)


def _neumann_rest(low):
    n = low.shape[-1]
    rest = -low
    power = low
    span = 2
    while span < n:
        power = _dot_raw(power, power, "bnn", False)
        rest = rest + power + _dot_raw(rest, power, "bnn", False)
        span *= 2
    return rest


@jax.custom_vjp
def _unit_lower_inverse_minus_eye(low):
    return _neumann_rest(low)


def _inverse_fwd(low):
    rest = _neumann_rest(low)
    return rest, rest


def _inverse_bwd(rest, g):
    left = g + _dot_raw(rest, g, "btn", False)
    return (-(left + _dot_raw(left, rest, "bnt", False)),)


_unit_lower_inverse_minus_eye.defvjp(_inverse_fwd, _inverse_bwd)


def _gdn_step(S, q, k, v, z, small, alog, dtb, gain):
    H = S.shape[0]
    C = GDN_CHUNK
    row = lax.broadcasted_iota(jnp.int32, (C, C), 0)
    col = lax.broadcasted_iota(jnp.int32, (C, C), 1)
    tril, strict, eye = (row >= col)[None], (row > col)[None], (row == col)[None]
    head = lax.broadcasted_iota(jnp.int32, (H, 1, LANES), 0)
    lane = lax.broadcasted_iota(jnp.int32, (H, 1, LANES), 2)
    rowc = lax.broadcasted_iota(jnp.int32, (1, C, 1), 1)
    beta_all = _sigmoid(small)
    g_all = -jnp.exp(alog) * _softplus(small + dtb)
    gc_all = mask_dot((row >= col).astype(F32), g_all)
    beta = jnp.sum(jnp.where(lane == head, beta_all[None], 0.0), axis=-1, keepdims=True)
    gc = jnp.sum(jnp.where(lane == head + H, gc_all[None], 0.0), axis=-1, keepdims=True)
    gc_row = jnp.sum(jnp.where(eye, gc, 0.0), axis=1, keepdims=True)
    decay = jnp.where(tril, jnp.exp(jnp.where(tril, gc - gc_row, 0.0)), 0.0)
    g_last = jnp.sum(jnp.where(rowc == C - 1, gc, 0.0), axis=1, keepdims=True)
    qn = _l2n(q) * (HEAD ** -0.5)
    kn = _l2n(k)
    kb = kn * beta
    low = jnp.where(strict, beta * mdot(kn, kn, "bnt", False) * decay, 0.0)
    inv_rest = _unit_lower_inverse_minus_eye(low)
    eg = jnp.exp(gc)
    vb, kbe = v * beta, kb * eg
    u = vb + mdot(inv_rest, vb, "bnn", False)
    w = kbe + mdot(inv_rest, kbe, "bnn", False)
    attn = mdot(qn, kn, "bnt", False) * decay
    v_new = u - mdot(w, S, "bnn", False)
    o = mdot(qn * eg, S, "bnn", False) + mdot(attn, v_new, "bnn", False)
    k_dec = kn * jnp.exp(g_last - gc)
    return _rms_gate(o, gain, z), S * jnp.exp(g_last) + mdot(k_dec, v_new, "btn", False)


def _hgrn_step(St, qr, fr, vi, z, lb0, lb1, gain):
    H = St.shape[0]
    C, SB = HGRN_CHUNK, HGRN_SUB
    row = lax.broadcasted_iota(jnp.int32, (C, C), 0)
    col = lax.broadcasted_iota(jnp.int32, (C, C), 1)
    blk_start = row - (row & (SB - 1))
    in_blk_f = ((row >= col) & (col >= blk_start)).astype(F32)
    before_f = (col < blk_start).astype(F32)
    sums_f = jnp.concatenate([in_blk_f, before_f], axis=0)
    m = jnp.maximum(lb0, lb1)
    e0, e1 = jnp.exp(lb0 - m), jnp.exp(lb1 - m)
    lb = e1 / (e0 + e1)
    f = lb + (1.0 - lb) * _sigmoid(fr)
    q = _silu(qr)
    k = 1.0 - f
    logf = jnp.log(f)
    sums = mask_dot(sums_f, to_wide(logf))
    inner, start = to_heads(_rows(sums, 0, C), H), to_heads(_rows(sums, C, 2 * C), H)
    b = start + inner
    b_last = jnp.sum(logf, axis=1, keepdims=True)
    o = mdot(q * jnp.exp(b), St, "bnt", False)
    qt = q * jnp.exp(inner)
    parts = []
    for blk in range(C // SB):
        lo, n = blk * SB, (blk + 1) * SB
        ref = jnp.sum(_rows(start, lo, n), axis=1, keepdims=True) * (1.0 / SB)
        kt = k * jnp.exp(jnp.minimum(ref - b, EXP_CLAMP))
        part = mdot(_rows(qt, lo, n), kt, "bnt", False)
        t_idx = lax.broadcasted_iota(jnp.int32, (1, SB, C), 1) + lo
        s_idx = lax.broadcasted_iota(jnp.int32, (1, SB, C), 2)
        parts.append(jnp.where(s_idx <= t_idx, part, 0.0))
    o = o + mdot(jnp.concatenate(parts, axis=1), vi, "bnn", False)
    k_dec = k * jnp.exp(b_last - b)
    return _rms_gate(o, gain, z), St * jnp.exp(b_last) + mdot(vi, k_dec, "btn", False)


def _post_norm(s, x, g, b):
    r = ALPHA * x + s
    d = r - jnp.mean(r, axis=-1, keepdims=True)
    var = jnp.mean(d * d, axis=-1, keepdims=True)
    return d * lax.rsqrt(var + NORM_EPS) * g + b


def _post_gate(x1, gate_pre, pp):
    return x1 + pp * _sigmoid(gate_pre)


def _pick(dim, cands):
    for c in cands:
        if dim % c == 0:
            return c
    return dim


def _matmul_tiles(M, K, tn, a_bytes, b_bytes, has_add):
    for tk in (4096, 2048, 1536, 1152, 1024, 640, 512, 384, 256, 128):
        if K % tk:
            continue
        for tm in (2048, 1152, 1024, 512, 384, 256, 128):
            if M % tm:
                continue
            blocks = tm * tk * a_bytes + tk * tn * b_bytes + tm * tn * 4 * (2 if has_add else 1)
            if 2 * blocks + (tm * tn * 4 if tk < K else 0) <= MATMUL_VMEM and tm >= min(M, 1024):
                return tm, tk
    return _pick(M, (512, 256, 128)), _pick(K, (512, 256, 128))


def _matmul(a, b, kind, name, add=None, after=None, also_bf16=False):
    if kind == "nn":
        (M, K), N = a.shape, b.shape[1]
    elif kind == "nt":
        (M, K), N = a.shape, b.shape[0]
    else:
        (K, M), N = a.shape, b.shape[1]
    has_add = add is not None
    tn = _pick(N, (512, 640, 384, 256, 128))
    tm, tk = _matmul_tiles(M, K, tn, a.dtype.itemsize, b.dtype.itemsize, has_add)
    nk = K // tk
    a_spec = pl.BlockSpec((tk, tm), lambda i, j, k: (k, i)) if kind == "tn" else pl.BlockSpec((tm, tk), lambda i, j, k: (i, k))
    b_spec = pl.BlockSpec((tn, tk), lambda i, j, k: (j, k)) if kind == "nt" else pl.BlockSpec((tk, tn), lambda i, j, k: (k, j))
    o_spec = pl.BlockSpec((tm, tn), lambda i, j, k: (i, j))

    extra = ([add] if has_add else []) + ([after] if after is not None else [])
    extra_specs = ([o_spec] if has_add else []) + ([pl.BlockSpec(TOKEN_SHAPE, lambda i, j, k: (0, 0))] if after is not None else [])

    out_dtypes = (F32, BF16) if also_bf16 else (F32,)

    def body(a_ref, b_ref, *rest):
        outs = rest[len(extra):len(extra) + len(out_dtypes)]

        def write(val):
            if has_add:
                val = val + rest[0][...]
            for o_ref in outs:
                o_ref[...] = val.astype(o_ref.dtype)

        part = _dot_raw(a_ref[...], b_ref[...], kind, False)
        if nk == 1:
            write(part)
            return
        acc = rest[-1]
        kk = pl.program_id(2)

        @pl.when(kk == 0)
        def _():
            acc[...] = part

        @pl.when(kk > 0)
        def _():
            acc[...] += part

        @pl.when(kk == nk - 1)
        def _():
            write(acc[...])

    result = pl.pallas_call(
        body, name=name, grid=(M // tm, N // tn, nk),
        in_specs=[a_spec, b_spec] + extra_specs,
        out_specs=[o_spec] * len(out_dtypes), out_shape=[jax.ShapeDtypeStruct((M, N), dt) for dt in out_dtypes],
        scratch_shapes=[pltpu.VMEM((tm, tn), F32)] if nk > 1 else [],
        compiler_params=_params(dimension_semantics=("parallel", "parallel", "arbitrary")),
    )(a, b, *extra)
    return result if also_bf16 else result[0]


def _halo_specs(ts, nt, width, prev=True, main=True, nxt=True):
    per = ts // SUBLANES
    last8 = nt * per - 1
    specs = []
    if prev:
        specs.append(pl.BlockSpec((SUBLANES, width), lambda cb, i: (jnp.maximum(i * per - 1, 0), cb)))
    if main:
        specs.append(pl.BlockSpec((ts, width), lambda cb, i: (i, cb)))
    if nxt:
        specs.append(pl.BlockSpec((SUBLANES, width), lambda cb, i: (jnp.minimum((i + 1) * per, last8), cb)))
    return specs


def _taps(ext, ktaps, lo, size):
    return [ext[lo:lo + size] if j == 0 else pltpu.roll(ext, j, 0)[lo:lo + size] for j in range(ktaps)]


def _ahead(ext, j, size):
    n = ext.shape[0]
    return ext[:size] if j == 0 else pltpu.roll(ext, n - j, 0)[:size]


def _lane_block(ref, k):
    return ref[:, k * MIXER_LANES:(k + 1) * MIXER_LANES]


def _mixer_a_fwd(proj_a, conv_w):
    T = proj_a.shape[0]
    nblk = proj_a.shape[1] // (4 * MIXER_LANES)
    ts = min(ROW_TILE, T)
    nt = T // ts

    def body(pp, pm, w_ref, y_ref):
        i = pl.program_id(1)
        u_prev = jnp.where(i > 0, _lane_block(pp, 0) * _lane_block(pp, 1), 0.0)
        ext = jnp.concatenate([u_prev, _lane_block(pm, 0) * _lane_block(pm, 1)], axis=0)
        t0, t1, t2 = _taps(ext, 3, SUBLANES, ts)
        cv = w_ref[2:3, :] * t0 + w_ref[1:2, :] * t1 + w_ref[0:1, :] * t2
        y_ref[...] = (_lane_block(pm, 2) * cv * _silu(_lane_block(pm, 3))).astype(y_ref.dtype)

    return pl.pallas_call(
        body, name="mixer_a_fwd", grid=(nblk, nt),
        in_specs=_halo_specs(ts, nt, 4 * MIXER_LANES, nxt=False)
        + [pl.BlockSpec((conv_w.shape[0], MIXER_LANES), lambda cb, i: (0, cb))],
        out_specs=pl.BlockSpec((ts, MIXER_LANES), lambda cb, i: (i, cb)),
        out_shape=jax.ShapeDtypeStruct((T, nblk * MIXER_LANES), BF16), compiler_params=_params(),
    )(proj_a, proj_a, conv_w)


def _mixer_a_bwd(proj_a, dy, conv_w):
    T = proj_a.shape[0]
    nblk = proj_a.shape[1] // (4 * MIXER_LANES)
    ts = min(ROW_TILE, T)
    nt = T // ts
    kt = conv_w.shape[0]

    def body(pp, pm, pn, dym, dyn, w_ref, dp_ref, dw_ref):
        i = pl.program_id(1)
        hm, cm, bm, zm = (_lane_block(pm, k) for k in range(4))
        u_prev = jnp.where(i > 0, _lane_block(pp, 0) * _lane_block(pp, 1), 0.0)
        ext = jnp.concatenate([u_prev, hm * cm], axis=0)
        dy_ext = jnp.concatenate([dym[...], jnp.where(i < nt - 1, dyn[...], 0.0)], axis=0)
        b_ext = jnp.concatenate([bm, _lane_block(pn, 2)], axis=0)
        sz_ext = _silu(jnp.concatenate([zm, _lane_block(pn, 3)], axis=0))
        dcv_ext = dy_ext * b_ext * sz_ext
        w = [w_ref[j:j + 1, :] for j in range(kt)]
        du = sum(w[kt - 1 - j] * _ahead(dcv_ext, j, ts) for j in range(kt))
        taps = _taps(ext, kt, SUBLANES, ts)
        cv = sum(w[kt - 1 - j] * taps[j] for j in range(kt))
        for part, d in enumerate((du * cm, du * hm, dym[...] * cv * sz_ext[:ts], dym[...] * bm * cv * _dsilu(zm))):
            dp_ref[:, part * MIXER_LANES:(part + 1) * MIXER_LANES] = d.astype(dp_ref.dtype)
        dcv = dcv_ext[:ts]

        @pl.when(i == 0)
        def _():
            dw_ref[...] = jnp.zeros_like(dw_ref)

        for j in range(kt):
            dw_ref[j:j + 1, :] += jnp.sum(dcv * taps[kt - 1 - j], axis=0, keepdims=True)

    return pl.pallas_call(
        body, name="mixer_a_bwd", grid=(nblk, nt),
        in_specs=_halo_specs(ts, nt, 4 * MIXER_LANES) + _halo_specs(ts, nt, MIXER_LANES, prev=False)
        + [pl.BlockSpec((kt, MIXER_LANES), lambda cb, i: (0, cb))],
        out_specs=[pl.BlockSpec((ts, 4 * MIXER_LANES), lambda cb, i: (i, cb)),
                   pl.BlockSpec((SUBLANES, MIXER_LANES), lambda cb, i: (0, cb))],
        out_shape=[jax.ShapeDtypeStruct(proj_a.shape, BF16),
                   jax.ShapeDtypeStruct((SUBLANES, nblk * MIXER_LANES), F32)],
        compiler_params=_params(),
    )(proj_a, proj_a, proj_a, dy, dy, conv_w)


def _conv_b_fwd(raw, conv_w):
    T = raw.shape[0]
    nblk = raw.shape[1] // CONV_LANES
    ts = min(ROW_TILE, T)
    nt = T // ts
    kt = conv_w.shape[0]

    def body(rp, rm, w_ref, y_ref):
        i = pl.program_id(1)
        ext = jnp.concatenate([jnp.where(i > 0, rp[...], 0.0), rm[...]], axis=0)
        taps = _taps(ext, kt, SUBLANES, ts)
        y_ref[...] = _silu(sum(w_ref[kt - 1 - j:kt - j, :] * taps[j] for j in range(kt)))

    return pl.pallas_call(
        body, name="conv_b_fwd", grid=(nblk, nt),
        in_specs=_halo_specs(ts, nt, CONV_LANES, nxt=False) + [pl.BlockSpec((kt, CONV_LANES), lambda cb, i: (0, cb))],
        out_specs=pl.BlockSpec((ts, CONV_LANES), lambda cb, i: (i, cb)),
        out_shape=jax.ShapeDtypeStruct(raw.shape, F32), compiler_params=_params(),
    )(raw, raw, conv_w)


def _conv_b_bwd(raw, dy, conv_w):
    T = raw.shape[0]
    nblk = raw.shape[1] // CONV_LANES
    ts = min(ROW_TILE, T)
    nt = T // ts
    kt = conv_w.shape[0]

    def body(rp, rm, rn, dym, dyn, w_ref, dr_ref, dw_ref):
        i = pl.program_id(1)
        ext = jnp.concatenate([jnp.where(i > 0, rp[...], 0.0), rm[...], rn[...]], axis=0)
        w = [w_ref[j:j + 1, :] for j in range(kt)]
        taps = _taps(ext, kt, SUBLANES, ts + SUBLANES)
        xc_ext = sum(w[kt - 1 - j] * taps[j] for j in range(kt))
        dy_ext = jnp.concatenate([dym[...], jnp.where(i < nt - 1, dyn[...], 0.0)], axis=0)
        dxc_ext = dy_ext * _dsilu(xc_ext)
        dr_ref[...] = sum(w[kt - 1 - j] * _ahead(dxc_ext, j, ts) for j in range(kt)).astype(dr_ref.dtype)
        dxc = dxc_ext[:ts]

        @pl.when(i == 0)
        def _():
            dw_ref[...] = jnp.zeros_like(dw_ref)

        for j in range(kt):
            dw_ref[j:j + 1, :] += jnp.sum(dxc * taps[kt - 1 - j][:ts], axis=0, keepdims=True)

    return pl.pallas_call(
        body, name="conv_b_bwd", grid=(nblk, nt),
        in_specs=_halo_specs(ts, nt, CONV_LANES) + _halo_specs(ts, nt, CONV_LANES, prev=False)
        + [pl.BlockSpec((kt, CONV_LANES), lambda cb, i: (0, cb))],
        out_specs=[pl.BlockSpec((ts, CONV_LANES), lambda cb, i: (i, cb)),
                   pl.BlockSpec((SUBLANES, CONV_LANES), lambda cb, i: (0, cb))],
        out_shape=[jax.ShapeDtypeStruct(raw.shape, BF16), jax.ShapeDtypeStruct((SUBLANES, nblk * CONV_LANES), F32)],
        compiler_params=_params(),
    )(raw, raw, raw, dy, dy, conv_w)


def _split_heads(ref, base, nheads, rows=slice(None)):
    return jnp.stack([ref[rows, base + h * HEAD: base + (h + 1) * HEAD] for h in range(nheads)], axis=0)


def _store_heads(ref, base, x, rows=slice(None), accumulate=False):
    for h in range(x.shape[0]):
        lanes = slice(base + h * HEAD, base + (h + 1) * HEAD)
        if accumulate:
            ref[rows, lanes] += x[h]
        else:
            ref[rows, lanes] = x[h].astype(ref.dtype)


def _gdn_fwd(qkv, proj_zs, alog, dtb, gain, H):
    T = qkv.shape[0]
    C, HW = GDN_CHUNK, H * HEAD
    nc = T // C
    zw = HW + LANES

    def body(qkv_ref, zs_ref, alog_ref, dtb_ref, gain_ref, o_ref, sall_ref, s_scr):
        @pl.when(pl.program_id(0) == 0)
        def _():
            s_scr[...] = jnp.zeros_like(s_scr)

        sall_ref[0] = s_scr[...]
        outs, states = _gdn_step(
            s_scr[...], _split_heads(qkv_ref, 0, H), _split_heads(qkv_ref, HW, H),
            _split_heads(qkv_ref, 2 * HW, H), _split_heads(zs_ref, 0, H), zs_ref[:, HW:HW + LANES],
            alog_ref[...], dtb_ref[...], gain_ref[...])
        _store_heads(o_ref, 0, outs)
        s_scr[...] = states

    row = pl.BlockSpec((1, LANES), lambda i: (0, 0))
    return pl.pallas_call(
        body, name="gdn_fwd", grid=(nc,),
        in_specs=[pl.BlockSpec((C, 3 * HW), lambda i: (i, 0)), pl.BlockSpec((C, zw), lambda i: (i, 0)), row, row, row],
        out_specs=[pl.BlockSpec((C, HW), lambda i: (i, 0)), pl.BlockSpec((1, H, HEAD, HEAD), lambda i: (i, 0, 0, 0))],
        out_shape=[jax.ShapeDtypeStruct((T, HW), BF16), jax.ShapeDtypeStruct((nc, H, HEAD, HEAD), F32)],
        scratch_shapes=[pltpu.VMEM((H, HEAD, HEAD), F32)], compiler_params=_params(),
    )(qkv, proj_zs, alog, dtb, gain)


def _gdn_bwd(qkv, proj_zs, do, s_all, alog, dtb, gain, H):
    T = qkv.shape[0]
    C, HW = GDN_CHUNK, H * HEAD
    nc = T // C
    zw = HW + LANES

    def body(qkv_ref, zs_ref, do_ref, sin_ref, alog_ref, dtb_ref, gain_ref,
             dqkv_ref, dzs_ref, dalog_ref, ddtb_ref, dgain_ref, ds_scr):
        @pl.when(pl.program_id(0) == 0)
        def _():
            ds_scr[...] = jnp.zeros_like(ds_scr)
            dalog_ref[...] = jnp.zeros_like(dalog_ref)
            ddtb_ref[...] = jnp.zeros_like(ddtb_ref)
            dgain_ref[...] = jnp.zeros_like(dgain_ref)

        primals = (sin_ref[0], _split_heads(qkv_ref, 0, H),
                   _split_heads(qkv_ref, HW, H), _split_heads(qkv_ref, 2 * HW, H), _split_heads(zs_ref, 0, H),
                   zs_ref[:, HW:HW + LANES], alog_ref[...], dtb_ref[...], gain_ref[...])
        _, vjp = jax.vjp(_gdn_step, *primals)
        dS, dq, dk, dv, dz, dsmall, dalog, ddtb, dgain = vjp((_split_heads(do_ref, 0, H), ds_scr[...]))
        ds_scr[...] = dS
        _store_heads(dqkv_ref, 0, dq)
        _store_heads(dqkv_ref, HW, dk)
        _store_heads(dqkv_ref, 2 * HW, dv)
        _store_heads(dzs_ref, 0, dz)
        dzs_ref[:, HW:HW + LANES] = dsmall.astype(dzs_ref.dtype)
        dalog_ref[...] += dalog
        ddtb_ref[...] += ddtb
        dgain_ref[...] += dgain

    row = pl.BlockSpec((1, LANES), lambda i: (0, 0))
    rev = lambda i: nc - 1 - i
    return pl.pallas_call(
        body, name="gdn_bwd", grid=(nc,),
        in_specs=[pl.BlockSpec((C, 3 * HW), lambda i: (rev(i), 0)), pl.BlockSpec((C, zw), lambda i: (rev(i), 0)),
                  pl.BlockSpec((C, HW), lambda i: (rev(i), 0)),
                  pl.BlockSpec((1, H, HEAD, HEAD), lambda i: (rev(i), 0, 0, 0)), row, row, row],
        out_specs=[pl.BlockSpec((C, 3 * HW), lambda i: (rev(i), 0)), pl.BlockSpec((C, zw), lambda i: (rev(i), 0)),
                   row, row, row],
        out_shape=[jax.ShapeDtypeStruct(qkv.shape, F32), jax.ShapeDtypeStruct(proj_zs.shape, BF16)]
        + [jax.ShapeDtypeStruct((1, LANES), F32)] * 3,
        scratch_shapes=[pltpu.VMEM((H, HEAD, HEAD), F32)], compiler_params=_params(),
    )(qkv, proj_zs, do, s_all, alog, dtb, gain)


def _hgrn_refs(proj_ref, lb_ref, HP):
    W = HP * HEAD
    return (_split_heads(proj_ref, 0, HP), _split_heads(proj_ref, W, HP), _split_heads(proj_ref, 2 * W, HP),
            _split_heads(proj_ref, 3 * W, HP), _split_heads(lb_ref, 0, HP, slice(0, 1)),
            _split_heads(lb_ref, 0, HP, slice(1, 2)))


def _hgrn_fwd(proj, lower_bounds, gain, nheads):
    T = proj.shape[0]
    C, HP = HGRN_CHUNK, HGRN_HEADS_PER_STEP
    ng, nc, W = nheads // HP, T // C, HP * HEAD

    def body(proj_ref, lb_ref, gain_ref, o_ref, sall_ref, s_scr):
        @pl.when(pl.program_id(1) == 0)
        def _():
            s_scr[...] = jnp.zeros_like(s_scr)

        sall_ref[0] = s_scr[...]
        qr, fr, vi, z, lb0, lb1 = _hgrn_refs(proj_ref, lb_ref, HP)
        outs, states = _hgrn_step(s_scr[...], qr, fr, vi, z, lb0, lb1, gain_ref[...])
        _store_heads(o_ref, 0, outs)
        s_scr[...] = states

    return pl.pallas_call(
        body, name="hgrn_fwd", grid=(ng, nc),
        in_specs=[pl.BlockSpec((C, 4 * W), lambda g, i: (i, g)), pl.BlockSpec((2, W), lambda g, i: (0, g)),
                  pl.BlockSpec((1, LANES), lambda g, i: (0, 0))],
        out_specs=[pl.BlockSpec((C, W), lambda g, i: (i, g)),
                   pl.BlockSpec((1, HP, HEAD, HEAD), lambda g, i: (i, g, 0, 0))],
        out_shape=[jax.ShapeDtypeStruct((T, nheads * HEAD), BF16), jax.ShapeDtypeStruct((nc, nheads, HEAD, HEAD), F32)],
        scratch_shapes=[pltpu.VMEM((HP, HEAD, HEAD), F32)], compiler_params=_params(),
    )(proj, lower_bounds, gain)


def _hgrn_bwd(proj, do, s_all, lower_bounds, gain, nheads):
    T = proj.shape[0]
    C, HP = HGRN_CHUNK, HGRN_HEADS_PER_STEP
    ng, nc, W = nheads // HP, T // C, HP * HEAD

    def body(proj_ref, do_ref, sin_ref, lb_ref, gain_ref, dproj_ref, dlb_ref, dgain_ref, ds_scr):
        first = pl.program_id(1) == 0

        @pl.when(first)
        def _():
            ds_scr[...] = jnp.zeros_like(ds_scr)
            dlb_ref[...] = jnp.zeros_like(dlb_ref)

        @pl.when(first & (pl.program_id(0) == 0))
        def _():
            dgain_ref[...] = jnp.zeros_like(dgain_ref)

        qr, fr, vi, z, lb0, lb1 = _hgrn_refs(proj_ref, lb_ref, HP)
        primals = (sin_ref[0], qr, fr, vi, z, lb0, lb1, gain_ref[...])
        _, vjp = jax.vjp(_hgrn_step, *primals)
        dS, dq, df, dv, dz, dlb0, dlb1, dgain = vjp((_split_heads(do_ref, 0, HP), ds_scr[...]))
        ds_scr[...] = dS
        for part, d in enumerate((dq, df, dv, dz)):
            _store_heads(dproj_ref, part * W, d)
        _store_heads(dlb_ref, 0, dlb0, slice(0, 1), accumulate=True)
        _store_heads(dlb_ref, 0, dlb1, slice(1, 2), accumulate=True)
        dgain_ref[...] += dgain

    rev = lambda i: nc - 1 - i
    return pl.pallas_call(
        body, name="hgrn_bwd", grid=(ng, nc),
        in_specs=[pl.BlockSpec((C, 4 * W), lambda g, i: (rev(i), g)), pl.BlockSpec((C, W), lambda g, i: (rev(i), g)),
                  pl.BlockSpec((1, HP, HEAD, HEAD), lambda g, i: (rev(i), g, 0, 0)),
                  pl.BlockSpec((2, W), lambda g, i: (0, g)), pl.BlockSpec((1, LANES), lambda g, i: (0, 0))],
        out_specs=[pl.BlockSpec((C, 4 * W), lambda g, i: (rev(i), g)), pl.BlockSpec((2, W), lambda g, i: (0, g)),
                   pl.BlockSpec((1, LANES), lambda g, i: (0, 0))],
        out_shape=[jax.ShapeDtypeStruct(proj.shape, BF16), jax.ShapeDtypeStruct(lower_bounds.shape, F32),
                   jax.ShapeDtypeStruct((1, LANES), F32)],
        scratch_shapes=[pltpu.VMEM((HP, HEAD, HEAD), F32)], compiler_params=_params(),
    )(proj, do, s_all, lower_bounds, gain)


def _post_specs(T):
    tr = min(POST_TILE, T)
    tile = lambda w: pl.BlockSpec((tr, w), lambda i: (i, 0))
    full = lambda r, w: pl.BlockSpec((r, w), lambda i: (0, 0))
    return tr, tile, full


def _post_fwd(s, x, p, g, b, wg, wpl, name):
    T, D = x.shape
    P = p.shape[1]
    tr, tile, full = _post_specs(T)

    def body(s_ref, x_ref, p_ref, g_ref, b_ref, wg_ref, wpl_ref, o_ref, o16_ref):
        x1 = _post_norm(s_ref[...], x_ref[...], g_ref[...], b_ref[...])
        xn = _post_gate(x1, _dot_raw(x1, wg_ref[...], "nn", False), _dot_raw(p_ref[...], wpl_ref[...], "nn", False))
        o_ref[...] = xn
        o16_ref[...] = xn.astype(BF16)

    return pl.pallas_call(
        body, name=name, grid=(T // tr,),
        in_specs=[tile(D), tile(D), tile(P), full(1, D), full(1, D), full(D, D), full(P, D)],
        out_specs=[tile(D), tile(D)],
        out_shape=[jax.ShapeDtypeStruct((T, D), F32), jax.ShapeDtypeStruct((T, D), BF16)], compiler_params=_params(),
    )(s, x, p, g, b, wg, wpl)


def _post_bwd(s, x, p, g, b, wg, wpl, dnext, name, with_loss):
    T, D = x.shape
    P = p.shape[1]
    tr, tile, full = _post_specs(T)

    def body(s_ref, x_ref, p_ref, g_ref, b_ref, wg_ref, wpl_ref, dn_ref,
             ds_ref, dx_ref, dg_ref, db_ref, dwg_ref, dwpl_ref, loss_ref):
        @pl.when(pl.program_id(0) == 0)
        def _():
            for r in (dg_ref, db_ref, dwg_ref, dwpl_ref, loss_ref):
                r[...] = jnp.zeros_like(r)

        x1, vjp_norm = jax.vjp(_post_norm, s_ref[...], x_ref[...], g_ref[...], b_ref[...])
        gate_pre = _dot_raw(x1, wg_ref[...], "nn", False)
        pp = _dot_raw(p_ref[...], wpl_ref[...], "nn", False)
        xn, vjp_gate = jax.vjp(_post_gate, x1, gate_pre, pp)
        if with_loss:
            err = xn - dn_ref[...]
            loss_ref[...] += 0.5 * jnp.sum(jnp.sum(err * err, axis=-1, keepdims=True), axis=0, keepdims=True) / D
            dn = err / D
        else:
            dn = dn_ref[...]
        dx1, dgp, dpp = vjp_gate(dn)
        dwg_ref[...] += _dot_raw(x1, dgp, "tn", False)
        dwpl_ref[...] += _dot_raw(p_ref[...], dpp, "tn", False)
        dx1 = dx1 + _dot_raw(dgp, wg_ref[...], "nt", False)
        ds, dx, dg, db = vjp_norm(dx1)
        ds_ref[...] = ds.astype(ds_ref.dtype)
        dx_ref[...] = dx
        dg_ref[...] += dg
        db_ref[...] += db

    return pl.pallas_call(
        body, name=name, grid=(T // tr,),
        in_specs=[tile(D), tile(D), tile(P), full(1, D), full(1, D), full(D, D), full(P, D), tile(D)],
        out_specs=[tile(D), tile(D), full(1, D), full(1, D), full(D, D), full(P, D), full(SUBLANES, LANES)],
        out_shape=[jax.ShapeDtypeStruct((T, D), BF16), jax.ShapeDtypeStruct((T, D), F32)]
        + [jax.ShapeDtypeStruct((1, D), F32)] * 2
        + [jax.ShapeDtypeStruct((D, D), F32), jax.ShapeDtypeStruct((P, D), F32),
           jax.ShapeDtypeStruct((SUBLANES, LANES), F32)],
        compiler_params=_params(),
    )(s, x, p, g, b, wg, wpl, dnext)


def _adam_math(w, g, m, v):
    m = ADAM_B1 * m + (1.0 - ADAM_B1) * g
    v = ADAM_B2 * v + (1.0 - ADAM_B2) * (g * g)
    m_hat = m / (1.0 - ADAM_B1 ** ADAM_STEP)
    v_hat = v / (1.0 - ADAM_B2 ** ADAM_STEP)
    return -ADAM_LR * (m_hat / (jnp.sqrt(v_hat) + ADAM_EPS) + ADAM_WD * w), m, v


def _shard_tiles(R, C):
    tr = _pick(R, (256, 128, 64, 32, 16, 8))
    return (tr, C) if tr < R or R % SUBLANES == 0 else (R, _pick(C, (256, 128)))


def _adam_sharded(w, m, v, g8, got, me, name):
    rows_apart = w.shape[1] == 1 and w.shape[0] > 1
    (L, R, C) = (1, w.shape[0], w.shape[2]) if rows_apart else w.shape
    tr, tc = (R, LANES) if rows_apart else _shard_tiles(R, C)
    nr, nc = R // tr, C // tc
    side_by_side = g8.ndim == 2

    def body(me_ref, w_ref, m_ref, v_ref, p_ref, *rest):
        got_refs, (g_ref, d_ref, mo_ref, vo_ref) = rest[:7], rest[7:]
        g = p_ref[...] if side_by_side else p_ref[0]
        for r in got_refs:
            g = g + r[0].astype(F32)
        if rows_apart:
            d, mn, vn = _adam_math(w_ref[:, 0, :], g, m_ref[:, 0, :], v_ref[:, 0, :])
            for ref, val in ((g_ref, g), (d_ref, d), (mo_ref, mn), (vo_ref, vn)):
                ref[:, 0, :] = val
            return
        d, mn, vn = _adam_math(w_ref[0], g, m_ref[0], v_ref[0])
        g_ref[0] = g
        d_ref[0] = d
        mo_ref[0] = mn
        vo_ref[0] = vn

    if rows_apart:
        t3 = pl.BlockSpec((tr, 1, tc), lambda l, i, j, q: (i, 0, j))
    else:
        t3 = pl.BlockSpec((1, tr, tc), lambda l, i, j, q: (l, i, j))
    slot = lambda k: pl.BlockSpec((1, tr, tc), lambda l, i, j, q: (k, l * nr + i, j))
    if side_by_side:
        mine = pl.BlockSpec((tr, tc), lambda l, i, j, q: (l * nr + i, q[0] * nc + j))
    else:
        mine = pl.BlockSpec((1, tr, tc), lambda l, i, j, q: (q[0], l * nr + i, j))
    return pl.pallas_call(
        body, name=name,
        grid_spec=pltpu.PrefetchScalarGridSpec(
            num_scalar_prefetch=1, grid=(L, nr, nc),
            in_specs=[t3, t3, t3, mine] + [slot(k) for k in range(7)], out_specs=[t3, t3, t3, t3]),
        out_shape=[jax.ShapeDtypeStruct(w.shape, F32)] * 4, compiler_params=_params(),
    )(me, w, m, v, g8, *([got] * 7))


def _small_rows(shapes):
    offsets, r = [], 0
    for rows, _ in shapes:
        offsets.append(r)
        r += rows
    return offsets, -(-(r + 1) // SUBLANES) * SUBLANES, max(cols for _, cols in shapes)


def _pack_small_grads(grads, loss):
    offsets, total, width = _small_rows([g.shape for g in grads])
    packed = jnp.zeros((total, width), F32)
    for g, r in zip(grads, offsets):
        packed = lax.dynamic_update_slice(packed, g, (r, 0))
    return lax.dynamic_update_slice(packed, loss.reshape(1, 1), (total - 1, 0))


def _adam_replicated(params, g8):
    shapes = [w.shape for w, _, _ in params]
    offsets, total, width = _small_rows(shapes)
    n = len(params)

    def body(*refs):
        g_ref, outs, loss_ref, g_scr = refs[3 * n], refs[3 * n + 1:7 * n + 1], refs[7 * n + 1], refs[7 * n + 2]
        g = g_ref[0]
        for k in range(1, 8):
            g = g + g_ref[k]
        g_scr[...] = g
        for i, (rows, cols) in enumerate(shapes):
            gp = g_scr[offsets[i]:offsets[i] + rows, 0:cols]
            d, mn, vn = _adam_math(refs[3 * i][...], gp, refs[3 * i + 1][...], refs[3 * i + 2][...])
            for ref, val in zip(outs[4 * i:4 * i + 4], (gp, d, mn, vn)):
                ref[...] = val
        loss_ref[...] = g_scr[total - 1:total, 0:LANES]

    out = pl.pallas_call(
        body, name="adam_replicated",
        out_shape=[jax.ShapeDtypeStruct(shp, F32) for shp in shapes for _ in range(4)]
        + [jax.ShapeDtypeStruct((1, LANES), F32)],
        scratch_shapes=[pltpu.VMEM((total, width), F32)], compiler_params=_params(),
    )(*[a for triple in params for a in triple], g8)
    return [out[4 * i:4 * i + 4] for i in range(n)], out[4 * n][0, 0]


def _place():
    return lax.axis_index("x"), lax.axis_index("y"), lax.axis_index("c")


def _all_gather(shard, name):
    def body(x_ref, out_ref, send_sems, recv_sems, local_sem):
        x, y, c = _place()
        me, sibling = (x, y, c), (x, y, 1 - c)
        chips = [(1 - x, y), (x, 1 - y), (1 - x, 1 - y)]

        def slab(px, py, pc):
            return out_ref.at[4 * px + 2 * py + pc]

        def copy(k, block, to, src=None):
            return pltpu.make_async_remote_copy(
                src_ref=slab(*block) if src is None else src, dst_ref=slab(*block),
                send_sem=send_sems.at[k], recv_sem=recv_sems.at[k], device_id=to, device_id_type=MESH)

        mine = pltpu.make_async_copy(x_ref, slab(*me), local_sem)
        mine.start()
        first = [copy(0, me, sibling, src=x_ref)]
        first += [copy(1 + j, me, (*chip, c), src=x_ref) for j, chip in enumerate(chips)]
        for cp in first:
            cp.start()
        passed = [copy(4 + j, (*chip, c), sibling) for j, chip in enumerate(chips)]
        for j, chip in enumerate(chips):
            copy(1 + j, (*chip, c), me).wait_recv()
            passed[j].start()
        copy(0, sibling, me).wait_recv()
        for j, chip in enumerate(chips):
            copy(4 + j, (*chip, 1 - c), me).wait_recv()
        for cp in first + passed:
            cp.wait_send()
        mine.wait()

    return pl.pallas_call(
        body, name=name, out_shape=jax.ShapeDtypeStruct((8,) + shard.shape, shard.dtype),
        in_specs=[pl.BlockSpec(memory_space=pl.ANY)], out_specs=pl.BlockSpec(memory_space=pl.ANY),
        scratch_shapes=[pltpu.SemaphoreType.DMA((7,)), pltpu.SemaphoreType.DMA((7,)), pltpu.SemaphoreType.DMA],
    )(shard)


_HBM = pl.BlockSpec(memory_space=pltpu.HBM)
_SEM = pl.BlockSpec(memory_space=pltpu.SEMAPHORE)
_DATAFLOW = pltpu.SideEffectType.DATAFLOW_SIDE_EFFECTING
TOKEN_SHAPE = (SUBLANES, LANES)


def _peers(x, y, c):
    flip = lambda v, bit: 1 - v if bit else v
    return [(flip(x, r >> 2 & 1), flip(y, r >> 1 & 1), flip(c, r & 1)) for r in range(1, 8)]


def _scatter_plan(x, y, c):
    return [(4 * px + 2 * py + pc, k, (px, py, pc)) for k, (px, py, pc) in enumerate(_peers(x, y, c))]


def _exchange_copies(plan, src_ref, land_ref, send_sems, recv_sems):
    C = land_ref.shape[-1]
    block = (lambda b: src_ref.at[b]) if len(src_ref.shape) == len(land_ref.shape) else (
        lambda b: src_ref.at[:, pl.ds(b * C, C)])
    return [pltpu.make_async_remote_copy(
        src_ref=block(blk), dst_ref=land_ref.at[slot], send_sem=send_sems.at[k], recv_sem=recv_sems.at[k],
        device_id=peer, device_id_type=MESH) for k, (blk, slot, peer) in enumerate(plan(*_place()))]


def _exchange_start(src, n_slots, block_shape, plan, name):
    land_shape = (n_slots,) + tuple(block_shape)
    n = len(plan(0, 0, 0))

    def body(src_ref, land_ref, send_sems, recv_sems, src_thru, land_thru, token):
        for cp in _exchange_copies(plan, src_ref, land_ref, send_sems, recv_sems):
            cp.start()
        token[...] = jnp.zeros_like(token)

    return pl.pallas_call(
        body, name=name,
        out_shape=(pltpu.SemaphoreType.DMA((n,)), pltpu.SemaphoreType.DMA((n,)), pltpu.HBM(src.shape, src.dtype),
                   pltpu.HBM(land_shape, src.dtype), jax.ShapeDtypeStruct(TOKEN_SHAPE, F32)),
        in_specs=(_HBM, _HBM), out_specs=(_SEM, _SEM, _HBM, _HBM, pl.BlockSpec(memory_space=pltpu.VMEM)),
        input_output_aliases={0: 2, 1: 3}, compiler_params=pltpu.CompilerParams(has_side_effects=_DATAFLOW),
    )(pltpu.with_memory_space_constraint(src, pltpu.HBM),
      pltpu.with_memory_space_constraint(lax.empty(land_shape, src.dtype), pltpu.HBM))


def _exchange_start_many(srcs, n_slots, block_shapes, plan, name):
    k, n = len(srcs), len(plan(0, 0, 0))
    land_shapes = [(n_slots,) + tuple(bs) for bs in block_shapes]

    def body(*refs):
        src_refs, land_refs, outs = refs[:k], refs[k:2 * k], refs[2 * k:]
        for i in range(k):
            for cp in _exchange_copies(plan, src_refs[i], land_refs[i], outs[2 * i], outs[2 * i + 1]):
                cp.start()
        outs[-1][...] = jnp.zeros_like(outs[-1])

    sems = [pltpu.SemaphoreType.DMA((n,))] * (2 * k)
    out = pl.pallas_call(
        body, name=name,
        out_shape=(*sems, *[pltpu.HBM(s.shape, s.dtype) for s in srcs],
                   *[pltpu.HBM(ls, s.dtype) for ls, s in zip(land_shapes, srcs)],
                   jax.ShapeDtypeStruct(TOKEN_SHAPE, F32)),
        in_specs=(_HBM,) * (2 * k),
        out_specs=(*[_SEM] * (2 * k), *[_HBM] * (2 * k), pl.BlockSpec(memory_space=pltpu.VMEM)),
        input_output_aliases={i: 2 * k + i for i in range(2 * k)},
        compiler_params=pltpu.CompilerParams(has_side_effects=_DATAFLOW),
    )(*[pltpu.with_memory_space_constraint(s, pltpu.HBM) for s in srcs],
      *[pltpu.with_memory_space_constraint(lax.empty(ls, s.dtype), pltpu.HBM) for ls, s in zip(land_shapes, srcs)])
    return [(out[2 * i], out[2 * i + 1], out[2 * k + i], out[3 * k + i], out[-1]) for i in range(k)]


def _exchange_wait(handle, plan, after, name):
    send_sems, recv_sems, src_thru, land_thru, _ = handle

    def body(src_ref, land_ref, send_sems, recv_sems, after_ref, src_dead, got_ref):
        for cp in _exchange_copies(plan, src_ref, land_ref, send_sems, recv_sems):
            cp.wait_send()
            cp.wait_recv()

    return pl.pallas_call(
        body, name=name,
        out_shape=(pltpu.HBM(src_thru.shape, src_thru.dtype), pltpu.HBM(land_thru.shape, land_thru.dtype)),
        in_specs=(_HBM, _HBM, _SEM, _SEM, pl.BlockSpec(memory_space=pl.ANY)), out_specs=(_HBM, _HBM),
        input_output_aliases={0: 0, 1: 1}, compiler_params=pltpu.CompilerParams(has_side_effects=_DATAFLOW),
    )(src_thru, land_thru, send_sems, recv_sems, after)[1]


def _gather_plan(x, y, c):
    return [(0, 4 * x + 2 * y + c, peer) for peer in _peers(x, y, c)]


class _LateGather:
    def __init__(self, shard, name, handle):
        self.shard, self.name, self.handle = shard, name, handle

    @classmethod
    def start_all(cls, shards, names):
        handles = _exchange_start_many([s[None] for s in shards], 8, [s.shape for s in shards], _gather_plan,
                                       "ag_late_start")
        return [cls(s, n, h) for s, n, h in zip(shards, names, handles)]

    def get(self, after):
        land = _exchange_wait(self.handle, _gather_plan, after, self.name + "_wait")
        x, y, c = _place()
        return lax.dynamic_update_slice(land, self.shard[None], (4 * x + 2 * y + c, 0, 0))


class _GradExchange:
    def __init__(self, me, layouts):
        self.me, self.layouts, self.pending = me, layouts, {}

    def start(self, tag, grad, grad16=None):
        g8 = self.layouts[tag](grad)
        g16 = g8.astype(BF16) if grad16 is None else self.layouts[tag](grad16)
        block_shape = g8.shape[1:] if g8.ndim == 3 else (g8.shape[0], g8.shape[1] // 8)
        handle = _exchange_start(g16, 7, block_shape, _scatter_plan, "rs_start_" + tag)
        self.pending[tag] = (g8, handle)
        return handle[4]

    def finish(self, tag, w, m, v, after):
        g8, handle = self.pending.pop(tag)
        got = _exchange_wait(handle, _scatter_plan, after, "rs_wait_" + tag)
        return _adam_sharded(w, m, v, g8, got, self.me, "adam_" + tag)


def _local_grads(x, p0, p1, target, wt_zs, wt_a, wt_qkv, late, conv_a, conv_b,
                 a_log, dt_bias, gdn_gain, lower_bounds, hgrn_gain, ln_g, ln_b, on_grad=None):
    H = a_log.shape[1]
    pad_small = ((0, 0), (H, LANES - 2 * H))
    alog_row = jnp.pad(a_log, pad_small)
    dtb_row = jnp.pad(dt_bias, pad_small)

    x16 = x.astype(BF16)
    proj_zs = _matmul(x16, wt_zs, "nt", "proj_even_zs", after=late.started)
    proj_a = _matmul(x16, wt_a, "nt", "proj_even_a", after=late.started)
    proj_qkv = _matmul(x16, wt_qkv, "nt", "proj_even_qkv", after=late.started)
    y_a = _mixer_a_fwd(proj_a, conv_a)
    qkv = _conv_b_fwd(proj_qkv, conv_b)
    o2, s_gdn = _gdn_fwd(qkv, proj_zs, alog_row, dtb_row, gdn_gain, H)
    woute_a, woute_b = late.out_even(o2)
    wg, wpl = late.gate(o2)
    s_e = _matmul(o2, woute_b, "nn", "out_even_b", add=_matmul(y_a, woute_a, "nn", "out_even_a"))
    x2, x2_16 = _post_fwd(s_e, x, p0, ln_g[0:1], ln_b[0:1], wg[0], wpl[0], "post_even_fwd")
    wino, wouto = late.odd(s_e)
    nheads_o = wouto.shape[0] // HEAD
    proj_o = _matmul(x2_16, wino, "nn", "proj_odd")
    o4, s_hgrn = _hgrn_fwd(proj_o, lower_bounds, hgrn_gain, nheads_o)
    s_o = _matmul(o4, wouto, "nn", "out_odd")
    ds_o, dx2, dlng1, dlnb1, dwg1, dwpl1, loss = _post_bwd(
        s_o, x2, p1, ln_g[1:2], ln_b[1:2], wg[1], wpl[1], target, "post_odd_loss_bwd", True)
    do4 = _matmul(ds_o, wouto, "nt", "d_out_odd_act")
    grads = {}

    def emit(tag, grad, grad16=None):
        grads[tag] = grad
        return on_grad(tag, grad, grad16) if on_grad is not None else jnp.zeros(TOKEN_SHAPE, F32)

    tok = emit("w_out_odd", *_matmul(o4, ds_o, "tn", "d_out_odd_w", also_bf16=True))
    dproj_o, dlb, dhgain = _hgrn_bwd(proj_o, do4, s_hgrn, lower_bounds, hgrn_gain + tok[0:1], nheads_o)
    dx2 = _matmul(dproj_o, wino, "nt", "d_proj_odd_act", add=dx2)
    tok = emit("w_in_odd", *_matmul(x2_16, dproj_o, "tn", "d_proj_odd_w", also_bf16=True))
    ds_e, dx, dlng0, dlnb0, dwg0, dwpl0, _ = _post_bwd(
        s_e, x, p0, ln_g[0:1], ln_b[0:1] + tok[0:1, 0:1], wg[0], wpl[0], dx2, "post_even_bwd", False)
    tok = emit("w_pl_gate", jnp.stack([dwg0, dwg1])) + emit("w_pl", jnp.stack([dwpl0, dwpl1]))
    dy_a = _matmul(ds_e, woute_a, "nt", "d_out_even_a_act")
    do2 = _matmul(ds_e, woute_b, "nt", "d_out_even_b_act")
    dwoute_a = _matmul(y_a, ds_e, "tn", "d_out_even_a_w")
    dwoute_b = _matmul(o2, ds_e, "tn", "d_out_even_b_w")
    tok = tok + emit("w_out_even", jnp.concatenate([dwoute_a, dwoute_b], axis=0))
    dqkv, dproj_zs, dalog, ddtb, dggain = _gdn_bwd(qkv, proj_zs, do2, s_gdn, alog_row, dtb_row, gdn_gain + tok[0:1], H)
    dproj_qkv, dconv_b = _conv_b_bwd(proj_qkv, dqkv, conv_b)
    dproj_a, dconv_a = _mixer_a_bwd(proj_a, dy_a, conv_a)
    emit("conv", (dconv_a[:conv_a.shape[0]], dconv_b[:conv_b.shape[0]]))
    tok = emit("w_in_even", (_matmul(dproj_zs, x16, "tn", "d_proj_even_zs_w"), _matmul(dproj_a, x16, "tn", "d_proj_even_a_w"),
                             _matmul(dproj_qkv, x16, "tn", "d_proj_even_qkv_w")))
    dx = _matmul(dproj_zs, wt_zs, "nn", "d_proj_even_zs_act", add=dx, after=tok)
    dx = _matmul(dproj_a, wt_a, "nn", "d_proj_even_a_act", add=dx)
    dx = _matmul(dproj_qkv, wt_qkv, "nn", "d_proj_even_qkv_act", add=dx)
    grads.update(
        loss=loss[0, 0], grad_x=dx, a_log=dalog[:, H:2 * H], dt_bias=ddtb[:, H:2 * H], gdn_gain=dggain,
        lower_bounds=dlb, hgrn_gain=dhgain, ln_g=jnp.concatenate([dlng0, dlng1], axis=0),
        ln_b=jnp.concatenate([dlnb0, dlnb1], axis=0))
    return grads


def _pad_rows(a, rows):
    return jnp.pad(a, ((0, rows - a.shape[0]), (0, 0)))


def _split_in_even(wt_full, AW, HW, H):
    D = wt_full.shape[1]
    n_a = 4 * AW
    n_main = n_a + 3 * HW
    wt_zs = jnp.concatenate([wt_full[n_main:n_main + HW], wt_full[n_main + HW:],
                             jnp.zeros((LANES - 2 * H, D), wt_full.dtype)], axis=0)
    wt_a = wt_full[:n_a].reshape(4, AW // MIXER_LANES, MIXER_LANES, D).transpose(1, 0, 2, 3).reshape(n_a, D)
    return wt_zs, wt_a, wt_full[n_a:n_main]


def _join_in_even(dt_zs, dt_a, dt_qkv, AW, HW, H):
    D = dt_a.shape[1]
    a_nat = dt_a.reshape(AW // MIXER_LANES, 4, MIXER_LANES, D).transpose(1, 0, 2, 3).reshape(4 * AW, D)
    return jnp.concatenate([a_nat, dt_qkv, dt_zs[:HW], dt_zs[HW:HW + 2 * H]], axis=0)


def kernel(x, p, w_in_even, conv_a_w, conv_b_w, a_log, dt_bias, gdn_norm_g, w_out_even, w_in_odd, lower_bounds, hgrn_norm_g, w_out_odd, ln_g, ln_b, w_pl, w_pl_gate, loss_target, m_w_in_even, m_conv_a_w, m_conv_b_w, m_a_log, m_dt_bias, m_gdn_norm_g, m_w_out_even, m_w_in_odd, m_lower_bounds, m_hgrn_norm_g, m_w_out_odd, m_ln_g, m_ln_b, m_w_pl, m_w_pl_gate, v_w_in_even, v_conv_a_w, v_conv_b_w, v_a_log, v_dt_bias, v_gdn_norm_g, v_w_out_even, v_w_in_odd, v_lower_bounds, v_hgrn_norm_g, v_w_out_odd, v_ln_g, v_ln_b, v_w_pl, v_w_pl_gate):
    xi, yi, ci = _place()
    me = jnp.reshape(4 * xi + 2 * yi + ci, (1,)).astype(jnp.int32)
    D = x.shape[2]
    H = a_log.shape[1]
    HW = H * HEAD
    AW = conv_a_w.shape[2] * 8
    OW = w_out_odd.shape[1] * 8
    PD = w_pl.shape[1]
    ka, kb = conv_a_w.shape[1], conv_b_w.shape[1]
    ca, cb = conv_a_w.shape[2], conv_b_w.shape[2]
    gw = HGRN_HEADS_PER_STEP * HEAD
    ngrp = OW // gw

    transposed = lambda a: jnp.transpose(a, (0, 2, 1))
    g_ine = _all_gather(transposed(w_in_even)[0].astype(BF16), "ag_w_in_even")
    wt_zs, wt_a, wt_qkv = _split_in_even(g_ine.reshape(-1, D), AW, HW, H)
    taps = jnp.concatenate([_pad_rows(conv_a_w[0], SUBLANES), _pad_rows(conv_b_w[0], SUBLANES)], axis=1)
    g_taps = _all_gather(taps, "ag_conv")
    conv_a = jnp.transpose(g_taps[:, :ka, :ca], (1, 0, 2)).reshape(ka, 8 * ca)
    conv_b = jnp.transpose(g_taps[:, :kb, ca:], (1, 0, 2)).reshape(kb, 8 * cb)
    behind = lambda shard, dep: lax.optimization_barrier((shard, dep))[0]
    late_oute, late_gate, late_pl, late_ino, late_outo = _LateGather.start_all(
        [behind(w_out_even[0].astype(BF16), (g_ine, g_taps)), w_pl_gate.astype(BF16).reshape(-1, D),
         w_pl.astype(BF16).reshape(DEPTH * PD, -1), w_in_odd[0].astype(BF16), w_out_odd[0].astype(BF16)],
        ["ag_w_out_even", "ag_w_pl_gate", "ag_w_pl", "ag_w_in_odd", "ag_w_out_odd"])

    class _Late:
        started = late_oute.handle[4]

        @staticmethod
        def out_even(after):
            woute = late_oute.get(after).reshape(-1, D)
            return woute[:AW], woute[AW:]

        @staticmethod
        def gate(after):
            g_gate, g_pl = late_gate.get(after), late_pl.get(after)
            return (g_gate.reshape(8, DEPTH, D // 8, D).transpose(1, 0, 2, 3).reshape(DEPTH, D, D),
                    g_pl.reshape(8, DEPTH, PD, D // 8).transpose(1, 2, 0, 3).reshape(DEPTH, PD, D))

        @staticmethod
        def odd(after):
            g_ino = late_ino.get(after)
            wino = jnp.transpose(g_ino, (1, 0, 2)).reshape(D, 4, ngrp, gw).transpose(0, 2, 1, 3).reshape(D, 4 * OW)
            return wino, late_outo.get(after).reshape(-1, D)

    sh = w_in_even.shape[2]
    tap_blocks = lambda g, width: _pad_rows(g, SUBLANES).reshape(SUBLANES, 8, width).transpose(1, 0, 2)
    owner_layout = {
        "w_in_even": lambda g: _join_in_even(*g, AW, HW, H).reshape(8, sh, D),
        "w_in_odd": lambda g: g.reshape(D, ngrp, 4, gw).transpose(0, 2, 1, 3).reshape(D, 4 * OW),
        "w_out_even": lambda g: g.reshape(8, -1, D),
        "w_out_odd": lambda g: g.reshape(8, -1, D),
        "w_pl_gate": lambda g: g.reshape(DEPTH, 8, D // 8, D).transpose(1, 0, 2, 3).reshape(8, DEPTH * D // 8, D),
        "w_pl": lambda g: g.reshape(DEPTH, PD, 8, D // 8).transpose(2, 0, 1, 3).reshape(8, DEPTH * PD, D // 8),
        "conv": lambda g: jnp.concatenate([tap_blocks(g[0], ca), tap_blocks(g[1], cb)], axis=2),
    }
    exchange = _GradExchange(me, owner_layout)
    gr = _local_grads(x[0], p[0, 0], p[1, 0], loss_target[0], wt_zs, wt_a, wt_qkv, _Late, conv_a, conv_b,
                      a_log, dt_bias, gdn_norm_g, lower_bounds, hgrn_norm_g, ln_g, ln_b, on_grad=exchange.start)

    last = gr["grad_x"]
    pack_taps = lambda a, b: jnp.concatenate([_pad_rows(a[0], SUBLANES), _pad_rows(b[0], SUBLANES)], axis=1)[None]
    o_outo = exchange.finish("w_out_odd", w_out_odd, m_w_out_odd, v_w_out_odd, last)
    o_ino = exchange.finish("w_in_odd", w_in_odd, m_w_in_odd, v_w_in_odd, last)
    o_gate = exchange.finish("w_pl_gate", w_pl_gate, m_w_pl_gate, v_w_pl_gate, last)
    o_pl = exchange.finish("w_pl", w_pl, m_w_pl, v_w_pl, last)
    o_oute = exchange.finish("w_out_even", w_out_even, m_w_out_even, v_w_out_even, last)
    o_taps = exchange.finish("conv", taps[None], pack_taps(m_conv_a_w, m_conv_b_w), pack_taps(v_conv_a_w, v_conv_b_w), last)
    others_done = sum(o[1][0, 0:1, 0:1] for o in (o_outo, o_ino, o_gate, o_pl, o_oute, o_taps))
    rows_first = lambda a: jnp.transpose(a, (2, 0, 1))
    o_ine = [jnp.transpose(o, (1, 2, 0)) for o in exchange.finish(
        "w_in_even", rows_first(w_in_even), rows_first(m_w_in_even), rows_first(v_w_in_even), others_done)]

    small_g = _pack_small_grads([gr["a_log"], gr["dt_bias"], gr["gdn_gain"], gr["lower_bounds"], gr["hgrn_gain"],
                                 gr["ln_g"], gr["ln_b"]], gr["loss"])
    o_small, loss = _adam_replicated(
        [(a_log, m_a_log, v_a_log), (dt_bias, m_dt_bias, v_dt_bias), (gdn_norm_g, m_gdn_norm_g, v_gdn_norm_g),
         (lower_bounds, m_lower_bounds, v_lower_bounds), (hgrn_norm_g, m_hgrn_norm_g, v_hgrn_norm_g),
         (ln_g, m_ln_g, v_ln_g), (ln_b, m_ln_b, v_ln_b)],
        _all_gather(behind(small_g, o_ine[0]), "ag_small_grads"))

    def leaves(kind):
        s_alog, s_dt, s_gg, s_lb, s_hg, s_lng, s_lnb = (o[kind] for o in o_small)
        t = o_taps[kind]
        return [o_ine[kind], t[:, :ka, :ca], t[:, :kb, ca:], s_alog, s_dt, s_gg, o_oute[kind],
                o_ino[kind], s_lb, s_hg, o_outo[kind], s_lng, s_lnb, o_pl[kind], o_gate[kind]]

    return (loss, gr["grad_x"][None], *leaves(0), *leaves(1), *leaves(2), *leaves(3))
```

```python
import functools

import jax
import jax.numpy as jnp
from jax import lax
from jax.experimental import pallas as pl
from jax.experimental.pallas import tpu as pltpu

F32 = jnp.float32
BF16 = jnp.bfloat16
MESH = pl.DeviceIdType.MESH

LANES = 128
SUBLANES = 8
HEAD = 128
GDN_CHUNK = 128
HGRN_CHUNK = 64
HGRN_SUB = 16
HGRN_HEADS_PER_STEP = 16
NORM_EPS = 1e-5
DEPTH = 2
ALPHA = (2.0 * DEPTH) ** 0.25
EXP_CLAMP = 80.0
ADAM_LR, ADAM_B1, ADAM_B2, ADAM_EPS, ADAM_WD, ADAM_STEP = 0.001, 0.9, 0.999, 1e-08, 0.01, 10
VMEM_LIMIT = 56 * 1024 * 1024
MATMUL_VMEM = 36 * 1024 * 1024
ROW_TILE = 1024
MIXER_LANES = 256
CONV_LANES = 512
POST_TILE = 512

_NOBATCH, _BATCH0 = ((), ()), ((0,), (0,))
_DIMS = {"nn": (((1,), (0,)), _NOBATCH), "nt": (((1,), (1,)), _NOBATCH), "tn": (((0,), (0,)), _NOBATCH),
         "bnn": (((2,), (1,)), _BATCH0), "bnt": (((2,), (2,)), _BATCH0), "btn": (((1,), (1,)), _BATCH0)}


def _params(**kw):
    return pltpu.CompilerParams(vmem_limit_bytes=VMEM_LIMIT, **kw)


def _dot_raw(a, b, kind, hi):
    if hi:
        return lax.dot_general(a, b, _DIMS[kind], precision=lax.Precision.HIGHEST, preferred_element_type=F32)
    return lax.dot_general(a.astype(BF16), b.astype(BF16), _DIMS[kind], preferred_element_type=F32)


@functools.partial(jax.custom_vjp, nondiff_argnums=(2, 3))
def mdot(a, b, kind, hi):
    return _dot_raw(a, b, kind, hi)


def _mdot_fwd(a, b, kind, hi):
    return _dot_raw(a, b, kind, hi), (a, b)


def _mdot_bwd(kind, hi, res, g):
    a, b = res
    pre, base = kind[:-2], kind[-2:]
    if base == "nn":
        return _dot_raw(g, b, pre + "nt", hi), _dot_raw(a, g, pre + "tn", hi)
    if base == "nt":
        return _dot_raw(g, b, pre + "nn", hi), _dot_raw(g, a, pre + "tn", hi)
    return _dot_raw(b, g, pre + "nt", hi), _dot_raw(a, g, pre + "nn", hi)


mdot.defvjp(_mdot_fwd, _mdot_bwd)


def _rows(x, lo, hi):
    return _take_rows(x, lo, hi, x.shape[-2])


@functools.partial(jax.custom_vjp, nondiff_argnums=(1, 2, 3))
def _take_rows(x, lo, hi, n):
    return x[..., lo:hi, :]


def _take_rows_fwd(x, lo, hi, n):
    return x[..., lo:hi, :], None


def _take_rows_bwd(lo, hi, n, _, g):
    parts = []
    if lo > 0:
        parts.append(jnp.zeros(g.shape[:-2] + (lo, g.shape[-1]), g.dtype))
    parts.append(g)
    if n - hi > 0:
        parts.append(jnp.zeros(g.shape[:-2] + (n - hi, g.shape[-1]), g.dtype))
    return (jnp.concatenate(parts, axis=-2) if len(parts) > 1 else g,)


_take_rows.defvjp(_take_rows_fwd, _take_rows_bwd)


def _heads_of(wide, nheads):
    return jnp.stack([wide[:, h * HEAD:(h + 1) * HEAD] for h in range(nheads)], axis=0)


def _wide_of(x):
    return jnp.concatenate([x[h] for h in range(x.shape[0])], axis=1)


@functools.partial(jax.custom_vjp, nondiff_argnums=(1,))
def to_heads(wide, nheads):
    return _heads_of(wide, nheads)


to_heads.defvjp(lambda wide, nheads: (_heads_of(wide, nheads), None), lambda nheads, _, g: (_wide_of(g),))


@jax.custom_vjp
def to_wide(x):
    return _wide_of(x)


to_wide.defvjp(lambda x: (_wide_of(x), None), lambda _, g: (_heads_of(g, g.shape[1] // HEAD),))


def _sigmoid(x):
    return jax.nn.sigmoid(x)


def _silu(x):
    return x * _sigmoid(x)


def _dsilu(x):
    s = _sigmoid(x)
    return s * (1.0 + x * (1.0 - s))


def _log1p(u):
    return jnp.where(u < 1e-4, u * (1.0 - 0.5 * u), jnp.log(1.0 + u))


def _softplus(x):
    return jnp.maximum(x, 0.0) + _log1p(jnp.exp(-jnp.abs(x)))


def _rms_gate(o, gain, z):
    return o * lax.rsqrt(jnp.mean(o * o, axis=-1, keepdims=True) + NORM_EPS) * gain * _silu(z)


def _l2n(x):
    return x * lax.rsqrt(jnp.sum(x * x, axis=-1, keepdims=True) + 1e-6)


def _split_dot_raw(m, x, kind):
    mb = m.astype(BF16)
    hi = x.astype(BF16)
    lo = (x - hi.astype(F32)).astype(BF16)
    dims = _DIMS[kind]
    return (lax.dot_general(mb, hi, dims, preferred_element_type=F32)
            + lax.dot_general(mb, lo, dims, preferred_element_type=F32))


@jax.custom_vjp
def mask_dot(m, x):
    return _split_dot_raw(m, x, "nn")


def _mask_dot_fwd(m, x):
    return _split_dot_raw(m, x, "nn"), m


def _mask_dot_bwd(m, g):
    return jnp.zeros_like(m), _split_dot_raw(m, g, "tn")


mask_dot.defvjp(_mask_dot_fwd, _mask_dot_bwd)


def _neumann_rest(low):
    n = low.shape[-1]
    rest = -low
    power = low
    span = 2
    while span < n:
        power = _dot_raw(power, power, "bnn", False)
        rest = rest + power + _dot_raw(rest, power, "bnn", False)
        span *= 2
    return rest


@jax.custom_vjp
def _unit_lower_inverse_minus_eye(low):
    return _neumann_rest(low)


def _inverse_fwd(low):
    rest = _neumann_rest(low)
    return rest, rest


def _inverse_bwd(rest, g):
    left = g + _dot_raw(rest, g, "btn", False)
    return (-(left + _dot_raw(left, rest, "bnt", False)),)


_unit_lower_inverse_minus_eye.defvjp(_inverse_fwd, _inverse_bwd)


def _gdn_step(S, q, k, v, z, small, alog, dtb, gain):
    H = S.shape[0]
    C = GDN_CHUNK
    row = lax.broadcasted_iota(jnp.int32, (C, C), 0)
    col = lax.broadcasted_iota(jnp.int32, (C, C), 1)
    tril, strict, eye = (row >= col)[None], (row > col)[None], (row == col)[None]
    head = lax.broadcasted_iota(jnp.int32, (H, 1, LANES), 0)
    lane = lax.broadcasted_iota(jnp.int32, (H, 1, LANES), 2)
    rowc = lax.broadcasted_iota(jnp.int32, (1, C, 1), 1)
    beta_all = _sigmoid(small)
    g_all = -jnp.exp(alog) * _softplus(small + dtb)
    gc_all = mask_dot((row >= col).astype(F32), g_all)
    beta = jnp.sum(jnp.where(lane == head, beta_all[None], 0.0), axis=-1, keepdims=True)
    gc = jnp.sum(jnp.where(lane == head + H, gc_all[None], 0.0), axis=-1, keepdims=True)
    gc_row = jnp.sum(jnp.where(eye, gc, 0.0), axis=1, keepdims=True)
    decay = jnp.where(tril, jnp.exp(jnp.where(tril, gc - gc_row, 0.0)), 0.0)
    g_last = jnp.sum(jnp.where(rowc == C - 1, gc, 0.0), axis=1, keepdims=True)
    qn = _l2n(q) * (HEAD ** -0.5)
    kn = _l2n(k)
    kb = kn * beta
    low = jnp.where(strict, beta * mdot(kn, kn, "bnt", False) * decay, 0.0)
    inv_rest = _unit_lower_inverse_minus_eye(low)
    eg = jnp.exp(gc)
    vb, kbe = v * beta, kb * eg
    u = vb + mdot(inv_rest, vb, "bnn", False)
    w = kbe + mdot(inv_rest, kbe, "bnn", False)
    attn = mdot(qn, kn, "bnt", False) * decay
    v_new = u - mdot(w, S, "bnn", False)
    o = mdot(qn * eg, S, "bnn", False) + mdot(attn, v_new, "bnn", False)
    k_dec = kn * jnp.exp(g_last - gc)
    return _rms_gate(o, gain, z), S * jnp.exp(g_last) + mdot(k_dec, v_new, "btn", False)


def _hgrn_step(St, qr, fr, vi, z, lb0, lb1, gain):
    H = St.shape[0]
    C, SB = HGRN_CHUNK, HGRN_SUB
    row = lax.broadcasted_iota(jnp.int32, (C, C), 0)
    col = lax.broadcasted_iota(jnp.int32, (C, C), 1)
    blk_start = row - (row & (SB - 1))
    in_blk_f = ((row >= col) & (col >= blk_start)).astype(F32)
    before_f = (col < blk_start).astype(F32)
    sums_f = jnp.concatenate([in_blk_f, before_f], axis=0)
    m = jnp.maximum(lb0, lb1)
    e0, e1 = jnp.exp(lb0 - m), jnp.exp(lb1 - m)
    lb = e1 / (e0 + e1)
    f = lb + (1.0 - lb) * _sigmoid(fr)
    q = _silu(qr)
    k = 1.0 - f
    logf = jnp.log(f)
    sums = mask_dot(sums_f, to_wide(logf))
    inner, start = to_heads(_rows(sums, 0, C), H), to_heads(_rows(sums, C, 2 * C), H)
    b = start + inner
    b_last = jnp.sum(logf, axis=1, keepdims=True)
    o = mdot(q * jnp.exp(b), St, "bnt", False)
    qt = q * jnp.exp(inner)
    parts = []
    for blk in range(C // SB):
        lo, n = blk * SB, (blk + 1) * SB
        ref = jnp.sum(_rows(start, lo, n), axis=1, keepdims=True) * (1.0 / SB)
        kt = k * jnp.exp(jnp.minimum(ref - b, EXP_CLAMP))
        part = mdot(_rows(qt, lo, n), kt, "bnt", False)
        t_idx = lax.broadcasted_iota(jnp.int32, (1, SB, C), 1) + lo
        s_idx = lax.broadcasted_iota(jnp.int32, (1, SB, C), 2)
        parts.append(jnp.where(s_idx <= t_idx, part, 0.0))
    o = o + mdot(jnp.concatenate(parts, axis=1), vi, "bnn", False)
    k_dec = k * jnp.exp(b_last - b)
    return _rms_gate(o, gain, z), St * jnp.exp(b_last) + mdot(vi, k_dec, "btn", False)


def _post_norm(s, x, g, b):
    r = ALPHA * x + s
    d = r - jnp.mean(r, axis=-1, keepdims=True)
    var = jnp.mean(d * d, axis=-1, keepdims=True)
    return d * lax.rsqrt(var + NORM_EPS) * g + b


def _post_gate(x1, gate_pre, pp):
    return x1 + pp * _sigmoid(gate_pre)


def _pick(dim, cands):
    for c in cands:
        if dim % c == 0:
            return c
    return dim


def _matmul_tiles(M, K, tn, a_bytes, b_bytes, has_add):
    for tk in (4096, 2048, 1536, 1152, 1024, 640, 512, 384, 256, 128):
        if K % tk:
            continue
        for tm in (2048, 1152, 1024, 512, 384, 256, 128):
            if M % tm:
                continue
            blocks = tm * tk * a_bytes + tk * tn * b_bytes + tm * tn * 4 * (2 if has_add else 1)
            if 2 * blocks + (tm * tn * 4 if tk < K else 0) <= MATMUL_VMEM and tm >= min(M, 1024):
                return tm, tk
    return _pick(M, (512, 256, 128)), _pick(K, (512, 256, 128))


def _matmul(a, b, kind, name, add=None, after=None, also_bf16=False):
    if kind == "nn":
        (M, K), N = a.shape, b.shape[1]
    elif kind == "nt":
        (M, K), N = a.shape, b.shape[0]
    else:
        (K, M), N = a.shape, b.shape[1]
    has_add = add is not None
    tn = _pick(N, (512, 640, 384, 256, 128))
    tm, tk = _matmul_tiles(M, K, tn, a.dtype.itemsize, b.dtype.itemsize, has_add)
    nk = K // tk
    a_spec = pl.BlockSpec((tk, tm), lambda i, j, k: (k, i)) if kind == "tn" else pl.BlockSpec((tm, tk), lambda i, j, k: (i, k))
    b_spec = pl.BlockSpec((tn, tk), lambda i, j, k: (j, k)) if kind == "nt" else pl.BlockSpec((tk, tn), lambda i, j, k: (k, j))
    o_spec = pl.BlockSpec((tm, tn), lambda i, j, k: (i, j))

    extra = ([add] if has_add else []) + ([after] if after is not None else [])
    extra_specs = ([o_spec] if has_add else []) + ([pl.BlockSpec(TOKEN_SHAPE, lambda i, j, k: (0, 0))] if after is not None else [])

    out_dtypes = (F32, BF16) if also_bf16 else (F32,)

    def body(a_ref, b_ref, *rest):
        outs = rest[len(extra):len(extra) + len(out_dtypes)]

        def write(val):
            if has_add:
                val = val + rest[0][...]
            for o_ref in outs:
                o_ref[...] = val.astype(o_ref.dtype)

        part = _dot_raw(a_ref[...], b_ref[...], kind, False)
        if nk == 1:
            write(part)
            return
        acc = rest[-1]
        kk = pl.program_id(2)

        @pl.when(kk == 0)
        def _():
            acc[...] = part

        @pl.when(kk > 0)
        def _():
            acc[...] += part

        @pl.when(kk == nk - 1)
        def _():
            write(acc[...])

    result = pl.pallas_call(
        body, name=name, grid=(M // tm, N // tn, nk),
        in_specs=[a_spec, b_spec] + extra_specs,
        out_specs=[o_spec] * len(out_dtypes), out_shape=[jax.ShapeDtypeStruct((M, N), dt) for dt in out_dtypes],
        scratch_shapes=[pltpu.VMEM((tm, tn), F32)] if nk > 1 else [],
        compiler_params=_params(dimension_semantics=("parallel", "parallel", "arbitrary")),
    )(a, b, *extra)
    return result if also_bf16 else result[0]


def _halo_specs(ts, nt, width, prev=True, main=True, nxt=True):
    per = ts // SUBLANES
    last8 = nt * per - 1
    specs = []
    if prev:
        specs.append(pl.BlockSpec((SUBLANES, width), lambda cb, i: (jnp.maximum(i * per - 1, 0), cb)))
    if main:
        specs.append(pl.BlockSpec((ts, width), lambda cb, i: (i, cb)))
    if nxt:
        specs.append(pl.BlockSpec((SUBLANES, width), lambda cb, i: (jnp.minimum((i + 1) * per, last8), cb)))
    return specs


def _taps(ext, ktaps, lo, size):
    return [ext[lo:lo + size] if j == 0 else pltpu.roll(ext, j, 0)[lo:lo + size] for j in range(ktaps)]


def _ahead(ext, j, size):
    n = ext.shape[0]
    return ext[:size] if j == 0 else pltpu.roll(ext, n - j, 0)[:size]


def _lane_block(ref, k):
    return ref[:, k * MIXER_LANES:(k + 1) * MIXER_LANES]


def _mixer_a_fwd(proj_a, conv_w):
    T = proj_a.shape[0]
    nblk = proj_a.shape[1] // (4 * MIXER_LANES)
    ts = min(ROW_TILE, T)
    nt = T // ts

    def body(pp, pm, w_ref, y_ref):
        i = pl.program_id(1)
        u_prev = jnp.where(i > 0, _lane_block(pp, 0) * _lane_block(pp, 1), 0.0)
        ext = jnp.concatenate([u_prev, _lane_block(pm, 0) * _lane_block(pm, 1)], axis=0)
        t0, t1, t2 = _taps(ext, 3, SUBLANES, ts)
        cv = w_ref[2:3, :] * t0 + w_ref[1:2, :] * t1 + w_ref[0:1, :] * t2
        y_ref[...] = (_lane_block(pm, 2) * cv * _silu(_lane_block(pm, 3))).astype(y_ref.dtype)

    return pl.pallas_call(
        body, name="mixer_a_fwd", grid=(nblk, nt),
        in_specs=_halo_specs(ts, nt, 4 * MIXER_LANES, nxt=False)
        + [pl.BlockSpec((conv_w.shape[0], MIXER_LANES), lambda cb, i: (0, cb))],
        out_specs=pl.BlockSpec((ts, MIXER_LANES), lambda cb, i: (i, cb)),
        out_shape=jax.ShapeDtypeStruct((T, nblk * MIXER_LANES), BF16), compiler_params=_params(),
    )(proj_a, proj_a, conv_w)


def _mixer_a_bwd(proj_a, dy, conv_w):
    T = proj_a.shape[0]
    nblk = proj_a.shape[1] // (4 * MIXER_LANES)
    ts = min(ROW_TILE, T)
    nt = T // ts
    kt = conv_w.shape[0]

    def body(pp, pm, pn, dym, dyn, w_ref, dp_ref, dw_ref):
        i = pl.program_id(1)
        hm, cm, bm, zm = (_lane_block(pm, k) for k in range(4))
        u_prev = jnp.where(i > 0, _lane_block(pp, 0) * _lane_block(pp, 1), 0.0)
        ext = jnp.concatenate([u_prev, hm * cm], axis=0)
        dy_ext = jnp.concatenate([dym[...], jnp.where(i < nt - 1, dyn[...], 0.0)], axis=0)
        b_ext = jnp.concatenate([bm, _lane_block(pn, 2)], axis=0)
        sz_ext = _silu(jnp.concatenate([zm, _lane_block(pn, 3)], axis=0))
        dcv_ext = dy_ext * b_ext * sz_ext
        w = [w_ref[j:j + 1, :] for j in range(kt)]
        du = sum(w[kt - 1 - j] * _ahead(dcv_ext, j, ts) for j in range(kt))
        taps = _taps(ext, kt, SUBLANES, ts)
        cv = sum(w[kt - 1 - j] * taps[j] for j in range(kt))
        for part, d in enumerate((du * cm, du * hm, dym[...] * cv * sz_ext[:ts], dym[...] * bm * cv * _dsilu(zm))):
            dp_ref[:, part * MIXER_LANES:(part + 1) * MIXER_LANES] = d.astype(dp_ref.dtype)
        dcv = dcv_ext[:ts]

        @pl.when(i == 0)
        def _():
            dw_ref[...] = jnp.zeros_like(dw_ref)

        for j in range(kt):
            dw_ref[j:j + 1, :] += jnp.sum(dcv * taps[kt - 1 - j], axis=0, keepdims=True)

    return pl.pallas_call(
        body, name="mixer_a_bwd", grid=(nblk, nt),
        in_specs=_halo_specs(ts, nt, 4 * MIXER_LANES) + _halo_specs(ts, nt, MIXER_LANES, prev=False)
        + [pl.BlockSpec((kt, MIXER_LANES), lambda cb, i: (0, cb))],
        out_specs=[pl.BlockSpec((ts, 4 * MIXER_LANES), lambda cb, i: (i, cb)),
                   pl.BlockSpec((SUBLANES, MIXER_LANES), lambda cb, i: (0, cb))],
        out_shape=[jax.ShapeDtypeStruct(proj_a.shape, BF16),
                   jax.ShapeDtypeStruct((SUBLANES, nblk * MIXER_LANES), F32)],
        compiler_params=_params(),
    )(proj_a, proj_a, proj_a, dy, dy, conv_w)


def _conv_b_fwd(raw, conv_w):
    T = raw.shape[0]
    nblk = raw.shape[1] // CONV_LANES
    ts = min(ROW_TILE, T)
    nt = T // ts
    kt = conv_w.shape[0]

    def body(rp, rm, w_ref, y_ref):
        i = pl.program_id(1)
        ext = jnp.concatenate([jnp.where(i > 0, rp[...], 0.0), rm[...]], axis=0)
        taps = _taps(ext, kt, SUBLANES, ts)
        y_ref[...] = _silu(sum(w_ref[kt - 1 - j:kt - j, :] * taps[j] for j in range(kt)))

    return pl.pallas_call(
        body, name="conv_b_fwd", grid=(nblk, nt),
        in_specs=_halo_specs(ts, nt, CONV_LANES, nxt=False) + [pl.BlockSpec((kt, CONV_LANES), lambda cb, i: (0, cb))],
        out_specs=pl.BlockSpec((ts, CONV_LANES), lambda cb, i: (i, cb)),
        out_shape=jax.ShapeDtypeStruct(raw.shape, F32), compiler_params=_params(),
    )(raw, raw, conv_w)


def _conv_b_bwd(raw, dy, conv_w):
    T = raw.shape[0]
    nblk = raw.shape[1] // CONV_LANES
    ts = min(ROW_TILE, T)
    nt = T // ts
    kt = conv_w.shape[0]

    def body(rp, rm, rn, dym, dyn, w_ref, dr_ref, dw_ref):
        i = pl.program_id(1)
        ext = jnp.concatenate([jnp.where(i > 0, rp[...], 0.0), rm[...], rn[...]], axis=0)
        w = [w_ref[j:j + 1, :] for j in range(kt)]
        taps = _taps(ext, kt, SUBLANES, ts + SUBLANES)
        xc_ext = sum(w[kt - 1 - j] * taps[j] for j in range(kt))
        dy_ext = jnp.concatenate([dym[...], jnp.where(i < nt - 1, dyn[...], 0.0)], axis=0)
        dxc_ext = dy_ext * _dsilu(xc_ext)
        dr_ref[...] = sum(w[kt - 1 - j] * _ahead(dxc_ext, j, ts) for j in range(kt)).astype(dr_ref.dtype)
        dxc = dxc_ext[:ts]

        @pl.when(i == 0)
        def _():
            dw_ref[...] = jnp.zeros_like(dw_ref)

        for j in range(kt):
            dw_ref[j:j + 1, :] += jnp.sum(dxc * taps[kt - 1 - j][:ts], axis=0, keepdims=True)

    return pl.pallas_call(
        body, name="conv_b_bwd", grid=(nblk, nt),
        in_specs=_halo_specs(ts, nt, CONV_LANES) + _halo_specs(ts, nt, CONV_LANES, prev=False)
        + [pl.BlockSpec((kt, CONV_LANES), lambda cb, i: (0, cb))],
        out_specs=[pl.BlockSpec((ts, CONV_LANES), lambda cb, i: (i, cb)),
                   pl.BlockSpec((SUBLANES, CONV_LANES), lambda cb, i: (0, cb))],
        out_shape=[jax.ShapeDtypeStruct(raw.shape, BF16), jax.ShapeDtypeStruct((SUBLANES, nblk * CONV_LANES), F32)],
        compiler_params=_params(),
    )(raw, raw, raw, dy, dy, conv_w)


def _split_heads(ref, base, nheads, rows=slice(None)):
    return jnp.stack([ref[rows, base + h * HEAD: base + (h + 1) * HEAD] for h in range(nheads)], axis=0)


def _store_heads(ref, base, x, rows=slice(None), accumulate=False):
    for h in range(x.shape[0]):
        lanes = slice(base + h * HEAD, base + (h + 1) * HEAD)
        if accumulate:
            ref[rows, lanes] += x[h]
        else:
            ref[rows, lanes] = x[h].astype(ref.dtype)


def _gdn_fwd(qkv, proj_zs, alog, dtb, gain, H):
    T = qkv.shape[0]
    C, HW = GDN_CHUNK, H * HEAD
    nc = T // C
    zw = HW + LANES

    def body(qkv_ref, zs_ref, alog_ref, dtb_ref, gain_ref, o_ref, sall_ref, s_scr):
        @pl.when(pl.program_id(0) == 0)
        def _():
            s_scr[...] = jnp.zeros_like(s_scr)

        sall_ref[0] = s_scr[...]
        outs, states = _gdn_step(
            s_scr[...], _split_heads(qkv_ref, 0, H), _split_heads(qkv_ref, HW, H),
            _split_heads(qkv_ref, 2 * HW, H), _split_heads(zs_ref, 0, H), zs_ref[:, HW:HW + LANES],
            alog_ref[...], dtb_ref[...], gain_ref[...])
        _store_heads(o_ref, 0, outs)
        s_scr[...] = states

    row = pl.BlockSpec((1, LANES), lambda i: (0, 0))
    return pl.pallas_call(
        body, name="gdn_fwd", grid=(nc,),
        in_specs=[pl.BlockSpec((C, 3 * HW), lambda i: (i, 0)), pl.BlockSpec((C, zw), lambda i: (i, 0)), row, row, row],
        out_specs=[pl.BlockSpec((C, HW), lambda i: (i, 0)), pl.BlockSpec((1, H, HEAD, HEAD), lambda i: (i, 0, 0, 0))],
        out_shape=[jax.ShapeDtypeStruct((T, HW), BF16), jax.ShapeDtypeStruct((nc, H, HEAD, HEAD), F32)],
        scratch_shapes=[pltpu.VMEM((H, HEAD, HEAD), F32)], compiler_params=_params(),
    )(qkv, proj_zs, alog, dtb, gain)


def _gdn_bwd(qkv, proj_zs, do, s_all, alog, dtb, gain, H):
    T = qkv.shape[0]
    C, HW = GDN_CHUNK, H * HEAD
    nc = T // C
    zw = HW + LANES

    def body(qkv_ref, zs_ref, do_ref, sin_ref, alog_ref, dtb_ref, gain_ref,
             dqkv_ref, dzs_ref, dalog_ref, ddtb_ref, dgain_ref, ds_scr):
        @pl.when(pl.program_id(0) == 0)
        def _():
            ds_scr[...] = jnp.zeros_like(ds_scr)
            dalog_ref[...] = jnp.zeros_like(dalog_ref)
            ddtb_ref[...] = jnp.zeros_like(ddtb_ref)
            dgain_ref[...] = jnp.zeros_like(dgain_ref)

        primals = (sin_ref[0], _split_heads(qkv_ref, 0, H),
                   _split_heads(qkv_ref, HW, H), _split_heads(qkv_ref, 2 * HW, H), _split_heads(zs_ref, 0, H),
                   zs_ref[:, HW:HW + LANES], alog_ref[...], dtb_ref[...], gain_ref[...])
        _, vjp = jax.vjp(_gdn_step, *primals)
        dS, dq, dk, dv, dz, dsmall, dalog, ddtb, dgain = vjp((_split_heads(do_ref, 0, H), ds_scr[...]))
        ds_scr[...] = dS
        _store_heads(dqkv_ref, 0, dq)
        _store_heads(dqkv_ref, HW, dk)
        _store_heads(dqkv_ref, 2 * HW, dv)
        _store_heads(dzs_ref, 0, dz)
        dzs_ref[:, HW:HW + LANES] = dsmall.astype(dzs_ref.dtype)
        dalog_ref[...] += dalog
        ddtb_ref[...] += ddtb
        dgain_ref[...] += dgain

    row = pl.BlockSpec((1, LANES), lambda i: (0, 0))
    rev = lambda i: nc - 1 - i
    return pl.pallas_call(
        body, name="gdn_bwd", grid=(nc,),
        in_specs=[pl.BlockSpec((C, 3 * HW), lambda i: (rev(i), 0)), pl.BlockSpec((C, zw), lambda i: (rev(i), 0)),
                  pl.BlockSpec((C, HW), lambda i: (rev(i), 0)),
                  pl.BlockSpec((1, H, HEAD, HEAD), lambda i: (rev(i), 0, 0, 0)), row, row, row],
        out_specs=[pl.BlockSpec((C, 3 * HW), lambda i: (rev(i), 0)), pl.BlockSpec((C, zw), lambda i: (rev(i), 0)),
                   row, row, row],
        out_shape=[jax.ShapeDtypeStruct(qkv.shape, F32), jax.ShapeDtypeStruct(proj_zs.shape, BF16)]
        + [jax.ShapeDtypeStruct((1, LANES), F32)] * 3,
        scratch_shapes=[pltpu.VMEM((H, HEAD, HEAD), F32)], compiler_params=_params(),
    )(qkv, proj_zs, do, s_all, alog, dtb, gain)


def _hgrn_refs(proj_ref, lb_ref, HP):
    W = HP * HEAD
    return (_split_heads(proj_ref, 0, HP), _split_heads(proj_ref, W, HP), _split_heads(proj_ref, 2 * W, HP),
            _split_heads(proj_ref, 3 * W, HP), _split_heads(lb_ref, 0, HP, slice(0, 1)),
            _split_heads(lb_ref, 0, HP, slice(1, 2)))


def _hgrn_fwd(proj, lower_bounds, gain, nheads):
    T = proj.shape[0]
    C, HP = HGRN_CHUNK, HGRN_HEADS_PER_STEP
    ng, nc, W = nheads // HP, T // C, HP * HEAD

    def body(proj_ref, lb_ref, gain_ref, o_ref, sall_ref, s_scr):
        @pl.when(pl.program_id(1) == 0)
        def _():
            s_scr[...] = jnp.zeros_like(s_scr)

        sall_ref[0] = s_scr[...]
        qr, fr, vi, z, lb0, lb1 = _hgrn_refs(proj_ref, lb_ref, HP)
        outs, states = _hgrn_step(s_scr[...], qr, fr, vi, z, lb0, lb1, gain_ref[...])
        _store_heads(o_ref, 0, outs)
        s_scr[...] = states

    return pl.pallas_call(
        body, name="hgrn_fwd", grid=(ng, nc),
        in_specs=[pl.BlockSpec((C, 4 * W), lambda g, i: (i, g)), pl.BlockSpec((2, W), lambda g, i: (0, g)),
                  pl.BlockSpec((1, LANES), lambda g, i: (0, 0))],
        out_specs=[pl.BlockSpec((C, W), lambda g, i: (i, g)),
                   pl.BlockSpec((1, HP, HEAD, HEAD), lambda g, i: (i, g, 0, 0))],
        out_shape=[jax.ShapeDtypeStruct((T, nheads * HEAD), BF16), jax.ShapeDtypeStruct((nc, nheads, HEAD, HEAD), F32)],
        scratch_shapes=[pltpu.VMEM((HP, HEAD, HEAD), F32)], compiler_params=_params(),
    )(proj, lower_bounds, gain)


def _hgrn_bwd(proj, do, s_all, lower_bounds, gain, nheads):
    T = proj.shape[0]
    C, HP = HGRN_CHUNK, HGRN_HEADS_PER_STEP
    ng, nc, W = nheads // HP, T // C, HP * HEAD

    def body(proj_ref, do_ref, sin_ref, lb_ref, gain_ref, dproj_ref, dlb_ref, dgain_ref, ds_scr):
        first = pl.program_id(1) == 0

        @pl.when(first)
        def _():
            ds_scr[...] = jnp.zeros_like(ds_scr)
            dlb_ref[...] = jnp.zeros_like(dlb_ref)

        @pl.when(first & (pl.program_id(0) == 0))
        def _():
            dgain_ref[...] = jnp.zeros_like(dgain_ref)

        qr, fr, vi, z, lb0, lb1 = _hgrn_refs(proj_ref, lb_ref, HP)
        primals = (sin_ref[0], qr, fr, vi, z, lb0, lb1, gain_ref[...])
        _, vjp = jax.vjp(_hgrn_step, *primals)
        dS, dq, df, dv, dz, dlb0, dlb1, dgain = vjp((_split_heads(do_ref, 0, HP), ds_scr[...]))
        ds_scr[...] = dS
        for part, d in enumerate((dq, df, dv, dz)):
            _store_heads(dproj_ref, part * W, d)
        _store_heads(dlb_ref, 0, dlb0, slice(0, 1), accumulate=True)
        _store_heads(dlb_ref, 0, dlb1, slice(1, 2), accumulate=True)
        dgain_ref[...] += dgain

    rev = lambda i: nc - 1 - i
    return pl.pallas_call(
        body, name="hgrn_bwd", grid=(ng, nc),
        in_specs=[pl.BlockSpec((C, 4 * W), lambda g, i: (rev(i), g)), pl.BlockSpec((C, W), lambda g, i: (rev(i), g)),
                  pl.BlockSpec((1, HP, HEAD, HEAD), lambda g, i: (rev(i), g, 0, 0)),
                  pl.BlockSpec((2, W), lambda g, i: (0, g)), pl.BlockSpec((1, LANES), lambda g, i: (0, 0))],
        out_specs=[pl.BlockSpec((C, 4 * W), lambda g, i: (rev(i), g)), pl.BlockSpec((2, W), lambda g, i: (0, g)),
                   pl.BlockSpec((1, LANES), lambda g, i: (0, 0))],
        out_shape=[jax.ShapeDtypeStruct(proj.shape, BF16), jax.ShapeDtypeStruct(lower_bounds.shape, F32),
                   jax.ShapeDtypeStruct((1, LANES), F32)],
        scratch_shapes=[pltpu.VMEM((HP, HEAD, HEAD), F32)], compiler_params=_params(),
    )(proj, do, s_all, lower_bounds, gain)


def _post_specs(T):
    tr = min(POST_TILE, T)
    tile = lambda w: pl.BlockSpec((tr, w), lambda i: (i, 0))
    full = lambda r, w: pl.BlockSpec((r, w), lambda i: (0, 0))
    return tr, tile, full


def _post_fwd(s, x, p, g, b, wg, wpl, name):
    T, D = x.shape
    P = p.shape[1]
    tr, tile, full = _post_specs(T)

    def body(s_ref, x_ref, p_ref, g_ref, b_ref, wg_ref, wpl_ref, o_ref, o16_ref):
        x1 = _post_norm(s_ref[...], x_ref[...], g_ref[...], b_ref[...])
        xn = _post_gate(x1, _dot_raw(x1, wg_ref[...], "nn", False), _dot_raw(p_ref[...], wpl_ref[...], "nn", False))
        o_ref[...] = xn
        o16_ref[...] = xn.astype(BF16)

    return pl.pallas_call(
        body, name=name, grid=(T // tr,),
        in_specs=[tile(D), tile(D), tile(P), full(1, D), full(1, D), full(D, D), full(P, D)],
        out_specs=[tile(D), tile(D)],
        out_shape=[jax.ShapeDtypeStruct((T, D), F32), jax.ShapeDtypeStruct((T, D), BF16)], compiler_params=_params(),
    )(s, x, p, g, b, wg, wpl)


def _post_bwd(s, x, p, g, b, wg, wpl, dnext, name, with_loss):
    T, D = x.shape
    P = p.shape[1]
    tr, tile, full = _post_specs(T)

    def body(s_ref, x_ref, p_ref, g_ref, b_ref, wg_ref, wpl_ref, dn_ref,
             ds_ref, dx_ref, dg_ref, db_ref, dwg_ref, dwpl_ref, loss_ref):
        @pl.when(pl.program_id(0) == 0)
        def _():
            for r in (dg_ref, db_ref, dwg_ref, dwpl_ref, loss_ref):
                r[...] = jnp.zeros_like(r)

        x1, vjp_norm = jax.vjp(_post_norm, s_ref[...], x_ref[...], g_ref[...], b_ref[...])
        gate_pre = _dot_raw(x1, wg_ref[...], "nn", False)
        pp = _dot_raw(p_ref[...], wpl_ref[...], "nn", False)
        xn, vjp_gate = jax.vjp(_post_gate, x1, gate_pre, pp)
        if with_loss:
            err = xn - dn_ref[...]
            loss_ref[...] += 0.5 * jnp.sum(jnp.sum(err * err, axis=-1, keepdims=True), axis=0, keepdims=True) / D
            dn = err / D
        else:
            dn = dn_ref[...]
        dx1, dgp, dpp = vjp_gate(dn)
        dwg_ref[...] += _dot_raw(x1, dgp, "tn", False)
        dwpl_ref[...] += _dot_raw(p_ref[...], dpp, "tn", False)
        dx1 = dx1 + _dot_raw(dgp, wg_ref[...], "nt", False)
        ds, dx, dg, db = vjp_norm(dx1)
        ds_ref[...] = ds.astype(ds_ref.dtype)
        dx_ref[...] = dx
        dg_ref[...] += dg
        db_ref[...] += db

    return pl.pallas_call(
        body, name=name, grid=(T // tr,),
        in_specs=[tile(D), tile(D), tile(P), full(1, D), full(1, D), full(D, D), full(P, D), tile(D)],
        out_specs=[tile(D), tile(D), full(1, D), full(1, D), full(D, D), full(P, D), full(SUBLANES, LANES)],
        out_shape=[jax.ShapeDtypeStruct((T, D), BF16), jax.ShapeDtypeStruct((T, D), F32)]
        + [jax.ShapeDtypeStruct((1, D), F32)] * 2
        + [jax.ShapeDtypeStruct((D, D), F32), jax.ShapeDtypeStruct((P, D), F32),
           jax.ShapeDtypeStruct((SUBLANES, LANES), F32)],
        compiler_params=_params(),
    )(s, x, p, g, b, wg, wpl, dnext)


def _adam_math(w, g, m, v):
    m = ADAM_B1 * m + (1.0 - ADAM_B1) * g
    v = ADAM_B2 * v + (1.0 - ADAM_B2) * (g * g)
    m_hat = m / (1.0 - ADAM_B1 ** ADAM_STEP)
    v_hat = v / (1.0 - ADAM_B2 ** ADAM_STEP)
    return -ADAM_LR * (m_hat / (jnp.sqrt(v_hat) + ADAM_EPS) + ADAM_WD * w), m, v


def _shard_tiles(R, C):
    tr = _pick(R, (256, 128, 64, 32, 16, 8))
    return (tr, C) if tr < R or R % SUBLANES == 0 else (R, _pick(C, (256, 128)))


def _adam_sharded(w, m, v, g8, got, me, name):
    rows_apart = w.shape[1] == 1 and w.shape[0] > 1
    (L, R, C) = (1, w.shape[0], w.shape[2]) if rows_apart else w.shape
    tr, tc = (R, LANES) if rows_apart else _shard_tiles(R, C)
    nr, nc = R // tr, C // tc
    side_by_side = g8.ndim == 2

    def body(me_ref, w_ref, m_ref, v_ref, p_ref, *rest):
        got_refs, (g_ref, d_ref, mo_ref, vo_ref) = rest[:7], rest[7:]
        g = p_ref[...] if side_by_side else p_ref[0]
        for r in got_refs:
            g = g + r[0].astype(F32)
        if rows_apart:
            d, mn, vn = _adam_math(w_ref[:, 0, :], g, m_ref[:, 0, :], v_ref[:, 0, :])
            for ref, val in ((g_ref, g), (d_ref, d), (mo_ref, mn), (vo_ref, vn)):
                ref[:, 0, :] = val
            return
        d, mn, vn = _adam_math(w_ref[0], g, m_ref[0], v_ref[0])
        g_ref[0] = g
        d_ref[0] = d
        mo_ref[0] = mn
        vo_ref[0] = vn

    if rows_apart:
        t3 = pl.BlockSpec((tr, 1, tc), lambda l, i, j, q: (i, 0, j))
    else:
        t3 = pl.BlockSpec((1, tr, tc), lambda l, i, j, q: (l, i, j))
    slot = lambda k: pl.BlockSpec((1, tr, tc), lambda l, i, j, q: (k, l * nr + i, j))
    if side_by_side:
        mine = pl.BlockSpec((tr, tc), lambda l, i, j, q: (l * nr + i, q[0] * nc + j))
    else:
        mine = pl.BlockSpec((1, tr, tc), lambda l, i, j, q: (q[0], l * nr + i, j))
    return pl.pallas_call(
        body, name=name,
        grid_spec=pltpu.PrefetchScalarGridSpec(
            num_scalar_prefetch=1, grid=(L, nr, nc),
            in_specs=[t3, t3, t3, mine] + [slot(k) for k in range(7)], out_specs=[t3, t3, t3, t3]),
        out_shape=[jax.ShapeDtypeStruct(w.shape, F32)] * 4, compiler_params=_params(),
    )(me, w, m, v, g8, *([got] * 7))


def _small_rows(shapes):
    offsets, r = [], 0
    for rows, _ in shapes:
        offsets.append(r)
        r += rows
    return offsets, -(-(r + 1) // SUBLANES) * SUBLANES, max(cols for _, cols in shapes)


def _pack_small_grads(grads, loss):
    offsets, total, width = _small_rows([g.shape for g in grads])
    packed = jnp.zeros((total, width), F32)
    for g, r in zip(grads, offsets):
        packed = lax.dynamic_update_slice(packed, g, (r, 0))
    return lax.dynamic_update_slice(packed, loss.reshape(1, 1), (total - 1, 0))


def _adam_replicated(params, g8):
    shapes = [w.shape for w, _, _ in params]
    offsets, total, width = _small_rows(shapes)
    n = len(params)

    def body(*refs):
        g_ref, outs, loss_ref, g_scr = refs[3 * n], refs[3 * n + 1:7 * n + 1], refs[7 * n + 1], refs[7 * n + 2]
        g = g_ref[0]
        for k in range(1, 8):
            g = g + g_ref[k]
        g_scr[...] = g
        for i, (rows, cols) in enumerate(shapes):
            gp = g_scr[offsets[i]:offsets[i] + rows, 0:cols]
            d, mn, vn = _adam_math(refs[3 * i][...], gp, refs[3 * i + 1][...], refs[3 * i + 2][...])
            for ref, val in zip(outs[4 * i:4 * i + 4], (gp, d, mn, vn)):
                ref[...] = val
        loss_ref[...] = g_scr[total - 1:total, 0:LANES]

    out = pl.pallas_call(
        body, name="adam_replicated",
        out_shape=[jax.ShapeDtypeStruct(shp, F32) for shp in shapes for _ in range(4)]
        + [jax.ShapeDtypeStruct((1, LANES), F32)],
        scratch_shapes=[pltpu.VMEM((total, width), F32)], compiler_params=_params(),
    )(*[a for triple in params for a in triple], g8)
    return [out[4 * i:4 * i + 4] for i in range(n)], out[4 * n][0, 0]


def _place():
    return lax.axis_index("x"), lax.axis_index("y"), lax.axis_index("c")


def _all_gather(shard, name):
    def body(x_ref, out_ref, send_sems, recv_sems, local_sem):
        x, y, c = _place()
        me, sibling = (x, y, c), (x, y, 1 - c)
        chips = [(1 - x, y), (x, 1 - y), (1 - x, 1 - y)]

        def slab(px, py, pc):
            return out_ref.at[4 * px + 2 * py + pc]

        def copy(k, block, to, src=None):
            return pltpu.make_async_remote_copy(
                src_ref=slab(*block) if src is None else src, dst_ref=slab(*block),
                send_sem=send_sems.at[k], recv_sem=recv_sems.at[k], device_id=to, device_id_type=MESH)

        mine = pltpu.make_async_copy(x_ref, slab(*me), local_sem)
        mine.start()
        first = [copy(0, me, sibling, src=x_ref)]
        first += [copy(1 + j, me, (*chip, c), src=x_ref) for j, chip in enumerate(chips)]
        for cp in first:
            cp.start()
        passed = [copy(4 + j, (*chip, c), sibling) for j, chip in enumerate(chips)]
        for j, chip in enumerate(chips):
            copy(1 + j, (*chip, c), me).wait_recv()
            passed[j].start()
        copy(0, sibling, me).wait_recv()
        for j, chip in enumerate(chips):
            copy(4 + j, (*chip, 1 - c), me).wait_recv()
        for cp in first + passed:
            cp.wait_send()
        mine.wait()

    return pl.pallas_call(
        body, name=name, out_shape=jax.ShapeDtypeStruct((8,) + shard.shape, shard.dtype),
        in_specs=[pl.BlockSpec(memory_space=pl.ANY)], out_specs=pl.BlockSpec(memory_space=pl.ANY),
        scratch_shapes=[pltpu.SemaphoreType.DMA((7,)), pltpu.SemaphoreType.DMA((7,)), pltpu.SemaphoreType.DMA],
    )(shard)


_HBM = pl.BlockSpec(memory_space=pltpu.HBM)
_SEM = pl.BlockSpec(memory_space=pltpu.SEMAPHORE)
_DATAFLOW = pltpu.SideEffectType.DATAFLOW_SIDE_EFFECTING
TOKEN_SHAPE = (SUBLANES, LANES)


def _peers(x, y, c):
    flip = lambda v, bit: 1 - v if bit else v
    return [(flip(x, r >> 2 & 1), flip(y, r >> 1 & 1), flip(c, r & 1)) for r in range(1, 8)]


def _scatter_plan(x, y, c):
    return [(4 * px + 2 * py + pc, k, (px, py, pc)) for k, (px, py, pc) in enumerate(_peers(x, y, c))]


def _exchange_copies(plan, src_ref, land_ref, send_sems, recv_sems):
    C = land_ref.shape[-1]
    block = (lambda b: src_ref.at[b]) if len(src_ref.shape) == len(land_ref.shape) else (
        lambda b: src_ref.at[:, pl.ds(b * C, C)])
    return [pltpu.make_async_remote_copy(
        src_ref=block(blk), dst_ref=land_ref.at[slot], send_sem=send_sems.at[k], recv_sem=recv_sems.at[k],
        device_id=peer, device_id_type=MESH) for k, (blk, slot, peer) in enumerate(plan(*_place()))]


def _exchange_start_many(srcs, n_slots, block_shapes, plan, name):
    k, n = len(srcs), len(plan(0, 0, 0))
    land_shapes = [(n_slots,) + tuple(bs) for bs in block_shapes]

    def body(*refs):
        src_refs, land_refs, outs = refs[:k], refs[k:2 * k], refs[2 * k:]
        for i in range(k):
            for cp in _exchange_copies(plan, src_refs[i], land_refs[i], outs[2 * i], outs[2 * i + 1]):
                cp.start()
        outs[-1][...] = jnp.zeros_like(outs[-1])

    sems = [pltpu.SemaphoreType.DMA((n,))] * (2 * k)
    out = pl.pallas_call(
        body, name=name,
        out_shape=(*sems, *[pltpu.HBM(s.shape, s.dtype) for s in srcs],
                   *[pltpu.HBM(ls, s.dtype) for ls, s in zip(land_shapes, srcs)],
                   jax.ShapeDtypeStruct(TOKEN_SHAPE, F32)),
        in_specs=(_HBM,) * (2 * k),
        out_specs=(*[_SEM] * (2 * k), *[_HBM] * (2 * k), pl.BlockSpec(memory_space=pltpu.VMEM)),
        input_output_aliases={i: 2 * k + i for i in range(2 * k)},
        compiler_params=pltpu.CompilerParams(has_side_effects=_DATAFLOW),
    )(*[pltpu.with_memory_space_constraint(s, pltpu.HBM) for s in srcs],
      *[pltpu.with_memory_space_constraint(lax.empty(ls, s.dtype), pltpu.HBM) for ls, s in zip(land_shapes, srcs)])
    return [(out[2 * i], out[2 * i + 1], out[2 * k + i], out[3 * k + i], out[-1]) for i in range(k)]


def _exchange_wait(handle, plan, after, name):
    send_sems, recv_sems, src_thru, land_thru, _ = handle

    def body(src_ref, land_ref, send_sems, recv_sems, after_ref, src_dead, got_ref):
        for cp in _exchange_copies(plan, src_ref, land_ref, send_sems, recv_sems):
            cp.wait_send()
            cp.wait_recv()

    return pl.pallas_call(
        body, name=name,
        out_shape=(pltpu.HBM(src_thru.shape, src_thru.dtype), pltpu.HBM(land_thru.shape, land_thru.dtype)),
        in_specs=(_HBM, _HBM, _SEM, _SEM, pl.BlockSpec(memory_space=pl.ANY)), out_specs=(_HBM, _HBM),
        input_output_aliases={0: 0, 1: 1}, compiler_params=pltpu.CompilerParams(has_side_effects=_DATAFLOW),
    )(src_thru, land_thru, send_sems, recv_sems, after)[1]


def _gather_plan(x, y, c):
    return [(0, 4 * x + 2 * y + c, peer) for peer in _peers(x, y, c)]


class _LateGather:
    def __init__(self, shard, name, handle):
        self.shard, self.name, self.handle = shard, name, handle

    @classmethod
    def start_all(cls, shards, names):
        handles = _exchange_start_many([s[None] for s in shards], 8, [s.shape for s in shards], _gather_plan,
                                       "ag_late_start")
        return [cls(s, n, h) for s, n, h in zip(shards, names, handles)]

    def get(self, after):
        land = _exchange_wait(self.handle, _gather_plan, after, self.name + "_wait")
        x, y, c = _place()
        return lax.dynamic_update_slice(land, self.shard[None], (4 * x + 2 * y + c, 0, 0))


class _GradExchange:
    def __init__(self, me, layouts, held_for):
        self.me, self.layouts, self.held_for, self.held, self.pending = me, layouts, held_for, [], {}

    def start(self, tag, grad, grad16=None):
        g8 = self.layouts[tag](grad)
        g16 = g8.astype(BF16) if grad16 is None else self.layouts[tag](grad16)
        block_shape = g8.shape[1:] if g8.ndim == 3 else (g8.shape[0], g8.shape[1] // 8)
        self.held.append((tag, g8, g16, block_shape))
        if tag in self.held_for:
            return jnp.zeros(TOKEN_SHAPE, F32)
        held, self.held = self.held, []
        handles = _exchange_start_many([h[2] for h in held], 7, [h[3] for h in held], _scatter_plan,
                                       "rs_start_" + tag)
        for (t, g8_t, _, _), handle in zip(held, handles):
            self.pending[t] = (g8_t, handle)
        return handles[0][4]

    def finish(self, tag, w, m, v, after):
        g8, handle = self.pending.pop(tag)
        got = _exchange_wait(handle, _scatter_plan, after, "rs_wait_" + tag)
        return _adam_sharded(w, m, v, g8, got, self.me, "adam_" + tag)


def _local_grads(x, p0, p1, target, wt_zs, wt_a, wt_qkv, late, conv_a, conv_b,
                 a_log, dt_bias, gdn_gain, lower_bounds, hgrn_gain, ln_g, ln_b, on_grad=None):
    H = a_log.shape[1]
    pad_small = ((0, 0), (H, LANES - 2 * H))
    alog_row = jnp.pad(a_log, pad_small)
    dtb_row = jnp.pad(dt_bias, pad_small)

    x16 = x.astype(BF16)
    proj_zs = _matmul(x16, wt_zs, "nt", "proj_even_zs", after=late.started)
    proj_a = _matmul(x16, wt_a, "nt", "proj_even_a", after=late.started)
    proj_qkv = _matmul(x16, wt_qkv, "nt", "proj_even_qkv", after=late.started)
    y_a = _mixer_a_fwd(proj_a, conv_a)
    qkv = _conv_b_fwd(proj_qkv, conv_b)
    o2, s_gdn = _gdn_fwd(qkv, proj_zs, alog_row, dtb_row, gdn_gain, H)
    woute_a, woute_b = late.out_even(o2)
    wg, wpl = late.gate(o2)
    s_e = _matmul(o2, woute_b, "nn", "out_even_b", add=_matmul(y_a, woute_a, "nn", "out_even_a"))
    x2, x2_16 = _post_fwd(s_e, x, p0, ln_g[0:1], ln_b[0:1], wg[0], wpl[0], "post_even_fwd")
    wino, wouto = late.odd(s_e)
    nheads_o = wouto.shape[0] // HEAD
    proj_o = _matmul(x2_16, wino, "nn", "proj_odd")
    o4, s_hgrn = _hgrn_fwd(proj_o, lower_bounds, hgrn_gain, nheads_o)
    s_o = _matmul(o4, wouto, "nn", "out_odd")
    ds_o, dx2, dlng1, dlnb1, dwg1, dwpl1, loss = _post_bwd(
        s_o, x2, p1, ln_g[1:2], ln_b[1:2], wg[1], wpl[1], target, "post_odd_loss_bwd", True)
    do4 = _matmul(ds_o, wouto, "nt", "d_out_odd_act")
    grads = {}

    def emit(tag, grad, grad16=None):
        grads[tag] = grad
        return on_grad(tag, grad, grad16) if on_grad is not None else jnp.zeros(TOKEN_SHAPE, F32)

    tok = emit("w_out_odd", *_matmul(o4, ds_o, "tn", "d_out_odd_w", also_bf16=True))
    dproj_o, dlb, dhgain = _hgrn_bwd(proj_o, do4, s_hgrn, lower_bounds, hgrn_gain + tok[0:1], nheads_o)
    dx2 = _matmul(dproj_o, wino, "nt", "d_proj_odd_act", add=dx2)
    tok = emit("w_in_odd", *_matmul(x2_16, dproj_o, "tn", "d_proj_odd_w", also_bf16=True))
    ds_e, dx, dlng0, dlnb0, dwg0, dwpl0, _ = _post_bwd(
        s_e, x, p0, ln_g[0:1], ln_b[0:1] + tok[0:1, 0:1], wg[0], wpl[0], dx2, "post_even_bwd", False)
    tok = emit("w_pl_gate", jnp.stack([dwg0, dwg1])) + emit("w_pl", jnp.stack([dwpl0, dwpl1]))
    dy_a = _matmul(ds_e, woute_a, "nt", "d_out_even_a_act")
    do2 = _matmul(ds_e, woute_b, "nt", "d_out_even_b_act")
    dwoute_a = _matmul(y_a, ds_e, "tn", "d_out_even_a_w")
    dwoute_b = _matmul(o2, ds_e, "tn", "d_out_even_b_w")
    tok = tok + emit("w_out_even", jnp.concatenate([dwoute_a, dwoute_b], axis=0))
    dqkv, dproj_zs, dalog, ddtb, dggain = _gdn_bwd(qkv, proj_zs, do2, s_gdn, alog_row, dtb_row, gdn_gain + tok[0:1], H)
    dproj_qkv, dconv_b = _conv_b_bwd(proj_qkv, dqkv, conv_b)
    dproj_a, dconv_a = _mixer_a_bwd(proj_a, dy_a, conv_a)
    emit("conv", (dconv_a[:conv_a.shape[0]], dconv_b[:conv_b.shape[0]]))
    tok = emit("w_in_even", (_matmul(dproj_zs, x16, "tn", "d_proj_even_zs_w"), _matmul(dproj_a, x16, "tn", "d_proj_even_a_w"),
                             _matmul(dproj_qkv, x16, "tn", "d_proj_even_qkv_w")))
    dx = _matmul(dproj_zs, wt_zs, "nn", "d_proj_even_zs_act", add=dx, after=tok)
    dx = _matmul(dproj_a, wt_a, "nn", "d_proj_even_a_act", add=dx)
    dx = _matmul(dproj_qkv, wt_qkv, "nn", "d_proj_even_qkv_act", add=dx)
    grads.update(
        loss=loss[0, 0], grad_x=dx, a_log=dalog[:, H:2 * H], dt_bias=ddtb[:, H:2 * H], gdn_gain=dggain,
        lower_bounds=dlb, hgrn_gain=dhgain, ln_g=jnp.concatenate([dlng0, dlng1], axis=0),
        ln_b=jnp.concatenate([dlnb0, dlnb1], axis=0))
    return grads


def _pad_rows(a, rows):
    return jnp.pad(a, ((0, rows - a.shape[0]), (0, 0)))


def _split_in_even(wt_full, AW, HW, H):
    D = wt_full.shape[1]
    n_a = 4 * AW
    n_main = n_a + 3 * HW
    wt_zs = jnp.concatenate([wt_full[n_main:n_main + HW], wt_full[n_main + HW:],
                             jnp.zeros((LANES - 2 * H, D), wt_full.dtype)], axis=0)
    wt_a = wt_full[:n_a].reshape(4, AW // MIXER_LANES, MIXER_LANES, D).transpose(1, 0, 2, 3).reshape(n_a, D)
    return wt_zs, wt_a, wt_full[n_a:n_main]


def _join_in_even(dt_zs, dt_a, dt_qkv, AW, HW, H):
    D = dt_a.shape[1]
    a_nat = dt_a.reshape(AW // MIXER_LANES, 4, MIXER_LANES, D).transpose(1, 0, 2, 3).reshape(4 * AW, D)
    return jnp.concatenate([a_nat, dt_qkv, dt_zs[:HW], dt_zs[HW:HW + 2 * H]], axis=0)


def kernel(x, p, w_in_even, conv_a_w, conv_b_w, a_log, dt_bias, gdn_norm_g, w_out_even, w_in_odd, lower_bounds, hgrn_norm_g, w_out_odd, ln_g, ln_b, w_pl, w_pl_gate, loss_target, m_w_in_even, m_conv_a_w, m_conv_b_w, m_a_log, m_dt_bias, m_gdn_norm_g, m_w_out_even, m_w_in_odd, m_lower_bounds, m_hgrn_norm_g, m_w_out_odd, m_ln_g, m_ln_b, m_w_pl, m_w_pl_gate, v_w_in_even, v_conv_a_w, v_conv_b_w, v_a_log, v_dt_bias, v_gdn_norm_g, v_w_out_even, v_w_in_odd, v_lower_bounds, v_hgrn_norm_g, v_w_out_odd, v_ln_g, v_ln_b, v_w_pl, v_w_pl_gate):
    xi, yi, ci = _place()
    me = jnp.reshape(4 * xi + 2 * yi + ci, (1,)).astype(jnp.int32)
    D = x.shape[2]
    H = a_log.shape[1]
    HW = H * HEAD
    AW = conv_a_w.shape[2] * 8
    OW = w_out_odd.shape[1] * 8
    PD = w_pl.shape[1]
    ka, kb = conv_a_w.shape[1], conv_b_w.shape[1]
    ca, cb = conv_a_w.shape[2], conv_b_w.shape[2]
    gw = HGRN_HEADS_PER_STEP * HEAD
    ngrp = OW // gw

    transposed = lambda a: jnp.transpose(a, (0, 2, 1))
    g_ine = _all_gather(transposed(w_in_even)[0].astype(BF16), "ag_w_in_even")
    wt_zs, wt_a, wt_qkv = _split_in_even(g_ine.reshape(-1, D), AW, HW, H)
    taps = jnp.concatenate([_pad_rows(conv_a_w[0], SUBLANES), _pad_rows(conv_b_w[0], SUBLANES)], axis=1)
    g_taps = _all_gather(taps, "ag_conv")
    conv_a = jnp.transpose(g_taps[:, :ka, :ca], (1, 0, 2)).reshape(ka, 8 * ca)
    conv_b = jnp.transpose(g_taps[:, :kb, ca:], (1, 0, 2)).reshape(kb, 8 * cb)
    behind = lambda shard, dep: lax.optimization_barrier((shard, dep))[0]
    late_oute, late_gate, late_pl, late_ino, late_outo = _LateGather.start_all(
        [behind(w_out_even[0].astype(BF16), (g_ine, g_taps)), w_pl_gate.astype(BF16).reshape(-1, D),
         w_pl.astype(BF16).reshape(DEPTH * PD, -1), w_in_odd[0].astype(BF16), w_out_odd[0].astype(BF16)],
        ["ag_w_out_even", "ag_w_pl_gate", "ag_w_pl", "ag_w_in_odd", "ag_w_out_odd"])

    class _Late:
        started = late_oute.handle[4]

        @staticmethod
        def out_even(after):
            woute = late_oute.get(after).reshape(-1, D)
            return woute[:AW], woute[AW:]

        @staticmethod
        def gate(after):
            g_gate, g_pl = late_gate.get(after), late_pl.get(after)
            return (g_gate.reshape(8, DEPTH, D // 8, D).transpose(1, 0, 2, 3).reshape(DEPTH, D, D),
                    g_pl.reshape(8, DEPTH, PD, D // 8).transpose(1, 2, 0, 3).reshape(DEPTH, PD, D))

        @staticmethod
        def odd(after):
            g_ino = late_ino.get(after)
            wino = jnp.transpose(g_ino, (1, 0, 2)).reshape(D, 4, ngrp, gw).transpose(0, 2, 1, 3).reshape(D, 4 * OW)
            return wino, late_outo.get(after).reshape(-1, D)

    sh = w_in_even.shape[2]
    tap_blocks = lambda g, width: _pad_rows(g, SUBLANES).reshape(SUBLANES, 8, width).transpose(1, 0, 2)
    owner_layout = {
        "w_in_even": lambda g: _join_in_even(*g, AW, HW, H).reshape(8, sh, D),
        "w_in_odd": lambda g: g.reshape(D, ngrp, 4, gw).transpose(0, 2, 1, 3).reshape(D, 4 * OW),
        "w_out_even": lambda g: g.reshape(8, -1, D),
        "w_out_odd": lambda g: g.reshape(8, -1, D),
        "w_pl_gate": lambda g: g.reshape(DEPTH, 8, D // 8, D).transpose(1, 0, 2, 3).reshape(8, DEPTH * D // 8, D),
        "w_pl": lambda g: g.reshape(DEPTH, PD, 8, D // 8).transpose(2, 0, 1, 3).reshape(8, DEPTH * PD, D // 8),
        "conv": lambda g: jnp.concatenate([tap_blocks(g[0], ca), tap_blocks(g[1], cb)], axis=2),
    }
    exchange = _GradExchange(me, owner_layout, held_for={"w_pl_gate": "w_out_even", "w_pl": "w_out_even"})
    gr = _local_grads(x[0], p[0, 0], p[1, 0], loss_target[0], wt_zs, wt_a, wt_qkv, _Late, conv_a, conv_b,
                      a_log, dt_bias, gdn_norm_g, lower_bounds, hgrn_norm_g, ln_g, ln_b, on_grad=exchange.start)

    last = gr["grad_x"]
    pack_taps = lambda a, b: jnp.concatenate([_pad_rows(a[0], SUBLANES), _pad_rows(b[0], SUBLANES)], axis=1)[None]
    o_outo = exchange.finish("w_out_odd", w_out_odd, m_w_out_odd, v_w_out_odd, last)
    o_ino = exchange.finish("w_in_odd", w_in_odd, m_w_in_odd, v_w_in_odd, last)
    o_gate = exchange.finish("w_pl_gate", w_pl_gate, m_w_pl_gate, v_w_pl_gate, last)
    o_pl = exchange.finish("w_pl", w_pl, m_w_pl, v_w_pl, last)
    o_oute = exchange.finish("w_out_even", w_out_even, m_w_out_even, v_w_out_even, last)
    o_taps = exchange.finish("conv", taps[None], pack_taps(m_conv_a_w, m_conv_b_w), pack_taps(v_conv_a_w, v_conv_b_w), last)
    others_done = sum(o[1][0, 0:1, 0:1] for o in (o_outo, o_ino, o_gate, o_pl, o_oute, o_taps))
    rows_first = lambda a: jnp.transpose(a, (2, 0, 1))
    o_ine = [jnp.transpose(o, (1, 2, 0)) for o in exchange.finish(
        "w_in_even", rows_first(w_in_even), rows_first(m_w_in_even), rows_first(v_w_in_even), others_done)]

    small_g = _pack_small_grads([gr["a_log"], gr["dt_bias"], gr["gdn_gain"], gr["lower_bounds"], gr["hgrn_gain"],
                                 gr["ln_g"], gr["ln_b"]], gr["loss"])
    o_small, loss = _adam_replicated(
        [(a_log, m_a_log, v_a_log), (dt_bias, m_dt_bias, v_dt_bias), (gdn_norm_g, m_gdn_norm_g, v_gdn_norm_g),
         (lower_bounds, m_lower_bounds, v_lower_bounds), (hgrn_norm_g, m_hgrn_norm_g, v_hgrn_norm_g),
         (ln_g, m_ln_g, v_ln_g), (ln_b, m_ln_b, v_ln_b)],
        _all_gather(behind(small_g, o_ine[0]), "ag_small_grads"))

    def leaves(kind):
        s_alog, s_dt, s_gg, s_lb, s_hg, s_lng, s_lnb = (o[kind] for o in o_small)
        t = o_taps[kind]
        return [o_ine[kind], t[:, :ka, :ca], t[:, :kb, ca:], s_alog, s_dt, s_gg, o_oute[kind],
                o_ino[kind], s_lb, s_hg, o_outo[kind], s_lng, s_lnb, o_pl[kind], o_gate[kind]]

    return (loss, gr["grad_x"][None], *leaves(0), *leaves(1), *leaves(2), *leaves(3))
```

```python
import functools

import jax
import jax.numpy as jnp
from jax import lax
from jax.experimental import pallas as pl
from jax.experimental.pallas import tpu as pltpu

F32 = jnp.float32
BF16 = jnp.bfloat16
MESH = pl.DeviceIdType.MESH

LANES = 128
SUBLANES = 8
HEAD = 128
GDN_CHUNK = 128
HGRN_CHUNK = 64
HGRN_SUB = 16
HGRN_HEADS_PER_STEP = 16
NORM_EPS = 1e-5
DEPTH = 2
ALPHA = (2.0 * DEPTH) ** 0.25
EXP_CLAMP = 80.0
ADAM_LR, ADAM_B1, ADAM_B2, ADAM_EPS, ADAM_WD, ADAM_STEP = 0.001, 0.9, 0.999, 1e-08, 0.01, 10
VMEM_LIMIT = 56 * 1024 * 1024
MATMUL_VMEM = 36 * 1024 * 1024
ROW_TILE = 1024
MIXER_LANES = 256
CONV_LANES = 512
POST_TILE = 512

_NOBATCH, _BATCH0 = ((), ()), ((0,), (0,))
_DIMS = {"nn": (((1,), (0,)), _NOBATCH), "nt": (((1,), (1,)), _NOBATCH), "tn": (((0,), (0,)), _NOBATCH),
         "bnn": (((2,), (1,)), _BATCH0), "bnt": (((2,), (2,)), _BATCH0), "btn": (((1,), (1,)), _BATCH0)}


def _params(**kw):
    return pltpu.CompilerParams(vmem_limit_bytes=VMEM_LIMIT, **kw)


def _dot_raw(a, b, kind, hi):
    if hi:
        return lax.dot_general(a, b, _DIMS[kind], precision=lax.Precision.HIGHEST, preferred_element_type=F32)
    return lax.dot_general(a.astype(BF16), b.astype(BF16), _DIMS[kind], preferred_element_type=F32)


@functools.partial(jax.custom_vjp, nondiff_argnums=(2, 3))
def mdot(a, b, kind, hi):
    return _dot_raw(a, b, kind, hi)


def _mdot_fwd(a, b, kind, hi):
    return _dot_raw(a, b, kind, hi), (a, b)


def _mdot_bwd(kind, hi, res, g):
    a, b = res
    pre, base = kind[:-2], kind[-2:]
    if base == "nn":
        return _dot_raw(g, b, pre + "nt", hi), _dot_raw(a, g, pre + "tn", hi)
    if base == "nt":
        return _dot_raw(g, b, pre + "nn", hi), _dot_raw(g, a, pre + "tn", hi)
    return _dot_raw(b, g, pre + "nt", hi), _dot_raw(a, g, pre + "nn", hi)


mdot.defvjp(_mdot_fwd, _mdot_bwd)


def _rows(x, lo, hi):
    return _take_rows(x, lo, hi, x.shape[-2])


@functools.partial(jax.custom_vjp, nondiff_argnums=(1, 2, 3))
def _take_rows(x, lo, hi, n):
    return x[..., lo:hi, :]


def _take_rows_fwd(x, lo, hi, n):
    return x[..., lo:hi, :], None


def _take_rows_bwd(lo, hi, n, _, g):
    parts = []
    if lo > 0:
        parts.append(jnp.zeros(g.shape[:-2] + (lo, g.shape[-1]), g.dtype))
    parts.append(g)
    if n - hi > 0:
        parts.append(jnp.zeros(g.shape[:-2] + (n - hi, g.shape[-1]), g.dtype))
    return (jnp.concatenate(parts, axis=-2) if len(parts) > 1 else g,)


_take_rows.defvjp(_take_rows_fwd, _take_rows_bwd)


def _heads_of(wide, nheads):
    return jnp.stack([wide[:, h * HEAD:(h + 1) * HEAD] for h in range(nheads)], axis=0)


def _wide_of(x):
    return jnp.concatenate([x[h] for h in range(x.shape[0])], axis=1)


@functools.partial(jax.custom_vjp, nondiff_argnums=(1,))
def to_heads(wide, nheads):
    return _heads_of(wide, nheads)


to_heads.defvjp(lambda wide, nheads: (_heads_of(wide, nheads), None), lambda nheads, _, g: (_wide_of(g),))


@jax.custom_vjp
def to_wide(x):
    return _wide_of(x)


to_wide.defvjp(lambda x: (_wide_of(x), None), lambda _, g: (_heads_of(g, g.shape[1] // HEAD),))


def _sigmoid(x):
    return jax.nn.sigmoid(x)


def _silu(x):
    return x * _sigmoid(x)


def _dsilu(x):
    s = _sigmoid(x)
    return s * (1.0 + x * (1.0 - s))


def _log1p(u):
    return jnp.where(u < 1e-4, u * (1.0 - 0.5 * u), jnp.log(1.0 + u))


def _softplus(x):
    return jnp.maximum(x, 0.0) + _log1p(jnp.exp(-jnp.abs(x)))


def _rms_gate(o, gain, z):
    return o * lax.rsqrt(jnp.mean(o * o, axis=-1, keepdims=True) + NORM_EPS) * gain * _silu(z)


def _l2n(x):
    return x * lax.rsqrt(jnp.sum(x * x, axis=-1, keepdims=True) + 1e-6)


def _split_dot_raw(m, x, kind):
    mb = m.astype(BF16)
    hi = x.astype(BF16)
    lo = (x - hi.astype(F32)).astype(BF16)
    dims = _DIMS[kind]
    return (lax.dot_general(mb, hi, dims, preferred_element_type=F32)
            + lax.dot_general(mb, lo, dims, preferred_element_type=F32))


@jax.custom_vjp
def mask_dot(m, x):
    return _split_dot_raw(m, x, "nn")


def _mask_dot_fwd(m, x):
    return _split_dot_raw(m, x, "nn"), m


def _mask_dot_bwd(m, g):
    return jnp.zeros_like(m), _split_dot_raw(m, g, "tn")


mask_dot.defvjp(_mask_dot_fwd, _mask_dot_bwd)


def _neumann_rest(low):
    n = low.shape[-1]
    rest = -low
    power = low
    span = 2
    while span < n:
        power = _dot_raw(power, power, "bnn", False)
        rest = rest + power + _dot_raw(rest, power, "bnn", False)
        span *= 2
    return rest


@jax.custom_vjp
def _unit_lower_inverse_minus_eye(low):
    return _neumann_rest(low)


def _inverse_fwd(low):
    rest = _neumann_rest(low)
    return rest, rest


def _inverse_bwd(rest, g):
    left = g + _dot_raw(rest, g, "btn", False)
    return (-(left + _dot_raw(left, rest, "bnt", False)),)


_unit_lower_inverse_minus_eye.defvjp(_inverse_fwd, _inverse_bwd)


def _gdn_step(S, q, k, v, z, small, alog, dtb, gain):
    H = S.shape[0]
    C = GDN_CHUNK
    row = lax.broadcasted_iota(jnp.int32, (C, C), 0)
    col = lax.broadcasted_iota(jnp.int32, (C, C), 1)
    tril, strict, eye = (row >= col)[None], (row > col)[None], (row == col)[None]
    head = lax.broadcasted_iota(jnp.int32, (H, 1, LANES), 0)
    lane = lax.broadcasted_iota(jnp.int32, (H, 1, LANES), 2)
    rowc = lax.broadcasted_iota(jnp.int32, (1, C, 1), 1)
    beta_all = _sigmoid(small)
    g_all = -jnp.exp(alog) * _softplus(small + dtb)
    gc_all = mask_dot((row >= col).astype(F32), g_all)
    beta = jnp.sum(jnp.where(lane == head, beta_all[None], 0.0), axis=-1, keepdims=True)
    gc = jnp.sum(jnp.where(lane == head + H, gc_all[None], 0.0), axis=-1, keepdims=True)
    gc_row = jnp.sum(jnp.where(eye, gc, 0.0), axis=1, keepdims=True)
    decay = jnp.where(tril, jnp.exp(jnp.where(tril, gc - gc_row, 0.0)), 0.0)
    g_last = jnp.sum(jnp.where(rowc == C - 1, gc, 0.0), axis=1, keepdims=True)
    qn = _l2n(q) * (HEAD ** -0.5)
    kn = _l2n(k)
    kb = kn * beta
    low = jnp.where(strict, beta * mdot(kn, kn, "bnt", False) * decay, 0.0)
    inv_rest = _unit_lower_inverse_minus_eye(low)
    eg = jnp.exp(gc)
    vb, kbe = v * beta, kb * eg
    u = vb + mdot(inv_rest, vb, "bnn", False)
    w = kbe + mdot(inv_rest, kbe, "bnn", False)
    attn = mdot(qn, kn, "bnt", False) * decay
    v_new = u - mdot(w, S, "bnn", False)
    o = mdot(qn * eg, S, "bnn", False) + mdot(attn, v_new, "bnn", False)
    k_dec = kn * jnp.exp(g_last - gc)
    return _rms_gate(o, gain, z), S * jnp.exp(g_last) + mdot(k_dec, v_new, "btn", False)


def _hgrn_step(St, qr, fr, vi, z, lb0, lb1, gain):
    H = St.shape[0]
    C, SB = HGRN_CHUNK, HGRN_SUB
    row = lax.broadcasted_iota(jnp.int32, (C, C), 0)
    col = lax.broadcasted_iota(jnp.int32, (C, C), 1)
    blk_start = row - (row & (SB - 1))
    in_blk_f = ((row >= col) & (col >= blk_start)).astype(F32)
    before_f = (col < blk_start).astype(F32)
    sums_f = jnp.concatenate([in_blk_f, before_f], axis=0)
    m = jnp.maximum(lb0, lb1)
    e0, e1 = jnp.exp(lb0 - m), jnp.exp(lb1 - m)
    lb = e1 / (e0 + e1)
    f = lb + (1.0 - lb) * _sigmoid(fr)
    q = _silu(qr)
    k = 1.0 - f
    logf = jnp.log(f)
    sums = mask_dot(sums_f, to_wide(logf))
    inner, start = to_heads(_rows(sums, 0, C), H), to_heads(_rows(sums, C, 2 * C), H)
    b = start + inner
    b_last = jnp.sum(logf, axis=1, keepdims=True)
    o = mdot(q * jnp.exp(b), St, "bnt", False)
    qt = q * jnp.exp(inner)
    parts = []
    for blk in range(C // SB):
        lo, n = blk * SB, (blk + 1) * SB
        ref = jnp.sum(_rows(start, lo, n), axis=1, keepdims=True) * (1.0 / SB)
        kt = k * jnp.exp(jnp.minimum(ref - b, EXP_CLAMP))
        part = mdot(_rows(qt, lo, n), kt, "bnt", False)
        t_idx = lax.broadcasted_iota(jnp.int32, (1, SB, C), 1) + lo
        s_idx = lax.broadcasted_iota(jnp.int32, (1, SB, C), 2)
        parts.append(jnp.where(s_idx <= t_idx, part, 0.0))
    o = o + mdot(jnp.concatenate(parts, axis=1), vi, "bnn", False)
    k_dec = k * jnp.exp(b_last - b)
    return _rms_gate(o, gain, z), St * jnp.exp(b_last) + mdot(vi, k_dec, "btn", False)


def _post_norm(s, x, g, b):
    r = ALPHA * x + s
    d = r - jnp.mean(r, axis=-1, keepdims=True)
    var = jnp.mean(d * d, axis=-1, keepdims=True)
    return d * lax.rsqrt(var + NORM_EPS) * g + b


def _post_gate(x1, gate_pre, pp):
    return x1 + pp * _sigmoid(gate_pre)


def _pick(dim, cands):
    for c in cands:
        if dim % c == 0:
            return c
    return dim


def _matmul_tiles(M, K, tn, a_bytes, b_bytes, has_add):
    for tk in (4096, 2048, 1536, 1152, 1024, 640, 512, 384, 256, 128):
        if K % tk:
            continue
        for tm in (2048, 1152, 1024, 512, 384, 256, 128):
            if M % tm:
                continue
            blocks = tm * tk * a_bytes + tk * tn * b_bytes + tm * tn * 4 * (2 if has_add else 1)
            if 2 * blocks + (tm * tn * 4 if tk < K else 0) <= MATMUL_VMEM and tm >= min(M, 1024):
                return tm, tk
    return _pick(M, (512, 256, 128)), _pick(K, (512, 256, 128))


def _matmul(a, b, kind, name, add=None, after=None, also_bf16=False):
    if kind == "nn":
        (M, K), N = a.shape, b.shape[1]
    elif kind == "nt":
        (M, K), N = a.shape, b.shape[0]
    else:
        (K, M), N = a.shape, b.shape[1]
    has_add = add is not None
    tn = _pick(N, (512, 640, 384, 256, 128))
    tm, tk = _matmul_tiles(M, K, tn, a.dtype.itemsize, b.dtype.itemsize, has_add)
    nk = K // tk
    a_spec = pl.BlockSpec((tk, tm), lambda i, j, k: (k, i)) if kind == "tn" else pl.BlockSpec((tm, tk), lambda i, j, k: (i, k))
    b_spec = pl.BlockSpec((tn, tk), lambda i, j, k: (j, k)) if kind == "nt" else pl.BlockSpec((tk, tn), lambda i, j, k: (k, j))
    o_spec = pl.BlockSpec((tm, tn), lambda i, j, k: (i, j))

    extra = ([add] if has_add else []) + ([after] if after is not None else [])
    extra_specs = ([o_spec] if has_add else []) + ([pl.BlockSpec(TOKEN_SHAPE, lambda i, j, k: (0, 0))] if after is not None else [])

    out_dtypes = (F32, BF16) if also_bf16 else (F32,)

    def body(a_ref, b_ref, *rest):
        outs = rest[len(extra):len(extra) + len(out_dtypes)]

        def write(val):
            if has_add:
                val = val + rest[0][...]
            for o_ref in outs:
                o_ref[...] = val.astype(o_ref.dtype)

        part = _dot_raw(a_ref[...], b_ref[...], kind, False)
        if nk == 1:
            write(part)
            return
        acc = rest[-1]
        kk = pl.program_id(2)

        @pl.when(kk == 0)
        def _():
            acc[...] = part

        @pl.when(kk > 0)
        def _():
            acc[...] += part

        @pl.when(kk == nk - 1)
        def _():
            write(acc[...])

    result = pl.pallas_call(
        body, name=name, grid=(M // tm, N // tn, nk),
        in_specs=[a_spec, b_spec] + extra_specs,
        out_specs=[o_spec] * len(out_dtypes), out_shape=[jax.ShapeDtypeStruct((M, N), dt) for dt in out_dtypes],
        scratch_shapes=[pltpu.VMEM((tm, tn), F32)] if nk > 1 else [],
        compiler_params=_params(dimension_semantics=("parallel", "parallel", "arbitrary")),
    )(a, b, *extra)
    return result if also_bf16 else result[0]


def _halo_specs(ts, nt, width, prev=True, main=True, nxt=True):
    per = ts // SUBLANES
    last8 = nt * per - 1
    specs = []
    if prev:
        specs.append(pl.BlockSpec((SUBLANES, width), lambda cb, i: (jnp.maximum(i * per - 1, 0), cb)))
    if main:
        specs.append(pl.BlockSpec((ts, width), lambda cb, i: (i, cb)))
    if nxt:
        specs.append(pl.BlockSpec((SUBLANES, width), lambda cb, i: (jnp.minimum((i + 1) * per, last8), cb)))
    return specs


def _taps(ext, ktaps, lo, size):
    return [ext[lo:lo + size] if j == 0 else pltpu.roll(ext, j, 0)[lo:lo + size] for j in range(ktaps)]


def _ahead(ext, j, size):
    n = ext.shape[0]
    return ext[:size] if j == 0 else pltpu.roll(ext, n - j, 0)[:size]


def _lane_block(ref, k):
    return ref[:, k * MIXER_LANES:(k + 1) * MIXER_LANES]


def _mixer_a_fwd(proj_a, conv_w):
    T = proj_a.shape[0]
    nblk = proj_a.shape[1] // (4 * MIXER_LANES)
    ts = min(ROW_TILE, T)
    nt = T // ts

    def body(pp, pm, w_ref, y_ref):
        i = pl.program_id(1)
        u_prev = jnp.where(i > 0, _lane_block(pp, 0) * _lane_block(pp, 1), 0.0)
        ext = jnp.concatenate([u_prev, _lane_block(pm, 0) * _lane_block(pm, 1)], axis=0)
        t0, t1, t2 = _taps(ext, 3, SUBLANES, ts)
        cv = w_ref[2:3, :] * t0 + w_ref[1:2, :] * t1 + w_ref[0:1, :] * t2
        y_ref[...] = (_lane_block(pm, 2) * cv * _silu(_lane_block(pm, 3))).astype(y_ref.dtype)

    return pl.pallas_call(
        body, name="mixer_a_fwd", grid=(nblk, nt),
        in_specs=_halo_specs(ts, nt, 4 * MIXER_LANES, nxt=False)
        + [pl.BlockSpec((conv_w.shape[0], MIXER_LANES), lambda cb, i: (0, cb))],
        out_specs=pl.BlockSpec((ts, MIXER_LANES), lambda cb, i: (i, cb)),
        out_shape=jax.ShapeDtypeStruct((T, nblk * MIXER_LANES), BF16), compiler_params=_params(),
    )(proj_a, proj_a, conv_w)


def _mixer_a_bwd(proj_a, dy, conv_w):
    T = proj_a.shape[0]
    nblk = proj_a.shape[1] // (4 * MIXER_LANES)
    ts = min(ROW_TILE, T)
    nt = T // ts
    kt = conv_w.shape[0]

    def body(pp, pm, pn, dym, dyn, w_ref, dp_ref, dw_ref):
        i = pl.program_id(1)
        hm, cm, bm, zm = (_lane_block(pm, k) for k in range(4))
        u_prev = jnp.where(i > 0, _lane_block(pp, 0) * _lane_block(pp, 1), 0.0)
        ext = jnp.concatenate([u_prev, hm * cm], axis=0)
        dy_ext = jnp.concatenate([dym[...], jnp.where(i < nt - 1, dyn[...], 0.0)], axis=0)
        b_ext = jnp.concatenate([bm, _lane_block(pn, 2)], axis=0)
        sz_ext = _silu(jnp.concatenate([zm, _lane_block(pn, 3)], axis=0))
        dcv_ext = dy_ext * b_ext * sz_ext
        w = [w_ref[j:j + 1, :] for j in range(kt)]
        du = sum(w[kt - 1 - j] * _ahead(dcv_ext, j, ts) for j in range(kt))
        taps = _taps(ext, kt, SUBLANES, ts)
        cv = sum(w[kt - 1 - j] * taps[j] for j in range(kt))
        for part, d in enumerate((du * cm, du * hm, dym[...] * cv * sz_ext[:ts], dym[...] * bm * cv * _dsilu(zm))):
            dp_ref[:, part * MIXER_LANES:(part + 1) * MIXER_LANES] = d.astype(dp_ref.dtype)
        dcv = dcv_ext[:ts]

        @pl.when(i == 0)
        def _():
            dw_ref[...] = jnp.zeros_like(dw_ref)

        for j in range(kt):
            dw_ref[j:j + 1, :] += jnp.sum(dcv * taps[kt - 1 - j], axis=0, keepdims=True)

    return pl.pallas_call(
        body, name="mixer_a_bwd", grid=(nblk, nt),
        in_specs=_halo_specs(ts, nt, 4 * MIXER_LANES) + _halo_specs(ts, nt, MIXER_LANES, prev=False)
        + [pl.BlockSpec((kt, MIXER_LANES), lambda cb, i: (0, cb))],
        out_specs=[pl.BlockSpec((ts, 4 * MIXER_LANES), lambda cb, i: (i, cb)),
                   pl.BlockSpec((SUBLANES, MIXER_LANES), lambda cb, i: (0, cb))],
        out_shape=[jax.ShapeDtypeStruct(proj_a.shape, BF16),
                   jax.ShapeDtypeStruct((SUBLANES, nblk * MIXER_LANES), F32)],
        compiler_params=_params(),
    )(proj_a, proj_a, proj_a, dy, dy, conv_w)


def _conv_b_fwd(raw, conv_w):
    T = raw.shape[0]
    nblk = raw.shape[1] // CONV_LANES
    ts = min(ROW_TILE, T)
    nt = T // ts
    kt = conv_w.shape[0]

    def body(rp, rm, w_ref, y_ref):
        i = pl.program_id(1)
        ext = jnp.concatenate([jnp.where(i > 0, rp[...], 0.0), rm[...]], axis=0)
        taps = _taps(ext, kt, SUBLANES, ts)
        y_ref[...] = _silu(sum(w_ref[kt - 1 - j:kt - j, :] * taps[j] for j in range(kt)))

    return pl.pallas_call(
        body, name="conv_b_fwd", grid=(nblk, nt),
        in_specs=_halo_specs(ts, nt, CONV_LANES, nxt=False) + [pl.BlockSpec((kt, CONV_LANES), lambda cb, i: (0, cb))],
        out_specs=pl.BlockSpec((ts, CONV_LANES), lambda cb, i: (i, cb)),
        out_shape=jax.ShapeDtypeStruct(raw.shape, F32), compiler_params=_params(),
    )(raw, raw, conv_w)


def _conv_b_bwd(raw, dy, conv_w):
    T = raw.shape[0]
    nblk = raw.shape[1] // CONV_LANES
    ts = min(ROW_TILE, T)
    nt = T // ts
    kt = conv_w.shape[0]

    def body(rp, rm, rn, dym, dyn, w_ref, dr_ref, dw_ref):
        i = pl.program_id(1)
        ext = jnp.concatenate([jnp.where(i > 0, rp[...], 0.0), rm[...], rn[...]], axis=0)
        w = [w_ref[j:j + 1, :] for j in range(kt)]
        taps = _taps(ext, kt, SUBLANES, ts + SUBLANES)
        xc_ext = sum(w[kt - 1 - j] * taps[j] for j in range(kt))
        dy_ext = jnp.concatenate([dym[...], jnp.where(i < nt - 1, dyn[...], 0.0)], axis=0)
        dxc_ext = dy_ext * _dsilu(xc_ext)
        dr_ref[...] = sum(w[kt - 1 - j] * _ahead(dxc_ext, j, ts) for j in range(kt)).astype(dr_ref.dtype)
        dxc = dxc_ext[:ts]

        @pl.when(i == 0)
        def _():
            dw_ref[...] = jnp.zeros_like(dw_ref)

        for j in range(kt):
            dw_ref[j:j + 1, :] += jnp.sum(dxc * taps[kt - 1 - j][:ts], axis=0, keepdims=True)

    return pl.pallas_call(
        body, name="conv_b_bwd", grid=(nblk, nt),
        in_specs=_halo_specs(ts, nt, CONV_LANES) + _halo_specs(ts, nt, CONV_LANES, prev=False)
        + [pl.BlockSpec((kt, CONV_LANES), lambda cb, i: (0, cb))],
        out_specs=[pl.BlockSpec((ts, CONV_LANES), lambda cb, i: (i, cb)),
                   pl.BlockSpec((SUBLANES, CONV_LANES), lambda cb, i: (0, cb))],
        out_shape=[jax.ShapeDtypeStruct(raw.shape, BF16), jax.ShapeDtypeStruct((SUBLANES, nblk * CONV_LANES), F32)],
        compiler_params=_params(),
    )(raw, raw, raw, dy, dy, conv_w)


def _split_heads(ref, base, nheads, rows=slice(None)):
    return jnp.stack([ref[rows, base + h * HEAD: base + (h + 1) * HEAD] for h in range(nheads)], axis=0)


def _store_heads(ref, base, x, rows=slice(None), accumulate=False):
    for h in range(x.shape[0]):
        lanes = slice(base + h * HEAD, base + (h + 1) * HEAD)
        if accumulate:
            ref[rows, lanes] += x[h]
        else:
            ref[rows, lanes] = x[h].astype(ref.dtype)


def _gdn_fwd(qkv, proj_zs, alog, dtb, gain, H):
    T = qkv.shape[0]
    C, HW = GDN_CHUNK, H * HEAD
    nc = T // C
    zw = HW + LANES

    def body(qkv_ref, zs_ref, alog_ref, dtb_ref, gain_ref, o_ref, sall_ref, s_scr):
        @pl.when(pl.program_id(0) == 0)
        def _():
            s_scr[...] = jnp.zeros_like(s_scr)

        sall_ref[0] = s_scr[...]
        outs, states = _gdn_step(
            s_scr[...], _split_heads(qkv_ref, 0, H), _split_heads(qkv_ref, HW, H),
            _split_heads(qkv_ref, 2 * HW, H), _split_heads(zs_ref, 0, H), zs_ref[:, HW:HW + LANES],
            alog_ref[...], dtb_ref[...], gain_ref[...])
        _store_heads(o_ref, 0, outs)
        s_scr[...] = states

    row = pl.BlockSpec((1, LANES), lambda i: (0, 0))
    return pl.pallas_call(
        body, name="gdn_fwd", grid=(nc,),
        in_specs=[pl.BlockSpec((C, 3 * HW), lambda i: (i, 0)), pl.BlockSpec((C, zw), lambda i: (i, 0)), row, row, row],
        out_specs=[pl.BlockSpec((C, HW), lambda i: (i, 0)), pl.BlockSpec((1, H, HEAD, HEAD), lambda i: (i, 0, 0, 0))],
        out_shape=[jax.ShapeDtypeStruct((T, HW), BF16), jax.ShapeDtypeStruct((nc, H, HEAD, HEAD), F32)],
        scratch_shapes=[pltpu.VMEM((H, HEAD, HEAD), F32)], compiler_params=_params(),
    )(qkv, proj_zs, alog, dtb, gain)


def _gdn_bwd(qkv, proj_zs, do, s_all, alog, dtb, gain, H):
    T = qkv.shape[0]
    C, HW = GDN_CHUNK, H * HEAD
    nc = T // C
    zw = HW + LANES

    def body(qkv_ref, zs_ref, do_ref, sin_ref, alog_ref, dtb_ref, gain_ref,
             dqkv_ref, dzs_ref, dalog_ref, ddtb_ref, dgain_ref, ds_scr):
        @pl.when(pl.program_id(0) == 0)
        def _():
            ds_scr[...] = jnp.zeros_like(ds_scr)
            dalog_ref[...] = jnp.zeros_like(dalog_ref)
            ddtb_ref[...] = jnp.zeros_like(ddtb_ref)
            dgain_ref[...] = jnp.zeros_like(dgain_ref)

        primals = (sin_ref[0], _split_heads(qkv_ref, 0, H),
                   _split_heads(qkv_ref, HW, H), _split_heads(qkv_ref, 2 * HW, H), _split_heads(zs_ref, 0, H),
                   zs_ref[:, HW:HW + LANES], alog_ref[...], dtb_ref[...], gain_ref[...])
        _, vjp = jax.vjp(_gdn_step, *primals)
        dS, dq, dk, dv, dz, dsmall, dalog, ddtb, dgain = vjp((_split_heads(do_ref, 0, H), ds_scr[...]))
        ds_scr[...] = dS
        _store_heads(dqkv_ref, 0, dq)
        _store_heads(dqkv_ref, HW, dk)
        _store_heads(dqkv_ref, 2 * HW, dv)
        _store_heads(dzs_ref, 0, dz)
        dzs_ref[:, HW:HW + LANES] = dsmall.astype(dzs_ref.dtype)
        dalog_ref[...] += dalog
        ddtb_ref[...] += ddtb
        dgain_ref[...] += dgain

    row = pl.BlockSpec((1, LANES), lambda i: (0, 0))
    rev = lambda i: nc - 1 - i
    return pl.pallas_call(
        body, name="gdn_bwd", grid=(nc,),
        in_specs=[pl.BlockSpec((C, 3 * HW), lambda i: (rev(i), 0)), pl.BlockSpec((C, zw), lambda i: (rev(i), 0)),
                  pl.BlockSpec((C, HW), lambda i: (rev(i), 0)),
                  pl.BlockSpec((1, H, HEAD, HEAD), lambda i: (rev(i), 0, 0, 0)), row, row, row],
        out_specs=[pl.BlockSpec((C, 3 * HW), lambda i: (rev(i), 0)), pl.BlockSpec((C, zw), lambda i: (rev(i), 0)),
                   row, row, row],
        out_shape=[jax.ShapeDtypeStruct(qkv.shape, F32), jax.ShapeDtypeStruct(proj_zs.shape, BF16)]
        + [jax.ShapeDtypeStruct((1, LANES), F32)] * 3,
        scratch_shapes=[pltpu.VMEM((H, HEAD, HEAD), F32)], compiler_params=_params(),
    )(qkv, proj_zs, do, s_all, alog, dtb, gain)


def _hgrn_refs(proj_ref, lb_ref, HP):
    W = HP * HEAD
    return (_split_heads(proj_ref, 0, HP), _split_heads(proj_ref, W, HP), _split_heads(proj_ref, 2 * W, HP),
            _split_heads(proj_ref, 3 * W, HP), _split_heads(lb_ref, 0, HP, slice(0, 1)),
            _split_heads(lb_ref, 0, HP, slice(1, 2)))


def _hgrn_fwd(proj, lower_bounds, gain, nheads):
    T = proj.shape[0]
    C, HP = HGRN_CHUNK, HGRN_HEADS_PER_STEP
    ng, nc, W = nheads // HP, T // C, HP * HEAD

    def body(proj_ref, lb_ref, gain_ref, o_ref, sall_ref, s_scr):
        @pl.when(pl.program_id(1) == 0)
        def _():
            s_scr[...] = jnp.zeros_like(s_scr)

        sall_ref[0] = s_scr[...]
        qr, fr, vi, z, lb0, lb1 = _hgrn_refs(proj_ref, lb_ref, HP)
        outs, states = _hgrn_step(s_scr[...], qr, fr, vi, z, lb0, lb1, gain_ref[...])
        _store_heads(o_ref, 0, outs)
        s_scr[...] = states

    return pl.pallas_call(
        body, name="hgrn_fwd", grid=(ng, nc),
        in_specs=[pl.BlockSpec((C, 4 * W), lambda g, i: (i, g)), pl.BlockSpec((2, W), lambda g, i: (0, g)),
                  pl.BlockSpec((1, LANES), lambda g, i: (0, 0))],
        out_specs=[pl.BlockSpec((C, W), lambda g, i: (i, g)),
                   pl.BlockSpec((1, HP, HEAD, HEAD), lambda g, i: (i, g, 0, 0))],
        out_shape=[jax.ShapeDtypeStruct((T, nheads * HEAD), BF16), jax.ShapeDtypeStruct((nc, nheads, HEAD, HEAD), F32)],
        scratch_shapes=[pltpu.VMEM((HP, HEAD, HEAD), F32)], compiler_params=_params(),
    )(proj, lower_bounds, gain)


def _hgrn_bwd(proj, do, s_all, lower_bounds, gain, nheads):
    T = proj.shape[0]
    C, HP = HGRN_CHUNK, HGRN_HEADS_PER_STEP
    ng, nc, W = nheads // HP, T // C, HP * HEAD

    def body(proj_ref, do_ref, sin_ref, lb_ref, gain_ref, dproj_ref, dlb_ref, dgain_ref, ds_scr):
        first = pl.program_id(1) == 0

        @pl.when(first)
        def _():
            ds_scr[...] = jnp.zeros_like(ds_scr)
            dlb_ref[...] = jnp.zeros_like(dlb_ref)

        @pl.when(first & (pl.program_id(0) == 0))
        def _():
            dgain_ref[...] = jnp.zeros_like(dgain_ref)

        qr, fr, vi, z, lb0, lb1 = _hgrn_refs(proj_ref, lb_ref, HP)
        primals = (sin_ref[0], qr, fr, vi, z, lb0, lb1, gain_ref[...])
        _, vjp = jax.vjp(_hgrn_step, *primals)
        dS, dq, df, dv, dz, dlb0, dlb1, dgain = vjp((_split_heads(do_ref, 0, HP), ds_scr[...]))
        ds_scr[...] = dS
        for part, d in enumerate((dq, df, dv, dz)):
            _store_heads(dproj_ref, part * W, d)
        _store_heads(dlb_ref, 0, dlb0, slice(0, 1), accumulate=True)
        _store_heads(dlb_ref, 0, dlb1, slice(1, 2), accumulate=True)
        dgain_ref[...] += dgain

    rev = lambda i: nc - 1 - i
    return pl.pallas_call(
        body, name="hgrn_bwd", grid=(ng, nc),
        in_specs=[pl.BlockSpec((C, 4 * W), lambda g, i: (rev(i), g)), pl.BlockSpec((C, W), lambda g, i: (rev(i), g)),
                  pl.BlockSpec((1, HP, HEAD, HEAD), lambda g, i: (rev(i), g, 0, 0)),
                  pl.BlockSpec((2, W), lambda g, i: (0, g)), pl.BlockSpec((1, LANES), lambda g, i: (0, 0))],
        out_specs=[pl.BlockSpec((C, 4 * W), lambda g, i: (rev(i), g)), pl.BlockSpec((2, W), lambda g, i: (0, g)),
                   pl.BlockSpec((1, LANES), lambda g, i: (0, 0))],
        out_shape=[jax.ShapeDtypeStruct(proj.shape, BF16), jax.ShapeDtypeStruct(lower_bounds.shape, F32),
                   jax.ShapeDtypeStruct((1, LANES), F32)],
        scratch_shapes=[pltpu.VMEM((HP, HEAD, HEAD), F32)], compiler_params=_params(),
    )(proj, do, s_all, lower_bounds, gain)


def _post_specs(T):
    tr = min(POST_TILE, T)
    tile = lambda w: pl.BlockSpec((tr, w), lambda i: (i, 0))
    full = lambda r, w: pl.BlockSpec((r, w), lambda i: (0, 0))
    return tr, tile, full


def _post_fwd(s, x, p, g, b, wg, wpl, name):
    T, D = x.shape
    P = p.shape[1]
    tr, tile, full = _post_specs(T)

    def body(s_ref, x_ref, p_ref, g_ref, b_ref, wg_ref, wpl_ref, o_ref, o16_ref):
        x1 = _post_norm(s_ref[...], x_ref[...], g_ref[...], b_ref[...])
        xn = _post_gate(x1, _dot_raw(x1, wg_ref[...], "nn", False), _dot_raw(p_ref[...], wpl_ref[...], "nn", False))
        o_ref[...] = xn
        o16_ref[...] = xn.astype(BF16)

    return pl.pallas_call(
        body, name=name, grid=(T // tr,),
        in_specs=[tile(D), tile(D), tile(P), full(1, D), full(1, D), full(D, D), full(P, D)],
        out_specs=[tile(D), tile(D)],
        out_shape=[jax.ShapeDtypeStruct((T, D), F32), jax.ShapeDtypeStruct((T, D), BF16)], compiler_params=_params(),
    )(s, x, p, g, b, wg, wpl)


def _post_bwd(s, x, p, g, b, wg, wpl, dnext, name, with_loss):
    T, D = x.shape
    P = p.shape[1]
    tr, tile, full = _post_specs(T)

    def body(s_ref, x_ref, p_ref, g_ref, b_ref, wg_ref, wpl_ref, dn_ref,
             ds_ref, dx_ref, dg_ref, db_ref, dwg_ref, dwpl_ref, loss_ref):
        @pl.when(pl.program_id(0) == 0)
        def _():
            for r in (dg_ref, db_ref, dwg_ref, dwpl_ref, loss_ref):
                r[...] = jnp.zeros_like(r)

        x1, vjp_norm = jax.vjp(_post_norm, s_ref[...], x_ref[...], g_ref[...], b_ref[...])
        gate_pre = _dot_raw(x1, wg_ref[...], "nn", False)
        pp = _dot_raw(p_ref[...], wpl_ref[...], "nn", False)
        xn, vjp_gate = jax.vjp(_post_gate, x1, gate_pre, pp)
        if with_loss:
            err = xn - dn_ref[...]
            loss_ref[...] += 0.5 * jnp.sum(jnp.sum(err * err, axis=-1, keepdims=True), axis=0, keepdims=True) / D
            dn = err / D
        else:
            dn = dn_ref[...]
        dx1, dgp, dpp = vjp_gate(dn)
        dwg_ref[...] += _dot_raw(x1, dgp, "tn", False)
        dwpl_ref[...] += _dot_raw(p_ref[...], dpp, "tn", False)
        dx1 = dx1 + _dot_raw(dgp, wg_ref[...], "nt", False)
        ds, dx, dg, db = vjp_norm(dx1)
        ds_ref[...] = ds.astype(ds_ref.dtype)
        dx_ref[...] = dx
        dg_ref[...] += dg
        db_ref[...] += db

    return pl.pallas_call(
        body, name=name, grid=(T // tr,),
        in_specs=[tile(D), tile(D), tile(P), full(1, D), full(1, D), full(D, D), full(P, D), tile(D)],
        out_specs=[tile(D), tile(D), full(1, D), full(1, D), full(D, D), full(P, D), full(SUBLANES, LANES)],
        out_shape=[jax.ShapeDtypeStruct((T, D), BF16), jax.ShapeDtypeStruct((T, D), F32)]
        + [jax.ShapeDtypeStruct((1, D), F32)] * 2
        + [jax.ShapeDtypeStruct((D, D), F32), jax.ShapeDtypeStruct((P, D), F32),
           jax.ShapeDtypeStruct((SUBLANES, LANES), F32)],
        compiler_params=_params(),
    )(s, x, p, g, b, wg, wpl, dnext)


def _adam_math(w, g, m, v):
    m = ADAM_B1 * m + (1.0 - ADAM_B1) * g
    v = ADAM_B2 * v + (1.0 - ADAM_B2) * (g * g)
    m_hat = m / (1.0 - ADAM_B1 ** ADAM_STEP)
    v_hat = v / (1.0 - ADAM_B2 ** ADAM_STEP)
    return -ADAM_LR * (m_hat / (jnp.sqrt(v_hat) + ADAM_EPS) + ADAM_WD * w), m, v


def _shard_tiles(R, C):
    tr = _pick(R, (256, 128, 64, 32, 16, 8))
    return (tr, C) if tr < R or R % SUBLANES == 0 else (R, _pick(C, (256, 128)))


def _adam_sharded(w, m, v, g8, got, me, name):
    rows_apart = w.shape[1] == 1 and w.shape[0] > 1
    (L, R, C) = (1, w.shape[0], w.shape[2]) if rows_apart else w.shape
    tr, tc = (R, LANES) if rows_apart else _shard_tiles(R, C)
    nr, nc = R // tr, C // tc
    side_by_side = g8.ndim == 2

    def body(me_ref, w_ref, m_ref, v_ref, p_ref, *rest):
        got_refs, (g_ref, d_ref, mo_ref, vo_ref) = rest[:7], rest[7:]
        g = p_ref[...] if side_by_side else p_ref[0]
        for r in got_refs:
            g = g + r[0].astype(F32)
        if rows_apart:
            d, mn, vn = _adam_math(w_ref[:, 0, :], g, m_ref[:, 0, :], v_ref[:, 0, :])
            for ref, val in ((g_ref, g), (d_ref, d), (mo_ref, mn), (vo_ref, vn)):
                ref[:, 0, :] = val
            return
        d, mn, vn = _adam_math(w_ref[0], g, m_ref[0], v_ref[0])
        g_ref[0] = g
        d_ref[0] = d
        mo_ref[0] = mn
        vo_ref[0] = vn

    if rows_apart:
        t3 = pl.BlockSpec((tr, 1, tc), lambda l, i, j, q: (i, 0, j))
    else:
        t3 = pl.BlockSpec((1, tr, tc), lambda l, i, j, q: (l, i, j))
    slot = lambda k: pl.BlockSpec((1, tr, tc), lambda l, i, j, q: (k, l * nr + i, j))
    if side_by_side:
        mine = pl.BlockSpec((tr, tc), lambda l, i, j, q: (l * nr + i, q[0] * nc + j))
    else:
        mine = pl.BlockSpec((1, tr, tc), lambda l, i, j, q: (q[0], l * nr + i, j))
    return pl.pallas_call(
        body, name=name,
        grid_spec=pltpu.PrefetchScalarGridSpec(
            num_scalar_prefetch=1, grid=(L, nr, nc),
            in_specs=[t3, t3, t3, mine] + [slot(k) for k in range(7)], out_specs=[t3, t3, t3, t3]),
        out_shape=[jax.ShapeDtypeStruct(w.shape, F32)] * 4, compiler_params=_params(),
    )(me, w, m, v, g8, *([got] * 7))


def _small_rows(shapes):
    offsets, r = [], 0
    for rows, _ in shapes:
        offsets.append(r)
        r += rows
    return offsets, -(-(r + 1) // SUBLANES) * SUBLANES, max(cols for _, cols in shapes)


def _pack_small_grads(grads, loss):
    offsets, total, width = _small_rows([g.shape for g in grads])
    packed = jnp.zeros((total, width), F32)
    for g, r in zip(grads, offsets):
        packed = lax.dynamic_update_slice(packed, g, (r, 0))
    return lax.dynamic_update_slice(packed, loss.reshape(1, 1), (total - 1, 0))


def _adam_replicated(params, g8):
    shapes = [w.shape for w, _, _ in params]
    offsets, total, width = _small_rows(shapes)
    n = len(params)

    def body(*refs):
        g_ref, outs, loss_ref, g_scr = refs[3 * n], refs[3 * n + 1:7 * n + 1], refs[7 * n + 1], refs[7 * n + 2]
        g = g_ref[0]
        for k in range(1, 8):
            g = g + g_ref[k]
        g_scr[...] = g
        for i, (rows, cols) in enumerate(shapes):
            gp = g_scr[offsets[i]:offsets[i] + rows, 0:cols]
            d, mn, vn = _adam_math(refs[3 * i][...], gp, refs[3 * i + 1][...], refs[3 * i + 2][...])
            for ref, val in zip(outs[4 * i:4 * i + 4], (gp, d, mn, vn)):
                ref[...] = val
        loss_ref[...] = g_scr[total - 1:total, 0:LANES]

    out = pl.pallas_call(
        body, name="adam_replicated",
        out_shape=[jax.ShapeDtypeStruct(shp, F32) for shp in shapes for _ in range(4)]
        + [jax.ShapeDtypeStruct((1, LANES), F32)],
        scratch_shapes=[pltpu.VMEM((total, width), F32)], compiler_params=_params(),
    )(*[a for triple in params for a in triple], g8)
    return [out[4 * i:4 * i + 4] for i in range(n)], out[4 * n][0, 0]


def _place():
    return lax.axis_index("x"), lax.axis_index("y"), lax.axis_index("c")


def _all_gather(shard, name):
    def body(x_ref, out_ref, send_sems, recv_sems, local_sem):
        x, y, c = _place()
        me, sibling = (x, y, c), (x, y, 1 - c)
        chips = [(1 - x, y), (x, 1 - y), (1 - x, 1 - y)]

        def slab(px, py, pc):
            return out_ref.at[4 * px + 2 * py + pc]

        def copy(k, block, to, src=None):
            return pltpu.make_async_remote_copy(
                src_ref=slab(*block) if src is None else src, dst_ref=slab(*block),
                send_sem=send_sems.at[k], recv_sem=recv_sems.at[k], device_id=to, device_id_type=MESH)

        mine = pltpu.make_async_copy(x_ref, slab(*me), local_sem)
        mine.start()
        first = [copy(0, me, sibling, src=x_ref)]
        first += [copy(1 + j, me, (*chip, c), src=x_ref) for j, chip in enumerate(chips)]
        for cp in first:
            cp.start()
        passed = [copy(4 + j, (*chip, c), sibling) for j, chip in enumerate(chips)]
        for j, chip in enumerate(chips):
            copy(1 + j, (*chip, c), me).wait_recv()
            passed[j].start()
        copy(0, sibling, me).wait_recv()
        for j, chip in enumerate(chips):
            copy(4 + j, (*chip, 1 - c), me).wait_recv()
        for cp in first + passed:
            cp.wait_send()
        mine.wait()

    return pl.pallas_call(
        body, name=name, out_shape=jax.ShapeDtypeStruct((8,) + shard.shape, shard.dtype),
        in_specs=[pl.BlockSpec(memory_space=pl.ANY)], out_specs=pl.BlockSpec(memory_space=pl.ANY),
        scratch_shapes=[pltpu.SemaphoreType.DMA((7,)), pltpu.SemaphoreType.DMA((7,)), pltpu.SemaphoreType.DMA],
    )(shard)


def _first_gather(shards, x, name):
    n = len(shards)
    T, D = x.shape
    rows = _pick(T, (512, 256, 128))

    def body(*refs):
        ins, x_ref = refs[:n], refs[n]
        outs, x16_ref = refs[n + 1:2 * n + 1], refs[2 * n + 1]
        send_sems, recv_sems, local_sems, buf32, buf16, cast_sem = refs[2 * n + 2:]
        x, y, c = _place()
        me, sibling = (x, y, c), (x, y, 1 - c)
        chips = [(1 - x, y), (x, 1 - y), (1 - x, 1 - y)]

        def copy(i, k, block, to, src=None):
            slab = outs[i].at[4 * block[0] + 2 * block[1] + block[2]]
            return pltpu.make_async_remote_copy(
                src_ref=slab if src is None else src, dst_ref=slab, send_sem=send_sems.at[7 * i + k],
                recv_sem=recv_sems.at[7 * i + k], device_id=to, device_id_type=MESH)

        mine, first = [], []
        for i in range(n):
            mine.append(pltpu.make_async_copy(ins[i], outs[i].at[4 * x + 2 * y + c], local_sems.at[i]))
            first.append(copy(i, 0, me, sibling, src=ins[i]))
            first += [copy(i, 1 + j, me, (*chip, c), src=ins[i]) for j, chip in enumerate(chips)]
        for cp in mine + first:
            cp.start()
        for t in range(T // rows):
            tile = pl.ds(t * rows, rows)
            load = pltpu.make_async_copy(x_ref.at[tile], buf32, cast_sem)
            load.start()
            load.wait()
            buf16[...] = buf32[...].astype(BF16)
            store = pltpu.make_async_copy(buf16, x16_ref.at[tile], cast_sem)
            store.start()
            store.wait()
        passed = []
        for i in range(n):
            for j, chip in enumerate(chips):
                copy(i, 1 + j, (*chip, c), me).wait_recv()
                passed.append(copy(i, 4 + j, (*chip, c), sibling))
                passed[-1].start()
        for i in range(n):
            copy(i, 0, sibling, me).wait_recv()
            for j, chip in enumerate(chips):
                copy(i, 4 + j, (*chip, 1 - c), me).wait_recv()
        for cp in first + passed:
            cp.wait_send()
        for cp in mine:
            cp.wait()

    any_spec = pl.BlockSpec(memory_space=pl.ANY)
    out = pl.pallas_call(
        body, name=name,
        out_shape=[jax.ShapeDtypeStruct((8,) + s.shape, s.dtype) for s in shards] + [jax.ShapeDtypeStruct((T, D), BF16)],
        in_specs=[any_spec] * (n + 1), out_specs=[any_spec] * (n + 1),
        scratch_shapes=[pltpu.SemaphoreType.DMA((7 * n,)), pltpu.SemaphoreType.DMA((7 * n,)),
                        pltpu.SemaphoreType.DMA((n,)), pltpu.VMEM((rows, D), F32), pltpu.VMEM((rows, D), BF16),
                        pltpu.SemaphoreType.DMA],
        compiler_params=_params(),
    )(*shards, x)
    return out[:n], out[n]


_HBM = pl.BlockSpec(memory_space=pltpu.HBM)
_SEM = pl.BlockSpec(memory_space=pltpu.SEMAPHORE)
_DATAFLOW = pltpu.SideEffectType.DATAFLOW_SIDE_EFFECTING
TOKEN_SHAPE = (SUBLANES, LANES)


def _peers(x, y, c):
    flip = lambda v, bit: 1 - v if bit else v
    return [(flip(x, r >> 2 & 1), flip(y, r >> 1 & 1), flip(c, r & 1)) for r in range(1, 8)]


def _scatter_plan(x, y, c):
    return [(4 * px + 2 * py + pc, k, (px, py, pc)) for k, (px, py, pc) in enumerate(_peers(x, y, c))]


def _exchange_copies(plan, src_ref, land_ref, send_sems, recv_sems):
    C = land_ref.shape[-1]
    block = (lambda b: src_ref.at[b]) if len(src_ref.shape) == len(land_ref.shape) else (
        lambda b: src_ref.at[:, pl.ds(b * C, C)])
    return [pltpu.make_async_remote_copy(
        src_ref=block(blk), dst_ref=land_ref.at[slot], send_sem=send_sems.at[k], recv_sem=recv_sems.at[k],
        device_id=peer, device_id_type=MESH) for k, (blk, slot, peer) in enumerate(plan(*_place()))]


def _exchange_start_many(srcs, n_slots, block_shapes, plan, name):
    k, n = len(srcs), len(plan(0, 0, 0))
    land_shapes = [(n_slots,) + tuple(bs) for bs in block_shapes]

    def body(*refs):
        src_refs, land_refs, outs = refs[:k], refs[k:2 * k], refs[2 * k:]
        for i in range(k):
            for cp in _exchange_copies(plan, src_refs[i], land_refs[i], outs[2 * i], outs[2 * i + 1]):
                cp.start()
        outs[-1][...] = jnp.zeros_like(outs[-1])

    sems = [pltpu.SemaphoreType.DMA((n,))] * (2 * k)
    out = pl.pallas_call(
        body, name=name,
        out_shape=(*sems, *[pltpu.HBM(s.shape, s.dtype) for s in srcs],
                   *[pltpu.HBM(ls, s.dtype) for ls, s in zip(land_shapes, srcs)],
                   jax.ShapeDtypeStruct(TOKEN_SHAPE, F32)),
        in_specs=(_HBM,) * (2 * k),
        out_specs=(*[_SEM] * (2 * k), *[_HBM] * (2 * k), pl.BlockSpec(memory_space=pltpu.VMEM)),
        input_output_aliases={i: 2 * k + i for i in range(2 * k)},
        compiler_params=pltpu.CompilerParams(has_side_effects=_DATAFLOW),
    )(*[pltpu.with_memory_space_constraint(s, pltpu.HBM) for s in srcs],
      *[pltpu.with_memory_space_constraint(lax.empty(ls, s.dtype), pltpu.HBM) for ls, s in zip(land_shapes, srcs)])
    return [(out[2 * i], out[2 * i + 1], out[2 * k + i], out[3 * k + i], out[-1]) for i in range(k)]


def _exchange_wait(handle, plan, after, name):
    send_sems, recv_sems, src_thru, land_thru, _ = handle

    def body(src_ref, land_ref, send_sems, recv_sems, after_ref, src_dead, got_ref):
        for cp in _exchange_copies(plan, src_ref, land_ref, send_sems, recv_sems):
            cp.wait_send()
            cp.wait_recv()

    return pl.pallas_call(
        body, name=name,
        out_shape=(pltpu.HBM(src_thru.shape, src_thru.dtype), pltpu.HBM(land_thru.shape, land_thru.dtype)),
        in_specs=(_HBM, _HBM, _SEM, _SEM, pl.BlockSpec(memory_space=pl.ANY)), out_specs=(_HBM, _HBM),
        input_output_aliases={0: 0, 1: 1}, compiler_params=pltpu.CompilerParams(has_side_effects=_DATAFLOW),
    )(src_thru, land_thru, send_sems, recv_sems, after)[1]


def _gather_plan(x, y, c):
    return [(0, 4 * x + 2 * y + c, peer) for peer in _peers(x, y, c)]


class _LateGather:
    def __init__(self, shard, name, handle):
        self.shard, self.name, self.handle = shard, name, handle

    @classmethod
    def start_all(cls, shards, names):
        handles = _exchange_start_many([s[None] for s in shards], 8, [s.shape for s in shards], _gather_plan,
                                       "ag_late_start")
        return [cls(s, n, h) for s, n, h in zip(shards, names, handles)]

    def get(self, after):
        land = _exchange_wait(self.handle, _gather_plan, after, self.name + "_wait")
        x, y, c = _place()
        return lax.dynamic_update_slice(land, self.shard[None], (4 * x + 2 * y + c, 0, 0))


class _GradExchange:
    def __init__(self, me, layouts, held_for):
        self.me, self.layouts, self.held_for, self.held, self.pending = me, layouts, held_for, [], {}

    def start(self, tag, grad, grad16=None):
        g8 = self.layouts[tag](grad)
        g16 = g8.astype(BF16) if grad16 is None else self.layouts[tag](grad16)
        block_shape = g8.shape[1:] if g8.ndim == 3 else (g8.shape[0], g8.shape[1] // 8)
        self.held.append((tag, g8, g16, block_shape))
        if tag in self.held_for:
            return jnp.zeros(TOKEN_SHAPE, F32)
        held, self.held = self.held, []
        handles = _exchange_start_many([h[2] for h in held], 7, [h[3] for h in held], _scatter_plan,
                                       "rs_start_" + tag)
        for (t, g8_t, _, _), handle in zip(held, handles):
            self.pending[t] = (g8_t, handle)
        return handles[0][4]

    def finish(self, tag, w, m, v, after):
        g8, handle = self.pending.pop(tag)
        got = _exchange_wait(handle, _scatter_plan, after, "rs_wait_" + tag)
        return _adam_sharded(w, m, v, g8, got, self.me, "adam_" + tag)


def _local_grads(x, p0, p1, target, wt_zs, wt_a, wt_qkv, late, conv_a, conv_b,
                 a_log, dt_bias, gdn_gain, lower_bounds, hgrn_gain, ln_g, ln_b, on_grad=None, x16=None):
    H = a_log.shape[1]
    pad_small = ((0, 0), (H, LANES - 2 * H))
    alog_row = jnp.pad(a_log, pad_small)
    dtb_row = jnp.pad(dt_bias, pad_small)

    if x16 is None:
        x16 = x.astype(BF16)
    proj_zs = _matmul(x16, wt_zs, "nt", "proj_even_zs", after=late.started)
    proj_a = _matmul(x16, wt_a, "nt", "proj_even_a", after=late.started)
    proj_qkv = _matmul(x16, wt_qkv, "nt", "proj_even_qkv", after=late.started)
    y_a = _mixer_a_fwd(proj_a, conv_a)
    qkv = _conv_b_fwd(proj_qkv, conv_b)
    o2, s_gdn = _gdn_fwd(qkv, proj_zs, alog_row, dtb_row, gdn_gain, H)
    woute_a, woute_b = late.out_even(o2)
    wg, wpl = late.gate(o2)
    s_e = _matmul(o2, woute_b, "nn", "out_even_b", add=_matmul(y_a, woute_a, "nn", "out_even_a"))
    x2, x2_16 = _post_fwd(s_e, x, p0, ln_g[0:1], ln_b[0:1], wg[0], wpl[0], "post_even_fwd")
    wino, wouto = late.odd(s_e)
    nheads_o = wouto.shape[0] // HEAD
    proj_o = _matmul(x2_16, wino, "nn", "proj_odd")
    o4, s_hgrn = _hgrn_fwd(proj_o, lower_bounds, hgrn_gain, nheads_o)
    s_o = _matmul(o4, wouto, "nn", "out_odd")
    ds_o, dx2, dlng1, dlnb1, dwg1, dwpl1, loss = _post_bwd(
        s_o, x2, p1, ln_g[1:2], ln_b[1:2], wg[1], wpl[1], target, "post_odd_loss_bwd", True)
    do4 = _matmul(ds_o, wouto, "nt", "d_out_odd_act")
    grads = {}

    def emit(tag, grad, grad16=None):
        grads[tag] = grad
        return on_grad(tag, grad, grad16) if on_grad is not None else jnp.zeros(TOKEN_SHAPE, F32)

    tok = emit("w_out_odd", *_matmul(o4, ds_o, "tn", "d_out_odd_w", also_bf16=True))
    dproj_o, dlb, dhgain = _hgrn_bwd(proj_o, do4, s_hgrn, lower_bounds, hgrn_gain + tok[0:1], nheads_o)
    dx2 = _matmul(dproj_o, wino, "nt", "d_proj_odd_act", add=dx2)
    tok = emit("w_in_odd", *_matmul(x2_16, dproj_o, "tn", "d_proj_odd_w", also_bf16=True))
    ds_e, dx, dlng0, dlnb0, dwg0, dwpl0, _ = _post_bwd(
        s_e, x, p0, ln_g[0:1], ln_b[0:1] + tok[0:1, 0:1], wg[0], wpl[0], dx2, "post_even_bwd", False)
    tok = emit("w_pl_gate", jnp.stack([dwg0, dwg1])) + emit("w_pl", jnp.stack([dwpl0, dwpl1]))
    dy_a = _matmul(ds_e, woute_a, "nt", "d_out_even_a_act")
    do2 = _matmul(ds_e, woute_b, "nt", "d_out_even_b_act")
    dwoute_a = _matmul(y_a, ds_e, "tn", "d_out_even_a_w")
    dwoute_b = _matmul(o2, ds_e, "tn", "d_out_even_b_w")
    tok = tok + emit("w_out_even", jnp.concatenate([dwoute_a, dwoute_b], axis=0))
    dqkv, dproj_zs, dalog, ddtb, dggain = _gdn_bwd(qkv, proj_zs, do2, s_gdn, alog_row, dtb_row, gdn_gain + tok[0:1], H)
    dproj_qkv, dconv_b = _conv_b_bwd(proj_qkv, dqkv, conv_b)
    dproj_a, dconv_a = _mixer_a_bwd(proj_a, dy_a, conv_a)
    emit("conv", (dconv_a[:conv_a.shape[0]], dconv_b[:conv_b.shape[0]]))
    tok = emit("w_in_even", (_matmul(dproj_zs, x16, "tn", "d_proj_even_zs_w"), _matmul(dproj_a, x16, "tn", "d_proj_even_a_w"),
                             _matmul(dproj_qkv, x16, "tn", "d_proj_even_qkv_w")))
    dx = _matmul(dproj_zs, wt_zs, "nn", "d_proj_even_zs_act", add=dx, after=tok)
    dx = _matmul(dproj_a, wt_a, "nn", "d_proj_even_a_act", add=dx)
    dx = _matmul(dproj_qkv, wt_qkv, "nn", "d_proj_even_qkv_act", add=dx)
    grads.update(
        loss=loss[0, 0], grad_x=dx, a_log=dalog[:, H:2 * H], dt_bias=ddtb[:, H:2 * H], gdn_gain=dggain,
        lower_bounds=dlb, hgrn_gain=dhgain, ln_g=jnp.concatenate([dlng0, dlng1], axis=0),
        ln_b=jnp.concatenate([dlnb0, dlnb1], axis=0))
    return grads


def _pad_rows(a, rows):
    return jnp.pad(a, ((0, rows - a.shape[0]), (0, 0)))


def _split_in_even(wt_full, AW, HW, H):
    D = wt_full.shape[1]
    n_a = 4 * AW
    n_main = n_a + 3 * HW
    wt_zs = jnp.concatenate([wt_full[n_main:n_main + HW], wt_full[n_main + HW:],
                             jnp.zeros((LANES - 2 * H, D), wt_full.dtype)], axis=0)
    wt_a = wt_full[:n_a].reshape(4, AW // MIXER_LANES, MIXER_LANES, D).transpose(1, 0, 2, 3).reshape(n_a, D)
    return wt_zs, wt_a, wt_full[n_a:n_main]


def _join_in_even(dt_zs, dt_a, dt_qkv, AW, HW, H):
    D = dt_a.shape[1]
    a_nat = dt_a.reshape(AW // MIXER_LANES, 4, MIXER_LANES, D).transpose(1, 0, 2, 3).reshape(4 * AW, D)
    return jnp.concatenate([a_nat, dt_qkv, dt_zs[:HW], dt_zs[HW:HW + 2 * H]], axis=0)


def kernel(x, p, w_in_even, conv_a_w, conv_b_w, a_log, dt_bias, gdn_norm_g, w_out_even, w_in_odd, lower_bounds, hgrn_norm_g, w_out_odd, ln_g, ln_b, w_pl, w_pl_gate, loss_target, m_w_in_even, m_conv_a_w, m_conv_b_w, m_a_log, m_dt_bias, m_gdn_norm_g, m_w_out_even, m_w_in_odd, m_lower_bounds, m_hgrn_norm_g, m_w_out_odd, m_ln_g, m_ln_b, m_w_pl, m_w_pl_gate, v_w_in_even, v_conv_a_w, v_conv_b_w, v_a_log, v_dt_bias, v_gdn_norm_g, v_w_out_even, v_w_in_odd, v_lower_bounds, v_hgrn_norm_g, v_w_out_odd, v_ln_g, v_ln_b, v_w_pl, v_w_pl_gate):
    xi, yi, ci = _place()
    me = jnp.reshape(4 * xi + 2 * yi + ci, (1,)).astype(jnp.int32)
    D = x.shape[2]
    H = a_log.shape[1]
    HW = H * HEAD
    AW = conv_a_w.shape[2] * 8
    OW = w_out_odd.shape[1] * 8
    PD = w_pl.shape[1]
    ka, kb = conv_a_w.shape[1], conv_b_w.shape[1]
    ca, cb = conv_a_w.shape[2], conv_b_w.shape[2]
    gw = HGRN_HEADS_PER_STEP * HEAD
    ngrp = OW // gw

    transposed = lambda a: jnp.transpose(a, (0, 2, 1))
    taps = jnp.concatenate([_pad_rows(conv_a_w[0], SUBLANES), _pad_rows(conv_b_w[0], SUBLANES)], axis=1)
    (g_ine, g_taps), x16 = _first_gather([transposed(w_in_even)[0].astype(BF16), taps], x[0], "ag_first")
    wt_zs, wt_a, wt_qkv = _split_in_even(g_ine.reshape(-1, D), AW, HW, H)
    conv_a = jnp.transpose(g_taps[:, :ka, :ca], (1, 0, 2)).reshape(ka, 8 * ca)
    conv_b = jnp.transpose(g_taps[:, :kb, ca:], (1, 0, 2)).reshape(kb, 8 * cb)
    behind = lambda shard, dep: lax.optimization_barrier((shard, dep))[0]
    late_oute, late_gate, late_pl, late_ino, late_outo = _LateGather.start_all(
        [behind(w_out_even[0].astype(BF16), (g_ine, g_taps)), w_pl_gate.astype(BF16).reshape(-1, D),
         w_pl.astype(BF16).reshape(DEPTH * PD, -1), w_in_odd[0].astype(BF16), w_out_odd[0].astype(BF16)],
        ["ag_w_out_even", "ag_w_pl_gate", "ag_w_pl", "ag_w_in_odd", "ag_w_out_odd"])

    class _Late:
        started = late_oute.handle[4]

        @staticmethod
        def out_even(after):
            woute = late_oute.get(after).reshape(-1, D)
            return woute[:AW], woute[AW:]

        @staticmethod
        def gate(after):
            g_gate, g_pl = late_gate.get(after), late_pl.get(after)
            return (g_gate.reshape(8, DEPTH, D // 8, D).transpose(1, 0, 2, 3).reshape(DEPTH, D, D),
                    g_pl.reshape(8, DEPTH, PD, D // 8).transpose(1, 2, 0, 3).reshape(DEPTH, PD, D))

        @staticmethod
        def odd(after):
            g_ino = late_ino.get(after)
            wino = jnp.transpose(g_ino, (1, 0, 2)).reshape(D, 4, ngrp, gw).transpose(0, 2, 1, 3).reshape(D, 4 * OW)
            return wino, late_outo.get(after).reshape(-1, D)

    sh = w_in_even.shape[2]
    tap_blocks = lambda g, width: _pad_rows(g, SUBLANES).reshape(SUBLANES, 8, width).transpose(1, 0, 2)
    owner_layout = {
        "w_in_even": lambda g: _join_in_even(*g, AW, HW, H).reshape(8, sh, D),
        "w_in_odd": lambda g: g.reshape(D, ngrp, 4, gw).transpose(0, 2, 1, 3).reshape(D, 4 * OW),
        "w_out_even": lambda g: g.reshape(8, -1, D),
        "w_out_odd": lambda g: g.reshape(8, -1, D),
        "w_pl_gate": lambda g: g.reshape(DEPTH, 8, D // 8, D).transpose(1, 0, 2, 3).reshape(8, DEPTH * D // 8, D),
        "w_pl": lambda g: g.reshape(DEPTH, PD, 8, D // 8).transpose(2, 0, 1, 3).reshape(8, DEPTH * PD, D // 8),
        "conv": lambda g: jnp.concatenate([tap_blocks(g[0], ca), tap_blocks(g[1], cb)], axis=2),
    }
    exchange = _GradExchange(me, owner_layout, held_for={"w_pl_gate": "w_out_even", "w_pl": "w_out_even"})
    gr = _local_grads(x[0], p[0, 0], p[1, 0], loss_target[0], wt_zs, wt_a, wt_qkv, _Late, conv_a, conv_b,
                      a_log, dt_bias, gdn_norm_g, lower_bounds, hgrn_norm_g, ln_g, ln_b, on_grad=exchange.start,
                      x16=x16)

    last = gr["grad_x"]
    pack_taps = lambda a, b: jnp.concatenate([_pad_rows(a[0], SUBLANES), _pad_rows(b[0], SUBLANES)], axis=1)[None]
    o_outo = exchange.finish("w_out_odd", w_out_odd, m_w_out_odd, v_w_out_odd, last)
    o_ino = exchange.finish("w_in_odd", w_in_odd, m_w_in_odd, v_w_in_odd, last)
    o_gate = exchange.finish("w_pl_gate", w_pl_gate, m_w_pl_gate, v_w_pl_gate, last)
    o_pl = exchange.finish("w_pl", w_pl, m_w_pl, v_w_pl, last)
    o_oute = exchange.finish("w_out_even", w_out_even, m_w_out_even, v_w_out_even, last)
    o_taps = exchange.finish("conv", taps[None], pack_taps(m_conv_a_w, m_conv_b_w), pack_taps(v_conv_a_w, v_conv_b_w), last)
    others_done = sum(o[1][0, 0:1, 0:1] for o in (o_outo, o_ino, o_gate, o_pl, o_oute, o_taps))
    rows_first = lambda a: jnp.transpose(a, (2, 0, 1))
    o_ine = [jnp.transpose(o, (1, 2, 0)) for o in exchange.finish(
        "w_in_even", rows_first(w_in_even), rows_first(m_w_in_even), rows_first(v_w_in_even), others_done)]

    small_g = _pack_small_grads([gr["a_log"], gr["dt_bias"], gr["gdn_gain"], gr["lower_bounds"], gr["hgrn_gain"],
                                 gr["ln_g"], gr["ln_b"]], gr["loss"])
    o_small, loss = _adam_replicated(
        [(a_log, m_a_log, v_a_log), (dt_bias, m_dt_bias, v_dt_bias), (gdn_norm_g, m_gdn_norm_g, v_gdn_norm_g),
         (lower_bounds, m_lower_bounds, v_lower_bounds), (hgrn_norm_g, m_hgrn_norm_g, v_hgrn_norm_g),
         (ln_g, m_ln_g, v_ln_g), (ln_b, m_ln_b, v_ln_b)],
        _all_gather(behind(small_g, o_ine[0]), "ag_small_grads"))

    def leaves(kind):
        s_alog, s_dt, s_gg, s_lb, s_hg, s_lng, s_lnb = (o[kind] for o in o_small)
        t = o_taps[kind]
        return [o_ine[kind], t[:, :ka, :ca], t[:, :kb, ca:], s_alog, s_dt, s_gg, o_oute[kind],
                o_ino[kind], s_lb, s_hg, o_outo[kind], s_lng, s_lnb, o_pl[kind], o_gate[kind]]

    return (loss, gr["grad_x"][None], *leaves(0), *leaves(1), *leaves(2), *leaves(3))
```

```python
import functools

import jax
import jax.numpy as jnp
from jax import lax
from jax.experimental import pallas as pl
from jax.experimental.pallas import tpu as pltpu

F32 = jnp.float32
BF16 = jnp.bfloat16
MESH = pl.DeviceIdType.MESH

LANES = 128
SUBLANES = 8
HEAD = 128
GDN_CHUNK = 128
HGRN_CHUNK = 64
HGRN_SUB = 16
HGRN_HEADS_PER_STEP = 16
NORM_EPS = 1e-5
DEPTH = 2
ALPHA = (2.0 * DEPTH) ** 0.25
EXP_CLAMP = 80.0
ADAM_LR, ADAM_B1, ADAM_B2, ADAM_EPS, ADAM_WD, ADAM_STEP = 0.001, 0.9, 0.999, 1e-08, 0.01, 10
VMEM_LIMIT = 56 * 1024 * 1024
MATMUL_VMEM = 36 * 1024 * 1024
ROW_TILE = 1024
MIXER_LANES = 256
CONV_LANES = 512
POST_TILE = 512

_NOBATCH, _BATCH0 = ((), ()), ((0,), (0,))
_DIMS = {"nn": (((1,), (0,)), _NOBATCH), "nt": (((1,), (1,)), _NOBATCH), "tn": (((0,), (0,)), _NOBATCH),
         "bnn": (((2,), (1,)), _BATCH0), "bnt": (((2,), (2,)), _BATCH0), "btn": (((1,), (1,)), _BATCH0)}


def _params(**kw):
    return pltpu.CompilerParams(vmem_limit_bytes=VMEM_LIMIT, **kw)


def _dot_raw(a, b, kind, hi):
    if hi:
        return lax.dot_general(a, b, _DIMS[kind], precision=lax.Precision.HIGHEST, preferred_element_type=F32)
    return lax.dot_general(a.astype(BF16), b.astype(BF16), _DIMS[kind], preferred_element_type=F32)


@functools.partial(jax.custom_vjp, nondiff_argnums=(2, 3))
def mdot(a, b, kind, hi):
    return _dot_raw(a, b, kind, hi)


def _mdot_fwd(a, b, kind, hi):
    return _dot_raw(a, b, kind, hi), (a, b)


def _mdot_bwd(kind, hi, res, g):
    a, b = res
    pre, base = kind[:-2], kind[-2:]
    if base == "nn":
        return _dot_raw(g, b, pre + "nt", hi), _dot_raw(a, g, pre + "tn", hi)
    if base == "nt":
        return _dot_raw(g, b, pre + "nn", hi), _dot_raw(g, a, pre + "tn", hi)
    return _dot_raw(b, g, pre + "nt", hi), _dot_raw(a, g, pre + "nn", hi)


mdot.defvjp(_mdot_fwd, _mdot_bwd)


def _rows(x, lo, hi):
    return _take_rows(x, lo, hi, x.shape[-2])


@functools.partial(jax.custom_vjp, nondiff_argnums=(1, 2, 3))
def _take_rows(x, lo, hi, n):
    return x[..., lo:hi, :]


def _take_rows_fwd(x, lo, hi, n):
    return x[..., lo:hi, :], None


def _take_rows_bwd(lo, hi, n, _, g):
    parts = []
    if lo > 0:
        parts.append(jnp.zeros(g.shape[:-2] + (lo, g.shape[-1]), g.dtype))
    parts.append(g)
    if n - hi > 0:
        parts.append(jnp.zeros(g.shape[:-2] + (n - hi, g.shape[-1]), g.dtype))
    return (jnp.concatenate(parts, axis=-2) if len(parts) > 1 else g,)


_take_rows.defvjp(_take_rows_fwd, _take_rows_bwd)


def _heads_of(wide, nheads):
    return jnp.stack([wide[:, h * HEAD:(h + 1) * HEAD] for h in range(nheads)], axis=0)


def _wide_of(x):
    return jnp.concatenate([x[h] for h in range(x.shape[0])], axis=1)


@functools.partial(jax.custom_vjp, nondiff_argnums=(1,))
def to_heads(wide, nheads):
    return _heads_of(wide, nheads)


to_heads.defvjp(lambda wide, nheads: (_heads_of(wide, nheads), None), lambda nheads, _, g: (_wide_of(g),))


@jax.custom_vjp
def to_wide(x):
    return _wide_of(x)


to_wide.defvjp(lambda x: (_wide_of(x), None), lambda _, g: (_heads_of(g, g.shape[1] // HEAD),))


def _sigmoid(x):
    return jax.nn.sigmoid(x)


def _silu(x):
    return x * _sigmoid(x)


def _dsilu(x):
    s = _sigmoid(x)
    return s * (1.0 + x * (1.0 - s))


def _log1p(u):
    return jnp.where(u < 1e-4, u * (1.0 - 0.5 * u), jnp.log(1.0 + u))


def _softplus(x):
    return jnp.maximum(x, 0.0) + _log1p(jnp.exp(-jnp.abs(x)))


def _rms_gate(o, gain, z):
    return o * lax.rsqrt(jnp.mean(o * o, axis=-1, keepdims=True) + NORM_EPS) * gain * _silu(z)


def _l2n(x):
    return x * lax.rsqrt(jnp.sum(x * x, axis=-1, keepdims=True) + 1e-6)


def _split_dot_raw(m, x, kind):
    mb = m.astype(BF16)
    hi = x.astype(BF16)
    lo = (x - hi.astype(F32)).astype(BF16)
    dims = _DIMS[kind]
    return (lax.dot_general(mb, hi, dims, preferred_element_type=F32)
            + lax.dot_general(mb, lo, dims, preferred_element_type=F32))


@jax.custom_vjp
def mask_dot(m, x):
    return _split_dot_raw(m, x, "nn")


def _mask_dot_fwd(m, x):
    return _split_dot_raw(m, x, "nn"), m


def _mask_dot_bwd(m, g):
    return jnp.zeros_like(m), _split_dot_raw(m, g, "tn")


mask_dot.defvjp(_mask_dot_fwd, _mask_dot_bwd)


def _neumann_rest(low):
    n = low.shape[-1]
    rest = -low
    power = low
    span = 2
    while span < n:
        power = _dot_raw(power, power, "bnn", False)
        rest = rest + power + _dot_raw(rest, power, "bnn", False)
        span *= 2
    return rest


@jax.custom_vjp
def _unit_lower_inverse_minus_eye(low):
    return _neumann_rest(low)


def _inverse_fwd(low):
    rest = _neumann_rest(low)
    return rest, rest


def _inverse_bwd(rest, g):
    left = g + _dot_raw(rest, g, "btn", False)
    return (-(left + _dot_raw(left, rest, "bnt", False)),)


_unit_lower_inverse_minus_eye.defvjp(_inverse_fwd, _inverse_bwd)


def _gdn_step(S, q, k, v, z, small, alog, dtb, gain):
    H = S.shape[0]
    C = GDN_CHUNK
    row = lax.broadcasted_iota(jnp.int32, (C, C), 0)
    col = lax.broadcasted_iota(jnp.int32, (C, C), 1)
    tril, strict, eye = (row >= col)[None], (row > col)[None], (row == col)[None]
    head = lax.broadcasted_iota(jnp.int32, (H, 1, LANES), 0)
    lane = lax.broadcasted_iota(jnp.int32, (H, 1, LANES), 2)
    rowc = lax.broadcasted_iota(jnp.int32, (1, C, 1), 1)
    beta_all = _sigmoid(small)
    g_all = -jnp.exp(alog) * _softplus(small + dtb)
    gc_all = mask_dot((row >= col).astype(F32), g_all)
    beta = jnp.sum(jnp.where(lane == head, beta_all[None], 0.0), axis=-1, keepdims=True)
    gc = jnp.sum(jnp.where(lane == head + H, gc_all[None], 0.0), axis=-1, keepdims=True)
    gc_row = jnp.sum(jnp.where(eye, gc, 0.0), axis=1, keepdims=True)
    decay = jnp.where(tril, jnp.exp(jnp.where(tril, gc - gc_row, 0.0)), 0.0)
    g_last = jnp.sum(jnp.where(rowc == C - 1, gc, 0.0), axis=1, keepdims=True)
    qn = _l2n(q) * (HEAD ** -0.5)
    kn = _l2n(k)
    kb = kn * beta
    low = jnp.where(strict, beta * mdot(kn, kn, "bnt", False) * decay, 0.0)
    inv_rest = _unit_lower_inverse_minus_eye(low)
    eg = jnp.exp(gc)
    vb, kbe = v * beta, kb * eg
    u = vb + mdot(inv_rest, vb, "bnn", False)
    w = kbe + mdot(inv_rest, kbe, "bnn", False)
    attn = mdot(qn, kn, "bnt", False) * decay
    v_new = u - mdot(w, S, "bnn", False)
    o = mdot(qn * eg, S, "bnn", False) + mdot(attn, v_new, "bnn", False)
    k_dec = kn * jnp.exp(g_last - gc)
    return _rms_gate(o, gain, z), S * jnp.exp(g_last) + mdot(k_dec, v_new, "btn", False)


def _hgrn_step(St, qr, fr, vi, z, lb0, lb1, gain):
    H = St.shape[0]
    C, SB = HGRN_CHUNK, HGRN_SUB
    row = lax.broadcasted_iota(jnp.int32, (C, C), 0)
    col = lax.broadcasted_iota(jnp.int32, (C, C), 1)
    blk_start = row - (row & (SB - 1))
    in_blk_f = ((row >= col) & (col >= blk_start)).astype(F32)
    before_f = (col < blk_start).astype(F32)
    sums_f = jnp.concatenate([in_blk_f, before_f], axis=0)
    m = jnp.maximum(lb0, lb1)
    e0, e1 = jnp.exp(lb0 - m), jnp.exp(lb1 - m)
    lb = e1 / (e0 + e1)
    f = lb + (1.0 - lb) * _sigmoid(fr)
    q = _silu(qr)
    k = 1.0 - f
    logf = jnp.log(f)
    sums = mask_dot(sums_f, to_wide(logf))
    inner, start = to_heads(_rows(sums, 0, C), H), to_heads(_rows(sums, C, 2 * C), H)
    b = start + inner
    b_last = jnp.sum(logf, axis=1, keepdims=True)
    o = mdot(q * jnp.exp(b), St, "bnt", False)
    qt = q * jnp.exp(inner)
    parts = []
    for blk in range(C // SB):
        lo, n = blk * SB, (blk + 1) * SB
        ref = jnp.sum(_rows(start, lo, n), axis=1, keepdims=True) * (1.0 / SB)
        kt = k * jnp.exp(jnp.minimum(ref - b, EXP_CLAMP))
        part = mdot(_rows(qt, lo, n), kt, "bnt", False)
        t_idx = lax.broadcasted_iota(jnp.int32, (1, SB, C), 1) + lo
        s_idx = lax.broadcasted_iota(jnp.int32, (1, SB, C), 2)
        parts.append(jnp.where(s_idx <= t_idx, part, 0.0))
    o = o + mdot(jnp.concatenate(parts, axis=1), vi, "bnn", False)
    k_dec = k * jnp.exp(b_last - b)
    return _rms_gate(o, gain, z), St * jnp.exp(b_last) + mdot(vi, k_dec, "btn", False)


def _post_norm(s, x, g, b):
    r = ALPHA * x + s
    d = r - jnp.mean(r, axis=-1, keepdims=True)
    var = jnp.mean(d * d, axis=-1, keepdims=True)
    return d * lax.rsqrt(var + NORM_EPS) * g + b


def _post_gate(x1, gate_pre, pp):
    return x1 + pp * _sigmoid(gate_pre)


def _pick(dim, cands):
    for c in cands:
        if dim % c == 0:
            return c
    return dim


def _matmul_tiles(M, K, tn, a_bytes, b_bytes, has_add):
    for tk in (4096, 2048, 1536, 1152, 1024, 640, 512, 384, 256, 128):
        if K % tk:
            continue
        for tm in (2048, 1152, 1024, 512, 384, 256, 128):
            if M % tm:
                continue
            blocks = tm * tk * a_bytes + tk * tn * b_bytes + tm * tn * 4 * (2 if has_add else 1)
            if 2 * blocks + (tm * tn * 4 if tk < K else 0) <= MATMUL_VMEM and tm >= min(M, 1024):
                return tm, tk
    return _pick(M, (512, 256, 128)), _pick(K, (512, 256, 128))


def _matmul(a, b, kind, name, add=None, after=None, also_bf16=False):
    if kind == "nn":
        (M, K), N = a.shape, b.shape[1]
    elif kind == "nt":
        (M, K), N = a.shape, b.shape[0]
    else:
        (K, M), N = a.shape, b.shape[1]
    has_add = add is not None
    tn = _pick(N, (512, 640, 384, 256, 128))
    tm, tk = _matmul_tiles(M, K, tn, a.dtype.itemsize, b.dtype.itemsize, has_add)
    nk = K // tk
    a_spec = pl.BlockSpec((tk, tm), lambda i, j, k: (k, i)) if kind == "tn" else pl.BlockSpec((tm, tk), lambda i, j, k: (i, k))
    b_spec = pl.BlockSpec((tn, tk), lambda i, j, k: (j, k)) if kind == "nt" else pl.BlockSpec((tk, tn), lambda i, j, k: (k, j))
    o_spec = pl.BlockSpec((tm, tn), lambda i, j, k: (i, j))

    extra = ([add] if has_add else []) + ([after] if after is not None else [])
    extra_specs = ([o_spec] if has_add else []) + ([pl.BlockSpec(TOKEN_SHAPE, lambda i, j, k: (0, 0))] if after is not None else [])

    out_dtypes = (F32, BF16) if also_bf16 else (F32,)

    def body(a_ref, b_ref, *rest):
        outs = rest[len(extra):len(extra) + len(out_dtypes)]

        def write(val):
            if has_add:
                val = val + rest[0][...]
            for o_ref in outs:
                o_ref[...] = val.astype(o_ref.dtype)

        part = _dot_raw(a_ref[...], b_ref[...], kind, False)
        if nk == 1:
            write(part)
            return
        acc = rest[-1]
        kk = pl.program_id(2)

        @pl.when(kk == 0)
        def _():
            acc[...] = part

        @pl.when(kk > 0)
        def _():
            acc[...] += part

        @pl.when(kk == nk - 1)
        def _():
            write(acc[...])

    result = pl.pallas_call(
        body, name=name, grid=(M // tm, N // tn, nk),
        in_specs=[a_spec, b_spec] + extra_specs,
        out_specs=[o_spec] * len(out_dtypes), out_shape=[jax.ShapeDtypeStruct((M, N), dt) for dt in out_dtypes],
        scratch_shapes=[pltpu.VMEM((tm, tn), F32)] if nk > 1 else [],
        compiler_params=_params(dimension_semantics=("parallel", "parallel", "arbitrary")),
    )(a, b, *extra)
    return result if also_bf16 else result[0]


def _halo_specs(ts, nt, width, prev=True, main=True, nxt=True):
    per = ts // SUBLANES
    last8 = nt * per - 1
    specs = []
    if prev:
        specs.append(pl.BlockSpec((SUBLANES, width), lambda cb, i: (jnp.maximum(i * per - 1, 0), cb)))
    if main:
        specs.append(pl.BlockSpec((ts, width), lambda cb, i: (i, cb)))
    if nxt:
        specs.append(pl.BlockSpec((SUBLANES, width), lambda cb, i: (jnp.minimum((i + 1) * per, last8), cb)))
    return specs


def _taps(ext, ktaps, lo, size):
    return [ext[lo:lo + size] if j == 0 else pltpu.roll(ext, j, 0)[lo:lo + size] for j in range(ktaps)]


def _ahead(ext, j, size):
    n = ext.shape[0]
    return ext[:size] if j == 0 else pltpu.roll(ext, n - j, 0)[:size]


def _lane_block(ref, k):
    return ref[:, k * MIXER_LANES:(k + 1) * MIXER_LANES]


def _mixer_a_fwd(proj_a, conv_w):
    T = proj_a.shape[0]
    nblk = proj_a.shape[1] // (4 * MIXER_LANES)
    ts = min(ROW_TILE, T)
    nt = T // ts

    def body(pp, pm, w_ref, y_ref):
        i = pl.program_id(1)
        u_prev = jnp.where(i > 0, _lane_block(pp, 0) * _lane_block(pp, 1), 0.0)
        ext = jnp.concatenate([u_prev, _lane_block(pm, 0) * _lane_block(pm, 1)], axis=0)
        t0, t1, t2 = _taps(ext, 3, SUBLANES, ts)
        cv = w_ref[2:3, :] * t0 + w_ref[1:2, :] * t1 + w_ref[0:1, :] * t2
        y_ref[...] = (_lane_block(pm, 2) * cv * _silu(_lane_block(pm, 3))).astype(y_ref.dtype)

    return pl.pallas_call(
        body, name="mixer_a_fwd", grid=(nblk, nt),
        in_specs=_halo_specs(ts, nt, 4 * MIXER_LANES, nxt=False)
        + [pl.BlockSpec((conv_w.shape[0], MIXER_LANES), lambda cb, i: (0, cb))],
        out_specs=pl.BlockSpec((ts, MIXER_LANES), lambda cb, i: (i, cb)),
        out_shape=jax.ShapeDtypeStruct((T, nblk * MIXER_LANES), BF16), compiler_params=_params(),
    )(proj_a, proj_a, conv_w)


def _mixer_a_bwd(proj_a, dy, conv_w):
    T = proj_a.shape[0]
    nblk = proj_a.shape[1] // (4 * MIXER_LANES)
    ts = min(ROW_TILE, T)
    nt = T // ts
    kt = conv_w.shape[0]

    def body(pp, pm, pn, dym, dyn, w_ref, dp_ref, dw_ref):
        i = pl.program_id(1)
        hm, cm, bm, zm = (_lane_block(pm, k) for k in range(4))
        u_prev = jnp.where(i > 0, _lane_block(pp, 0) * _lane_block(pp, 1), 0.0)
        ext = jnp.concatenate([u_prev, hm * cm], axis=0)
        dy_ext = jnp.concatenate([dym[...], jnp.where(i < nt - 1, dyn[...], 0.0)], axis=0)
        b_ext = jnp.concatenate([bm, _lane_block(pn, 2)], axis=0)
        sz_ext = _silu(jnp.concatenate([zm, _lane_block(pn, 3)], axis=0))
        dcv_ext = dy_ext * b_ext * sz_ext
        w = [w_ref[j:j + 1, :] for j in range(kt)]
        du = sum(w[kt - 1 - j] * _ahead(dcv_ext, j, ts) for j in range(kt))
        taps = _taps(ext, kt, SUBLANES, ts)
        cv = sum(w[kt - 1 - j] * taps[j] for j in range(kt))
        for part, d in enumerate((du * cm, du * hm, dym[...] * cv * sz_ext[:ts], dym[...] * bm * cv * _dsilu(zm))):
            dp_ref[:, part * MIXER_LANES:(part + 1) * MIXER_LANES] = d.astype(dp_ref.dtype)
        dcv = dcv_ext[:ts]

        @pl.when(i == 0)
        def _():
            dw_ref[...] = jnp.zeros_like(dw_ref)

        for j in range(kt):
            dw_ref[j:j + 1, :] += jnp.sum(dcv * taps[kt - 1 - j], axis=0, keepdims=True)

    return pl.pallas_call(
        body, name="mixer_a_bwd", grid=(nblk, nt),
        in_specs=_halo_specs(ts, nt, 4 * MIXER_LANES) + _halo_specs(ts, nt, MIXER_LANES, prev=False)
        + [pl.BlockSpec((kt, MIXER_LANES), lambda cb, i: (0, cb))],
        out_specs=[pl.BlockSpec((ts, 4 * MIXER_LANES), lambda cb, i: (i, cb)),
                   pl.BlockSpec((SUBLANES, MIXER_LANES), lambda cb, i: (0, cb))],
        out_shape=[jax.ShapeDtypeStruct(proj_a.shape, BF16),
                   jax.ShapeDtypeStruct((SUBLANES, nblk * MIXER_LANES), F32)],
        compiler_params=_params(),
    )(proj_a, proj_a, proj_a, dy, dy, conv_w)


def _conv_b_fwd(raw, conv_w):
    T = raw.shape[0]
    nblk = raw.shape[1] // CONV_LANES
    ts = min(ROW_TILE, T)
    nt = T // ts
    kt = conv_w.shape[0]

    def body(rp, rm, w_ref, y_ref):
        i = pl.program_id(1)
        ext = jnp.concatenate([jnp.where(i > 0, rp[...], 0.0), rm[...]], axis=0)
        taps = _taps(ext, kt, SUBLANES, ts)
        y_ref[...] = _silu(sum(w_ref[kt - 1 - j:kt - j, :] * taps[j] for j in range(kt)))

    return pl.pallas_call(
        body, name="conv_b_fwd", grid=(nblk, nt),
        in_specs=_halo_specs(ts, nt, CONV_LANES, nxt=False) + [pl.BlockSpec((kt, CONV_LANES), lambda cb, i: (0, cb))],
        out_specs=pl.BlockSpec((ts, CONV_LANES), lambda cb, i: (i, cb)),
        out_shape=jax.ShapeDtypeStruct(raw.shape, F32), compiler_params=_params(),
    )(raw, raw, conv_w)


def _conv_b_bwd(raw, dy, conv_w):
    T = raw.shape[0]
    nblk = raw.shape[1] // CONV_LANES
    ts = min(ROW_TILE, T)
    nt = T // ts
    kt = conv_w.shape[0]

    def body(rp, rm, rn, dym, dyn, w_ref, dr_ref, dw_ref):
        i = pl.program_id(1)
        ext = jnp.concatenate([jnp.where(i > 0, rp[...], 0.0), rm[...], rn[...]], axis=0)
        w = [w_ref[j:j + 1, :] for j in range(kt)]
        taps = _taps(ext, kt, SUBLANES, ts + SUBLANES)
        xc_ext = sum(w[kt - 1 - j] * taps[j] for j in range(kt))
        dy_ext = jnp.concatenate([dym[...], jnp.where(i < nt - 1, dyn[...], 0.0)], axis=0)
        dxc_ext = dy_ext * _dsilu(xc_ext)
        dr_ref[...] = sum(w[kt - 1 - j] * _ahead(dxc_ext, j, ts) for j in range(kt)).astype(dr_ref.dtype)
        dxc = dxc_ext[:ts]

        @pl.when(i == 0)
        def _():
            dw_ref[...] = jnp.zeros_like(dw_ref)

        for j in range(kt):
            dw_ref[j:j + 1, :] += jnp.sum(dxc * taps[kt - 1 - j][:ts], axis=0, keepdims=True)

    return pl.pallas_call(
        body, name="conv_b_bwd", grid=(nblk, nt),
        in_specs=_halo_specs(ts, nt, CONV_LANES) + _halo_specs(ts, nt, CONV_LANES, prev=False)
        + [pl.BlockSpec((kt, CONV_LANES), lambda cb, i: (0, cb))],
        out_specs=[pl.BlockSpec((ts, CONV_LANES), lambda cb, i: (i, cb)),
                   pl.BlockSpec((SUBLANES, CONV_LANES), lambda cb, i: (0, cb))],
        out_shape=[jax.ShapeDtypeStruct(raw.shape, BF16), jax.ShapeDtypeStruct((SUBLANES, nblk * CONV_LANES), F32)],
        compiler_params=_params(),
    )(raw, raw, raw, dy, dy, conv_w)


def _split_heads(ref, base, nheads, rows=slice(None)):
    return jnp.stack([ref[rows, base + h * HEAD: base + (h + 1) * HEAD] for h in range(nheads)], axis=0)


def _store_heads(ref, base, x, rows=slice(None), accumulate=False):
    for h in range(x.shape[0]):
        lanes = slice(base + h * HEAD, base + (h + 1) * HEAD)
        if accumulate:
            ref[rows, lanes] += x[h]
        else:
            ref[rows, lanes] = x[h].astype(ref.dtype)


def _gdn_fwd(qkv, proj_zs, alog, dtb, gain, H):
    T = qkv.shape[0]
    C, HW = GDN_CHUNK, H * HEAD
    nc = T // C
    zw = HW + LANES

    def body(qkv_ref, zs_ref, alog_ref, dtb_ref, gain_ref, o_ref, sall_ref, s_scr):
        @pl.when(pl.program_id(0) == 0)
        def _():
            s_scr[...] = jnp.zeros_like(s_scr)

        sall_ref[0] = s_scr[...]
        outs, states = _gdn_step(
            s_scr[...], _split_heads(qkv_ref, 0, H), _split_heads(qkv_ref, HW, H),
            _split_heads(qkv_ref, 2 * HW, H), _split_heads(zs_ref, 0, H), zs_ref[:, HW:HW + LANES],
            alog_ref[...], dtb_ref[...], gain_ref[...])
        _store_heads(o_ref, 0, outs)
        s_scr[...] = states

    row = pl.BlockSpec((1, LANES), lambda i: (0, 0))
    return pl.pallas_call(
        body, name="gdn_fwd", grid=(nc,),
        in_specs=[pl.BlockSpec((C, 3 * HW), lambda i: (i, 0)), pl.BlockSpec((C, zw), lambda i: (i, 0)), row, row, row],
        out_specs=[pl.BlockSpec((C, HW), lambda i: (i, 0)), pl.BlockSpec((1, H, HEAD, HEAD), lambda i: (i, 0, 0, 0))],
        out_shape=[jax.ShapeDtypeStruct((T, HW), BF16), jax.ShapeDtypeStruct((nc, H, HEAD, HEAD), F32)],
        scratch_shapes=[pltpu.VMEM((H, HEAD, HEAD), F32)], compiler_params=_params(),
    )(qkv, proj_zs, alog, dtb, gain)


def _gdn_bwd(qkv, proj_zs, do, s_all, alog, dtb, gain, H):
    T = qkv.shape[0]
    C, HW = GDN_CHUNK, H * HEAD
    nc = T // C
    zw = HW + LANES

    def body(qkv_ref, zs_ref, do_ref, sin_ref, alog_ref, dtb_ref, gain_ref,
             dqkv_ref, dzs_ref, dalog_ref, ddtb_ref, dgain_ref, ds_scr):
        @pl.when(pl.program_id(0) == 0)
        def _():
            ds_scr[...] = jnp.zeros_like(ds_scr)
            dalog_ref[...] = jnp.zeros_like(dalog_ref)
            ddtb_ref[...] = jnp.zeros_like(ddtb_ref)
            dgain_ref[...] = jnp.zeros_like(dgain_ref)

        primals = (sin_ref[0], _split_heads(qkv_ref, 0, H),
                   _split_heads(qkv_ref, HW, H), _split_heads(qkv_ref, 2 * HW, H), _split_heads(zs_ref, 0, H),
                   zs_ref[:, HW:HW + LANES], alog_ref[...], dtb_ref[...], gain_ref[...])
        _, vjp = jax.vjp(_gdn_step, *primals)
        dS, dq, dk, dv, dz, dsmall, dalog, ddtb, dgain = vjp((_split_heads(do_ref, 0, H), ds_scr[...]))
        ds_scr[...] = dS
        _store_heads(dqkv_ref, 0, dq)
        _store_heads(dqkv_ref, HW, dk)
        _store_heads(dqkv_ref, 2 * HW, dv)
        _store_heads(dzs_ref, 0, dz)
        dzs_ref[:, HW:HW + LANES] = dsmall.astype(dzs_ref.dtype)
        dalog_ref[...] += dalog
        ddtb_ref[...] += ddtb
        dgain_ref[...] += dgain

    row = pl.BlockSpec((1, LANES), lambda i: (0, 0))
    rev = lambda i: nc - 1 - i
    return pl.pallas_call(
        body, name="gdn_bwd", grid=(nc,),
        in_specs=[pl.BlockSpec((C, 3 * HW), lambda i: (rev(i), 0)), pl.BlockSpec((C, zw), lambda i: (rev(i), 0)),
                  pl.BlockSpec((C, HW), lambda i: (rev(i), 0)),
                  pl.BlockSpec((1, H, HEAD, HEAD), lambda i: (rev(i), 0, 0, 0)), row, row, row],
        out_specs=[pl.BlockSpec((C, 3 * HW), lambda i: (rev(i), 0)), pl.BlockSpec((C, zw), lambda i: (rev(i), 0)),
                   row, row, row],
        out_shape=[jax.ShapeDtypeStruct(qkv.shape, F32), jax.ShapeDtypeStruct(proj_zs.shape, BF16)]
        + [jax.ShapeDtypeStruct((1, LANES), F32)] * 3,
        scratch_shapes=[pltpu.VMEM((H, HEAD, HEAD), F32)], compiler_params=_params(),
    )(qkv, proj_zs, do, s_all, alog, dtb, gain)


def _hgrn_refs(proj_ref, lb_ref, HP):
    W = HP * HEAD
    return (_split_heads(proj_ref, 0, HP), _split_heads(proj_ref, W, HP), _split_heads(proj_ref, 2 * W, HP),
            _split_heads(proj_ref, 3 * W, HP), _split_heads(lb_ref, 0, HP, slice(0, 1)),
            _split_heads(lb_ref, 0, HP, slice(1, 2)))


def _hgrn_fwd(proj, lower_bounds, gain, nheads):
    T = proj.shape[0]
    C, HP = HGRN_CHUNK, HGRN_HEADS_PER_STEP
    ng, nc, W = nheads // HP, T // C, HP * HEAD

    def body(proj_ref, lb_ref, gain_ref, o_ref, sall_ref, s_scr):
        @pl.when(pl.program_id(1) == 0)
        def _():
            s_scr[...] = jnp.zeros_like(s_scr)

        sall_ref[0] = s_scr[...]
        qr, fr, vi, z, lb0, lb1 = _hgrn_refs(proj_ref, lb_ref, HP)
        outs, states = _hgrn_step(s_scr[...], qr, fr, vi, z, lb0, lb1, gain_ref[...])
        _store_heads(o_ref, 0, outs)
        s_scr[...] = states

    return pl.pallas_call(
        body, name="hgrn_fwd", grid=(ng, nc),
        in_specs=[pl.BlockSpec((C, 4 * W), lambda g, i: (i, g)), pl.BlockSpec((2, W), lambda g, i: (0, g)),
                  pl.BlockSpec((1, LANES), lambda g, i: (0, 0))],
        out_specs=[pl.BlockSpec((C, W), lambda g, i: (i, g)),
                   pl.BlockSpec((1, HP, HEAD, HEAD), lambda g, i: (i, g, 0, 0))],
        out_shape=[jax.ShapeDtypeStruct((T, nheads * HEAD), BF16), jax.ShapeDtypeStruct((nc, nheads, HEAD, HEAD), F32)],
        scratch_shapes=[pltpu.VMEM((HP, HEAD, HEAD), F32)], compiler_params=_params(),
    )(proj, lower_bounds, gain)


def _hgrn_bwd(proj, do, s_all, lower_bounds, gain, nheads):
    T = proj.shape[0]
    C, HP = HGRN_CHUNK, HGRN_HEADS_PER_STEP
    ng, nc, W = nheads // HP, T // C, HP * HEAD

    def body(proj_ref, do_ref, sin_ref, lb_ref, gain_ref, dproj_ref, dlb_ref, dgain_ref, ds_scr):
        first = pl.program_id(1) == 0

        @pl.when(first)
        def _():
            ds_scr[...] = jnp.zeros_like(ds_scr)
            dlb_ref[...] = jnp.zeros_like(dlb_ref)

        @pl.when(first & (pl.program_id(0) == 0))
        def _():
            dgain_ref[...] = jnp.zeros_like(dgain_ref)

        qr, fr, vi, z, lb0, lb1 = _hgrn_refs(proj_ref, lb_ref, HP)
        primals = (sin_ref[0], qr, fr, vi, z, lb0, lb1, gain_ref[...])
        _, vjp = jax.vjp(_hgrn_step, *primals)
        dS, dq, df, dv, dz, dlb0, dlb1, dgain = vjp((_split_heads(do_ref, 0, HP), ds_scr[...]))
        ds_scr[...] = dS
        for part, d in enumerate((dq, df, dv, dz)):
            _store_heads(dproj_ref, part * W, d)
        _store_heads(dlb_ref, 0, dlb0, slice(0, 1), accumulate=True)
        _store_heads(dlb_ref, 0, dlb1, slice(1, 2), accumulate=True)
        dgain_ref[...] += dgain

    rev = lambda i: nc - 1 - i
    return pl.pallas_call(
        body, name="hgrn_bwd", grid=(ng, nc),
        in_specs=[pl.BlockSpec((C, 4 * W), lambda g, i: (rev(i), g)), pl.BlockSpec((C, W), lambda g, i: (rev(i), g)),
                  pl.BlockSpec((1, HP, HEAD, HEAD), lambda g, i: (rev(i), g, 0, 0)),
                  pl.BlockSpec((2, W), lambda g, i: (0, g)), pl.BlockSpec((1, LANES), lambda g, i: (0, 0))],
        out_specs=[pl.BlockSpec((C, 4 * W), lambda g, i: (rev(i), g)), pl.BlockSpec((2, W), lambda g, i: (0, g)),
                   pl.BlockSpec((1, LANES), lambda g, i: (0, 0))],
        out_shape=[jax.ShapeDtypeStruct(proj.shape, BF16), jax.ShapeDtypeStruct(lower_bounds.shape, F32),
                   jax.ShapeDtypeStruct((1, LANES), F32)],
        scratch_shapes=[pltpu.VMEM((HP, HEAD, HEAD), F32)], compiler_params=_params(),
    )(proj, do, s_all, lower_bounds, gain)


def _post_specs(T):
    tr = min(POST_TILE, T)
    tile = lambda w: pl.BlockSpec((tr, w), lambda i: (i, 0))
    full = lambda r, w: pl.BlockSpec((r, w), lambda i: (0, 0))
    return tr, tile, full


def _post_fwd(s, x, p, g, b, wg, wpl, name):
    T, D = x.shape
    P = p.shape[1]
    tr, tile, full = _post_specs(T)

    def body(s_ref, x_ref, p_ref, g_ref, b_ref, wg_ref, wpl_ref, o_ref, o16_ref):
        x1 = _post_norm(s_ref[...], x_ref[...], g_ref[...], b_ref[...])
        xn = _post_gate(x1, _dot_raw(x1, wg_ref[...], "nn", False), _dot_raw(p_ref[...], wpl_ref[...], "nn", False))
        o_ref[...] = xn
        o16_ref[...] = xn.astype(BF16)

    return pl.pallas_call(
        body, name=name, grid=(T // tr,),
        in_specs=[tile(D), tile(D), tile(P), full(1, D), full(1, D), full(D, D), full(P, D)],
        out_specs=[tile(D), tile(D)],
        out_shape=[jax.ShapeDtypeStruct((T, D), F32), jax.ShapeDtypeStruct((T, D), BF16)], compiler_params=_params(),
    )(s, x, p, g, b, wg, wpl)


def _post_bwd(s, x, p, g, b, wg, wpl, dnext, name, with_loss):
    T, D = x.shape
    P = p.shape[1]
    tr, tile, full = _post_specs(T)

    def body(s_ref, x_ref, p_ref, g_ref, b_ref, wg_ref, wpl_ref, dn_ref,
             ds_ref, dx_ref, dg_ref, db_ref, dwg_ref, dwpl_ref, loss_ref):
        @pl.when(pl.program_id(0) == 0)
        def _():
            for r in (dg_ref, db_ref, dwg_ref, dwpl_ref, loss_ref):
                r[...] = jnp.zeros_like(r)

        x1, vjp_norm = jax.vjp(_post_norm, s_ref[...], x_ref[...], g_ref[...], b_ref[...])
        gate_pre = _dot_raw(x1, wg_ref[...], "nn", False)
        pp = _dot_raw(p_ref[...], wpl_ref[...], "nn", False)
        xn, vjp_gate = jax.vjp(_post_gate, x1, gate_pre, pp)
        if with_loss:
            err = xn - dn_ref[...]
            loss_ref[...] += 0.5 * jnp.sum(jnp.sum(err * err, axis=-1, keepdims=True), axis=0, keepdims=True) / D
            dn = err / D
        else:
            dn = dn_ref[...]
        dx1, dgp, dpp = vjp_gate(dn)
        dwg_ref[...] += _dot_raw(x1, dgp, "tn", False)
        dwpl_ref[...] += _dot_raw(p_ref[...], dpp, "tn", False)
        dx1 = dx1 + _dot_raw(dgp, wg_ref[...], "nt", False)
        ds, dx, dg, db = vjp_norm(dx1)
        ds_ref[...] = ds.astype(ds_ref.dtype)
        dx_ref[...] = dx
        dg_ref[...] += dg
        db_ref[...] += db

    return pl.pallas_call(
        body, name=name, grid=(T // tr,),
        in_specs=[tile(D), tile(D), tile(P), full(1, D), full(1, D), full(D, D), full(P, D), tile(D)],
        out_specs=[tile(D), tile(D), full(1, D), full(1, D), full(D, D), full(P, D), full(SUBLANES, LANES)],
        out_shape=[jax.ShapeDtypeStruct((T, D), BF16), jax.ShapeDtypeStruct((T, D), F32)]
        + [jax.ShapeDtypeStruct((1, D), F32)] * 2
        + [jax.ShapeDtypeStruct((D, D), F32), jax.ShapeDtypeStruct((P, D), F32),
           jax.ShapeDtypeStruct((SUBLANES, LANES), F32)],
        compiler_params=_params(),
    )(s, x, p, g, b, wg, wpl, dnext)


def _adam_math(w, g, m, v):
    m = ADAM_B1 * m + (1.0 - ADAM_B1) * g
    v = ADAM_B2 * v + (1.0 - ADAM_B2) * (g * g)
    m_hat = m / (1.0 - ADAM_B1 ** ADAM_STEP)
    v_hat = v / (1.0 - ADAM_B2 ** ADAM_STEP)
    return -ADAM_LR * (m_hat / (jnp.sqrt(v_hat) + ADAM_EPS) + ADAM_WD * w), m, v


def _shard_tiles(R, C):
    tr = _pick(R, (256, 128, 64, 32, 16, 8))
    return (tr, C) if tr < R or R % SUBLANES == 0 else (R, _pick(C, (256, 128)))


def _adam_sharded(w, m, v, g8, got, me, name):
    rows_apart = w.shape[1] == 1 and w.shape[0] > 1
    (L, R, C) = (1, w.shape[0], w.shape[2]) if rows_apart else w.shape
    tr, tc = (R, LANES) if rows_apart else _shard_tiles(R, C)
    nr, nc = R // tr, C // tc
    side_by_side = g8.ndim == 2

    def body(me_ref, w_ref, m_ref, v_ref, p_ref, *rest):
        got_refs, (g_ref, d_ref, mo_ref, vo_ref) = rest[:7], rest[7:]
        g = p_ref[...] if side_by_side else p_ref[0]
        for r in got_refs:
            g = g + r[0].astype(F32)
        if rows_apart:
            d, mn, vn = _adam_math(w_ref[:, 0, :], g, m_ref[:, 0, :], v_ref[:, 0, :])
            for ref, val in ((g_ref, g), (d_ref, d), (mo_ref, mn), (vo_ref, vn)):
                ref[:, 0, :] = val
            return
        d, mn, vn = _adam_math(w_ref[0], g, m_ref[0], v_ref[0])
        g_ref[0] = g
        d_ref[0] = d
        mo_ref[0] = mn
        vo_ref[0] = vn

    if rows_apart:
        t3 = pl.BlockSpec((tr, 1, tc), lambda l, i, j, q: (i, 0, j))
    else:
        t3 = pl.BlockSpec((1, tr, tc), lambda l, i, j, q: (l, i, j))
    slot = lambda k: pl.BlockSpec((1, tr, tc), lambda l, i, j, q: (k, l * nr + i, j))
    if side_by_side:
        mine = pl.BlockSpec((tr, tc), lambda l, i, j, q: (l * nr + i, q[0] * nc + j))
    else:
        mine = pl.BlockSpec((1, tr, tc), lambda l, i, j, q: (q[0], l * nr + i, j))
    return pl.pallas_call(
        body, name=name,
        grid_spec=pltpu.PrefetchScalarGridSpec(
            num_scalar_prefetch=1, grid=(L, nr, nc),
            in_specs=[t3, t3, t3, mine] + [slot(k) for k in range(7)], out_specs=[t3, t3, t3, t3]),
        out_shape=[jax.ShapeDtypeStruct(w.shape, F32)] * 4, compiler_params=_params(),
    )(me, w, m, v, g8, *([got] * 7))


def _small_rows(shapes):
    offsets, r = [], 0
    for rows, _ in shapes:
        offsets.append(r)
        r += rows
    return offsets, -(-(r + 1) // SUBLANES) * SUBLANES, max(cols for _, cols in shapes)


def _pack_small_grads(grads, loss):
    offsets, total, width = _small_rows([g.shape for g in grads])
    packed = jnp.zeros((total, width), F32)
    for g, r in zip(grads, offsets):
        packed = lax.dynamic_update_slice(packed, g, (r, 0))
    return lax.dynamic_update_slice(packed, loss.reshape(1, 1), (total - 1, 0))


def _adam_replicated(params, g8):
    shapes = [w.shape for w, _, _ in params]
    offsets, total, width = _small_rows(shapes)
    n = len(params)

    def body(*refs):
        g_ref, outs, loss_ref, g_scr = refs[3 * n], refs[3 * n + 1:7 * n + 1], refs[7 * n + 1], refs[7 * n + 2]
        g = g_ref[0]
        for k in range(1, 8):
            g = g + g_ref[k]
        g_scr[...] = g
        for i, (rows, cols) in enumerate(shapes):
            gp = g_scr[offsets[i]:offsets[i] + rows, 0:cols]
            d, mn, vn = _adam_math(refs[3 * i][...], gp, refs[3 * i + 1][...], refs[3 * i + 2][...])
            for ref, val in zip(outs[4 * i:4 * i + 4], (gp, d, mn, vn)):
                ref[...] = val
        loss_ref[...] = g_scr[total - 1:total, 0:LANES]

    out = pl.pallas_call(
        body, name="adam_replicated",
        out_shape=[jax.ShapeDtypeStruct(shp, F32) for shp in shapes for _ in range(4)]
        + [jax.ShapeDtypeStruct((1, LANES), F32)],
        scratch_shapes=[pltpu.VMEM((total, width), F32)], compiler_params=_params(),
    )(*[a for triple in params for a in triple], g8)
    return [out[4 * i:4 * i + 4] for i in range(n)], out[4 * n][0, 0]


def _place():
    return lax.axis_index("x"), lax.axis_index("y"), lax.axis_index("c")


def _first_gather(shards, x, name):
    n = len(shards)
    T, D = x.shape
    rows = _pick(T, (512, 256, 128))

    def body(*refs):
        ins, x_ref = refs[:n], refs[n]
        outs, x16_ref = refs[n + 1:2 * n + 1], refs[2 * n + 1]
        send_sems, recv_sems, local_sems, buf32, buf16, cast_sem = refs[2 * n + 2:]
        x, y, c = _place()
        me, sibling = (x, y, c), (x, y, 1 - c)
        chips = [(1 - x, y), (x, 1 - y), (1 - x, 1 - y)]

        def copy(i, k, block, to, src=None):
            slab = outs[i].at[4 * block[0] + 2 * block[1] + block[2]]
            return pltpu.make_async_remote_copy(
                src_ref=slab if src is None else src, dst_ref=slab, send_sem=send_sems.at[7 * i + k],
                recv_sem=recv_sems.at[7 * i + k], device_id=to, device_id_type=MESH)

        mine, first = [], []
        for i in range(n):
            mine.append(pltpu.make_async_copy(ins[i], outs[i].at[4 * x + 2 * y + c], local_sems.at[i]))
            first.append(copy(i, 0, me, sibling, src=ins[i]))
            first += [copy(i, 1 + j, me, (*chip, c), src=ins[i]) for j, chip in enumerate(chips)]
        for cp in mine + first:
            cp.start()
        for t in range(T // rows):
            tile = pl.ds(t * rows, rows)
            load = pltpu.make_async_copy(x_ref.at[tile], buf32, cast_sem)
            load.start()
            load.wait()
            buf16[...] = buf32[...].astype(BF16)
            store = pltpu.make_async_copy(buf16, x16_ref.at[tile], cast_sem)
            store.start()
            store.wait()
        passed = []
        for i in range(n):
            for j, chip in enumerate(chips):
                copy(i, 1 + j, (*chip, c), me).wait_recv()
                passed.append(copy(i, 4 + j, (*chip, c), sibling))
                passed[-1].start()
        for i in range(n):
            copy(i, 0, sibling, me).wait_recv()
            for j, chip in enumerate(chips):
                copy(i, 4 + j, (*chip, 1 - c), me).wait_recv()
        for cp in first + passed:
            cp.wait_send()
        for cp in mine:
            cp.wait()

    any_spec = pl.BlockSpec(memory_space=pl.ANY)
    out = pl.pallas_call(
        body, name=name,
        out_shape=[jax.ShapeDtypeStruct((8,) + s.shape, s.dtype) for s in shards] + [jax.ShapeDtypeStruct((T, D), BF16)],
        in_specs=[any_spec] * (n + 1), out_specs=[any_spec] * (n + 1),
        scratch_shapes=[pltpu.SemaphoreType.DMA((7 * n,)), pltpu.SemaphoreType.DMA((7 * n,)),
                        pltpu.SemaphoreType.DMA((n,)), pltpu.VMEM((rows, D), F32), pltpu.VMEM((rows, D), BF16),
                        pltpu.SemaphoreType.DMA],
        compiler_params=_params(),
    )(*shards, x)
    return out[:n], out[n]


_HBM = pl.BlockSpec(memory_space=pltpu.HBM)
_SEM = pl.BlockSpec(memory_space=pltpu.SEMAPHORE)
_DATAFLOW = pltpu.SideEffectType.DATAFLOW_SIDE_EFFECTING
TOKEN_SHAPE = (SUBLANES, LANES)


def _peers(x, y, c):
    flip = lambda v, bit: 1 - v if bit else v
    return [(flip(x, r >> 2 & 1), flip(y, r >> 1 & 1), flip(c, r & 1)) for r in range(1, 8)]


def _scatter_plan(x, y, c):
    return [(4 * px + 2 * py + pc, k, (px, py, pc)) for k, (px, py, pc) in enumerate(_peers(x, y, c))]


def _exchange_copies(plan, src_ref, land_ref, send_sems, recv_sems):
    C = land_ref.shape[-1]
    block = (lambda b: src_ref.at[b]) if len(src_ref.shape) == len(land_ref.shape) else (
        lambda b: src_ref.at[:, pl.ds(b * C, C)])
    return [pltpu.make_async_remote_copy(
        src_ref=block(blk), dst_ref=land_ref.at[slot], send_sem=send_sems.at[k], recv_sem=recv_sems.at[k],
        device_id=peer, device_id_type=MESH) for k, (blk, slot, peer) in enumerate(plan(*_place()))]


def _exchange_start_many(srcs, n_slots, block_shapes, plan, name):
    k, n = len(srcs), len(plan(0, 0, 0))
    land_shapes = [(n_slots,) + tuple(bs) for bs in block_shapes]

    def body(*refs):
        src_refs, land_refs, outs = refs[:k], refs[k:2 * k], refs[2 * k:]
        for i in range(k):
            for cp in _exchange_copies(plan, src_refs[i], land_refs[i], outs[2 * i], outs[2 * i + 1]):
                cp.start()
        outs[-1][...] = jnp.zeros_like(outs[-1])

    sems = [pltpu.SemaphoreType.DMA((n,))] * (2 * k)
    out = pl.pallas_call(
        body, name=name,
        out_shape=(*sems, *[pltpu.HBM(s.shape, s.dtype) for s in srcs],
                   *[pltpu.HBM(ls, s.dtype) for ls, s in zip(land_shapes, srcs)],
                   jax.ShapeDtypeStruct(TOKEN_SHAPE, F32)),
        in_specs=(_HBM,) * (2 * k),
        out_specs=(*[_SEM] * (2 * k), *[_HBM] * (2 * k), pl.BlockSpec(memory_space=pltpu.VMEM)),
        input_output_aliases={i: 2 * k + i for i in range(2 * k)},
        compiler_params=pltpu.CompilerParams(has_side_effects=_DATAFLOW),
    )(*[pltpu.with_memory_space_constraint(s, pltpu.HBM) for s in srcs],
      *[pltpu.with_memory_space_constraint(lax.empty(ls, s.dtype), pltpu.HBM) for ls, s in zip(land_shapes, srcs)])
    return [(out[2 * i], out[2 * i + 1], out[2 * k + i], out[3 * k + i], out[-1]) for i in range(k)]


def _exchange_wait(handle, plan, after, name):
    send_sems, recv_sems, src_thru, land_thru, _ = handle

    def body(src_ref, land_ref, send_sems, recv_sems, after_ref, src_dead, got_ref):
        for cp in _exchange_copies(plan, src_ref, land_ref, send_sems, recv_sems):
            cp.wait_send()
            cp.wait_recv()

    return pl.pallas_call(
        body, name=name,
        out_shape=(pltpu.HBM(src_thru.shape, src_thru.dtype), pltpu.HBM(land_thru.shape, land_thru.dtype)),
        in_specs=(_HBM, _HBM, _SEM, _SEM, pl.BlockSpec(memory_space=pl.ANY)), out_specs=(_HBM, _HBM),
        input_output_aliases={0: 0, 1: 1}, compiler_params=pltpu.CompilerParams(has_side_effects=_DATAFLOW),
    )(src_thru, land_thru, send_sems, recv_sems, after)[1]


def _gather_plan(x, y, c):
    return [(0, 4 * x + 2 * y + c, peer) for peer in _peers(x, y, c)]


class _LateGather:
    def __init__(self, shard, name, handle):
        self.shard, self.name, self.handle = shard, name, handle

    @classmethod
    def start_all(cls, shards, names, call_name):
        handles = _exchange_start_many([s[None] for s in shards], 8, [s.shape for s in shards], _gather_plan,
                                       call_name)
        return [cls(s, n, h) for s, n, h in zip(shards, names, handles)]

    def get(self, after):
        land = _exchange_wait(self.handle, _gather_plan, after, self.name + "_wait")
        x, y, c = _place()
        return lax.dynamic_update_slice(land, self.shard[None], (4 * x + 2 * y + c, 0, 0))


class _GradExchange:
    def __init__(self, me, layouts, held_for):
        self.me, self.layouts, self.held_for, self.held, self.pending = me, layouts, held_for, [], {}

    def start(self, tag, grad, grad16=None):
        g8 = self.layouts[tag](grad)
        g16 = g8.astype(BF16) if grad16 is None else self.layouts[tag](grad16)
        block_shape = g8.shape[1:] if g8.ndim == 3 else (g8.shape[0], g8.shape[1] // 8)
        self.held.append((tag, g8, g16, block_shape))
        if tag in self.held_for:
            return jnp.zeros(TOKEN_SHAPE, F32)
        held, self.held = self.held, []
        handles = _exchange_start_many([h[2] for h in held], 7, [h[3] for h in held], _scatter_plan,
                                       "rs_start_" + tag)
        for (t, g8_t, _, _), handle in zip(held, handles):
            self.pending[t] = (g8_t, handle)
        return handles[0][4]

    def finish(self, tag, w, m, v, after):
        g8, handle = self.pending.pop(tag)
        got = _exchange_wait(handle, _scatter_plan, after, "rs_wait_" + tag)
        return _adam_sharded(w, m, v, g8, got, self.me, "adam_" + tag)


def _local_grads(x, p0, p1, target, wt_zs, wt_a, wt_qkv, late, conv_a, conv_b,
                 a_log, dt_bias, gdn_gain, lower_bounds, hgrn_gain, ln_g, ln_b, on_grad=None, x16=None):
    H = a_log.shape[1]
    pad_small = ((0, 0), (H, LANES - 2 * H))
    alog_row = jnp.pad(a_log, pad_small)
    dtb_row = jnp.pad(dt_bias, pad_small)

    if x16 is None:
        x16 = x.astype(BF16)
    proj_zs = _matmul(x16, wt_zs, "nt", "proj_even_zs", after=late.started)
    proj_a = _matmul(x16, wt_a, "nt", "proj_even_a", after=late.started)
    proj_qkv = _matmul(x16, wt_qkv, "nt", "proj_even_qkv", after=late.started)
    y_a = _mixer_a_fwd(proj_a, conv_a)
    qkv = _conv_b_fwd(proj_qkv, conv_b)
    o2, s_gdn = _gdn_fwd(qkv, proj_zs, alog_row, dtb_row, gdn_gain, H)
    woute_a, woute_b = late.out_even(o2)
    wg, wpl = late.gate(o2)
    s_e = _matmul(o2, woute_b, "nn", "out_even_b", add=_matmul(y_a, woute_a, "nn", "out_even_a"))
    x2, x2_16 = _post_fwd(s_e, x, p0, ln_g[0:1], ln_b[0:1], wg[0], wpl[0], "post_even_fwd")
    wino, wouto = late.odd(s_e)
    nheads_o = wouto.shape[0] // HEAD
    proj_o = _matmul(x2_16, wino, "nn", "proj_odd")
    o4, s_hgrn = _hgrn_fwd(proj_o, lower_bounds, hgrn_gain, nheads_o)
    s_o = _matmul(o4, wouto, "nn", "out_odd")
    ds_o, dx2, dlng1, dlnb1, dwg1, dwpl1, loss = _post_bwd(
        s_o, x2, p1, ln_g[1:2], ln_b[1:2], wg[1], wpl[1], target, "post_odd_loss_bwd", True)
    do4 = _matmul(ds_o, wouto, "nt", "d_out_odd_act")
    grads = {}

    def emit(tag, grad, grad16=None):
        grads[tag] = grad
        return on_grad(tag, grad, grad16) if on_grad is not None else jnp.zeros(TOKEN_SHAPE, F32)

    tok = emit("w_out_odd", *_matmul(o4, ds_o, "tn", "d_out_odd_w", also_bf16=True))
    dproj_o, dlb, dhgain = _hgrn_bwd(proj_o, do4, s_hgrn, lower_bounds, hgrn_gain + tok[0:1], nheads_o)
    dx2 = _matmul(dproj_o, wino, "nt", "d_proj_odd_act", add=dx2)
    tok = emit("w_in_odd", *_matmul(x2_16, dproj_o, "tn", "d_proj_odd_w", also_bf16=True))
    ds_e, dx, dlng0, dlnb0, dwg0, dwpl0, _ = _post_bwd(
        s_e, x, p0, ln_g[0:1], ln_b[0:1] + tok[0:1, 0:1], wg[0], wpl[0], dx2, "post_even_bwd", False)
    tok = emit("w_pl_gate", jnp.stack([dwg0, dwg1])) + emit("w_pl", jnp.stack([dwpl0, dwpl1]))
    dy_a = _matmul(ds_e, woute_a, "nt", "d_out_even_a_act")
    do2 = _matmul(ds_e, woute_b, "nt", "d_out_even_b_act")
    dwoute_a = _matmul(y_a, ds_e, "tn", "d_out_even_a_w")
    dwoute_b = _matmul(o2, ds_e, "tn", "d_out_even_b_w")
    tok = tok + emit("w_out_even", jnp.concatenate([dwoute_a, dwoute_b], axis=0))
    dqkv, dproj_zs, dalog, ddtb, dggain = _gdn_bwd(qkv, proj_zs, do2, s_gdn, alog_row, dtb_row, gdn_gain + tok[0:1], H)
    dproj_qkv, dconv_b = _conv_b_bwd(proj_qkv, dqkv, conv_b)
    dproj_a, dconv_a = _mixer_a_bwd(proj_a, dy_a, conv_a)
    emit("conv", (dconv_a[:conv_a.shape[0]], dconv_b[:conv_b.shape[0]]))
    tok = emit("w_in_even", (_matmul(dproj_zs, x16, "tn", "d_proj_even_zs_w"), _matmul(dproj_a, x16, "tn", "d_proj_even_a_w"),
                             _matmul(dproj_qkv, x16, "tn", "d_proj_even_qkv_w")))
    dx = _matmul(dproj_zs, wt_zs, "nn", "d_proj_even_zs_act", add=dx, after=tok)
    dx = _matmul(dproj_a, wt_a, "nn", "d_proj_even_a_act", add=dx)
    dx = _matmul(dproj_qkv, wt_qkv, "nn", "d_proj_even_qkv_act", add=dx)
    grads.update(
        loss=loss[0, 0], grad_x=dx, a_log=dalog[:, H:2 * H], dt_bias=ddtb[:, H:2 * H], gdn_gain=dggain,
        lower_bounds=dlb, hgrn_gain=dhgain, ln_g=jnp.concatenate([dlng0, dlng1], axis=0),
        ln_b=jnp.concatenate([dlnb0, dlnb1], axis=0))
    return grads


def _pad_rows(a, rows):
    return jnp.pad(a, ((0, rows - a.shape[0]), (0, 0)))


def _split_in_even(wt_full, AW, HW, H):
    D = wt_full.shape[1]
    n_a = 4 * AW
    n_main = n_a + 3 * HW
    wt_zs = jnp.concatenate([wt_full[n_main:n_main + HW], wt_full[n_main + HW:],
                             jnp.zeros((LANES - 2 * H, D), wt_full.dtype)], axis=0)
    wt_a = wt_full[:n_a].reshape(4, AW // MIXER_LANES, MIXER_LANES, D).transpose(1, 0, 2, 3).reshape(n_a, D)
    return wt_zs, wt_a, wt_full[n_a:n_main]


def _join_in_even(dt_zs, dt_a, dt_qkv, AW, HW, H):
    D = dt_a.shape[1]
    a_nat = dt_a.reshape(AW // MIXER_LANES, 4, MIXER_LANES, D).transpose(1, 0, 2, 3).reshape(4 * AW, D)
    return jnp.concatenate([a_nat, dt_qkv, dt_zs[:HW], dt_zs[HW:HW + 2 * H]], axis=0)


def kernel(x, p, w_in_even, conv_a_w, conv_b_w, a_log, dt_bias, gdn_norm_g, w_out_even, w_in_odd, lower_bounds, hgrn_norm_g, w_out_odd, ln_g, ln_b, w_pl, w_pl_gate, loss_target, m_w_in_even, m_conv_a_w, m_conv_b_w, m_a_log, m_dt_bias, m_gdn_norm_g, m_w_out_even, m_w_in_odd, m_lower_bounds, m_hgrn_norm_g, m_w_out_odd, m_ln_g, m_ln_b, m_w_pl, m_w_pl_gate, v_w_in_even, v_conv_a_w, v_conv_b_w, v_a_log, v_dt_bias, v_gdn_norm_g, v_w_out_even, v_w_in_odd, v_lower_bounds, v_hgrn_norm_g, v_w_out_odd, v_ln_g, v_ln_b, v_w_pl, v_w_pl_gate):
    xi, yi, ci = _place()
    me = jnp.reshape(4 * xi + 2 * yi + ci, (1,)).astype(jnp.int32)
    D = x.shape[2]
    H = a_log.shape[1]
    HW = H * HEAD
    AW = conv_a_w.shape[2] * 8
    OW = w_out_odd.shape[1] * 8
    PD = w_pl.shape[1]
    ka, kb = conv_a_w.shape[1], conv_b_w.shape[1]
    ca, cb = conv_a_w.shape[2], conv_b_w.shape[2]
    gw = HGRN_HEADS_PER_STEP * HEAD
    ngrp = OW // gw

    transposed = lambda a: jnp.transpose(a, (0, 2, 1))
    taps = jnp.concatenate([_pad_rows(conv_a_w[0], SUBLANES), _pad_rows(conv_b_w[0], SUBLANES)], axis=1)
    (g_ine, g_taps), x16 = _first_gather([transposed(w_in_even)[0].astype(BF16), taps], x[0], "ag_first")
    wt_zs, wt_a, wt_qkv = _split_in_even(g_ine.reshape(-1, D), AW, HW, H)
    conv_a = jnp.transpose(g_taps[:, :ka, :ca], (1, 0, 2)).reshape(ka, 8 * ca)
    conv_b = jnp.transpose(g_taps[:, :kb, ca:], (1, 0, 2)).reshape(kb, 8 * cb)
    behind = lambda shard, dep: lax.optimization_barrier((shard, dep))[0]
    late_oute, late_gate, late_pl, late_ino, late_outo = _LateGather.start_all(
        [behind(w_out_even[0].astype(BF16), (g_ine, g_taps)), w_pl_gate.astype(BF16).reshape(-1, D),
         w_pl.astype(BF16).reshape(DEPTH * PD, -1), w_in_odd[0].astype(BF16), w_out_odd[0].astype(BF16)],
        ["ag_w_out_even", "ag_w_pl_gate", "ag_w_pl", "ag_w_in_odd", "ag_w_out_odd"], "ag_late_start")

    class _Late:
        started = late_oute.handle[4]

        @staticmethod
        def out_even(after):
            woute = late_oute.get(after).reshape(-1, D)
            return woute[:AW], woute[AW:]

        @staticmethod
        def gate(after):
            g_gate, g_pl = late_gate.get(after), late_pl.get(after)
            return (g_gate.reshape(8, DEPTH, D // 8, D).transpose(1, 0, 2, 3).reshape(DEPTH, D, D),
                    g_pl.reshape(8, DEPTH, PD, D // 8).transpose(1, 2, 0, 3).reshape(DEPTH, PD, D))

        @staticmethod
        def odd(after):
            g_ino = late_ino.get(after)
            wino = jnp.transpose(g_ino, (1, 0, 2)).reshape(D, 4, ngrp, gw).transpose(0, 2, 1, 3).reshape(D, 4 * OW)
            return wino, late_outo.get(after).reshape(-1, D)

    sh = w_in_even.shape[2]
    tap_blocks = lambda g, width: _pad_rows(g, SUBLANES).reshape(SUBLANES, 8, width).transpose(1, 0, 2)
    owner_layout = {
        "w_in_even": lambda g: _join_in_even(*g, AW, HW, H).reshape(8, sh, D),
        "w_in_odd": lambda g: g.reshape(D, ngrp, 4, gw).transpose(0, 2, 1, 3).reshape(D, 4 * OW),
        "w_out_even": lambda g: g.reshape(8, -1, D),
        "w_out_odd": lambda g: g.reshape(8, -1, D),
        "w_pl_gate": lambda g: g.reshape(DEPTH, 8, D // 8, D).transpose(1, 0, 2, 3).reshape(8, DEPTH * D // 8, D),
        "w_pl": lambda g: g.reshape(DEPTH, PD, 8, D // 8).transpose(2, 0, 1, 3).reshape(8, DEPTH * PD, D // 8),
        "conv": lambda g: jnp.concatenate([tap_blocks(g[0], ca), tap_blocks(g[1], cb)], axis=2),
    }
    exchange = _GradExchange(me, owner_layout, held_for={"w_pl_gate": "w_out_even", "w_pl": "w_out_even"})
    gr = _local_grads(x[0], p[0, 0], p[1, 0], loss_target[0], wt_zs, wt_a, wt_qkv, _Late, conv_a, conv_b,
                      a_log, dt_bias, gdn_norm_g, lower_bounds, hgrn_norm_g, ln_g, ln_b, on_grad=exchange.start,
                      x16=x16)

    last = gr["grad_x"]
    pack_taps = lambda a, b: jnp.concatenate([_pad_rows(a[0], SUBLANES), _pad_rows(b[0], SUBLANES)], axis=1)[None]
    o_outo = exchange.finish("w_out_odd", w_out_odd, m_w_out_odd, v_w_out_odd, last)
    o_ino = exchange.finish("w_in_odd", w_in_odd, m_w_in_odd, v_w_in_odd, last)
    o_gate = exchange.finish("w_pl_gate", w_pl_gate, m_w_pl_gate, v_w_pl_gate, last)
    o_pl = exchange.finish("w_pl", w_pl, m_w_pl, v_w_pl, last)
    o_oute = exchange.finish("w_out_even", w_out_even, m_w_out_even, v_w_out_even, last)
    o_taps = exchange.finish("conv", taps[None], pack_taps(m_conv_a_w, m_conv_b_w), pack_taps(v_conv_a_w, v_conv_b_w), last)
    others_done = sum(o[1][0, 0:1, 0:1] for o in (o_outo, o_ino, o_gate, o_pl, o_oute, o_taps))
    rows_first = lambda a: jnp.transpose(a, (2, 0, 1))
    o_ine = [jnp.transpose(o, (1, 2, 0)) for o in exchange.finish(
        "w_in_even", rows_first(w_in_even), rows_first(m_w_in_even), rows_first(v_w_in_even), others_done)]

    small_g = _pack_small_grads([gr["a_log"], gr["dt_bias"], gr["gdn_gain"], gr["lower_bounds"], gr["hgrn_gain"],
                                 gr["ln_g"], gr["ln_b"]], gr["loss"])
    o_small, loss = _adam_replicated(
        [(a_log, m_a_log, v_a_log), (dt_bias, m_dt_bias, v_dt_bias), (gdn_norm_g, m_gdn_norm_g, v_gdn_norm_g),
         (lower_bounds, m_lower_bounds, v_lower_bounds), (hgrn_norm_g, m_hgrn_norm_g, v_hgrn_norm_g),
         (ln_g, m_ln_g, v_ln_g), (ln_b, m_ln_b, v_ln_b)],
        _LateGather.start_all([small_g], ["ag_small_grads"], "ag_small_start")[0].get(o_ine[0]))

    def leaves(kind):
        s_alog, s_dt, s_gg, s_lb, s_hg, s_lng, s_lnb = (o[kind] for o in o_small)
        t = o_taps[kind]
        return [o_ine[kind], t[:, :ka, :ca], t[:, :kb, ca:], s_alog, s_dt, s_gg, o_oute[kind],
                o_ino[kind], s_lb, s_hg, o_outo[kind], s_lng, s_lnb, o_pl[kind], o_gate[kind]]

    return (loss, gr["grad_x"][None], *leaves(0), *leaves(1), *leaves(2), *leaves(3))
```
